```python
import math
import jax, jax.numpy as jnp
from jax import lax
import numpy as np

D_MODEL = 1024
BATCH = 4
SEQ = 8192
DEPTH = 2

MLA_HEADS = 8
MLA_Q_RANK = 512
MLA_KV_RANK = 256
MLA_NOPE = 128
MLA_ROPE = 64
MLA_V = 128
ROPE_THETA = 10000.0
Q_BLOCK = 128

NSA_HEADS = 16
NSA_GROUPS = 4
NSA_DK = 96
NSA_DV = 64
CMP_LEN = 32
CMP_STRIDE = 16
SEL_BLOCK = 64
SEL_TOPN = 16
WINDOW = 512
NSA_Q_BLOCK = 64
N_BRANCH = 3
FORCE = 1e6

REL_BUCKETS = 32
REL_MAX_DIST = 128

D_FF = 2816
N_EXPERTS = 8
TOP_K = 2
D_FF_EXPERT = 3584
MOE_BLOCK = 256

LN_EPS = 1e-5
RMS_EPS = 1e-6

kernel_name = 'hybrid_mla_nsa_deepnorm_moe'


def layer_norm(x, g, b):
    xf = x.astype(jnp.float32)
    mu = jnp.mean(xf, -1, keepdims=True)
    var = jnp.mean(jnp.square(xf - mu), -1, keepdims=True)
    return ((xf - mu) * lax.rsqrt(var + LN_EPS) * g + b).astype(x.dtype)


def rms_norm(x, g):
    xf = x.astype(jnp.float32)
    return (xf * lax.rsqrt(jnp.mean(xf * xf, -1, keepdims=True) + RMS_EPS) * g).astype(x.dtype)


def rope(x, pos):
    half = x.shape[-1] // 2
    freq = ROPE_THETA ** (-jnp.arange(half, dtype=jnp.float32) / half)
    ang = pos.astype(jnp.float32)[:, None] * freq[None, :]
    shp = (x.shape[1],) + (1,) * (x.ndim - 3) + (half,)
    cos, sin = jnp.cos(ang).reshape(shp), jnp.sin(ang).reshape(shp)
    x1, x2 = x[..., :half], x[..., half:]
    return jnp.concatenate([x1 * cos - x2 * sin, x1 * sin + x2 * cos], -1).astype(x.dtype)


def rel_bucket(dist):
    n = jnp.maximum(dist, 0)
    max_exact = REL_BUCKETS // 2
    nf = jnp.maximum(n, 1).astype(jnp.float32)
    large = max_exact + (jnp.log(nf / max_exact) / math.log(REL_MAX_DIST / max_exact)
                         * (REL_BUCKETS - max_exact)).astype(jnp.int32)
    large = jnp.minimum(large, REL_BUCKETS - 1)
    return jnp.where(n < max_exact, n, large)


def masked_softmax(s, mask):
    s = jnp.where(mask, s.astype(jnp.float32), -jnp.inf)
    m = jnp.max(s, -1, keepdims=True)
    m = jnp.where(jnp.isfinite(m), m, 0.0)
    p = jnp.exp(s - m)
    return p / jnp.maximum(jnp.sum(p, -1, keepdims=True), 1e-30)


def mla_mixer(x, w_in, q_norm, w_q_up, kv_norm, w_kv_up, w_out):
    B, S, _ = x.shape
    H = MLA_HEADS
    pos = jnp.arange(S)
    lat = x @ w_in
    q_lat, kv_lat, k_rope = jnp.split(lat, [MLA_Q_RANK, MLA_Q_RANK + MLA_KV_RANK], axis=-1)
    q = (rms_norm(q_lat, q_norm) @ w_q_up).reshape(B, S, H, MLA_NOPE + MLA_ROPE)
    q_nope, q_rope = q[..., :MLA_NOPE], rope(q[..., MLA_NOPE:], pos)
    kv = (rms_norm(kv_lat, kv_norm) @ w_kv_up).reshape(B, S, H, MLA_NOPE + MLA_V)
    k_nope, v = kv[..., :MLA_NOPE], kv[..., MLA_NOPE:]
    k_rope = rope(k_rope, pos)
    scale = (MLA_NOPE + MLA_ROPE) ** -0.5

    def block(i):
        s0 = i * Q_BLOCK
        qn = lax.dynamic_slice_in_dim(q_nope, s0, Q_BLOCK, 1)
        qr = lax.dynamic_slice_in_dim(q_rope, s0, Q_BLOCK, 1)
        s = (jnp.einsum('bqhd,bkhd->bhqk', qn, k_nope)
             + jnp.einsum('bqhr,bkr->bhqk', qr, k_rope)) * scale
        qpos = s0 + jnp.arange(Q_BLOCK)
        p = masked_softmax(s, qpos[:, None] >= pos[None, :])
        return jnp.einsum('bhqk,bkhd->bqhd', p.astype(v.dtype), v)

    o = lax.map(block, jnp.arange(S // Q_BLOCK))
    o = jnp.moveaxis(o, 0, 1).reshape(B, S, H * MLA_V)
    return o @ w_out


def compress_blocks(blocks, pe, w1, w2):
    h = jnp.einsum('bnlgd,lde->bnge', blocks + pe[:, None, :], w1)
    return jax.nn.gelu(h) @ w2


def nsa_mixer(x, w_in, pe_k, w1_k, w2_k, pe_v, w1_v, w2_v, rel_bias, w_out):
    B, S, _ = x.shape
    H, G, DK, DV = NSA_HEADS, NSA_GROUPS, NSA_DK, NSA_DV
    HG = H // G
    sizes = [H * DK, G * DK, G * DV, G * DK, G * DV, G * DK, G * DV, H * N_BRANCH]
    cuts = [int(c) for c in np.cumsum(sizes)[:-1]]
    q, kc_tok, vc_tok, ks_tok, vs_tok, kw_tok, vw_tok, gate = jnp.split(x @ w_in, cuts, axis=-1)
    q = q.reshape(B, S, G, HG, DK)
    kc_tok, ks_tok, kw_tok = (t.reshape(B, S, G, DK) for t in (kc_tok, ks_tok, kw_tok))
    vc_tok, vs_tok, vw_tok = (t.reshape(B, S, G, DV) for t in (vc_tok, vs_tok, vw_tok))
    gate = jax.nn.sigmoid(gate.astype(jnp.float32)).reshape(B, S, G, HG, N_BRANCH)
    scale = DK ** -0.5
    rb = rel_bias.reshape(REL_BUCKETS, G, HG)

    nc = (S - CMP_LEN) // CMP_STRIDE + 1
    cstart = jnp.arange(nc) * CMP_STRIDE
    cend = cstart + CMP_LEN - 1
    tok = cstart[:, None] + jnp.arange(CMP_LEN)[None, :]
    k_cmp = compress_blocks(kc_tok[:, tok], pe_k, w1_k, w2_k)
    v_cmp = compress_blocks(vc_tok[:, tok], pe_v, w1_v, w2_v)

    nb = S // SEL_BLOCK
    n_sel = min(SEL_TOPN, nb)
    sstart = jnp.arange(nb) * SEL_BLOCK
    overlap = ((cstart[:, None] <= sstart[None, :] + SEL_BLOCK - 1)
               & (cend[:, None] >= sstart[None, :])).astype(jnp.float32)
    k_blk = ks_tok.reshape(B, nb, SEL_BLOCK, G, DK).transpose(0, 3, 1, 2, 4)
    v_blk = vs_tok.reshape(B, nb, SEL_BLOCK, G, DV).transpose(0, 3, 1, 2, 4)
    b_ix = jnp.arange(B)[:, None, None, None]
    g_ix = jnp.arange(G)[None, :, None, None]
    g_ix5 = jnp.arange(G)[None, :, None, None, None]

    kw_pad = jnp.pad(kw_tok, ((0, 0), (WINDOW, 0), (0, 0), (0, 0)))
    vw_pad = jnp.pad(vw_tok, ((0, 0), (WINDOW, 0), (0, 0), (0, 0)))
    QB = NSA_Q_BLOCK
    span = WINDOW + QB

    def head_bias(bucket):
        return rb[bucket].transpose(2, 3, 0, 1)

    def block(i):
        s0 = i * QB
        qpos = s0 + jnp.arange(QB)
        qi = lax.dynamic_slice_in_dim(q, s0, QB, 1)
        gi = lax.dynamic_slice_in_dim(gate, s0, QB, 1)
        dist_c = qpos[:, None] - cend[None, :]
        s_c = jnp.einsum('bqghd,bngd->bghqn', qi, k_cmp) * scale + head_bias(rel_bucket(dist_c))
        p_c = masked_softmax(s_c, dist_c >= 0)
        o_c = jnp.einsum('bghqn,bngd->bqghd', p_c.astype(v_cmp.dtype), v_cmp)
        imp = jnp.einsum('bghqn,nj->bgqj', p_c, overlap)
        cur = (qpos // SEL_BLOCK)[:, None]
        blk = jnp.arange(nb)[None, :]
        forced = (blk == 0) | (blk == cur) | (blk == cur - 1)
        imp = jnp.where(blk > cur, -FORCE, jnp.where(forced, FORCE, imp))
        _, sel = lax.top_k(imp, n_sel)
        k_s = k_blk[b_ix, g_ix, sel]
        v_s = v_blk[b_ix, g_ix, sel]
        dist_s = qpos[:, None, None] - (sel[..., None] * SEL_BLOCK + jnp.arange(SEL_BLOCK))
        bias_s = rb[rel_bucket(dist_s), g_ix5].transpose(0, 1, 5, 2, 3, 4)
        s_s = jnp.einsum('bqghd,bgqnkd->bghqnk', qi, k_s) * scale + bias_s
        m = n_sel * SEL_BLOCK
        p_s = masked_softmax(s_s.reshape(B, G, HG, QB, m), (dist_s >= 0).reshape(B, G, 1, QB, m))
        o_s = jnp.einsum('bghqm,bgqmd->bqghd', p_s.astype(v_s.dtype), v_s.reshape(B, G, QB, m, DV))
        k_w = lax.dynamic_slice_in_dim(kw_pad, s0, span, 1)
        v_w = lax.dynamic_slice_in_dim(vw_pad, s0, span, 1)
        kpos_w = s0 - WINDOW + jnp.arange(span)
        dist_w = qpos[:, None] - kpos_w[None, :]
        mask_w = (dist_w >= 0) & (dist_w < WINDOW) & (kpos_w[None, :] >= 0)
        s_w = jnp.einsum('bqghd,bkgd->bghqk', qi, k_w) * scale + head_bias(rel_bucket(dist_w))
        p_w = masked_softmax(s_w, mask_w)
        o_w = jnp.einsum('bghqk,bkgd->bqghd', p_w.astype(v_w.dtype), v_w)
        o = gi[..., 0:1] * o_c + gi[..., 1:2] * o_s + gi[..., 2:3] * o_w
        return o.astype(x.dtype)

    o = lax.map(block, jnp.arange(S // QB))
    o = jnp.moveaxis(o, 0, 1).reshape(B, S, H * DV)
    return o @ w_out


def swiglu(x, w_gate, w_up, w_down):
    return (jax.nn.silu(x @ w_gate) * (x @ w_up)) @ w_down


def moe_swiglu(x, w_router, w_gate, w_up, w_down):
    B, S, D = x.shape
    T = B * S
    A = T * TOP_K
    xt = x.reshape(T, D)
    logits = (xt @ w_router).astype(jnp.float32)
    top_val, top_idx = lax.top_k(logits, TOP_K)
    wts = jax.nn.softmax(top_val, -1)
    exp_flat = top_idx.reshape(A)
    order = jnp.argsort(exp_flat)
    exp_sorted = exp_flat[order]
    tok_sorted = order // TOP_K
    w_sorted = wts.reshape(A)[order]
    counts = jnp.bincount(exp_flat, length=N_EXPERTS)
    padded = ((counts + MOE_BLOCK - 1) // MOE_BLOCK) * MOE_BLOCK
    grp_start = jnp.cumsum(counts) - counts
    pad_end = jnp.cumsum(padded)
    pad_start = pad_end - padded
    dest = pad_start[exp_sorted] + (jnp.arange(A) - grp_start[exp_sorted])
    n_blocks = (A + MOE_BLOCK - 1) // MOE_BLOCK + N_EXPERTS
    R = n_blocks * MOE_BLOCK
    row_tok = jnp.zeros((R,), jnp.int32).at[dest].set(tok_sorted.astype(jnp.int32))
    row_w = jnp.zeros((R,), jnp.float32).at[dest].set(w_sorted)
    block_expert = jnp.minimum(
        jnp.searchsorted(pad_end, jnp.arange(n_blocks) * MOE_BLOCK, side='right'), N_EXPERTS - 1)

    def expert_block(args):
        toks, wr, e = args
        xb = xt[toks]
        h = jax.nn.silu(xb @ w_gate[e]) * (xb @ w_up[e])
        return (h @ w_down[e]) * wr[:, None].astype(xb.dtype)

    out = lax.map(expert_block, (row_tok.reshape(n_blocks, MOE_BLOCK),
                                 row_w.reshape(n_blocks, MOE_BLOCK), block_expert))
    y = jnp.zeros_like(xt).at[row_tok].add(out.reshape(R, D))
    return y.reshape(B, S, D)


def setup_inputs(seed: int = 0) -> dict:
    key = jax.random.key(seed)
    ks = jax.random.split(key, 28)
    f32 = jnp.float32
    ne, no = (DEPTH + 1) // 2, DEPTH // 2
    beta = (8.0 * DEPTH) ** -0.25
    D = D_MODEL
    nsa_in = (NSA_HEADS * NSA_DK + 3 * NSA_GROUPS * NSA_DK + 3 * NSA_GROUPS * NSA_DV
              + NSA_HEADS * N_BRANCH)

    def w(k, shape, fan_in, s=1.0):
        return jax.random.normal(k, shape, f32) * (s * fan_in ** -0.5)

    def gain(k, shape):
        return 1.0 + 0.02 * jax.random.normal(k, shape, f32)

    def small(k, shape, s=0.02):
        return s * jax.random.normal(k, shape, f32)

    return {
        'x': jax.random.normal(ks[0], (BATCH, SEQ, D), f32),
        'mla_w_in': w(ks[1], (ne, D, MLA_Q_RANK + MLA_KV_RANK + MLA_ROPE), D),
        'mla_q_norm': gain(ks[2], (ne, MLA_Q_RANK)),
        'mla_w_q_up': w(ks[3], (ne, MLA_Q_RANK, MLA_HEADS * (MLA_NOPE + MLA_ROPE)), MLA_Q_RANK),
        'mla_kv_norm': gain(ks[4], (ne, MLA_KV_RANK)),
        'mla_w_kv_up': w(ks[5], (ne, MLA_KV_RANK, MLA_HEADS * (MLA_NOPE + MLA_V)), MLA_KV_RANK),
        'mla_w_out': w(ks[6], (ne, MLA_HEADS * MLA_V, D), MLA_HEADS * MLA_V, beta),
        'nsa_w_in': w(ks[7], (no, D, nsa_in), D),
        'nsa_cmp_pe_k': small(ks[8], (no, CMP_LEN, NSA_DK), 0.1),
        'nsa_cmp_w1_k': w(ks[9], (no, CMP_LEN, NSA_DK, NSA_DK), CMP_LEN * NSA_DK),
        'nsa_cmp_w2_k': w(ks[10], (no, NSA_DK, NSA_DK), NSA_DK),
        'nsa_cmp_pe_v': small(ks[11], (no, CMP_LEN, NSA_DV), 0.1),
        'nsa_cmp_w1_v': w(ks[12], (no, CMP_LEN, NSA_DV, NSA_DV), CMP_LEN * NSA_DV),
        'nsa_cmp_w2_v': w(ks[13], (no, NSA_DV, NSA_DV), NSA_DV),
        'nsa_w_out': w(ks[14], (no, NSA_HEADS * NSA_DV, D), NSA_HEADS * NSA_DV, beta),
        'rel_bias': small(ks[15], (REL_BUCKETS, NSA_HEADS), 0.3),
        'ffn_w_gate': w(ks[16], (ne, D, D_FF), D),
        'ffn_w_up': w(ks[17], (ne, D, D_FF), D),
        'ffn_w_down': w(ks[18], (ne, D_FF, D), D_FF, beta),
        'moe_w_router': w(ks[19], (no, D, N_EXPERTS), D),
        'moe_w_gate': w(ks[20], (no, N_EXPERTS, D, D_FF_EXPERT), D),
        'moe_w_up': w(ks[21], (no, N_EXPERTS, D, D_FF_EXPERT), D),
        'moe_w_down': w(ks[22], (no, N_EXPERTS, D_FF_EXPERT, D), D_FF_EXPERT, beta),
        'ln_mix_g': gain(ks[23], (DEPTH, D)),
        'ln_mix_b': small(ks[24], (DEPTH, D)),
        'ln_ffn_g': gain(ks[25], (DEPTH, D)),
        'ln_ffn_b': small(ks[26], (DEPTH, D)),
    }


def reference(x, mla_w_in, mla_q_norm, mla_w_q_up, mla_kv_norm, mla_w_kv_up, mla_w_out,
              nsa_w_in, nsa_cmp_pe_k, nsa_cmp_w1_k, nsa_cmp_w2_k, nsa_cmp_pe_v, nsa_cmp_w1_v,
              nsa_cmp_w2_v, nsa_w_out, rel_bias, ffn_w_gate, ffn_w_up, ffn_w_down,
              moe_w_router, moe_w_gate, moe_w_up, moe_w_down,
              ln_mix_g, ln_mix_b, ln_ffn_g, ln_ffn_b):
    alpha = (2.0 * DEPTH) ** 0.25
    for i in range(DEPTH):
        j = i // 2
        if i % 2 == 0:
            h = mla_mixer(x, mla_w_in[j], mla_q_norm[j], mla_w_q_up[j], mla_kv_norm[j],
                          mla_w_kv_up[j], mla_w_out[j])
        else:
            h = nsa_mixer(x, nsa_w_in[j], nsa_cmp_pe_k[j], nsa_cmp_w1_k[j], nsa_cmp_w2_k[j],
                          nsa_cmp_pe_v[j], nsa_cmp_w1_v[j], nsa_cmp_w2_v[j], rel_bias, nsa_w_out[j])
        x = layer_norm(alpha * x + h, ln_mix_g[i], ln_mix_b[i])
        if i % 2 == 0:
            f = swiglu(x, ffn_w_gate[j], ffn_w_up[j], ffn_w_down[j])
        else:
            f = moe_swiglu(x, moe_w_router[j], moe_w_gate[j], moe_w_up[j], moe_w_down[j])
        x = layer_norm(alpha * x + f, ln_ffn_g[i], ln_ffn_b[i])
    return x
```

```python
import functools
import math

import numpy as np
import jax
import jax.numpy as jnp
from jax import lax
from jax.experimental import pallas as pl
from jax.experimental.pallas import tpu as pltpu

F32 = jnp.float32
BF16 = jnp.bfloat16

D_MODEL = 1024
DEPTH = 2

MLA_HEADS = 8
MLA_Q_RANK = 512
MLA_KV_RANK = 256
MLA_NOPE = 128
MLA_ROPE = 64
MLA_V = 128
ROPE_THETA = 10000.0

NSA_HEADS = 16
NSA_GROUPS = 4
NSA_HG = NSA_HEADS // NSA_GROUPS
NSA_DK = 96
NSA_DV = 64
CMP_LEN = 32
CMP_STRIDE = 16
SEL_BLOCK = 64
SEL_TOPN = 16
WINDOW = 512
N_BRANCH = 3
FORCE = 1e6

REL_BUCKETS = 32
REL_MAX_DIST = 128

D_FF = 2816
N_EXPERTS = 8
TOP_K = 2
D_FF_EXPERT = 3584

LN_EPS = 1e-5
RMS_EPS = 1e-6

ALPHA = (2.0 * DEPTH) ** 0.25

NEG = -1e30
NEG_TEST = -1e29

V7X_VMEM_LIMIT = 56 * 1024 * 1024

NSA_TQ = 128
NSA_LANES = NSA_HG * NSA_TQ
NSA_KC = 128

MOE_BLK = 512
MOE_TC = 256
MOE_TF = 512


def _cparams(sem, vmem=V7X_VMEM_LIMIT):
    return pltpu.CompilerParams(dimension_semantics=sem, vmem_limit_bytes=vmem)


def _layer_norm(r, g, b):
    mu = jnp.mean(r, -1, keepdims=True)
    d = r - mu
    var = jnp.mean(d * d, -1, keepdims=True)
    return d * lax.rsqrt(var + LN_EPS) * g + b


def _rms_norm(x, g):
    return x * lax.rsqrt(jnp.mean(x * x, -1, keepdims=True) + RMS_EPS) * g


def _split3(a):
    a1 = a.astype(BF16)
    r1 = a - a1.astype(F32)
    a2 = r1.astype(BF16)
    a3 = (r1 - a2.astype(F32)).astype(BF16)
    return a1, a2, a3


def _dot(a, b):
    return jnp.dot(a, b, preferred_element_type=F32)


def _dot_nt(a, b):
    return lax.dot_general(a, b, (((1,), (1,)), ((), ())), preferred_element_type=F32)


def _linear_body(*refs, has_norm, has_ln, act):
    it = iter(refs)
    x_ref = next(it)
    w_ref = next(it)
    g_ref = next(it) if has_norm else None
    if has_ln:
        res_ref, lg_ref, lb_ref = next(it), next(it), next(it)
    o_ref = next(it)
    x = x_ref[...]
    if has_norm:
        x = _rms_norm(x.astype(F32), g_ref[...])
    acc = _dot(x.astype(BF16), w_ref[...])
    if act == "sigmoid":
        acc = jax.nn.sigmoid(acc)
    if has_ln:
        acc = _layer_norm(ALPHA * res_ref[...] + acc, lg_ref[...], lb_ref[...])
    o_ref[...] = acc.astype(o_ref.dtype)


def _linear(x, w, *, tm, tn, out_dtype, xcol=0, norm_gain=None, ln=None, act=None, name):
    m = x.shape[0]
    k, n = w.shape
    assert m % tm == 0 and n % tn == 0
    in_specs = [pl.BlockSpec((tm, k), lambda i, j: (i, xcol)),
                pl.BlockSpec((k, tn), lambda i, j: (0, j))]
    args = [x, w]
    if norm_gain is not None:
        in_specs.append(pl.BlockSpec((1, k), lambda i, j: (0, 0)))
        args.append(norm_gain.reshape(1, k))
    if ln is not None:
        assert tn == n
        res, lg, lb = ln
        in_specs += [pl.BlockSpec((tm, n), lambda i, j: (i, 0)),
                     pl.BlockSpec((1, n), lambda i, j: (0, 0)),
                     pl.BlockSpec((1, n), lambda i, j: (0, 0))]
        args += [res, lg.reshape(1, n), lb.reshape(1, n)]
    body = functools.partial(_linear_body, has_norm=norm_gain is not None,
                             has_ln=ln is not None, act=act)
    return pl.pallas_call(
        body,
        out_shape=jax.ShapeDtypeStruct((m, n), out_dtype),
        grid=(m // tm, n // tn),
        in_specs=in_specs,
        out_specs=pl.BlockSpec((tm, tn), lambda i, j: (i, j)),
        compiler_params=_cparams(("parallel", "arbitrary")),
        name=name,
    )(*args)


def _res_ln_body(x_ref, f_ref, g_ref, b_ref, o_ref):
    o_ref[...] = _layer_norm(ALPHA * x_ref[...] + f_ref[...], g_ref[...], b_ref[...])


def _res_ln(x, f, g, b, *, tm=512):
    m, n = x.shape
    row = pl.BlockSpec((tm, n), lambda i: (i, 0))
    vec = pl.BlockSpec((1, n), lambda i: (0, 0))
    return pl.pallas_call(
        _res_ln_body,
        out_shape=jax.ShapeDtypeStruct((m, n), F32),
        grid=(m // tm,),
        in_specs=[row, row, vec, vec],
        out_specs=row,
        compiler_params=_cparams(("parallel",)),
        name="res_ln",
    )(x, f, g.reshape(1, n), b.reshape(1, n))


def _rope_tables(s):
    half = MLA_ROPE // 2
    freq = ROPE_THETA ** (-jnp.arange(half, dtype=F32) / half)
    ang = jnp.arange(s).astype(F32)[:, None] * freq[None, :]
    cos, sin = jnp.cos(ang), jnp.sin(ang)
    return jnp.concatenate([cos, cos], -1), jnp.concatenate([-sin, sin], -1)


def _mla_q_body(lat_ref, g_ref, wn_ref, wa_ref, wb_ref, cos_ref, sin_ref, o_ref):
    xn = _rms_norm(lat_ref[...], g_ref[...]).astype(BF16)
    cos, sin = cos_ref[...], sin_ref[...]
    for h in range(MLA_HEADS):
        o_ref[0, h, :, :MLA_NOPE] = _dot(xn, wn_ref[h]).astype(BF16)
        rot = _dot(xn, wa_ref[h]) * cos + _dot(xn, wb_ref[h]) * sin
        o_ref[0, h, :, MLA_NOPE:] = rot.astype(BF16)


def _mla_q_proj(lat, gain, wn, wa, wb, cosx, sinx, b, s, *, tm=512):
    ns = s // tm
    dq = MLA_NOPE + MLA_ROPE
    full3 = lambda i: (0, 0, 0)
    return pl.pallas_call(
        _mla_q_body,
        out_shape=jax.ShapeDtypeStruct((b, MLA_HEADS, s, dq), BF16),
        grid=(b * ns,),
        in_specs=[pl.BlockSpec((tm, MLA_Q_RANK), lambda i: (i, 0)),
                  pl.BlockSpec((1, MLA_Q_RANK), lambda i: (0, 0)),
                  pl.BlockSpec(wn.shape, full3),
                  pl.BlockSpec(wa.shape, full3),
                  pl.BlockSpec(wb.shape, full3),
                  pl.BlockSpec((tm, MLA_ROPE), lambda i: (i % ns, 0)),
                  pl.BlockSpec((tm, MLA_ROPE), lambda i: (i % ns, 0))],
        out_specs=pl.BlockSpec((1, MLA_HEADS, tm, dq), lambda i: (i // ns, 0, i % ns, 0)),
        compiler_params=_cparams(("parallel",)),
        name="mla_q_proj",
    )(lat, gain.reshape(1, -1), wn, wa, wb, cosx, sinx)


def _mla_kv_body(lat_ref, g_ref, w_ref, kr_ref, cos_ref, sin_ref, kv_ref, kro_ref):
    xn = _rms_norm(lat_ref[...], g_ref[...]).astype(BF16)
    kv_ref[...] = _dot(xn, w_ref[...]).astype(BF16)
    kr = kr_ref[...]
    rot = kr[:, :MLA_ROPE] * cos_ref[...] + kr[:, MLA_ROPE:] * sin_ref[...]
    kro_ref[...] = rot.astype(BF16)


def _mla_kv_proj(lat, gain, w, cosx, sinx, s, *, tm=512):
    t = lat.shape[0]
    ns = s // tm
    n = w.shape[1]
    return pl.pallas_call(
        _mla_kv_body,
        out_shape=(jax.ShapeDtypeStruct((t, n), BF16), jax.ShapeDtypeStruct((t, MLA_ROPE), BF16)),
        grid=(t // tm,),
        in_specs=[pl.BlockSpec((tm, MLA_KV_RANK), lambda i: (i, MLA_Q_RANK // MLA_KV_RANK)),
                  pl.BlockSpec((1, MLA_KV_RANK), lambda i: (0, 0)),
                  pl.BlockSpec(w.shape, lambda i: (0, 0)),
                  pl.BlockSpec((tm, 2 * MLA_ROPE),
                               lambda i: (i, (MLA_Q_RANK + MLA_KV_RANK) // (2 * MLA_ROPE))),
                  pl.BlockSpec((tm, MLA_ROPE), lambda i: (i % ns, 0)),
                  pl.BlockSpec((tm, MLA_ROPE), lambda i: (i % ns, 0))],
        out_specs=(pl.BlockSpec((tm, n), lambda i: (i, 0)),
                   pl.BlockSpec((tm, MLA_ROPE), lambda i: (i, 0))),
        compiler_params=_cparams(("parallel",)),
        name="mla_kv_proj",
    )(lat, gain.reshape(1, -1), w, lat, cosx, sinx)


def _mla_attn_body(q_ref, kn_ref, kr_ref, v_ref, o_ref, m_scr, l_scr, acc_scr, *, scale, tq):
    i = pl.program_id(2)
    j = pl.program_id(3)

    @pl.when(j == 0)
    def _():
        m_scr[...] = jnp.full_like(m_scr, NEG)
        l_scr[...] = jnp.zeros_like(l_scr)
        acc_scr[...] = jnp.zeros_like(acc_scr)

    def step(masked):
        s = _dot_nt(q_ref[0, 0, :, :MLA_NOPE], kn_ref[0])
        s = s + _dot_nt(q_ref[0, 0, :, MLA_NOPE:], kr_ref[0])
        s = s * scale
        if masked:
            row = lax.broadcasted_iota(jnp.int32, s.shape, 0)
            col = lax.broadcasted_iota(jnp.int32, s.shape, 1)
            s = jnp.where(row >= col, s, NEG)
        m_old = m_scr[...]
        m_new = jnp.maximum(m_old, jnp.max(s, -1, keepdims=True))
        a = jnp.exp(m_old - m_new)
        p = jnp.exp(s - m_new)
        l_scr[...] = a * l_scr[...] + jnp.sum(p, -1, keepdims=True)
        acc_scr[...] = a * acc_scr[...] + _dot(p.astype(BF16), v_ref[0])
        m_scr[...] = m_new

    @pl.when(j < i)
    def _():
        step(False)

    @pl.when(j == i)
    def _():
        step(True)
        o_ref[0] = (acc_scr[...] / jnp.maximum(l_scr[...], 1e-30)).astype(o_ref.dtype)


def _mla_attention(q, kv, kr, b, s, *, tq=512):
    nq = s // tq
    scale = (MLA_NOPE + MLA_ROPE) ** -0.5
    body = functools.partial(_mla_attn_body, scale=scale, tq=tq)
    return pl.pallas_call(
        body,
        out_shape=jax.ShapeDtypeStruct((b, s, MLA_HEADS * MLA_V), BF16),
        grid=(b, MLA_HEADS, nq, nq),
        in_specs=[pl.BlockSpec((1, 1, tq, MLA_NOPE + MLA_ROPE), lambda bb, h, i, j: (bb, h, i, 0)),
                  pl.BlockSpec((1, tq, MLA_NOPE), lambda bb, h, i, j: (bb, jnp.minimum(j, i), 2 * h)),
                  pl.BlockSpec((1, tq, MLA_ROPE), lambda bb, h, i, j: (bb, jnp.minimum(j, i), 0)),
                  pl.BlockSpec((1, tq, MLA_V), lambda bb, h, i, j: (bb, jnp.minimum(j, i), 2 * h + 1))],
        out_specs=pl.BlockSpec((1, tq, MLA_V), lambda bb, h, i, j: (bb, i, h)),
        scratch_shapes=[pltpu.VMEM((tq, 1), F32), pltpu.VMEM((tq, 1), F32),
                        pltpu.VMEM((tq, MLA_V), F32)],
        compiler_params=_cparams(("parallel", "parallel", "parallel", "arbitrary")),
        name="mla_attention",
    )(q, kv, kr, kv)


def _mla_mixer(x2, b, s, w_in, q_norm, w_q_up, kv_norm, w_kv_up):
    r0 = MLA_Q_RANK + MLA_KV_RANK
    half = MLA_ROPE // 2
    w_in_ext = jnp.concatenate([w_in, w_in[:, r0 + half:r0 + MLA_ROPE], w_in[:, r0:r0 + half]], 1)
    lat = _linear(x2, w_in_ext.astype(BF16), tm=512, tn=w_in_ext.shape[1], out_dtype=F32, name="mla_in")
    wq = w_q_up.reshape(MLA_Q_RANK, MLA_HEADS, MLA_NOPE + MLA_ROPE).transpose(1, 0, 2)
    wn = wq[..., :MLA_NOPE].astype(BF16)
    wr = wq[..., MLA_NOPE:]
    wa = wr.astype(BF16)
    wb = jnp.concatenate([wr[..., half:], wr[..., :half]], -1).astype(BF16)
    cosx, sinx = _rope_tables(s)
    q = _mla_q_proj(lat, q_norm, wn, wa, wb, cosx, sinx, b, s)
    kv, kr = _mla_kv_proj(lat, kv_norm, w_kv_up.astype(BF16), cosx, sinx, s)
    kv = kv.reshape(b, s, -1)
    kr = kr.reshape(b, s, MLA_ROPE)
    o = _mla_attention(q, kv, kr, b, s)
    return o.reshape(b * s, MLA_HEADS * MLA_V)


def _ffn_body(x_ref, wg_ref, wu_ref, wd_ref, lg_ref, lb_ref, o_ref, acc_scr):
    f = pl.program_id(1)
    xb = x_ref[...].astype(BF16)
    h = jax.nn.silu(_dot(xb, wg_ref[...])) * _dot(xb, wu_ref[...])
    part = _dot(h.astype(BF16), wd_ref[...])

    @pl.when(f == 0)
    def _():
        acc_scr[...] = part

    @pl.when(f > 0)
    def _():
        acc_scr[...] += part

    @pl.when(f == pl.num_programs(1) - 1)
    def _():
        o_ref[...] = _layer_norm(ALPHA * x_ref[...] + acc_scr[...], lg_ref[...], lb_ref[...])


def _ffn_ln(x, wg, wu, wd, lg, lb, *, tm=512, tf=1408):
    m, d = x.shape
    dff = wg.shape[1]
    assert dff % tf == 0
    vec = pl.BlockSpec((1, d), lambda i, f: (0, 0))
    return pl.pallas_call(
        _ffn_body,
        out_shape=jax.ShapeDtypeStruct((m, d), F32),
        grid=(m // tm, dff // tf),
        in_specs=[pl.BlockSpec((tm, d), lambda i, f: (i, 0)),
                  pl.BlockSpec((d, tf), lambda i, f: (0, f)),
                  pl.BlockSpec((d, tf), lambda i, f: (0, f)),
                  pl.BlockSpec((tf, d), lambda i, f: (f, 0)),
                  vec, vec],
        out_specs=pl.BlockSpec((tm, d), lambda i, f: (i, 0)),
        scratch_shapes=[pltpu.VMEM((tm, d), F32)],
        compiler_params=_cparams(("parallel", "arbitrary")),
        name="ffn_ln",
    )(x, wg, wu, wd, lg.reshape(1, d), lb.reshape(1, d))


def _compress_body(a_ref, pe_ref, w1_ref, w2_ref, o_ref):
    a = (a_ref[...].astype(F32) + pe_ref[...]).astype(BF16)
    h = jax.nn.gelu(_dot(a, w1_ref[...]))
    o_ref[...] = _dot(h.astype(BF16), w2_ref[...]).astype(o_ref.dtype)


def _compress(a, pe, w1, w2, *, tm=512):
    m, k = a.shape
    d = w2.shape[1]
    return pl.pallas_call(
        _compress_body,
        out_shape=jax.ShapeDtypeStruct((m, d), BF16),
        grid=(m // tm,),
        in_specs=[pl.BlockSpec((tm, k), lambda i: (i, 0)),
                  pl.BlockSpec((1, k), lambda i: (0, 0)),
                  pl.BlockSpec((k, d), lambda i: (0, 0)),
                  pl.BlockSpec((d, d), lambda i: (0, 0))],
        out_specs=pl.BlockSpec((tm, d), lambda i: (i, 0)),
        compiler_params=_cparams(("parallel",)),
        name="nsa_compress",
    )(a, pe.reshape(1, k), w1.reshape(k, d).astype(BF16), w2.astype(BF16))


def _rel_bucket_np(dist):
    n = np.maximum(dist, 0)
    max_exact = REL_BUCKETS // 2
    nf = np.maximum(n, 1).astype(np.float32)
    large = max_exact + (np.log(nf / np.float32(max_exact)) / np.float32(math.log(REL_MAX_DIST / max_exact))
                         * np.float32(REL_BUCKETS - max_exact)).astype(np.int32)
    large = np.minimum(large, REL_BUCKETS - 1)
    return np.where(n < max_exact, n, large).astype(np.int32)


def _nsa_tables(rel_bias, s):
    g, hg, tq, kc = NSA_GROUPS, NSA_HG, NSA_TQ, NSA_KC
    assert np.all(_rel_bucket_np(np.arange(tq - 15, s + tq)) == REL_BUCKETS - 1)
    rb = rel_bias.reshape(REL_BUCKETS, g, hg)

    def tile(dist, valid):
        vals = rb[_rel_bucket_np(dist)]
        vals = jnp.where(valid[:, :, None, None], vals, NEG)
        return vals.transpose(2, 0, 3, 1).reshape(g, dist.shape[0], hg * tq)

    i = np.arange(kc)[:, None]
    j = np.arange(tq)[None, :]
    d0 = j - i
    d1 = tq + j - i
    d4 = 4 * tq + j - i
    tiles = jnp.stack([tile(d0, d0 >= 0), tile(d1, d1 >= 0), tile(d4, d4 < WINDOW)], 1)
    c31 = jnp.broadcast_to(rb[REL_BUCKETS - 1][:, None, :, None], (g, 1, hg, tq)).reshape(g, 1, hg * tq)
    r = np.arange(16)[:, None]
    dc_gen = j + (tq - CMP_LEN + 1) - CMP_STRIDE * r
    dc_first = j - CMP_STRIDE * r - (CMP_LEN - 1)
    band = jnp.stack([tile(dc_gen, dc_gen >= 0), tile(dc_first, dc_first >= 0)], 1)
    band = jnp.where(band > NEG_TEST, band - c31[:, None], NEG)
    return tiles, band, c31


def _overlap_t(nc_pad, nb):
    n = np.arange(nc_pad)[None, :]
    jb = np.arange(nb)[:, None]
    cstart = n * CMP_STRIDE
    cend = cstart + CMP_LEN - 1
    sstart = jb * SEL_BLOCK
    ov = (cstart <= sstart + SEL_BLOCK - 1) & (cend >= sstart) & (n < nc_pad - 1)
    return jnp.asarray(ov.astype(np.float32), BF16)


def _nsa_body(qt_ref, kc_ref, vct_ref, ks_ref, vst_ref, kw_ref, vwt_ref, gate_ref, tiles_ref,
              band_ref, c31_ref, ovt_ref, o_ref,
              s_scr, sel_scr, m_s, l_s, acc_s, m_w, l_w, acc_w, *, scale, nb):
    qt = pl.program_id(2)
    L = NSA_LANES
    q_t = qt_ref[0, 0, 0]
    c31 = c31_ref[0]
    ncp = kc_ref.shape[2]

    s = _dot(kc_ref[0, 0], q_t) * scale + c31
    first = (qt == 0).astype(jnp.int32)
    bs = pl.multiple_of(8 * jnp.maximum(qt, 1) - 8, 8)
    row = lax.broadcasted_iota(jnp.int32, (ncp, L), 0)
    s_scr[...] = jnp.where(row < bs + 16, s, NEG)
    s_scr[pl.ds(bs, 16), :] += band_ref[0, first]
    s = s_scr[...]
    m = jnp.max(s, 0, keepdims=True)
    m = jnp.where(m < NEG_TEST, 0.0, m)
    p = jnp.exp(s - m)
    den = jnp.maximum(jnp.sum(p, 0, keepdims=True), 1e-30)
    p = p * (1.0 / den)
    o_c = _dot(vct_ref[0, 0], p.astype(BF16))

    psum = p[:, 0:NSA_TQ]
    for h in range(1, NSA_HG):
        psum = psum + p[:, h * NSA_TQ:(h + 1) * NSA_TQ]
    p1, p2, p3 = _split3(psum)
    ovt = ovt_ref[...]
    imp = _dot(ovt, p1) + _dot(ovt, p2) + _dot(ovt, p3)
    blk = lax.broadcasted_iota(jnp.int32, (nb, NSA_TQ), 0)
    lane = lax.broadcasted_iota(jnp.int32, (nb, NSA_TQ), 1)
    cur = (NSA_TQ // SEL_BLOCK) * qt + lane // SEL_BLOCK
    forced = (blk == 0) | (blk == cur) | (blk == cur - 1)
    v = jnp.where(blk > cur, -FORCE, jnp.where(forced, FORCE, imp))
    blk_f = blk.astype(F32)
    sel = jnp.zeros((nb, NSA_TQ), F32)
    for _ in range(min(SEL_TOPN, nb)):
        mx = jnp.max(v, 0, keepdims=True)
        idx = jnp.min(jnp.where(v == mx, blk_f, float(nb)), 0, keepdims=True)
        hit = blk_f == idx
        sel = jnp.where(hit, 1.0, sel)
        v = jnp.where(hit, -jnp.inf, v)
    selneg = jnp.where(sel > 0.5, 0.0, NEG)
    selneg = jnp.concatenate([selneg] * NSA_HG, 1)
    per = NSA_KC // SEL_BLOCK
    for c in range(nb // per):
        sel_scr[c] = selneg[per * c:per * (c + 1), :]

    def chunk(kc_idx, add, k_ref, vt_ref, m_scr, l_scr, acc_scr):
        sc = _dot(k_ref[0, 0, kc_idx], q_t) * scale + add
        m_old = m_scr[...]
        m_new = jnp.maximum(m_old, jnp.max(sc, 0, keepdims=True))
        a = jnp.exp(m_old - m_new)
        pp = jnp.exp(sc - m_new)
        l_scr[...] = a * l_scr[...] + jnp.sum(pp, 0, keepdims=True)
        acc_scr[...] = a * acc_scr[...] + _dot(vt_ref[0, 0, kc_idx], pp.astype(BF16))
        m_scr[...] = m_new

    def sel_rows(kc_idx, base):
        rows = sel_scr[kc_idx]
        parts = [jnp.broadcast_to(rows[r:r + 1, :] + base, (SEL_BLOCK, L)) for r in range(per)]
        return jnp.concatenate(parts, 0)

    for m_scr, l_scr, acc_scr in ((m_s, l_s, acc_s), (m_w, l_w, acc_w)):
        m_scr[...] = jnp.full_like(m_scr, NEG)
        l_scr[...] = jnp.zeros_like(l_scr)
        acc_scr[...] = jnp.zeros_like(acc_scr)

    def far_body(kc_idx, carry):
        chunk(kc_idx, sel_rows(kc_idx, c31), ks_ref, vst_ref, m_s, l_s, acc_s)
        return carry

    lax.fori_loop(0, jnp.maximum(qt - 1, 0), far_body, 0)

    @pl.when(qt >= 1)
    def _():
        chunk(qt - 1, sel_rows(qt - 1, 0.0) + tiles_ref[0, 1], ks_ref, vst_ref, m_s, l_s, acc_s)

    chunk(qt, sel_rows(qt, 0.0) + tiles_ref[0, 0], ks_ref, vst_ref, m_s, l_s, acc_s)

    chunk(qt, tiles_ref[0, 0], kw_ref, vwt_ref, m_w, l_w, acc_w)

    @pl.when(qt >= 1)
    def _():
        chunk(qt - 1, tiles_ref[0, 1], kw_ref, vwt_ref, m_w, l_w, acc_w)

    for delta in (2, 3):
        @pl.when(qt >= delta)
        def _(delta=delta):
            chunk(qt - delta, jnp.broadcast_to(c31, (NSA_KC, L)), kw_ref, vwt_ref, m_w, l_w, acc_w)

    @pl.when(qt >= 4)
    def _():
        chunk(qt - 4, tiles_ref[0, 2], kw_ref, vwt_ref, m_w, l_w, acc_w)

    gate = gate_ref[0, 0, 0]
    o_s = acc_s[...] / jnp.maximum(l_s[...], 1e-30)
    o_w = acc_w[...] / jnp.maximum(l_w[...], 1e-30)
    o = gate[0:1, :] * o_c + gate[1:2, :] * o_s + gate[2:3, :] * o_w
    o_ref[0, 0, 0] = o.astype(o_ref.dtype)


def _nsa_attention(q_t, kcmp, vcmp_t, ks, vs_t, kw, vw_t, gate, tiles, band, c31, ovt):
    b, g, nqt = q_t.shape[:3]
    ncp = kcmp.shape[2]
    nkc = ks.shape[2]
    nb = ovt.shape[0]
    L = NSA_LANES
    body = functools.partial(_nsa_body, scale=NSA_DK ** -0.5, nb=nb)
    grp = lambda bb, gg, t: (bb, gg, 0, 0)
    grp5 = lambda bb, gg, t: (bb, gg, 0, 0, 0)
    return pl.pallas_call(
        body,
        out_shape=jax.ShapeDtypeStruct((b, g, nqt, NSA_DV, L), BF16),
        grid=(b, g, nqt),
        in_specs=[pl.BlockSpec((1, 1, 1, NSA_DK, L), lambda bb, gg, t: (bb, gg, t, 0, 0)),
                  pl.BlockSpec((1, 1, ncp, NSA_DK), grp),
                  pl.BlockSpec((1, 1, NSA_DV, ncp), grp),
                  pl.BlockSpec((1, 1, nkc, NSA_KC, NSA_DK), grp5),
                  pl.BlockSpec((1, 1, nkc, NSA_DV, NSA_KC), grp5),
                  pl.BlockSpec((1, 1, nkc, NSA_KC, NSA_DK), grp5),
                  pl.BlockSpec((1, 1, nkc, NSA_DV, NSA_KC), grp5),
                  pl.BlockSpec((1, 1, 1, N_BRANCH, L), lambda bb, gg, t: (bb, gg, t, 0, 0)),
                  pl.BlockSpec((1, 3, NSA_KC, L), lambda bb, gg, t: (gg, 0, 0, 0)),
                  pl.BlockSpec((1, 2, 16, L), lambda bb, gg, t: (gg, 0, 0, 0)),
                  pl.BlockSpec((1, 1, L), lambda bb, gg, t: (gg, 0, 0)),
                  pl.BlockSpec(ovt.shape, lambda bb, gg, t: (0, 0))],
        out_specs=pl.BlockSpec((1, 1, 1, NSA_DV, L), lambda bb, gg, t: (bb, gg, t, 0, 0)),
        scratch_shapes=[pltpu.VMEM((ncp, L), F32),
                        pltpu.VMEM((nb * SEL_BLOCK // NSA_KC, NSA_KC // SEL_BLOCK, L), F32),
                        pltpu.VMEM((1, L), F32), pltpu.VMEM((1, L), F32), pltpu.VMEM((NSA_DV, L), F32),
                        pltpu.VMEM((1, L), F32), pltpu.VMEM((1, L), F32), pltpu.VMEM((NSA_DV, L), F32)],
        compiler_params=_cparams(("parallel", "parallel", "arbitrary")),
        name="nsa_attention",
    )(q_t, kcmp, vcmp_t, ks, vs_t, kw, vw_t, gate, tiles, band, c31, ovt)


def _nsa_mixer(x2, b, s, w_in, pe_k, w1_k, w2_k, pe_v, w1_v, w2_v, rel_bias):
    t = b * s
    h, g, hg, dk, dv = NSA_HEADS, NSA_GROUPS, NSA_HG, NSA_DK, NSA_DV
    nqt = s // NSA_TQ
    nkc = s // NSA_KC
    nb = s // SEL_BLOCK
    sizes = [h * dk, g * dk, g * dv, g * dk, g * dv, g * dk, g * dv]
    cuts = [int(c) for c in np.cumsum(sizes)]
    n_main = cuts[-1]
    n_pad = -n_main % 128
    w_main = jnp.pad(w_in[:, :n_main], ((0, 0), (0, n_pad))).astype(BF16)
    y = _linear(x2, w_main, tm=512, tn=(n_main + n_pad) // 3, out_dtype=BF16, name="nsa_in")
    w_gate = jnp.pad(w_in[:, n_main:], ((0, 0), (0, 128 - h * N_BRANCH))).astype(BF16)
    gate = _linear(x2, w_gate, tm=512, tn=128, out_dtype=F32, act="sigmoid", name="nsa_gate")

    q = y[:, :cuts[0]].reshape(b, nqt, NSA_TQ, g, hg, dk)
    q_t = q.transpose(0, 3, 1, 5, 4, 2).reshape(b, g, nqt, dk, hg * NSA_TQ)
    kc_tok = y[:, cuts[0]:cuts[1]].reshape(b, s, g, dk)
    vc_tok = y[:, cuts[1]:cuts[2]].reshape(b, s, g, dv)

    def keys(a):
        return a.reshape(b, nkc, NSA_KC, g, dk).transpose(0, 3, 1, 2, 4)

    def vals_t(a):
        return a.reshape(b, nkc, NSA_KC, g, dv).transpose(0, 3, 1, 4, 2)

    ks, vs_t = keys(y[:, cuts[2]:cuts[3]]), vals_t(y[:, cuts[3]:cuts[4]])
    kw, vw_t = keys(y[:, cuts[4]:cuts[5]]), vals_t(y[:, cuts[5]:cuts[6]])
    gate = gate[:, :h * N_BRANCH].reshape(b, nqt, NSA_TQ, g, hg, N_BRANCH)
    gate = gate.transpose(0, 3, 1, 5, 4, 2).reshape(b, g, nqt, N_BRANCH, hg * NSA_TQ)

    nch = s // CMP_STRIDE

    def unfold(tok, d):
        ch = tok.reshape(b, nch, CMP_STRIDE, g, d).transpose(0, 3, 1, 2, 4).reshape(b, g, nch, CMP_STRIDE * d)
        nxt = jnp.concatenate([ch[:, :, 1:], jnp.zeros_like(ch[:, :, :1])], 2)
        return jnp.concatenate([ch, nxt], -1).reshape(b * g * nch, CMP_LEN * d)

    k_cmp = _compress(unfold(kc_tok, dk), pe_k, w1_k, w2_k).reshape(b, g, nch, dk)
    v_cmp = _compress(unfold(vc_tok, dv), pe_v, w1_v, w2_v).reshape(b, g, nch, dv)
    vcmp_t = v_cmp.transpose(0, 1, 3, 2)

    tiles, band, c31 = _nsa_tables(rel_bias, s)
    ovt = _overlap_t(nch, nb)
    o_t = _nsa_attention(q_t, k_cmp, vcmp_t, ks, vs_t, kw, vw_t, gate, tiles, band, c31, ovt)
    o = o_t.reshape(b, g, nqt, dv, hg, NSA_TQ).transpose(0, 2, 5, 1, 4, 3)
    return o.reshape(t, h * dv)


def _router_body(x_ref, w_ref, o_ref):
    x1, x2, x3 = _split3(x_ref[...])
    w1, w2, w3 = _split3(w_ref[...])
    acc = _dot(x1, w1)
    acc += _dot(x1, w2) + _dot(x2, w1)
    acc += _dot(x1, w3) + _dot(x2, w2) + _dot(x3, w1)
    o_ref[...] = acc


def _router(x, w, *, tm=512):
    m, d = x.shape
    wp = jnp.pad(w, ((0, 0), (0, 128 - w.shape[1])))
    return pl.pallas_call(
        _router_body,
        out_shape=jax.ShapeDtypeStruct((m, 128), F32),
        grid=(m // tm,),
        in_specs=[pl.BlockSpec((tm, d), lambda i: (i, 0)), pl.BlockSpec((d, 128), lambda i: (0, 0))],
        out_specs=pl.BlockSpec((tm, 128), lambda i: (i, 0)),
        compiler_params=_cparams(("parallel",)),
        name="moe_router",
    )(x, wp)[:, :w.shape[1]]


def _dispatch_body(ir_ref, ic_ref, fl_ref, x_ref, rt_ref, o_ref):
    i = pl.program_id(0)
    flag = fl_ref[i]
    tok = ic_ref[i] * MOE_TC + lax.broadcasted_iota(jnp.int32, (MOE_BLK, MOE_TC), 1)
    onehot = jnp.where(rt_ref[...] == tok, 1.0, 0.0).astype(BF16)
    rows = _dot(onehot, x_ref[...])

    @pl.when(flag == 3)
    def _():
        o_ref[...] = rows.astype(o_ref.dtype)

    @pl.when(flag == 1)
    def _():
        o_ref[...] = (o_ref[...].astype(F32) + rows).astype(o_ref.dtype)


def _dispatch(x_bf, row_tok_col, item_r, item_c, flags):
    t, d = x_bf.shape
    r = row_tok_col.shape[0]
    ni = item_r.shape[0]
    gs = pltpu.PrefetchScalarGridSpec(
        num_scalar_prefetch=3, grid=(ni,),
        in_specs=[pl.BlockSpec((MOE_TC, d), lambda i, ir, ic, fl: (ic[i], 0)),
                  pl.BlockSpec((MOE_BLK, 1), lambda i, ir, ic, fl: (ir[i], 0))],
        out_specs=pl.BlockSpec((MOE_BLK, d), lambda i, ir, ic, fl: (ir[i], 0)))
    return pl.pallas_call(
        _dispatch_body, grid_spec=gs,
        out_shape=jax.ShapeDtypeStruct((r, d), BF16),
        compiler_params=_cparams(("arbitrary",)),
        name="moe_dispatch",
    )(item_r, item_c, flags, x_bf, row_tok_col)


def _expert_body(be_ref, x_ref, wg_ref, wu_ref, wd_ref, rw_ref, o_ref):
    f = pl.program_id(1)
    xb = x_ref[...]
    h = jax.nn.silu(_dot(xb, wg_ref[0])) * _dot(xb, wu_ref[0])
    part = _dot(h.astype(BF16), wd_ref[0])

    @pl.when(f == 0)
    def _():
        o_ref[...] = part

    @pl.when(f > 0)
    def _():
        o_ref[...] += part

    @pl.when(f == pl.num_programs(1) - 1)
    def _():
        o_ref[...] = o_ref[...] * rw_ref[...]


def _experts(xs, wg, wu, wd, row_w_col, block_expert):
    r, d = xs.shape
    nbk = r // MOE_BLK
    nf = wg.shape[2] // MOE_TF
    gs = pltpu.PrefetchScalarGridSpec(
        num_scalar_prefetch=1, grid=(nbk, nf),
        in_specs=[pl.BlockSpec((MOE_BLK, d), lambda i, f, be: (i, 0)),
                  pl.BlockSpec((1, d, MOE_TF), lambda i, f, be: (be[i], 0, f)),
                  pl.BlockSpec((1, d, MOE_TF), lambda i, f, be: (be[i], 0, f)),
                  pl.BlockSpec((1, MOE_TF, d), lambda i, f, be: (be[i], f, 0)),
                  pl.BlockSpec((MOE_BLK, 1), lambda i, f, be: (i, 0))],
        out_specs=pl.BlockSpec((MOE_BLK, d), lambda i, f, be: (i, 0)))
    return pl.pallas_call(
        _expert_body, grid_spec=gs,
        out_shape=jax.ShapeDtypeStruct((r, d), F32),
        compiler_params=_cparams(("parallel", "arbitrary")),
        name="moe_experts",
    )(block_expert, xs, wg, wu, wd, row_w_col)


def _combine_body(ir_ref, ic_ref, fl_ref, y_ref, rt_ref, o_ref):
    i = pl.program_id(0)
    flag = fl_ref[i]
    tok = ic_ref[i] * MOE_TC + lax.broadcasted_iota(jnp.int32, (MOE_TC, MOE_BLK), 0)
    onehot = jnp.where(rt_ref[0] == tok, 1.0, 0.0).astype(BF16)
    y = y_ref[...]
    y1 = y.astype(BF16)
    y2 = (y - y1.astype(F32)).astype(BF16)
    part = _dot(onehot, y1) + _dot(onehot, y2)

    @pl.when(flag == 3)
    def _():
        o_ref[...] = part

    @pl.when(flag == 1)
    def _():
        o_ref[...] += part


def _combine(out_rows, row_tok_lane, item_r, item_c, flags, t):
    r, d = out_rows.shape
    ni = item_r.shape[0]
    gs = pltpu.PrefetchScalarGridSpec(
        num_scalar_prefetch=3, grid=(ni,),
        in_specs=[pl.BlockSpec((MOE_BLK, d), lambda i, ir, ic, fl: (ir[i], 0)),
                  pl.BlockSpec((1, 1, MOE_BLK), lambda i, ir, ic, fl: (ir[i], 0, 0))],
        out_specs=pl.BlockSpec((MOE_TC, d), lambda i, ir, ic, fl: (ic[i], 0)))
    return pl.pallas_call(
        _combine_body, grid_spec=gs,
        out_shape=jax.ShapeDtypeStruct((t, d), F32),
        compiler_params=_cparams(("arbitrary",)),
        name="moe_combine",
    )(item_r, item_c, flags, out_rows, row_tok_lane)


def _moe_plan(top_idx, wts, t):
    e, blk, tc = N_EXPERTS, MOE_BLK, MOE_TC
    a = t * TOP_K
    i32 = jnp.int32
    exp_flat = top_idx.reshape(a).astype(i32)
    order = jnp.argsort(exp_flat, stable=True)
    exp_sorted = exp_flat[order]
    tok_sorted = (order // TOP_K).astype(i32)
    w_sorted = wts.reshape(a)[order]
    counts = jnp.bincount(exp_flat, length=e).astype(i32)
    padded = ((counts + blk - 1) // blk) * blk
    grp_start = jnp.cumsum(counts) - counts
    pad_end = jnp.cumsum(padded)
    pad_start = pad_end - padded
    dest = pad_start[exp_sorted] + (jnp.arange(a, dtype=i32) - grp_start[exp_sorted])
    nbk = a // blk + e
    r = nbk * blk
    row_tok = jnp.full((r,), -1, i32).at[dest].set(tok_sorted)
    row_w = jnp.zeros((r,), F32).at[dest].set(w_sorted)
    block_expert = jnp.minimum(jnp.searchsorted(pad_end, jnp.arange(nbk, dtype=i32) * blk, side="right"),
                               e - 1).astype(i32)

    rt = row_tok.reshape(nbk, blk)
    valid = rt >= 0
    t_lo = jnp.min(jnp.where(valid, rt, t), 1)
    t_hi = jnp.max(rt, 1)
    has = t_hi >= 0
    c_lo = jnp.where(has, t_lo // tc, 0)
    c_hi = jnp.where(has, t_hi // tc, 0)
    n_it = c_hi - c_lo + 1
    off_end = jnp.cumsum(n_it)
    off_start = off_end - n_it
    total = off_end[-1]
    ni = nbk + e * (t // tc)
    idx = jnp.arange(ni, dtype=i32)
    ok = idx < total
    ir = jnp.minimum(jnp.searchsorted(off_end, idx, side="right"), nbk - 1).astype(i32)
    ic = jnp.where(ok, c_lo[ir] + idx - off_start[ir], c_hi[nbk - 1]).astype(i32)
    first = ok & (idx == off_start[ir])
    d_flags = ok.astype(i32) + 2 * first.astype(i32)

    key = jnp.where(ok, ic * nbk + ir, jnp.iinfo(jnp.int32).max)
    perm = jnp.argsort(key)
    ok2 = ok[perm]
    last = total - 1
    cr = jnp.where(ok2, ir[perm], ir[perm][last]).astype(i32)
    cc = jnp.where(ok2, ic[perm], ic[perm][last]).astype(i32)
    first2 = ok2 & jnp.concatenate([jnp.ones((1,), bool), cc[1:] != cc[:-1]])
    c_flags = ok2.astype(i32) + 2 * first2.astype(i32)
    return dict(row_tok=row_tok, row_w=row_w, block_expert=block_expert,
                d_items=(ir, ic, d_flags), c_items=(cr, cc, c_flags), nbk=nbk)


def _moe(x2, w_router, wg, wu, wd):
    t, d = x2.shape
    logits = _router(x2, w_router)
    top_val, top_idx = lax.top_k(logits, TOP_K)
    wts = jax.nn.softmax(top_val, -1)
    plan = _moe_plan(top_idx, wts, t)
    nbk = plan["nbk"]
    xs = _dispatch(x2.astype(BF16), plan["row_tok"].reshape(-1, 1), *plan["d_items"])
    out_rows = _experts(xs, wg.astype(BF16), wu.astype(BF16), wd.astype(BF16),
                        plan["row_w"].reshape(-1, 1), plan["block_expert"])
    return _combine(out_rows, plan["row_tok"].reshape(nbk, 1, MOE_BLK), *plan["c_items"], t)


def _forward(x, mla_w_in, mla_q_norm, mla_w_q_up, mla_kv_norm, mla_w_kv_up, mla_w_out, nsa_w_in,
             nsa_cmp_pe_k, nsa_cmp_w1_k, nsa_cmp_w2_k, nsa_cmp_pe_v, nsa_cmp_w1_v, nsa_cmp_w2_v,
             nsa_w_out, rel_bias, ffn_w_gate, ffn_w_up, ffn_w_down, moe_w_router, moe_w_gate,
             moe_w_up, moe_w_down, ln_mix_g, ln_mix_b, ln_ffn_g, ln_ffn_b):
    b, s, d = x.shape
    x2 = x.reshape(b * s, d)
    o = _mla_mixer(x2, b, s, mla_w_in[0], mla_q_norm[0], mla_w_q_up[0], mla_kv_norm[0], mla_w_kv_up[0])
    x2 = _linear(o, mla_w_out[0].astype(BF16), tm=512, tn=d, out_dtype=F32,
                 ln=(x2, ln_mix_g[0], ln_mix_b[0]), name="mla_out_ln")
    x2 = _ffn_ln(x2, ffn_w_gate[0].astype(BF16), ffn_w_up[0].astype(BF16), ffn_w_down[0].astype(BF16),
                 ln_ffn_g[0], ln_ffn_b[0])
    o = _nsa_mixer(x2, b, s, nsa_w_in[0], nsa_cmp_pe_k[0], nsa_cmp_w1_k[0], nsa_cmp_w2_k[0],
                   nsa_cmp_pe_v[0], nsa_cmp_w1_v[0], nsa_cmp_w2_v[0], rel_bias)
    x2 = _linear(o, nsa_w_out[0].astype(BF16), tm=512, tn=d, out_dtype=F32,
                 ln=(x2, ln_mix_g[1], ln_mix_b[1]), name="nsa_out_ln")
    f = _moe(x2, moe_w_router[0], moe_w_gate[0], moe_w_up[0], moe_w_down[0])
    x2 = _res_ln(x2, f, ln_ffn_g[1], ln_ffn_b[1])
    return x2.reshape(b, s, d)


kernel = jax.jit(_forward)
```

```python
import functools
import math

import numpy as np
import jax
import jax.numpy as jnp
from jax import lax
from jax.experimental import pallas as pl
from jax.experimental.pallas import tpu as pltpu

F32 = jnp.float32
BF16 = jnp.bfloat16

D_MODEL = 1024
DEPTH = 2

MLA_HEADS = 8
MLA_Q_RANK = 512
MLA_KV_RANK = 256
MLA_NOPE = 128
MLA_ROPE = 64
MLA_V = 128
ROPE_THETA = 10000.0

NSA_HEADS = 16
NSA_GROUPS = 4
NSA_HG = NSA_HEADS // NSA_GROUPS
NSA_DK = 96
NSA_DV = 64
CMP_LEN = 32
CMP_STRIDE = 16
SEL_BLOCK = 64
SEL_TOPN = 16
WINDOW = 512
N_BRANCH = 3
FORCE = 1e6

REL_BUCKETS = 32
REL_MAX_DIST = 128

D_FF = 2816
N_EXPERTS = 8
TOP_K = 2
D_FF_EXPERT = 3584

LN_EPS = 1e-5
RMS_EPS = 1e-6

ALPHA = (2.0 * DEPTH) ** 0.25

NEG = -1e30
NEG_TEST = -1e29

V7X_VMEM_LIMIT = 56 * 1024 * 1024

LOG2E = 1.4426950408889634

NSA_TQ = 256
NSA_LANES = NSA_HG * NSA_TQ
NSA_KC = 256
NSA_BAND = 24

MOE_BLK = 512
MOE_TC = 256
MOE_TF = 512


def _cparams(sem, vmem=V7X_VMEM_LIMIT):
    return pltpu.CompilerParams(dimension_semantics=sem, vmem_limit_bytes=vmem)


def _layer_norm(r, g, b):
    mu = jnp.mean(r, -1, keepdims=True)
    d = r - mu
    var = jnp.mean(d * d, -1, keepdims=True)
    return d * lax.rsqrt(var + LN_EPS) * g + b


def _rms_norm(x, g):
    return x * lax.rsqrt(jnp.mean(x * x, -1, keepdims=True) + RMS_EPS) * g


def _split3(a):
    a1 = a.astype(BF16)
    r1 = a - a1.astype(F32)
    a2 = r1.astype(BF16)
    a3 = (r1 - a2.astype(F32)).astype(BF16)
    return a1, a2, a3


def _dot(a, b):
    return jnp.dot(a, b, preferred_element_type=F32)


def _dot_nt(a, b):
    return lax.dot_general(a, b, (((1,), (1,)), ((), ())), preferred_element_type=F32)


def _linear_body(*refs, has_norm, has_scale, has_ln, act):
    it = iter(refs)
    x_ref = next(it)
    w_ref = next(it)
    g_ref = next(it) if has_norm else None
    cs_ref = next(it) if has_scale else None
    if has_ln:
        res_ref, lg_ref, lb_ref = next(it), next(it), next(it)
    o_ref = next(it)
    x = x_ref[...]
    if has_norm:
        x = _rms_norm(x.astype(F32), g_ref[...])
    acc = _dot(x.astype(BF16), w_ref[...])
    if has_scale:
        acc = acc * cs_ref[...]
    if act == "sigmoid":
        acc = jax.nn.sigmoid(acc)
    if has_ln:
        acc = _layer_norm(ALPHA * res_ref[...] + acc, lg_ref[...], lb_ref[...])
    o_ref[...] = acc.astype(o_ref.dtype)


def _linear(x, w, *, tm, tn, out_dtype, xcol=0, norm_gain=None, colscale=None, ln=None, act=None, name):
    m = x.shape[0]
    k, n = w.shape
    assert m % tm == 0 and n % tn == 0
    in_specs = [pl.BlockSpec((tm, k), lambda i, j: (i, xcol)),
                pl.BlockSpec((k, tn), lambda i, j: (0, j))]
    args = [x, w]
    if norm_gain is not None:
        in_specs.append(pl.BlockSpec((1, k), lambda i, j: (0, 0)))
        args.append(norm_gain.reshape(1, k))
    if colscale is not None:
        in_specs.append(pl.BlockSpec((1, tn), lambda i, j: (0, j)))
        args.append(colscale.reshape(1, n))
    if ln is not None:
        assert tn == n
        res, lg, lb = ln
        in_specs += [pl.BlockSpec((tm, n), lambda i, j: (i, 0)),
                     pl.BlockSpec((1, n), lambda i, j: (0, 0)),
                     pl.BlockSpec((1, n), lambda i, j: (0, 0))]
        args += [res, lg.reshape(1, n), lb.reshape(1, n)]
    body = functools.partial(_linear_body, has_norm=norm_gain is not None,
                             has_scale=colscale is not None, has_ln=ln is not None, act=act)
    return pl.pallas_call(
        body,
        out_shape=jax.ShapeDtypeStruct((m, n), out_dtype),
        grid=(m // tm, n // tn),
        in_specs=in_specs,
        out_specs=pl.BlockSpec((tm, tn), lambda i, j: (i, j)),
        compiler_params=_cparams(("parallel", "arbitrary")),
        name=name,
    )(*args)


def _res_ln_body(x_ref, f_ref, g_ref, b_ref, o_ref):
    o_ref[...] = _layer_norm(ALPHA * x_ref[...] + f_ref[...], g_ref[...], b_ref[...])


def _res_ln(x, f, g, b, *, tm=512):
    m, n = x.shape
    row = pl.BlockSpec((tm, n), lambda i: (i, 0))
    vec = pl.BlockSpec((1, n), lambda i: (0, 0))
    return pl.pallas_call(
        _res_ln_body,
        out_shape=jax.ShapeDtypeStruct((m, n), F32),
        grid=(m // tm,),
        in_specs=[row, row, vec, vec],
        out_specs=row,
        compiler_params=_cparams(("parallel",)),
        name="res_ln",
    )(x, f, g.reshape(1, n), b.reshape(1, n))


def _rope_tables(s):
    half = MLA_ROPE // 2
    freq = ROPE_THETA ** (-jnp.arange(half, dtype=F32) / half)
    ang = jnp.arange(s).astype(F32)[:, None] * freq[None, :]
    cos, sin = jnp.cos(ang), jnp.sin(ang)
    return jnp.concatenate([cos, cos], -1), jnp.concatenate([-sin, sin], -1)


MLA_DQ = MLA_NOPE + MLA_ROPE
MLA_QROWS = MLA_NOPE + 2 * MLA_ROPE
MLA_T = 512
MLA_HPS = 2


def _mla_q_body(lat_ref, g_ref, w_ref, cos_ref, sin_ref, o_ref, *, qscale):
    xn = _rms_norm(lat_ref[...], g_ref[...]).astype(BF16)
    y = _dot_nt(w_ref[...], xn)
    cos, sin = cos_ref[...], sin_ref[...]
    for h in range(MLA_HEADS):
        r0 = h * MLA_QROWS
        o_ref[0, h, :MLA_NOPE, :] = (y[r0:r0 + MLA_NOPE] * qscale).astype(BF16)
        a = y[r0 + MLA_NOPE:r0 + MLA_DQ]
        bb = y[r0 + MLA_DQ:r0 + MLA_QROWS]
        o_ref[0, h, MLA_NOPE:, :] = ((a * cos + bb * sin) * qscale).astype(BF16)


def _mla_q_proj(lat, gain, w_t, cos_t, sin_t, b, s, *, tm=MLA_T):
    ns = s // tm
    body = functools.partial(_mla_q_body, qscale=(MLA_DQ ** -0.5) * LOG2E)
    return pl.pallas_call(
        body,
        out_shape=jax.ShapeDtypeStruct((b, MLA_HEADS, MLA_DQ, s), BF16),
        grid=(b * ns,),
        in_specs=[pl.BlockSpec((tm, MLA_Q_RANK), lambda i: (i, 0)),
                  pl.BlockSpec((1, MLA_Q_RANK), lambda i: (0, 0)),
                  pl.BlockSpec(w_t.shape, lambda i: (0, 0)),
                  pl.BlockSpec((MLA_ROPE, tm), lambda i: (0, i % ns)),
                  pl.BlockSpec((MLA_ROPE, tm), lambda i: (0, i % ns))],
        out_specs=pl.BlockSpec((1, MLA_HEADS, MLA_DQ, tm), lambda i: (i // ns, 0, 0, i % ns)),
        compiler_params=_cparams(("parallel",)),
        name="mla_q_proj",
    )(lat, gain.reshape(1, -1), w_t, cos_t, sin_t)


def _mla_kv_body(lat_ref, g_ref, wk_ref, wvt_ref, kr_ref, cos_ref, sin_ref, k_ref, vt_ref):
    xn = _rms_norm(lat_ref[...], g_ref[...]).astype(BF16)
    kn = _dot(xn, wk_ref[...]).astype(BF16)
    vt = _dot_nt(wvt_ref[...], xn).astype(BF16)
    kr = kr_ref[...]
    rot = (kr[:, :MLA_ROPE] * cos_ref[...] + kr[:, MLA_ROPE:] * sin_ref[...]).astype(BF16)
    for h in range(MLA_HEADS):
        k_ref[0, h, :, :MLA_NOPE] = kn[:, h * MLA_NOPE:(h + 1) * MLA_NOPE]
        k_ref[0, h, :, MLA_NOPE:] = rot
        vt_ref[0, h, 0] = vt[h * MLA_V:(h + 1) * MLA_V]


def _mla_kv_proj(lat, gain, wk, wv_t, cosx, sinx, b, s, *, tm=MLA_T):
    ns = s // tm
    return pl.pallas_call(
        _mla_kv_body,
        out_shape=(jax.ShapeDtypeStruct((b, MLA_HEADS, s, MLA_DQ), BF16),
                   jax.ShapeDtypeStruct((b, MLA_HEADS, ns, MLA_V, tm), BF16)),
        grid=(b * ns,),
        in_specs=[pl.BlockSpec((tm, MLA_KV_RANK), lambda i: (i, MLA_Q_RANK // MLA_KV_RANK)),
                  pl.BlockSpec((1, MLA_KV_RANK), lambda i: (0, 0)),
                  pl.BlockSpec(wk.shape, lambda i: (0, 0)),
                  pl.BlockSpec(wv_t.shape, lambda i: (0, 0)),
                  pl.BlockSpec((tm, 2 * MLA_ROPE),
                               lambda i: (i, (MLA_Q_RANK + MLA_KV_RANK) // (2 * MLA_ROPE))),
                  pl.BlockSpec((tm, MLA_ROPE), lambda i: (i % ns, 0)),
                  pl.BlockSpec((tm, MLA_ROPE), lambda i: (i % ns, 0))],
        out_specs=(pl.BlockSpec((1, MLA_HEADS, tm, MLA_DQ), lambda i: (i // ns, 0, i % ns, 0)),
                   pl.BlockSpec((1, MLA_HEADS, 1, MLA_V, tm), lambda i: (i // ns, 0, i % ns, 0, 0))),
        compiler_params=_cparams(("parallel",)),
        name="mla_kv_proj",
    )(lat, gain.reshape(1, -1), wk, wv_t, lat, cosx, sinx)


def _mla_attn_body(qt_ref, k_ref, vt_ref, o_ref, *scr):
    i = pl.program_id(2)
    tk = MLA_T
    streams = [scr[3 * h:3 * h + 3] for h in range(MLA_HPS)]
    for m_scr, l_scr, acc_scr in streams:
        m_scr[...] = jnp.full_like(m_scr, NEG)
        l_scr[...] = jnp.zeros_like(l_scr)
        acc_scr[...] = jnp.zeros_like(acc_scr)

    def chunk(j, h, masked):
        m_scr, l_scr, acc_scr = streams[h]
        k = k_ref[0, h, pl.ds(pl.multiple_of(j * tk, tk), tk), :]
        s = _dot(k, qt_ref[0, h])
        if masked:
            key = lax.broadcasted_iota(jnp.int32, s.shape, 0)
            qry = lax.broadcasted_iota(jnp.int32, s.shape, 1)
            s = jnp.where(key <= qry, s, NEG)
        m_old = m_scr[...]
        m_new = jnp.maximum(m_old, jnp.max(s, 0, keepdims=True))
        a = jnp.exp2(m_old - m_new)
        p = jnp.exp2(s - m_new)
        l_scr[...] = a * l_scr[...] + jnp.sum(p, 0, keepdims=True)
        acc_scr[...] = a * acc_scr[...] + _dot(vt_ref[0, h, j], p.astype(BF16))
        m_scr[...] = m_new

    def far_body(j, carry):
        for h in range(MLA_HPS):
            chunk(j, h, False)
        return carry

    lax.fori_loop(0, i, far_body, 0)
    for h in range(MLA_HPS):
        chunk(i, h, True)
    for h, (m_scr, l_scr, acc_scr) in enumerate(streams):
        o_t = acc_scr[...] * (1.0 / jnp.maximum(l_scr[...], 1e-30))
        o_ref[0, :, h * MLA_V:(h + 1) * MLA_V] = o_t.T.astype(o_ref.dtype)


def _mla_attention(q_t, k, v_t, b, s):
    tq = MLA_T
    nq = s // tq
    hp = MLA_HPS
    stream = [pltpu.VMEM((1, tq), F32), pltpu.VMEM((1, tq), F32), pltpu.VMEM((MLA_V, tq), F32)]
    return pl.pallas_call(
        _mla_attn_body,
        out_shape=jax.ShapeDtypeStruct((b, s, MLA_HEADS * MLA_V), BF16),
        grid=(b, MLA_HEADS // hp, nq),
        in_specs=[pl.BlockSpec((1, hp, MLA_DQ, tq), lambda bb, h, i: (bb, h, 0, i)),
                  pl.BlockSpec((1, hp, s, MLA_DQ), lambda bb, h, i: (bb, h, 0, 0)),
                  pl.BlockSpec((1, hp, nq, MLA_V, tq), lambda bb, h, i: (bb, h, 0, 0, 0))],
        out_specs=pl.BlockSpec((1, tq, hp * MLA_V), lambda bb, h, i: (bb, i, h)),
        scratch_shapes=hp * stream,
        compiler_params=_cparams(("parallel", "parallel", "arbitrary")),
        name="mla_attention",
    )(q_t, k, v_t)


def _mla_mixer(x2, b, s, w_in, q_norm, w_q_up, kv_norm, w_kv_up):
    r0 = MLA_Q_RANK + MLA_KV_RANK
    half = MLA_ROPE // 2
    w_in_ext = jnp.concatenate([w_in, w_in[:, r0 + half:r0 + MLA_ROPE], w_in[:, r0:r0 + half]], 1)
    lat = _linear(x2, w_in_ext.astype(BF16), tm=512, tn=w_in_ext.shape[1], out_dtype=F32, name="mla_in")
    wq = w_q_up.reshape(MLA_Q_RANK, MLA_HEADS, MLA_DQ)
    wr = wq[..., MLA_NOPE:]
    wq = jnp.concatenate([wq, wr[..., half:], wr[..., :half]], -1)
    wq_t = wq.reshape(MLA_Q_RANK, MLA_HEADS * MLA_QROWS).T.astype(BF16)
    wkv = w_kv_up.reshape(MLA_KV_RANK, MLA_HEADS, MLA_NOPE + MLA_V)
    wk = wkv[..., :MLA_NOPE].reshape(MLA_KV_RANK, MLA_HEADS * MLA_NOPE).astype(BF16)
    wv_t = wkv[..., MLA_NOPE:].reshape(MLA_KV_RANK, MLA_HEADS * MLA_V).T.astype(BF16)
    cosx, sinx = _rope_tables(s)
    q_t = _mla_q_proj(lat, q_norm, wq_t, cosx.T, sinx.T, b, s)
    k, v_t = _mla_kv_proj(lat, kv_norm, wk, wv_t, cosx, sinx, b, s)
    o = _mla_attention(q_t, k, v_t, b, s)
    return o.reshape(b * s, MLA_HEADS * MLA_V)


def _ffn_body(x_ref, wg_ref, wu_ref, wd_ref, lg_ref, lb_ref, o_ref, acc_scr):
    f = pl.program_id(1)
    xb = x_ref[...].astype(BF16)
    h = jax.nn.silu(_dot(xb, wg_ref[...])) * _dot(xb, wu_ref[...])
    part = _dot(h.astype(BF16), wd_ref[...])

    @pl.when(f == 0)
    def _():
        acc_scr[...] = part

    @pl.when(f > 0)
    def _():
        acc_scr[...] += part

    @pl.when(f == pl.num_programs(1) - 1)
    def _():
        o_ref[...] = _layer_norm(ALPHA * x_ref[...] + acc_scr[...], lg_ref[...], lb_ref[...])


def _ffn_ln(x, wg, wu, wd, lg, lb, *, tm=512, tf=1408):
    m, d = x.shape
    dff = wg.shape[1]
    assert dff % tf == 0
    vec = pl.BlockSpec((1, d), lambda i, f: (0, 0))
    return pl.pallas_call(
        _ffn_body,
        out_shape=jax.ShapeDtypeStruct((m, d), F32),
        grid=(m // tm, dff // tf),
        in_specs=[pl.BlockSpec((tm, d), lambda i, f: (i, 0)),
                  pl.BlockSpec((d, tf), lambda i, f: (0, f)),
                  pl.BlockSpec((d, tf), lambda i, f: (0, f)),
                  pl.BlockSpec((tf, d), lambda i, f: (f, 0)),
                  vec, vec],
        out_specs=pl.BlockSpec((tm, d), lambda i, f: (i, 0)),
        scratch_shapes=[pltpu.VMEM((tm, d), F32)],
        compiler_params=_cparams(("parallel", "arbitrary")),
        name="ffn_ln",
    )(x, wg, wu, wd, lg.reshape(1, d), lb.reshape(1, d))


def _compress_body(a_ref, pe_ref, w1_ref, w2_ref, o_ref):
    a = (a_ref[...].astype(F32) + pe_ref[...]).astype(BF16)
    h = jax.nn.gelu(_dot(a, w1_ref[...]))
    o_ref[...] = _dot(h.astype(BF16), w2_ref[...]).astype(o_ref.dtype)


def _compress(a, pe, w1, w2, *, tm=512):
    m, k = a.shape
    d = w2.shape[1]
    return pl.pallas_call(
        _compress_body,
        out_shape=jax.ShapeDtypeStruct((m, d), BF16),
        grid=(m // tm,),
        in_specs=[pl.BlockSpec((tm, k), lambda i: (i, 0)),
                  pl.BlockSpec((1, k), lambda i: (0, 0)),
                  pl.BlockSpec((k, d), lambda i: (0, 0)),
                  pl.BlockSpec((d, d), lambda i: (0, 0))],
        out_specs=pl.BlockSpec((tm, d), lambda i: (i, 0)),
        compiler_params=_cparams(("parallel",)),
        name="nsa_compress",
    )(a, pe.reshape(1, k), w1.reshape(k, d).astype(BF16), w2.astype(BF16))


def _rel_bucket_np(dist):
    n = np.maximum(dist, 0)
    max_exact = REL_BUCKETS // 2
    nf = np.maximum(n, 1).astype(np.float32)
    large = max_exact + (np.log(nf / np.float32(max_exact)) / np.float32(math.log(REL_MAX_DIST / max_exact))
                         * np.float32(REL_BUCKETS - max_exact)).astype(np.int32)
    large = np.minimum(large, REL_BUCKETS - 1)
    return np.where(n < max_exact, n, large).astype(np.int32)


def _nsa_tables(rel_bias, s):
    g, hg, tq, kc = NSA_GROUPS, NSA_HG, NSA_TQ, NSA_KC
    assert tq == kc and tq % CMP_STRIDE == 0 and WINDOW == 2 * kc
    assert np.all(_rel_bucket_np(np.arange(tq // 2 - 15, s + tq)) == REL_BUCKETS - 1)
    rb = rel_bias.reshape(REL_BUCKETS, g, hg) * LOG2E

    def tile(dist, valid):
        vals = rb[_rel_bucket_np(dist)]
        vals = jnp.where(valid[:, :, None, None], vals, NEG)
        return vals.transpose(2, 0, 3, 1).reshape(g, dist.shape[0], hg * tq)

    i = np.arange(kc)[:, None]
    j = np.arange(tq)[None, :]
    d0 = j - i
    d1 = tq + j - i
    d2 = 2 * tq + j - i
    tiles = jnp.stack([tile(d0, d0 >= 0), tile(d1, d1 >= 0), tile(d2, d2 < WINDOW)], 1)
    c31 = jnp.broadcast_to(rb[REL_BUCKETS - 1][:, None, :, None], (g, 1, hg, tq)).reshape(g, 1, hg * tq)
    r = np.arange(NSA_BAND)[:, None]
    dc_gen = j + (8 * CMP_STRIDE - CMP_LEN + 1) - CMP_STRIDE * r
    dc_first = j - CMP_STRIDE * r - (CMP_LEN - 1)
    band = jnp.stack([tile(dc_gen, dc_gen >= 0), tile(dc_first, dc_first >= 0)], 1)
    band = jnp.where(band > NEG_TEST, band - c31[:, None], NEG)
    return tiles, band, c31


def _overlap_t(nc_pad, nb):
    n = np.arange(nc_pad)[None, :]
    jb = np.arange(nb)[:, None]
    cstart = n * CMP_STRIDE
    cend = cstart + CMP_LEN - 1
    sstart = jb * SEL_BLOCK
    ov = (cstart <= sstart + SEL_BLOCK - 1) & (cend >= sstart) & (n < nc_pad - 1)
    return jnp.asarray(ov.astype(np.float32), BF16)


def _nsa_body(qt_ref, kc_ref, vct_ref, ks_ref, vst_ref, kw_ref, vwt_ref, gate_ref, tiles_ref,
              band_ref, c31_ref, ovt_ref, o_ref,
              s_scr, sel_scr, m_a, l_a, acc_a, m_b, l_b, acc_b, m_w, l_w, acc_w, *, nb):
    t = pl.program_id(2)
    L = NSA_LANES
    q_t = qt_ref[0, 0, 0]
    c31 = c31_ref[0]
    ncp = kc_ref.shape[2]
    per = NSA_KC // SEL_BLOCK
    blocks_per_tile = NSA_TQ // CMP_STRIDE

    s = _dot(kc_ref[0, 0], q_t) + c31
    first = (t == 0).astype(jnp.int32)
    bs = pl.multiple_of((blocks_per_tile * t - 8) * (1 - first), 8)
    row = lax.broadcasted_iota(jnp.int32, (ncp, L), 0)
    s_scr[...] = jnp.where(row < bs + NSA_BAND, s, NEG)
    s_scr[pl.ds(bs, NSA_BAND), :] += band_ref[0, first]
    s = s_scr[...]
    m = jnp.max(s, 0, keepdims=True)
    m = jnp.where(m < NEG_TEST, 0.0, m)
    p = jnp.exp2(s - m)
    den = jnp.maximum(jnp.sum(p, 0, keepdims=True), 1e-30)
    p = p * (1.0 / den)
    o_c = _dot(vct_ref[0, 0], p.astype(BF16))

    psum = p[:, 0:NSA_TQ]
    for h in range(1, NSA_HG):
        psum = psum + p[:, h * NSA_TQ:(h + 1) * NSA_TQ]
    p1, p2, p3 = _split3(psum)
    ovt = ovt_ref[...]
    imp = _dot(ovt, p1) + _dot(ovt, p2) + _dot(ovt, p3)
    blk = lax.broadcasted_iota(jnp.int32, (nb, NSA_TQ), 0)
    lane = lax.broadcasted_iota(jnp.int32, (nb, NSA_TQ), 1)
    cur = (NSA_TQ // SEL_BLOCK) * t + lane // SEL_BLOCK
    forced = (blk == 0) | (blk == cur) | (blk == cur - 1)
    v = jnp.where(blk > cur, -FORCE, jnp.where(forced, FORCE, imp))
    blk_f = blk.astype(F32)
    sel = jnp.zeros((nb, NSA_TQ), F32)
    for _ in range(min(SEL_TOPN, nb)):
        mx = jnp.max(v, 0, keepdims=True)
        idx = jnp.min(jnp.where(v == mx, blk_f, float(nb)), 0, keepdims=True)
        hit = blk_f == idx
        sel = jnp.where(hit, 1.0, sel)
        v = jnp.where(hit, -jnp.inf, v)
    selneg = jnp.where(sel > 0.5, 0.0, NEG)
    selneg = jnp.concatenate([selneg] * NSA_HG, 1)
    for c in range(nb // per):
        sel_scr[c] = selneg[per * c:per * (c + 1), :]

    def chunk(kc_idx, add, k_ref, vt_ref, st):
        m_scr, l_scr, acc_scr = st
        sc = _dot(k_ref[0, 0, kc_idx], q_t) + add
        m_old = m_scr[...]
        m_new = jnp.maximum(m_old, jnp.max(sc, 0, keepdims=True))
        a = jnp.exp2(m_old - m_new)
        pp = jnp.exp2(sc - m_new)
        l_scr[...] = a * l_scr[...] + jnp.sum(pp, 0, keepdims=True)
        acc_scr[...] = a * acc_scr[...] + _dot(vt_ref[0, 0, kc_idx], pp.astype(BF16))
        m_scr[...] = m_new

    def sel_rows(kc_idx, base):
        rows = sel_scr[kc_idx]
        parts = [jnp.broadcast_to(rows[r:r + 1, :] + base, (SEL_BLOCK, L)) for r in range(per)]
        return jnp.concatenate(parts, 0)

    st_a, st_b, st_w = (m_a, l_a, acc_a), (m_b, l_b, acc_b), (m_w, l_w, acc_w)
    for m_scr, l_scr, acc_scr in (st_a, st_b, st_w):
        m_scr[...] = jnp.full_like(m_scr, NEG)
        l_scr[...] = jnp.zeros_like(l_scr)
        acc_scr[...] = jnp.zeros_like(acc_scr)

    n_far = jnp.maximum(t - 1, 0)

    def pair_body(i, carry):
        chunk(2 * i, sel_rows(2 * i, c31), ks_ref, vst_ref, st_a)
        chunk(2 * i + 1, sel_rows(2 * i + 1, c31), ks_ref, vst_ref, st_b)
        return carry

    lax.fori_loop(0, n_far // 2, pair_body, 0)

    def sel_chunk(delta, st):
        kc_idx = t - delta
        chunk(kc_idx, sel_rows(kc_idx, 0.0) + tiles_ref[0, delta], ks_ref, vst_ref, st)

    def win_chunk(delta):
        chunk(t - delta, tiles_ref[0, delta], kw_ref, vwt_ref, st_w)

    @pl.when(t >= 2)
    def _():
        @pl.when(n_far % 2 == 1)
        def _():
            chunk(n_far - 1, sel_rows(n_far - 1, c31), ks_ref, vst_ref, st_a)
        sel_chunk(1, st_b)
        sel_chunk(0, st_a)
        win_chunk(0)
        win_chunk(1)
        win_chunk(2)

    @pl.when(t == 1)
    def _():
        sel_chunk(1, st_b)
        sel_chunk(0, st_a)
        win_chunk(0)
        win_chunk(1)

    @pl.when(t == 0)
    def _():
        sel_chunk(0, st_a)
        win_chunk(0)

    gate = gate_ref[0, 0, 0]
    m_s = jnp.maximum(m_a[...], m_b[...])
    w_a = jnp.exp2(m_a[...] - m_s)
    w_b = jnp.exp2(m_b[...] - m_s)
    l_s = l_a[...] * w_a + l_b[...] * w_b
    o_s = (acc_a[...] * w_a + acc_b[...] * w_b) * (1.0 / jnp.maximum(l_s, 1e-30))
    o_w = acc_w[...] * (1.0 / jnp.maximum(l_w[...], 1e-30))
    o = gate[0:1, :] * o_c + gate[1:2, :] * o_s + gate[2:3, :] * o_w
    o_ref[0, 0, 0] = o.astype(o_ref.dtype)


def _nsa_attention(q_t, kcmp, vcmp_t, ks, vs_t, kw, vw_t, gate, tiles, band, c31, ovt):
    b, g, nqt = q_t.shape[:3]
    ncp = kcmp.shape[2]
    nkc = ks.shape[2]
    nb = ovt.shape[0]
    L = NSA_LANES
    body = functools.partial(_nsa_body, nb=nb)
    stream = [pltpu.VMEM((1, L), F32), pltpu.VMEM((1, L), F32), pltpu.VMEM((NSA_DV, L), F32)]
    grp = lambda bb, gg, t: (bb, gg, 0, 0)
    grp5 = lambda bb, gg, t: (bb, gg, 0, 0, 0)
    return pl.pallas_call(
        body,
        out_shape=jax.ShapeDtypeStruct((b, g, nqt, NSA_DV, L), BF16),
        grid=(b, g, nqt),
        in_specs=[pl.BlockSpec((1, 1, 1, NSA_DK, L), lambda bb, gg, t: (bb, gg, t, 0, 0)),
                  pl.BlockSpec((1, 1, ncp, NSA_DK), grp),
                  pl.BlockSpec((1, 1, NSA_DV, ncp), grp),
                  pl.BlockSpec((1, 1, nkc, NSA_KC, NSA_DK), grp5),
                  pl.BlockSpec((1, 1, nkc, NSA_DV, NSA_KC), grp5),
                  pl.BlockSpec((1, 1, nkc, NSA_KC, NSA_DK), grp5),
                  pl.BlockSpec((1, 1, nkc, NSA_DV, NSA_KC), grp5),
                  pl.BlockSpec((1, 1, 1, N_BRANCH, L), lambda bb, gg, t: (bb, gg, t, 0, 0)),
                  pl.BlockSpec((1, 3, NSA_KC, L), lambda bb, gg, t: (gg, 0, 0, 0)),
                  pl.BlockSpec((1, 2, NSA_BAND, L), lambda bb, gg, t: (gg, 0, 0, 0)),
                  pl.BlockSpec((1, 1, L), lambda bb, gg, t: (gg, 0, 0)),
                  pl.BlockSpec(ovt.shape, lambda bb, gg, t: (0, 0))],
        out_specs=pl.BlockSpec((1, 1, 1, NSA_DV, L), lambda bb, gg, t: (bb, gg, t, 0, 0)),
        scratch_shapes=[pltpu.VMEM((ncp, L), F32),
                        pltpu.VMEM((nb * SEL_BLOCK // NSA_KC, NSA_KC // SEL_BLOCK, L), F32)] + 3 * stream,
        compiler_params=_cparams(("parallel", "parallel", "arbitrary")),
        name="nsa_attention",
    )(q_t, kcmp, vcmp_t, ks, vs_t, kw, vw_t, gate, tiles, band, c31, ovt)


def _nsa_mixer(x2, b, s, w_in, pe_k, w1_k, w2_k, pe_v, w1_v, w2_v, rel_bias):
    t = b * s
    h, g, hg, dk, dv = NSA_HEADS, NSA_GROUPS, NSA_HG, NSA_DK, NSA_DV
    nqt = s // NSA_TQ
    nkc = s // NSA_KC
    nb = s // SEL_BLOCK
    sizes = [h * dk, g * dk, g * dv, g * dk, g * dv, g * dk, g * dv]
    cuts = [int(c) for c in np.cumsum(sizes)]
    n_main = cuts[-1]
    n_pad = -n_main % 128
    w_main = jnp.pad(w_in[:, :n_main], ((0, 0), (0, n_pad))).astype(BF16)
    colscale = jnp.where(jnp.arange(n_main + n_pad) < cuts[0], (dk ** -0.5) * LOG2E, 1.0).astype(F32)
    y = _linear(x2, w_main, tm=512, tn=(n_main + n_pad) // 3, out_dtype=BF16, colscale=colscale,
                name="nsa_in")
    w_gate = jnp.pad(w_in[:, n_main:], ((0, 0), (0, 128 - h * N_BRANCH))).astype(BF16)
    gate = _linear(x2, w_gate, tm=512, tn=128, out_dtype=F32, act="sigmoid", name="nsa_gate")

    q = y[:, :cuts[0]].reshape(b, nqt, NSA_TQ, g, hg, dk)
    q_t = q.transpose(0, 3, 1, 5, 4, 2).reshape(b, g, nqt, dk, hg * NSA_TQ)
    kc_tok = y[:, cuts[0]:cuts[1]].reshape(b, s, g, dk)
    vc_tok = y[:, cuts[1]:cuts[2]].reshape(b, s, g, dv)

    def keys(a):
        return a.reshape(b, nkc, NSA_KC, g, dk).transpose(0, 3, 1, 2, 4)

    def vals_t(a):
        return a.reshape(b, nkc, NSA_KC, g, dv).transpose(0, 3, 1, 4, 2)

    ks, vs_t = keys(y[:, cuts[2]:cuts[3]]), vals_t(y[:, cuts[3]:cuts[4]])
    kw, vw_t = keys(y[:, cuts[4]:cuts[5]]), vals_t(y[:, cuts[5]:cuts[6]])
    gate = gate[:, :h * N_BRANCH].reshape(b, nqt, NSA_TQ, g, hg, N_BRANCH)
    gate = gate.transpose(0, 3, 1, 5, 4, 2).reshape(b, g, nqt, N_BRANCH, hg * NSA_TQ)

    nch = s // CMP_STRIDE

    def unfold(tok, d):
        ch = tok.reshape(b, nch, CMP_STRIDE, g, d).transpose(0, 3, 1, 2, 4).reshape(b, g, nch, CMP_STRIDE * d)
        nxt = jnp.concatenate([ch[:, :, 1:], jnp.zeros_like(ch[:, :, :1])], 2)
        return jnp.concatenate([ch, nxt], -1).reshape(b * g * nch, CMP_LEN * d)

    k_cmp = _compress(unfold(kc_tok, dk), pe_k, w1_k, w2_k).reshape(b, g, nch, dk)
    v_cmp = _compress(unfold(vc_tok, dv), pe_v, w1_v, w2_v).reshape(b, g, nch, dv)
    vcmp_t = v_cmp.transpose(0, 1, 3, 2)

    tiles, band, c31 = _nsa_tables(rel_bias, s)
    ovt = _overlap_t(nch, nb)
    o_t = _nsa_attention(q_t, k_cmp, vcmp_t, ks, vs_t, kw, vw_t, gate, tiles, band, c31, ovt)
    o = o_t.reshape(b, g, nqt, dv, hg, NSA_TQ).transpose(0, 2, 5, 1, 4, 3)
    return o.reshape(t, h * dv)


def _router_body(x_ref, w_ref, o_ref):
    x1, x2, x3 = _split3(x_ref[...])
    w1, w2, w3 = _split3(w_ref[...])
    acc = _dot(x1, w1)
    acc += _dot(x1, w2) + _dot(x2, w1)
    acc += _dot(x1, w3) + _dot(x2, w2) + _dot(x3, w1)
    o_ref[...] = acc


def _router(x, w, *, tm=512):
    m, d = x.shape
    wp = jnp.pad(w, ((0, 0), (0, 128 - w.shape[1])))
    return pl.pallas_call(
        _router_body,
        out_shape=jax.ShapeDtypeStruct((m, 128), F32),
        grid=(m // tm,),
        in_specs=[pl.BlockSpec((tm, d), lambda i: (i, 0)), pl.BlockSpec((d, 128), lambda i: (0, 0))],
        out_specs=pl.BlockSpec((tm, 128), lambda i: (i, 0)),
        compiler_params=_cparams(("parallel",)),
        name="moe_router",
    )(x, wp)[:, :w.shape[1]]


def _dispatch_body(ir_ref, ic_ref, fl_ref, x_ref, rt_ref, o_ref):
    i = pl.program_id(0)
    flag = fl_ref[i]
    tok = ic_ref[i] * MOE_TC + lax.broadcasted_iota(jnp.int32, (MOE_BLK, MOE_TC), 1)
    onehot = jnp.where(rt_ref[...] == tok, 1.0, 0.0).astype(BF16)
    rows = _dot(onehot, x_ref[...])

    @pl.when(flag == 3)
    def _():
        o_ref[...] = rows.astype(o_ref.dtype)

    @pl.when(flag == 1)
    def _():
        o_ref[...] = (o_ref[...].astype(F32) + rows).astype(o_ref.dtype)


def _dispatch(x_bf, row_tok_col, item_r, item_c, flags):
    t, d = x_bf.shape
    r = row_tok_col.shape[0]
    ni = item_r.shape[0]
    gs = pltpu.PrefetchScalarGridSpec(
        num_scalar_prefetch=3, grid=(ni,),
        in_specs=[pl.BlockSpec((MOE_TC, d), lambda i, ir, ic, fl: (ic[i], 0)),
                  pl.BlockSpec((MOE_BLK, 1), lambda i, ir, ic, fl: (ir[i], 0))],
        out_specs=pl.BlockSpec((MOE_BLK, d), lambda i, ir, ic, fl: (ir[i], 0)))
    return pl.pallas_call(
        _dispatch_body, grid_spec=gs,
        out_shape=jax.ShapeDtypeStruct((r, d), BF16),
        compiler_params=_cparams(("arbitrary",)),
        name="moe_dispatch",
    )(item_r, item_c, flags, x_bf, row_tok_col)


def _expert_body(be_ref, x_ref, wg_ref, wu_ref, wd_ref, rw_ref, o_ref):
    f = pl.program_id(1)
    xb = x_ref[...]
    h = jax.nn.silu(_dot(xb, wg_ref[0])) * _dot(xb, wu_ref[0])
    part = _dot(h.astype(BF16), wd_ref[0])

    @pl.when(f == 0)
    def _():
        o_ref[...] = part

    @pl.when(f > 0)
    def _():
        o_ref[...] += part

    @pl.when(f == pl.num_programs(1) - 1)
    def _():
        o_ref[...] = o_ref[...] * rw_ref[...]


def _experts(xs, wg, wu, wd, row_w_col, block_expert):
    r, d = xs.shape
    nbk = r // MOE_BLK
    nf = wg.shape[2] // MOE_TF
    gs = pltpu.PrefetchScalarGridSpec(
        num_scalar_prefetch=1, grid=(nbk, nf),
        in_specs=[pl.BlockSpec((MOE_BLK, d), lambda i, f, be: (i, 0)),
                  pl.BlockSpec((1, d, MOE_TF), lambda i, f, be: (be[i], 0, f)),
                  pl.BlockSpec((1, d, MOE_TF), lambda i, f, be: (be[i], 0, f)),
                  pl.BlockSpec((1, MOE_TF, d), lambda i, f, be: (be[i], f, 0)),
                  pl.BlockSpec((MOE_BLK, 1), lambda i, f, be: (i, 0))],
        out_specs=pl.BlockSpec((MOE_BLK, d), lambda i, f, be: (i, 0)))
    return pl.pallas_call(
        _expert_body, grid_spec=gs,
        out_shape=jax.ShapeDtypeStruct((r, d), F32),
        compiler_params=_cparams(("parallel", "arbitrary")),
        name="moe_experts",
    )(block_expert, xs, wg, wu, wd, row_w_col)


def _combine_body(ir_ref, ic_ref, fl_ref, y_ref, rt_ref, o_ref):
    i = pl.program_id(0)
    flag = fl_ref[i]
    tok = ic_ref[i] * MOE_TC + lax.broadcasted_iota(jnp.int32, (MOE_TC, MOE_BLK), 0)
    onehot = jnp.where(rt_ref[0] == tok, 1.0, 0.0).astype(BF16)
    y = y_ref[...]
    y1 = y.astype(BF16)
    y2 = (y - y1.astype(F32)).astype(BF16)
    part = _dot(onehot, y1) + _dot(onehot, y2)

    @pl.when(flag == 3)
    def _():
        o_ref[...] = part

    @pl.when(flag == 1)
    def _():
        o_ref[...] += part


def _combine(out_rows, row_tok_lane, item_r, item_c, flags, t):
    r, d = out_rows.shape
    ni = item_r.shape[0]
    gs = pltpu.PrefetchScalarGridSpec(
        num_scalar_prefetch=3, grid=(ni,),
        in_specs=[pl.BlockSpec((MOE_BLK, d), lambda i, ir, ic, fl: (ir[i], 0)),
                  pl.BlockSpec((1, 1, MOE_BLK), lambda i, ir, ic, fl: (ir[i], 0, 0))],
        out_specs=pl.BlockSpec((MOE_TC, d), lambda i, ir, ic, fl: (ic[i], 0)))
    return pl.pallas_call(
        _combine_body, grid_spec=gs,
        out_shape=jax.ShapeDtypeStruct((t, d), F32),
        compiler_params=_cparams(("arbitrary",)),
        name="moe_combine",
    )(item_r, item_c, flags, out_rows, row_tok_lane)


def _moe_plan(top_idx, wts, t):
    e, blk, tc = N_EXPERTS, MOE_BLK, MOE_TC
    a = t * TOP_K
    i32 = jnp.int32
    exp_flat = top_idx.reshape(a).astype(i32)
    order = jnp.argsort(exp_flat, stable=True)
    exp_sorted = exp_flat[order]
    tok_sorted = (order // TOP_K).astype(i32)
    w_sorted = wts.reshape(a)[order]
    counts = jnp.bincount(exp_flat, length=e).astype(i32)
    padded = ((counts + blk - 1) // blk) * blk
    grp_start = jnp.cumsum(counts) - counts
    pad_end = jnp.cumsum(padded)
    pad_start = pad_end - padded
    dest = pad_start[exp_sorted] + (jnp.arange(a, dtype=i32) - grp_start[exp_sorted])
    nbk = a // blk + e
    r = nbk * blk
    row_tok = jnp.full((r,), -1, i32).at[dest].set(tok_sorted)
    row_w = jnp.zeros((r,), F32).at[dest].set(w_sorted)
    block_expert = jnp.minimum(jnp.searchsorted(pad_end, jnp.arange(nbk, dtype=i32) * blk, side="right"),
                               e - 1).astype(i32)

    rt = row_tok.reshape(nbk, blk)
    valid = rt >= 0
    t_lo = jnp.min(jnp.where(valid, rt, t), 1)
    t_hi = jnp.max(rt, 1)
    has = t_hi >= 0
    c_lo = jnp.where(has, t_lo // tc, 0)
    c_hi = jnp.where(has, t_hi // tc, 0)
    n_it = c_hi - c_lo + 1
    off_end = jnp.cumsum(n_it)
    off_start = off_end - n_it
    total = off_end[-1]
    ni = nbk + e * (t // tc)
    idx = jnp.arange(ni, dtype=i32)
    ok = idx < total
    ir = jnp.minimum(jnp.searchsorted(off_end, idx, side="right"), nbk - 1).astype(i32)
    ic = jnp.where(ok, c_lo[ir] + idx - off_start[ir], c_hi[nbk - 1]).astype(i32)
    first = ok & (idx == off_start[ir])
    d_flags = ok.astype(i32) + 2 * first.astype(i32)

    key = jnp.where(ok, ic * nbk + ir, jnp.iinfo(jnp.int32).max)
    perm = jnp.argsort(key)
    ok2 = ok[perm]
    last = total - 1
    cr = jnp.where(ok2, ir[perm], ir[perm][last]).astype(i32)
    cc = jnp.where(ok2, ic[perm], ic[perm][last]).astype(i32)
    first2 = ok2 & jnp.concatenate([jnp.ones((1,), bool), cc[1:] != cc[:-1]])
    c_flags = ok2.astype(i32) + 2 * first2.astype(i32)
    return dict(row_tok=row_tok, row_w=row_w, block_expert=block_expert,
                d_items=(ir, ic, d_flags), c_items=(cr, cc, c_flags), nbk=nbk)


def _moe(x2, w_router, wg, wu, wd):
    t, d = x2.shape
    logits = _router(x2, w_router)
    top_val, top_idx = lax.top_k(logits, TOP_K)
    wts = jax.nn.softmax(top_val, -1)
    plan = _moe_plan(top_idx, wts, t)
    nbk = plan["nbk"]
    xs = _dispatch(x2.astype(BF16), plan["row_tok"].reshape(-1, 1), *plan["d_items"])
    out_rows = _experts(xs, wg.astype(BF16), wu.astype(BF16), wd.astype(BF16),
                        plan["row_w"].reshape(-1, 1), plan["block_expert"])
    return _combine(out_rows, plan["row_tok"].reshape(nbk, 1, MOE_BLK), *plan["c_items"], t)


def _forward(x, mla_w_in, mla_q_norm, mla_w_q_up, mla_kv_norm, mla_w_kv_up, mla_w_out, nsa_w_in,
             nsa_cmp_pe_k, nsa_cmp_w1_k, nsa_cmp_w2_k, nsa_cmp_pe_v, nsa_cmp_w1_v, nsa_cmp_w2_v,
             nsa_w_out, rel_bias, ffn_w_gate, ffn_w_up, ffn_w_down, moe_w_router, moe_w_gate,
             moe_w_up, moe_w_down, ln_mix_g, ln_mix_b, ln_ffn_g, ln_ffn_b):
    b, s, d = x.shape
    x2 = x.reshape(b * s, d)
    o = _mla_mixer(x2, b, s, mla_w_in[0], mla_q_norm[0], mla_w_q_up[0], mla_kv_norm[0], mla_w_kv_up[0])
    x2 = _linear(o, mla_w_out[0].astype(BF16), tm=512, tn=d, out_dtype=F32,
                 ln=(x2, ln_mix_g[0], ln_mix_b[0]), name="mla_out_ln")
    x2 = _ffn_ln(x2, ffn_w_gate[0].astype(BF16), ffn_w_up[0].astype(BF16), ffn_w_down[0].astype(BF16),
                 ln_ffn_g[0], ln_ffn_b[0])
    o = _nsa_mixer(x2, b, s, nsa_w_in[0], nsa_cmp_pe_k[0], nsa_cmp_w1_k[0], nsa_cmp_w2_k[0],
                   nsa_cmp_pe_v[0], nsa_cmp_w1_v[0], nsa_cmp_w2_v[0], rel_bias)
    x2 = _linear(o, nsa_w_out[0].astype(BF16), tm=512, tn=d, out_dtype=F32,
                 ln=(x2, ln_mix_g[1], ln_mix_b[1]), name="nsa_out_ln")
    f = _moe(x2, moe_w_router[0], moe_w_gate[0], moe_w_up[0], moe_w_down[0])
    x2 = _res_ln(x2, f, ln_ffn_g[1], ln_ffn_b[1])
    return x2.reshape(b, s, d)


kernel = jax.jit(_forward)
```

```python
import functools
import math

import numpy as np
import jax
import jax.numpy as jnp
from jax import lax
from jax.experimental import pallas as pl
from jax.experimental.pallas import tpu as pltpu

F32 = jnp.float32
BF16 = jnp.bfloat16

D_MODEL = 1024
DEPTH = 2

MLA_HEADS = 8
MLA_Q_RANK = 512
MLA_KV_RANK = 256
MLA_NOPE = 128
MLA_ROPE = 64
MLA_V = 128
ROPE_THETA = 10000.0

NSA_HEADS = 16
NSA_GROUPS = 4
NSA_HG = NSA_HEADS // NSA_GROUPS
NSA_DK = 96
NSA_DV = 64
CMP_LEN = 32
CMP_STRIDE = 16
SEL_BLOCK = 64
SEL_TOPN = 16
WINDOW = 512
N_BRANCH = 3
FORCE = 1e6

REL_BUCKETS = 32
REL_MAX_DIST = 128

D_FF = 2816
N_EXPERTS = 8
TOP_K = 2
D_FF_EXPERT = 3584

LN_EPS = 1e-5
RMS_EPS = 1e-6

ALPHA = (2.0 * DEPTH) ** 0.25

NEG = -1e30
NEG_TEST = -1e29

V7X_VMEM_LIMIT = 56 * 1024 * 1024

LOG2E = 1.4426950408889634

NSA_TQ = 256
NSA_LANES = NSA_HG * NSA_TQ
NSA_KC = 256
NSA_BAND = 24
NSA_PIPE_DEPTH = 3

MOE_BLK = 512
MOE_TC = 256
MOE_TF = 512


def _cparams(sem, vmem=V7X_VMEM_LIMIT):
    return pltpu.CompilerParams(dimension_semantics=sem, vmem_limit_bytes=vmem)


def _layer_norm(r, g, b):
    mu = jnp.mean(r, -1, keepdims=True)
    d = r - mu
    var = jnp.mean(d * d, -1, keepdims=True)
    return d * lax.rsqrt(var + LN_EPS) * g + b


def _rms_norm(x, g):
    return x * lax.rsqrt(jnp.mean(x * x, -1, keepdims=True) + RMS_EPS) * g


def _split3(a):
    a1 = a.astype(BF16)
    r1 = a - a1.astype(F32)
    a2 = r1.astype(BF16)
    a3 = (r1 - a2.astype(F32)).astype(BF16)
    return a1, a2, a3


def _dot(a, b):
    return jnp.dot(a, b, preferred_element_type=F32)


def _dot_nt(a, b):
    return lax.dot_general(a, b, (((1,), (1,)), ((), ())), preferred_element_type=F32)


def _linear_body(*refs, has_norm, has_scale, has_ln, act):
    it = iter(refs)
    x_ref = next(it)
    w_ref = next(it)
    g_ref = next(it) if has_norm else None
    cs_ref = next(it) if has_scale else None
    if has_ln:
        res_ref, lg_ref, lb_ref = next(it), next(it), next(it)
    o_ref = next(it)
    x = x_ref[...]
    if has_norm:
        x = _rms_norm(x.astype(F32), g_ref[...])
    acc = _dot(x.astype(BF16), w_ref[...])
    if has_scale:
        acc = acc * cs_ref[...]
    if act == "sigmoid":
        acc = jax.nn.sigmoid(acc)
    if has_ln:
        acc = _layer_norm(ALPHA * res_ref[...] + acc, lg_ref[...], lb_ref[...])
    o_ref[...] = acc.astype(o_ref.dtype)


def _linear(x, w, *, tm, tn, out_dtype, xcol=0, norm_gain=None, colscale=None, ln=None, act=None, name):
    m = x.shape[0]
    k, n = w.shape
    assert m % tm == 0 and n % tn == 0
    in_specs = [pl.BlockSpec((tm, k), lambda i, j: (i, xcol)),
                pl.BlockSpec((k, tn), lambda i, j: (0, j))]
    args = [x, w]
    if norm_gain is not None:
        in_specs.append(pl.BlockSpec((1, k), lambda i, j: (0, 0)))
        args.append(norm_gain.reshape(1, k))
    if colscale is not None:
        in_specs.append(pl.BlockSpec((1, tn), lambda i, j: (0, j)))
        args.append(colscale.reshape(1, n))
    if ln is not None:
        assert tn == n
        res, lg, lb = ln
        in_specs += [pl.BlockSpec((tm, n), lambda i, j: (i, 0)),
                     pl.BlockSpec((1, n), lambda i, j: (0, 0)),
                     pl.BlockSpec((1, n), lambda i, j: (0, 0))]
        args += [res, lg.reshape(1, n), lb.reshape(1, n)]
    body = functools.partial(_linear_body, has_norm=norm_gain is not None,
                             has_scale=colscale is not None, has_ln=ln is not None, act=act)
    return pl.pallas_call(
        body,
        out_shape=jax.ShapeDtypeStruct((m, n), out_dtype),
        grid=(m // tm, n // tn),
        in_specs=in_specs,
        out_specs=pl.BlockSpec((tm, tn), lambda i, j: (i, j)),
        compiler_params=_cparams(("parallel", "arbitrary")),
        name=name,
    )(*args)


def _res_ln_body(x_ref, f_ref, g_ref, b_ref, o_ref):
    o_ref[...] = _layer_norm(ALPHA * x_ref[...] + f_ref[...], g_ref[...], b_ref[...])


def _res_ln(x, f, g, b, *, tm=512):
    m, n = x.shape
    row = pl.BlockSpec((tm, n), lambda i: (i, 0))
    vec = pl.BlockSpec((1, n), lambda i: (0, 0))
    return pl.pallas_call(
        _res_ln_body,
        out_shape=jax.ShapeDtypeStruct((m, n), F32),
        grid=(m // tm,),
        in_specs=[row, row, vec, vec],
        out_specs=row,
        compiler_params=_cparams(("parallel",)),
        name="res_ln",
    )(x, f, g.reshape(1, n), b.reshape(1, n))


def _rope_tables(s):
    half = MLA_ROPE // 2
    freq = ROPE_THETA ** (-jnp.arange(half, dtype=F32) / half)
    ang = jnp.arange(s).astype(F32)[:, None] * freq[None, :]
    cos, sin = jnp.cos(ang), jnp.sin(ang)
    return jnp.concatenate([cos, cos], -1), jnp.concatenate([-sin, sin], -1)


MLA_DQ = MLA_NOPE + MLA_ROPE
MLA_QROWS = MLA_NOPE + 2 * MLA_ROPE
MLA_T = 512
MLA_HPS = 2
MLA_SUB = 256
MLA_PIPE_DEPTH = 3


def _mla_q_body(lat_ref, g_ref, w_ref, cos_ref, sin_ref, o_ref, *, qscale):
    xn = _rms_norm(lat_ref[...], g_ref[...]).astype(BF16)
    y = _dot_nt(w_ref[...], xn)
    cos, sin = cos_ref[...], sin_ref[...]
    for h in range(MLA_HEADS):
        r0 = h * MLA_QROWS
        o_ref[0, h, :MLA_NOPE, :] = (y[r0:r0 + MLA_NOPE] * qscale).astype(BF16)
        a = y[r0 + MLA_NOPE:r0 + MLA_DQ]
        bb = y[r0 + MLA_DQ:r0 + MLA_QROWS]
        o_ref[0, h, MLA_NOPE:, :] = ((a * cos + bb * sin) * qscale).astype(BF16)


def _mla_q_proj(lat, gain, w_t, cos_t, sin_t, b, s, *, tm=MLA_T):
    ns = s // tm
    body = functools.partial(_mla_q_body, qscale=(MLA_DQ ** -0.5) * LOG2E)
    return pl.pallas_call(
        body,
        out_shape=jax.ShapeDtypeStruct((b, MLA_HEADS, MLA_DQ, s), BF16),
        grid=(b * ns,),
        in_specs=[pl.BlockSpec((tm, MLA_Q_RANK), lambda i: (i, 0)),
                  pl.BlockSpec((1, MLA_Q_RANK), lambda i: (0, 0)),
                  pl.BlockSpec(w_t.shape, lambda i: (0, 0)),
                  pl.BlockSpec((MLA_ROPE, tm), lambda i: (0, i % ns)),
                  pl.BlockSpec((MLA_ROPE, tm), lambda i: (0, i % ns))],
        out_specs=pl.BlockSpec((1, MLA_HEADS, MLA_DQ, tm), lambda i: (i // ns, 0, 0, i % ns)),
        compiler_params=_cparams(("parallel",)),
        name="mla_q_proj",
    )(lat, gain.reshape(1, -1), w_t, cos_t, sin_t)


def _mla_kv_body(lat_ref, g_ref, wk_ref, wvt_ref, kr_ref, cos_ref, sin_ref, k_ref, vt_ref):
    xn = _rms_norm(lat_ref[...], g_ref[...]).astype(BF16)
    kn = _dot(xn, wk_ref[...]).astype(BF16)
    vt = _dot_nt(wvt_ref[...], xn).astype(BF16)
    kr = kr_ref[...]
    rot = (kr[:, :MLA_ROPE] * cos_ref[...] + kr[:, MLA_ROPE:] * sin_ref[...]).astype(BF16)
    for h in range(MLA_HEADS):
        k_ref[0, h, :, :MLA_NOPE] = kn[:, h * MLA_NOPE:(h + 1) * MLA_NOPE]
        k_ref[0, h, :, MLA_NOPE:] = rot
        vt_ref[0, h, 0] = vt[h * MLA_V:(h + 1) * MLA_V]


def _mla_kv_proj(lat, gain, wk, wv_t, cosx, sinx, b, s, *, tm=MLA_T):
    ns = s // tm
    return pl.pallas_call(
        _mla_kv_body,
        out_shape=(jax.ShapeDtypeStruct((b, MLA_HEADS, s, MLA_DQ), BF16),
                   jax.ShapeDtypeStruct((b, MLA_HEADS, ns, MLA_V, tm), BF16)),
        grid=(b * ns,),
        in_specs=[pl.BlockSpec((tm, MLA_KV_RANK), lambda i: (i, MLA_Q_RANK // MLA_KV_RANK)),
                  pl.BlockSpec((1, MLA_KV_RANK), lambda i: (0, 0)),
                  pl.BlockSpec(wk.shape, lambda i: (0, 0)),
                  pl.BlockSpec(wv_t.shape, lambda i: (0, 0)),
                  pl.BlockSpec((tm, 2 * MLA_ROPE),
                               lambda i: (i, (MLA_Q_RANK + MLA_KV_RANK) // (2 * MLA_ROPE))),
                  pl.BlockSpec((tm, MLA_ROPE), lambda i: (i % ns, 0)),
                  pl.BlockSpec((tm, MLA_ROPE), lambda i: (i % ns, 0))],
        out_specs=(pl.BlockSpec((1, MLA_HEADS, tm, MLA_DQ), lambda i: (i // ns, 0, i % ns, 0)),
                   pl.BlockSpec((1, MLA_HEADS, 1, MLA_V, tm), lambda i: (i // ns, 0, i % ns, 0, 0))),
        compiler_params=_cparams(("parallel",)),
        name="mla_kv_proj",
    )(lat, gain.reshape(1, -1), wk, wv_t, lat, cosx, sinx)


def _mla_attn_body(qt_ref, k_ref, vt_ref, o_ref, *scr):
    i = pl.program_id(2)
    sub = MLA_SUB
    nsub = MLA_T // sub
    state = {(h, ql): scr[3 * (nsub * h + ql):3 * (nsub * h + ql) + 3]
             for h in range(MLA_HPS) for ql in range(nsub)}
    for m_scr, l_scr, acc_scr in state.values():
        m_scr[...] = jnp.full_like(m_scr, NEG)
        l_scr[...] = jnp.zeros_like(l_scr)
        acc_scr[...] = jnp.zeros_like(acc_scr)

    def scores(chain):
        j, h, kk, ql, diag = chain
        k = k_ref[0, h, pl.ds(pl.multiple_of(j * MLA_T + kk * sub, sub), sub), :]
        s = _dot(k, qt_ref[0, h, :, ql * sub:(ql + 1) * sub])
        if diag:
            key = lax.broadcasted_iota(jnp.int32, s.shape, 0)
            qry = lax.broadcasted_iota(jnp.int32, s.shape, 1)
            s = jnp.where(key <= qry, s, NEG)
        return s

    def update(chain, s):
        j, h, kk, ql, _ = chain
        m_scr, l_scr, acc_scr = state[(h, ql)]
        m_old = m_scr[...]
        m_new = jnp.maximum(m_old, jnp.max(s, 0, keepdims=True))
        a = jnp.exp2(m_old - m_new)
        p = jnp.exp2(s - m_new)
        l_scr[...] = a * l_scr[...] + jnp.sum(p, 0, keepdims=True)
        acc_scr[...] = a * acc_scr[...] + _dot(vt_ref[0, h, j, :, kk * sub:(kk + 1) * sub], p.astype(BF16))
        m_scr[...] = m_new

    def run_chains(chains):
        pending = []
        for chain in chains:
            pending.append((chain, scores(chain)))
            if len(pending) > MLA_PIPE_DEPTH:
                update(*pending.pop(0))
        for item in pending:
            update(*item)

    def far_body(j, carry):
        run_chains([(j, h, kk, ql, False) for kk in range(nsub) for h in range(MLA_HPS) for ql in range(nsub)])
        return carry

    lax.fori_loop(0, i, far_body, 0)
    run_chains([(i, h, kk, ql, kk == ql) for kk in range(nsub) for h in range(MLA_HPS)
                for ql in range(nsub) if kk <= ql])
    for (h, ql), (m_scr, l_scr, acc_scr) in state.items():
        o_t = acc_scr[...] * (1.0 / jnp.maximum(l_scr[...], 1e-30))
        o_ref[0, ql * sub:(ql + 1) * sub, h * MLA_V:(h + 1) * MLA_V] = o_t.T.astype(o_ref.dtype)


def _mla_attention(q_t, k, v_t, b, s):
    tq = MLA_T
    nq = s // tq
    hp = MLA_HPS
    sub_state = [pltpu.VMEM((1, MLA_SUB), F32), pltpu.VMEM((1, MLA_SUB), F32), pltpu.VMEM((MLA_V, MLA_SUB), F32)]
    return pl.pallas_call(
        _mla_attn_body,
        out_shape=jax.ShapeDtypeStruct((b, s, MLA_HEADS * MLA_V), BF16),
        grid=(b, MLA_HEADS // hp, nq),
        in_specs=[pl.BlockSpec((1, hp, MLA_DQ, tq), lambda bb, h, i: (bb, h, 0, i)),
                  pl.BlockSpec((1, hp, s, MLA_DQ), lambda bb, h, i: (bb, h, 0, 0)),
                  pl.BlockSpec((1, hp, nq, MLA_V, tq), lambda bb, h, i: (bb, h, 0, 0, 0))],
        out_specs=pl.BlockSpec((1, tq, hp * MLA_V), lambda bb, h, i: (bb, i, h)),
        scratch_shapes=hp * (tq // MLA_SUB) * sub_state,
        compiler_params=_cparams(("parallel", "parallel", "arbitrary")),
        name="mla_attention",
    )(q_t, k, v_t)


def _mla_mixer(x2, b, s, w_in, q_norm, w_q_up, kv_norm, w_kv_up):
    r0 = MLA_Q_RANK + MLA_KV_RANK
    half = MLA_ROPE // 2
    w_in_ext = jnp.concatenate([w_in, w_in[:, r0 + half:r0 + MLA_ROPE], w_in[:, r0:r0 + half]], 1)
    lat = _linear(x2, w_in_ext.astype(BF16), tm=512, tn=w_in_ext.shape[1], out_dtype=F32, name="mla_in")
    wq = w_q_up.reshape(MLA_Q_RANK, MLA_HEADS, MLA_DQ)
    wr = wq[..., MLA_NOPE:]
    wq = jnp.concatenate([wq, wr[..., half:], wr[..., :half]], -1)
    wq_t = wq.reshape(MLA_Q_RANK, MLA_HEADS * MLA_QROWS).T.astype(BF16)
    wkv = w_kv_up.reshape(MLA_KV_RANK, MLA_HEADS, MLA_NOPE + MLA_V)
    wk = wkv[..., :MLA_NOPE].reshape(MLA_KV_RANK, MLA_HEADS * MLA_NOPE).astype(BF16)
    wv_t = wkv[..., MLA_NOPE:].reshape(MLA_KV_RANK, MLA_HEADS * MLA_V).T.astype(BF16)
    cosx, sinx = _rope_tables(s)
    q_t = _mla_q_proj(lat, q_norm, wq_t, cosx.T, sinx.T, b, s)
    k, v_t = _mla_kv_proj(lat, kv_norm, wk, wv_t, cosx, sinx, b, s)
    o = _mla_attention(q_t, k, v_t, b, s)
    return o.reshape(b * s, MLA_HEADS * MLA_V)


def _ffn_body(x_ref, wg_ref, wu_ref, wd_ref, lg_ref, lb_ref, o_ref, acc_scr):
    f = pl.program_id(1)
    xb = x_ref[...].astype(BF16)
    h = jax.nn.silu(_dot(xb, wg_ref[...])) * _dot(xb, wu_ref[...])
    part = _dot(h.astype(BF16), wd_ref[...])

    @pl.when(f == 0)
    def _():
        acc_scr[...] = part

    @pl.when(f > 0)
    def _():
        acc_scr[...] += part

    @pl.when(f == pl.num_programs(1) - 1)
    def _():
        o_ref[...] = _layer_norm(ALPHA * x_ref[...] + acc_scr[...], lg_ref[...], lb_ref[...])


def _ffn_ln(x, wg, wu, wd, lg, lb, *, tm=512, tf=1408):
    m, d = x.shape
    dff = wg.shape[1]
    assert dff % tf == 0
    vec = pl.BlockSpec((1, d), lambda i, f: (0, 0))
    return pl.pallas_call(
        _ffn_body,
        out_shape=jax.ShapeDtypeStruct((m, d), F32),
        grid=(m // tm, dff // tf),
        in_specs=[pl.BlockSpec((tm, d), lambda i, f: (i, 0)),
                  pl.BlockSpec((d, tf), lambda i, f: (0, f)),
                  pl.BlockSpec((d, tf), lambda i, f: (0, f)),
                  pl.BlockSpec((tf, d), lambda i, f: (f, 0)),
                  vec, vec],
        out_specs=pl.BlockSpec((tm, d), lambda i, f: (i, 0)),
        scratch_shapes=[pltpu.VMEM((tm, d), F32)],
        compiler_params=_cparams(("parallel", "arbitrary")),
        name="ffn_ln",
    )(x, wg, wu, wd, lg.reshape(1, d), lb.reshape(1, d))


def _compress_body(a_ref, pe_ref, w1_ref, w2_ref, o_ref):
    a = (a_ref[...].astype(F32) + pe_ref[...]).astype(BF16)
    h = jax.nn.gelu(_dot(a, w1_ref[...]))
    o_ref[...] = _dot(h.astype(BF16), w2_ref[...]).astype(o_ref.dtype)


def _compress(a, pe, w1, w2, *, tm=512):
    m, k = a.shape
    d = w2.shape[1]
    return pl.pallas_call(
        _compress_body,
        out_shape=jax.ShapeDtypeStruct((m, d), BF16),
        grid=(m // tm,),
        in_specs=[pl.BlockSpec((tm, k), lambda i: (i, 0)),
                  pl.BlockSpec((1, k), lambda i: (0, 0)),
                  pl.BlockSpec((k, d), lambda i: (0, 0)),
                  pl.BlockSpec((d, d), lambda i: (0, 0))],
        out_specs=pl.BlockSpec((tm, d), lambda i: (i, 0)),
        compiler_params=_cparams(("parallel",)),
        name="nsa_compress",
    )(a, pe.reshape(1, k), w1.reshape(k, d).astype(BF16), w2.astype(BF16))


def _rel_bucket_np(dist):
    n = np.maximum(dist, 0)
    max_exact = REL_BUCKETS // 2
    nf = np.maximum(n, 1).astype(np.float32)
    large = max_exact + (np.log(nf / np.float32(max_exact)) / np.float32(math.log(REL_MAX_DIST / max_exact))
                         * np.float32(REL_BUCKETS - max_exact)).astype(np.int32)
    large = np.minimum(large, REL_BUCKETS - 1)
    return np.where(n < max_exact, n, large).astype(np.int32)


def _nsa_tables(rel_bias, s):
    g, hg, tq, kc = NSA_GROUPS, NSA_HG, NSA_TQ, NSA_KC
    assert tq == kc and tq % CMP_STRIDE == 0 and WINDOW == 2 * kc
    assert np.all(_rel_bucket_np(np.arange(tq // 2 - 15, s + tq)) == REL_BUCKETS - 1)
    rb = rel_bias.reshape(REL_BUCKETS, g, hg) * LOG2E

    def tile(base, step, rows, valid):
        p = tq + step * rows
        k = np.arange(p)
        k = np.where(k < p - step * (rows - 1), k, k - p)
        d = base + k
        vec = jnp.where(valid(d)[:, None, None], rb[_rel_bucket_np(d)], NEG)
        vec = vec.transpose(1, 2, 0)
        flat = jnp.tile(vec, (1, 1, rows))[..., :rows * (p - step)]
        mat = flat.reshape(g, hg, rows, p - step)[..., :tq]
        return mat.transpose(0, 2, 1, 3).reshape(g, rows, hg * tq)

    causal = lambda d: d >= 0
    tiles = jnp.stack([tile(0, 1, kc, causal), tile(tq, 1, kc, causal),
                       tile(2 * tq, 1, kc, lambda d: d < WINDOW)], 1)
    c31 = jnp.broadcast_to(rb[REL_BUCKETS - 1][:, None, :, None], (g, 1, hg, tq)).reshape(g, 1, hg * tq)
    band = jnp.stack([tile(8 * CMP_STRIDE - CMP_LEN + 1, CMP_STRIDE, NSA_BAND, causal),
                      tile(-(CMP_LEN - 1), CMP_STRIDE, NSA_BAND, causal)], 1)
    band = jnp.where(band > NEG_TEST, band - c31[:, None], NEG)
    return tiles, band, c31


def _overlap_t(nc_pad, nb):
    n = np.arange(nc_pad)[None, :]
    jb = np.arange(nb)[:, None]
    cstart = n * CMP_STRIDE
    cend = cstart + CMP_LEN - 1
    sstart = jb * SEL_BLOCK
    ov = (cstart <= sstart + SEL_BLOCK - 1) & (cend >= sstart) & (n < nc_pad - 1)
    return jnp.asarray(ov.astype(np.float32), BF16)


def _nsa_body(qt_ref, kc_ref, vct_ref, ks_ref, vst_ref, kw_ref, vwt_ref, gate_ref, tiles_ref,
              band_ref, c31_ref, ovt_ref, o_ref, s_scr, sel_scr, *state, nb):
    t = pl.program_id(2)
    st_a, st_b, st_w = ([state[3 * (NSA_HG * s_ + h):3 * (NSA_HG * s_ + h) + 3] for h in range(NSA_HG)]
                        for s_ in range(3))
    L = NSA_LANES
    q_t = qt_ref[0, 0, 0]
    c31 = c31_ref[0]
    ncp = kc_ref.shape[2]
    per = NSA_KC // SEL_BLOCK
    blocks_per_tile = NSA_TQ // CMP_STRIDE

    s = _dot(kc_ref[0, 0], q_t) + c31
    first = (t == 0).astype(jnp.int32)
    bs = pl.multiple_of((blocks_per_tile * t - 8) * (1 - first), 8)
    row = lax.broadcasted_iota(jnp.int32, (ncp, L), 0)
    s_scr[...] = jnp.where(row < bs + NSA_BAND, s, NEG)
    s_scr[pl.ds(bs, NSA_BAND), :] += band_ref[0, first]
    s = s_scr[...]
    m = jnp.max(s, 0, keepdims=True)
    m = jnp.where(m < NEG_TEST, 0.0, m)
    p = jnp.exp2(s - m)
    den = jnp.maximum(jnp.sum(p, 0, keepdims=True), 1e-30)
    p = p * (1.0 / den)
    o_c = _dot(vct_ref[0, 0], p.astype(BF16))

    psum = p[:, 0:NSA_TQ]
    for h in range(1, NSA_HG):
        psum = psum + p[:, h * NSA_TQ:(h + 1) * NSA_TQ]
    p1, p2, p3 = _split3(psum)
    ovt = ovt_ref[...]
    imp = _dot(ovt, p1) + _dot(ovt, p2) + _dot(ovt, p3)
    blk = lax.broadcasted_iota(jnp.int32, (nb, NSA_TQ), 0)
    lane = lax.broadcasted_iota(jnp.int32, (nb, NSA_TQ), 1)
    cur = (NSA_TQ // SEL_BLOCK) * t + lane // SEL_BLOCK
    forced = (blk == 0) | (blk == cur) | (blk == cur - 1)
    v = jnp.where(blk > cur, -FORCE, jnp.where(forced, FORCE, imp))
    blk_f = blk.astype(F32)
    sel = jnp.zeros((nb, NSA_TQ), F32)
    for _ in range(min(SEL_TOPN, nb)):
        mx = jnp.max(v, 0, keepdims=True)
        idx = jnp.min(jnp.where(v == mx, blk_f, float(nb)), 0, keepdims=True)
        hit = blk_f == idx
        sel = jnp.where(hit, 1.0, sel)
        v = jnp.where(hit, -jnp.inf, v)
    selneg = jnp.where(sel > 0.5, 0.0, NEG)
    selneg = jnp.concatenate([selneg] * NSA_HG, 1)
    for c in range(nb // per):
        sel_scr[c] = selneg[per * c:per * (c + 1), :]

    def scores(chain):
        kind, kc_idx, delta, h, _ = chain
        sl = slice(h * NSA_TQ, (h + 1) * NSA_TQ)
        k_ref = kw_ref if kind == "win" else ks_ref
        sc = _dot(k_ref[0, 0, kc_idx], qt_ref[0, 0, 0, :, sl])
        if kind == "win":
            return sc + tiles_ref[0, delta, :, sl]
        rows = sel_scr[kc_idx, :, sl]
        if delta is None:
            rows = rows + c31[:, sl]
        add = jnp.concatenate([jnp.broadcast_to(rows[r:r + 1], (SEL_BLOCK, NSA_TQ)) for r in range(per)], 0)
        if delta is not None:
            add = add + tiles_ref[0, delta, :, sl]
        return sc + add

    def update(chain, sc):
        kind, kc_idx, _, h, st = chain
        m_scr, l_scr, acc_scr = st[h]
        vt_ref = vwt_ref if kind == "win" else vst_ref
        m_old = m_scr[...]
        m_new = jnp.maximum(m_old, jnp.max(sc, 0, keepdims=True))
        a = jnp.exp2(m_old - m_new)
        pp = jnp.exp2(sc - m_new)
        l_scr[...] = a * l_scr[...] + jnp.sum(pp, 0, keepdims=True)
        acc_scr[...] = a * acc_scr[...] + _dot(vt_ref[0, 0, kc_idx], pp.astype(BF16))
        m_scr[...] = m_new

    def run_chains(chains, depth=NSA_PIPE_DEPTH):
        pending = []
        for chain in chains:
            pending.append((chain, scores(chain)))
            if len(pending) > depth:
                update(*pending.pop(0))
        for item in pending:
            update(*item)

    def sel_far(kc_idx, st):
        return [("sel", kc_idx, None, h, st) for h in range(NSA_HG)]

    def sel_near(delta, st):
        return [("sel", t - delta, delta, h, st) for h in range(NSA_HG)]

    def win(delta):
        return [("win", t - delta, delta, h, st_w) for h in range(NSA_HG)]

    for st in (st_a, st_b, st_w):
        for m_scr, l_scr, acc_scr in st:
            m_scr[...] = jnp.full_like(m_scr, NEG)
            l_scr[...] = jnp.zeros_like(l_scr)
            acc_scr[...] = jnp.zeros_like(acc_scr)

    n_far = jnp.maximum(t - 1, 0)

    def pair_body(i, carry):
        run_chains(sel_far(2 * i, st_a) + sel_far(2 * i + 1, st_b))
        return carry

    lax.fori_loop(0, n_far // 2, pair_body, 0)

    @pl.when(t >= 2)
    def _():
        @pl.when(n_far % 2 == 1)
        def _():
            run_chains(sel_far(n_far - 1, st_a))
        run_chains(sel_near(1, st_b) + sel_near(0, st_a) + win(0) + win(1) + win(2))

    @pl.when(t == 1)
    def _():
        run_chains(sel_near(1, st_b) + sel_near(0, st_a) + win(0) + win(1))

    @pl.when(t == 0)
    def _():
        run_chains(sel_near(0, st_a) + win(0))

    gate = gate_ref[0, 0, 0]
    for h in range(NSA_HG):
        sl = slice(h * NSA_TQ, (h + 1) * NSA_TQ)
        (ma, la, aa), (mb, lb, ab), (_, lw, aw) = st_a[h], st_b[h], st_w[h]
        m_s = jnp.maximum(ma[...], mb[...])
        w_a = jnp.exp2(ma[...] - m_s)
        w_b = jnp.exp2(mb[...] - m_s)
        l_s = la[...] * w_a + lb[...] * w_b
        o_s = (aa[...] * w_a + ab[...] * w_b) * (1.0 / jnp.maximum(l_s, 1e-30))
        o_w = aw[...] * (1.0 / jnp.maximum(lw[...], 1e-30))
        o = gate[0:1, sl] * o_c[:, sl] + gate[1:2, sl] * o_s + gate[2:3, sl] * o_w
        o_ref[0, 0, 0, :, sl] = o.astype(o_ref.dtype)


def _nsa_attention(q_t, kcmp, vcmp_t, ks, vs_t, kw, vw_t, gate, tiles, band, c31, ovt):
    b, g, nqt = q_t.shape[:3]
    ncp = kcmp.shape[2]
    nkc = ks.shape[2]
    nb = ovt.shape[0]
    L = NSA_LANES
    body = functools.partial(_nsa_body, nb=nb)
    head_state = [pltpu.VMEM((1, NSA_TQ), F32), pltpu.VMEM((1, NSA_TQ), F32), pltpu.VMEM((NSA_DV, NSA_TQ), F32)]
    grp = lambda bb, gg, t: (bb, gg, 0, 0)
    grp5 = lambda bb, gg, t: (bb, gg, 0, 0, 0)
    return pl.pallas_call(
        body,
        out_shape=jax.ShapeDtypeStruct((b, g, nqt, NSA_DV, L), BF16),
        grid=(b, g, nqt),
        in_specs=[pl.BlockSpec((1, 1, 1, NSA_DK, L), lambda bb, gg, t: (bb, gg, t, 0, 0)),
                  pl.BlockSpec((1, 1, ncp, NSA_DK), grp),
                  pl.BlockSpec((1, 1, NSA_DV, ncp), grp),
                  pl.BlockSpec((1, 1, nkc, NSA_KC, NSA_DK), grp5),
                  pl.BlockSpec((1, 1, nkc, NSA_DV, NSA_KC), grp5),
                  pl.BlockSpec((1, 1, nkc, NSA_KC, NSA_DK), grp5),
                  pl.BlockSpec((1, 1, nkc, NSA_DV, NSA_KC), grp5),
                  pl.BlockSpec((1, 1, 1, N_BRANCH, L), lambda bb, gg, t: (bb, gg, t, 0, 0)),
                  pl.BlockSpec((1, 3, NSA_KC, L), lambda bb, gg, t: (gg, 0, 0, 0)),
                  pl.BlockSpec((1, 2, NSA_BAND, L), lambda bb, gg, t: (gg, 0, 0, 0)),
                  pl.BlockSpec((1, 1, L), lambda bb, gg, t: (gg, 0, 0)),
                  pl.BlockSpec(ovt.shape, lambda bb, gg, t: (0, 0))],
        out_specs=pl.BlockSpec((1, 1, 1, NSA_DV, L), lambda bb, gg, t: (bb, gg, t, 0, 0)),
        scratch_shapes=[pltpu.VMEM((ncp, L), F32),
                        pltpu.VMEM((nb * SEL_BLOCK // NSA_KC, NSA_KC // SEL_BLOCK, L), F32)]
        + 3 * NSA_HG * head_state,
        compiler_params=_cparams(("parallel", "parallel", "arbitrary")),
        name="nsa_attention",
    )(q_t, kcmp, vcmp_t, ks, vs_t, kw, vw_t, gate, tiles, band, c31, ovt)


def _nsa_mixer(x2, b, s, w_in, pe_k, w1_k, w2_k, pe_v, w1_v, w2_v, rel_bias):
    t = b * s
    h, g, hg, dk, dv = NSA_HEADS, NSA_GROUPS, NSA_HG, NSA_DK, NSA_DV
    nqt = s // NSA_TQ
    nkc = s // NSA_KC
    nb = s // SEL_BLOCK
    sizes = [h * dk, g * dk, g * dv, g * dk, g * dv, g * dk, g * dv]
    cuts = [int(c) for c in np.cumsum(sizes)]
    n_main = cuts[-1]
    n_pad = -n_main % 128
    w_main = jnp.pad(w_in[:, :n_main], ((0, 0), (0, n_pad))).astype(BF16)
    colscale = jnp.where(jnp.arange(n_main + n_pad) < cuts[0], (dk ** -0.5) * LOG2E, 1.0).astype(F32)
    y = _linear(x2, w_main, tm=512, tn=(n_main + n_pad) // 3, out_dtype=BF16, colscale=colscale,
                name="nsa_in")
    w_gate = jnp.pad(w_in[:, n_main:], ((0, 0), (0, 128 - h * N_BRANCH))).astype(BF16)
    gate = _linear(x2, w_gate, tm=512, tn=128, out_dtype=F32, act="sigmoid", name="nsa_gate")

    q = y[:, :cuts[0]].reshape(b, nqt, NSA_TQ, g, hg, dk)
    q_t = q.transpose(0, 3, 1, 5, 4, 2).reshape(b, g, nqt, dk, hg * NSA_TQ)
    kc_tok = y[:, cuts[0]:cuts[1]].reshape(b, s, g, dk)
    vc_tok = y[:, cuts[1]:cuts[2]].reshape(b, s, g, dv)

    def keys(a):
        return a.reshape(b, nkc, NSA_KC, g, dk).transpose(0, 3, 1, 2, 4)

    def vals_t(a):
        return a.reshape(b, nkc, NSA_KC, g, dv).transpose(0, 3, 1, 4, 2)

    ks, vs_t = keys(y[:, cuts[2]:cuts[3]]), vals_t(y[:, cuts[3]:cuts[4]])
    kw, vw_t = keys(y[:, cuts[4]:cuts[5]]), vals_t(y[:, cuts[5]:cuts[6]])
    gate = gate[:, :h * N_BRANCH].reshape(b, nqt, NSA_TQ, g, hg, N_BRANCH)
    gate = gate.transpose(0, 3, 1, 5, 4, 2).reshape(b, g, nqt, N_BRANCH, hg * NSA_TQ)

    nch = s // CMP_STRIDE

    def unfold(tok, d):
        ch = tok.reshape(b, nch, CMP_STRIDE, g, d).transpose(0, 3, 1, 2, 4).reshape(b, g, nch, CMP_STRIDE * d)
        nxt = jnp.concatenate([ch[:, :, 1:], jnp.zeros_like(ch[:, :, :1])], 2)
        return jnp.concatenate([ch, nxt], -1).reshape(b * g * nch, CMP_LEN * d)

    k_cmp = _compress(unfold(kc_tok, dk), pe_k, w1_k, w2_k).reshape(b, g, nch, dk)
    v_cmp = _compress(unfold(vc_tok, dv), pe_v, w1_v, w2_v).reshape(b, g, nch, dv)
    vcmp_t = v_cmp.transpose(0, 1, 3, 2)

    tiles, band, c31 = _nsa_tables(rel_bias, s)
    ovt = _overlap_t(nch, nb)
    o_t = _nsa_attention(q_t, k_cmp, vcmp_t, ks, vs_t, kw, vw_t, gate, tiles, band, c31, ovt)
    o = o_t.reshape(b, g, nqt, dv, hg, NSA_TQ).transpose(0, 2, 5, 1, 4, 3)
    return o.reshape(t, h * dv)


def _router_body(x_ref, w_ref, o_ref):
    x1, x2, x3 = _split3(x_ref[...])
    w1, w2, w3 = _split3(w_ref[...])
    acc = _dot(x1, w1)
    acc += _dot(x1, w2) + _dot(x2, w1)
    acc += _dot(x1, w3) + _dot(x2, w2) + _dot(x3, w1)
    o_ref[...] = acc


def _router(x, w, *, tm=512):
    m, d = x.shape
    wp = jnp.pad(w, ((0, 0), (0, 128 - w.shape[1])))
    return pl.pallas_call(
        _router_body,
        out_shape=jax.ShapeDtypeStruct((m, 128), F32),
        grid=(m // tm,),
        in_specs=[pl.BlockSpec((tm, d), lambda i: (i, 0)), pl.BlockSpec((d, 128), lambda i: (0, 0))],
        out_specs=pl.BlockSpec((tm, 128), lambda i: (i, 0)),
        compiler_params=_cparams(("parallel",)),
        name="moe_router",
    )(x, wp)[:, :w.shape[1]]


def _dispatch_body(ir_ref, ic_ref, fl_ref, x_ref, rt_ref, o_ref):
    i = pl.program_id(0)
    flag = fl_ref[i]
    tok = ic_ref[i] * MOE_TC + lax.broadcasted_iota(jnp.int32, (MOE_BLK, MOE_TC), 1)
    onehot = jnp.where(rt_ref[...] == tok, 1.0, 0.0).astype(BF16)
    rows = _dot(onehot, x_ref[...])

    @pl.when(flag == 3)
    def _():
        o_ref[...] = rows.astype(o_ref.dtype)

    @pl.when(flag == 1)
    def _():
        o_ref[...] = (o_ref[...].astype(F32) + rows).astype(o_ref.dtype)


def _dispatch(x_bf, row_tok_col, item_r, item_c, flags):
    t, d = x_bf.shape
    r = row_tok_col.shape[0]
    ni = item_r.shape[0]
    gs = pltpu.PrefetchScalarGridSpec(
        num_scalar_prefetch=3, grid=(ni,),
        in_specs=[pl.BlockSpec((MOE_TC, d), lambda i, ir, ic, fl: (ic[i], 0)),
                  pl.BlockSpec((MOE_BLK, 1), lambda i, ir, ic, fl: (ir[i], 0))],
        out_specs=pl.BlockSpec((MOE_BLK, d), lambda i, ir, ic, fl: (ir[i], 0)))
    return pl.pallas_call(
        _dispatch_body, grid_spec=gs,
        out_shape=jax.ShapeDtypeStruct((r, d), BF16),
        compiler_params=_cparams(("arbitrary",)),
        name="moe_dispatch",
    )(item_r, item_c, flags, x_bf, row_tok_col)


def _expert_body(be_ref, x_ref, wg_ref, wu_ref, wd_ref, rw_ref, o_ref):
    f = pl.program_id(1)
    xb = x_ref[...]
    h = jax.nn.silu(_dot(xb, wg_ref[0])) * _dot(xb, wu_ref[0])
    part = _dot(h.astype(BF16), wd_ref[0])

    @pl.when(f == 0)
    def _():
        o_ref[...] = part

    @pl.when(f > 0)
    def _():
        o_ref[...] += part

    @pl.when(f == pl.num_programs(1) - 1)
    def _():
        o_ref[...] = o_ref[...] * rw_ref[...]


def _experts(xs, wg, wu, wd, row_w_col, block_expert):
    r, d = xs.shape
    nbk = r // MOE_BLK
    nf = wg.shape[2] // MOE_TF
    gs = pltpu.PrefetchScalarGridSpec(
        num_scalar_prefetch=1, grid=(nbk, nf),
        in_specs=[pl.BlockSpec((MOE_BLK, d), lambda i, f, be: (i, 0)),
                  pl.BlockSpec((1, d, MOE_TF), lambda i, f, be: (be[i], 0, f)),
                  pl.BlockSpec((1, d, MOE_TF), lambda i, f, be: (be[i], 0, f)),
                  pl.BlockSpec((1, MOE_TF, d), lambda i, f, be: (be[i], f, 0)),
                  pl.BlockSpec((MOE_BLK, 1), lambda i, f, be: (i, 0))],
        out_specs=pl.BlockSpec((MOE_BLK, d), lambda i, f, be: (i, 0)))
    return pl.pallas_call(
        _expert_body, grid_spec=gs,
        out_shape=jax.ShapeDtypeStruct((r, d), F32),
        compiler_params=_cparams(("parallel", "arbitrary")),
        name="moe_experts",
    )(block_expert, xs, wg, wu, wd, row_w_col)


def _combine_body(ir_ref, ic_ref, fl_ref, y_ref, rt_ref, o_ref):
    i = pl.program_id(0)
    flag = fl_ref[i]
    tok = ic_ref[i] * MOE_TC + lax.broadcasted_iota(jnp.int32, (MOE_TC, MOE_BLK), 0)
    onehot = jnp.where(rt_ref[0] == tok, 1.0, 0.0).astype(BF16)
    y = y_ref[...]
    y1 = y.astype(BF16)
    y2 = (y - y1.astype(F32)).astype(BF16)
    part = _dot(onehot, y1) + _dot(onehot, y2)

    @pl.when(flag == 3)
    def _():
        o_ref[...] = part

    @pl.when(flag == 1)
    def _():
        o_ref[...] += part


def _combine(out_rows, row_tok_lane, item_r, item_c, flags, t):
    r, d = out_rows.shape
    ni = item_r.shape[0]
    gs = pltpu.PrefetchScalarGridSpec(
        num_scalar_prefetch=3, grid=(ni,),
        in_specs=[pl.BlockSpec((MOE_BLK, d), lambda i, ir, ic, fl: (ir[i], 0)),
                  pl.BlockSpec((1, 1, MOE_BLK), lambda i, ir, ic, fl: (ir[i], 0, 0))],
        out_specs=pl.BlockSpec((MOE_TC, d), lambda i, ir, ic, fl: (ic[i], 0)))
    return pl.pallas_call(
        _combine_body, grid_spec=gs,
        out_shape=jax.ShapeDtypeStruct((t, d), F32),
        compiler_params=_cparams(("arbitrary",)),
        name="moe_combine",
    )(item_r, item_c, flags, out_rows, row_tok_lane)


def _moe_plan(top_idx, wts, t):
    e, blk, tc = N_EXPERTS, MOE_BLK, MOE_TC
    a = t * TOP_K
    i32 = jnp.int32
    exp_flat = top_idx.reshape(a).astype(i32)
    tok_flat = jnp.arange(a, dtype=i32) // TOP_K
    _, tok_sorted, w_sorted = lax.sort((exp_flat, tok_flat, wts.reshape(a)), num_keys=1, is_stable=True)
    counts = jnp.sum((exp_flat[:, None] == jnp.arange(e, dtype=i32)[None, :]).astype(i32), 0)
    padded = ((counts + blk - 1) // blk) * blk
    grp_start = jnp.cumsum(counts) - counts
    pad_end = jnp.cumsum(padded)
    pad_start = pad_end - padded
    nbk = a // blk + e
    r = nbk * blk
    tok_ext = jnp.concatenate([tok_sorted, jnp.full((r - a,), -1, i32)])
    w_ext = jnp.concatenate([w_sorted, jnp.zeros((r - a,), F32)])
    rows = jnp.arange(r, dtype=i32)
    row_tok = jnp.full((r,), -1, i32)
    row_w = jnp.zeros((r,), F32)
    for ee in range(e):
        inside = (rows >= pad_start[ee]) & (rows < pad_start[ee] + counts[ee])
        shift = pad_start[ee] - grp_start[ee]
        row_tok = jnp.where(inside, jnp.roll(tok_ext, shift), row_tok)
        row_w = jnp.where(inside, jnp.roll(w_ext, shift), row_w)
    blk_first = jnp.arange(nbk, dtype=i32) * blk
    block_expert = jnp.minimum(jnp.sum((pad_end[None, :] <= blk_first[:, None]).astype(i32), 1), e - 1)

    rt = row_tok.reshape(nbk, blk)
    valid = rt >= 0
    t_lo = jnp.min(jnp.where(valid, rt, t), 1)
    t_hi = jnp.max(rt, 1)
    has = t_hi >= 0
    c_lo = jnp.where(has, t_lo // tc, 0)
    c_hi = jnp.where(has, t_hi // tc, 0)
    n_it = c_hi - c_lo + 1
    off_end = jnp.cumsum(n_it)
    off_start = off_end - n_it
    total = off_end[-1]
    ni = nbk + e * (t // tc)
    idx = jnp.arange(ni, dtype=i32)
    ok = idx < total
    ir = jnp.minimum(jnp.sum((off_end[None, :] <= idx[:, None]).astype(i32), 1), nbk - 1)
    ic = jnp.where(ok, c_lo[ir] + idx - off_start[ir], c_hi[nbk - 1]).astype(i32)
    first = ok & (idx == off_start[ir])
    d_flags = ok.astype(i32) + 2 * first.astype(i32)

    key = jnp.where(ok, ic * nbk + ir, jnp.iinfo(jnp.int32).max)
    perm = jnp.argsort(key)
    ok2 = ok[perm]
    last = total - 1
    cr = jnp.where(ok2, ir[perm], ir[perm][last]).astype(i32)
    cc = jnp.where(ok2, ic[perm], ic[perm][last]).astype(i32)
    first2 = ok2 & jnp.concatenate([jnp.ones((1,), bool), cc[1:] != cc[:-1]])
    c_flags = ok2.astype(i32) + 2 * first2.astype(i32)
    return dict(row_tok=row_tok, row_w=row_w, block_expert=block_expert,
                d_items=(ir, ic, d_flags), c_items=(cr, cc, c_flags), nbk=nbk)


def _moe(x2, w_router, wg, wu, wd):
    t, d = x2.shape
    logits = _router(x2, w_router)
    top_val, top_idx = lax.top_k(logits, TOP_K)
    wts = jax.nn.softmax(top_val, -1)
    plan = _moe_plan(top_idx, wts, t)
    nbk = plan["nbk"]
    xs = _dispatch(x2.astype(BF16), plan["row_tok"].reshape(-1, 1), *plan["d_items"])
    out_rows = _experts(xs, wg.astype(BF16), wu.astype(BF16), wd.astype(BF16),
                        plan["row_w"].reshape(-1, 1), plan["block_expert"])
    return _combine(out_rows, plan["row_tok"].reshape(nbk, 1, MOE_BLK), *plan["c_items"], t)


def _forward(x, mla_w_in, mla_q_norm, mla_w_q_up, mla_kv_norm, mla_w_kv_up, mla_w_out, nsa_w_in,
             nsa_cmp_pe_k, nsa_cmp_w1_k, nsa_cmp_w2_k, nsa_cmp_pe_v, nsa_cmp_w1_v, nsa_cmp_w2_v,
             nsa_w_out, rel_bias, ffn_w_gate, ffn_w_up, ffn_w_down, moe_w_router, moe_w_gate,
             moe_w_up, moe_w_down, ln_mix_g, ln_mix_b, ln_ffn_g, ln_ffn_b):
    b, s, d = x.shape
    x2 = x.reshape(b * s, d)
    o = _mla_mixer(x2, b, s, mla_w_in[0], mla_q_norm[0], mla_w_q_up[0], mla_kv_norm[0], mla_w_kv_up[0])
    x2 = _linear(o, mla_w_out[0].astype(BF16), tm=512, tn=d, out_dtype=F32,
                 ln=(x2, ln_mix_g[0], ln_mix_b[0]), name="mla_out_ln")
    x2 = _ffn_ln(x2, ffn_w_gate[0].astype(BF16), ffn_w_up[0].astype(BF16), ffn_w_down[0].astype(BF16),
                 ln_ffn_g[0], ln_ffn_b[0])
    o = _nsa_mixer(x2, b, s, nsa_w_in[0], nsa_cmp_pe_k[0], nsa_cmp_w1_k[0], nsa_cmp_w2_k[0],
                   nsa_cmp_pe_v[0], nsa_cmp_w1_v[0], nsa_cmp_w2_v[0], rel_bias)
    x2 = _linear(o, nsa_w_out[0].astype(BF16), tm=512, tn=d, out_dtype=F32,
                 ln=(x2, ln_mix_g[1], ln_mix_b[1]), name="nsa_out_ln")
    f = _moe(x2, moe_w_router[0], moe_w_gate[0], moe_w_up[0], moe_w_down[0])
    x2 = _res_ln(x2, f, ln_ffn_g[1], ln_ffn_b[1])
    return x2.reshape(b, s, d)


kernel = jax.jit(_forward)
```

```python
import functools
import math

import numpy as np
import jax
import jax.numpy as jnp
from jax import lax
from jax.experimental import pallas as pl
from jax.experimental.pallas import tpu as pltpu

F32 = jnp.float32
BF16 = jnp.bfloat16

D_MODEL = 1024
DEPTH = 2

MLA_HEADS = 8
MLA_Q_RANK = 512
MLA_KV_RANK = 256
MLA_NOPE = 128
MLA_ROPE = 64
MLA_V = 128
ROPE_THETA = 10000.0

NSA_HEADS = 16
NSA_GROUPS = 4
NSA_HG = NSA_HEADS // NSA_GROUPS
NSA_DK = 96
NSA_DV = 64
CMP_LEN = 32
CMP_STRIDE = 16
SEL_BLOCK = 64
SEL_TOPN = 16
WINDOW = 512
N_BRANCH = 3
FORCE = 1e6

REL_BUCKETS = 32
REL_MAX_DIST = 128

D_FF = 2816
N_EXPERTS = 8
TOP_K = 2
D_FF_EXPERT = 3584

LN_EPS = 1e-5
RMS_EPS = 1e-6

ALPHA = (2.0 * DEPTH) ** 0.25

NEG = -1e30
NEG_TEST = -1e29

V7X_VMEM_LIMIT = 56 * 1024 * 1024

LOG2E = 1.4426950408889634

NSA_TQ = 256
NSA_LANES = NSA_HG * NSA_TQ
NSA_KC = 256
NSA_DKP = 128
NSA_GATE_ROWS = 8
NSA_BAND = 24
NSA_PIPE_DEPTH = 3

MOE_BLK = 512
MOE_TC = 512
MOE_TF = 512


def _cparams(sem, vmem=V7X_VMEM_LIMIT):
    return pltpu.CompilerParams(dimension_semantics=sem, vmem_limit_bytes=vmem)


def _layer_norm(r, g, b):
    mu = jnp.mean(r, -1, keepdims=True)
    d = r - mu
    var = jnp.mean(d * d, -1, keepdims=True)
    return d * lax.rsqrt(var + LN_EPS) * g + b


def _rms_norm(x, g):
    return x * lax.rsqrt(jnp.mean(x * x, -1, keepdims=True) + RMS_EPS) * g


def _split3(a):
    a1 = a.astype(BF16)
    r1 = a - a1.astype(F32)
    a2 = r1.astype(BF16)
    a3 = (r1 - a2.astype(F32)).astype(BF16)
    return a1, a2, a3


def _dot(a, b):
    return jnp.dot(a, b, preferred_element_type=F32)


def _dot_nt(a, b):
    return lax.dot_general(a, b, (((1,), (1,)), ((), ())), preferred_element_type=F32)


def _linear_body(*refs, has_norm, has_scale, has_ln, act):
    it = iter(refs)
    x_ref = next(it)
    w_ref = next(it)
    g_ref = next(it) if has_norm else None
    cs_ref = next(it) if has_scale else None
    if has_ln:
        res_ref, lg_ref, lb_ref = next(it), next(it), next(it)
    o_ref = next(it)
    x = x_ref[...]
    if has_norm:
        x = _rms_norm(x.astype(F32), g_ref[...])
    acc = _dot(x.astype(BF16), w_ref[...])
    if has_scale:
        acc = acc * cs_ref[...]
    if act == "sigmoid":
        acc = jax.nn.sigmoid(acc)
    if has_ln:
        acc = _layer_norm(ALPHA * res_ref[...] + acc, lg_ref[...], lb_ref[...])
    o_ref[...] = acc.astype(o_ref.dtype)


def _linear(x, w, *, tm, tn, out_dtype, xcol=0, norm_gain=None, colscale=None, ln=None, act=None, name):
    m = x.shape[0]
    k, n = w.shape
    assert m % tm == 0 and n % tn == 0
    in_specs = [pl.BlockSpec((tm, k), lambda i, j: (i, xcol)),
                pl.BlockSpec((k, tn), lambda i, j: (0, j))]
    args = [x, w]
    if norm_gain is not None:
        in_specs.append(pl.BlockSpec((1, k), lambda i, j: (0, 0)))
        args.append(norm_gain.reshape(1, k))
    if colscale is not None:
        in_specs.append(pl.BlockSpec((1, tn), lambda i, j: (0, j)))
        args.append(colscale.reshape(1, n))
    if ln is not None:
        assert tn == n
        res, lg, lb = ln
        in_specs += [pl.BlockSpec((tm, n), lambda i, j: (i, 0)),
                     pl.BlockSpec((1, n), lambda i, j: (0, 0)),
                     pl.BlockSpec((1, n), lambda i, j: (0, 0))]
        args += [res, lg.reshape(1, n), lb.reshape(1, n)]
    body = functools.partial(_linear_body, has_norm=norm_gain is not None,
                             has_scale=colscale is not None, has_ln=ln is not None, act=act)
    return pl.pallas_call(
        body,
        out_shape=jax.ShapeDtypeStruct((m, n), out_dtype),
        grid=(m // tm, n // tn),
        in_specs=in_specs,
        out_specs=pl.BlockSpec((tm, tn), lambda i, j: (i, j)),
        compiler_params=_cparams(("parallel", "arbitrary")),
        name=name,
    )(*args)


def _res_ln_body(x_ref, f_ref, g_ref, b_ref, o_ref):
    o_ref[...] = _layer_norm(ALPHA * x_ref[...] + f_ref[...], g_ref[...], b_ref[...])


def _res_ln(x, f, g, b, *, tm=512):
    m, n = x.shape
    row = pl.BlockSpec((tm, n), lambda i: (i, 0))
    vec = pl.BlockSpec((1, n), lambda i: (0, 0))
    return pl.pallas_call(
        _res_ln_body,
        out_shape=jax.ShapeDtypeStruct((m, n), F32),
        grid=(m // tm,),
        in_specs=[row, row, vec, vec],
        out_specs=row,
        compiler_params=_cparams(("parallel",)),
        name="res_ln",
    )(x, f, g.reshape(1, n), b.reshape(1, n))


def _rope_tables(s):
    half = MLA_ROPE // 2
    freq = ROPE_THETA ** (-jnp.arange(half, dtype=F32) / half)
    ang = jnp.arange(s).astype(F32)[:, None] * freq[None, :]
    cos, sin = jnp.cos(ang), jnp.sin(ang)
    return jnp.concatenate([cos, cos], -1), jnp.concatenate([-sin, sin], -1)


MLA_DQ = MLA_NOPE + MLA_ROPE
MLA_QROWS = MLA_NOPE + 2 * MLA_ROPE
MLA_T = 512
MLA_HPS = 2
MLA_SUB = 256
MLA_PIPE_DEPTH = 3


def _mla_q_body(lat_ref, g_ref, w_ref, cos_ref, sin_ref, o_ref, *, qscale):
    xn = _rms_norm(lat_ref[...], g_ref[...]).astype(BF16)
    y = _dot_nt(w_ref[...], xn)
    cos, sin = cos_ref[...], sin_ref[...]
    for h in range(MLA_HEADS):
        r0 = h * MLA_QROWS
        o_ref[0, h, :MLA_NOPE, :] = (y[r0:r0 + MLA_NOPE] * qscale).astype(BF16)
        a = y[r0 + MLA_NOPE:r0 + MLA_DQ]
        bb = y[r0 + MLA_DQ:r0 + MLA_QROWS]
        o_ref[0, h, MLA_NOPE:, :] = ((a * cos + bb * sin) * qscale).astype(BF16)


def _mla_q_proj(lat, gain, w_t, cos_t, sin_t, b, s, *, tm=MLA_T):
    ns = s // tm
    body = functools.partial(_mla_q_body, qscale=(MLA_DQ ** -0.5) * LOG2E)
    return pl.pallas_call(
        body,
        out_shape=jax.ShapeDtypeStruct((b, MLA_HEADS, MLA_DQ, s), BF16),
        grid=(b * ns,),
        in_specs=[pl.BlockSpec((tm, MLA_Q_RANK), lambda i: (i, 0)),
                  pl.BlockSpec((1, MLA_Q_RANK), lambda i: (0, 0)),
                  pl.BlockSpec(w_t.shape, lambda i: (0, 0)),
                  pl.BlockSpec((MLA_ROPE, tm), lambda i: (0, i % ns)),
                  pl.BlockSpec((MLA_ROPE, tm), lambda i: (0, i % ns))],
        out_specs=pl.BlockSpec((1, MLA_HEADS, MLA_DQ, tm), lambda i: (i // ns, 0, 0, i % ns)),
        compiler_params=_cparams(("parallel",)),
        name="mla_q_proj",
    )(lat, gain.reshape(1, -1), w_t, cos_t, sin_t)


def _mla_kv_body(lat_ref, g_ref, wk_ref, wvt_ref, kr_ref, cos_ref, sin_ref, k_ref, vt_ref):
    xn = _rms_norm(lat_ref[...], g_ref[...]).astype(BF16)
    kn = _dot(xn, wk_ref[...]).astype(BF16)
    vt = _dot_nt(wvt_ref[...], xn).astype(BF16)
    kr = kr_ref[...]
    rot = (kr[:, :MLA_ROPE] * cos_ref[...] + kr[:, MLA_ROPE:] * sin_ref[...]).astype(BF16)
    for h in range(MLA_HEADS):
        k_ref[0, h, :, :MLA_NOPE] = kn[:, h * MLA_NOPE:(h + 1) * MLA_NOPE]
        k_ref[0, h, :, MLA_NOPE:] = rot
        vt_ref[0, h, 0] = vt[h * MLA_V:(h + 1) * MLA_V]


def _mla_kv_proj(lat, gain, wk, wv_t, cosx, sinx, b, s, *, tm=MLA_T):
    ns = s // tm
    return pl.pallas_call(
        _mla_kv_body,
        out_shape=(jax.ShapeDtypeStruct((b, MLA_HEADS, s, MLA_DQ), BF16),
                   jax.ShapeDtypeStruct((b, MLA_HEADS, ns, MLA_V, tm), BF16)),
        grid=(b * ns,),
        in_specs=[pl.BlockSpec((tm, MLA_KV_RANK), lambda i: (i, MLA_Q_RANK // MLA_KV_RANK)),
                  pl.BlockSpec((1, MLA_KV_RANK), lambda i: (0, 0)),
                  pl.BlockSpec(wk.shape, lambda i: (0, 0)),
                  pl.BlockSpec(wv_t.shape, lambda i: (0, 0)),
                  pl.BlockSpec((tm, 2 * MLA_ROPE),
                               lambda i: (i, (MLA_Q_RANK + MLA_KV_RANK) // (2 * MLA_ROPE))),
                  pl.BlockSpec((tm, MLA_ROPE), lambda i: (i % ns, 0)),
                  pl.BlockSpec((tm, MLA_ROPE), lambda i: (i % ns, 0))],
        out_specs=(pl.BlockSpec((1, MLA_HEADS, tm, MLA_DQ), lambda i: (i // ns, 0, i % ns, 0)),
                   pl.BlockSpec((1, MLA_HEADS, 1, MLA_V, tm), lambda i: (i // ns, 0, i % ns, 0, 0))),
        compiler_params=_cparams(("parallel",)),
        name="mla_kv_proj",
    )(lat, gain.reshape(1, -1), wk, wv_t, lat, cosx, sinx)


def _mla_attn_body(qt_ref, k_ref, vt_ref, o_ref, *scr):
    i = pl.program_id(2)
    sub = MLA_SUB
    nsub = MLA_T // sub
    state = {(h, ql): scr[3 * (nsub * h + ql):3 * (nsub * h + ql) + 3]
             for h in range(MLA_HPS) for ql in range(nsub)}
    for m_scr, l_scr, acc_scr in state.values():
        m_scr[...] = jnp.full_like(m_scr, NEG)
        l_scr[...] = jnp.zeros_like(l_scr)
        acc_scr[...] = jnp.zeros_like(acc_scr)

    def scores(chain):
        j, h, kk, ql, diag = chain
        k = k_ref[0, h, pl.ds(pl.multiple_of(j * MLA_T + kk * sub, sub), sub), :]
        s = _dot(k, qt_ref[0, h, :, ql * sub:(ql + 1) * sub])
        if diag:
            key = lax.broadcasted_iota(jnp.int32, s.shape, 0)
            qry = lax.broadcasted_iota(jnp.int32, s.shape, 1)
            s = jnp.where(key <= qry, s, NEG)
        return s

    def update(chain, s):
        j, h, kk, ql, _ = chain
        m_scr, l_scr, acc_scr = state[(h, ql)]
        m_old = m_scr[...]
        m_new = jnp.maximum(m_old, jnp.max(s, 0, keepdims=True))
        a = jnp.exp2(m_old - m_new)
        p = jnp.exp2(s - m_new)
        l_scr[...] = a * l_scr[...] + jnp.sum(p, 0, keepdims=True)
        acc_scr[...] = a * acc_scr[...] + _dot(vt_ref[0, h, j, :, kk * sub:(kk + 1) * sub], p.astype(BF16))
        m_scr[...] = m_new

    def run_chains(chains):
        pending = []
        for chain in chains:
            pending.append((chain, scores(chain)))
            if len(pending) > MLA_PIPE_DEPTH:
                update(*pending.pop(0))
        for item in pending:
            update(*item)

    def far_body(j, carry):
        run_chains([(j, h, kk, ql, False) for kk in range(nsub) for h in range(MLA_HPS) for ql in range(nsub)])
        return carry

    lax.fori_loop(0, i, far_body, 0)
    run_chains([(i, h, kk, ql, kk == ql) for kk in range(nsub) for h in range(MLA_HPS)
                for ql in range(nsub) if kk <= ql])
    for (h, ql), (m_scr, l_scr, acc_scr) in state.items():
        o_t = acc_scr[...] * (1.0 / jnp.maximum(l_scr[...], 1e-30))
        o_ref[0, ql * sub:(ql + 1) * sub, h * MLA_V:(h + 1) * MLA_V] = o_t.T.astype(o_ref.dtype)


def _mla_attention(q_t, k, v_t, b, s):
    tq = MLA_T
    nq = s // tq
    hp = MLA_HPS
    sub_state = [pltpu.VMEM((1, MLA_SUB), F32), pltpu.VMEM((1, MLA_SUB), F32), pltpu.VMEM((MLA_V, MLA_SUB), F32)]
    return pl.pallas_call(
        _mla_attn_body,
        out_shape=jax.ShapeDtypeStruct((b, s, MLA_HEADS * MLA_V), BF16),
        grid=(b, MLA_HEADS // hp, nq),
        in_specs=[pl.BlockSpec((1, hp, MLA_DQ, tq), lambda bb, h, i: (bb, h, 0, i)),
                  pl.BlockSpec((1, hp, s, MLA_DQ), lambda bb, h, i: (bb, h, 0, 0)),
                  pl.BlockSpec((1, hp, nq, MLA_V, tq), lambda bb, h, i: (bb, h, 0, 0, 0))],
        out_specs=pl.BlockSpec((1, tq, hp * MLA_V), lambda bb, h, i: (bb, i, h)),
        scratch_shapes=hp * (tq // MLA_SUB) * sub_state,
        compiler_params=_cparams(("parallel", "parallel", "arbitrary")),
        name="mla_attention",
    )(q_t, k, v_t)


def _mla_mixer(x2, b, s, w_in, q_norm, w_q_up, kv_norm, w_kv_up):
    r0 = MLA_Q_RANK + MLA_KV_RANK
    half = MLA_ROPE // 2
    w_in_ext = jnp.concatenate([w_in, w_in[:, r0 + half:r0 + MLA_ROPE], w_in[:, r0:r0 + half]], 1)
    lat = _linear(x2, w_in_ext.astype(BF16), tm=512, tn=w_in_ext.shape[1], out_dtype=F32, name="mla_in")
    wq = w_q_up.reshape(MLA_Q_RANK, MLA_HEADS, MLA_DQ)
    wr = wq[..., MLA_NOPE:]
    wq = jnp.concatenate([wq, wr[..., half:], wr[..., :half]], -1)
    wq_t = wq.reshape(MLA_Q_RANK, MLA_HEADS * MLA_QROWS).T.astype(BF16)
    wkv = w_kv_up.reshape(MLA_KV_RANK, MLA_HEADS, MLA_NOPE + MLA_V)
    wk = wkv[..., :MLA_NOPE].reshape(MLA_KV_RANK, MLA_HEADS * MLA_NOPE).astype(BF16)
    wv_t = wkv[..., MLA_NOPE:].reshape(MLA_KV_RANK, MLA_HEADS * MLA_V).T.astype(BF16)
    cosx, sinx = _rope_tables(s)
    q_t = _mla_q_proj(lat, q_norm, wq_t, cosx.T, sinx.T, b, s)
    k, v_t = _mla_kv_proj(lat, kv_norm, wk, wv_t, cosx, sinx, b, s)
    o = _mla_attention(q_t, k, v_t, b, s)
    return o.reshape(b * s, MLA_HEADS * MLA_V)


def _ffn_body(x_ref, wg_ref, wu_ref, wd_ref, lg_ref, lb_ref, o_ref, acc_scr):
    f = pl.program_id(1)
    xb = x_ref[...].astype(BF16)
    h = jax.nn.silu(_dot(xb, wg_ref[...])) * _dot(xb, wu_ref[...])
    part = _dot(h.astype(BF16), wd_ref[...])

    @pl.when(f == 0)
    def _():
        acc_scr[...] = part

    @pl.when(f > 0)
    def _():
        acc_scr[...] += part

    @pl.when(f == pl.num_programs(1) - 1)
    def _():
        o_ref[...] = _layer_norm(ALPHA * x_ref[...] + acc_scr[...], lg_ref[...], lb_ref[...])


def _ffn_ln(x, wg, wu, wd, lg, lb, *, tm=512, tf=1408):
    m, d = x.shape
    dff = wg.shape[1]
    assert dff % tf == 0
    vec = pl.BlockSpec((1, d), lambda i, f: (0, 0))
    return pl.pallas_call(
        _ffn_body,
        out_shape=jax.ShapeDtypeStruct((m, d), F32),
        grid=(m // tm, dff // tf),
        in_specs=[pl.BlockSpec((tm, d), lambda i, f: (i, 0)),
                  pl.BlockSpec((d, tf), lambda i, f: (0, f)),
                  pl.BlockSpec((d, tf), lambda i, f: (0, f)),
                  pl.BlockSpec((tf, d), lambda i, f: (f, 0)),
                  vec, vec],
        out_specs=pl.BlockSpec((tm, d), lambda i, f: (i, 0)),
        scratch_shapes=[pltpu.VMEM((tm, d), F32)],
        compiler_params=_cparams(("parallel", "arbitrary")),
        name="ffn_ln",
    )(x, wg, wu, wd, lg.reshape(1, d), lb.reshape(1, d))


def _compress_body(a_ref, pe_ref, w1_ref, w2_ref, o_ref):
    a = (a_ref[...].astype(F32) + pe_ref[...]).astype(BF16)
    h = jax.nn.gelu(_dot(a, w1_ref[...]))
    o_ref[...] = _dot(h.astype(BF16), w2_ref[...]).astype(o_ref.dtype)


def _compress(a, pe, w1, w2, *, tm=512):
    m, k = a.shape
    dh, d = w2.shape
    return pl.pallas_call(
        _compress_body,
        out_shape=jax.ShapeDtypeStruct((m, d), BF16),
        grid=(m // tm,),
        in_specs=[pl.BlockSpec((tm, k), lambda i: (i, 0)),
                  pl.BlockSpec((1, k), lambda i: (0, 0)),
                  pl.BlockSpec((k, dh), lambda i: (0, 0)),
                  pl.BlockSpec((dh, d), lambda i: (0, 0))],
        out_specs=pl.BlockSpec((tm, d), lambda i: (i, 0)),
        compiler_params=_cparams(("parallel",)),
        name="nsa_compress",
    )(a, pe.reshape(1, k), w1.reshape(k, dh).astype(BF16), w2.astype(BF16))


def _rel_bucket_np(dist):
    n = np.maximum(dist, 0)
    max_exact = REL_BUCKETS // 2
    nf = np.maximum(n, 1).astype(np.float32)
    large = max_exact + (np.log(nf / np.float32(max_exact)) / np.float32(math.log(REL_MAX_DIST / max_exact))
                         * np.float32(REL_BUCKETS - max_exact)).astype(np.int32)
    large = np.minimum(large, REL_BUCKETS - 1)
    return np.where(n < max_exact, n, large).astype(np.int32)


def _nsa_tables(rel_bias, s):
    g, hg, tq, kc = NSA_GROUPS, NSA_HG, NSA_TQ, NSA_KC
    assert tq == kc and tq % CMP_STRIDE == 0 and WINDOW == 2 * kc
    assert np.all(_rel_bucket_np(np.arange(tq // 2 - 15, s + tq)) == REL_BUCKETS - 1)
    rb = rel_bias.reshape(REL_BUCKETS, g, hg) * LOG2E

    def tile(base, step, rows, valid):
        p = tq + step * rows
        k = np.arange(p)
        k = np.where(k < p - step * (rows - 1), k, k - p)
        d = base + k
        vec = jnp.where(valid(d)[:, None, None], rb[_rel_bucket_np(d)], NEG)
        vec = vec.transpose(1, 2, 0)
        flat = jnp.tile(vec, (1, 1, rows))[..., :rows * (p - step)]
        mat = flat.reshape(g, hg, rows, p - step)[..., :tq]
        return mat.transpose(0, 2, 1, 3).reshape(g, rows, hg * tq)

    causal = lambda d: d >= 0
    tiles = jnp.stack([tile(0, 1, kc, causal), tile(tq, 1, kc, causal),
                       tile(2 * tq, 1, kc, lambda d: d < WINDOW)], 1)
    c31 = jnp.broadcast_to(rb[REL_BUCKETS - 1][:, None, :, None], (g, 1, hg, tq)).reshape(g, 1, hg * tq)
    band = jnp.stack([tile(8 * CMP_STRIDE - CMP_LEN + 1, CMP_STRIDE, NSA_BAND, causal),
                      tile(-(CMP_LEN - 1), CMP_STRIDE, NSA_BAND, causal)], 1)
    band = jnp.where(band > NEG_TEST, band - c31[:, None], NEG)
    return tiles, band, c31


def _overlap_t(nc_pad, nb):
    n = np.arange(nc_pad)[None, :]
    jb = np.arange(nb)[:, None]
    cstart = n * CMP_STRIDE
    cend = cstart + CMP_LEN - 1
    sstart = jb * SEL_BLOCK
    ov = (cstart <= sstart + SEL_BLOCK - 1) & (cend >= sstart) & (n < nc_pad - 1)
    return jnp.asarray(ov.astype(np.float32), BF16)


def _nsa_body(qt_ref, kc_ref, vct_ref, ks_ref, vst_ref, kw_ref, vwt_ref, gate_ref, tiles_ref,
              band_ref, c31_ref, ovt_ref, o_ref, s_scr, sel_scr, *state, nb):
    t = pl.program_id(2)
    st_a, st_b, st_w = ([state[3 * (NSA_HG * s_ + h):3 * (NSA_HG * s_ + h) + 3] for h in range(NSA_HG)]
                        for s_ in range(3))
    L = NSA_LANES
    q_t = qt_ref[0, 0, 0]
    c31 = c31_ref[0]
    ncp = kc_ref.shape[2]
    per = NSA_KC // SEL_BLOCK
    blocks_per_tile = NSA_TQ // CMP_STRIDE

    s = _dot(kc_ref[0, 0], q_t) + c31
    first = (t == 0).astype(jnp.int32)
    bs = pl.multiple_of((blocks_per_tile * t - 8) * (1 - first), 8)
    row = lax.broadcasted_iota(jnp.int32, (ncp, L), 0)
    s_scr[...] = jnp.where(row < bs + NSA_BAND, s, NEG)
    s_scr[pl.ds(bs, NSA_BAND), :] += band_ref[0, first]
    s = s_scr[...]
    m = jnp.max(s, 0, keepdims=True)
    m = jnp.where(m < NEG_TEST, 0.0, m)
    p = jnp.exp2(s - m)
    den = jnp.maximum(jnp.sum(p, 0, keepdims=True), 1e-30)
    p = p * (1.0 / den)
    o_c = _dot(vct_ref[0, 0], p.astype(BF16))

    psum = p[:, 0:NSA_TQ]
    for h in range(1, NSA_HG):
        psum = psum + p[:, h * NSA_TQ:(h + 1) * NSA_TQ]
    p1, p2, p3 = _split3(psum)
    ovt = ovt_ref[...]
    imp = _dot(ovt, p1) + _dot(ovt, p2) + _dot(ovt, p3)
    blk = lax.broadcasted_iota(jnp.int32, (nb, NSA_TQ), 0)
    lane = lax.broadcasted_iota(jnp.int32, (nb, NSA_TQ), 1)
    cur = (NSA_TQ // SEL_BLOCK) * t + lane // SEL_BLOCK
    forced = (blk == 0) | (blk == cur) | (blk == cur - 1)
    v = jnp.where(blk > cur, -FORCE, jnp.where(forced, FORCE, imp))
    blk_f = blk.astype(F32)
    sel = jnp.zeros((nb, NSA_TQ), F32)
    for _ in range(min(SEL_TOPN, nb)):
        mx = jnp.max(v, 0, keepdims=True)
        idx = jnp.min(jnp.where(v == mx, blk_f, float(nb)), 0, keepdims=True)
        hit = blk_f == idx
        sel = jnp.where(hit, 1.0, sel)
        v = jnp.where(hit, -jnp.inf, v)
    selneg = jnp.where(sel > 0.5, 0.0, NEG)
    selneg = jnp.concatenate([selneg] * NSA_HG, 1)
    for c in range(nb // per):
        sel_scr[c] = selneg[per * c:per * (c + 1), :]

    def scores(chain):
        kind, kc_idx, delta, h, _ = chain
        sl = slice(h * NSA_TQ, (h + 1) * NSA_TQ)
        k_ref = kw_ref if kind == "win" else ks_ref
        sc = _dot(k_ref[0, 0, kc_idx], qt_ref[0, 0, 0, :, sl])
        if kind == "win":
            return sc + tiles_ref[0, delta, :, sl]
        rows = sel_scr[kc_idx, :, sl]
        if delta is None:
            rows = rows + c31[:, sl]
        add = jnp.concatenate([jnp.broadcast_to(rows[r:r + 1], (SEL_BLOCK, NSA_TQ)) for r in range(per)], 0)
        if delta is not None:
            add = add + tiles_ref[0, delta, :, sl]
        return sc + add

    def update(chain, sc):
        kind, kc_idx, _, h, st = chain
        m_scr, l_scr, acc_scr = st[h]
        vt_ref = vwt_ref if kind == "win" else vst_ref
        m_old = m_scr[...]
        m_new = jnp.maximum(m_old, jnp.max(sc, 0, keepdims=True))
        a = jnp.exp2(m_old - m_new)
        pp = jnp.exp2(sc - m_new)
        l_scr[...] = a * l_scr[...] + jnp.sum(pp, 0, keepdims=True)
        acc_scr[...] = a * acc_scr[...] + _dot(vt_ref[0, 0, kc_idx], pp.astype(BF16))
        m_scr[...] = m_new

    def run_chains(chains, depth=NSA_PIPE_DEPTH):
        pending = []
        for chain in chains:
            pending.append((chain, scores(chain)))
            if len(pending) > depth:
                update(*pending.pop(0))
        for item in pending:
            update(*item)

    def sel_far(kc_idx, st):
        return [("sel", kc_idx, None, h, st) for h in range(NSA_HG)]

    def sel_near(delta, st):
        return [("sel", t - delta, delta, h, st) for h in range(NSA_HG)]

    def win(delta):
        return [("win", t - delta, delta, h, st_w) for h in range(NSA_HG)]

    for st in (st_a, st_b, st_w):
        for m_scr, l_scr, acc_scr in st:
            m_scr[...] = jnp.full_like(m_scr, NEG)
            l_scr[...] = jnp.zeros_like(l_scr)
            acc_scr[...] = jnp.zeros_like(acc_scr)

    n_far = jnp.maximum(t - 1, 0)

    def pair_body(i, carry):
        run_chains(sel_far(2 * i, st_a) + sel_far(2 * i + 1, st_b))
        return carry

    lax.fori_loop(0, n_far // 2, pair_body, 0)

    @pl.when(t >= 2)
    def _():
        @pl.when(n_far % 2 == 1)
        def _():
            run_chains(sel_far(n_far - 1, st_a))
        run_chains(sel_near(1, st_b) + sel_near(0, st_a) + win(0) + win(1) + win(2))

    @pl.when(t == 1)
    def _():
        run_chains(sel_near(1, st_b) + sel_near(0, st_a) + win(0) + win(1))

    @pl.when(t == 0)
    def _():
        run_chains(sel_near(0, st_a) + win(0))

    gate = gate_ref[0, 0, 0]
    outs = []
    for h in range(NSA_HG):
        sl = slice(h * NSA_TQ, (h + 1) * NSA_TQ)
        (ma, la, aa), (mb, lb, ab), (_, lw, aw) = st_a[h], st_b[h], st_w[h]
        m_s = jnp.maximum(ma[...], mb[...])
        w_a = jnp.exp2(ma[...] - m_s)
        w_b = jnp.exp2(mb[...] - m_s)
        l_s = la[...] * w_a + lb[...] * w_b
        o_s = (aa[...] * w_a + ab[...] * w_b) * (1.0 / jnp.maximum(l_s, 1e-30))
        o_w = aw[...] * (1.0 / jnp.maximum(lw[...], 1e-30))
        outs.append(gate[0:1, sl] * o_c[:, sl] + gate[1:2, sl] * o_s + gate[2:3, sl] * o_w)
    o_ref[0] = jnp.concatenate(outs, 0).T.astype(o_ref.dtype)


def _nsa_attention(q_t, kcmp, vcmp_t, ks, vs_t, kw, vw_t, gate, tiles, band, c31, ovt):
    b, g, nqt = q_t.shape[:3]
    ncp = kcmp.shape[2]
    nkc = ks.shape[2]
    nb = ovt.shape[0]
    L = NSA_LANES
    body = functools.partial(_nsa_body, nb=nb)
    head_state = [pltpu.VMEM((1, NSA_TQ), F32), pltpu.VMEM((1, NSA_TQ), F32), pltpu.VMEM((NSA_DV, NSA_TQ), F32)]
    grp = lambda bb, gg, t: (bb, gg, 0, 0)
    grp5 = lambda bb, gg, t: (bb, gg, 0, 0, 0)
    return pl.pallas_call(
        body,
        out_shape=jax.ShapeDtypeStruct((b, nqt * NSA_TQ, g * NSA_HG * NSA_DV), BF16),
        grid=(b, g, nqt),
        in_specs=[pl.BlockSpec((1, 1, 1, NSA_DKP, L), lambda bb, gg, t: (bb, gg, t, 0, 0)),
                  pl.BlockSpec((1, 1, ncp, NSA_DKP), grp),
                  pl.BlockSpec((1, 1, NSA_DV, ncp), grp),
                  pl.BlockSpec((1, 1, nkc, NSA_KC, NSA_DKP), grp5),
                  pl.BlockSpec((1, 1, nkc, NSA_DV, NSA_KC), grp5),
                  pl.BlockSpec((1, 1, nkc, NSA_KC, NSA_DKP), grp5),
                  pl.BlockSpec((1, 1, nkc, NSA_DV, NSA_KC), grp5),
                  pl.BlockSpec((1, 1, 1, N_BRANCH, L), lambda bb, gg, t: (bb, gg, t, 0, 0)),
                  pl.BlockSpec((1, 3, NSA_KC, L), lambda bb, gg, t: (gg, 0, 0, 0)),
                  pl.BlockSpec((1, 2, NSA_BAND, L), lambda bb, gg, t: (gg, 0, 0, 0)),
                  pl.BlockSpec((1, 1, L), lambda bb, gg, t: (gg, 0, 0)),
                  pl.BlockSpec(ovt.shape, lambda bb, gg, t: (0, 0))],
        out_specs=pl.BlockSpec((1, NSA_TQ, NSA_HG * NSA_DV), lambda bb, gg, t: (bb, t, gg)),
        scratch_shapes=[pltpu.VMEM((ncp, L), F32),
                        pltpu.VMEM((nb * SEL_BLOCK // NSA_KC, NSA_KC // SEL_BLOCK, L), F32)]
        + 3 * NSA_HG * head_state,
        compiler_params=_cparams(("parallel", "parallel", "arbitrary")),
        name="nsa_attention",
    )(q_t, kcmp, vcmp_t, ks, vs_t, kw, vw_t, gate, tiles, band, c31, ovt)


def _nsa_proj_body(x_ref, wq_ref, wk_ref, wv_ref, wg_ref, wc_ref,
                   q_ref, ks_ref, kw_ref, vs_ref, vw_ref, gate_ref, c_ref, *, qscale):
    g, hg, tq = NSA_GROUPS, NSA_HG, NSA_TQ
    xb = x_ref[...].astype(BF16)
    q_t = _dot_nt(wq_ref[...], xb) * qscale
    gate_t = jax.nn.sigmoid(_dot_nt(wg_ref[...], xb))
    for gg in range(g):
        for h in range(hg):
            head = gg * hg + h
            q_ref[0, gg, 0, :, h * tq:(h + 1) * tq] = q_t[head * NSA_DKP:(head + 1) * NSA_DKP].astype(BF16)
            r0 = head * NSA_GATE_ROWS
            gate_ref[0, gg, 0, :, h * tq:(h + 1) * tq] = gate_t[r0:r0 + N_BRANCH]
    k = _dot(xb, wk_ref[...]).astype(BF16)
    v_t = _dot_nt(wv_ref[...], xb).astype(BF16)
    for gg in range(g):
        ks_ref[0, gg, 0] = k[:, gg * NSA_DKP:(gg + 1) * NSA_DKP]
        kw_ref[0, gg, 0] = k[:, (g + gg) * NSA_DKP:(g + gg + 1) * NSA_DKP]
        vs_ref[0, gg, 0] = v_t[gg * NSA_DV:(gg + 1) * NSA_DV]
        vw_ref[0, gg, 0] = v_t[(g + gg) * NSA_DV:(g + gg + 1) * NSA_DV]
    c_ref[...] = _dot(xb, wc_ref[...]).astype(BF16)


def _nsa_proj(x2, b, s, wq_t, wk, wv_t, wg_t, wc):
    g, hg, tq, L = NSA_GROUPS, NSA_HG, NSA_TQ, NSA_LANES
    nqt = s // tq
    t, d = x2.shape
    full = lambda i: (0, 0)
    tile5 = lambda i: (i // nqt, 0, i % nqt, 0, 0)
    body = functools.partial(_nsa_proj_body, qscale=(NSA_DK ** -0.5) * LOG2E)
    return pl.pallas_call(
        body,
        out_shape=(jax.ShapeDtypeStruct((b, g, nqt, NSA_DKP, L), BF16),
                   jax.ShapeDtypeStruct((b, g, nqt, NSA_KC, NSA_DKP), BF16),
                   jax.ShapeDtypeStruct((b, g, nqt, NSA_KC, NSA_DKP), BF16),
                   jax.ShapeDtypeStruct((b, g, nqt, NSA_DV, NSA_KC), BF16),
                   jax.ShapeDtypeStruct((b, g, nqt, NSA_DV, NSA_KC), BF16),
                   jax.ShapeDtypeStruct((b, g, nqt, N_BRANCH, L), F32),
                   jax.ShapeDtypeStruct((t, wc.shape[1]), BF16)),
        grid=(t // tq,),
        in_specs=[pl.BlockSpec((tq, d), lambda i: (i, 0)),
                  pl.BlockSpec(wq_t.shape, full), pl.BlockSpec(wk.shape, full), pl.BlockSpec(wv_t.shape, full),
                  pl.BlockSpec(wg_t.shape, full), pl.BlockSpec(wc.shape, full)],
        out_specs=(pl.BlockSpec((1, g, 1, NSA_DKP, L), tile5),
                   pl.BlockSpec((1, g, 1, NSA_KC, NSA_DKP), tile5),
                   pl.BlockSpec((1, g, 1, NSA_KC, NSA_DKP), tile5),
                   pl.BlockSpec((1, g, 1, NSA_DV, NSA_KC), tile5),
                   pl.BlockSpec((1, g, 1, NSA_DV, NSA_KC), tile5),
                   pl.BlockSpec((1, g, 1, N_BRANCH, L), tile5),
                   pl.BlockSpec((tq, wc.shape[1]), lambda i: (i, 0))),
        compiler_params=_cparams(("parallel",)),
        name="nsa_proj",
    )(x2, wq_t, wk, wv_t, wg_t, wc)


def _nsa_mixer(x2, b, s, w_in, pe_k, w1_k, w2_k, pe_v, w1_v, w2_v, rel_bias):
    assert NSA_TQ == NSA_KC
    t, d = x2.shape
    h, g, hg, dk, dv = NSA_HEADS, NSA_GROUPS, NSA_HG, NSA_DK, NSA_DV
    nb = s // SEL_BLOCK
    sizes = [h * dk, g * dk, g * dv, g * dk, g * dv, g * dk, g * dv, h * N_BRANCH]
    c = [0] + [int(v) for v in np.cumsum(sizes)]
    cols = [w_in[:, c[i]:c[i + 1]] for i in range(len(sizes))]
    w_q, w_kc, w_vc, w_ks, w_vs, w_kw, w_vw, w_gate = cols

    def pad_last(a, n):
        return jnp.pad(a, [(0, 0)] * (a.ndim - 1) + [(0, n - a.shape[-1])])

    wq_t = pad_last(w_q.reshape(d, h, dk), NSA_DKP).reshape(d, h * NSA_DKP).T.astype(BF16)
    wk = jnp.concatenate([pad_last(w.reshape(d, g, dk), NSA_DKP).reshape(d, g * NSA_DKP)
                          for w in (w_ks, w_kw)], 1).astype(BF16)
    wv_t = jnp.concatenate([w_vs, w_vw], 1).T.astype(BF16)
    wg_t = pad_last(w_gate.reshape(d, h, N_BRANCH), NSA_GATE_ROWS).reshape(d, h * NSA_GATE_ROWS).T.astype(BF16)
    wc = jnp.concatenate([w_kc, w_vc], 1).astype(BF16)
    q_t, ks, kw, vs_t, vw_t, gate, ctok = _nsa_proj(x2, b, s, wq_t, wk, wv_t, wg_t, wc)
    kc_tok = ctok[:, :g * dk].reshape(b, s, g, dk)
    vc_tok = ctok[:, g * dk:].reshape(b, s, g, dv)

    nch = s // CMP_STRIDE

    def unfold(tok, dd):
        ch = tok.reshape(b, nch, CMP_STRIDE, g, dd).transpose(0, 3, 1, 2, 4).reshape(b, g, nch, CMP_STRIDE * dd)
        nxt = jnp.concatenate([ch[:, :, 1:], jnp.zeros_like(ch[:, :, :1])], 2)
        return jnp.concatenate([ch, nxt], -1).reshape(b * g * nch, CMP_LEN * dd)

    k_cmp = _compress(unfold(kc_tok, dk), pe_k, w1_k, pad_last(w2_k, NSA_DKP)).reshape(b, g, nch, NSA_DKP)
    v_cmp = _compress(unfold(vc_tok, dv), pe_v, w1_v, w2_v).reshape(b, g, nch, dv)
    vcmp_t = v_cmp.transpose(0, 1, 3, 2)

    tiles, band, c31 = _nsa_tables(rel_bias, s)
    ovt = _overlap_t(nch, nb)
    o = _nsa_attention(q_t, k_cmp, vcmp_t, ks, vs_t, kw, vw_t, gate, tiles, band, c31, ovt)
    return o.reshape(t, h * dv)


def _router_body(x_ref, w_ref, o_ref):
    x1, x2, x3 = _split3(x_ref[...])
    w1, w2, w3 = _split3(w_ref[...])
    acc = _dot(x1, w1)
    acc += _dot(x1, w2) + _dot(x2, w1)
    acc += _dot(x1, w3) + _dot(x2, w2) + _dot(x3, w1)
    o_ref[...] = acc


def _router(x, w, *, tm=512):
    m, d = x.shape
    wp = jnp.pad(w, ((0, 0), (0, 128 - w.shape[1])))
    return pl.pallas_call(
        _router_body,
        out_shape=jax.ShapeDtypeStruct((m, 128), F32),
        grid=(m // tm,),
        in_specs=[pl.BlockSpec((tm, d), lambda i: (i, 0)), pl.BlockSpec((d, 128), lambda i: (0, 0))],
        out_specs=pl.BlockSpec((tm, 128), lambda i: (i, 0)),
        compiler_params=_cparams(("parallel",)),
        name="moe_router",
    )(x, wp)[:, :w.shape[1]]


def _dispatch_body(ir_ref, ic_ref, fl_ref, x_ref, rt_ref, o_ref):
    i = pl.program_id(0)
    flag = fl_ref[i]
    tok = ic_ref[i] * MOE_TC + lax.broadcasted_iota(jnp.int32, (MOE_BLK, MOE_TC), 1)
    onehot = jnp.where(rt_ref[...] == tok, 1.0, 0.0).astype(BF16)
    rows = _dot(onehot, x_ref[...])

    @pl.when(flag == 3)
    def _():
        o_ref[...] = rows.astype(o_ref.dtype)

    @pl.when(flag == 1)
    def _():
        o_ref[...] = (o_ref[...].astype(F32) + rows).astype(o_ref.dtype)


def _dispatch(x_bf, row_tok_col, item_r, item_c, flags):
    t, d = x_bf.shape
    r = row_tok_col.shape[0]
    ni = item_r.shape[0]
    gs = pltpu.PrefetchScalarGridSpec(
        num_scalar_prefetch=3, grid=(ni,),
        in_specs=[pl.BlockSpec((MOE_TC, d), lambda i, ir, ic, fl: (ic[i], 0)),
                  pl.BlockSpec((MOE_BLK, 1), lambda i, ir, ic, fl: (ir[i], 0))],
        out_specs=pl.BlockSpec((MOE_BLK, d), lambda i, ir, ic, fl: (ir[i], 0)))
    return pl.pallas_call(
        _dispatch_body, grid_spec=gs,
        out_shape=jax.ShapeDtypeStruct((r, d), BF16),
        compiler_params=_cparams(("arbitrary",)),
        name="moe_dispatch",
    )(item_r, item_c, flags, x_bf, row_tok_col)


def _expert_body(be_ref, x_ref, wg_ref, wu_ref, wd_ref, rw_ref, o_ref, acc_scr):
    f = pl.program_id(1)
    xb = x_ref[...]
    h = jax.nn.silu(_dot(xb, wg_ref[0])) * _dot(xb, wu_ref[0])
    part = _dot(h.astype(BF16), wd_ref[0])

    @pl.when(f == 0)
    def _():
        acc_scr[...] = part

    @pl.when(f > 0)
    def _():
        acc_scr[...] += part

    @pl.when(f == pl.num_programs(1) - 1)
    def _():
        o_ref[...] = (acc_scr[...] * rw_ref[...]).astype(o_ref.dtype)


def _experts(xs, wg, wu, wd, row_w_col, block_expert):
    r, d = xs.shape
    nbk = r // MOE_BLK
    nf = wg.shape[2] // MOE_TF
    gs = pltpu.PrefetchScalarGridSpec(
        num_scalar_prefetch=1, grid=(nbk, nf),
        in_specs=[pl.BlockSpec((MOE_BLK, d), lambda i, f, be: (i, 0)),
                  pl.BlockSpec((1, d, MOE_TF), lambda i, f, be: (be[i], 0, f)),
                  pl.BlockSpec((1, d, MOE_TF), lambda i, f, be: (be[i], 0, f)),
                  pl.BlockSpec((1, MOE_TF, d), lambda i, f, be: (be[i], f, 0)),
                  pl.BlockSpec((MOE_BLK, 1), lambda i, f, be: (i, 0))],
        out_specs=pl.BlockSpec((MOE_BLK, d), lambda i, f, be: (i, 0)),
        scratch_shapes=[pltpu.VMEM((MOE_BLK, d), F32)])
    return pl.pallas_call(
        _expert_body, grid_spec=gs,
        out_shape=jax.ShapeDtypeStruct((r, d), BF16),
        compiler_params=_cparams(("parallel", "arbitrary")),
        name="moe_experts",
    )(block_expert, xs, wg, wu, wd, row_w_col)


def _combine_body(ir_ref, ic_ref, fl_ref, y_ref, rt_ref, x_ref, g_ref, b_ref, o_ref):
    i = pl.program_id(0)
    flag = fl_ref[i]
    tok = ic_ref[i] * MOE_TC + lax.broadcasted_iota(jnp.int32, (MOE_TC, MOE_BLK), 0)
    onehot = jnp.where(rt_ref[0] == tok, 1.0, 0.0).astype(BF16)
    part = _dot(onehot, y_ref[...])

    @pl.when((flag & 3) == 3)
    def _():
        o_ref[...] = part

    @pl.when((flag & 3) == 1)
    def _():
        o_ref[...] += part

    @pl.when((flag & 4) == 4)
    def _():
        o_ref[...] = _layer_norm(ALPHA * x_ref[...] + o_ref[...], g_ref[...], b_ref[...])


def _combine_ln(out_rows, row_tok_lane, item_r, item_c, flags, x, g, b):
    r, d = out_rows.shape
    t = x.shape[0]
    ni = item_r.shape[0]
    vec = pl.BlockSpec((1, d), lambda i, ir, ic, fl: (0, 0))
    gs = pltpu.PrefetchScalarGridSpec(
        num_scalar_prefetch=3, grid=(ni,),
        in_specs=[pl.BlockSpec((MOE_BLK, d), lambda i, ir, ic, fl: (ir[i], 0)),
                  pl.BlockSpec((1, 1, MOE_BLK), lambda i, ir, ic, fl: (ir[i], 0, 0)),
                  pl.BlockSpec((MOE_TC, d), lambda i, ir, ic, fl: (ic[i], 0)),
                  vec, vec],
        out_specs=pl.BlockSpec((MOE_TC, d), lambda i, ir, ic, fl: (ic[i], 0)))
    return pl.pallas_call(
        _combine_body, grid_spec=gs,
        out_shape=jax.ShapeDtypeStruct((t, d), F32),
        compiler_params=_cparams(("arbitrary",)),
        name="moe_combine_ln",
    )(item_r, item_c, flags, out_rows, row_tok_lane, x, g.reshape(1, d), b.reshape(1, d))


def _moe_plan(top_idx, wts, t):
    e, blk, tc = N_EXPERTS, MOE_BLK, MOE_TC
    a = t * TOP_K
    i32 = jnp.int32
    exp_flat = top_idx.reshape(a).astype(i32)
    tok_flat = jnp.arange(a, dtype=i32) // TOP_K
    _, tok_sorted, w_sorted = lax.sort((exp_flat, tok_flat, wts.reshape(a)), num_keys=1, is_stable=True)
    counts = jnp.sum((exp_flat[:, None] == jnp.arange(e, dtype=i32)[None, :]).astype(i32), 0)
    padded = ((counts + blk - 1) // blk) * blk
    grp_start = jnp.cumsum(counts) - counts
    pad_end = jnp.cumsum(padded)
    pad_start = pad_end - padded
    nbk = a // blk + e
    r = nbk * blk
    tok_ext = jnp.concatenate([tok_sorted, jnp.full((r - a,), -1, i32)])
    w_ext = jnp.concatenate([w_sorted, jnp.zeros((r - a,), F32)])
    rows = jnp.arange(r, dtype=i32)
    row_tok = jnp.full((r,), -1, i32)
    row_w = jnp.zeros((r,), F32)
    for ee in range(e):
        inside = (rows >= pad_start[ee]) & (rows < pad_start[ee] + counts[ee])
        shift = pad_start[ee] - grp_start[ee]
        row_tok = jnp.where(inside, jnp.roll(tok_ext, shift), row_tok)
        row_w = jnp.where(inside, jnp.roll(w_ext, shift), row_w)
    blk_first = jnp.arange(nbk, dtype=i32) * blk
    block_expert = jnp.minimum(jnp.sum((pad_end[None, :] <= blk_first[:, None]).astype(i32), 1), e - 1)

    rt = row_tok.reshape(nbk, blk)
    valid = rt >= 0
    t_lo = jnp.min(jnp.where(valid, rt, t), 1)
    t_hi = jnp.max(rt, 1)
    has = t_hi >= 0
    c_lo = jnp.where(has, t_lo // tc, 0)
    c_hi = jnp.where(has, t_hi // tc, 0)
    n_it = c_hi - c_lo + 1
    off_end = jnp.cumsum(n_it)
    off_start = off_end - n_it
    total = off_end[-1]
    ni = nbk + e * (t // tc)
    idx = jnp.arange(ni, dtype=i32)
    ok = idx < total
    ir = jnp.minimum(jnp.sum((off_end[None, :] <= idx[:, None]).astype(i32), 1), nbk - 1)
    ic = jnp.where(ok, c_lo[ir] + idx - off_start[ir], c_hi[nbk - 1]).astype(i32)
    first = ok & (idx == off_start[ir])
    d_flags = ok.astype(i32) + 2 * first.astype(i32)

    key = jnp.where(ok, ic * nbk + ir, jnp.iinfo(jnp.int32).max)
    perm = jnp.argsort(key)
    ok2 = ok[perm]
    last = total - 1
    cr = jnp.where(ok2, ir[perm], ir[perm][last]).astype(i32)
    cc = jnp.where(ok2, ic[perm], ic[perm][last]).astype(i32)
    first2 = ok2 & jnp.concatenate([jnp.ones((1,), bool), cc[1:] != cc[:-1]])
    last2 = ok2 & jnp.concatenate([(cc[1:] != cc[:-1]) | ~ok2[1:], jnp.ones((1,), bool)])
    c_flags = ok2.astype(i32) + 2 * first2.astype(i32) + 4 * last2.astype(i32)
    return dict(row_tok=row_tok, row_w=row_w, block_expert=block_expert,
                d_items=(ir, ic, d_flags), c_items=(cr, cc, c_flags), nbk=nbk)


def _moe_ln(x2, w_router, wg, wu, wd, ln_g, ln_b):
    t, d = x2.shape
    logits = _router(x2, w_router)
    top_val, top_idx = lax.top_k(logits, TOP_K)
    wts = jax.nn.softmax(top_val, -1)
    plan = _moe_plan(top_idx, wts, t)
    nbk = plan["nbk"]
    xs = _dispatch(x2.astype(BF16), plan["row_tok"].reshape(-1, 1), *plan["d_items"])
    out_rows = _experts(xs, wg.astype(BF16), wu.astype(BF16), wd.astype(BF16),
                        plan["row_w"].reshape(-1, 1), plan["block_expert"])
    return _combine_ln(out_rows, plan["row_tok"].reshape(nbk, 1, MOE_BLK), *plan["c_items"], x2, ln_g, ln_b)


def _forward(x, mla_w_in, mla_q_norm, mla_w_q_up, mla_kv_norm, mla_w_kv_up, mla_w_out, nsa_w_in,
             nsa_cmp_pe_k, nsa_cmp_w1_k, nsa_cmp_w2_k, nsa_cmp_pe_v, nsa_cmp_w1_v, nsa_cmp_w2_v,
             nsa_w_out, rel_bias, ffn_w_gate, ffn_w_up, ffn_w_down, moe_w_router, moe_w_gate,
             moe_w_up, moe_w_down, ln_mix_g, ln_mix_b, ln_ffn_g, ln_ffn_b):
    b, s, d = x.shape
    x2 = x.reshape(b * s, d)
    o = _mla_mixer(x2, b, s, mla_w_in[0], mla_q_norm[0], mla_w_q_up[0], mla_kv_norm[0], mla_w_kv_up[0])
    x2 = _linear(o, mla_w_out[0].astype(BF16), tm=512, tn=d, out_dtype=F32,
                 ln=(x2, ln_mix_g[0], ln_mix_b[0]), name="mla_out_ln")
    x2 = _ffn_ln(x2, ffn_w_gate[0].astype(BF16), ffn_w_up[0].astype(BF16), ffn_w_down[0].astype(BF16),
                 ln_ffn_g[0], ln_ffn_b[0])
    o = _nsa_mixer(x2, b, s, nsa_w_in[0], nsa_cmp_pe_k[0], nsa_cmp_w1_k[0], nsa_cmp_w2_k[0],
                   nsa_cmp_pe_v[0], nsa_cmp_w1_v[0], nsa_cmp_w2_v[0], rel_bias)
    x2 = _linear(o, nsa_w_out[0].astype(BF16), tm=512, tn=d, out_dtype=F32,
                 ln=(x2, ln_mix_g[1], ln_mix_b[1]), name="nsa_out_ln")
    x2 = _moe_ln(x2, moe_w_router[0], moe_w_gate[0], moe_w_up[0], moe_w_down[0], ln_ffn_g[1], ln_ffn_b[1])
    return x2.reshape(b, s, d)


@jax.jit
def kernel(x, mla_w_in, mla_q_norm, mla_w_q_up, mla_kv_norm, mla_w_kv_up, mla_w_out, nsa_w_in,
           nsa_cmp_pe_k, nsa_cmp_w1_k, nsa_cmp_w2_k, nsa_cmp_pe_v, nsa_cmp_w1_v, nsa_cmp_w2_v,
           nsa_w_out, rel_bias, ffn_w_gate, ffn_w_up, ffn_w_down, moe_w_router, moe_w_gate,
           moe_w_up, moe_w_down, ln_mix_g, ln_mix_b, ln_ffn_g, ln_ffn_b):
    return _forward(x, mla_w_in, mla_q_norm, mla_w_q_up, mla_kv_norm, mla_w_kv_up, mla_w_out, nsa_w_in,
                    nsa_cmp_pe_k, nsa_cmp_w1_k, nsa_cmp_w2_k, nsa_cmp_pe_v, nsa_cmp_w1_v, nsa_cmp_w2_v,
                    nsa_w_out, rel_bias, ffn_w_gate, ffn_w_up, ffn_w_down, moe_w_router, moe_w_gate,
                    moe_w_up, moe_w_down, ln_mix_g, ln_mix_b, ln_ffn_g, ln_ffn_b)
```

```python
import functools
import math

import numpy as np
import jax
import jax.numpy as jnp
from jax import lax
from jax.experimental import pallas as pl
from jax.experimental.pallas import tpu as pltpu

F32 = jnp.float32
BF16 = jnp.bfloat16

D_MODEL = 1024
DEPTH = 2

MLA_HEADS = 8
MLA_Q_RANK = 512
MLA_KV_RANK = 256
MLA_NOPE = 128
MLA_ROPE = 64
MLA_V = 128
ROPE_THETA = 10000.0

NSA_HEADS = 16
NSA_GROUPS = 4
NSA_HG = NSA_HEADS // NSA_GROUPS
NSA_DK = 96
NSA_DV = 64
CMP_LEN = 32
CMP_STRIDE = 16
SEL_BLOCK = 64
SEL_TOPN = 16
WINDOW = 512
N_BRANCH = 3
FORCE = 1e6

REL_BUCKETS = 32
REL_MAX_DIST = 128

D_FF = 2816
N_EXPERTS = 8
TOP_K = 2
D_FF_EXPERT = 3584

LN_EPS = 1e-5
RMS_EPS = 1e-6

ALPHA = (2.0 * DEPTH) ** 0.25

NEG = -1e30
NEG_TEST = -1e29

V7X_VMEM_LIMIT = 56 * 1024 * 1024

LOG2E = 1.4426950408889634

NSA_TQ = 256
NSA_LANES = NSA_HG * NSA_TQ
NSA_KC = 256
NSA_DKP = 128
NSA_GATE_ROWS = 8
NSA_BAND = 24
NSA_PIPE_DEPTH = 3

MOE_BLK = 512
MOE_TC = 512
MOE_TF = 512


def _cparams(sem, vmem=V7X_VMEM_LIMIT):
    return pltpu.CompilerParams(dimension_semantics=sem, vmem_limit_bytes=vmem)


def _layer_norm(r, g, b):
    mu = jnp.mean(r, -1, keepdims=True)
    d = r - mu
    var = jnp.mean(d * d, -1, keepdims=True)
    return d * lax.rsqrt(var + LN_EPS) * g + b


def _rms_norm(x, g):
    return x * lax.rsqrt(jnp.mean(x * x, -1, keepdims=True) + RMS_EPS) * g


def _split3(a):
    a1 = a.astype(BF16)
    r1 = a - a1.astype(F32)
    a2 = r1.astype(BF16)
    a3 = (r1 - a2.astype(F32)).astype(BF16)
    return a1, a2, a3


def _dot(a, b):
    return jnp.dot(a, b, preferred_element_type=F32)


def _dot_nt(a, b):
    return lax.dot_general(a, b, (((1,), (1,)), ((), ())), preferred_element_type=F32)


def _linear_body(*refs, has_norm, has_scale, has_ln, act):
    it = iter(refs)
    x_ref = next(it)
    w_ref = next(it)
    g_ref = next(it) if has_norm else None
    cs_ref = next(it) if has_scale else None
    if has_ln:
        res_ref, lg_ref, lb_ref = next(it), next(it), next(it)
    o_ref = next(it)
    x = x_ref[...]
    if has_norm:
        x = _rms_norm(x.astype(F32), g_ref[...])
    acc = _dot(x.astype(BF16), w_ref[...])
    if has_scale:
        acc = acc * cs_ref[...]
    if act == "sigmoid":
        acc = jax.nn.sigmoid(acc)
    if has_ln:
        acc = _layer_norm(ALPHA * res_ref[...] + acc, lg_ref[...], lb_ref[...])
    o_ref[...] = acc.astype(o_ref.dtype)


def _linear(x, w, *, tm, tn, out_dtype, xcol=0, norm_gain=None, colscale=None, ln=None, act=None, name):
    m = x.shape[0]
    k, n = w.shape
    assert m % tm == 0 and n % tn == 0
    in_specs = [pl.BlockSpec((tm, k), lambda i, j: (i, xcol)),
                pl.BlockSpec((k, tn), lambda i, j: (0, j))]
    args = [x, w]
    if norm_gain is not None:
        in_specs.append(pl.BlockSpec((1, k), lambda i, j: (0, 0)))
        args.append(norm_gain.reshape(1, k))
    if colscale is not None:
        in_specs.append(pl.BlockSpec((1, tn), lambda i, j: (0, j)))
        args.append(colscale.reshape(1, n))
    if ln is not None:
        assert tn == n
        res, lg, lb = ln
        in_specs += [pl.BlockSpec((tm, n), lambda i, j: (i, 0)),
                     pl.BlockSpec((1, n), lambda i, j: (0, 0)),
                     pl.BlockSpec((1, n), lambda i, j: (0, 0))]
        args += [res, lg.reshape(1, n), lb.reshape(1, n)]
    body = functools.partial(_linear_body, has_norm=norm_gain is not None,
                             has_scale=colscale is not None, has_ln=ln is not None, act=act)
    return pl.pallas_call(
        body,
        out_shape=jax.ShapeDtypeStruct((m, n), out_dtype),
        grid=(m // tm, n // tn),
        in_specs=in_specs,
        out_specs=pl.BlockSpec((tm, tn), lambda i, j: (i, j)),
        compiler_params=_cparams(("parallel", "arbitrary")),
        name=name,
    )(*args)


def _res_ln_body(x_ref, f_ref, g_ref, b_ref, o_ref):
    o_ref[...] = _layer_norm(ALPHA * x_ref[...] + f_ref[...], g_ref[...], b_ref[...])


def _res_ln(x, f, g, b, *, tm=512):
    m, n = x.shape
    row = pl.BlockSpec((tm, n), lambda i: (i, 0))
    vec = pl.BlockSpec((1, n), lambda i: (0, 0))
    return pl.pallas_call(
        _res_ln_body,
        out_shape=jax.ShapeDtypeStruct((m, n), F32),
        grid=(m // tm,),
        in_specs=[row, row, vec, vec],
        out_specs=row,
        compiler_params=_cparams(("parallel",)),
        name="res_ln",
    )(x, f, g.reshape(1, n), b.reshape(1, n))


def _rope_tables(s):
    half = MLA_ROPE // 2
    freq = ROPE_THETA ** (-jnp.arange(half, dtype=F32) / half)
    ang = jnp.arange(s).astype(F32)[:, None] * freq[None, :]
    cos, sin = jnp.cos(ang), jnp.sin(ang)
    return jnp.concatenate([cos, cos], -1), jnp.concatenate([-sin, sin], -1)


MLA_DQ = MLA_NOPE + MLA_ROPE
MLA_QROWS = MLA_NOPE + 2 * MLA_ROPE
MLA_T = 512
MLA_HPS = 2
MLA_SUB = 256
MLA_PIPE_DEPTH = 3


def _mla_q_body(lat_ref, g_ref, w_ref, cos_ref, sin_ref, o_ref, *, qscale):
    xn = _rms_norm(lat_ref[...], g_ref[...]).astype(BF16)
    y = _dot_nt(w_ref[...], xn)
    cos, sin = cos_ref[...], sin_ref[...]
    for h in range(MLA_HEADS):
        r0 = h * MLA_QROWS
        o_ref[0, h, :MLA_NOPE, :] = (y[r0:r0 + MLA_NOPE] * qscale).astype(BF16)
        a = y[r0 + MLA_NOPE:r0 + MLA_DQ]
        bb = y[r0 + MLA_DQ:r0 + MLA_QROWS]
        o_ref[0, h, MLA_NOPE:, :] = ((a * cos + bb * sin) * qscale).astype(BF16)


def _mla_q_proj(lat, gain, w_t, cos_t, sin_t, b, s, *, tm=MLA_T):
    ns = s // tm
    body = functools.partial(_mla_q_body, qscale=(MLA_DQ ** -0.5) * LOG2E)
    return pl.pallas_call(
        body,
        out_shape=jax.ShapeDtypeStruct((b, MLA_HEADS, MLA_DQ, s), BF16),
        grid=(b * ns,),
        in_specs=[pl.BlockSpec((tm, MLA_Q_RANK), lambda i: (i, 0)),
                  pl.BlockSpec((1, MLA_Q_RANK), lambda i: (0, 0)),
                  pl.BlockSpec(w_t.shape, lambda i: (0, 0)),
                  pl.BlockSpec((MLA_ROPE, tm), lambda i: (0, i % ns)),
                  pl.BlockSpec((MLA_ROPE, tm), lambda i: (0, i % ns))],
        out_specs=pl.BlockSpec((1, MLA_HEADS, MLA_DQ, tm), lambda i: (i // ns, 0, 0, i % ns)),
        compiler_params=_cparams(("parallel",)),
        name="mla_q_proj",
    )(lat, gain.reshape(1, -1), w_t, cos_t, sin_t)


def _mla_kv_body(lat_ref, g_ref, wk_ref, wvt_ref, kr_ref, cos_ref, sin_ref, k_ref, vt_ref):
    xn = _rms_norm(lat_ref[...], g_ref[...]).astype(BF16)
    kn = _dot(xn, wk_ref[...]).astype(BF16)
    vt = _dot_nt(wvt_ref[...], xn).astype(BF16)
    kr = kr_ref[...]
    rot = (kr[:, :MLA_ROPE] * cos_ref[...] + kr[:, MLA_ROPE:] * sin_ref[...]).astype(BF16)
    for h in range(MLA_HEADS):
        k_ref[0, h, :, :MLA_NOPE] = kn[:, h * MLA_NOPE:(h + 1) * MLA_NOPE]
        k_ref[0, h, :, MLA_NOPE:] = rot
        vt_ref[0, h, 0] = vt[h * MLA_V:(h + 1) * MLA_V]


def _mla_kv_proj(lat, gain, wk, wv_t, cosx, sinx, b, s, *, tm=MLA_T):
    ns = s // tm
    return pl.pallas_call(
        _mla_kv_body,
        out_shape=(jax.ShapeDtypeStruct((b, MLA_HEADS, s, MLA_DQ), BF16),
                   jax.ShapeDtypeStruct((b, MLA_HEADS, ns, MLA_V, tm), BF16)),
        grid=(b * ns,),
        in_specs=[pl.BlockSpec((tm, MLA_KV_RANK), lambda i: (i, MLA_Q_RANK // MLA_KV_RANK)),
                  pl.BlockSpec((1, MLA_KV_RANK), lambda i: (0, 0)),
                  pl.BlockSpec(wk.shape, lambda i: (0, 0)),
                  pl.BlockSpec(wv_t.shape, lambda i: (0, 0)),
                  pl.BlockSpec((tm, 2 * MLA_ROPE),
                               lambda i: (i, (MLA_Q_RANK + MLA_KV_RANK) // (2 * MLA_ROPE))),
                  pl.BlockSpec((tm, MLA_ROPE), lambda i: (i % ns, 0)),
                  pl.BlockSpec((tm, MLA_ROPE), lambda i: (i % ns, 0))],
        out_specs=(pl.BlockSpec((1, MLA_HEADS, tm, MLA_DQ), lambda i: (i // ns, 0, i % ns, 0)),
                   pl.BlockSpec((1, MLA_HEADS, 1, MLA_V, tm), lambda i: (i // ns, 0, i % ns, 0, 0))),
        compiler_params=_cparams(("parallel",)),
        name="mla_kv_proj",
    )(lat, gain.reshape(1, -1), wk, wv_t, lat, cosx, sinx)


def _mla_attn_body(qt_ref, k_ref, vt_ref, o_ref, *scr):
    i = pl.program_id(2)
    sub = MLA_SUB
    nsub = MLA_T // sub
    state = {(h, ql): scr[3 * (nsub * h + ql):3 * (nsub * h + ql) + 3]
             for h in range(MLA_HPS) for ql in range(nsub)}
    for m_scr, l_scr, acc_scr in state.values():
        m_scr[...] = jnp.full_like(m_scr, NEG)
        l_scr[...] = jnp.zeros_like(l_scr)
        acc_scr[...] = jnp.zeros_like(acc_scr)

    def scores(chain):
        j, h, kk, ql, diag = chain
        k = k_ref[0, h, pl.ds(pl.multiple_of(j * MLA_T + kk * sub, sub), sub), :]
        s = _dot(k, qt_ref[0, h, :, ql * sub:(ql + 1) * sub])
        if diag:
            key = lax.broadcasted_iota(jnp.int32, s.shape, 0)
            qry = lax.broadcasted_iota(jnp.int32, s.shape, 1)
            s = jnp.where(key <= qry, s, NEG)
        return s

    def update(chain, s):
        j, h, kk, ql, _ = chain
        m_scr, l_scr, acc_scr = state[(h, ql)]
        m_old = m_scr[...]
        m_new = jnp.maximum(m_old, jnp.max(s, 0, keepdims=True))
        a = jnp.exp2(m_old - m_new)
        p = jnp.exp2(s - m_new)
        l_scr[...] = a * l_scr[...] + jnp.sum(p, 0, keepdims=True)
        acc_scr[...] = a * acc_scr[...] + _dot(vt_ref[0, h, j, :, kk * sub:(kk + 1) * sub], p.astype(BF16))
        m_scr[...] = m_new

    def run_chains(chains):
        pending = []
        for chain in chains:
            pending.append((chain, scores(chain)))
            if len(pending) > MLA_PIPE_DEPTH:
                update(*pending.pop(0))
        for item in pending:
            update(*item)

    def far_body(j, carry):
        run_chains([(j, h, kk, ql, False) for kk in range(nsub) for h in range(MLA_HPS) for ql in range(nsub)])
        return carry

    lax.fori_loop(0, i, far_body, 0)
    run_chains([(i, h, kk, ql, kk == ql) for kk in range(nsub) for h in range(MLA_HPS)
                for ql in range(nsub) if kk <= ql])
    for (h, ql), (m_scr, l_scr, acc_scr) in state.items():
        o_t = acc_scr[...] * (1.0 / jnp.maximum(l_scr[...], 1e-30))
        o_ref[0, ql * sub:(ql + 1) * sub, h * MLA_V:(h + 1) * MLA_V] = o_t.T.astype(o_ref.dtype)


def _mla_attention(q_t, k, v_t, b, s):
    tq = MLA_T
    nq = s // tq
    hp = MLA_HPS
    sub_state = [pltpu.VMEM((1, MLA_SUB), F32), pltpu.VMEM((1, MLA_SUB), F32), pltpu.VMEM((MLA_V, MLA_SUB), F32)]
    return pl.pallas_call(
        _mla_attn_body,
        out_shape=jax.ShapeDtypeStruct((b, s, MLA_HEADS * MLA_V), BF16),
        grid=(b, MLA_HEADS // hp, nq),
        in_specs=[pl.BlockSpec((1, hp, MLA_DQ, tq), lambda bb, h, i: (bb, h, 0, i)),
                  pl.BlockSpec((1, hp, s, MLA_DQ), lambda bb, h, i: (bb, h, 0, 0)),
                  pl.BlockSpec((1, hp, nq, MLA_V, tq), lambda bb, h, i: (bb, h, 0, 0, 0))],
        out_specs=pl.BlockSpec((1, tq, hp * MLA_V), lambda bb, h, i: (bb, i, h)),
        scratch_shapes=hp * (tq // MLA_SUB) * sub_state,
        compiler_params=_cparams(("parallel", "parallel", "arbitrary")),
        name="mla_attention",
    )(q_t, k, v_t)


def _mla_mixer(x2, b, s, w_in, q_norm, w_q_up, kv_norm, w_kv_up):
    r0 = MLA_Q_RANK + MLA_KV_RANK
    half = MLA_ROPE // 2
    w_in_ext = jnp.concatenate([w_in, w_in[:, r0 + half:r0 + MLA_ROPE], w_in[:, r0:r0 + half]], 1)
    lat = _linear(x2, w_in_ext.astype(BF16), tm=512, tn=w_in_ext.shape[1], out_dtype=F32, name="mla_in")
    wq = w_q_up.reshape(MLA_Q_RANK, MLA_HEADS, MLA_DQ)
    wr = wq[..., MLA_NOPE:]
    wq = jnp.concatenate([wq, wr[..., half:], wr[..., :half]], -1)
    wq_t = wq.reshape(MLA_Q_RANK, MLA_HEADS * MLA_QROWS).T.astype(BF16)
    wkv = w_kv_up.reshape(MLA_KV_RANK, MLA_HEADS, MLA_NOPE + MLA_V)
    wk = wkv[..., :MLA_NOPE].reshape(MLA_KV_RANK, MLA_HEADS * MLA_NOPE).astype(BF16)
    wv_t = wkv[..., MLA_NOPE:].reshape(MLA_KV_RANK, MLA_HEADS * MLA_V).T.astype(BF16)
    cosx, sinx = _rope_tables(s)
    q_t = _mla_q_proj(lat, q_norm, wq_t, cosx.T, sinx.T, b, s)
    k, v_t = _mla_kv_proj(lat, kv_norm, wk, wv_t, cosx, sinx, b, s)
    o = _mla_attention(q_t, k, v_t, b, s)
    return o.reshape(b * s, MLA_HEADS * MLA_V)


def _ffn_body(x_ref, wg_ref, wu_ref, wd_ref, lg_ref, lb_ref, o_ref, acc_scr):
    f = pl.program_id(1)
    xb = x_ref[...].astype(BF16)
    h = jax.nn.silu(_dot(xb, wg_ref[...])) * _dot(xb, wu_ref[...])
    part = _dot(h.astype(BF16), wd_ref[...])

    @pl.when(f == 0)
    def _():
        acc_scr[...] = part

    @pl.when(f > 0)
    def _():
        acc_scr[...] += part

    @pl.when(f == pl.num_programs(1) - 1)
    def _():
        o_ref[...] = _layer_norm(ALPHA * x_ref[...] + acc_scr[...], lg_ref[...], lb_ref[...])


def _ffn_ln(x, wg, wu, wd, lg, lb, *, tm=512, tf=1408):
    m, d = x.shape
    dff = wg.shape[1]
    assert dff % tf == 0
    vec = pl.BlockSpec((1, d), lambda i, f: (0, 0))
    return pl.pallas_call(
        _ffn_body,
        out_shape=jax.ShapeDtypeStruct((m, d), F32),
        grid=(m // tm, dff // tf),
        in_specs=[pl.BlockSpec((tm, d), lambda i, f: (i, 0)),
                  pl.BlockSpec((d, tf), lambda i, f: (0, f)),
                  pl.BlockSpec((d, tf), lambda i, f: (0, f)),
                  pl.BlockSpec((tf, d), lambda i, f: (f, 0)),
                  vec, vec],
        out_specs=pl.BlockSpec((tm, d), lambda i, f: (i, 0)),
        scratch_shapes=[pltpu.VMEM((tm, d), F32)],
        compiler_params=_cparams(("parallel", "arbitrary")),
        name="ffn_ln",
    )(x, wg, wu, wd, lg.reshape(1, d), lb.reshape(1, d))


def _compress_body(a_ref, pe_ref, w1_ref, w2_ref, o_ref):
    a = (a_ref[...].astype(F32) + pe_ref[...]).astype(BF16)
    h = jax.nn.gelu(_dot(a, w1_ref[...]))
    o_ref[...] = _dot(h.astype(BF16), w2_ref[...]).astype(o_ref.dtype)


def _compress(a, pe, w1, w2, *, tm=512):
    m, k = a.shape
    dh, d = w2.shape
    return pl.pallas_call(
        _compress_body,
        out_shape=jax.ShapeDtypeStruct((m, d), BF16),
        grid=(m // tm,),
        in_specs=[pl.BlockSpec((tm, k), lambda i: (i, 0)),
                  pl.BlockSpec((1, k), lambda i: (0, 0)),
                  pl.BlockSpec((k, dh), lambda i: (0, 0)),
                  pl.BlockSpec((dh, d), lambda i: (0, 0))],
        out_specs=pl.BlockSpec((tm, d), lambda i: (i, 0)),
        compiler_params=_cparams(("parallel",)),
        name="nsa_compress",
    )(a, pe.reshape(1, k), w1.reshape(k, dh).astype(BF16), w2.astype(BF16))


def _rel_bucket_np(dist):
    n = np.maximum(dist, 0)
    max_exact = REL_BUCKETS // 2
    nf = np.maximum(n, 1).astype(np.float32)
    large = max_exact + (np.log(nf / np.float32(max_exact)) / np.float32(math.log(REL_MAX_DIST / max_exact))
                         * np.float32(REL_BUCKETS - max_exact)).astype(np.int32)
    large = np.minimum(large, REL_BUCKETS - 1)
    return np.where(n < max_exact, n, large).astype(np.int32)


def _nsa_tables(rel_bias, s):
    g, hg, tq, kc = NSA_GROUPS, NSA_HG, NSA_TQ, NSA_KC
    assert tq == kc and tq % CMP_STRIDE == 0 and WINDOW == 2 * kc
    assert np.all(_rel_bucket_np(np.arange(tq // 2 - 15, s + tq)) == REL_BUCKETS - 1)
    rb = rel_bias.reshape(REL_BUCKETS, g, hg) * LOG2E

    def tile(base, step, rows, valid):
        p = tq + step * rows
        k = np.arange(p)
        k = np.where(k < p - step * (rows - 1), k, k - p)
        d = base + k
        vec = jnp.where(valid(d)[:, None, None], rb[_rel_bucket_np(d)], NEG)
        vec = vec.transpose(1, 2, 0)
        flat = jnp.tile(vec, (1, 1, rows))[..., :rows * (p - step)]
        mat = flat.reshape(g, hg, rows, p - step)[..., :tq]
        return mat.transpose(0, 2, 1, 3).reshape(g, rows, hg * tq)

    causal = lambda d: d >= 0
    c31 = jnp.broadcast_to(rb[REL_BUCKETS - 1][:, None, :, None], (g, 1, hg, tq)).reshape(g, 1, hg * tq)
    tiles = jnp.stack([tile(0, 1, kc, causal), tile(tq, 1, kc, causal),
                       tile(2 * tq, 1, kc, lambda d: d < WINDOW),
                       jnp.broadcast_to(c31, (g, kc, hg * tq))], 1)
    band = jnp.stack([tile(8 * CMP_STRIDE - CMP_LEN + 1, CMP_STRIDE, NSA_BAND, causal),
                      tile(-(CMP_LEN - 1), CMP_STRIDE, NSA_BAND, causal)], 1)
    band = jnp.where(band > NEG_TEST, band - c31[:, None], NEG)
    return tiles, band, c31


def _overlap_t(nc_pad, nb):
    n = np.arange(nc_pad)[None, :]
    jb = np.arange(nb)[:, None]
    cstart = n * CMP_STRIDE
    cend = cstart + CMP_LEN - 1
    sstart = jb * SEL_BLOCK
    ov = (cstart <= sstart + SEL_BLOCK - 1) & (cend >= sstart) & (n < nc_pad - 1)
    return jnp.asarray(ov.astype(np.float32), BF16)


def _nsa_body(qt_ref, kc_ref, vct_ref, k_ref, vt_ref, gate_ref, tiles_ref,
              band_ref, c31_ref, ovt_ref, o_ref, s_scr, sel_scr, *scr, nb):
    t = pl.program_id(2)
    ring_a, ring_b, state = scr[:NSA_HG], scr[NSA_HG:2 * NSA_HG], scr[2 * NSA_HG:]
    nkc = k_ref.shape[3]
    L = NSA_LANES
    q_t = qt_ref[0, 0, 0]
    c31 = c31_ref[0]
    ncp = kc_ref.shape[2]
    per = NSA_KC // SEL_BLOCK
    blocks_per_tile = NSA_TQ // CMP_STRIDE

    s = _dot(kc_ref[0, 0], q_t) + c31
    first = (t == 0).astype(jnp.int32)
    bs = pl.multiple_of((blocks_per_tile * t - 8) * (1 - first), 8)
    row = lax.broadcasted_iota(jnp.int32, (ncp, L), 0)
    s_scr[...] = jnp.where(row < bs + NSA_BAND, s, NEG)
    s_scr[pl.ds(bs, NSA_BAND), :] += band_ref[0, first]
    s = s_scr[...]
    m = jnp.max(s, 0, keepdims=True)
    m = jnp.where(m < NEG_TEST, 0.0, m)
    p = jnp.exp2(s - m)
    den = jnp.maximum(jnp.sum(p, 0, keepdims=True), 1e-30)
    p = p * (1.0 / den)
    o_c = _dot(vct_ref[0, 0], p.astype(BF16))

    psum = p[:, 0:NSA_TQ]
    for h in range(1, NSA_HG):
        psum = psum + p[:, h * NSA_TQ:(h + 1) * NSA_TQ]
    p1, p2, p3 = _split3(psum)
    ovt = ovt_ref[...]
    imp = _dot(ovt, p1) + _dot(ovt, p2) + _dot(ovt, p3)
    blk = lax.broadcasted_iota(jnp.int32, (nb, NSA_TQ), 0)
    lane = lax.broadcasted_iota(jnp.int32, (nb, NSA_TQ), 1)
    cur = (NSA_TQ // SEL_BLOCK) * t + lane // SEL_BLOCK
    forced = (blk == 0) | (blk == cur) | (blk == cur - 1)
    v = jnp.where(blk > cur, -FORCE, jnp.where(forced, FORCE, imp))
    blk_f = blk.astype(F32)
    sel = jnp.zeros((nb, NSA_TQ), F32)
    for _ in range(min(SEL_TOPN, nb)):
        mx = jnp.max(v, 0, keepdims=True)
        idx = jnp.min(jnp.where(v == mx, blk_f, float(nb)), 0, keepdims=True)
        hit = blk_f == idx
        sel = jnp.where(hit, 1.0, sel)
        v = jnp.where(hit, -jnp.inf, v)
    selneg = jnp.where(sel > 0.5, 0.0, NEG)
    selneg = jnp.concatenate([selneg] * NSA_HG, 1)
    for c in range(nb // per):
        sel_scr[c] = selneg[per * c:per * (c + 1), :]

    sel_scr[nkc] = jnp.zeros((per, L), F32)
    sel_scr[nkc + 1] = jnp.full((per, L), NEG, F32)

    n_sel = t + 1
    n_slots = n_sel + jnp.minimum(t, 2) + 1

    def slot_params(c):
        is_sel = c < n_sel
        is_win = jnp.logical_and(c >= n_sel, c < n_slots)
        d_win = c - n_sel
        delta = t - c
        br = is_win.astype(jnp.int32)
        kidx = jnp.where(is_sel, c, jnp.where(is_win, t - d_win, 0))
        sidx = jnp.where(is_sel, c, jnp.where(is_win, nkc, nkc + 1))
        tidx = jnp.where(is_sel, jnp.where(delta >= 2, 3, delta), jnp.where(is_win, d_win, 3))
        return br, kidx, sidx, tidx

    def issue(c, ring):
        br, kidx, sidx, tidx = slot_params(c)
        k = k_ref[0, 0, br, kidx]
        for h in range(NSA_HG):
            sl = slice(h * NSA_TQ, (h + 1) * NSA_TQ)
            rows = sel_scr[sidx, :, sl]
            add = jnp.concatenate([jnp.broadcast_to(rows[r:r + 1], (SEL_BLOCK, NSA_TQ)) for r in range(per)], 0)
            ring[h][...] = _dot(k, qt_ref[0, 0, 0, :, sl]) + add + tiles_ref[0, tidx, :, sl]
            yield

    def consume(c, ring):
        br, kidx, _, _ = slot_params(c)
        vt = vt_ref[0, 0, br, kidx]
        for h in range(NSA_HG):
            m_scr, l_scr, acc_scr = state[3 * h:3 * h + 3]
            sc = ring[h][...]
            m_old = m_scr[br]
            m_new = jnp.maximum(m_old, jnp.max(sc, 0, keepdims=True))
            a = jnp.exp2(m_old - m_new)
            pp = jnp.exp2(sc - m_new)
            l_scr[br] = a * l_scr[br] + jnp.sum(pp, 0, keepdims=True)
            acc_scr[br] = a * acc_scr[br] + _dot(vt, pp.astype(BF16))
            m_scr[br] = m_new
            yield

    def interleave(*gens):
        for _ in zip(*gens):
            pass

    for h in range(NSA_HG):
        m_scr, l_scr, acc_scr = state[3 * h:3 * h + 3]
        m_scr[...] = jnp.full_like(m_scr, NEG)
        l_scr[...] = jnp.zeros_like(l_scr)
        acc_scr[...] = jnp.zeros_like(acc_scr)

    interleave(issue(0, ring_a))

    def trip(i, carry):
        c0 = 2 * i
        interleave(issue(c0 + 1, ring_b), consume(c0, ring_a))
        interleave(issue(c0 + 2, ring_a), consume(c0 + 1, ring_b))
        return carry

    lax.fori_loop(0, (n_slots + 1) // 2, trip, 0)

    gate = gate_ref[0, 0, 0]
    outs = []
    for h in range(NSA_HG):
        sl = slice(h * NSA_TQ, (h + 1) * NSA_TQ)
        _, l_scr, acc_scr = state[3 * h:3 * h + 3]
        o_s = acc_scr[0] * (1.0 / jnp.maximum(l_scr[0], 1e-30))
        o_w = acc_scr[1] * (1.0 / jnp.maximum(l_scr[1], 1e-30))
        outs.append(gate[0:1, sl] * o_c[:, sl] + gate[1:2, sl] * o_s + gate[2:3, sl] * o_w)
    o_ref[0] = jnp.concatenate(outs, 0).T.astype(o_ref.dtype)


def _nsa_attention(q_t, kcmp, vcmp_t, k, v_t, gate, tiles, band, c31, ovt):
    b, g, nqt = q_t.shape[:3]
    ncp = kcmp.shape[2]
    nkc = k.shape[3]
    nb = ovt.shape[0]
    L = NSA_LANES
    body = functools.partial(_nsa_body, nb=nb)
    ring = NSA_HG * [pltpu.VMEM((NSA_KC, NSA_TQ), F32)]
    head_state = [pltpu.VMEM((2, 1, NSA_TQ), F32), pltpu.VMEM((2, 1, NSA_TQ), F32),
                  pltpu.VMEM((2, NSA_DV, NSA_TQ), F32)]
    grp = lambda bb, gg, t: (bb, gg, 0, 0)
    grp6 = lambda bb, gg, t: (bb, gg, 0, 0, 0, 0)
    return pl.pallas_call(
        body,
        out_shape=jax.ShapeDtypeStruct((b, nqt * NSA_TQ, g * NSA_HG * NSA_DV), BF16),
        grid=(b, g, nqt),
        in_specs=[pl.BlockSpec((1, 1, 1, NSA_DKP, L), lambda bb, gg, t: (bb, gg, t, 0, 0)),
                  pl.BlockSpec((1, 1, ncp, NSA_DKP), grp),
                  pl.BlockSpec((1, 1, NSA_DV, ncp), grp),
                  pl.BlockSpec((1, 1, 2, nkc, NSA_KC, NSA_DKP), grp6),
                  pl.BlockSpec((1, 1, 2, nkc, NSA_DV, NSA_KC), grp6),
                  pl.BlockSpec((1, 1, 1, N_BRANCH, L), lambda bb, gg, t: (bb, gg, t, 0, 0)),
                  pl.BlockSpec((1, 4, NSA_KC, L), lambda bb, gg, t: (gg, 0, 0, 0)),
                  pl.BlockSpec((1, 2, NSA_BAND, L), lambda bb, gg, t: (gg, 0, 0, 0)),
                  pl.BlockSpec((1, 1, L), lambda bb, gg, t: (gg, 0, 0)),
                  pl.BlockSpec(ovt.shape, lambda bb, gg, t: (0, 0))],
        out_specs=pl.BlockSpec((1, NSA_TQ, NSA_HG * NSA_DV), lambda bb, gg, t: (bb, t, gg)),
        scratch_shapes=[pltpu.VMEM((ncp, L), F32),
                        pltpu.VMEM((nb * SEL_BLOCK // NSA_KC + 2, NSA_KC // SEL_BLOCK, L), F32)]
        + 2 * ring + NSA_HG * head_state,
        compiler_params=_cparams(("parallel", "parallel", "arbitrary")),
        name="nsa_attention",
    )(q_t, kcmp, vcmp_t, k, v_t, gate, tiles, band, c31, ovt)


def _nsa_proj_body(x_ref, wq_ref, wk_ref, wv_ref, wg_ref, wc_ref,
                   q_ref, k_ref, v_ref, gate_ref, c_ref, *, qscale):
    g, hg, tq = NSA_GROUPS, NSA_HG, NSA_TQ
    xb = x_ref[...].astype(BF16)
    q_t = _dot_nt(wq_ref[...], xb) * qscale
    gate_t = jax.nn.sigmoid(_dot_nt(wg_ref[...], xb))
    for gg in range(g):
        for h in range(hg):
            head = gg * hg + h
            q_ref[0, gg, 0, :, h * tq:(h + 1) * tq] = q_t[head * NSA_DKP:(head + 1) * NSA_DKP].astype(BF16)
            r0 = head * NSA_GATE_ROWS
            gate_ref[0, gg, 0, :, h * tq:(h + 1) * tq] = gate_t[r0:r0 + N_BRANCH]
    k = _dot(xb, wk_ref[...]).astype(BF16)
    v_t = _dot_nt(wv_ref[...], xb).astype(BF16)
    for gg in range(g):
        for br in range(2):
            k_ref[0, gg, br, 0] = k[:, (br * g + gg) * NSA_DKP:(br * g + gg + 1) * NSA_DKP]
            v_ref[0, gg, br, 0] = v_t[(br * g + gg) * NSA_DV:(br * g + gg + 1) * NSA_DV]
    c_ref[...] = _dot(xb, wc_ref[...]).astype(BF16)


def _nsa_proj(x2, b, s, wq_t, wk, wv_t, wg_t, wc):
    g, hg, tq, L = NSA_GROUPS, NSA_HG, NSA_TQ, NSA_LANES
    nqt = s // tq
    t, d = x2.shape
    full = lambda i: (0, 0)
    tile5 = lambda i: (i // nqt, 0, i % nqt, 0, 0)
    tile6 = lambda i: (i // nqt, 0, 0, i % nqt, 0, 0)
    body = functools.partial(_nsa_proj_body, qscale=(NSA_DK ** -0.5) * LOG2E)
    return pl.pallas_call(
        body,
        out_shape=(jax.ShapeDtypeStruct((b, g, nqt, NSA_DKP, L), BF16),
                   jax.ShapeDtypeStruct((b, g, 2, nqt, NSA_KC, NSA_DKP), BF16),
                   jax.ShapeDtypeStruct((b, g, 2, nqt, NSA_DV, NSA_KC), BF16),
                   jax.ShapeDtypeStruct((b, g, nqt, N_BRANCH, L), F32),
                   jax.ShapeDtypeStruct((t, wc.shape[1]), BF16)),
        grid=(t // tq,),
        in_specs=[pl.BlockSpec((tq, d), lambda i: (i, 0)),
                  pl.BlockSpec(wq_t.shape, full), pl.BlockSpec(wk.shape, full), pl.BlockSpec(wv_t.shape, full),
                  pl.BlockSpec(wg_t.shape, full), pl.BlockSpec(wc.shape, full)],
        out_specs=(pl.BlockSpec((1, g, 1, NSA_DKP, L), tile5),
                   pl.BlockSpec((1, g, 2, 1, NSA_KC, NSA_DKP), tile6),
                   pl.BlockSpec((1, g, 2, 1, NSA_DV, NSA_KC), tile6),
                   pl.BlockSpec((1, g, 1, N_BRANCH, L), tile5),
                   pl.BlockSpec((tq, wc.shape[1]), lambda i: (i, 0))),
        compiler_params=_cparams(("parallel",)),
        name="nsa_proj",
    )(x2, wq_t, wk, wv_t, wg_t, wc)


def _nsa_mixer(x2, b, s, w_in, pe_k, w1_k, w2_k, pe_v, w1_v, w2_v, rel_bias):
    assert NSA_TQ == NSA_KC
    t, d = x2.shape
    h, g, hg, dk, dv = NSA_HEADS, NSA_GROUPS, NSA_HG, NSA_DK, NSA_DV
    nb = s // SEL_BLOCK
    sizes = [h * dk, g * dk, g * dv, g * dk, g * dv, g * dk, g * dv, h * N_BRANCH]
    c = [0] + [int(v) for v in np.cumsum(sizes)]
    cols = [w_in[:, c[i]:c[i + 1]] for i in range(len(sizes))]
    w_q, w_kc, w_vc, w_ks, w_vs, w_kw, w_vw, w_gate = cols

    def pad_last(a, n):
        return jnp.pad(a, [(0, 0)] * (a.ndim - 1) + [(0, n - a.shape[-1])])

    wq_t = pad_last(w_q.reshape(d, h, dk), NSA_DKP).reshape(d, h * NSA_DKP).T.astype(BF16)
    wk = jnp.concatenate([pad_last(w.reshape(d, g, dk), NSA_DKP).reshape(d, g * NSA_DKP)
                          for w in (w_ks, w_kw)], 1).astype(BF16)
    wv_t = jnp.concatenate([w_vs, w_vw], 1).T.astype(BF16)
    wg_t = pad_last(w_gate.reshape(d, h, N_BRANCH), NSA_GATE_ROWS).reshape(d, h * NSA_GATE_ROWS).T.astype(BF16)
    wc = jnp.concatenate([w_kc, w_vc], 1).astype(BF16)
    q_t, k, v_t, gate, ctok = _nsa_proj(x2, b, s, wq_t, wk, wv_t, wg_t, wc)
    kc_tok = ctok[:, :g * dk].reshape(b, s, g, dk)
    vc_tok = ctok[:, g * dk:].reshape(b, s, g, dv)

    nch = s // CMP_STRIDE

    def unfold(tok, dd):
        ch = tok.reshape(b, nch, CMP_STRIDE, g, dd).transpose(0, 3, 1, 2, 4).reshape(b, g, nch, CMP_STRIDE * dd)
        nxt = jnp.concatenate([ch[:, :, 1:], jnp.zeros_like(ch[:, :, :1])], 2)
        return jnp.concatenate([ch, nxt], -1).reshape(b * g * nch, CMP_LEN * dd)

    k_cmp = _compress(unfold(kc_tok, dk), pe_k, w1_k, pad_last(w2_k, NSA_DKP)).reshape(b, g, nch, NSA_DKP)
    v_cmp = _compress(unfold(vc_tok, dv), pe_v, w1_v, w2_v).reshape(b, g, nch, dv)
    vcmp_t = v_cmp.transpose(0, 1, 3, 2)

    tiles, band, c31 = _nsa_tables(rel_bias, s)
    ovt = _overlap_t(nch, nb)
    o = _nsa_attention(q_t, k_cmp, vcmp_t, k, v_t, gate, tiles, band, c31, ovt)
    return o.reshape(t, h * dv)


def _router_body(x_ref, w_ref, o_ref):
    x1, x2, x3 = _split3(x_ref[...])
    w1, w2, w3 = _split3(w_ref[...])
    acc = _dot(x1, w1)
    acc += _dot(x1, w2) + _dot(x2, w1)
    acc += _dot(x1, w3) + _dot(x2, w2) + _dot(x3, w1)
    o_ref[...] = acc


def _router(x, w, *, tm=512):
    m, d = x.shape
    wp = jnp.pad(w, ((0, 0), (0, 128 - w.shape[1])))
    return pl.pallas_call(
        _router_body,
        out_shape=jax.ShapeDtypeStruct((m, 128), F32),
        grid=(m // tm,),
        in_specs=[pl.BlockSpec((tm, d), lambda i: (i, 0)), pl.BlockSpec((d, 128), lambda i: (0, 0))],
        out_specs=pl.BlockSpec((tm, 128), lambda i: (i, 0)),
        compiler_params=_cparams(("parallel",)),
        name="moe_router",
    )(x, wp)[:, :w.shape[1]]


def _dispatch_body(ir_ref, ic_ref, fl_ref, x_ref, rt_ref, o_ref):
    i = pl.program_id(0)
    flag = fl_ref[i]
    tok = ic_ref[i] * MOE_TC + lax.broadcasted_iota(jnp.int32, (MOE_BLK, MOE_TC), 1)
    onehot = jnp.where(rt_ref[...] == tok, 1.0, 0.0).astype(BF16)
    rows = _dot(onehot, x_ref[...])

    @pl.when(flag == 3)
    def _():
        o_ref[...] = rows.astype(o_ref.dtype)

    @pl.when(flag == 1)
    def _():
        o_ref[...] = (o_ref[...].astype(F32) + rows).astype(o_ref.dtype)


def _dispatch(x_bf, row_tok_col, item_r, item_c, flags):
    t, d = x_bf.shape
    r = row_tok_col.shape[0]
    ni = item_r.shape[0]
    gs = pltpu.PrefetchScalarGridSpec(
        num_scalar_prefetch=3, grid=(ni,),
        in_specs=[pl.BlockSpec((MOE_TC, d), lambda i, ir, ic, fl: (ic[i], 0)),
                  pl.BlockSpec((MOE_BLK, 1), lambda i, ir, ic, fl: (ir[i], 0))],
        out_specs=pl.BlockSpec((MOE_BLK, d), lambda i, ir, ic, fl: (ir[i], 0)))
    return pl.pallas_call(
        _dispatch_body, grid_spec=gs,
        out_shape=jax.ShapeDtypeStruct((r, d), BF16),
        compiler_params=_cparams(("arbitrary",)),
        name="moe_dispatch",
    )(item_r, item_c, flags, x_bf, row_tok_col)


def _expert_body(be_ref, x_ref, wg_ref, wu_ref, wd_ref, rw_ref, o_ref, acc_scr):
    f = pl.program_id(1)
    xb = x_ref[...]
    h = jax.nn.silu(_dot(xb, wg_ref[0])) * _dot(xb, wu_ref[0])
    part = _dot(h.astype(BF16), wd_ref[0])

    @pl.when(f == 0)
    def _():
        acc_scr[...] = part

    @pl.when(f > 0)
    def _():
        acc_scr[...] += part

    @pl.when(f == pl.num_programs(1) - 1)
    def _():
        o_ref[...] = (acc_scr[...] * rw_ref[...]).astype(o_ref.dtype)


def _experts(xs, wg, wu, wd, row_w_col, block_expert):
    r, d = xs.shape
    nbk = r // MOE_BLK
    nf = wg.shape[2] // MOE_TF
    gs = pltpu.PrefetchScalarGridSpec(
        num_scalar_prefetch=1, grid=(nbk, nf),
        in_specs=[pl.BlockSpec((MOE_BLK, d), lambda i, f, be: (i, 0)),
                  pl.BlockSpec((1, d, MOE_TF), lambda i, f, be: (be[i], 0, f)),
                  pl.BlockSpec((1, d, MOE_TF), lambda i, f, be: (be[i], 0, f)),
                  pl.BlockSpec((1, MOE_TF, d), lambda i, f, be: (be[i], f, 0)),
                  pl.BlockSpec((MOE_BLK, 1), lambda i, f, be: (i, 0))],
        out_specs=pl.BlockSpec((MOE_BLK, d), lambda i, f, be: (i, 0)),
        scratch_shapes=[pltpu.VMEM((MOE_BLK, d), F32)])
    return pl.pallas_call(
        _expert_body, grid_spec=gs,
        out_shape=jax.ShapeDtypeStruct((r, d), BF16),
        compiler_params=_cparams(("parallel", "arbitrary")),
        name="moe_experts",
    )(block_expert, xs, wg, wu, wd, row_w_col)


def _combine_body(ir_ref, ic_ref, fl_ref, y_ref, rt_ref, x_ref, g_ref, b_ref, o_ref):
    i = pl.program_id(0)
    flag = fl_ref[i]
    tok = ic_ref[i] * MOE_TC + lax.broadcasted_iota(jnp.int32, (MOE_TC, MOE_BLK), 0)
    onehot = jnp.where(rt_ref[0] == tok, 1.0, 0.0).astype(BF16)
    part = _dot(onehot, y_ref[...])

    @pl.when((flag & 3) == 3)
    def _():
        o_ref[...] = part

    @pl.when((flag & 3) == 1)
    def _():
        o_ref[...] += part

    @pl.when((flag & 4) == 4)
    def _():
        o_ref[...] = _layer_norm(ALPHA * x_ref[...] + o_ref[...], g_ref[...], b_ref[...])


def _combine_ln(out_rows, row_tok_lane, item_r, item_c, flags, x, g, b):
    r, d = out_rows.shape
    t = x.shape[0]
    ni = item_r.shape[0]
    vec = pl.BlockSpec((1, d), lambda i, ir, ic, fl: (0, 0))
    gs = pltpu.PrefetchScalarGridSpec(
        num_scalar_prefetch=3, grid=(ni,),
        in_specs=[pl.BlockSpec((MOE_BLK, d), lambda i, ir, ic, fl: (ir[i], 0)),
                  pl.BlockSpec((1, 1, MOE_BLK), lambda i, ir, ic, fl: (ir[i], 0, 0)),
                  pl.BlockSpec((MOE_TC, d), lambda i, ir, ic, fl: (ic[i], 0)),
                  vec, vec],
        out_specs=pl.BlockSpec((MOE_TC, d), lambda i, ir, ic, fl: (ic[i], 0)))
    return pl.pallas_call(
        _combine_body, grid_spec=gs,
        out_shape=jax.ShapeDtypeStruct((t, d), F32),
        compiler_params=_cparams(("arbitrary",)),
        name="moe_combine_ln",
    )(item_r, item_c, flags, out_rows, row_tok_lane, x, g.reshape(1, d), b.reshape(1, d))


def _moe_plan(top_idx, wts, t):
    e, blk, tc = N_EXPERTS, MOE_BLK, MOE_TC
    a = t * TOP_K
    i32 = jnp.int32
    exp_flat = top_idx.reshape(a).astype(i32)
    tok_flat = jnp.arange(a, dtype=i32) // TOP_K
    _, tok_sorted, w_sorted = lax.sort((exp_flat, tok_flat, wts.reshape(a)), num_keys=1, is_stable=True)
    counts = jnp.sum((exp_flat[:, None] == jnp.arange(e, dtype=i32)[None, :]).astype(i32), 0)
    padded = ((counts + blk - 1) // blk) * blk
    grp_start = jnp.cumsum(counts) - counts
    pad_end = jnp.cumsum(padded)
    pad_start = pad_end - padded
    nbk = a // blk + e
    r = nbk * blk
    tok_ext = jnp.concatenate([tok_sorted, jnp.full((r - a,), -1, i32)])
    w_ext = jnp.concatenate([w_sorted, jnp.zeros((r - a,), F32)])
    rows = jnp.arange(r, dtype=i32)
    row_tok = jnp.full((r,), -1, i32)
    row_w = jnp.zeros((r,), F32)
    for ee in range(e):
        inside = (rows >= pad_start[ee]) & (rows < pad_start[ee] + counts[ee])
        shift = pad_start[ee] - grp_start[ee]
        row_tok = jnp.where(inside, jnp.roll(tok_ext, shift), row_tok)
        row_w = jnp.where(inside, jnp.roll(w_ext, shift), row_w)
    blk_first = jnp.arange(nbk, dtype=i32) * blk
    block_expert = jnp.minimum(jnp.sum((pad_end[None, :] <= blk_first[:, None]).astype(i32), 1), e - 1)

    rt = row_tok.reshape(nbk, blk)
    valid = rt >= 0
    t_lo = jnp.min(jnp.where(valid, rt, t), 1)
    t_hi = jnp.max(rt, 1)
    has = t_hi >= 0
    c_lo = jnp.where(has, t_lo // tc, 0)
    c_hi = jnp.where(has, t_hi // tc, 0)
    n_it = c_hi - c_lo + 1
    off_end = jnp.cumsum(n_it)
    off_start = off_end - n_it
    total = off_end[-1]
    ni = nbk + e * (t // tc)
    idx = jnp.arange(ni, dtype=i32)
    ok = idx < total
    ir = jnp.minimum(jnp.sum((off_end[None, :] <= idx[:, None]).astype(i32), 1), nbk - 1)
    ic = jnp.where(ok, c_lo[ir] + idx - off_start[ir], c_hi[nbk - 1]).astype(i32)
    first = ok & (idx == off_start[ir])
    d_flags = ok.astype(i32) + 2 * first.astype(i32)

    key = jnp.where(ok, ic * nbk + ir, jnp.iinfo(jnp.int32).max)
    perm = jnp.argsort(key)
    ok2 = ok[perm]
    last = total - 1
    cr = jnp.where(ok2, ir[perm], ir[perm][last]).astype(i32)
    cc = jnp.where(ok2, ic[perm], ic[perm][last]).astype(i32)
    first2 = ok2 & jnp.concatenate([jnp.ones((1,), bool), cc[1:] != cc[:-1]])
    last2 = ok2 & jnp.concatenate([(cc[1:] != cc[:-1]) | ~ok2[1:], jnp.ones((1,), bool)])
    c_flags = ok2.astype(i32) + 2 * first2.astype(i32) + 4 * last2.astype(i32)
    return dict(row_tok=row_tok, row_w=row_w, block_expert=block_expert,
                d_items=(ir, ic, d_flags), c_items=(cr, cc, c_flags), nbk=nbk)


def _moe_ln(x2, w_router, wg, wu, wd, ln_g, ln_b):
    t, d = x2.shape
    logits = _router(x2, w_router)
    top_val, top_idx = lax.top_k(logits, TOP_K)
    wts = jax.nn.softmax(top_val, -1)
    plan = _moe_plan(top_idx, wts, t)
    nbk = plan["nbk"]
    xs = _dispatch(x2.astype(BF16), plan["row_tok"].reshape(-1, 1), *plan["d_items"])
    out_rows = _experts(xs, wg.astype(BF16), wu.astype(BF16), wd.astype(BF16),
                        plan["row_w"].reshape(-1, 1), plan["block_expert"])
    return _combine_ln(out_rows, plan["row_tok"].reshape(nbk, 1, MOE_BLK), *plan["c_items"], x2, ln_g, ln_b)


def _forward(x, mla_w_in, mla_q_norm, mla_w_q_up, mla_kv_norm, mla_w_kv_up, mla_w_out, nsa_w_in,
             nsa_cmp_pe_k, nsa_cmp_w1_k, nsa_cmp_w2_k, nsa_cmp_pe_v, nsa_cmp_w1_v, nsa_cmp_w2_v,
             nsa_w_out, rel_bias, ffn_w_gate, ffn_w_up, ffn_w_down, moe_w_router, moe_w_gate,
             moe_w_up, moe_w_down, ln_mix_g, ln_mix_b, ln_ffn_g, ln_ffn_b):
    b, s, d = x.shape
    x2 = x.reshape(b * s, d)
    o = _mla_mixer(x2, b, s, mla_w_in[0], mla_q_norm[0], mla_w_q_up[0], mla_kv_norm[0], mla_w_kv_up[0])
    x2 = _linear(o, mla_w_out[0].astype(BF16), tm=512, tn=d, out_dtype=F32,
                 ln=(x2, ln_mix_g[0], ln_mix_b[0]), name="mla_out_ln")
    x2 = _ffn_ln(x2, ffn_w_gate[0].astype(BF16), ffn_w_up[0].astype(BF16), ffn_w_down[0].astype(BF16),
                 ln_ffn_g[0], ln_ffn_b[0])
    o = _nsa_mixer(x2, b, s, nsa_w_in[0], nsa_cmp_pe_k[0], nsa_cmp_w1_k[0], nsa_cmp_w2_k[0],
                   nsa_cmp_pe_v[0], nsa_cmp_w1_v[0], nsa_cmp_w2_v[0], rel_bias)
    x2 = _linear(o, nsa_w_out[0].astype(BF16), tm=512, tn=d, out_dtype=F32,
                 ln=(x2, ln_mix_g[1], ln_mix_b[1]), name="nsa_out_ln")
    x2 = _moe_ln(x2, moe_w_router[0], moe_w_gate[0], moe_w_up[0], moe_w_down[0], ln_ffn_g[1], ln_ffn_b[1])
    return x2.reshape(b, s, d)


@jax.jit
def kernel(x, mla_w_in, mla_q_norm, mla_w_q_up, mla_kv_norm, mla_w_kv_up, mla_w_out, nsa_w_in,
           nsa_cmp_pe_k, nsa_cmp_w1_k, nsa_cmp_w2_k, nsa_cmp_pe_v, nsa_cmp_w1_v, nsa_cmp_w2_v,
           nsa_w_out, rel_bias, ffn_w_gate, ffn_w_up, ffn_w_down, moe_w_router, moe_w_gate,
           moe_w_up, moe_w_down, ln_mix_g, ln_mix_b, ln_ffn_g, ln_ffn_b):
    return _forward(x, mla_w_in, mla_q_norm, mla_w_q_up, mla_kv_norm, mla_w_kv_up, mla_w_out, nsa_w_in,
                    nsa_cmp_pe_k, nsa_cmp_w1_k, nsa_cmp_w2_k, nsa_cmp_pe_v, nsa_cmp_w1_v, nsa_cmp_w2_v,
                    nsa_w_out, rel_bias, ffn_w_gate, ffn_w_up, ffn_w_down, moe_w_router, moe_w_gate,
                    moe_w_up, moe_w_down, ln_mix_g, ln_mix_b, ln_ffn_g, ln_ffn_b)
```

```python
import functools
import math

import numpy as np
import jax
import jax.numpy as jnp
from jax import lax
from jax.experimental import pallas as pl
from jax.experimental.pallas import tpu as pltpu

F32 = jnp.float32
BF16 = jnp.bfloat16

D_MODEL = 1024
DEPTH = 2

MLA_HEADS = 8
MLA_Q_RANK = 512
MLA_KV_RANK = 256
MLA_NOPE = 128
MLA_ROPE = 64
MLA_V = 128
ROPE_THETA = 10000.0

NSA_HEADS = 16
NSA_GROUPS = 4
NSA_HG = NSA_HEADS // NSA_GROUPS
NSA_DK = 96
NSA_DV = 64
CMP_LEN = 32
CMP_STRIDE = 16
SEL_BLOCK = 64
SEL_TOPN = 16
WINDOW = 512
N_BRANCH = 3
FORCE = 1e6

REL_BUCKETS = 32
REL_MAX_DIST = 128

D_FF = 2816
N_EXPERTS = 8
TOP_K = 2
D_FF_EXPERT = 3584

LN_EPS = 1e-5
RMS_EPS = 1e-6

ALPHA = (2.0 * DEPTH) ** 0.25

NEG = -1e30
NEG_TEST = -1e29

V7X_VMEM_LIMIT = 56 * 1024 * 1024

LOG2E = 1.4426950408889634

NSA_TQ = 256
NSA_LANES = NSA_HG * NSA_TQ
NSA_KC = 256
NSA_DKP = 128
NSA_GATE_ROWS = 8
NSA_BAND = 24

MOE_BLK = 512
MOE_TC = 512
MOE_TF = 1792


def _cparams(sem, vmem=V7X_VMEM_LIMIT):
    return pltpu.CompilerParams(dimension_semantics=sem, vmem_limit_bytes=vmem)


def _layer_norm(r, g, b):
    mu = jnp.mean(r, -1, keepdims=True)
    d = r - mu
    var = jnp.mean(d * d, -1, keepdims=True)
    return d * lax.rsqrt(var + LN_EPS) * g + b


def _rms_norm(x, g):
    return x * lax.rsqrt(jnp.mean(x * x, -1, keepdims=True) + RMS_EPS) * g


def _split3(a):
    a1 = a.astype(BF16)
    r1 = a - a1.astype(F32)
    a2 = r1.astype(BF16)
    a3 = (r1 - a2.astype(F32)).astype(BF16)
    return a1, a2, a3


def _dot(a, b):
    return jnp.dot(a, b, preferred_element_type=F32)


def _dot_nt(a, b):
    return lax.dot_general(a, b, (((1,), (1,)), ((), ())), preferred_element_type=F32)


def _linear_body(*refs, has_ln):
    it = iter(refs)
    x_ref = next(it)
    w_ref = next(it)
    if has_ln:
        res_ref, lg_ref, lb_ref = next(it), next(it), next(it)
    o_ref = next(it)
    acc = _dot(x_ref[...].astype(BF16), w_ref[...])
    if has_ln:
        acc = _layer_norm(ALPHA * res_ref[...] + acc, lg_ref[...], lb_ref[...])
    o_ref[...] = acc.astype(o_ref.dtype)


def _linear(x, w, *, tm, tn, out_dtype, ln=None, name):
    m, k = x.shape
    n = w.shape[1]
    assert m % tm == 0 and n % tn == 0
    in_specs = [pl.BlockSpec((tm, k), lambda i, j: (i, 0)),
                pl.BlockSpec((k, tn), lambda i, j: (0, j))]
    args = [x, w]
    if ln is not None:
        assert tn == n
        res, lg, lb = ln
        in_specs += [pl.BlockSpec((tm, n), lambda i, j: (i, 0)),
                     pl.BlockSpec((1, n), lambda i, j: (0, 0)),
                     pl.BlockSpec((1, n), lambda i, j: (0, 0))]
        args += [res, lg.reshape(1, n), lb.reshape(1, n)]
    return pl.pallas_call(
        functools.partial(_linear_body, has_ln=ln is not None),
        out_shape=jax.ShapeDtypeStruct((m, n), out_dtype),
        grid=(m // tm, n // tn),
        in_specs=in_specs,
        out_specs=pl.BlockSpec((tm, tn), lambda i, j: (i, j)),
        compiler_params=_cparams(("parallel", "arbitrary")),
        name=name,
    )(*args)


def _rope_tables(s):
    half = MLA_ROPE // 2
    freq = ROPE_THETA ** (-jnp.arange(half, dtype=F32) / half)
    ang = jnp.arange(s).astype(F32)[:, None] * freq[None, :]
    cos, sin = jnp.cos(ang), jnp.sin(ang)
    return jnp.concatenate([cos, cos], -1), jnp.concatenate([-sin, sin], -1)


MLA_DQ = MLA_NOPE + MLA_ROPE
MLA_QROWS = MLA_NOPE + 2 * MLA_ROPE
MLA_T = 512
MLA_HPS = 2
MLA_SUB = 256


def _mla_q_body(lat_ref, g_ref, w_ref, cos_ref, sin_ref, o_ref, *, qscale):
    xn = _rms_norm(lat_ref[...], g_ref[...]).astype(BF16)
    y = _dot_nt(w_ref[...], xn)
    cos, sin = cos_ref[...], sin_ref[...]
    for h in range(MLA_HEADS):
        r0 = h * MLA_QROWS
        o_ref[0, h, :MLA_NOPE, :] = (y[r0:r0 + MLA_NOPE] * qscale).astype(BF16)
        a = y[r0 + MLA_NOPE:r0 + MLA_DQ]
        bb = y[r0 + MLA_DQ:r0 + MLA_QROWS]
        o_ref[0, h, MLA_NOPE:, :] = ((a * cos + bb * sin) * qscale).astype(BF16)


def _mla_q_proj(lat, gain, w_t, cos_t, sin_t, b, s, *, tm=MLA_T):
    ns = s // tm
    body = functools.partial(_mla_q_body, qscale=(MLA_DQ ** -0.5) * LOG2E)
    return pl.pallas_call(
        body,
        out_shape=jax.ShapeDtypeStruct((b, MLA_HEADS, MLA_DQ, s), BF16),
        grid=(b * ns,),
        in_specs=[pl.BlockSpec((tm, MLA_Q_RANK), lambda i: (i, 0)),
                  pl.BlockSpec((1, MLA_Q_RANK), lambda i: (0, 0)),
                  pl.BlockSpec(w_t.shape, lambda i: (0, 0)),
                  pl.BlockSpec((MLA_ROPE, tm), lambda i: (0, i % ns)),
                  pl.BlockSpec((MLA_ROPE, tm), lambda i: (0, i % ns))],
        out_specs=pl.BlockSpec((1, MLA_HEADS, MLA_DQ, tm), lambda i: (i // ns, 0, 0, i % ns)),
        compiler_params=_cparams(("parallel",)),
        name="mla_q_proj",
    )(lat, gain.reshape(1, -1), w_t, cos_t, sin_t)


def _mla_kv_body(lat_ref, g_ref, wk_ref, wvt_ref, kr_ref, cos_ref, sin_ref, k_ref, vt_ref):
    xn = _rms_norm(lat_ref[...], g_ref[...]).astype(BF16)
    kn = _dot(xn, wk_ref[...]).astype(BF16)
    vt = _dot_nt(wvt_ref[...], xn).astype(BF16)
    kr = kr_ref[...]
    rot = (kr[:, :MLA_ROPE] * cos_ref[...] + kr[:, MLA_ROPE:] * sin_ref[...]).astype(BF16)
    for h in range(MLA_HEADS):
        k_ref[0, h, :, :MLA_NOPE] = kn[:, h * MLA_NOPE:(h + 1) * MLA_NOPE]
        k_ref[0, h, :, MLA_NOPE:] = rot
        for kk in range(MLA_T // MLA_SUB):
            vt_ref[0, h, kk] = vt[h * MLA_V:(h + 1) * MLA_V, kk * MLA_SUB:(kk + 1) * MLA_SUB]


def _mla_kv_proj(lat, gain, wk, wv_t, cosx, sinx, b, s, *, tm=MLA_T):
    ns = s // tm
    nsub = tm // MLA_SUB
    return pl.pallas_call(
        _mla_kv_body,
        out_shape=(jax.ShapeDtypeStruct((b, MLA_HEADS, s, MLA_DQ), BF16),
                   jax.ShapeDtypeStruct((b, MLA_HEADS, ns * nsub, MLA_V, MLA_SUB), BF16)),
        grid=(b * ns,),
        in_specs=[pl.BlockSpec((tm, MLA_KV_RANK), lambda i: (i, MLA_Q_RANK // MLA_KV_RANK)),
                  pl.BlockSpec((1, MLA_KV_RANK), lambda i: (0, 0)),
                  pl.BlockSpec(wk.shape, lambda i: (0, 0)),
                  pl.BlockSpec(wv_t.shape, lambda i: (0, 0)),
                  pl.BlockSpec((tm, 2 * MLA_ROPE),
                               lambda i: (i, (MLA_Q_RANK + MLA_KV_RANK) // (2 * MLA_ROPE))),
                  pl.BlockSpec((tm, MLA_ROPE), lambda i: (i % ns, 0)),
                  pl.BlockSpec((tm, MLA_ROPE), lambda i: (i % ns, 0))],
        out_specs=(pl.BlockSpec((1, MLA_HEADS, tm, MLA_DQ), lambda i: (i // ns, 0, i % ns, 0)),
                   pl.BlockSpec((1, MLA_HEADS, nsub, MLA_V, MLA_SUB), lambda i: (i // ns, 0, i % ns, 0, 0))),
        compiler_params=_cparams(("parallel",)),
        name="mla_kv_proj",
    )(lat, gain.reshape(1, -1), wk, wv_t, lat, cosx, sinx)


def _mla_attn_body(qt_ref, k_ref, vt_ref, o_ref, *scr):
    i = pl.program_id(2)
    sub = MLA_SUB
    nsub = MLA_T // sub
    assert nsub == 2
    chains_all = [(h, ql) for ql in range(nsub) for h in range(MLA_HPS)]
    nch = len(chains_all)
    ring_a, ring_b, scr = scr[:nch], scr[nch:2 * nch], scr[2 * nch:]
    state = {(h, ql): scr[3 * (nsub * h + ql):3 * (nsub * h + ql) + 3]
             for h in range(MLA_HPS) for ql in range(nsub)}
    for m_scr, l_scr, acc_scr in state.values():
        m_scr[...] = jnp.full_like(m_scr, NEG)
        l_scr[...] = jnp.zeros_like(l_scr)
        acc_scr[...] = jnp.zeros_like(acc_scr)

    def issue(sk, ring, chains):
        for idx, (h, ql) in enumerate(chains_all):
            if (h, ql) in chains:
                k = k_ref[0, h, pl.ds(pl.multiple_of(sk * sub, sub), sub), :]
                ring[idx][...] = _dot(k, qt_ref[0, h, :, ql * sub:(ql + 1) * sub])
            yield

    def consume(sk, ring, chains, diag_ql=None):
        for idx, (h, ql) in enumerate(chains_all):
            if (h, ql) in chains:
                m_scr, l_scr, acc_scr = state[(h, ql)]
                s = ring[idx][...]
                if ql == diag_ql:
                    key = lax.broadcasted_iota(jnp.int32, s.shape, 0)
                    qry = lax.broadcasted_iota(jnp.int32, s.shape, 1)
                    s = jnp.where(key <= qry, s, NEG)
                m_old = m_scr[...]
                m_new = jnp.maximum(m_old, jnp.max(s, 0, keepdims=True))
                a = jnp.exp2(m_old - m_new)
                p = jnp.exp2(s - m_new)
                l_scr[...] = a * l_scr[...] + jnp.sum(p, 0, keepdims=True)
                acc_scr[...] = a * acc_scr[...] + _dot(vt_ref[0, h, sk], p.astype(BF16))
                m_scr[...] = m_new
            yield

    def interleave(*gens):
        for _ in zip(*gens):
            pass

    interleave(issue(0, ring_a, chains_all))

    def trip(j, carry):
        c0 = 2 * j
        interleave(issue(c0 + 1, ring_b, chains_all), consume(c0, ring_a, chains_all))
        interleave(issue(c0 + 2, ring_a, chains_all), consume(c0 + 1, ring_b, chains_all))
        return carry

    lax.fori_loop(0, i, trip, 0)
    upper = [c for c in chains_all if c[1] == 1]
    interleave(issue(2 * i + 1, ring_b, upper), consume(2 * i, ring_a, chains_all, diag_ql=0))
    interleave(consume(2 * i + 1, ring_b, upper, diag_ql=1))
    for (h, ql), (m_scr, l_scr, acc_scr) in state.items():
        o_t = acc_scr[...] * (1.0 / jnp.maximum(l_scr[...], 1e-30))
        o_ref[0, ql * sub:(ql + 1) * sub, h * MLA_V:(h + 1) * MLA_V] = o_t.T.astype(o_ref.dtype)


def _mla_attention(q_t, k, v_t, b, s):
    tq = MLA_T
    nq = s // tq
    hp = MLA_HPS
    nsub = tq // MLA_SUB
    sub_state = [pltpu.VMEM((1, MLA_SUB), F32), pltpu.VMEM((1, MLA_SUB), F32), pltpu.VMEM((MLA_V, MLA_SUB), F32)]
    ring = hp * nsub * [pltpu.VMEM((MLA_SUB, MLA_SUB), F32)]
    return pl.pallas_call(
        _mla_attn_body,
        out_shape=jax.ShapeDtypeStruct((b, s, MLA_HEADS * MLA_V), BF16),
        grid=(b, MLA_HEADS // hp, nq),
        in_specs=[pl.BlockSpec((1, hp, MLA_DQ, tq), lambda bb, h, i: (bb, h, 0, i)),
                  pl.BlockSpec((1, hp, s, MLA_DQ), lambda bb, h, i: (bb, h, 0, 0)),
                  pl.BlockSpec((1, hp, nq * nsub, MLA_V, MLA_SUB), lambda bb, h, i: (bb, h, 0, 0, 0))],
        out_specs=pl.BlockSpec((1, tq, hp * MLA_V), lambda bb, h, i: (bb, i, h)),
        scratch_shapes=2 * ring + hp * nsub * sub_state,
        compiler_params=_cparams(("parallel", "parallel", "arbitrary")),
        name="mla_attention",
    )(q_t, k, v_t)


def _mla_mixer(x2, b, s, w_in, q_norm, w_q_up, kv_norm, w_kv_up):
    r0 = MLA_Q_RANK + MLA_KV_RANK
    half = MLA_ROPE // 2
    w_in_ext = jnp.concatenate([w_in, w_in[:, r0 + half:r0 + MLA_ROPE], w_in[:, r0:r0 + half]], 1)
    lat = _linear(x2, w_in_ext.astype(BF16), tm=512, tn=w_in_ext.shape[1], out_dtype=F32, name="mla_in")
    wq = w_q_up.reshape(MLA_Q_RANK, MLA_HEADS, MLA_DQ)
    wr = wq[..., MLA_NOPE:]
    wq = jnp.concatenate([wq, wr[..., half:], wr[..., :half]], -1)
    wq_t = wq.reshape(MLA_Q_RANK, MLA_HEADS * MLA_QROWS).T.astype(BF16)
    wkv = w_kv_up.reshape(MLA_KV_RANK, MLA_HEADS, MLA_NOPE + MLA_V)
    wk = wkv[..., :MLA_NOPE].reshape(MLA_KV_RANK, MLA_HEADS * MLA_NOPE).astype(BF16)
    wv_t = wkv[..., MLA_NOPE:].reshape(MLA_KV_RANK, MLA_HEADS * MLA_V).T.astype(BF16)
    cosx, sinx = _rope_tables(s)
    q_t = _mla_q_proj(lat, q_norm, wq_t, cosx.T, sinx.T, b, s)
    k, v_t = _mla_kv_proj(lat, kv_norm, wk, wv_t, cosx, sinx, b, s)
    o = _mla_attention(q_t, k, v_t, b, s)
    return o.reshape(b * s, MLA_HEADS * MLA_V)


def _ffn_body(x_ref, wg_ref, wu_ref, wd_ref, lg_ref, lb_ref, o_ref, acc_scr):
    f = pl.program_id(1)
    xb = x_ref[...].astype(BF16)
    h = jax.nn.silu(_dot(xb, wg_ref[...])) * _dot(xb, wu_ref[...])
    part = _dot(h.astype(BF16), wd_ref[...])

    @pl.when(f == 0)
    def _():
        acc_scr[...] = part

    @pl.when(f > 0)
    def _():
        acc_scr[...] += part

    @pl.when(f == pl.num_programs(1) - 1)
    def _():
        o_ref[...] = _layer_norm(ALPHA * x_ref[...] + acc_scr[...], lg_ref[...], lb_ref[...])


def _ffn_ln(x, wg, wu, wd, lg, lb, *, tm=512, tf=1408):
    m, d = x.shape
    dff = wg.shape[1]
    assert dff % tf == 0
    vec = pl.BlockSpec((1, d), lambda i, f: (0, 0))
    return pl.pallas_call(
        _ffn_body,
        out_shape=jax.ShapeDtypeStruct((m, d), F32),
        grid=(m // tm, dff // tf),
        in_specs=[pl.BlockSpec((tm, d), lambda i, f: (i, 0)),
                  pl.BlockSpec((d, tf), lambda i, f: (0, f)),
                  pl.BlockSpec((d, tf), lambda i, f: (0, f)),
                  pl.BlockSpec((tf, d), lambda i, f: (f, 0)),
                  vec, vec],
        out_specs=pl.BlockSpec((tm, d), lambda i, f: (i, 0)),
        scratch_shapes=[pltpu.VMEM((tm, d), F32)],
        compiler_params=_cparams(("parallel", "arbitrary")),
        name="ffn_ln",
    )(x, wg, wu, wd, lg.reshape(1, d), lb.reshape(1, d))


def _compress_body(a_ref, pe_ref, w1_ref, w2_ref, o_ref):
    a = (a_ref[...].astype(F32) + pe_ref[...]).astype(BF16)
    h = jax.nn.gelu(_dot(a, w1_ref[...]))
    o_ref[...] = _dot(h.astype(BF16), w2_ref[...]).astype(o_ref.dtype)


def _compress(a, pe, w1, w2, *, tm=512):
    m, k = a.shape
    dh, d = w2.shape
    return pl.pallas_call(
        _compress_body,
        out_shape=jax.ShapeDtypeStruct((m, d), BF16),
        grid=(m // tm,),
        in_specs=[pl.BlockSpec((tm, k), lambda i: (i, 0)),
                  pl.BlockSpec((1, k), lambda i: (0, 0)),
                  pl.BlockSpec((k, dh), lambda i: (0, 0)),
                  pl.BlockSpec((dh, d), lambda i: (0, 0))],
        out_specs=pl.BlockSpec((tm, d), lambda i: (i, 0)),
        compiler_params=_cparams(("parallel",)),
        name="nsa_compress",
    )(a, pe.reshape(1, k), w1.reshape(k, dh).astype(BF16), w2.astype(BF16))


def _rel_bucket_np(dist):
    n = np.maximum(dist, 0)
    max_exact = REL_BUCKETS // 2
    nf = np.maximum(n, 1).astype(np.float32)
    large = max_exact + (np.log(nf / np.float32(max_exact)) / np.float32(math.log(REL_MAX_DIST / max_exact))
                         * np.float32(REL_BUCKETS - max_exact)).astype(np.int32)
    large = np.minimum(large, REL_BUCKETS - 1)
    return np.where(n < max_exact, n, large).astype(np.int32)


def _nsa_tables(rel_bias, s):
    g, hg, tq, kc = NSA_GROUPS, NSA_HG, NSA_TQ, NSA_KC
    assert tq == kc and tq % CMP_STRIDE == 0 and WINDOW == 2 * kc
    assert np.all(_rel_bucket_np(np.arange(tq // 2 - 15, s + tq)) == REL_BUCKETS - 1)
    rb = rel_bias.reshape(REL_BUCKETS, g, hg) * LOG2E

    def tile(base, step, rows, valid):
        p = tq + step * rows
        k = np.arange(p)
        k = np.where(k < p - step * (rows - 1), k, k - p)
        d = base + k
        vec = jnp.where(valid(d)[:, None, None], rb[_rel_bucket_np(d)], NEG)
        vec = vec.transpose(1, 2, 0)
        flat = jnp.tile(vec, (1, 1, rows))[..., :rows * (p - step)]
        mat = flat.reshape(g, hg, rows, p - step)[..., :tq]
        return mat.transpose(0, 2, 1, 3).reshape(g, rows, hg * tq)

    causal = lambda d: d >= 0
    c31 = jnp.broadcast_to(rb[REL_BUCKETS - 1][:, None, :, None], (g, 1, hg, tq)).reshape(g, 1, hg * tq)
    tiles = jnp.stack([tile(0, 1, kc, causal), tile(tq, 1, kc, causal),
                       tile(2 * tq, 1, kc, lambda d: d < WINDOW),
                       jnp.broadcast_to(c31, (g, kc, hg * tq))], 1)
    band = jnp.stack([tile(8 * CMP_STRIDE - CMP_LEN + 1, CMP_STRIDE, NSA_BAND, causal),
                      tile(-(CMP_LEN - 1), CMP_STRIDE, NSA_BAND, causal)], 1)
    band = jnp.where(band > NEG_TEST, band - c31[:, None], NEG)
    return tiles, band, c31


def _overlap_t(nc_pad, nb):
    n = np.arange(nc_pad)[None, :]
    jb = np.arange(nb)[:, None]
    cstart = n * CMP_STRIDE
    cend = cstart + CMP_LEN - 1
    sstart = jb * SEL_BLOCK
    ov = (cstart <= sstart + SEL_BLOCK - 1) & (cend >= sstart) & (n < nc_pad - 1)
    return jnp.asarray(ov.astype(np.float32), BF16)


def _nsa_body(qt_ref, kc_ref, vct_ref, k_ref, vt_ref, gate_ref, tiles_ref,
              band_ref, c31_ref, ovt_ref, o_ref, s_scr, sel_scr, *scr, nb):
    t = pl.program_id(2)
    ring_a, ring_b, state = scr[:NSA_HG], scr[NSA_HG:2 * NSA_HG], scr[2 * NSA_HG:]
    nkc = k_ref.shape[3]
    L = NSA_LANES
    q_t = qt_ref[0, 0, 0]
    c31 = c31_ref[0]
    ncp = kc_ref.shape[2]
    per = NSA_KC // SEL_BLOCK
    blocks_per_tile = NSA_TQ // CMP_STRIDE

    s = _dot(kc_ref[0, 0], q_t) + c31
    first = (t == 0).astype(jnp.int32)
    bs = pl.multiple_of((blocks_per_tile * t - 8) * (1 - first), 8)
    row = lax.broadcasted_iota(jnp.int32, (ncp, L), 0)
    s_scr[...] = jnp.where(row < bs + NSA_BAND, s, NEG)
    s_scr[pl.ds(bs, NSA_BAND), :] += band_ref[0, first]
    s = s_scr[...]
    m = jnp.max(s, 0, keepdims=True)
    m = jnp.where(m < NEG_TEST, 0.0, m)
    p = jnp.exp2(s - m)
    den = jnp.maximum(jnp.sum(p, 0, keepdims=True), 1e-30)
    p = p * (1.0 / den)
    o_c = _dot(vct_ref[0, 0], p.astype(BF16))

    psum = p[:, 0:NSA_TQ]
    for h in range(1, NSA_HG):
        psum = psum + p[:, h * NSA_TQ:(h + 1) * NSA_TQ]
    p1, p2, p3 = _split3(psum)
    ovt = ovt_ref[...]
    imp = _dot(ovt, p1) + _dot(ovt, p2) + _dot(ovt, p3)
    blk = lax.broadcasted_iota(jnp.int32, (nb, NSA_TQ), 0)
    lane = lax.broadcasted_iota(jnp.int32, (nb, NSA_TQ), 1)
    cur = (NSA_TQ // SEL_BLOCK) * t + lane // SEL_BLOCK
    forced = (blk == 0) | (blk == cur) | (blk == cur - 1)
    v = jnp.where(blk > cur, -FORCE, jnp.where(forced, FORCE, imp))
    blk_f = blk.astype(F32)
    sel = jnp.zeros((nb, NSA_TQ), F32)
    for _ in range(min(SEL_TOPN, nb)):
        mx = jnp.max(v, 0, keepdims=True)
        idx = jnp.min(jnp.where(v == mx, blk_f, float(nb)), 0, keepdims=True)
        hit = blk_f == idx
        sel = jnp.where(hit, 1.0, sel)
        v = jnp.where(hit, -jnp.inf, v)
    selneg = jnp.where(sel > 0.5, 0.0, NEG)
    selneg = jnp.concatenate([selneg] * NSA_HG, 1)
    for c in range(nb // per):
        sel_scr[c] = selneg[per * c:per * (c + 1), :]

    sel_scr[nkc] = jnp.zeros((per, L), F32)
    sel_scr[nkc + 1] = jnp.full((per, L), NEG, F32)

    n_sel = t + 1
    n_slots = n_sel + jnp.minimum(t, 2) + 1

    def slot_params(c):
        is_sel = c < n_sel
        is_win = jnp.logical_and(c >= n_sel, c < n_slots)
        d_win = c - n_sel
        delta = t - c
        br = is_win.astype(jnp.int32)
        kidx = jnp.where(is_sel, c, jnp.where(is_win, t - d_win, 0))
        sidx = jnp.where(is_sel, c, jnp.where(is_win, nkc, nkc + 1))
        tidx = jnp.where(is_sel, jnp.where(delta >= 2, 3, delta), jnp.where(is_win, d_win, 3))
        return br, kidx, sidx, tidx

    def issue(c, ring):
        br, kidx, sidx, tidx = slot_params(c)
        k = k_ref[0, 0, br, kidx]
        for h in range(NSA_HG):
            sl = slice(h * NSA_TQ, (h + 1) * NSA_TQ)
            rows = sel_scr[sidx, :, sl]
            add = jnp.concatenate([jnp.broadcast_to(rows[r:r + 1], (SEL_BLOCK, NSA_TQ)) for r in range(per)], 0)
            ring[h][...] = _dot(k, qt_ref[0, 0, 0, :, sl]) + add + tiles_ref[0, tidx, :, sl]
            yield

    def consume(c, ring):
        br, kidx, _, _ = slot_params(c)
        vt = vt_ref[0, 0, br, kidx]
        for h in range(NSA_HG):
            m_scr, l_scr, acc_scr = state[3 * h:3 * h + 3]
            sc = ring[h][...]
            m_old = m_scr[br]
            m_new = jnp.maximum(m_old, jnp.max(sc, 0, keepdims=True))
            a = jnp.exp2(m_old - m_new)
            pp = jnp.exp2(sc - m_new)
            l_scr[br] = a * l_scr[br] + jnp.sum(pp, 0, keepdims=True)
            acc_scr[br] = a * acc_scr[br] + _dot(vt, pp.astype(BF16))
            m_scr[br] = m_new
            yield

    def interleave(*gens):
        for _ in zip(*gens):
            pass

    for h in range(NSA_HG):
        m_scr, l_scr, acc_scr = state[3 * h:3 * h + 3]
        m_scr[...] = jnp.full_like(m_scr, NEG)
        l_scr[...] = jnp.zeros_like(l_scr)
        acc_scr[...] = jnp.zeros_like(acc_scr)

    interleave(issue(0, ring_a))

    def trip(i, carry):
        c0 = 2 * i
        interleave(issue(c0 + 1, ring_b), consume(c0, ring_a))
        interleave(issue(c0 + 2, ring_a), consume(c0 + 1, ring_b))
        return carry

    lax.fori_loop(0, (n_slots + 1) // 2, trip, 0)

    gate = gate_ref[0, 0, 0]
    outs = []
    for h in range(NSA_HG):
        sl = slice(h * NSA_TQ, (h + 1) * NSA_TQ)
        _, l_scr, acc_scr = state[3 * h:3 * h + 3]
        o_s = acc_scr[0] * (1.0 / jnp.maximum(l_scr[0], 1e-30))
        o_w = acc_scr[1] * (1.0 / jnp.maximum(l_scr[1], 1e-30))
        outs.append(gate[0:1, sl] * o_c[:, sl] + gate[1:2, sl] * o_s + gate[2:3, sl] * o_w)
    o_ref[0] = jnp.concatenate(outs, 0).T.astype(o_ref.dtype)


def _nsa_attention(q_t, kcmp, vcmp_t, k, v_t, gate, tiles, band, c31, ovt):
    b, g, nqt = q_t.shape[:3]
    ncp = kcmp.shape[2]
    nkc = k.shape[3]
    nb = ovt.shape[0]
    L = NSA_LANES
    body = functools.partial(_nsa_body, nb=nb)
    ring = NSA_HG * [pltpu.VMEM((NSA_KC, NSA_TQ), F32)]
    head_state = [pltpu.VMEM((2, 1, NSA_TQ), F32), pltpu.VMEM((2, 1, NSA_TQ), F32),
                  pltpu.VMEM((2, NSA_DV, NSA_TQ), F32)]
    grp = lambda bb, gg, t: (bb, gg, 0, 0)
    grp6 = lambda bb, gg, t: (bb, gg, 0, 0, 0, 0)
    return pl.pallas_call(
        body,
        out_shape=jax.ShapeDtypeStruct((b, nqt * NSA_TQ, g * NSA_HG * NSA_DV), BF16),
        grid=(b, g, nqt),
        in_specs=[pl.BlockSpec((1, 1, 1, NSA_DKP, L), lambda bb, gg, t: (bb, gg, t, 0, 0)),
                  pl.BlockSpec((1, 1, ncp, NSA_DKP), grp),
                  pl.BlockSpec((1, 1, NSA_DV, ncp), grp),
                  pl.BlockSpec((1, 1, 2, nkc, NSA_KC, NSA_DKP), grp6),
                  pl.BlockSpec((1, 1, 2, nkc, NSA_DV, NSA_KC), grp6),
                  pl.BlockSpec((1, 1, 1, N_BRANCH, L), lambda bb, gg, t: (bb, gg, t, 0, 0)),
                  pl.BlockSpec((1, 4, NSA_KC, L), lambda bb, gg, t: (gg, 0, 0, 0)),
                  pl.BlockSpec((1, 2, NSA_BAND, L), lambda bb, gg, t: (gg, 0, 0, 0)),
                  pl.BlockSpec((1, 1, L), lambda bb, gg, t: (gg, 0, 0)),
                  pl.BlockSpec(ovt.shape, lambda bb, gg, t: (0, 0))],
        out_specs=pl.BlockSpec((1, NSA_TQ, NSA_HG * NSA_DV), lambda bb, gg, t: (bb, t, gg)),
        scratch_shapes=[pltpu.VMEM((ncp, L), F32),
                        pltpu.VMEM((nb * SEL_BLOCK // NSA_KC + 2, NSA_KC // SEL_BLOCK, L), F32)]
        + 2 * ring + NSA_HG * head_state,
        compiler_params=_cparams(("parallel", "parallel", "arbitrary")),
        name="nsa_attention",
    )(q_t, kcmp, vcmp_t, k, v_t, gate, tiles, band, c31, ovt)


def _nsa_proj_body(x_ref, wq_ref, wk_ref, wv_ref, wg_ref, wc_ref,
                   q_ref, k_ref, v_ref, gate_ref, c_ref, *, qscale):
    g, hg, tq = NSA_GROUPS, NSA_HG, NSA_TQ
    xb = x_ref[...].astype(BF16)
    q_t = _dot_nt(wq_ref[...], xb) * qscale
    gate_t = jax.nn.sigmoid(_dot_nt(wg_ref[...], xb))
    for gg in range(g):
        for h in range(hg):
            head = gg * hg + h
            q_ref[0, gg, 0, :, h * tq:(h + 1) * tq] = q_t[head * NSA_DKP:(head + 1) * NSA_DKP].astype(BF16)
            r0 = head * NSA_GATE_ROWS
            gate_ref[0, gg, 0, :, h * tq:(h + 1) * tq] = gate_t[r0:r0 + N_BRANCH]
    k = _dot(xb, wk_ref[...]).astype(BF16)
    v_t = _dot_nt(wv_ref[...], xb).astype(BF16)
    for gg in range(g):
        for br in range(2):
            k_ref[0, gg, br, 0] = k[:, (br * g + gg) * NSA_DKP:(br * g + gg + 1) * NSA_DKP]
            v_ref[0, gg, br, 0] = v_t[(br * g + gg) * NSA_DV:(br * g + gg + 1) * NSA_DV]
    c_ref[...] = _dot(xb, wc_ref[...]).astype(BF16)


def _nsa_proj(x2, b, s, wq_t, wk, wv_t, wg_t, wc):
    g, hg, tq, L = NSA_GROUPS, NSA_HG, NSA_TQ, NSA_LANES
    nqt = s // tq
    t, d = x2.shape
    full = lambda i: (0, 0)
    tile5 = lambda i: (i // nqt, 0, i % nqt, 0, 0)
    tile6 = lambda i: (i // nqt, 0, 0, i % nqt, 0, 0)
    body = functools.partial(_nsa_proj_body, qscale=(NSA_DK ** -0.5) * LOG2E)
    return pl.pallas_call(
        body,
        out_shape=(jax.ShapeDtypeStruct((b, g, nqt, NSA_DKP, L), BF16),
                   jax.ShapeDtypeStruct((b, g, 2, nqt, NSA_KC, NSA_DKP), BF16),
                   jax.ShapeDtypeStruct((b, g, 2, nqt, NSA_DV, NSA_KC), BF16),
                   jax.ShapeDtypeStruct((b, g, nqt, N_BRANCH, L), F32),
                   jax.ShapeDtypeStruct((t, wc.shape[1]), BF16)),
        grid=(t // tq,),
        in_specs=[pl.BlockSpec((tq, d), lambda i: (i, 0)),
                  pl.BlockSpec(wq_t.shape, full), pl.BlockSpec(wk.shape, full), pl.BlockSpec(wv_t.shape, full),
                  pl.BlockSpec(wg_t.shape, full), pl.BlockSpec(wc.shape, full)],
        out_specs=(pl.BlockSpec((1, g, 1, NSA_DKP, L), tile5),
                   pl.BlockSpec((1, g, 2, 1, NSA_KC, NSA_DKP), tile6),
                   pl.BlockSpec((1, g, 2, 1, NSA_DV, NSA_KC), tile6),
                   pl.BlockSpec((1, g, 1, N_BRANCH, L), tile5),
                   pl.BlockSpec((tq, wc.shape[1]), lambda i: (i, 0))),
        compiler_params=_cparams(("parallel",)),
        name="nsa_proj",
    )(x2, wq_t, wk, wv_t, wg_t, wc)


def _nsa_mixer(x2, b, s, w_in, pe_k, w1_k, w2_k, pe_v, w1_v, w2_v, rel_bias):
    assert NSA_TQ == NSA_KC
    t, d = x2.shape
    h, g, hg, dk, dv = NSA_HEADS, NSA_GROUPS, NSA_HG, NSA_DK, NSA_DV
    nb = s // SEL_BLOCK
    sizes = [h * dk, g * dk, g * dv, g * dk, g * dv, g * dk, g * dv, h * N_BRANCH]
    c = [0] + [int(v) for v in np.cumsum(sizes)]
    cols = [w_in[:, c[i]:c[i + 1]] for i in range(len(sizes))]
    w_q, w_kc, w_vc, w_ks, w_vs, w_kw, w_vw, w_gate = cols

    def pad_last(a, n):
        return jnp.pad(a, [(0, 0)] * (a.ndim - 1) + [(0, n - a.shape[-1])])

    wq_t = pad_last(w_q.reshape(d, h, dk), NSA_DKP).reshape(d, h * NSA_DKP).T.astype(BF16)
    wk = jnp.concatenate([pad_last(w.reshape(d, g, dk), NSA_DKP).reshape(d, g * NSA_DKP)
                          for w in (w_ks, w_kw)], 1).astype(BF16)
    wv_t = jnp.concatenate([w_vs, w_vw], 1).T.astype(BF16)
    wg_t = pad_last(w_gate.reshape(d, h, N_BRANCH), NSA_GATE_ROWS).reshape(d, h * NSA_GATE_ROWS).T.astype(BF16)
    wc = jnp.concatenate([w_kc, w_vc], 1).astype(BF16)
    q_t, k, v_t, gate, ctok = _nsa_proj(x2, b, s, wq_t, wk, wv_t, wg_t, wc)
    kc_tok = ctok[:, :g * dk].reshape(b, s, g, dk)
    vc_tok = ctok[:, g * dk:].reshape(b, s, g, dv)

    nch = s // CMP_STRIDE

    def unfold(tok, dd):
        ch = tok.reshape(b, nch, CMP_STRIDE, g, dd).transpose(0, 3, 1, 2, 4).reshape(b, g, nch, CMP_STRIDE * dd)
        nxt = jnp.concatenate([ch[:, :, 1:], jnp.zeros_like(ch[:, :, :1])], 2)
        return jnp.concatenate([ch, nxt], -1).reshape(b * g * nch, CMP_LEN * dd)

    k_cmp = _compress(unfold(kc_tok, dk), pe_k, w1_k, pad_last(w2_k, NSA_DKP)).reshape(b, g, nch, NSA_DKP)
    v_cmp = _compress(unfold(vc_tok, dv), pe_v, w1_v, w2_v).reshape(b, g, nch, dv)
    vcmp_t = v_cmp.transpose(0, 1, 3, 2)

    tiles, band, c31 = _nsa_tables(rel_bias, s)
    ovt = _overlap_t(nch, nb)
    o = _nsa_attention(q_t, k_cmp, vcmp_t, k, v_t, gate, tiles, band, c31, ovt)
    return o.reshape(t, h * dv)


def _router_body(x_ref, w_ref, o_ref):
    x1, x2, x3 = _split3(x_ref[...])
    w1, w2, w3 = _split3(w_ref[...])
    acc = _dot(x1, w1)
    acc += _dot(x1, w2) + _dot(x2, w1)
    acc += _dot(x1, w3) + _dot(x2, w2) + _dot(x3, w1)
    o_ref[...] = acc


def _router(x, w, *, tm=512):
    m, d = x.shape
    wp = jnp.pad(w, ((0, 0), (0, 128 - w.shape[1])))
    return pl.pallas_call(
        _router_body,
        out_shape=jax.ShapeDtypeStruct((m, 128), F32),
        grid=(m // tm,),
        in_specs=[pl.BlockSpec((tm, d), lambda i: (i, 0)), pl.BlockSpec((d, 128), lambda i: (0, 0))],
        out_specs=pl.BlockSpec((tm, 128), lambda i: (i, 0)),
        compiler_params=_cparams(("parallel",)),
        name="moe_router",
    )(x, wp)[:, :w.shape[1]]


def _dispatch_body(ir_ref, ic_ref, fl_ref, x_ref, rt_ref, o_ref):
    i = pl.program_id(0)
    flag = fl_ref[i]
    tok = ic_ref[i] * MOE_TC + lax.broadcasted_iota(jnp.int32, (MOE_BLK, MOE_TC), 1)
    onehot = jnp.where(rt_ref[...] == tok, 1.0, 0.0).astype(BF16)
    rows = _dot(onehot, x_ref[...])

    @pl.when(flag == 3)
    def _():
        o_ref[...] = rows.astype(o_ref.dtype)

    @pl.when(flag == 1)
    def _():
        o_ref[...] = (o_ref[...].astype(F32) + rows).astype(o_ref.dtype)


def _dispatch(x_bf, row_tok_col, item_r, item_c, flags):
    t, d = x_bf.shape
    r = row_tok_col.shape[0]
    ni = item_r.shape[0]
    gs = pltpu.PrefetchScalarGridSpec(
        num_scalar_prefetch=3, grid=(ni,),
        in_specs=[pl.BlockSpec((MOE_TC, d), lambda i, ir, ic, fl: (ic[i], 0)),
                  pl.BlockSpec((MOE_BLK, 1), lambda i, ir, ic, fl: (ir[i], 0))],
        out_specs=pl.BlockSpec((MOE_BLK, d), lambda i, ir, ic, fl: (ir[i], 0)))
    return pl.pallas_call(
        _dispatch_body, grid_spec=gs,
        out_shape=jax.ShapeDtypeStruct((r, d), BF16),
        compiler_params=_cparams(("arbitrary",)),
        name="moe_dispatch",
    )(item_r, item_c, flags, x_bf, row_tok_col)


def _expert_body(be_ref, x_ref, wg_ref, wu_ref, wd_ref, rw_ref, o_ref, acc_scr):
    f = pl.program_id(1)
    xb = x_ref[...]
    h = jax.nn.silu(_dot(xb, wg_ref[0])) * _dot(xb, wu_ref[0])
    part = _dot(h.astype(BF16), wd_ref[0])

    @pl.when(f == 0)
    def _():
        acc_scr[...] = part

    @pl.when(f > 0)
    def _():
        acc_scr[...] += part

    @pl.when(f == pl.num_programs(1) - 1)
    def _():
        o_ref[...] = (acc_scr[...] * rw_ref[...]).astype(o_ref.dtype)


def _experts(xs, wg, wu, wd, row_w_col, block_expert):
    r, d = xs.shape
    nbk = r // MOE_BLK
    nf = wg.shape[2] // MOE_TF
    gs = pltpu.PrefetchScalarGridSpec(
        num_scalar_prefetch=1, grid=(nbk, nf),
        in_specs=[pl.BlockSpec((MOE_BLK, d), lambda i, f, be: (i, 0)),
                  pl.BlockSpec((1, d, MOE_TF), lambda i, f, be: (be[i], 0, f)),
                  pl.BlockSpec((1, d, MOE_TF), lambda i, f, be: (be[i], 0, f)),
                  pl.BlockSpec((1, MOE_TF, d), lambda i, f, be: (be[i], f, 0)),
                  pl.BlockSpec((MOE_BLK, 1), lambda i, f, be: (i, 0))],
        out_specs=pl.BlockSpec((MOE_BLK, d), lambda i, f, be: (i, 0)),
        scratch_shapes=[pltpu.VMEM((MOE_BLK, d), F32)])
    return pl.pallas_call(
        _expert_body, grid_spec=gs,
        out_shape=jax.ShapeDtypeStruct((r, d), BF16),
        compiler_params=_cparams(("parallel", "arbitrary")),
        name="moe_experts",
    )(block_expert, xs, wg, wu, wd, row_w_col)


def _combine_body(ir_ref, ic_ref, fl_ref, y_ref, rt_ref, x_ref, g_ref, b_ref, o_ref):
    i = pl.program_id(0)
    flag = fl_ref[i]
    tok = ic_ref[i] * MOE_TC + lax.broadcasted_iota(jnp.int32, (MOE_TC, MOE_BLK), 0)
    onehot = jnp.where(rt_ref[0] == tok, 1.0, 0.0).astype(BF16)
    part = _dot(onehot, y_ref[...])

    @pl.when((flag & 3) == 3)
    def _():
        o_ref[...] = part

    @pl.when((flag & 3) == 1)
    def _():
        o_ref[...] += part

    @pl.when((flag & 4) == 4)
    def _():
        o_ref[...] = _layer_norm(ALPHA * x_ref[...] + o_ref[...], g_ref[...], b_ref[...])


def _combine_ln(out_rows, row_tok_lane, item_r, item_c, flags, x, g, b):
    r, d = out_rows.shape
    t = x.shape[0]
    ni = item_r.shape[0]
    vec = pl.BlockSpec((1, d), lambda i, ir, ic, fl: (0, 0))
    gs = pltpu.PrefetchScalarGridSpec(
        num_scalar_prefetch=3, grid=(ni,),
        in_specs=[pl.BlockSpec((MOE_BLK, d), lambda i, ir, ic, fl: (ir[i], 0)),
                  pl.BlockSpec((1, 1, MOE_BLK), lambda i, ir, ic, fl: (ir[i], 0, 0)),
                  pl.BlockSpec((MOE_TC, d), lambda i, ir, ic, fl: (ic[i], 0)),
                  vec, vec],
        out_specs=pl.BlockSpec((MOE_TC, d), lambda i, ir, ic, fl: (ic[i], 0)))
    return pl.pallas_call(
        _combine_body, grid_spec=gs,
        out_shape=jax.ShapeDtypeStruct((t, d), F32),
        compiler_params=_cparams(("arbitrary",)),
        name="moe_combine_ln",
    )(item_r, item_c, flags, out_rows, row_tok_lane, x, g.reshape(1, d), b.reshape(1, d))


def _moe_plan(top_idx, wts, t):
    e, blk, tc = N_EXPERTS, MOE_BLK, MOE_TC
    a = t * TOP_K
    i32 = jnp.int32
    exp_flat = top_idx.reshape(a).astype(i32)
    tok_flat = jnp.arange(a, dtype=i32) // TOP_K
    _, tok_sorted, w_sorted = lax.sort((exp_flat, tok_flat, wts.reshape(a)), num_keys=1, is_stable=True)
    counts = jnp.sum((exp_flat[:, None] == jnp.arange(e, dtype=i32)[None, :]).astype(i32), 0)
    padded = ((counts + blk - 1) // blk) * blk
    grp_start = jnp.cumsum(counts) - counts
    pad_end = jnp.cumsum(padded)
    pad_start = pad_end - padded
    nbk = a // blk + e
    r = nbk * blk
    tok_ext = jnp.concatenate([tok_sorted, jnp.full((r - a,), -1, i32)])
    w_ext = jnp.concatenate([w_sorted, jnp.zeros((r - a,), F32)])
    rows = jnp.arange(r, dtype=i32)
    row_tok = jnp.full((r,), -1, i32)
    row_w = jnp.zeros((r,), F32)
    for ee in range(e):
        inside = (rows >= pad_start[ee]) & (rows < pad_start[ee] + counts[ee])
        shift = pad_start[ee] - grp_start[ee]
        row_tok = jnp.where(inside, jnp.roll(tok_ext, shift), row_tok)
        row_w = jnp.where(inside, jnp.roll(w_ext, shift), row_w)
    blk_first = jnp.arange(nbk, dtype=i32) * blk
    block_expert = jnp.minimum(jnp.sum((pad_end[None, :] <= blk_first[:, None]).astype(i32), 1), e - 1)

    rt = row_tok.reshape(nbk, blk)
    valid = rt >= 0
    t_lo = jnp.min(jnp.where(valid, rt, t), 1)
    t_hi = jnp.max(rt, 1)
    has = t_hi >= 0
    c_lo = jnp.where(has, t_lo // tc, 0)
    c_hi = jnp.where(has, t_hi // tc, 0)
    n_it = c_hi - c_lo + 1
    off_end = jnp.cumsum(n_it)
    off_start = off_end - n_it
    total = off_end[-1]
    ni = nbk + e * (t // tc)
    idx = jnp.arange(ni, dtype=i32)
    ok = idx < total
    ir = jnp.minimum(jnp.sum((off_end[None, :] <= idx[:, None]).astype(i32), 1), nbk - 1)
    ic = jnp.where(ok, c_lo[ir] + idx - off_start[ir], c_hi[nbk - 1]).astype(i32)
    first = ok & (idx == off_start[ir])
    d_flags = ok.astype(i32) + 2 * first.astype(i32)

    key = jnp.where(ok, ic * nbk + ir, jnp.iinfo(jnp.int32).max)
    perm = jnp.argsort(key)
    ok2 = ok[perm]
    last = total - 1
    cr = jnp.where(ok2, ir[perm], ir[perm][last]).astype(i32)
    cc = jnp.where(ok2, ic[perm], ic[perm][last]).astype(i32)
    first2 = ok2 & jnp.concatenate([jnp.ones((1,), bool), cc[1:] != cc[:-1]])
    last2 = ok2 & jnp.concatenate([(cc[1:] != cc[:-1]) | ~ok2[1:], jnp.ones((1,), bool)])
    c_flags = ok2.astype(i32) + 2 * first2.astype(i32) + 4 * last2.astype(i32)
    return dict(row_tok=row_tok, row_w=row_w, block_expert=block_expert,
                d_items=(ir, ic, d_flags), c_items=(cr, cc, c_flags), nbk=nbk)


def _moe_ln(x2, w_router, wg, wu, wd, ln_g, ln_b):
    t, d = x2.shape
    logits = _router(x2, w_router)
    top_val, top_idx = lax.top_k(logits, TOP_K)
    wts = jax.nn.softmax(top_val, -1)
    plan = _moe_plan(top_idx, wts, t)
    nbk = plan["nbk"]
    xs = _dispatch(x2.astype(BF16), plan["row_tok"].reshape(-1, 1), *plan["d_items"])
    out_rows = _experts(xs, wg.astype(BF16), wu.astype(BF16), wd.astype(BF16),
                        plan["row_w"].reshape(-1, 1), plan["block_expert"])
    return _combine_ln(out_rows, plan["row_tok"].reshape(nbk, 1, MOE_BLK), *plan["c_items"], x2, ln_g, ln_b)


def _forward(x, mla_w_in, mla_q_norm, mla_w_q_up, mla_kv_norm, mla_w_kv_up, mla_w_out, nsa_w_in,
             nsa_cmp_pe_k, nsa_cmp_w1_k, nsa_cmp_w2_k, nsa_cmp_pe_v, nsa_cmp_w1_v, nsa_cmp_w2_v,
             nsa_w_out, rel_bias, ffn_w_gate, ffn_w_up, ffn_w_down, moe_w_router, moe_w_gate,
             moe_w_up, moe_w_down, ln_mix_g, ln_mix_b, ln_ffn_g, ln_ffn_b):
    b, s, d = x.shape
    x2 = x.reshape(b * s, d)
    o = _mla_mixer(x2, b, s, mla_w_in[0], mla_q_norm[0], mla_w_q_up[0], mla_kv_norm[0], mla_w_kv_up[0])
    x2 = _linear(o, mla_w_out[0].astype(BF16), tm=512, tn=d, out_dtype=F32,
                 ln=(x2, ln_mix_g[0], ln_mix_b[0]), name="mla_out_ln")
    x2 = _ffn_ln(x2, ffn_w_gate[0].astype(BF16), ffn_w_up[0].astype(BF16), ffn_w_down[0].astype(BF16),
                 ln_ffn_g[0], ln_ffn_b[0])
    o = _nsa_mixer(x2, b, s, nsa_w_in[0], nsa_cmp_pe_k[0], nsa_cmp_w1_k[0], nsa_cmp_w2_k[0],
                   nsa_cmp_pe_v[0], nsa_cmp_w1_v[0], nsa_cmp_w2_v[0], rel_bias)
    x2 = _linear(o, nsa_w_out[0].astype(BF16), tm=512, tn=d, out_dtype=F32,
                 ln=(x2, ln_mix_g[1], ln_mix_b[1]), name="nsa_out_ln")
    x2 = _moe_ln(x2, moe_w_router[0], moe_w_gate[0], moe_w_up[0], moe_w_down[0], ln_ffn_g[1], ln_ffn_b[1])
    return x2.reshape(b, s, d)


@jax.jit
def kernel(x, mla_w_in, mla_q_norm, mla_w_q_up, mla_kv_norm, mla_w_kv_up, mla_w_out, nsa_w_in,
           nsa_cmp_pe_k, nsa_cmp_w1_k, nsa_cmp_w2_k, nsa_cmp_pe_v, nsa_cmp_w1_v, nsa_cmp_w2_v,
           nsa_w_out, rel_bias, ffn_w_gate, ffn_w_up, ffn_w_down, moe_w_router, moe_w_gate,
           moe_w_up, moe_w_down, ln_mix_g, ln_mix_b, ln_ffn_g, ln_ffn_b):
    return _forward(x, mla_w_in, mla_q_norm, mla_w_q_up, mla_kv_norm, mla_w_kv_up, mla_w_out, nsa_w_in,
                    nsa_cmp_pe_k, nsa_cmp_w1_k, nsa_cmp_w2_k, nsa_cmp_pe_v, nsa_cmp_w1_v, nsa_cmp_w2_v,
                    nsa_w_out, rel_bias, ffn_w_gate, ffn_w_up, ffn_w_down, moe_w_router, moe_w_gate,
                    moe_w_up, moe_w_down, ln_mix_g, ln_mix_b, ln_ffn_g, ln_ffn_b)
```

```python
import functools
import math

import numpy as np
import jax
import jax.numpy as jnp
from jax import lax
from jax.experimental import pallas as pl
from jax.experimental.pallas import tpu as pltpu

F32 = jnp.float32
BF16 = jnp.bfloat16

D_MODEL = 1024
DEPTH = 2

MLA_HEADS = 8
MLA_Q_RANK = 512
MLA_KV_RANK = 256
MLA_NOPE = 128
MLA_ROPE = 64
MLA_V = 128
ROPE_THETA = 10000.0

NSA_HEADS = 16
NSA_GROUPS = 4
NSA_HG = NSA_HEADS // NSA_GROUPS
NSA_DK = 96
NSA_DV = 64
CMP_LEN = 32
CMP_STRIDE = 16
SEL_BLOCK = 64
SEL_TOPN = 16
WINDOW = 512
N_BRANCH = 3
FORCE = 1e6

REL_BUCKETS = 32
REL_MAX_DIST = 128

D_FF = 2816
N_EXPERTS = 8
TOP_K = 2
D_FF_EXPERT = 3584

LN_EPS = 1e-5
RMS_EPS = 1e-6

ALPHA = (2.0 * DEPTH) ** 0.25

NEG = -1e30
NEG_TEST = -1e29

V7X_VMEM_LIMIT = 56 * 1024 * 1024

LOG2E = 1.4426950408889634

NSA_TQ = 256
NSA_LANES = NSA_HG * NSA_TQ
NSA_KC = 256
NSA_DKP = 128
NSA_GATE_ROWS = 8
NSA_BAND = 24
NSA_SEL_ROWS = 16
NSA_VROWS = NSA_DV + 16
NSA_HEAD_SLOTS = 4
NSA_TAIL_SLOTS = 4

MOE_BLK = 512
MOE_TC = 512
MOE_TF = 1792


def _cparams(sem, vmem=V7X_VMEM_LIMIT):
    return pltpu.CompilerParams(dimension_semantics=sem, vmem_limit_bytes=vmem)


def _layer_norm(r, g, b):
    mu = jnp.mean(r, -1, keepdims=True)
    d = r - mu
    var = jnp.mean(d * d, -1, keepdims=True)
    return d * lax.rsqrt(var + LN_EPS) * g + b


def _rms_norm(x, g):
    return x * lax.rsqrt(jnp.mean(x * x, -1, keepdims=True) + RMS_EPS) * g


def _split3(a):
    a1 = a.astype(BF16)
    r1 = a - a1.astype(F32)
    a2 = r1.astype(BF16)
    a3 = (r1 - a2.astype(F32)).astype(BF16)
    return a1, a2, a3


def _dot(a, b):
    return jnp.dot(a, b, preferred_element_type=F32)


def _dot_nt(a, b):
    return lax.dot_general(a, b, (((1,), (1,)), ((), ())), preferred_element_type=F32)


def _linear_body(*refs, has_ln):
    it = iter(refs)
    x_ref = next(it)
    w_ref = next(it)
    if has_ln:
        res_ref, lg_ref, lb_ref = next(it), next(it), next(it)
    o_ref = next(it)
    acc = _dot(x_ref[...].astype(BF16), w_ref[...])
    if has_ln:
        acc = _layer_norm(ALPHA * res_ref[...] + acc, lg_ref[...], lb_ref[...])
    o_ref[...] = acc.astype(o_ref.dtype)


def _linear(x, w, *, tm, tn, out_dtype, ln=None, name):
    m, k = x.shape
    n = w.shape[1]
    assert m % tm == 0 and n % tn == 0
    in_specs = [pl.BlockSpec((tm, k), lambda i, j: (i, 0)),
                pl.BlockSpec((k, tn), lambda i, j: (0, j))]
    args = [x, w]
    if ln is not None:
        assert tn == n
        res, lg, lb = ln
        in_specs += [pl.BlockSpec((tm, n), lambda i, j: (i, 0)),
                     pl.BlockSpec((1, n), lambda i, j: (0, 0)),
                     pl.BlockSpec((1, n), lambda i, j: (0, 0))]
        args += [res, lg.reshape(1, n), lb.reshape(1, n)]
    return pl.pallas_call(
        functools.partial(_linear_body, has_ln=ln is not None),
        out_shape=jax.ShapeDtypeStruct((m, n), out_dtype),
        grid=(m // tm, n // tn),
        in_specs=in_specs,
        out_specs=pl.BlockSpec((tm, tn), lambda i, j: (i, j)),
        compiler_params=_cparams(("parallel", "arbitrary")),
        name=name,
    )(*args)


def _rope_tables(s):
    half = MLA_ROPE // 2
    freq = ROPE_THETA ** (-jnp.arange(half, dtype=F32) / half)
    ang = jnp.arange(s).astype(F32)[:, None] * freq[None, :]
    cos, sin = jnp.cos(ang), jnp.sin(ang)
    return jnp.concatenate([cos, cos], -1), jnp.concatenate([-sin, sin], -1)


MLA_DQ = MLA_NOPE + MLA_ROPE
MLA_QROWS = MLA_NOPE + 2 * MLA_ROPE
MLA_T = 512
MLA_HPS = 2
MLA_SUB = 256


def _mla_q_body(lat_ref, g_ref, w_ref, cos_ref, sin_ref, o_ref, *, qscale):
    xn = _rms_norm(lat_ref[...], g_ref[...]).astype(BF16)
    y = _dot_nt(w_ref[...], xn)
    cos, sin = cos_ref[...], sin_ref[...]
    for h in range(MLA_HEADS):
        r0 = h * MLA_QROWS
        o_ref[0, h, :MLA_NOPE, :] = (y[r0:r0 + MLA_NOPE] * qscale).astype(BF16)
        a = y[r0 + MLA_NOPE:r0 + MLA_DQ]
        bb = y[r0 + MLA_DQ:r0 + MLA_QROWS]
        o_ref[0, h, MLA_NOPE:, :] = ((a * cos + bb * sin) * qscale).astype(BF16)


def _mla_q_proj(lat, gain, w_t, cos_t, sin_t, b, s, *, tm=MLA_T):
    ns = s // tm
    body = functools.partial(_mla_q_body, qscale=(MLA_DQ ** -0.5) * LOG2E)
    return pl.pallas_call(
        body,
        out_shape=jax.ShapeDtypeStruct((b, MLA_HEADS, MLA_DQ, s), BF16),
        grid=(b * ns,),
        in_specs=[pl.BlockSpec((tm, MLA_Q_RANK), lambda i: (i, 0)),
                  pl.BlockSpec((1, MLA_Q_RANK), lambda i: (0, 0)),
                  pl.BlockSpec(w_t.shape, lambda i: (0, 0)),
                  pl.BlockSpec((MLA_ROPE, tm), lambda i: (0, i % ns)),
                  pl.BlockSpec((MLA_ROPE, tm), lambda i: (0, i % ns))],
        out_specs=pl.BlockSpec((1, MLA_HEADS, MLA_DQ, tm), lambda i: (i // ns, 0, 0, i % ns)),
        compiler_params=_cparams(("parallel",)),
        name="mla_q_proj",
    )(lat, gain.reshape(1, -1), w_t, cos_t, sin_t)


def _mla_kv_body(lat_ref, g_ref, wk_ref, wvt_ref, kr_ref, cos_ref, sin_ref, k_ref, vt_ref):
    xn = _rms_norm(lat_ref[...], g_ref[...]).astype(BF16)
    kn = _dot(xn, wk_ref[...]).astype(BF16)
    vt = _dot_nt(wvt_ref[...], xn).astype(BF16)
    kr = kr_ref[...]
    rot = (kr[:, :MLA_ROPE] * cos_ref[...] + kr[:, MLA_ROPE:] * sin_ref[...]).astype(BF16)
    for h in range(MLA_HEADS):
        k_ref[0, h, :, :MLA_NOPE] = kn[:, h * MLA_NOPE:(h + 1) * MLA_NOPE]
        k_ref[0, h, :, MLA_NOPE:] = rot
        for kk in range(MLA_T // MLA_SUB):
            vt_ref[0, h, kk] = vt[h * MLA_V:(h + 1) * MLA_V, kk * MLA_SUB:(kk + 1) * MLA_SUB]


def _mla_kv_proj(lat, gain, wk, wv_t, cosx, sinx, b, s, *, tm=MLA_T):
    ns = s // tm
    nsub = tm // MLA_SUB
    return pl.pallas_call(
        _mla_kv_body,
        out_shape=(jax.ShapeDtypeStruct((b, MLA_HEADS, s, MLA_DQ), BF16),
                   jax.ShapeDtypeStruct((b, MLA_HEADS, ns * nsub, MLA_V, MLA_SUB), BF16)),
        grid=(b * ns,),
        in_specs=[pl.BlockSpec((tm, MLA_KV_RANK), lambda i: (i, MLA_Q_RANK // MLA_KV_RANK)),
                  pl.BlockSpec((1, MLA_KV_RANK), lambda i: (0, 0)),
                  pl.BlockSpec(wk.shape, lambda i: (0, 0)),
                  pl.BlockSpec(wv_t.shape, lambda i: (0, 0)),
                  pl.BlockSpec((tm, 2 * MLA_ROPE),
                               lambda i: (i, (MLA_Q_RANK + MLA_KV_RANK) // (2 * MLA_ROPE))),
                  pl.BlockSpec((tm, MLA_ROPE), lambda i: (i % ns, 0)),
                  pl.BlockSpec((tm, MLA_ROPE), lambda i: (i % ns, 0))],
        out_specs=(pl.BlockSpec((1, MLA_HEADS, tm, MLA_DQ), lambda i: (i // ns, 0, i % ns, 0)),
                   pl.BlockSpec((1, MLA_HEADS, nsub, MLA_V, MLA_SUB), lambda i: (i // ns, 0, i % ns, 0, 0))),
        compiler_params=_cparams(("parallel",)),
        name="mla_kv_proj",
    )(lat, gain.reshape(1, -1), wk, wv_t, lat, cosx, sinx)


def _mla_attn_body(qt_ref, k_ref, vt_ref, o_ref, *scr):
    i = pl.program_id(2)
    sub = MLA_SUB
    nsub = MLA_T // sub
    assert nsub == 2
    chains_all = [(h, ql) for ql in range(nsub) for h in range(MLA_HPS)]
    nch = len(chains_all)
    ring_a, ring_b, scr = scr[:nch], scr[nch:2 * nch], scr[2 * nch:]
    state = {(h, ql): scr[3 * (nsub * h + ql):3 * (nsub * h + ql) + 3]
             for h in range(MLA_HPS) for ql in range(nsub)}
    for m_scr, l_scr, acc_scr in state.values():
        m_scr[...] = jnp.full_like(m_scr, NEG)
        l_scr[...] = jnp.zeros_like(l_scr)
        acc_scr[...] = jnp.zeros_like(acc_scr)

    def issue(sk, ring, chains):
        for idx, (h, ql) in enumerate(chains_all):
            if (h, ql) in chains:
                k = k_ref[0, h, pl.ds(pl.multiple_of(sk * sub, sub), sub), :]
                ring[idx][...] = _dot(k, qt_ref[0, h, :, ql * sub:(ql + 1) * sub])
            yield

    def consume(sk, ring, chains, diag_ql=None):
        for idx, (h, ql) in enumerate(chains_all):
            if (h, ql) in chains:
                m_scr, l_scr, acc_scr = state[(h, ql)]
                s = ring[idx][...]
                if ql == diag_ql:
                    key = lax.broadcasted_iota(jnp.int32, s.shape, 0)
                    qry = lax.broadcasted_iota(jnp.int32, s.shape, 1)
                    s = jnp.where(key <= qry, s, NEG)
                m_old = m_scr[...]
                m_new = jnp.maximum(m_old, jnp.max(s, 0, keepdims=True))
                a = jnp.exp2(m_old - m_new)
                p = jnp.exp2(s - m_new)
                l_scr[...] = a * l_scr[...] + jnp.sum(p, 0, keepdims=True)
                acc_scr[...] = a * acc_scr[...] + _dot(vt_ref[0, h, sk], p.astype(BF16))
                m_scr[...] = m_new
            yield

    def interleave(*gens):
        for _ in zip(*gens):
            pass

    interleave(issue(0, ring_a, chains_all))

    def trip(j, carry):
        c0 = 2 * j
        interleave(issue(c0 + 1, ring_b, chains_all), consume(c0, ring_a, chains_all))
        interleave(issue(c0 + 2, ring_a, chains_all), consume(c0 + 1, ring_b, chains_all))
        return carry

    lax.fori_loop(0, i, trip, 0)
    upper = [c for c in chains_all if c[1] == 1]
    interleave(issue(2 * i + 1, ring_b, upper), consume(2 * i, ring_a, chains_all, diag_ql=0))
    interleave(consume(2 * i + 1, ring_b, upper, diag_ql=1))
    for (h, ql), (m_scr, l_scr, acc_scr) in state.items():
        o_t = acc_scr[...] * (1.0 / jnp.maximum(l_scr[...], 1e-30))
        o_ref[0, ql * sub:(ql + 1) * sub, h * MLA_V:(h + 1) * MLA_V] = o_t.T.astype(o_ref.dtype)


def _mla_attention(q_t, k, v_t, b, s):
    tq = MLA_T
    nq = s // tq
    hp = MLA_HPS
    nsub = tq // MLA_SUB
    sub_state = [pltpu.VMEM((1, MLA_SUB), F32), pltpu.VMEM((1, MLA_SUB), F32), pltpu.VMEM((MLA_V, MLA_SUB), F32)]
    ring = hp * nsub * [pltpu.VMEM((MLA_SUB, MLA_SUB), F32)]
    return pl.pallas_call(
        _mla_attn_body,
        out_shape=jax.ShapeDtypeStruct((b, s, MLA_HEADS * MLA_V), BF16),
        grid=(b, MLA_HEADS // hp, nq),
        in_specs=[pl.BlockSpec((1, hp, MLA_DQ, tq), lambda bb, h, i: (bb, h, 0, i)),
                  pl.BlockSpec((1, hp, s, MLA_DQ), lambda bb, h, i: (bb, h, 0, 0)),
                  pl.BlockSpec((1, hp, nq * nsub, MLA_V, MLA_SUB), lambda bb, h, i: (bb, h, 0, 0, 0))],
        out_specs=pl.BlockSpec((1, tq, hp * MLA_V), lambda bb, h, i: (bb, i, h)),
        scratch_shapes=2 * ring + hp * nsub * sub_state,
        compiler_params=_cparams(("parallel", "parallel", "arbitrary")),
        name="mla_attention",
    )(q_t, k, v_t)


def _mla_mixer(x2, b, s, w_in, q_norm, w_q_up, kv_norm, w_kv_up):
    r0 = MLA_Q_RANK + MLA_KV_RANK
    half = MLA_ROPE // 2
    w_in_ext = jnp.concatenate([w_in, w_in[:, r0 + half:r0 + MLA_ROPE], w_in[:, r0:r0 + half]], 1)
    lat = _linear(x2, w_in_ext.astype(BF16), tm=512, tn=w_in_ext.shape[1], out_dtype=F32, name="mla_in")
    wq = w_q_up.reshape(MLA_Q_RANK, MLA_HEADS, MLA_DQ)
    wr = wq[..., MLA_NOPE:]
    wq = jnp.concatenate([wq, wr[..., half:], wr[..., :half]], -1)
    wq_t = wq.reshape(MLA_Q_RANK, MLA_HEADS * MLA_QROWS).T.astype(BF16)
    wkv = w_kv_up.reshape(MLA_KV_RANK, MLA_HEADS, MLA_NOPE + MLA_V)
    wk = wkv[..., :MLA_NOPE].reshape(MLA_KV_RANK, MLA_HEADS * MLA_NOPE).astype(BF16)
    wv_t = wkv[..., MLA_NOPE:].reshape(MLA_KV_RANK, MLA_HEADS * MLA_V).T.astype(BF16)
    cosx, sinx = _rope_tables(s)
    q_t = _mla_q_proj(lat, q_norm, wq_t, cosx.T, sinx.T, b, s)
    k, v_t = _mla_kv_proj(lat, kv_norm, wk, wv_t, cosx, sinx, b, s)
    o = _mla_attention(q_t, k, v_t, b, s)
    return o.reshape(b * s, MLA_HEADS * MLA_V)


def _ffn_body(x_ref, wg_ref, wu_ref, wd_ref, lg_ref, lb_ref, o_ref, acc_scr):
    f = pl.program_id(1)
    xb = x_ref[...].astype(BF16)
    h = jax.nn.silu(_dot(xb, wg_ref[...])) * _dot(xb, wu_ref[...])
    part = _dot(h.astype(BF16), wd_ref[...])

    @pl.when(f == 0)
    def _():
        acc_scr[...] = part

    @pl.when(f > 0)
    def _():
        acc_scr[...] += part

    @pl.when(f == pl.num_programs(1) - 1)
    def _():
        o_ref[...] = _layer_norm(ALPHA * x_ref[...] + acc_scr[...], lg_ref[...], lb_ref[...])


def _ffn_ln(x, wg, wu, wd, lg, lb, *, tm=512, tf=1408):
    m, d = x.shape
    dff = wg.shape[1]
    assert dff % tf == 0
    vec = pl.BlockSpec((1, d), lambda i, f: (0, 0))
    return pl.pallas_call(
        _ffn_body,
        out_shape=jax.ShapeDtypeStruct((m, d), F32),
        grid=(m // tm, dff // tf),
        in_specs=[pl.BlockSpec((tm, d), lambda i, f: (i, 0)),
                  pl.BlockSpec((d, tf), lambda i, f: (0, f)),
                  pl.BlockSpec((d, tf), lambda i, f: (0, f)),
                  pl.BlockSpec((tf, d), lambda i, f: (f, 0)),
                  vec, vec],
        out_specs=pl.BlockSpec((tm, d), lambda i, f: (i, 0)),
        scratch_shapes=[pltpu.VMEM((tm, d), F32)],
        compiler_params=_cparams(("parallel", "arbitrary")),
        name="ffn_ln",
    )(x, wg, wu, wd, lg.reshape(1, d), lb.reshape(1, d))


def _compress_body(a_ref, pe_ref, w1_ref, w2_ref, o_ref):
    a = (a_ref[...].astype(F32) + pe_ref[...]).astype(BF16)
    h = jax.nn.gelu(_dot(a, w1_ref[...]))
    o_ref[...] = _dot(h.astype(BF16), w2_ref[...]).astype(o_ref.dtype)


def _compress(a, pe, w1, w2, *, tm=512):
    m, k = a.shape
    dh, d = w2.shape
    return pl.pallas_call(
        _compress_body,
        out_shape=jax.ShapeDtypeStruct((m, d), BF16),
        grid=(m // tm,),
        in_specs=[pl.BlockSpec((tm, k), lambda i: (i, 0)),
                  pl.BlockSpec((1, k), lambda i: (0, 0)),
                  pl.BlockSpec((k, dh), lambda i: (0, 0)),
                  pl.BlockSpec((dh, d), lambda i: (0, 0))],
        out_specs=pl.BlockSpec((tm, d), lambda i: (i, 0)),
        compiler_params=_cparams(("parallel",)),
        name="nsa_compress",
    )(a, pe.reshape(1, k), w1.reshape(k, dh).astype(BF16), w2.astype(BF16))


def _rel_bucket_np(dist):
    n = np.maximum(dist, 0)
    max_exact = REL_BUCKETS // 2
    nf = np.maximum(n, 1).astype(np.float32)
    large = max_exact + (np.log(nf / np.float32(max_exact)) / np.float32(math.log(REL_MAX_DIST / max_exact))
                         * np.float32(REL_BUCKETS - max_exact)).astype(np.int32)
    large = np.minimum(large, REL_BUCKETS - 1)
    return np.where(n < max_exact, n, large).astype(np.int32)


def _nsa_tables(rel_bias, s):
    g, hg, tq, kc = NSA_GROUPS, NSA_HG, NSA_TQ, NSA_KC
    assert tq == kc and tq % CMP_STRIDE == 0 and WINDOW == 2 * kc
    assert np.all(_rel_bucket_np(np.arange(tq // 2 - 15, s + tq)) == REL_BUCKETS - 1)
    rb = rel_bias.reshape(REL_BUCKETS, g, hg) * LOG2E

    def tile(base, step, rows, valid):
        p = tq + step * rows
        k = np.arange(p)
        k = np.where(k < p - step * (rows - 1), k, k - p)
        d = base + k
        vec = jnp.where(valid(d)[:, None, None], rb[_rel_bucket_np(d)], NEG)
        vec = vec.transpose(1, 2, 0)
        flat = jnp.tile(vec, (1, 1, rows))[..., :rows * (p - step)]
        mat = flat.reshape(g, hg, rows, p - step)[..., :tq]
        return mat.transpose(0, 2, 1, 3).reshape(g, rows, hg * tq)

    causal = lambda d: d >= 0
    far = jnp.broadcast_to(rb[REL_BUCKETS - 1][:, None, :, None], (g, 1, hg, tq)).reshape(g, 1, hg * tq)
    rel = lambda x: jnp.where(x > NEG_TEST, x - far, NEG)
    tiles = jnp.stack([rel(tile(0, 1, kc, causal)), rel(tile(tq, 1, kc, causal)),
                       rel(tile(2 * tq, 1, kc, lambda d: d < WINDOW)),
                       jnp.zeros((g, kc, hg * tq), F32)], 1)
    band = jnp.stack([rel(tile(8 * CMP_STRIDE - CMP_LEN + 1, CMP_STRIDE, NSA_BAND, causal)),
                      rel(tile(-(CMP_LEN - 1), CMP_STRIDE, NSA_BAND, causal))], 1)
    return tiles, band


def _overlap_t(nc_pad, nb):
    n = np.arange(nc_pad)[None, :]
    jb = np.arange(nb)[:, None]
    cstart = n * CMP_STRIDE
    cend = cstart + CMP_LEN - 1
    sstart = jb * SEL_BLOCK
    ov = (cstart <= sstart + SEL_BLOCK - 1) & (cend >= sstart) & (n < nc_pad - 1)
    return jnp.asarray(ov.astype(np.float32), BF16)


def _nsa_body(qt_ref, kc_ref, vct_ref, k_ref, vt_ref, gate_ref, tiles_ref,
              band_ref, ovt_ref, o_ref, s_scr, sel_scr, qa_scr, *scr, nb):
    t = pl.program_id(2)
    ring_a, ring_b, state = scr[:NSA_HG], scr[NSA_HG:2 * NSA_HG], scr[2 * NSA_HG:]
    nkc = k_ref.shape[3]
    L = NSA_LANES
    q_t = qt_ref[0, 0, 0]
    ncp = kc_ref.shape[2]
    per = NSA_KC // SEL_BLOCK
    blocks_per_tile = NSA_TQ // CMP_STRIDE

    n_slab = nb // NSA_SEL_ROWS

    n_win = jnp.minimum(t, 2) + 1
    n_slots = n_win + t + 1

    def slot_params(c):
        is_win = c < n_win
        j = c - n_win
        is_sel = jnp.logical_and(c >= n_win, j <= t)
        delta = t - j
        br = is_win.astype(jnp.int32)
        kidx = jnp.where(is_win, t - c, jnp.where(is_sel, j, 0))
        sidx = jnp.where(is_sel, j // (NSA_SEL_ROWS // per), jnp.where(is_win, 0, n_slab))
        tidx = jnp.where(is_win, c, jnp.where(jnp.logical_and(is_sel, delta < 2), delta, 3))
        return br, kidx, sidx, tidx

    def issue(c, ring, set_rows=True):
        br, kidx, sidx, _ = slot_params(c)
        k = k_ref[0, 0, br, kidx]
        if set_rows:
            qa_scr[NSA_DK:NSA_DK + NSA_SEL_ROWS, :] = sel_scr[sidx]
        for h in range(NSA_HG):
            ring[h][...] = _dot(k, qa_scr[:, h * NSA_TQ:(h + 1) * NSA_TQ])
            yield

    def consume(c, ring, near):
        br, kidx, _, tidx = slot_params(c)
        vt = vt_ref[0, 0, br, kidx]
        for h in range(NSA_HG):
            m_scr, acc_scr = state[2 * h:2 * h + 2]
            sc = ring[h][...]
            if near:
                sc = sc + tiles_ref[0, tidx, :, h * NSA_TQ:(h + 1) * NSA_TQ]
            m_old = m_scr[br]
            m_new = jnp.maximum(m_old, jnp.max(sc, 0, keepdims=True))
            pp = jnp.exp2(sc - m_new)
            acc_scr[br] = jnp.exp2(m_old - m_new) * acc_scr[br] + _dot(vt, pp.astype(BF16))
            m_scr[br] = m_new
            yield

    def interleave(*gens):
        for _ in zip(*gens):
            pass

    for h in range(NSA_HG):
        m_scr, acc_scr = state[2 * h:2 * h + 2]
        m_scr[...] = jnp.full_like(m_scr, NEG)
        acc_scr[...] = jnp.zeros_like(acc_scr)
    qa_scr[...] = q_t

    s = _dot(kc_ref[0, 0], q_t)
    first = (t == 0).astype(jnp.int32)
    bs = pl.multiple_of((blocks_per_tile * t - 8) * (1 - first), 8)
    row = lax.broadcasted_iota(jnp.int32, (ncp, L), 0)
    s_scr[...] = jnp.where(row < bs + NSA_BAND, s, NEG)
    s_scr[pl.ds(bs, NSA_BAND), :] += band_ref[0, first]
    s = s_scr[...]
    m = jnp.max(s, 0, keepdims=True)
    m = jnp.where(m < NEG_TEST, 0.0, m)
    p = jnp.exp2(s - m)
    den = jnp.maximum(jnp.sum(p, 0, keepdims=True), 1e-30)
    p = p * (1.0 / den)
    o_c = _dot(vct_ref[0, 0], p.astype(BF16))

    psum = p[:, 0:NSA_TQ]
    for h in range(1, NSA_HG):
        psum = psum + p[:, h * NSA_TQ:(h + 1) * NSA_TQ]
    p1, p2, p3 = _split3(psum)
    ovt = ovt_ref[...]
    imp = _dot(ovt, p1) + _dot(ovt, p2) + _dot(ovt, p3)
    interleave(issue(0, ring_a, set_rows=False))
    blk = lax.broadcasted_iota(jnp.int32, (nb, NSA_TQ), 0)
    lane = lax.broadcasted_iota(jnp.int32, (nb, NSA_TQ), 1)
    cur = (NSA_TQ // SEL_BLOCK) * t + lane // SEL_BLOCK
    forced = (blk == 0) | (blk == cur) | (blk == cur - 1)
    v = jnp.where(blk > cur, -FORCE, jnp.where(forced, FORCE, imp))
    blk_f = blk.astype(F32)
    sel = jnp.zeros((nb, NSA_TQ), F32)
    for _ in range(min(SEL_TOPN, nb)):
        mx = jnp.max(v, 0, keepdims=True)
        idx = jnp.min(jnp.where(v == mx, blk_f, float(nb)), 0, keepdims=True)
        hit = blk_f == idx
        sel = jnp.where(hit, 1.0, sel)
        v = jnp.where(hit, -jnp.inf, v)
    selneg = jnp.where(sel > 0.5, 0.0, NEG)
    selneg = jnp.concatenate([selneg] * NSA_HG, 1).astype(BF16)
    for u in range(n_slab):
        sel_scr[u] = selneg[NSA_SEL_ROWS * u:NSA_SEL_ROWS * (u + 1), :]
    sel_scr[n_slab] = jnp.full((NSA_SEL_ROWS, L), NEG, BF16)

    def make_trip(first_slot, near):
        def trip(i, carry):
            c0 = first_slot + 2 * i
            interleave(issue(c0 + 1, ring_b), consume(c0, ring_a, near))
            interleave(issue(c0 + 2, ring_a), consume(c0 + 1, ring_b, near))
            return carry
        return trip

    far_trips = jnp.maximum(n_slots - 2 - NSA_HEAD_SLOTS, 0) // 2
    lax.fori_loop(0, NSA_HEAD_SLOTS // 2, make_trip(0, True), 0)
    lax.fori_loop(0, far_trips, make_trip(NSA_HEAD_SLOTS, False), 0)
    lax.fori_loop(0, NSA_TAIL_SLOTS // 2, make_trip(NSA_HEAD_SLOTS + 2 * far_trips, True), 0)

    gate = gate_ref[0, 0, 0]
    outs = []
    for h in range(NSA_HG):
        sl = slice(h * NSA_TQ, (h + 1) * NSA_TQ)
        acc_scr = state[2 * h + 1]
        o_s = acc_scr[0, :NSA_DV] * (1.0 / jnp.maximum(acc_scr[0, NSA_DV:NSA_DV + 1], 1e-30))
        o_w = acc_scr[1, :NSA_DV] * (1.0 / jnp.maximum(acc_scr[1, NSA_DV:NSA_DV + 1], 1e-30))
        outs.append(gate[0:1, sl] * o_c[:, sl] + gate[1:2, sl] * o_s + gate[2:3, sl] * o_w)
    o_ref[0] = jnp.concatenate(outs, 0).T.astype(o_ref.dtype)


def _nsa_attention(q_t, kcmp, vcmp_t, k, v_t, gate, tiles, band, ovt):
    b, g, nqt = q_t.shape[:3]
    ncp = kcmp.shape[2]
    nkc = k.shape[3]
    nb = ovt.shape[0]
    L = NSA_LANES
    body = functools.partial(_nsa_body, nb=nb)
    ring = NSA_HG * [pltpu.VMEM((NSA_KC, NSA_TQ), F32)]
    head_state = [pltpu.VMEM((2, 1, NSA_TQ), F32), pltpu.VMEM((2, NSA_VROWS, NSA_TQ), F32)]
    grp = lambda bb, gg, t: (bb, gg, 0, 0)
    grp6 = lambda bb, gg, t: (bb, gg, 0, 0, 0, 0)
    return pl.pallas_call(
        body,
        out_shape=jax.ShapeDtypeStruct((b, nqt * NSA_TQ, g * NSA_HG * NSA_DV), BF16),
        grid=(b, g, nqt),
        in_specs=[pl.BlockSpec((1, 1, 1, NSA_DKP, L), lambda bb, gg, t: (bb, gg, t, 0, 0)),
                  pl.BlockSpec((1, 1, ncp, NSA_DKP), grp),
                  pl.BlockSpec((1, 1, NSA_DV, ncp), grp),
                  pl.BlockSpec((1, 1, 2, nkc, NSA_KC, NSA_DKP), grp6),
                  pl.BlockSpec((1, 1, 2, nkc, NSA_VROWS, NSA_KC), grp6),
                  pl.BlockSpec((1, 1, 1, N_BRANCH, L), lambda bb, gg, t: (bb, gg, t, 0, 0)),
                  pl.BlockSpec((1, 4, NSA_KC, L), lambda bb, gg, t: (gg, 0, 0, 0)),
                  pl.BlockSpec((1, 2, NSA_BAND, L), lambda bb, gg, t: (gg, 0, 0, 0)),
                  pl.BlockSpec(ovt.shape, lambda bb, gg, t: (0, 0))],
        out_specs=pl.BlockSpec((1, NSA_TQ, NSA_HG * NSA_DV), lambda bb, gg, t: (bb, t, gg)),
        scratch_shapes=[pltpu.VMEM((ncp, L), F32),
                        pltpu.VMEM((nb // NSA_SEL_ROWS + 1, NSA_SEL_ROWS, L), BF16),
                        pltpu.VMEM((NSA_DKP, L), BF16)]
        + 2 * ring + NSA_HG * head_state,
        compiler_params=_cparams(("parallel", "parallel", "arbitrary")),
        name="nsa_attention",
    )(q_t, kcmp, vcmp_t, k, v_t, gate, tiles, band, ovt)


def _nsa_proj_body(x_ref, wq_ref, wk_ref, wv_ref, wg_ref, wc_ref,
                   q_ref, k_ref, v_ref, gate_ref, c_ref, *, qscale):
    g, hg, tq = NSA_GROUPS, NSA_HG, NSA_TQ
    xb = x_ref[...].astype(BF16)
    q_t = _dot_nt(wq_ref[...], xb) * qscale
    gate_t = jax.nn.sigmoid(_dot_nt(wg_ref[...], xb))
    for gg in range(g):
        for h in range(hg):
            head = gg * hg + h
            q_ref[0, gg, 0, :, h * tq:(h + 1) * tq] = q_t[head * NSA_DKP:(head + 1) * NSA_DKP].astype(BF16)
            r0 = head * NSA_GATE_ROWS
            gate_ref[0, gg, 0, :, h * tq:(h + 1) * tq] = gate_t[r0:r0 + N_BRANCH]
    k = _dot(xb, wk_ref[...])
    v_t = _dot_nt(wv_ref[...], xb).astype(BF16)
    per = tq // SEL_BLOCK
    chunk = pl.program_id(0) % (NSA_SEL_ROWS // per)
    row = lax.broadcasted_iota(jnp.int32, (tq, NSA_DKP), 0)
    col = lax.broadcasted_iota(jnp.int32, (tq, NSA_DKP), 1)
    blk_flag = jnp.where(col - NSA_DK == per * chunk + row // SEL_BLOCK, 1.0, 0.0)
    ones_rows = jnp.where(lax.broadcasted_iota(jnp.int32, (NSA_VROWS - NSA_DV, tq), 0) == 0, 1.0, 0.0).astype(BF16)
    for gg in range(g):
        for br in range(2):
            kb = k[:, (br * g + gg) * NSA_DKP:(br * g + gg + 1) * NSA_DKP]
            if br == 0:
                kb = kb + blk_flag
            k_ref[0, gg, br, 0] = kb.astype(BF16)
            v_ref[0, gg, br, 0, :NSA_DV, :] = v_t[(br * g + gg) * NSA_DV:(br * g + gg + 1) * NSA_DV]
            v_ref[0, gg, br, 0, NSA_DV:, :] = ones_rows
    c_ref[...] = _dot(xb, wc_ref[...]).astype(BF16)


def _nsa_proj(x2, b, s, wq_t, wk, wv_t, wg_t, wc):
    g, hg, tq, L = NSA_GROUPS, NSA_HG, NSA_TQ, NSA_LANES
    nqt = s // tq
    assert nqt % (NSA_SEL_ROWS * SEL_BLOCK // tq) == 0 and (s // SEL_BLOCK) % NSA_SEL_ROWS == 0
    t, d = x2.shape
    full = lambda i: (0, 0)
    tile5 = lambda i: (i // nqt, 0, i % nqt, 0, 0)
    tile6 = lambda i: (i // nqt, 0, 0, i % nqt, 0, 0)
    body = functools.partial(_nsa_proj_body, qscale=(NSA_DK ** -0.5) * LOG2E)
    return pl.pallas_call(
        body,
        out_shape=(jax.ShapeDtypeStruct((b, g, nqt, NSA_DKP, L), BF16),
                   jax.ShapeDtypeStruct((b, g, 2, nqt, NSA_KC, NSA_DKP), BF16),
                   jax.ShapeDtypeStruct((b, g, 2, nqt, NSA_VROWS, NSA_KC), BF16),
                   jax.ShapeDtypeStruct((b, g, nqt, N_BRANCH, L), F32),
                   jax.ShapeDtypeStruct((t, wc.shape[1]), BF16)),
        grid=(t // tq,),
        in_specs=[pl.BlockSpec((tq, d), lambda i: (i, 0)),
                  pl.BlockSpec(wq_t.shape, full), pl.BlockSpec(wk.shape, full), pl.BlockSpec(wv_t.shape, full),
                  pl.BlockSpec(wg_t.shape, full), pl.BlockSpec(wc.shape, full)],
        out_specs=(pl.BlockSpec((1, g, 1, NSA_DKP, L), tile5),
                   pl.BlockSpec((1, g, 2, 1, NSA_KC, NSA_DKP), tile6),
                   pl.BlockSpec((1, g, 2, 1, NSA_VROWS, NSA_KC), tile6),
                   pl.BlockSpec((1, g, 1, N_BRANCH, L), tile5),
                   pl.BlockSpec((tq, wc.shape[1]), lambda i: (i, 0))),
        compiler_params=_cparams(("parallel",)),
        name="nsa_proj",
    )(x2, wq_t, wk, wv_t, wg_t, wc)


def _nsa_mixer(x2, b, s, w_in, pe_k, w1_k, w2_k, pe_v, w1_v, w2_v, rel_bias):
    assert NSA_TQ == NSA_KC
    t, d = x2.shape
    h, g, hg, dk, dv = NSA_HEADS, NSA_GROUPS, NSA_HG, NSA_DK, NSA_DV
    nb = s // SEL_BLOCK
    sizes = [h * dk, g * dk, g * dv, g * dk, g * dv, g * dk, g * dv, h * N_BRANCH]
    c = [0] + [int(v) for v in np.cumsum(sizes)]
    cols = [w_in[:, c[i]:c[i + 1]] for i in range(len(sizes))]
    w_q, w_kc, w_vc, w_ks, w_vs, w_kw, w_vw, w_gate = cols

    def pad_last(a, n):
        return jnp.pad(a, [(0, 0)] * (a.ndim - 1) + [(0, n - a.shape[-1])])

    wq_t = pad_last(w_q.reshape(d, h, dk), NSA_DKP).reshape(d, h * NSA_DKP).T.astype(BF16)
    wk = jnp.concatenate([pad_last(w.reshape(d, g, dk), NSA_DKP).reshape(d, g * NSA_DKP)
                          for w in (w_ks, w_kw)], 1).astype(BF16)
    wv_t = jnp.concatenate([w_vs, w_vw], 1).T.astype(BF16)
    wg_t = pad_last(w_gate.reshape(d, h, N_BRANCH), NSA_GATE_ROWS).reshape(d, h * NSA_GATE_ROWS).T.astype(BF16)
    wc = jnp.concatenate([w_kc, w_vc], 1).astype(BF16)
    q_t, k, v_t, gate, ctok = _nsa_proj(x2, b, s, wq_t, wk, wv_t, wg_t, wc)
    kc_tok = ctok[:, :g * dk].reshape(b, s, g, dk)
    vc_tok = ctok[:, g * dk:].reshape(b, s, g, dv)

    nch = s // CMP_STRIDE

    def unfold(tok, dd):
        ch = tok.reshape(b, nch, CMP_STRIDE, g, dd).transpose(0, 3, 1, 2, 4).reshape(b, g, nch, CMP_STRIDE * dd)
        nxt = jnp.concatenate([ch[:, :, 1:], jnp.zeros_like(ch[:, :, :1])], 2)
        return jnp.concatenate([ch, nxt], -1).reshape(b * g * nch, CMP_LEN * dd)

    k_cmp = _compress(unfold(kc_tok, dk), pe_k, w1_k, pad_last(w2_k, NSA_DKP)).reshape(b, g, nch, NSA_DKP)
    v_cmp = _compress(unfold(vc_tok, dv), pe_v, w1_v, w2_v).reshape(b, g, nch, dv)
    vcmp_t = v_cmp.transpose(0, 1, 3, 2)

    tiles, band = _nsa_tables(rel_bias, s)
    ovt = _overlap_t(nch, nb)
    o = _nsa_attention(q_t, k_cmp, vcmp_t, k, v_t, gate, tiles, band, ovt)
    return o.reshape(t, h * dv)


def _router_body(x_ref, w_ref, o_ref):
    x1, x2, x3 = _split3(x_ref[...])
    w1, w2, w3 = _split3(w_ref[...])
    acc = _dot(x1, w1)
    acc += _dot(x1, w2) + _dot(x2, w1)
    acc += _dot(x1, w3) + _dot(x2, w2) + _dot(x3, w1)
    o_ref[...] = acc


def _router(x, w, *, tm=512):
    m, d = x.shape
    wp = jnp.pad(w, ((0, 0), (0, 128 - w.shape[1])))
    return pl.pallas_call(
        _router_body,
        out_shape=jax.ShapeDtypeStruct((m, 128), F32),
        grid=(m // tm,),
        in_specs=[pl.BlockSpec((tm, d), lambda i: (i, 0)), pl.BlockSpec((d, 128), lambda i: (0, 0))],
        out_specs=pl.BlockSpec((tm, 128), lambda i: (i, 0)),
        compiler_params=_cparams(("parallel",)),
        name="moe_router",
    )(x, wp)[:, :w.shape[1]]


def _dispatch_body(ir_ref, ic_ref, fl_ref, x_ref, rt_ref, o_ref):
    i = pl.program_id(0)
    flag = fl_ref[i]
    tok = ic_ref[i] * MOE_TC + lax.broadcasted_iota(jnp.int32, (MOE_BLK, MOE_TC), 1)
    onehot = jnp.where(rt_ref[...] == tok, 1.0, 0.0).astype(BF16)
    rows = _dot(onehot, x_ref[...])

    @pl.when(flag == 3)
    def _():
        o_ref[...] = rows.astype(o_ref.dtype)

    @pl.when(flag == 1)
    def _():
        o_ref[...] = (o_ref[...].astype(F32) + rows).astype(o_ref.dtype)


def _dispatch(x_bf, row_tok_col, item_r, item_c, flags):
    t, d = x_bf.shape
    r = row_tok_col.shape[0]
    ni = item_r.shape[0]
    gs = pltpu.PrefetchScalarGridSpec(
        num_scalar_prefetch=3, grid=(ni,),
        in_specs=[pl.BlockSpec((MOE_TC, d), lambda i, ir, ic, fl: (ic[i], 0)),
                  pl.BlockSpec((MOE_BLK, 1), lambda i, ir, ic, fl: (ir[i], 0))],
        out_specs=pl.BlockSpec((MOE_BLK, d), lambda i, ir, ic, fl: (ir[i], 0)))
    return pl.pallas_call(
        _dispatch_body, grid_spec=gs,
        out_shape=jax.ShapeDtypeStruct((r, d), BF16),
        compiler_params=_cparams(("arbitrary",)),
        name="moe_dispatch",
    )(item_r, item_c, flags, x_bf, row_tok_col)


def _expert_body(be_ref, x_ref, wg_ref, wu_ref, wd_ref, rw_ref, o_ref, acc_scr):
    f = pl.program_id(1)
    xb = x_ref[...]
    h = jax.nn.silu(_dot(xb, wg_ref[0])) * _dot(xb, wu_ref[0])
    part = _dot(h.astype(BF16), wd_ref[0])

    @pl.when(f == 0)
    def _():
        acc_scr[...] = part

    @pl.when(f > 0)
    def _():
        acc_scr[...] += part

    @pl.when(f == pl.num_programs(1) - 1)
    def _():
        o_ref[...] = (acc_scr[...] * rw_ref[...]).astype(o_ref.dtype)


def _experts(xs, wg, wu, wd, row_w_col, block_expert):
    r, d = xs.shape
    nbk = r // MOE_BLK
    nf = wg.shape[2] // MOE_TF
    gs = pltpu.PrefetchScalarGridSpec(
        num_scalar_prefetch=1, grid=(nbk, nf),
        in_specs=[pl.BlockSpec((MOE_BLK, d), lambda i, f, be: (i, 0)),
                  pl.BlockSpec((1, d, MOE_TF), lambda i, f, be: (be[i], 0, f)),
                  pl.BlockSpec((1, d, MOE_TF), lambda i, f, be: (be[i], 0, f)),
                  pl.BlockSpec((1, MOE_TF, d), lambda i, f, be: (be[i], f, 0)),
                  pl.BlockSpec((MOE_BLK, 1), lambda i, f, be: (i, 0))],
        out_specs=pl.BlockSpec((MOE_BLK, d), lambda i, f, be: (i, 0)),
        scratch_shapes=[pltpu.VMEM((MOE_BLK, d), F32)])
    return pl.pallas_call(
        _expert_body, grid_spec=gs,
        out_shape=jax.ShapeDtypeStruct((r, d), BF16),
        compiler_params=_cparams(("parallel", "arbitrary")),
        name="moe_experts",
    )(block_expert, xs, wg, wu, wd, row_w_col)


def _combine_body(ir_ref, ic_ref, fl_ref, y_ref, rt_ref, x_ref, g_ref, b_ref, o_ref):
    i = pl.program_id(0)
    flag = fl_ref[i]
    tok = ic_ref[i] * MOE_TC + lax.broadcasted_iota(jnp.int32, (MOE_TC, MOE_BLK), 0)
    onehot = jnp.where(rt_ref[0] == tok, 1.0, 0.0).astype(BF16)
    part = _dot(onehot, y_ref[...])

    @pl.when((flag & 3) == 3)
    def _():
        o_ref[...] = part

    @pl.when((flag & 3) == 1)
    def _():
        o_ref[...] += part

    @pl.when((flag & 4) == 4)
    def _():
        o_ref[...] = _layer_norm(ALPHA * x_ref[...] + o_ref[...], g_ref[...], b_ref[...])


def _combine_ln(out_rows, row_tok_lane, item_r, item_c, flags, x, g, b):
    r, d = out_rows.shape
    t = x.shape[0]
    ni = item_r.shape[0]
    vec = pl.BlockSpec((1, d), lambda i, ir, ic, fl: (0, 0))
    gs = pltpu.PrefetchScalarGridSpec(
        num_scalar_prefetch=3, grid=(ni,),
        in_specs=[pl.BlockSpec((MOE_BLK, d), lambda i, ir, ic, fl: (ir[i], 0)),
                  pl.BlockSpec((1, 1, MOE_BLK), lambda i, ir, ic, fl: (ir[i], 0, 0)),
                  pl.BlockSpec((MOE_TC, d), lambda i, ir, ic, fl: (ic[i], 0)),
                  vec, vec],
        out_specs=pl.BlockSpec((MOE_TC, d), lambda i, ir, ic, fl: (ic[i], 0)))
    return pl.pallas_call(
        _combine_body, grid_spec=gs,
        out_shape=jax.ShapeDtypeStruct((t, d), F32),
        compiler_params=_cparams(("arbitrary",)),
        name="moe_combine_ln",
    )(item_r, item_c, flags, out_rows, row_tok_lane, x, g.reshape(1, d), b.reshape(1, d))


def _moe_plan(top_idx, wts, t):
    e, blk, tc = N_EXPERTS, MOE_BLK, MOE_TC
    a = t * TOP_K
    i32 = jnp.int32
    exp_flat = top_idx.reshape(a).astype(i32)
    tok_flat = jnp.arange(a, dtype=i32) // TOP_K
    _, tok_sorted, w_sorted = lax.sort((exp_flat, tok_flat, wts.reshape(a)), num_keys=1, is_stable=True)
    counts = jnp.sum((exp_flat[:, None] == jnp.arange(e, dtype=i32)[None, :]).astype(i32), 0)
    padded = ((counts + blk - 1) // blk) * blk
    grp_start = jnp.cumsum(counts) - counts
    pad_end = jnp.cumsum(padded)
    pad_start = pad_end - padded
    nbk = a // blk + e
    r = nbk * blk
    tok_ext = jnp.concatenate([tok_sorted, jnp.full((r - a,), -1, i32)])
    w_ext = jnp.concatenate([w_sorted, jnp.zeros((r - a,), F32)])
    rows = jnp.arange(r, dtype=i32)
    row_tok = jnp.full((r,), -1, i32)
    row_w = jnp.zeros((r,), F32)
    for ee in range(e):
        inside = (rows >= pad_start[ee]) & (rows < pad_start[ee] + counts[ee])
        shift = pad_start[ee] - grp_start[ee]
        row_tok = jnp.where(inside, jnp.roll(tok_ext, shift), row_tok)
        row_w = jnp.where(inside, jnp.roll(w_ext, shift), row_w)
    blk_first = jnp.arange(nbk, dtype=i32) * blk
    block_expert = jnp.minimum(jnp.sum((pad_end[None, :] <= blk_first[:, None]).astype(i32), 1), e - 1)

    rt = row_tok.reshape(nbk, blk)
    valid = rt >= 0
    t_lo = jnp.min(jnp.where(valid, rt, t), 1)
    t_hi = jnp.max(rt, 1)
    has = t_hi >= 0
    c_lo = jnp.where(has, t_lo // tc, 0)
    c_hi = jnp.where(has, t_hi // tc, 0)
    n_it = c_hi - c_lo + 1
    off_end = jnp.cumsum(n_it)
    off_start = off_end - n_it
    total = off_end[-1]
    ni = nbk + e * (t // tc)
    idx = jnp.arange(ni, dtype=i32)
    ok = idx < total
    ir = jnp.minimum(jnp.sum((off_end[None, :] <= idx[:, None]).astype(i32), 1), nbk - 1)
    ic = jnp.where(ok, c_lo[ir] + idx - off_start[ir], c_hi[nbk - 1]).astype(i32)
    first = ok & (idx == off_start[ir])
    d_flags = ok.astype(i32) + 2 * first.astype(i32)

    key = jnp.where(ok, ic * nbk + ir, jnp.iinfo(jnp.int32).max)
    perm = jnp.argsort(key)
    ok2 = ok[perm]
    last = total - 1
    cr = jnp.where(ok2, ir[perm], ir[perm][last]).astype(i32)
    cc = jnp.where(ok2, ic[perm], ic[perm][last]).astype(i32)
    first2 = ok2 & jnp.concatenate([jnp.ones((1,), bool), cc[1:] != cc[:-1]])
    last2 = ok2 & jnp.concatenate([(cc[1:] != cc[:-1]) | ~ok2[1:], jnp.ones((1,), bool)])
    c_flags = ok2.astype(i32) + 2 * first2.astype(i32) + 4 * last2.astype(i32)
    return dict(row_tok=row_tok, row_w=row_w, block_expert=block_expert,
                d_items=(ir, ic, d_flags), c_items=(cr, cc, c_flags), nbk=nbk)


def _moe_ln(x2, w_router, wg, wu, wd, ln_g, ln_b):
    t, d = x2.shape
    logits = _router(x2, w_router)
    top_val, top_idx = lax.top_k(logits, TOP_K)
    wts = jax.nn.softmax(top_val, -1)
    plan = _moe_plan(top_idx, wts, t)
    nbk = plan["nbk"]
    xs = _dispatch(x2.astype(BF16), plan["row_tok"].reshape(-1, 1), *plan["d_items"])
    out_rows = _experts(xs, wg.astype(BF16), wu.astype(BF16), wd.astype(BF16),
                        plan["row_w"].reshape(-1, 1), plan["block_expert"])
    return _combine_ln(out_rows, plan["row_tok"].reshape(nbk, 1, MOE_BLK), *plan["c_items"], x2, ln_g, ln_b)


def _forward(x, mla_w_in, mla_q_norm, mla_w_q_up, mla_kv_norm, mla_w_kv_up, mla_w_out, nsa_w_in,
             nsa_cmp_pe_k, nsa_cmp_w1_k, nsa_cmp_w2_k, nsa_cmp_pe_v, nsa_cmp_w1_v, nsa_cmp_w2_v,
             nsa_w_out, rel_bias, ffn_w_gate, ffn_w_up, ffn_w_down, moe_w_router, moe_w_gate,
             moe_w_up, moe_w_down, ln_mix_g, ln_mix_b, ln_ffn_g, ln_ffn_b):
    b, s, d = x.shape
    x2 = x.reshape(b * s, d)
    o = _mla_mixer(x2, b, s, mla_w_in[0], mla_q_norm[0], mla_w_q_up[0], mla_kv_norm[0], mla_w_kv_up[0])
    x2 = _linear(o, mla_w_out[0].astype(BF16), tm=512, tn=d, out_dtype=F32,
                 ln=(x2, ln_mix_g[0], ln_mix_b[0]), name="mla_out_ln")
    x2 = _ffn_ln(x2, ffn_w_gate[0].astype(BF16), ffn_w_up[0].astype(BF16), ffn_w_down[0].astype(BF16),
                 ln_ffn_g[0], ln_ffn_b[0])
    o = _nsa_mixer(x2, b, s, nsa_w_in[0], nsa_cmp_pe_k[0], nsa_cmp_w1_k[0], nsa_cmp_w2_k[0],
                   nsa_cmp_pe_v[0], nsa_cmp_w1_v[0], nsa_cmp_w2_v[0], rel_bias)
    x2 = _linear(o, nsa_w_out[0].astype(BF16), tm=512, tn=d, out_dtype=F32,
                 ln=(x2, ln_mix_g[1], ln_mix_b[1]), name="nsa_out_ln")
    x2 = _moe_ln(x2, moe_w_router[0], moe_w_gate[0], moe_w_up[0], moe_w_down[0], ln_ffn_g[1], ln_ffn_b[1])
    return x2.reshape(b, s, d)


@jax.jit
def kernel(x, mla_w_in, mla_q_norm, mla_w_q_up, mla_kv_norm, mla_w_kv_up, mla_w_out, nsa_w_in,
           nsa_cmp_pe_k, nsa_cmp_w1_k, nsa_cmp_w2_k, nsa_cmp_pe_v, nsa_cmp_w1_v, nsa_cmp_w2_v,
           nsa_w_out, rel_bias, ffn_w_gate, ffn_w_up, ffn_w_down, moe_w_router, moe_w_gate,
           moe_w_up, moe_w_down, ln_mix_g, ln_mix_b, ln_ffn_g, ln_ffn_b):
    return _forward(x, mla_w_in, mla_q_norm, mla_w_q_up, mla_kv_norm, mla_w_kv_up, mla_w_out, nsa_w_in,
                    nsa_cmp_pe_k, nsa_cmp_w1_k, nsa_cmp_w2_k, nsa_cmp_pe_v, nsa_cmp_w1_v, nsa_cmp_w2_v,
                    nsa_w_out, rel_bias, ffn_w_gate, ffn_w_up, ffn_w_down, moe_w_router, moe_w_gate,
                    moe_w_up, moe_w_down, ln_mix_g, ln_mix_b, ln_ffn_g, ln_ffn_b)
```

```python
import functools
import math

import numpy as np
import jax
import jax.numpy as jnp
from jax import lax
from jax.experimental import pallas as pl
from jax.experimental.pallas import tpu as pltpu

F32 = jnp.float32
BF16 = jnp.bfloat16

D_MODEL = 1024
DEPTH = 2

MLA_HEADS = 8
MLA_Q_RANK = 512
MLA_KV_RANK = 256
MLA_NOPE = 128
MLA_ROPE = 64
MLA_V = 128
ROPE_THETA = 10000.0

NSA_HEADS = 16
NSA_GROUPS = 4
NSA_HG = NSA_HEADS // NSA_GROUPS
NSA_DK = 96
NSA_DV = 64
CMP_LEN = 32
CMP_STRIDE = 16
SEL_BLOCK = 64
SEL_TOPN = 16
WINDOW = 512
N_BRANCH = 3
FORCE = 1e6

REL_BUCKETS = 32
REL_MAX_DIST = 128

D_FF = 2816
N_EXPERTS = 8
TOP_K = 2
D_FF_EXPERT = 3584

LN_EPS = 1e-5
RMS_EPS = 1e-6

ALPHA = (2.0 * DEPTH) ** 0.25

NEG = -1e30
NEG_TEST = -1e29

V7X_VMEM_LIMIT = 56 * 1024 * 1024

LOG2E = 1.4426950408889634

NSA_TQ = 256
NSA_LANES = NSA_HG * NSA_TQ
NSA_KC = 256
NSA_DKP = 128
NSA_GATE_ROWS = 8
NSA_BAND = 24
NSA_SEL_ROWS = 16
NSA_VROWS = NSA_DV + 16
NSA_HEAD_SLOTS = 4
NSA_TILES_PER_VARIANT = 8

MOE_BLK = 512
MOE_TC = 512
MOE_TF = 1792


def _cparams(sem, vmem=V7X_VMEM_LIMIT):
    return pltpu.CompilerParams(dimension_semantics=sem, vmem_limit_bytes=vmem)


def _layer_norm(r, g, b):
    mu = jnp.mean(r, -1, keepdims=True)
    d = r - mu
    var = jnp.mean(d * d, -1, keepdims=True)
    return d * lax.rsqrt(var + LN_EPS) * g + b


def _rms_norm(x, g):
    return x * lax.rsqrt(jnp.mean(x * x, -1, keepdims=True) + RMS_EPS) * g


def _split3(a):
    a1 = a.astype(BF16)
    r1 = a - a1.astype(F32)
    a2 = r1.astype(BF16)
    a3 = (r1 - a2.astype(F32)).astype(BF16)
    return a1, a2, a3


def _dot(a, b):
    return jnp.dot(a, b, preferred_element_type=F32)


def _dot_nt(a, b):
    return lax.dot_general(a, b, (((1,), (1,)), ((), ())), preferred_element_type=F32)


def _linear_body(*refs, has_ln):
    it = iter(refs)
    x_ref = next(it)
    w_ref = next(it)
    if has_ln:
        res_ref, lg_ref, lb_ref = next(it), next(it), next(it)
    o_ref = next(it)
    acc = _dot(x_ref[...].astype(BF16), w_ref[...])
    if has_ln:
        acc = _layer_norm(ALPHA * res_ref[...] + acc, lg_ref[...], lb_ref[...])
    o_ref[...] = acc.astype(o_ref.dtype)


def _linear(x, w, *, tm, tn, out_dtype, ln=None, name):
    m, k = x.shape
    n = w.shape[1]
    assert m % tm == 0 and n % tn == 0
    in_specs = [pl.BlockSpec((tm, k), lambda i, j: (i, 0)),
                pl.BlockSpec((k, tn), lambda i, j: (0, j))]
    args = [x, w]
    if ln is not None:
        assert tn == n
        res, lg, lb = ln
        in_specs += [pl.BlockSpec((tm, n), lambda i, j: (i, 0)),
                     pl.BlockSpec((1, n), lambda i, j: (0, 0)),
                     pl.BlockSpec((1, n), lambda i, j: (0, 0))]
        args += [res, lg.reshape(1, n), lb.reshape(1, n)]
    return pl.pallas_call(
        functools.partial(_linear_body, has_ln=ln is not None),
        out_shape=jax.ShapeDtypeStruct((m, n), out_dtype),
        grid=(m // tm, n // tn),
        in_specs=in_specs,
        out_specs=pl.BlockSpec((tm, tn), lambda i, j: (i, j)),
        compiler_params=_cparams(("parallel", "arbitrary")),
        name=name,
    )(*args)


def _rope_tables(s):
    half = MLA_ROPE // 2
    freq = ROPE_THETA ** (-jnp.arange(half, dtype=F32) / half)
    ang = jnp.arange(s).astype(F32)[:, None] * freq[None, :]
    cos, sin = jnp.cos(ang), jnp.sin(ang)
    return jnp.concatenate([cos, cos], -1), jnp.concatenate([-sin, sin], -1)


MLA_DQ = MLA_NOPE + MLA_ROPE
MLA_QROWS = MLA_NOPE + 2 * MLA_ROPE
MLA_T = 512
MLA_HPS = 2
MLA_SUB = 256


def _mla_q_body(lat_ref, g_ref, w_ref, cos_ref, sin_ref, o_ref, *, qscale):
    xn = _rms_norm(lat_ref[...], g_ref[...]).astype(BF16)
    y = _dot_nt(w_ref[...], xn)
    cos, sin = cos_ref[...], sin_ref[...]
    for h in range(MLA_HEADS):
        r0 = h * MLA_QROWS
        o_ref[0, h, :MLA_NOPE, :] = (y[r0:r0 + MLA_NOPE] * qscale).astype(BF16)
        a = y[r0 + MLA_NOPE:r0 + MLA_DQ]
        bb = y[r0 + MLA_DQ:r0 + MLA_QROWS]
        o_ref[0, h, MLA_NOPE:, :] = ((a * cos + bb * sin) * qscale).astype(BF16)


def _mla_q_proj(lat, gain, w_t, cos_t, sin_t, b, s, *, tm=MLA_T):
    ns = s // tm
    body = functools.partial(_mla_q_body, qscale=(MLA_DQ ** -0.5) * LOG2E)
    return pl.pallas_call(
        body,
        out_shape=jax.ShapeDtypeStruct((b, MLA_HEADS, MLA_DQ, s), BF16),
        grid=(b * ns,),
        in_specs=[pl.BlockSpec((tm, MLA_Q_RANK), lambda i: (i, 0)),
                  pl.BlockSpec((1, MLA_Q_RANK), lambda i: (0, 0)),
                  pl.BlockSpec(w_t.shape, lambda i: (0, 0)),
                  pl.BlockSpec((MLA_ROPE, tm), lambda i: (0, i % ns)),
                  pl.BlockSpec((MLA_ROPE, tm), lambda i: (0, i % ns))],
        out_specs=pl.BlockSpec((1, MLA_HEADS, MLA_DQ, tm), lambda i: (i // ns, 0, 0, i % ns)),
        compiler_params=_cparams(("parallel",)),
        name="mla_q_proj",
    )(lat, gain.reshape(1, -1), w_t, cos_t, sin_t)


def _mla_kv_body(lat_ref, g_ref, wk_ref, wvt_ref, kr_ref, cos_ref, sin_ref, k_ref, vt_ref):
    xn = _rms_norm(lat_ref[...], g_ref[...]).astype(BF16)
    kn = _dot(xn, wk_ref[...]).astype(BF16)
    vt = _dot_nt(wvt_ref[...], xn).astype(BF16)
    kr = kr_ref[...]
    rot = (kr[:, :MLA_ROPE] * cos_ref[...] + kr[:, MLA_ROPE:] * sin_ref[...]).astype(BF16)
    for h in range(MLA_HEADS):
        k_ref[0, h, :, :MLA_NOPE] = kn[:, h * MLA_NOPE:(h + 1) * MLA_NOPE]
        k_ref[0, h, :, MLA_NOPE:] = rot
        for kk in range(MLA_T // MLA_SUB):
            vt_ref[0, h, kk] = vt[h * MLA_V:(h + 1) * MLA_V, kk * MLA_SUB:(kk + 1) * MLA_SUB]


def _mla_kv_proj(lat, gain, wk, wv_t, cosx, sinx, b, s, *, tm=MLA_T):
    ns = s // tm
    nsub = tm // MLA_SUB
    return pl.pallas_call(
        _mla_kv_body,
        out_shape=(jax.ShapeDtypeStruct((b, MLA_HEADS, s, MLA_DQ), BF16),
                   jax.ShapeDtypeStruct((b, MLA_HEADS, ns * nsub, MLA_V, MLA_SUB), BF16)),
        grid=(b * ns,),
        in_specs=[pl.BlockSpec((tm, MLA_KV_RANK), lambda i: (i, MLA_Q_RANK // MLA_KV_RANK)),
                  pl.BlockSpec((1, MLA_KV_RANK), lambda i: (0, 0)),
                  pl.BlockSpec(wk.shape, lambda i: (0, 0)),
                  pl.BlockSpec(wv_t.shape, lambda i: (0, 0)),
                  pl.BlockSpec((tm, 2 * MLA_ROPE),
                               lambda i: (i, (MLA_Q_RANK + MLA_KV_RANK) // (2 * MLA_ROPE))),
                  pl.BlockSpec((tm, MLA_ROPE), lambda i: (i % ns, 0)),
                  pl.BlockSpec((tm, MLA_ROPE), lambda i: (i % ns, 0))],
        out_specs=(pl.BlockSpec((1, MLA_HEADS, tm, MLA_DQ), lambda i: (i // ns, 0, i % ns, 0)),
                   pl.BlockSpec((1, MLA_HEADS, nsub, MLA_V, MLA_SUB), lambda i: (i // ns, 0, i % ns, 0, 0))),
        compiler_params=_cparams(("parallel",)),
        name="mla_kv_proj",
    )(lat, gain.reshape(1, -1), wk, wv_t, lat, cosx, sinx)


def _mla_attn_body(qt_ref, k_ref, vt_ref, o_ref, *scr):
    i = pl.program_id(2)
    sub = MLA_SUB
    nsub = MLA_T // sub
    assert nsub == 2
    chains_all = [(h, ql) for ql in range(nsub) for h in range(MLA_HPS)]
    nch = len(chains_all)
    ring_a, ring_b, scr = scr[:nch], scr[nch:2 * nch], scr[2 * nch:]
    state = {(h, ql): scr[3 * (nsub * h + ql):3 * (nsub * h + ql) + 3]
             for h in range(MLA_HPS) for ql in range(nsub)}
    for m_scr, l_scr, acc_scr in state.values():
        m_scr[...] = jnp.full_like(m_scr, NEG)
        l_scr[...] = jnp.zeros_like(l_scr)
        acc_scr[...] = jnp.zeros_like(acc_scr)

    def issue(sk, ring, chains):
        for idx, (h, ql) in enumerate(chains_all):
            if (h, ql) in chains:
                k = k_ref[0, h, pl.ds(pl.multiple_of(sk * sub, sub), sub), :]
                ring[idx][...] = _dot(k, qt_ref[0, h, :, ql * sub:(ql + 1) * sub])
            yield

    def consume(sk, ring, chains, diag_ql=None):
        for idx, (h, ql) in enumerate(chains_all):
            if (h, ql) in chains:
                m_scr, l_scr, acc_scr = state[(h, ql)]
                s = ring[idx][...]
                if ql == diag_ql:
                    key = lax.broadcasted_iota(jnp.int32, s.shape, 0)
                    qry = lax.broadcasted_iota(jnp.int32, s.shape, 1)
                    s = jnp.where(key <= qry, s, NEG)
                m_old = m_scr[...]
                m_new = jnp.maximum(m_old, jnp.max(s, 0, keepdims=True))
                a = jnp.exp2(m_old - m_new)
                p = jnp.exp2(s - m_new)
                l_scr[...] = a * l_scr[...] + jnp.sum(p, 0, keepdims=True)
                acc_scr[...] = a * acc_scr[...] + _dot(vt_ref[0, h, sk], p.astype(BF16))
                m_scr[...] = m_new
            yield

    def interleave(*gens):
        for _ in zip(*gens):
            pass

    interleave(issue(0, ring_a, chains_all))

    def trip(j, carry):
        c0 = 2 * j
        interleave(issue(c0 + 1, ring_b, chains_all), consume(c0, ring_a, chains_all))
        interleave(issue(c0 + 2, ring_a, chains_all), consume(c0 + 1, ring_b, chains_all))
        return carry

    lax.fori_loop(0, i, trip, 0)
    upper = [c for c in chains_all if c[1] == 1]
    interleave(issue(2 * i + 1, ring_b, upper), consume(2 * i, ring_a, chains_all, diag_ql=0))
    interleave(consume(2 * i + 1, ring_b, upper, diag_ql=1))
    for (h, ql), (m_scr, l_scr, acc_scr) in state.items():
        o_t = acc_scr[...] * (1.0 / jnp.maximum(l_scr[...], 1e-30))
        o_ref[0, ql * sub:(ql + 1) * sub, h * MLA_V:(h + 1) * MLA_V] = o_t.T.astype(o_ref.dtype)


def _mla_attention(q_t, k, v_t, b, s):
    tq = MLA_T
    nq = s // tq
    hp = MLA_HPS
    nsub = tq // MLA_SUB
    sub_state = [pltpu.VMEM((1, MLA_SUB), F32), pltpu.VMEM((1, MLA_SUB), F32), pltpu.VMEM((MLA_V, MLA_SUB), F32)]
    ring = hp * nsub * [pltpu.VMEM((MLA_SUB, MLA_SUB), F32)]
    return pl.pallas_call(
        _mla_attn_body,
        out_shape=jax.ShapeDtypeStruct((b, s, MLA_HEADS * MLA_V), BF16),
        grid=(b, MLA_HEADS // hp, nq),
        in_specs=[pl.BlockSpec((1, hp, MLA_DQ, tq), lambda bb, h, i: (bb, h, 0, i)),
                  pl.BlockSpec((1, hp, s, MLA_DQ), lambda bb, h, i: (bb, h, 0, 0)),
                  pl.BlockSpec((1, hp, nq * nsub, MLA_V, MLA_SUB), lambda bb, h, i: (bb, h, 0, 0, 0))],
        out_specs=pl.BlockSpec((1, tq, hp * MLA_V), lambda bb, h, i: (bb, i, h)),
        scratch_shapes=2 * ring + hp * nsub * sub_state,
        compiler_params=_cparams(("parallel", "parallel", "arbitrary")),
        name="mla_attention",
    )(q_t, k, v_t)


def _mla_mixer(x2, b, s, w_in, q_norm, w_q_up, kv_norm, w_kv_up):
    r0 = MLA_Q_RANK + MLA_KV_RANK
    half = MLA_ROPE // 2
    w_in_ext = jnp.concatenate([w_in, w_in[:, r0 + half:r0 + MLA_ROPE], w_in[:, r0:r0 + half]], 1)
    lat = _linear(x2, w_in_ext.astype(BF16), tm=512, tn=w_in_ext.shape[1], out_dtype=F32, name="mla_in")
    wq = w_q_up.reshape(MLA_Q_RANK, MLA_HEADS, MLA_DQ)
    wr = wq[..., MLA_NOPE:]
    wq = jnp.concatenate([wq, wr[..., half:], wr[..., :half]], -1)
    wq_t = wq.reshape(MLA_Q_RANK, MLA_HEADS * MLA_QROWS).T.astype(BF16)
    wkv = w_kv_up.reshape(MLA_KV_RANK, MLA_HEADS, MLA_NOPE + MLA_V)
    wk = wkv[..., :MLA_NOPE].reshape(MLA_KV_RANK, MLA_HEADS * MLA_NOPE).astype(BF16)
    wv_t = wkv[..., MLA_NOPE:].reshape(MLA_KV_RANK, MLA_HEADS * MLA_V).T.astype(BF16)
    cosx, sinx = _rope_tables(s)
    q_t = _mla_q_proj(lat, q_norm, wq_t, cosx.T, sinx.T, b, s)
    k, v_t = _mla_kv_proj(lat, kv_norm, wk, wv_t, cosx, sinx, b, s)
    o = _mla_attention(q_t, k, v_t, b, s)
    return o.reshape(b * s, MLA_HEADS * MLA_V)


def _ffn_body(x_ref, wg_ref, wu_ref, wd_ref, lg_ref, lb_ref, o_ref, acc_scr):
    f = pl.program_id(1)
    xb = x_ref[...].astype(BF16)
    h = jax.nn.silu(_dot(xb, wg_ref[...])) * _dot(xb, wu_ref[...])
    part = _dot(h.astype(BF16), wd_ref[...])

    @pl.when(f == 0)
    def _():
        acc_scr[...] = part

    @pl.when(f > 0)
    def _():
        acc_scr[...] += part

    @pl.when(f == pl.num_programs(1) - 1)
    def _():
        o_ref[...] = _layer_norm(ALPHA * x_ref[...] + acc_scr[...], lg_ref[...], lb_ref[...])


def _ffn_ln(x, wg, wu, wd, lg, lb, *, tm=512, tf=1408):
    m, d = x.shape
    dff = wg.shape[1]
    assert dff % tf == 0
    vec = pl.BlockSpec((1, d), lambda i, f: (0, 0))
    return pl.pallas_call(
        _ffn_body,
        out_shape=jax.ShapeDtypeStruct((m, d), F32),
        grid=(m // tm, dff // tf),
        in_specs=[pl.BlockSpec((tm, d), lambda i, f: (i, 0)),
                  pl.BlockSpec((d, tf), lambda i, f: (0, f)),
                  pl.BlockSpec((d, tf), lambda i, f: (0, f)),
                  pl.BlockSpec((tf, d), lambda i, f: (f, 0)),
                  vec, vec],
        out_specs=pl.BlockSpec((tm, d), lambda i, f: (i, 0)),
        scratch_shapes=[pltpu.VMEM((tm, d), F32)],
        compiler_params=_cparams(("parallel", "arbitrary")),
        name="ffn_ln",
    )(x, wg, wu, wd, lg.reshape(1, d), lb.reshape(1, d))


def _compress_body(a_ref, pe_ref, w1_ref, w2_ref, o_ref):
    a = (a_ref[...].astype(F32) + pe_ref[...]).astype(BF16)
    h = jax.nn.gelu(_dot(a, w1_ref[...]))
    o_ref[...] = _dot(h.astype(BF16), w2_ref[...]).astype(o_ref.dtype)


def _compress(a, pe, w1, w2, *, tm=512):
    m, k = a.shape
    dh, d = w2.shape
    return pl.pallas_call(
        _compress_body,
        out_shape=jax.ShapeDtypeStruct((m, d), BF16),
        grid=(m // tm,),
        in_specs=[pl.BlockSpec((tm, k), lambda i: (i, 0)),
                  pl.BlockSpec((1, k), lambda i: (0, 0)),
                  pl.BlockSpec((k, dh), lambda i: (0, 0)),
                  pl.BlockSpec((dh, d), lambda i: (0, 0))],
        out_specs=pl.BlockSpec((tm, d), lambda i: (i, 0)),
        compiler_params=_cparams(("parallel",)),
        name="nsa_compress",
    )(a, pe.reshape(1, k), w1.reshape(k, dh).astype(BF16), w2.astype(BF16))


def _rel_bucket_np(dist):
    n = np.maximum(dist, 0)
    max_exact = REL_BUCKETS // 2
    nf = np.maximum(n, 1).astype(np.float32)
    large = max_exact + (np.log(nf / np.float32(max_exact)) / np.float32(math.log(REL_MAX_DIST / max_exact))
                         * np.float32(REL_BUCKETS - max_exact)).astype(np.int32)
    large = np.minimum(large, REL_BUCKETS - 1)
    return np.where(n < max_exact, n, large).astype(np.int32)


def _nsa_tables(rel_bias, s):
    g, hg, tq, kc = NSA_GROUPS, NSA_HG, NSA_TQ, NSA_KC
    assert tq == kc and tq % CMP_STRIDE == 0 and WINDOW == 2 * kc
    assert np.all(_rel_bucket_np(np.arange(tq // 2 - 15, s + tq)) == REL_BUCKETS - 1)
    rb = rel_bias.reshape(REL_BUCKETS, g, hg) * LOG2E

    def tile(base, step, rows, valid):
        p = tq + step * rows
        k = np.arange(p)
        k = np.where(k < p - step * (rows - 1), k, k - p)
        d = base + k
        vec = jnp.where(valid(d)[:, None, None], rb[_rel_bucket_np(d)], NEG)
        vec = vec.transpose(1, 2, 0)
        flat = jnp.tile(vec, (1, 1, rows))[..., :rows * (p - step)]
        mat = flat.reshape(g, hg, rows, p - step)[..., :tq]
        return mat.transpose(0, 2, 1, 3).reshape(g, rows, hg * tq)

    causal = lambda d: d >= 0
    far = jnp.broadcast_to(rb[REL_BUCKETS - 1][:, None, :, None], (g, 1, hg, tq)).reshape(g, 1, hg * tq)
    rel = lambda x: jnp.where(x > NEG_TEST, x - far, NEG)
    tiles = jnp.stack([rel(tile(0, 1, kc, causal)), rel(tile(tq, 1, kc, causal)),
                       rel(tile(2 * tq, 1, kc, lambda d: d < WINDOW)),
                       jnp.zeros((g, kc, hg * tq), F32)], 1)
    band = jnp.stack([rel(tile(8 * CMP_STRIDE - CMP_LEN + 1, CMP_STRIDE, NSA_BAND, causal)),
                      rel(tile(-(CMP_LEN - 1), CMP_STRIDE, NSA_BAND, causal))], 1)
    return tiles, band


def _overlap_t(nc_pad, nb):
    n = np.arange(nc_pad)[None, :]
    jb = np.arange(nb)[:, None]
    cstart = n * CMP_STRIDE
    cend = cstart + CMP_LEN - 1
    sstart = jb * SEL_BLOCK
    ov = (cstart <= sstart + SEL_BLOCK - 1) & (cend >= sstart) & (n < nc_pad - 1)
    return jnp.asarray(ov.astype(np.float32), BF16)


def _nsa_body(qt_ref, kc_ref, vct_ref, k_ref, vt_ref, gate_ref, tiles_ref,
              band_ref, ovt_ref, o_ref, s_scr, sel_scr, qa_scr, oc_scr, *scr, nb, nqt):
    t = pl.program_id(2)
    ring_a, ring_b, state = scr[:NSA_HG], scr[NSA_HG:2 * NSA_HG], scr[2 * NSA_HG:]
    nkc = k_ref.shape[3]
    L = NSA_LANES
    q_t = qt_ref[0, 0, 0]
    ncp = kc_ref.shape[2]
    per = NSA_KC // SEL_BLOCK
    blocks_per_tile = NSA_TQ // CMP_STRIDE

    n_slab = nb // NSA_SEL_ROWS

    n_win = jnp.minimum(t, 2) + 1
    n_slots = n_win + t + 1

    def slot_params(c):
        is_win = c < n_win
        j = c - n_win
        is_sel = jnp.logical_and(c >= n_win, j <= t)
        delta = t - j
        br = is_win.astype(jnp.int32)
        kidx = jnp.where(is_win, t - c, jnp.where(is_sel, j, 0))
        sidx = jnp.where(is_sel, j // (NSA_SEL_ROWS // per), jnp.where(is_win, 0, n_slab))
        tidx = jnp.where(is_win, c, jnp.where(jnp.logical_and(is_sel, delta < 2), delta, 3))
        return br, kidx, sidx, tidx

    def issue(c, ring, set_rows=True):
        br, kidx, sidx, _ = slot_params(c)
        k = k_ref[0, 0, br, kidx]
        if set_rows:
            qa_scr[NSA_DK:NSA_DK + NSA_SEL_ROWS, :] = sel_scr[sidx]
        for h in range(NSA_HG):
            ring[h][...] = _dot(k, qa_scr[:, h * NSA_TQ:(h + 1) * NSA_TQ])
            yield

    def consume(c, ring, near):
        br, kidx, _, tidx = slot_params(c)
        vt = vt_ref[0, 0, br, kidx]
        for h in range(NSA_HG):
            m_scr, acc_scr = state[2 * h:2 * h + 2]
            sc = ring[h][...]
            if near:
                sc = sc + tiles_ref[0, tidx, :, h * NSA_TQ:(h + 1) * NSA_TQ]
            m_old = m_scr[br]
            m_new = jnp.maximum(m_old, jnp.max(sc, 0, keepdims=True))
            pp = jnp.exp2(sc - m_new)
            acc_scr[br] = jnp.exp2(m_old - m_new) * acc_scr[br] + _dot(vt, pp.astype(BF16))
            m_scr[br] = m_new
            yield

    def interleave(*gens):
        for _ in zip(*gens):
            pass

    for h in range(NSA_HG):
        m_scr, acc_scr = state[2 * h:2 * h + 2]
        m_scr[...] = jnp.full_like(m_scr, NEG)
        acc_scr[...] = jnp.zeros_like(acc_scr)
    qa_scr[...] = q_t

    def compress_and_select(nrows, nblk):
        s = _dot(kc_ref[0, 0, :nrows], q_t)
        first = (t == 0).astype(jnp.int32)
        bs = pl.multiple_of((blocks_per_tile * t - 8) * (1 - first), 8)
        row = lax.broadcasted_iota(jnp.int32, (nrows, L), 0)
        s_scr[:nrows] = jnp.where(row < bs + NSA_BAND, s, NEG)
        s_scr[pl.ds(bs, NSA_BAND), :] += band_ref[0, first]
        s = s_scr[:nrows]
        m = jnp.max(s, 0, keepdims=True)
        m = jnp.where(m < NEG_TEST, 0.0, m)
        p = jnp.exp2(s - m)
        den = jnp.maximum(jnp.sum(p, 0, keepdims=True), 1e-30)
        p = p * (1.0 / den)
        oc_scr[...] = _dot(vct_ref[0, 0, :, :nrows], p.astype(BF16))

        psum = p[:, 0:NSA_TQ]
        for h in range(1, NSA_HG):
            psum = psum + p[:, h * NSA_TQ:(h + 1) * NSA_TQ]
        p1, p2, p3 = _split3(psum)
        ovt = ovt_ref[:nblk, :nrows]
        imp = _dot(ovt, p1) + _dot(ovt, p2) + _dot(ovt, p3)
        interleave(issue(0, ring_a, set_rows=False))
        blk = lax.broadcasted_iota(jnp.int32, (nblk, NSA_TQ), 0)
        lane = lax.broadcasted_iota(jnp.int32, (nblk, NSA_TQ), 1)
        cur = (NSA_TQ // SEL_BLOCK) * t + lane // SEL_BLOCK
        forced = (blk == 0) | (blk == cur) | (blk == cur - 1)
        v = jnp.where(blk > cur, -FORCE, jnp.where(forced, FORCE, imp))
        blk_f = blk.astype(F32)
        sel = jnp.zeros((nblk, NSA_TQ), F32)
        for _ in range(min(SEL_TOPN, nblk)):
            mx = jnp.max(v, 0, keepdims=True)
            idx = jnp.min(jnp.where(v == mx, blk_f, float(nblk)), 0, keepdims=True)
            hit = blk_f == idx
            sel = jnp.where(hit, 1.0, sel)
            v = jnp.where(hit, -jnp.inf, v)
        selneg = jnp.where(sel > 0.5, 0.0, NEG)
        selneg = jnp.concatenate([selneg] * NSA_HG, 1).astype(BF16)
        for u in range(nblk // NSA_SEL_ROWS):
            sel_scr[u] = selneg[NSA_SEL_ROWS * u:NSA_SEL_ROWS * (u + 1), :]

    n_var = -(-nqt // NSA_TILES_PER_VARIANT)
    for i in range(n_var):
        nrows = min(ncp, (i + 1) * NSA_TILES_PER_VARIANT * blocks_per_tile)
        nblk = min(nb, (i + 1) * NSA_TILES_PER_VARIANT * NSA_TQ // SEL_BLOCK)

        @pl.when(t // NSA_TILES_PER_VARIANT == i)
        def _(nrows=nrows, nblk=nblk):
            compress_and_select(nrows, nblk)

    sel_scr[n_slab] = jnp.full((NSA_SEL_ROWS, L), NEG, BF16)

    def make_trip(first_slot, near):
        def trip(i, carry):
            c0 = first_slot + 2 * i
            interleave(issue(c0 + 1, ring_b), consume(c0, ring_a, near))
            interleave(issue(c0 + 2, ring_a), consume(c0 + 1, ring_b, near))
            return carry
        return trip

    far_trips = jnp.maximum(n_slots - 2 - NSA_HEAD_SLOTS, 0) // 2
    tail_first = NSA_HEAD_SLOTS + 2 * far_trips
    tail_trips = (jnp.maximum(n_slots - tail_first, 0) + 1) // 2
    lax.fori_loop(0, NSA_HEAD_SLOTS // 2, make_trip(0, True), 0)
    lax.fori_loop(0, far_trips, make_trip(NSA_HEAD_SLOTS, False), 0)
    lax.fori_loop(0, tail_trips, make_trip(tail_first, True), 0)

    gate = gate_ref[0, 0, 0]
    outs = []
    for h in range(NSA_HG):
        sl = slice(h * NSA_TQ, (h + 1) * NSA_TQ)
        acc_scr = state[2 * h + 1]
        o_s = acc_scr[0, :NSA_DV] * (1.0 / jnp.maximum(acc_scr[0, NSA_DV:NSA_DV + 1], 1e-30))
        o_w = acc_scr[1, :NSA_DV] * (1.0 / jnp.maximum(acc_scr[1, NSA_DV:NSA_DV + 1], 1e-30))
        outs.append(gate[0:1, sl] * oc_scr[:, sl] + gate[1:2, sl] * o_s + gate[2:3, sl] * o_w)
    o_ref[0] = jnp.concatenate(outs, 0).T.astype(o_ref.dtype)


def _nsa_attention(q_t, kcmp, vcmp_t, k, v_t, gate, tiles, band, ovt):
    b, g, nqt = q_t.shape[:3]
    ncp = kcmp.shape[2]
    nkc = k.shape[3]
    nb = ovt.shape[0]
    L = NSA_LANES
    body = functools.partial(_nsa_body, nb=nb, nqt=nqt)
    ring = NSA_HG * [pltpu.VMEM((NSA_KC, NSA_TQ), F32)]
    head_state = [pltpu.VMEM((2, 1, NSA_TQ), F32), pltpu.VMEM((2, NSA_VROWS, NSA_TQ), F32)]
    grp = lambda bb, gg, t: (bb, gg, 0, 0)
    grp6 = lambda bb, gg, t: (bb, gg, 0, 0, 0, 0)
    return pl.pallas_call(
        body,
        out_shape=jax.ShapeDtypeStruct((b, nqt * NSA_TQ, g * NSA_HG * NSA_DV), BF16),
        grid=(b, g, nqt),
        in_specs=[pl.BlockSpec((1, 1, 1, NSA_DKP, L), lambda bb, gg, t: (bb, gg, t, 0, 0)),
                  pl.BlockSpec((1, 1, ncp, NSA_DKP), grp),
                  pl.BlockSpec((1, 1, NSA_DV, ncp), grp),
                  pl.BlockSpec((1, 1, 2, nkc, NSA_KC, NSA_DKP), grp6),
                  pl.BlockSpec((1, 1, 2, nkc, NSA_VROWS, NSA_KC), grp6),
                  pl.BlockSpec((1, 1, 1, N_BRANCH, L), lambda bb, gg, t: (bb, gg, t, 0, 0)),
                  pl.BlockSpec((1, 4, NSA_KC, L), lambda bb, gg, t: (gg, 0, 0, 0)),
                  pl.BlockSpec((1, 2, NSA_BAND, L), lambda bb, gg, t: (gg, 0, 0, 0)),
                  pl.BlockSpec(ovt.shape, lambda bb, gg, t: (0, 0))],
        out_specs=pl.BlockSpec((1, NSA_TQ, NSA_HG * NSA_DV), lambda bb, gg, t: (bb, t, gg)),
        scratch_shapes=[pltpu.VMEM((ncp, L), F32),
                        pltpu.VMEM((nb // NSA_SEL_ROWS + 1, NSA_SEL_ROWS, L), BF16),
                        pltpu.VMEM((NSA_DKP, L), BF16),
                        pltpu.VMEM((NSA_DV, L), F32)]
        + 2 * ring + NSA_HG * head_state,
        compiler_params=_cparams(("parallel", "parallel", "arbitrary")),
        name="nsa_attention",
    )(q_t, kcmp, vcmp_t, k, v_t, gate, tiles, band, ovt)


def _nsa_proj_body(x_ref, wq_ref, wk_ref, wv_ref, wg_ref, wc_ref,
                   q_ref, k_ref, v_ref, gate_ref, c_ref, *, qscale):
    g, hg, tq = NSA_GROUPS, NSA_HG, NSA_TQ
    xb = x_ref[...].astype(BF16)
    q_t = _dot_nt(wq_ref[...], xb) * qscale
    gate_t = jax.nn.sigmoid(_dot_nt(wg_ref[...], xb))
    for gg in range(g):
        for h in range(hg):
            head = gg * hg + h
            q_ref[0, gg, 0, :, h * tq:(h + 1) * tq] = q_t[head * NSA_DKP:(head + 1) * NSA_DKP].astype(BF16)
            r0 = head * NSA_GATE_ROWS
            gate_ref[0, gg, 0, :, h * tq:(h + 1) * tq] = gate_t[r0:r0 + N_BRANCH]
    k = _dot(xb, wk_ref[...])
    v_t = _dot_nt(wv_ref[...], xb).astype(BF16)
    per = tq // SEL_BLOCK
    chunk = pl.program_id(0) % (NSA_SEL_ROWS // per)
    row = lax.broadcasted_iota(jnp.int32, (tq, NSA_DKP), 0)
    col = lax.broadcasted_iota(jnp.int32, (tq, NSA_DKP), 1)
    blk_flag = jnp.where(col - NSA_DK == per * chunk + row // SEL_BLOCK, 1.0, 0.0)
    ones_rows = jnp.where(lax.broadcasted_iota(jnp.int32, (NSA_VROWS - NSA_DV, tq), 0) == 0, 1.0, 0.0).astype(BF16)
    for gg in range(g):
        for br in range(2):
            kb = k[:, (br * g + gg) * NSA_DKP:(br * g + gg + 1) * NSA_DKP]
            if br == 0:
                kb = kb + blk_flag
            k_ref[0, gg, br, 0] = kb.astype(BF16)
            v_ref[0, gg, br, 0, :NSA_DV, :] = v_t[(br * g + gg) * NSA_DV:(br * g + gg + 1) * NSA_DV]
            v_ref[0, gg, br, 0, NSA_DV:, :] = ones_rows
    c_ref[...] = _dot(xb, wc_ref[...]).astype(BF16)


def _nsa_proj(x2, b, s, wq_t, wk, wv_t, wg_t, wc):
    g, hg, tq, L = NSA_GROUPS, NSA_HG, NSA_TQ, NSA_LANES
    nqt = s // tq
    assert nqt % (NSA_SEL_ROWS * SEL_BLOCK // tq) == 0 and (s // SEL_BLOCK) % NSA_SEL_ROWS == 0
    t, d = x2.shape
    full = lambda i: (0, 0)
    tile5 = lambda i: (i // nqt, 0, i % nqt, 0, 0)
    tile6 = lambda i: (i // nqt, 0, 0, i % nqt, 0, 0)
    body = functools.partial(_nsa_proj_body, qscale=(NSA_DK ** -0.5) * LOG2E)
    return pl.pallas_call(
        body,
        out_shape=(jax.ShapeDtypeStruct((b, g, nqt, NSA_DKP, L), BF16),
                   jax.ShapeDtypeStruct((b, g, 2, nqt, NSA_KC, NSA_DKP), BF16),
                   jax.ShapeDtypeStruct((b, g, 2, nqt, NSA_VROWS, NSA_KC), BF16),
                   jax.ShapeDtypeStruct((b, g, nqt, N_BRANCH, L), F32),
                   jax.ShapeDtypeStruct((t, wc.shape[1]), BF16)),
        grid=(t // tq,),
        in_specs=[pl.BlockSpec((tq, d), lambda i: (i, 0)),
                  pl.BlockSpec(wq_t.shape, full), pl.BlockSpec(wk.shape, full), pl.BlockSpec(wv_t.shape, full),
                  pl.BlockSpec(wg_t.shape, full), pl.BlockSpec(wc.shape, full)],
        out_specs=(pl.BlockSpec((1, g, 1, NSA_DKP, L), tile5),
                   pl.BlockSpec((1, g, 2, 1, NSA_KC, NSA_DKP), tile6),
                   pl.BlockSpec((1, g, 2, 1, NSA_VROWS, NSA_KC), tile6),
                   pl.BlockSpec((1, g, 1, N_BRANCH, L), tile5),
                   pl.BlockSpec((tq, wc.shape[1]), lambda i: (i, 0))),
        compiler_params=_cparams(("parallel",)),
        name="nsa_proj",
    )(x2, wq_t, wk, wv_t, wg_t, wc)


def _nsa_mixer(x2, b, s, w_in, pe_k, w1_k, w2_k, pe_v, w1_v, w2_v, rel_bias):
    assert NSA_TQ == NSA_KC
    t, d = x2.shape
    h, g, hg, dk, dv = NSA_HEADS, NSA_GROUPS, NSA_HG, NSA_DK, NSA_DV
    nb = s // SEL_BLOCK
    sizes = [h * dk, g * dk, g * dv, g * dk, g * dv, g * dk, g * dv, h * N_BRANCH]
    c = [0] + [int(v) for v in np.cumsum(sizes)]
    cols = [w_in[:, c[i]:c[i + 1]] for i in range(len(sizes))]
    w_q, w_kc, w_vc, w_ks, w_vs, w_kw, w_vw, w_gate = cols

    def pad_last(a, n):
        return jnp.pad(a, [(0, 0)] * (a.ndim - 1) + [(0, n - a.shape[-1])])

    wq_t = pad_last(w_q.reshape(d, h, dk), NSA_DKP).reshape(d, h * NSA_DKP).T.astype(BF16)
    wk = jnp.concatenate([pad_last(w.reshape(d, g, dk), NSA_DKP).reshape(d, g * NSA_DKP)
                          for w in (w_ks, w_kw)], 1).astype(BF16)
    wv_t = jnp.concatenate([w_vs, w_vw], 1).T.astype(BF16)
    wg_t = pad_last(w_gate.reshape(d, h, N_BRANCH), NSA_GATE_ROWS).reshape(d, h * NSA_GATE_ROWS).T.astype(BF16)
    wc = jnp.concatenate([w_kc, w_vc], 1).astype(BF16)
    q_t, k, v_t, gate, ctok = _nsa_proj(x2, b, s, wq_t, wk, wv_t, wg_t, wc)
    kc_tok = ctok[:, :g * dk].reshape(b, s, g, dk)
    vc_tok = ctok[:, g * dk:].reshape(b, s, g, dv)

    nch = s // CMP_STRIDE

    def unfold(tok, dd):
        ch = tok.reshape(b, nch, CMP_STRIDE, g, dd).transpose(0, 3, 1, 2, 4).reshape(b, g, nch, CMP_STRIDE * dd)
        nxt = jnp.concatenate([ch[:, :, 1:], jnp.zeros_like(ch[:, :, :1])], 2)
        return jnp.concatenate([ch, nxt], -1).reshape(b * g * nch, CMP_LEN * dd)

    k_cmp = _compress(unfold(kc_tok, dk), pe_k, w1_k, pad_last(w2_k, NSA_DKP)).reshape(b, g, nch, NSA_DKP)
    v_cmp = _compress(unfold(vc_tok, dv), pe_v, w1_v, w2_v).reshape(b, g, nch, dv)
    vcmp_t = v_cmp.transpose(0, 1, 3, 2)

    tiles, band = _nsa_tables(rel_bias, s)
    ovt = _overlap_t(nch, nb)
    o = _nsa_attention(q_t, k_cmp, vcmp_t, k, v_t, gate, tiles, band, ovt)
    return o.reshape(t, h * dv)


def _router_body(x_ref, w_ref, o_ref):
    x1, x2, x3 = _split3(x_ref[...])
    w1, w2, w3 = _split3(w_ref[...])
    acc = _dot(x1, w1)
    acc += _dot(x1, w2) + _dot(x2, w1)
    acc += _dot(x1, w3) + _dot(x2, w2) + _dot(x3, w1)
    o_ref[...] = acc


def _router(x, w, *, tm=512):
    m, d = x.shape
    wp = jnp.pad(w, ((0, 0), (0, 128 - w.shape[1])))
    return pl.pallas_call(
        _router_body,
        out_shape=jax.ShapeDtypeStruct((m, 128), F32),
        grid=(m // tm,),
        in_specs=[pl.BlockSpec((tm, d), lambda i: (i, 0)), pl.BlockSpec((d, 128), lambda i: (0, 0))],
        out_specs=pl.BlockSpec((tm, 128), lambda i: (i, 0)),
        compiler_params=_cparams(("parallel",)),
        name="moe_router",
    )(x, wp)[:, :w.shape[1]]


def _dispatch_body(ir_ref, ic_ref, fl_ref, x_ref, rt_ref, o_ref):
    i = pl.program_id(0)
    flag = fl_ref[i]
    tok = ic_ref[i] * MOE_TC + lax.broadcasted_iota(jnp.int32, (MOE_BLK, MOE_TC), 1)
    onehot = jnp.where(rt_ref[...] == tok, 1.0, 0.0).astype(BF16)
    rows = _dot(onehot, x_ref[...])

    @pl.when(flag == 3)
    def _():
        o_ref[...] = rows.astype(o_ref.dtype)

    @pl.when(flag == 1)
    def _():
        o_ref[...] = (o_ref[...].astype(F32) + rows).astype(o_ref.dtype)


def _dispatch(x_bf, row_tok_col, item_r, item_c, flags):
    t, d = x_bf.shape
    r = row_tok_col.shape[0]
    ni = item_r.shape[0]
    gs = pltpu.PrefetchScalarGridSpec(
        num_scalar_prefetch=3, grid=(ni,),
        in_specs=[pl.BlockSpec((MOE_TC, d), lambda i, ir, ic, fl: (ic[i], 0)),
                  pl.BlockSpec((MOE_BLK, 1), lambda i, ir, ic, fl: (ir[i], 0))],
        out_specs=pl.BlockSpec((MOE_BLK, d), lambda i, ir, ic, fl: (ir[i], 0)))
    return pl.pallas_call(
        _dispatch_body, grid_spec=gs,
        out_shape=jax.ShapeDtypeStruct((r, d), BF16),
        compiler_params=_cparams(("arbitrary",)),
        name="moe_dispatch",
    )(item_r, item_c, flags, x_bf, row_tok_col)


def _expert_body(be_ref, x_ref, wg_ref, wu_ref, wd_ref, rw_ref, o_ref, acc_scr):
    f = pl.program_id(1)
    xb = x_ref[...]
    h = jax.nn.silu(_dot(xb, wg_ref[0])) * _dot(xb, wu_ref[0])
    part = _dot(h.astype(BF16), wd_ref[0])

    @pl.when(f == 0)
    def _():
        acc_scr[...] = part

    @pl.when(f > 0)
    def _():
        acc_scr[...] += part

    @pl.when(f == pl.num_programs(1) - 1)
    def _():
        o_ref[...] = (acc_scr[...] * rw_ref[...]).astype(o_ref.dtype)


def _experts(xs, wg, wu, wd, row_w_col, block_expert):
    r, d = xs.shape
    nbk = r // MOE_BLK
    nf = wg.shape[2] // MOE_TF
    gs = pltpu.PrefetchScalarGridSpec(
        num_scalar_prefetch=1, grid=(nbk, nf),
        in_specs=[pl.BlockSpec((MOE_BLK, d), lambda i, f, be: (i, 0)),
                  pl.BlockSpec((1, d, MOE_TF), lambda i, f, be: (be[i], 0, f)),
                  pl.BlockSpec((1, d, MOE_TF), lambda i, f, be: (be[i], 0, f)),
                  pl.BlockSpec((1, MOE_TF, d), lambda i, f, be: (be[i], f, 0)),
                  pl.BlockSpec((MOE_BLK, 1), lambda i, f, be: (i, 0))],
        out_specs=pl.BlockSpec((MOE_BLK, d), lambda i, f, be: (i, 0)),
        scratch_shapes=[pltpu.VMEM((MOE_BLK, d), F32)])
    return pl.pallas_call(
        _expert_body, grid_spec=gs,
        out_shape=jax.ShapeDtypeStruct((r, d), BF16),
        compiler_params=_cparams(("parallel", "arbitrary")),
        name="moe_experts",
    )(block_expert, xs, wg, wu, wd, row_w_col)


def _combine_body(ir_ref, ic_ref, fl_ref, y_ref, rt_ref, x_ref, g_ref, b_ref, o_ref):
    i = pl.program_id(0)
    flag = fl_ref[i]
    tok = ic_ref[i] * MOE_TC + lax.broadcasted_iota(jnp.int32, (MOE_TC, MOE_BLK), 0)
    onehot = jnp.where(rt_ref[0] == tok, 1.0, 0.0).astype(BF16)
    part = _dot(onehot, y_ref[...])

    @pl.when((flag & 3) == 3)
    def _():
        o_ref[...] = part

    @pl.when((flag & 3) == 1)
    def _():
        o_ref[...] += part

    @pl.when((flag & 4) == 4)
    def _():
        o_ref[...] = _layer_norm(ALPHA * x_ref[...] + o_ref[...], g_ref[...], b_ref[...])


def _combine_ln(out_rows, row_tok_lane, item_r, item_c, flags, x, g, b):
    r, d = out_rows.shape
    t = x.shape[0]
    ni = item_r.shape[0]
    vec = pl.BlockSpec((1, d), lambda i, ir, ic, fl: (0, 0))
    gs = pltpu.PrefetchScalarGridSpec(
        num_scalar_prefetch=3, grid=(ni,),
        in_specs=[pl.BlockSpec((MOE_BLK, d), lambda i, ir, ic, fl: (ir[i], 0)),
                  pl.BlockSpec((1, 1, MOE_BLK), lambda i, ir, ic, fl: (ir[i], 0, 0)),
                  pl.BlockSpec((MOE_TC, d), lambda i, ir, ic, fl: (ic[i], 0)),
                  vec, vec],
        out_specs=pl.BlockSpec((MOE_TC, d), lambda i, ir, ic, fl: (ic[i], 0)))
    return pl.pallas_call(
        _combine_body, grid_spec=gs,
        out_shape=jax.ShapeDtypeStruct((t, d), F32),
        compiler_params=_cparams(("arbitrary",)),
        name="moe_combine_ln",
    )(item_r, item_c, flags, out_rows, row_tok_lane, x, g.reshape(1, d), b.reshape(1, d))


def _moe_plan(top_idx, wts, t):
    e, blk, tc = N_EXPERTS, MOE_BLK, MOE_TC
    a = t * TOP_K
    i32 = jnp.int32
    exp_flat = top_idx.reshape(a).astype(i32)
    tok_flat = jnp.arange(a, dtype=i32) // TOP_K
    _, tok_sorted, w_sorted = lax.sort((exp_flat, tok_flat, wts.reshape(a)), num_keys=1, is_stable=True)
    counts = jnp.sum((exp_flat[:, None] == jnp.arange(e, dtype=i32)[None, :]).astype(i32), 0)
    padded = ((counts + blk - 1) // blk) * blk
    grp_start = jnp.cumsum(counts) - counts
    pad_end = jnp.cumsum(padded)
    pad_start = pad_end - padded
    nbk = a // blk + e
    r = nbk * blk
    tok_ext = jnp.concatenate([tok_sorted, jnp.full((r - a,), -1, i32)])
    w_ext = jnp.concatenate([w_sorted, jnp.zeros((r - a,), F32)])
    rows = jnp.arange(r, dtype=i32)
    row_tok = jnp.full((r,), -1, i32)
    row_w = jnp.zeros((r,), F32)
    for ee in range(e):
        inside = (rows >= pad_start[ee]) & (rows < pad_start[ee] + counts[ee])
        shift = pad_start[ee] - grp_start[ee]
        row_tok = jnp.where(inside, jnp.roll(tok_ext, shift), row_tok)
        row_w = jnp.where(inside, jnp.roll(w_ext, shift), row_w)
    blk_first = jnp.arange(nbk, dtype=i32) * blk
    block_expert = jnp.minimum(jnp.sum((pad_end[None, :] <= blk_first[:, None]).astype(i32), 1), e - 1)

    rt = row_tok.reshape(nbk, blk)
    valid = rt >= 0
    t_lo = jnp.min(jnp.where(valid, rt, t), 1)
    t_hi = jnp.max(rt, 1)
    has = t_hi >= 0
    c_lo = jnp.where(has, t_lo // tc, 0)
    c_hi = jnp.where(has, t_hi // tc, 0)
    n_it = c_hi - c_lo + 1
    off_end = jnp.cumsum(n_it)
    off_start = off_end - n_it
    total = off_end[-1]
    ni = nbk + e * (t // tc)
    idx = jnp.arange(ni, dtype=i32)
    ok = idx < total
    ir = jnp.minimum(jnp.sum((off_end[None, :] <= idx[:, None]).astype(i32), 1), nbk - 1)
    ic = jnp.where(ok, c_lo[ir] + idx - off_start[ir], c_hi[nbk - 1]).astype(i32)
    first = ok & (idx == off_start[ir])
    d_flags = ok.astype(i32) + 2 * first.astype(i32)

    key = jnp.where(ok, ic * nbk + ir, jnp.iinfo(jnp.int32).max)
    perm = jnp.argsort(key)
    ok2 = ok[perm]
    last = total - 1
    cr = jnp.where(ok2, ir[perm], ir[perm][last]).astype(i32)
    cc = jnp.where(ok2, ic[perm], ic[perm][last]).astype(i32)
    first2 = ok2 & jnp.concatenate([jnp.ones((1,), bool), cc[1:] != cc[:-1]])
    last2 = ok2 & jnp.concatenate([(cc[1:] != cc[:-1]) | ~ok2[1:], jnp.ones((1,), bool)])
    c_flags = ok2.astype(i32) + 2 * first2.astype(i32) + 4 * last2.astype(i32)
    return dict(row_tok=row_tok, row_w=row_w, block_expert=block_expert,
                d_items=(ir, ic, d_flags), c_items=(cr, cc, c_flags), nbk=nbk)


def _moe_ln(x2, w_router, wg, wu, wd, ln_g, ln_b):
    t, d = x2.shape
    logits = _router(x2, w_router)
    top_val, top_idx = lax.top_k(logits, TOP_K)
    wts = jax.nn.softmax(top_val, -1)
    plan = _moe_plan(top_idx, wts, t)
    nbk = plan["nbk"]
    xs = _dispatch(x2.astype(BF16), plan["row_tok"].reshape(-1, 1), *plan["d_items"])
    out_rows = _experts(xs, wg.astype(BF16), wu.astype(BF16), wd.astype(BF16),
                        plan["row_w"].reshape(-1, 1), plan["block_expert"])
    return _combine_ln(out_rows, plan["row_tok"].reshape(nbk, 1, MOE_BLK), *plan["c_items"], x2, ln_g, ln_b)


def _forward(x, mla_w_in, mla_q_norm, mla_w_q_up, mla_kv_norm, mla_w_kv_up, mla_w_out, nsa_w_in,
             nsa_cmp_pe_k, nsa_cmp_w1_k, nsa_cmp_w2_k, nsa_cmp_pe_v, nsa_cmp_w1_v, nsa_cmp_w2_v,
             nsa_w_out, rel_bias, ffn_w_gate, ffn_w_up, ffn_w_down, moe_w_router, moe_w_gate,
             moe_w_up, moe_w_down, ln_mix_g, ln_mix_b, ln_ffn_g, ln_ffn_b):
    b, s, d = x.shape
    x2 = x.reshape(b * s, d)
    o = _mla_mixer(x2, b, s, mla_w_in[0], mla_q_norm[0], mla_w_q_up[0], mla_kv_norm[0], mla_w_kv_up[0])
    x2 = _linear(o, mla_w_out[0].astype(BF16), tm=512, tn=d, out_dtype=F32,
                 ln=(x2, ln_mix_g[0], ln_mix_b[0]), name="mla_out_ln")
    x2 = _ffn_ln(x2, ffn_w_gate[0].astype(BF16), ffn_w_up[0].astype(BF16), ffn_w_down[0].astype(BF16),
                 ln_ffn_g[0], ln_ffn_b[0])
    o = _nsa_mixer(x2, b, s, nsa_w_in[0], nsa_cmp_pe_k[0], nsa_cmp_w1_k[0], nsa_cmp_w2_k[0],
                   nsa_cmp_pe_v[0], nsa_cmp_w1_v[0], nsa_cmp_w2_v[0], rel_bias)
    x2 = _linear(o, nsa_w_out[0].astype(BF16), tm=512, tn=d, out_dtype=F32,
                 ln=(x2, ln_mix_g[1], ln_mix_b[1]), name="nsa_out_ln")
    x2 = _moe_ln(x2, moe_w_router[0], moe_w_gate[0], moe_w_up[0], moe_w_down[0], ln_ffn_g[1], ln_ffn_b[1])
    return x2.reshape(b, s, d)


@jax.jit
def kernel(x, mla_w_in, mla_q_norm, mla_w_q_up, mla_kv_norm, mla_w_kv_up, mla_w_out, nsa_w_in,
           nsa_cmp_pe_k, nsa_cmp_w1_k, nsa_cmp_w2_k, nsa_cmp_pe_v, nsa_cmp_w1_v, nsa_cmp_w2_v,
           nsa_w_out, rel_bias, ffn_w_gate, ffn_w_up, ffn_w_down, moe_w_router, moe_w_gate,
           moe_w_up, moe_w_down, ln_mix_g, ln_mix_b, ln_ffn_g, ln_ffn_b):
    return _forward(x, mla_w_in, mla_q_norm, mla_w_q_up, mla_kv_norm, mla_w_kv_up, mla_w_out, nsa_w_in,
                    nsa_cmp_pe_k, nsa_cmp_w1_k, nsa_cmp_w2_k, nsa_cmp_pe_v, nsa_cmp_w1_v, nsa_cmp_w2_v,
                    nsa_w_out, rel_bias, ffn_w_gate, ffn_w_up, ffn_w_down, moe_w_router, moe_w_gate,
                    moe_w_up, moe_w_down, ln_mix_g, ln_mix_b, ln_ffn_g, ln_ffn_b)
```

```python
import functools
import math

import numpy as np
import jax
import jax.numpy as jnp
from jax import lax
from jax.experimental import pallas as pl
from jax.experimental.pallas import tpu as pltpu

F32 = jnp.float32
BF16 = jnp.bfloat16

D_MODEL = 1024
DEPTH = 2

MLA_HEADS = 8
MLA_Q_RANK = 512
MLA_KV_RANK = 256
MLA_NOPE = 128
MLA_ROPE = 64
MLA_V = 128
ROPE_THETA = 10000.0

NSA_HEADS = 16
NSA_GROUPS = 4
NSA_HG = NSA_HEADS // NSA_GROUPS
NSA_DK = 96
NSA_DV = 64
CMP_LEN = 32
CMP_STRIDE = 16
SEL_BLOCK = 64
SEL_TOPN = 16
WINDOW = 512
N_BRANCH = 3
FORCE = 1e6

REL_BUCKETS = 32
REL_MAX_DIST = 128

D_FF = 2816
N_EXPERTS = 8
TOP_K = 2
D_FF_EXPERT = 3584

LN_EPS = 1e-5
RMS_EPS = 1e-6

ALPHA = (2.0 * DEPTH) ** 0.25

NEG = -1e30
NEG_TEST = -1e29

V7X_VMEM_LIMIT = 56 * 1024 * 1024

LOG2E = 1.4426950408889634

NSA_TQ = 256
NSA_LANES = NSA_HG * NSA_TQ
NSA_KC = 256
NSA_DKP = 128
NSA_GATE_ROWS = 8
NSA_BAND = 24
NSA_SEL_ROWS = 16
NSA_VROWS = NSA_DV + 16
NSA_HEAD_SLOTS = 4
NSA_FAR_UNROLL = 2
NSA_TILES_PER_VARIANT = 8

MOE_BLK = 512
MOE_TC = 512
MOE_TF = 1792


def _cparams(sem, vmem=V7X_VMEM_LIMIT):
    return pltpu.CompilerParams(dimension_semantics=sem, vmem_limit_bytes=vmem)


def _layer_norm(r, g, b):
    mu = jnp.mean(r, -1, keepdims=True)
    d = r - mu
    var = jnp.mean(d * d, -1, keepdims=True)
    return d * lax.rsqrt(var + LN_EPS) * g + b


def _rms_norm(x, g):
    return x * lax.rsqrt(jnp.mean(x * x, -1, keepdims=True) + RMS_EPS) * g


def _split3(a):
    a1 = a.astype(BF16)
    r1 = a - a1.astype(F32)
    a2 = r1.astype(BF16)
    a3 = (r1 - a2.astype(F32)).astype(BF16)
    return a1, a2, a3


def _dot(a, b):
    return jnp.dot(a, b, preferred_element_type=F32)


def _dot_nt(a, b):
    return lax.dot_general(a, b, (((1,), (1,)), ((), ())), preferred_element_type=F32)


def _linear_body(*refs, has_ln):
    it = iter(refs)
    x_ref = next(it)
    w_ref = next(it)
    if has_ln:
        res_ref, lg_ref, lb_ref = next(it), next(it), next(it)
    o_ref = next(it)
    acc = _dot(x_ref[...].astype(BF16), w_ref[...])
    if has_ln:
        acc = _layer_norm(ALPHA * res_ref[...] + acc, lg_ref[...], lb_ref[...])
    o_ref[...] = acc.astype(o_ref.dtype)


def _linear(x, w, *, tm, tn, out_dtype, ln=None, name):
    m, k = x.shape
    n = w.shape[1]
    assert m % tm == 0 and n % tn == 0
    in_specs = [pl.BlockSpec((tm, k), lambda i, j: (i, 0)),
                pl.BlockSpec((k, tn), lambda i, j: (0, j))]
    args = [x, w]
    if ln is not None:
        assert tn == n
        res, lg, lb = ln
        in_specs += [pl.BlockSpec((tm, n), lambda i, j: (i, 0)),
                     pl.BlockSpec((1, n), lambda i, j: (0, 0)),
                     pl.BlockSpec((1, n), lambda i, j: (0, 0))]
        args += [res, lg.reshape(1, n), lb.reshape(1, n)]
    return pl.pallas_call(
        functools.partial(_linear_body, has_ln=ln is not None),
        out_shape=jax.ShapeDtypeStruct((m, n), out_dtype),
        grid=(m // tm, n // tn),
        in_specs=in_specs,
        out_specs=pl.BlockSpec((tm, tn), lambda i, j: (i, j)),
        compiler_params=_cparams(("parallel", "arbitrary")),
        name=name,
    )(*args)


def _rope_tables(s):
    half = MLA_ROPE // 2
    freq = ROPE_THETA ** (-jnp.arange(half, dtype=F32) / half)
    ang = jnp.arange(s).astype(F32)[:, None] * freq[None, :]
    cos, sin = jnp.cos(ang), jnp.sin(ang)
    return jnp.concatenate([cos, cos], -1), jnp.concatenate([-sin, sin], -1)


MLA_DQ = MLA_NOPE + MLA_ROPE
MLA_QROWS = MLA_NOPE + 2 * MLA_ROPE
MLA_T = 512
MLA_HPS = 2
MLA_SUB = 256


def _mla_q_body(lat_ref, g_ref, w_ref, cos_ref, sin_ref, o_ref, *, qscale):
    xn = _rms_norm(lat_ref[...], g_ref[...]).astype(BF16)
    y = _dot_nt(w_ref[...], xn)
    cos, sin = cos_ref[...], sin_ref[...]
    for h in range(MLA_HEADS):
        r0 = h * MLA_QROWS
        o_ref[0, h, :MLA_NOPE, :] = (y[r0:r0 + MLA_NOPE] * qscale).astype(BF16)
        a = y[r0 + MLA_NOPE:r0 + MLA_DQ]
        bb = y[r0 + MLA_DQ:r0 + MLA_QROWS]
        o_ref[0, h, MLA_NOPE:, :] = ((a * cos + bb * sin) * qscale).astype(BF16)


def _mla_q_proj(lat, gain, w_t, cos_t, sin_t, b, s, *, tm=MLA_T):
    ns = s // tm
    body = functools.partial(_mla_q_body, qscale=(MLA_DQ ** -0.5) * LOG2E)
    return pl.pallas_call(
        body,
        out_shape=jax.ShapeDtypeStruct((b, MLA_HEADS, MLA_DQ, s), BF16),
        grid=(b * ns,),
        in_specs=[pl.BlockSpec((tm, MLA_Q_RANK), lambda i: (i, 0)),
                  pl.BlockSpec((1, MLA_Q_RANK), lambda i: (0, 0)),
                  pl.BlockSpec(w_t.shape, lambda i: (0, 0)),
                  pl.BlockSpec((MLA_ROPE, tm), lambda i: (0, i % ns)),
                  pl.BlockSpec((MLA_ROPE, tm), lambda i: (0, i % ns))],
        out_specs=pl.BlockSpec((1, MLA_HEADS, MLA_DQ, tm), lambda i: (i // ns, 0, 0, i % ns)),
        compiler_params=_cparams(("parallel",)),
        name="mla_q_proj",
    )(lat, gain.reshape(1, -1), w_t, cos_t, sin_t)


def _mla_kv_body(lat_ref, g_ref, wk_ref, wvt_ref, kr_ref, cos_ref, sin_ref, k_ref, vt_ref):
    xn = _rms_norm(lat_ref[...], g_ref[...]).astype(BF16)
    kn = _dot(xn, wk_ref[...]).astype(BF16)
    vt = _dot_nt(wvt_ref[...], xn).astype(BF16)
    kr = kr_ref[...]
    rot = (kr[:, :MLA_ROPE] * cos_ref[...] + kr[:, MLA_ROPE:] * sin_ref[...]).astype(BF16)
    for h in range(MLA_HEADS):
        k_ref[0, h, :, :MLA_NOPE] = kn[:, h * MLA_NOPE:(h + 1) * MLA_NOPE]
        k_ref[0, h, :, MLA_NOPE:] = rot
        for kk in range(MLA_T // MLA_SUB):
            vt_ref[0, h, kk] = vt[h * MLA_V:(h + 1) * MLA_V, kk * MLA_SUB:(kk + 1) * MLA_SUB]


def _mla_kv_proj(lat, gain, wk, wv_t, cosx, sinx, b, s, *, tm=MLA_T):
    ns = s // tm
    nsub = tm // MLA_SUB
    return pl.pallas_call(
        _mla_kv_body,
        out_shape=(jax.ShapeDtypeStruct((b, MLA_HEADS, s, MLA_DQ), BF16),
                   jax.ShapeDtypeStruct((b, MLA_HEADS, ns * nsub, MLA_V, MLA_SUB), BF16)),
        grid=(b * ns,),
        in_specs=[pl.BlockSpec((tm, MLA_KV_RANK), lambda i: (i, MLA_Q_RANK // MLA_KV_RANK)),
                  pl.BlockSpec((1, MLA_KV_RANK), lambda i: (0, 0)),
                  pl.BlockSpec(wk.shape, lambda i: (0, 0)),
                  pl.BlockSpec(wv_t.shape, lambda i: (0, 0)),
                  pl.BlockSpec((tm, 2 * MLA_ROPE),
                               lambda i: (i, (MLA_Q_RANK + MLA_KV_RANK) // (2 * MLA_ROPE))),
                  pl.BlockSpec((tm, MLA_ROPE), lambda i: (i % ns, 0)),
                  pl.BlockSpec((tm, MLA_ROPE), lambda i: (i % ns, 0))],
        out_specs=(pl.BlockSpec((1, MLA_HEADS, tm, MLA_DQ), lambda i: (i // ns, 0, i % ns, 0)),
                   pl.BlockSpec((1, MLA_HEADS, nsub, MLA_V, MLA_SUB), lambda i: (i // ns, 0, i % ns, 0, 0))),
        compiler_params=_cparams(("parallel",)),
        name="mla_kv_proj",
    )(lat, gain.reshape(1, -1), wk, wv_t, lat, cosx, sinx)


def _mla_attn_body(qt_ref, k_ref, vt_ref, o_ref, *scr):
    i = pl.program_id(2)
    sub = MLA_SUB
    nsub = MLA_T // sub
    assert nsub == 2
    chains_all = [(h, ql) for ql in range(nsub) for h in range(MLA_HPS)]
    nch = len(chains_all)
    ring_a, ring_b, scr = scr[:nch], scr[nch:2 * nch], scr[2 * nch:]
    state = {(h, ql): scr[3 * (nsub * h + ql):3 * (nsub * h + ql) + 3]
             for h in range(MLA_HPS) for ql in range(nsub)}
    for m_scr, l_scr, acc_scr in state.values():
        m_scr[...] = jnp.full_like(m_scr, NEG)
        l_scr[...] = jnp.zeros_like(l_scr)
        acc_scr[...] = jnp.zeros_like(acc_scr)

    def issue(sk, ring, chains):
        for idx, (h, ql) in enumerate(chains_all):
            if (h, ql) in chains:
                k = k_ref[0, h, pl.ds(pl.multiple_of(sk * sub, sub), sub), :]
                ring[idx][...] = _dot(k, qt_ref[0, h, :, ql * sub:(ql + 1) * sub])
            yield

    def consume(sk, ring, chains, diag_ql=None):
        for idx, (h, ql) in enumerate(chains_all):
            if (h, ql) in chains:
                m_scr, l_scr, acc_scr = state[(h, ql)]
                s = ring[idx][...]
                if ql == diag_ql:
                    key = lax.broadcasted_iota(jnp.int32, s.shape, 0)
                    qry = lax.broadcasted_iota(jnp.int32, s.shape, 1)
                    s = jnp.where(key <= qry, s, NEG)
                m_old = m_scr[...]
                m_new = jnp.maximum(m_old, jnp.max(s, 0, keepdims=True))
                a = jnp.exp2(m_old - m_new)
                p = jnp.exp2(s - m_new)
                l_scr[...] = a * l_scr[...] + jnp.sum(p, 0, keepdims=True)
                acc_scr[...] = a * acc_scr[...] + _dot(vt_ref[0, h, sk], p.astype(BF16))
                m_scr[...] = m_new
            yield

    def interleave(*gens):
        for _ in zip(*gens):
            pass

    interleave(issue(0, ring_a, chains_all))

    def trip(j, carry):
        c0 = 2 * j
        interleave(issue(c0 + 1, ring_b, chains_all), consume(c0, ring_a, chains_all))
        interleave(issue(c0 + 2, ring_a, chains_all), consume(c0 + 1, ring_b, chains_all))
        return carry

    lax.fori_loop(0, i, trip, 0)
    upper = [c for c in chains_all if c[1] == 1]
    interleave(issue(2 * i + 1, ring_b, upper), consume(2 * i, ring_a, chains_all, diag_ql=0))
    interleave(consume(2 * i + 1, ring_b, upper, diag_ql=1))
    for (h, ql), (m_scr, l_scr, acc_scr) in state.items():
        o_t = acc_scr[...] * (1.0 / jnp.maximum(l_scr[...], 1e-30))
        o_ref[0, ql * sub:(ql + 1) * sub, h * MLA_V:(h + 1) * MLA_V] = o_t.T.astype(o_ref.dtype)


def _mla_attention(q_t, k, v_t, b, s):
    tq = MLA_T
    nq = s // tq
    hp = MLA_HPS
    nsub = tq // MLA_SUB
    sub_state = [pltpu.VMEM((1, MLA_SUB), F32), pltpu.VMEM((1, MLA_SUB), F32), pltpu.VMEM((MLA_V, MLA_SUB), F32)]
    ring = hp * nsub * [pltpu.VMEM((MLA_SUB, MLA_SUB), F32)]
    return pl.pallas_call(
        _mla_attn_body,
        out_shape=jax.ShapeDtypeStruct((b, s, MLA_HEADS * MLA_V), BF16),
        grid=(b, MLA_HEADS // hp, nq),
        in_specs=[pl.BlockSpec((1, hp, MLA_DQ, tq), lambda bb, h, i: (bb, h, 0, i)),
                  pl.BlockSpec((1, hp, s, MLA_DQ), lambda bb, h, i: (bb, h, 0, 0)),
                  pl.BlockSpec((1, hp, nq * nsub, MLA_V, MLA_SUB), lambda bb, h, i: (bb, h, 0, 0, 0))],
        out_specs=pl.BlockSpec((1, tq, hp * MLA_V), lambda bb, h, i: (bb, i, h)),
        scratch_shapes=2 * ring + hp * nsub * sub_state,
        compiler_params=_cparams(("parallel", "parallel", "arbitrary")),
        name="mla_attention",
    )(q_t, k, v_t)


def _mla_mixer(x2, b, s, w_in, q_norm, w_q_up, kv_norm, w_kv_up):
    r0 = MLA_Q_RANK + MLA_KV_RANK
    half = MLA_ROPE // 2
    w_in_ext = jnp.concatenate([w_in, w_in[:, r0 + half:r0 + MLA_ROPE], w_in[:, r0:r0 + half]], 1)
    lat = _linear(x2, w_in_ext.astype(BF16), tm=512, tn=w_in_ext.shape[1], out_dtype=F32, name="mla_in")
    wq = w_q_up.reshape(MLA_Q_RANK, MLA_HEADS, MLA_DQ)
    wr = wq[..., MLA_NOPE:]
    wq = jnp.concatenate([wq, wr[..., half:], wr[..., :half]], -1)
    wq_t = wq.reshape(MLA_Q_RANK, MLA_HEADS * MLA_QROWS).T.astype(BF16)
    wkv = w_kv_up.reshape(MLA_KV_RANK, MLA_HEADS, MLA_NOPE + MLA_V)
    wk = wkv[..., :MLA_NOPE].reshape(MLA_KV_RANK, MLA_HEADS * MLA_NOPE).astype(BF16)
    wv_t = wkv[..., MLA_NOPE:].reshape(MLA_KV_RANK, MLA_HEADS * MLA_V).T.astype(BF16)
    cosx, sinx = _rope_tables(s)
    q_t = _mla_q_proj(lat, q_norm, wq_t, cosx.T, sinx.T, b, s)
    k, v_t = _mla_kv_proj(lat, kv_norm, wk, wv_t, cosx, sinx, b, s)
    o = _mla_attention(q_t, k, v_t, b, s)
    return o.reshape(b * s, MLA_HEADS * MLA_V)


def _ffn_body(x_ref, wg_ref, wu_ref, wd_ref, lg_ref, lb_ref, o_ref, acc_scr):
    f = pl.program_id(1)
    xb = x_ref[...].astype(BF16)
    h = jax.nn.silu(_dot(xb, wg_ref[...])) * _dot(xb, wu_ref[...])
    part = _dot(h.astype(BF16), wd_ref[...])

    @pl.when(f == 0)
    def _():
        acc_scr[...] = part

    @pl.when(f > 0)
    def _():
        acc_scr[...] += part

    @pl.when(f == pl.num_programs(1) - 1)
    def _():
        o_ref[...] = _layer_norm(ALPHA * x_ref[...] + acc_scr[...], lg_ref[...], lb_ref[...])


def _ffn_ln(x, wg, wu, wd, lg, lb, *, tm=512, tf=1408):
    m, d = x.shape
    dff = wg.shape[1]
    assert dff % tf == 0
    vec = pl.BlockSpec((1, d), lambda i, f: (0, 0))
    return pl.pallas_call(
        _ffn_body,
        out_shape=jax.ShapeDtypeStruct((m, d), F32),
        grid=(m // tm, dff // tf),
        in_specs=[pl.BlockSpec((tm, d), lambda i, f: (i, 0)),
                  pl.BlockSpec((d, tf), lambda i, f: (0, f)),
                  pl.BlockSpec((d, tf), lambda i, f: (0, f)),
                  pl.BlockSpec((tf, d), lambda i, f: (f, 0)),
                  vec, vec],
        out_specs=pl.BlockSpec((tm, d), lambda i, f: (i, 0)),
        scratch_shapes=[pltpu.VMEM((tm, d), F32)],
        compiler_params=_cparams(("parallel", "arbitrary")),
        name="ffn_ln",
    )(x, wg, wu, wd, lg.reshape(1, d), lb.reshape(1, d))


def _compress_body(a_ref, pe_ref, w1_ref, w2_ref, o_ref):
    a = (a_ref[...].astype(F32) + pe_ref[...]).astype(BF16)
    h = jax.nn.gelu(_dot(a, w1_ref[...]))
    o_ref[...] = _dot(h.astype(BF16), w2_ref[...]).astype(o_ref.dtype)


def _compress(a, pe, w1, w2, *, tm=512):
    m, k = a.shape
    dh, d = w2.shape
    return pl.pallas_call(
        _compress_body,
        out_shape=jax.ShapeDtypeStruct((m, d), BF16),
        grid=(m // tm,),
        in_specs=[pl.BlockSpec((tm, k), lambda i: (i, 0)),
                  pl.BlockSpec((1, k), lambda i: (0, 0)),
                  pl.BlockSpec((k, dh), lambda i: (0, 0)),
                  pl.BlockSpec((dh, d), lambda i: (0, 0))],
        out_specs=pl.BlockSpec((tm, d), lambda i: (i, 0)),
        compiler_params=_cparams(("parallel",)),
        name="nsa_compress",
    )(a, pe.reshape(1, k), w1.reshape(k, dh).astype(BF16), w2.astype(BF16))


def _rel_bucket_np(dist):
    n = np.maximum(dist, 0)
    max_exact = REL_BUCKETS // 2
    nf = np.maximum(n, 1).astype(np.float32)
    large = max_exact + (np.log(nf / np.float32(max_exact)) / np.float32(math.log(REL_MAX_DIST / max_exact))
                         * np.float32(REL_BUCKETS - max_exact)).astype(np.int32)
    large = np.minimum(large, REL_BUCKETS - 1)
    return np.where(n < max_exact, n, large).astype(np.int32)


def _nsa_tables(rel_bias, s):
    g, hg, tq, kc = NSA_GROUPS, NSA_HG, NSA_TQ, NSA_KC
    assert tq == kc and tq % CMP_STRIDE == 0 and WINDOW == 2 * kc
    assert np.all(_rel_bucket_np(np.arange(tq // 2 - 15, s + tq)) == REL_BUCKETS - 1)
    rb = rel_bias.reshape(REL_BUCKETS, g, hg) * LOG2E

    def tile(base, step, rows, valid):
        p = tq + step * rows
        k = np.arange(p)
        k = np.where(k < p - step * (rows - 1), k, k - p)
        d = base + k
        vec = jnp.where(valid(d)[:, None, None], rb[_rel_bucket_np(d)], NEG)
        vec = vec.transpose(1, 2, 0)
        flat = jnp.tile(vec, (1, 1, rows))[..., :rows * (p - step)]
        mat = flat.reshape(g, hg, rows, p - step)[..., :tq]
        return mat.transpose(0, 2, 1, 3).reshape(g, rows, hg * tq)

    causal = lambda d: d >= 0
    far = jnp.broadcast_to(rb[REL_BUCKETS - 1][:, None, :, None], (g, 1, hg, tq)).reshape(g, 1, hg * tq)
    rel = lambda x: jnp.where(x > NEG_TEST, x - far, NEG)
    tiles = jnp.stack([rel(tile(0, 1, kc, causal)), rel(tile(tq, 1, kc, causal)),
                       rel(tile(2 * tq, 1, kc, lambda d: d < WINDOW)),
                       jnp.zeros((g, kc, hg * tq), F32)], 1)
    band = jnp.stack([rel(tile(8 * CMP_STRIDE - CMP_LEN + 1, CMP_STRIDE, NSA_BAND, causal)),
                      rel(tile(-(CMP_LEN - 1), CMP_STRIDE, NSA_BAND, causal))], 1)
    return tiles, band


def _overlap_t(nc_pad, nb):
    n = np.arange(nc_pad)[None, :]
    jb = np.arange(nb)[:, None]
    cstart = n * CMP_STRIDE
    cend = cstart + CMP_LEN - 1
    sstart = jb * SEL_BLOCK
    ov = (cstart <= sstart + SEL_BLOCK - 1) & (cend >= sstart) & (n < nc_pad - 1)
    return jnp.asarray(ov.astype(np.float32), BF16)


def _nsa_body(qt_ref, kc_ref, vct_ref, k_ref, vt_ref, gate_ref, tiles_ref,
              band_ref, ovt_ref, o_ref, s_scr, sel_scr, qa_scr, oc_scr, *scr, nb, nqt):
    t = pl.program_id(2)
    ring_a, ring_b, state = scr[:NSA_HG], scr[NSA_HG:2 * NSA_HG], scr[2 * NSA_HG:]
    nkc = k_ref.shape[3]
    L = NSA_LANES
    q_t = qt_ref[0, 0, 0]
    ncp = kc_ref.shape[2]
    per = NSA_KC // SEL_BLOCK
    blocks_per_tile = NSA_TQ // CMP_STRIDE

    n_slab = nb // NSA_SEL_ROWS

    n_win = jnp.minimum(t, 2) + 1
    n_slots = n_win + t + 1

    def slot_params(c):
        is_win = c < n_win
        j = c - n_win
        is_sel = jnp.logical_and(c >= n_win, j <= t)
        delta = t - j
        br = is_win.astype(jnp.int32)
        kidx = jnp.where(is_win, t - c, jnp.where(is_sel, j, 0))
        sidx = jnp.where(is_sel, j // (NSA_SEL_ROWS // per), jnp.where(is_win, 0, n_slab))
        tidx = jnp.where(is_win, c, jnp.where(jnp.logical_and(is_sel, delta < 2), delta, 3))
        return br, kidx, sidx, tidx

    def issue(c, ring, set_rows=True):
        br, kidx, sidx, _ = slot_params(c)
        k = k_ref[0, 0, br, kidx]
        if set_rows:
            qa_scr[NSA_DK:NSA_DK + NSA_SEL_ROWS, :] = sel_scr[sidx]
        for h in range(NSA_HG):
            ring[h][...] = _dot(k, qa_scr[:, h * NSA_TQ:(h + 1) * NSA_TQ])
            yield

    def consume(c, ring, near):
        br, kidx, _, tidx = slot_params(c)
        vt = vt_ref[0, 0, br, kidx]
        for h in range(NSA_HG):
            m_scr, acc_scr = state[2 * h:2 * h + 2]
            sc = ring[h][...]
            if near:
                sc = sc + tiles_ref[0, tidx, :, h * NSA_TQ:(h + 1) * NSA_TQ]
            m_old = m_scr[br]
            m_new = jnp.maximum(m_old, jnp.max(sc, 0, keepdims=True))
            pp = jnp.exp2(sc - m_new)
            acc_scr[br] = jnp.exp2(m_old - m_new) * acc_scr[br] + _dot(vt, pp.astype(BF16))
            m_scr[br] = m_new
            yield

    def interleave(*gens):
        for _ in zip(*gens):
            pass

    for h in range(NSA_HG):
        m_scr, acc_scr = state[2 * h:2 * h + 2]
        m_scr[...] = jnp.full_like(m_scr, NEG)
        acc_scr[...] = jnp.zeros_like(acc_scr)
    qa_scr[...] = q_t

    def compress_and_select(nrows, nblk):
        s = _dot(kc_ref[0, 0, :nrows], q_t)
        first = (t == 0).astype(jnp.int32)
        bs = pl.multiple_of((blocks_per_tile * t - 8) * (1 - first), 8)
        row = lax.broadcasted_iota(jnp.int32, (nrows, L), 0)
        s_scr[:nrows] = jnp.where(row < bs + NSA_BAND, s, NEG)
        s_scr[pl.ds(bs, NSA_BAND), :] += band_ref[0, first]
        s = s_scr[:nrows]
        m = jnp.max(s, 0, keepdims=True)
        m = jnp.where(m < NEG_TEST, 0.0, m)
        p = jnp.exp2(s - m)
        den = jnp.maximum(jnp.sum(p, 0, keepdims=True), 1e-30)
        p = p * (1.0 / den)
        oc_scr[...] = _dot(vct_ref[0, 0, :, :nrows], p.astype(BF16))

        psum = p[:, 0:NSA_TQ]
        for h in range(1, NSA_HG):
            psum = psum + p[:, h * NSA_TQ:(h + 1) * NSA_TQ]
        p1, p2, p3 = _split3(psum)
        ovt = ovt_ref[:nblk, :nrows]
        imp = _dot(ovt, p1) + _dot(ovt, p2) + _dot(ovt, p3)
        interleave(issue(0, ring_a, set_rows=False))
        blk = lax.broadcasted_iota(jnp.int32, (nblk, NSA_TQ), 0)
        lane = lax.broadcasted_iota(jnp.int32, (nblk, NSA_TQ), 1)
        cur = (NSA_TQ // SEL_BLOCK) * t + lane // SEL_BLOCK
        forced = (blk == 0) | (blk == cur) | (blk == cur - 1)
        v = jnp.where(blk > cur, -FORCE, jnp.where(forced, -jnp.inf, imp))
        blk_f = blk.astype(F32)
        sel = jnp.where(forced, 1.0, 0.0)
        for _ in range(min(SEL_TOPN, nblk) - 3):
            mx = jnp.max(v, 0, keepdims=True)
            idx = jnp.min(jnp.where(v == mx, blk_f, float(nblk)), 0, keepdims=True)
            hit = blk_f == idx
            sel = jnp.where(hit, 1.0, sel)
            v = jnp.where(hit, -jnp.inf, v)
        selneg = jnp.where(sel > 0.5, 0.0, NEG)
        selneg = jnp.concatenate([selneg] * NSA_HG, 1).astype(BF16)
        for u in range(nblk // NSA_SEL_ROWS):
            sel_scr[u] = selneg[NSA_SEL_ROWS * u:NSA_SEL_ROWS * (u + 1), :]

    n_var = -(-nqt // NSA_TILES_PER_VARIANT)
    for i in range(n_var):
        nrows = min(ncp, (i + 1) * NSA_TILES_PER_VARIANT * blocks_per_tile)
        nblk = min(nb, (i + 1) * NSA_TILES_PER_VARIANT * NSA_TQ // SEL_BLOCK)

        @pl.when(t // NSA_TILES_PER_VARIANT == i)
        def _(nrows=nrows, nblk=nblk):
            compress_and_select(nrows, nblk)

    sel_scr[n_slab] = jnp.full((NSA_SEL_ROWS, L), NEG, BF16)

    def make_trip(first_slot, near, pairs=1):
        def trip(i, carry):
            for pr in range(pairs):
                c0 = first_slot + 2 * (pairs * i + pr)
                interleave(issue(c0 + 1, ring_b), consume(c0, ring_a, near))
                interleave(issue(c0 + 2, ring_a), consume(c0 + 1, ring_b, near))
            return carry
        return trip

    far_trips = jnp.maximum(n_slots - 2 - NSA_HEAD_SLOTS, 0) // (2 * NSA_FAR_UNROLL)
    tail_first = NSA_HEAD_SLOTS + 2 * NSA_FAR_UNROLL * far_trips
    tail_trips = (jnp.maximum(n_slots - tail_first, 0) + 1) // 2
    lax.fori_loop(0, NSA_HEAD_SLOTS // 2, make_trip(0, True), 0)
    lax.fori_loop(0, far_trips, make_trip(NSA_HEAD_SLOTS, False, NSA_FAR_UNROLL), 0)
    lax.fori_loop(0, tail_trips, make_trip(tail_first, True), 0)

    gate = gate_ref[0, 0, 0]
    outs = []
    for h in range(NSA_HG):
        sl = slice(h * NSA_TQ, (h + 1) * NSA_TQ)
        acc_scr = state[2 * h + 1]
        o_s = acc_scr[0, :NSA_DV] * (1.0 / jnp.maximum(acc_scr[0, NSA_DV:NSA_DV + 1], 1e-30))
        o_w = acc_scr[1, :NSA_DV] * (1.0 / jnp.maximum(acc_scr[1, NSA_DV:NSA_DV + 1], 1e-30))
        outs.append(gate[0:1, sl] * oc_scr[:, sl] + gate[1:2, sl] * o_s + gate[2:3, sl] * o_w)
    o_ref[0] = jnp.concatenate(outs, 0).T.astype(o_ref.dtype)


def _nsa_attention(q_t, kcmp, vcmp_t, k, v_t, gate, tiles, band, ovt):
    b, g, nqt = q_t.shape[:3]
    ncp = kcmp.shape[2]
    nkc = k.shape[3]
    nb = ovt.shape[0]
    L = NSA_LANES
    body = functools.partial(_nsa_body, nb=nb, nqt=nqt)
    ring = NSA_HG * [pltpu.VMEM((NSA_KC, NSA_TQ), F32)]
    head_state = [pltpu.VMEM((2, 1, NSA_TQ), F32), pltpu.VMEM((2, NSA_VROWS, NSA_TQ), F32)]
    grp = lambda bb, gg, t: (bb, gg, 0, 0)
    grp6 = lambda bb, gg, t: (bb, gg, 0, 0, 0, 0)
    return pl.pallas_call(
        body,
        out_shape=jax.ShapeDtypeStruct((b, nqt * NSA_TQ, g * NSA_HG * NSA_DV), BF16),
        grid=(b, g, nqt),
        in_specs=[pl.BlockSpec((1, 1, 1, NSA_DKP, L), lambda bb, gg, t: (bb, gg, t, 0, 0)),
                  pl.BlockSpec((1, 1, ncp, NSA_DKP), grp),
                  pl.BlockSpec((1, 1, NSA_DV, ncp), grp),
                  pl.BlockSpec((1, 1, 2, nkc, NSA_KC, NSA_DKP), grp6),
                  pl.BlockSpec((1, 1, 2, nkc, NSA_VROWS, NSA_KC), grp6),
                  pl.BlockSpec((1, 1, 1, N_BRANCH, L), lambda bb, gg, t: (bb, gg, t, 0, 0)),
                  pl.BlockSpec((1, 4, NSA_KC, L), lambda bb, gg, t: (gg, 0, 0, 0)),
                  pl.BlockSpec((1, 2, NSA_BAND, L), lambda bb, gg, t: (gg, 0, 0, 0)),
                  pl.BlockSpec(ovt.shape, lambda bb, gg, t: (0, 0))],
        out_specs=pl.BlockSpec((1, NSA_TQ, NSA_HG * NSA_DV), lambda bb, gg, t: (bb, t, gg)),
        scratch_shapes=[pltpu.VMEM((ncp, L), F32),
                        pltpu.VMEM((nb // NSA_SEL_ROWS + 1, NSA_SEL_ROWS, L), BF16),
                        pltpu.VMEM((NSA_DKP, L), BF16),
                        pltpu.VMEM((NSA_DV, L), F32)]
        + 2 * ring + NSA_HG * head_state,
        compiler_params=_cparams(("parallel", "parallel", "arbitrary")),
        name="nsa_attention",
    )(q_t, kcmp, vcmp_t, k, v_t, gate, tiles, band, ovt)


def _nsa_proj_body(x_ref, wq_ref, wk_ref, wv_ref, wg_ref, wc_ref,
                   q_ref, k_ref, v_ref, gate_ref, c_ref, *, qscale):
    g, hg, tq = NSA_GROUPS, NSA_HG, NSA_TQ
    xb = x_ref[...].astype(BF16)
    q_t = _dot_nt(wq_ref[...], xb) * qscale
    gate_t = jax.nn.sigmoid(_dot_nt(wg_ref[...], xb))
    for gg in range(g):
        for h in range(hg):
            head = gg * hg + h
            q_ref[0, gg, 0, :, h * tq:(h + 1) * tq] = q_t[head * NSA_DKP:(head + 1) * NSA_DKP].astype(BF16)
            r0 = head * NSA_GATE_ROWS
            gate_ref[0, gg, 0, :, h * tq:(h + 1) * tq] = gate_t[r0:r0 + N_BRANCH]
    k = _dot(xb, wk_ref[...])
    v_t = _dot_nt(wv_ref[...], xb).astype(BF16)
    per = tq // SEL_BLOCK
    chunk = pl.program_id(0) % (NSA_SEL_ROWS // per)
    row = lax.broadcasted_iota(jnp.int32, (tq, NSA_DKP), 0)
    col = lax.broadcasted_iota(jnp.int32, (tq, NSA_DKP), 1)
    blk_flag = jnp.where(col - NSA_DK == per * chunk + row // SEL_BLOCK, 1.0, 0.0)
    ones_rows = jnp.where(lax.broadcasted_iota(jnp.int32, (NSA_VROWS - NSA_DV, tq), 0) == 0, 1.0, 0.0).astype(BF16)
    for gg in range(g):
        for br in range(2):
            kb = k[:, (br * g + gg) * NSA_DKP:(br * g + gg + 1) * NSA_DKP]
            if br == 0:
                kb = kb + blk_flag
            k_ref[0, gg, br, 0] = kb.astype(BF16)
            v_ref[0, gg, br, 0, :NSA_DV, :] = v_t[(br * g + gg) * NSA_DV:(br * g + gg + 1) * NSA_DV]
            v_ref[0, gg, br, 0, NSA_DV:, :] = ones_rows
    c_ref[...] = _dot(xb, wc_ref[...]).astype(BF16)


def _nsa_proj(x2, b, s, wq_t, wk, wv_t, wg_t, wc):
    g, hg, tq, L = NSA_GROUPS, NSA_HG, NSA_TQ, NSA_LANES
    nqt = s // tq
    assert nqt % (NSA_SEL_ROWS * SEL_BLOCK // tq) == 0 and (s // SEL_BLOCK) % NSA_SEL_ROWS == 0
    t, d = x2.shape
    full = lambda i: (0, 0)
    tile5 = lambda i: (i // nqt, 0, i % nqt, 0, 0)
    tile6 = lambda i: (i // nqt, 0, 0, i % nqt, 0, 0)
    body = functools.partial(_nsa_proj_body, qscale=(NSA_DK ** -0.5) * LOG2E)
    return pl.pallas_call(
        body,
        out_shape=(jax.ShapeDtypeStruct((b, g, nqt, NSA_DKP, L), BF16),
                   jax.ShapeDtypeStruct((b, g, 2, nqt, NSA_KC, NSA_DKP), BF16),
                   jax.ShapeDtypeStruct((b, g, 2, nqt, NSA_VROWS, NSA_KC), BF16),
                   jax.ShapeDtypeStruct((b, g, nqt, N_BRANCH, L), F32),
                   jax.ShapeDtypeStruct((t, wc.shape[1]), BF16)),
        grid=(t // tq,),
        in_specs=[pl.BlockSpec((tq, d), lambda i: (i, 0)),
                  pl.BlockSpec(wq_t.shape, full), pl.BlockSpec(wk.shape, full), pl.BlockSpec(wv_t.shape, full),
                  pl.BlockSpec(wg_t.shape, full), pl.BlockSpec(wc.shape, full)],
        out_specs=(pl.BlockSpec((1, g, 1, NSA_DKP, L), tile5),
                   pl.BlockSpec((1, g, 2, 1, NSA_KC, NSA_DKP), tile6),
                   pl.BlockSpec((1, g, 2, 1, NSA_VROWS, NSA_KC), tile6),
                   pl.BlockSpec((1, g, 1, N_BRANCH, L), tile5),
                   pl.BlockSpec((tq, wc.shape[1]), lambda i: (i, 0))),
        compiler_params=_cparams(("parallel",)),
        name="nsa_proj",
    )(x2, wq_t, wk, wv_t, wg_t, wc)


def _nsa_mixer(x2, b, s, w_in, pe_k, w1_k, w2_k, pe_v, w1_v, w2_v, rel_bias):
    assert NSA_TQ == NSA_KC
    t, d = x2.shape
    h, g, hg, dk, dv = NSA_HEADS, NSA_GROUPS, NSA_HG, NSA_DK, NSA_DV
    nb = s // SEL_BLOCK
    sizes = [h * dk, g * dk, g * dv, g * dk, g * dv, g * dk, g * dv, h * N_BRANCH]
    c = [0] + [int(v) for v in np.cumsum(sizes)]
    cols = [w_in[:, c[i]:c[i + 1]] for i in range(len(sizes))]
    w_q, w_kc, w_vc, w_ks, w_vs, w_kw, w_vw, w_gate = cols

    def pad_last(a, n):
        return jnp.pad(a, [(0, 0)] * (a.ndim - 1) + [(0, n - a.shape[-1])])

    wq_t = pad_last(w_q.reshape(d, h, dk), NSA_DKP).reshape(d, h * NSA_DKP).T.astype(BF16)
    wk = jnp.concatenate([pad_last(w.reshape(d, g, dk), NSA_DKP).reshape(d, g * NSA_DKP)
                          for w in (w_ks, w_kw)], 1).astype(BF16)
    wv_t = jnp.concatenate([w_vs, w_vw], 1).T.astype(BF16)
    wg_t = pad_last(w_gate.reshape(d, h, N_BRANCH), NSA_GATE_ROWS).reshape(d, h * NSA_GATE_ROWS).T.astype(BF16)
    wc = jnp.concatenate([w_kc, w_vc], 1).astype(BF16)
    q_t, k, v_t, gate, ctok = _nsa_proj(x2, b, s, wq_t, wk, wv_t, wg_t, wc)
    kc_tok = ctok[:, :g * dk].reshape(b, s, g, dk)
    vc_tok = ctok[:, g * dk:].reshape(b, s, g, dv)

    nch = s // CMP_STRIDE

    def unfold(tok, dd):
        ch = tok.reshape(b, nch, CMP_STRIDE, g, dd).transpose(0, 3, 1, 2, 4).reshape(b, g, nch, CMP_STRIDE * dd)
        nxt = jnp.concatenate([ch[:, :, 1:], jnp.zeros_like(ch[:, :, :1])], 2)
        return jnp.concatenate([ch, nxt], -1).reshape(b * g * nch, CMP_LEN * dd)

    k_cmp = _compress(unfold(kc_tok, dk), pe_k, w1_k, pad_last(w2_k, NSA_DKP)).reshape(b, g, nch, NSA_DKP)
    v_cmp = _compress(unfold(vc_tok, dv), pe_v, w1_v, w2_v).reshape(b, g, nch, dv)
    vcmp_t = v_cmp.transpose(0, 1, 3, 2)

    tiles, band = _nsa_tables(rel_bias, s)
    ovt = _overlap_t(nch, nb)
    o = _nsa_attention(q_t, k_cmp, vcmp_t, k, v_t, gate, tiles, band, ovt)
    return o.reshape(t, h * dv)


def _out_ln_router_body(o_ref, w_ref, res_ref, lg_ref, lb_ref, wr_ref, x_ref, xb_ref, logit_ref):
    y = _layer_norm(ALPHA * res_ref[...] + _dot(o_ref[...], w_ref[...]), lg_ref[...], lb_ref[...])
    x_ref[...] = y
    xb_ref[...] = y.astype(BF16)
    x1, x2, x3 = _split3(y)
    w1, w2, w3 = _split3(wr_ref[...])
    acc = _dot(x1, w1)
    acc += _dot(x1, w2) + _dot(x2, w1)
    acc += _dot(x1, w3) + _dot(x2, w2) + _dot(x3, w1)
    logit_ref[...] = acc


def _out_ln_router(o, w_out, res, lg, lb, w_router, *, tm=512):
    m, k = o.shape
    n = w_out.shape[1]
    ne = w_router.shape[1]
    wr = jnp.pad(w_router, ((0, 0), (0, 128 - ne)))
    row = lambda i: (i, 0)
    full = lambda i: (0, 0)
    x, xb, logits = pl.pallas_call(
        _out_ln_router_body,
        out_shape=(jax.ShapeDtypeStruct((m, n), F32), jax.ShapeDtypeStruct((m, n), BF16),
                   jax.ShapeDtypeStruct((m, 128), F32)),
        grid=(m // tm,),
        in_specs=[pl.BlockSpec((tm, k), row), pl.BlockSpec((k, n), full), pl.BlockSpec((tm, n), row),
                  pl.BlockSpec((1, n), full), pl.BlockSpec((1, n), full), pl.BlockSpec((n, 128), full)],
        out_specs=(pl.BlockSpec((tm, n), row), pl.BlockSpec((tm, n), row), pl.BlockSpec((tm, 128), row)),
        compiler_params=_cparams(("parallel",)),
        name="nsa_out_ln_router",
    )(o, w_out, res, lg.reshape(1, n), lb.reshape(1, n), wr)
    return x, xb, logits[:, :ne]


def _dispatch_body(ir_ref, ic_ref, fl_ref, x_ref, rt_ref, o_ref):
    i = pl.program_id(0)
    flag = fl_ref[i]
    tok = ic_ref[i] * MOE_TC + lax.broadcasted_iota(jnp.int32, (MOE_BLK, MOE_TC), 1)
    onehot = jnp.where(rt_ref[...] == tok, 1.0, 0.0).astype(BF16)
    rows = _dot(onehot, x_ref[...])

    @pl.when(flag == 3)
    def _():
        o_ref[...] = rows.astype(o_ref.dtype)

    @pl.when(flag == 1)
    def _():
        o_ref[...] = (o_ref[...].astype(F32) + rows).astype(o_ref.dtype)


def _dispatch(x_bf, row_tok_col, item_r, item_c, flags):
    t, d = x_bf.shape
    r = row_tok_col.shape[0]
    ni = item_r.shape[0]
    gs = pltpu.PrefetchScalarGridSpec(
        num_scalar_prefetch=3, grid=(ni,),
        in_specs=[pl.BlockSpec((MOE_TC, d), lambda i, ir, ic, fl: (ic[i], 0)),
                  pl.BlockSpec((MOE_BLK, 1), lambda i, ir, ic, fl: (ir[i], 0))],
        out_specs=pl.BlockSpec((MOE_BLK, d), lambda i, ir, ic, fl: (ir[i], 0)))
    return pl.pallas_call(
        _dispatch_body, grid_spec=gs,
        out_shape=jax.ShapeDtypeStruct((r, d), BF16),
        compiler_params=_cparams(("arbitrary",)),
        name="moe_dispatch",
    )(item_r, item_c, flags, x_bf, row_tok_col)


def _expert_body(be_ref, x_ref, wg_ref, wu_ref, wd_ref, rw_ref, o_ref, acc_scr):
    f = pl.program_id(1)
    xb = x_ref[...]
    h = jax.nn.silu(_dot(xb, wg_ref[0])) * _dot(xb, wu_ref[0])
    part = _dot(h.astype(BF16), wd_ref[0])

    @pl.when(f == 0)
    def _():
        acc_scr[...] = part

    @pl.when(f > 0)
    def _():
        acc_scr[...] += part

    @pl.when(f == pl.num_programs(1) - 1)
    def _():
        o_ref[...] = (acc_scr[...] * rw_ref[...]).astype(o_ref.dtype)


def _experts(xs, wg, wu, wd, row_w_col, block_expert):
    r, d = xs.shape
    nbk = r // MOE_BLK
    nf = wg.shape[2] // MOE_TF
    gs = pltpu.PrefetchScalarGridSpec(
        num_scalar_prefetch=1, grid=(nbk, nf),
        in_specs=[pl.BlockSpec((MOE_BLK, d), lambda i, f, be: (i, 0)),
                  pl.BlockSpec((1, d, MOE_TF), lambda i, f, be: (be[i], 0, f)),
                  pl.BlockSpec((1, d, MOE_TF), lambda i, f, be: (be[i], 0, f)),
                  pl.BlockSpec((1, MOE_TF, d), lambda i, f, be: (be[i], f, 0)),
                  pl.BlockSpec((MOE_BLK, 1), lambda i, f, be: (i, 0))],
        out_specs=pl.BlockSpec((MOE_BLK, d), lambda i, f, be: (i, 0)),
        scratch_shapes=[pltpu.VMEM((MOE_BLK, d), F32)])
    return pl.pallas_call(
        _expert_body, grid_spec=gs,
        out_shape=jax.ShapeDtypeStruct((r, d), BF16),
        compiler_params=_cparams(("parallel", "arbitrary")),
        name="moe_experts",
    )(block_expert, xs, wg, wu, wd, row_w_col)


def _combine_body(ir_ref, ic_ref, fl_ref, y_ref, rt_ref, x_ref, g_ref, b_ref, o_ref):
    i = pl.program_id(0)
    flag = fl_ref[i]
    tok = ic_ref[i] * MOE_TC + lax.broadcasted_iota(jnp.int32, (MOE_TC, MOE_BLK), 0)
    onehot = jnp.where(rt_ref[0] == tok, 1.0, 0.0).astype(BF16)
    part = _dot(onehot, y_ref[...])

    @pl.when((flag & 3) == 3)
    def _():
        o_ref[...] = part

    @pl.when((flag & 3) == 1)
    def _():
        o_ref[...] += part

    @pl.when((flag & 4) == 4)
    def _():
        o_ref[...] = _layer_norm(ALPHA * x_ref[...] + o_ref[...], g_ref[...], b_ref[...])


def _combine_ln(out_rows, row_tok_lane, item_r, item_c, flags, x, g, b):
    r, d = out_rows.shape
    t = x.shape[0]
    ni = item_r.shape[0]
    vec = pl.BlockSpec((1, d), lambda i, ir, ic, fl: (0, 0))
    gs = pltpu.PrefetchScalarGridSpec(
        num_scalar_prefetch=3, grid=(ni,),
        in_specs=[pl.BlockSpec((MOE_BLK, d), lambda i, ir, ic, fl: (ir[i], 0)),
                  pl.BlockSpec((1, 1, MOE_BLK), lambda i, ir, ic, fl: (ir[i], 0, 0)),
                  pl.BlockSpec((MOE_TC, d), lambda i, ir, ic, fl: (ic[i], 0)),
                  vec, vec],
        out_specs=pl.BlockSpec((MOE_TC, d), lambda i, ir, ic, fl: (ic[i], 0)))
    return pl.pallas_call(
        _combine_body, grid_spec=gs,
        out_shape=jax.ShapeDtypeStruct((t, d), F32),
        compiler_params=_cparams(("arbitrary",)),
        name="moe_combine_ln",
    )(item_r, item_c, flags, out_rows, row_tok_lane, x, g.reshape(1, d), b.reshape(1, d))


def _moe_plan(top_idx, wts, t):
    e, blk, tc = N_EXPERTS, MOE_BLK, MOE_TC
    a = t * TOP_K
    i32 = jnp.int32
    exp_flat = top_idx.reshape(a).astype(i32)
    tok_flat = jnp.arange(a, dtype=i32) // TOP_K
    _, tok_sorted, w_sorted = lax.sort((exp_flat, tok_flat, wts.reshape(a)), num_keys=1, is_stable=True)
    counts = jnp.sum((exp_flat[:, None] == jnp.arange(e, dtype=i32)[None, :]).astype(i32), 0)
    padded = ((counts + blk - 1) // blk) * blk
    grp_start = jnp.cumsum(counts) - counts
    pad_end = jnp.cumsum(padded)
    pad_start = pad_end - padded
    nbk = a // blk + e
    r = nbk * blk
    tok_ext = jnp.concatenate([tok_sorted, jnp.full((r - a,), -1, i32)])
    w_ext = jnp.concatenate([w_sorted, jnp.zeros((r - a,), F32)])
    rows = jnp.arange(r, dtype=i32)
    row_tok = jnp.full((r,), -1, i32)
    row_w = jnp.zeros((r,), F32)
    for ee in range(e):
        inside = (rows >= pad_start[ee]) & (rows < pad_start[ee] + counts[ee])
        shift = pad_start[ee] - grp_start[ee]
        row_tok = jnp.where(inside, jnp.roll(tok_ext, shift), row_tok)
        row_w = jnp.where(inside, jnp.roll(w_ext, shift), row_w)
    blk_first = jnp.arange(nbk, dtype=i32) * blk
    block_expert = jnp.minimum(jnp.sum((pad_end[None, :] <= blk_first[:, None]).astype(i32), 1), e - 1)

    rt = row_tok.reshape(nbk, blk)
    valid = rt >= 0
    t_lo = jnp.min(jnp.where(valid, rt, t), 1)
    t_hi = jnp.max(rt, 1)
    has = t_hi >= 0
    c_lo = jnp.where(has, t_lo // tc, 0)
    c_hi = jnp.where(has, t_hi // tc, 0)
    n_it = c_hi - c_lo + 1
    off_end = jnp.cumsum(n_it)
    off_start = off_end - n_it
    total = off_end[-1]
    ni = nbk + e * (t // tc)
    idx = jnp.arange(ni, dtype=i32)
    ok = idx < total
    ir = jnp.minimum(jnp.sum((off_end[None, :] <= idx[:, None]).astype(i32), 1), nbk - 1)
    ic = jnp.where(ok, c_lo[ir] + idx - off_start[ir], c_hi[nbk - 1]).astype(i32)
    first = ok & (idx == off_start[ir])
    d_flags = ok.astype(i32) + 2 * first.astype(i32)

    key = jnp.where(ok, ic * nbk + ir, jnp.iinfo(jnp.int32).max)
    perm = jnp.argsort(key)
    ok2 = ok[perm]
    last = total - 1
    cr = jnp.where(ok2, ir[perm], ir[perm][last]).astype(i32)
    cc = jnp.where(ok2, ic[perm], ic[perm][last]).astype(i32)
    first2 = ok2 & jnp.concatenate([jnp.ones((1,), bool), cc[1:] != cc[:-1]])
    last2 = ok2 & jnp.concatenate([(cc[1:] != cc[:-1]) | ~ok2[1:], jnp.ones((1,), bool)])
    c_flags = ok2.astype(i32) + 2 * first2.astype(i32) + 4 * last2.astype(i32)
    return dict(row_tok=row_tok, row_w=row_w, block_expert=block_expert,
                d_items=(ir, ic, d_flags), c_items=(cr, cc, c_flags), nbk=nbk)


def _moe_ln(x2, x_bf, logits, wg, wu, wd, ln_g, ln_b):
    t, d = x2.shape
    top_val, top_idx = lax.top_k(logits, TOP_K)
    wts = jax.nn.softmax(top_val, -1)
    plan = _moe_plan(top_idx, wts, t)
    nbk = plan["nbk"]
    xs = _dispatch(x_bf, plan["row_tok"].reshape(-1, 1), *plan["d_items"])
    out_rows = _experts(xs, wg.astype(BF16), wu.astype(BF16), wd.astype(BF16),
                        plan["row_w"].reshape(-1, 1), plan["block_expert"])
    return _combine_ln(out_rows, plan["row_tok"].reshape(nbk, 1, MOE_BLK), *plan["c_items"], x2, ln_g, ln_b)


def _forward(x, mla_w_in, mla_q_norm, mla_w_q_up, mla_kv_norm, mla_w_kv_up, mla_w_out, nsa_w_in,
             nsa_cmp_pe_k, nsa_cmp_w1_k, nsa_cmp_w2_k, nsa_cmp_pe_v, nsa_cmp_w1_v, nsa_cmp_w2_v,
             nsa_w_out, rel_bias, ffn_w_gate, ffn_w_up, ffn_w_down, moe_w_router, moe_w_gate,
             moe_w_up, moe_w_down, ln_mix_g, ln_mix_b, ln_ffn_g, ln_ffn_b):
    b, s, d = x.shape
    x2 = x.reshape(b * s, d)
    o = _mla_mixer(x2, b, s, mla_w_in[0], mla_q_norm[0], mla_w_q_up[0], mla_kv_norm[0], mla_w_kv_up[0])
    x2 = _linear(o, mla_w_out[0].astype(BF16), tm=1024, tn=d, out_dtype=F32,
                 ln=(x2, ln_mix_g[0], ln_mix_b[0]), name="mla_out_ln")
    x2 = _ffn_ln(x2, ffn_w_gate[0].astype(BF16), ffn_w_up[0].astype(BF16), ffn_w_down[0].astype(BF16),
                 ln_ffn_g[0], ln_ffn_b[0])
    o = _nsa_mixer(x2, b, s, nsa_w_in[0], nsa_cmp_pe_k[0], nsa_cmp_w1_k[0], nsa_cmp_w2_k[0],
                   nsa_cmp_pe_v[0], nsa_cmp_w1_v[0], nsa_cmp_w2_v[0], rel_bias)
    x2, x_bf, logits = _out_ln_router(o, nsa_w_out[0].astype(BF16), x2, ln_mix_g[1], ln_mix_b[1],
                                      moe_w_router[0])
    x2 = _moe_ln(x2, x_bf, logits, moe_w_gate[0], moe_w_up[0], moe_w_down[0], ln_ffn_g[1], ln_ffn_b[1])
    return x2.reshape(b, s, d)


@jax.jit
def kernel(x, mla_w_in, mla_q_norm, mla_w_q_up, mla_kv_norm, mla_w_kv_up, mla_w_out, nsa_w_in,
           nsa_cmp_pe_k, nsa_cmp_w1_k, nsa_cmp_w2_k, nsa_cmp_pe_v, nsa_cmp_w1_v, nsa_cmp_w2_v,
           nsa_w_out, rel_bias, ffn_w_gate, ffn_w_up, ffn_w_down, moe_w_router, moe_w_gate,
           moe_w_up, moe_w_down, ln_mix_g, ln_mix_b, ln_ffn_g, ln_ffn_b):
    return _forward(x, mla_w_in, mla_q_norm, mla_w_q_up, mla_kv_norm, mla_w_kv_up, mla_w_out, nsa_w_in,
                    nsa_cmp_pe_k, nsa_cmp_w1_k, nsa_cmp_w2_k, nsa_cmp_pe_v, nsa_cmp_w1_v, nsa_cmp_w2_v,
                    nsa_w_out, rel_bias, ffn_w_gate, ffn_w_up, ffn_w_down, moe_w_router, moe_w_gate,
                    moe_w_up, moe_w_down, ln_mix_g, ln_mix_b, ln_ffn_g, ln_ffn_b)
```

```python
import functools
import math

import numpy as np
import jax
import jax.numpy as jnp
from jax import lax
from jax.experimental import pallas as pl
from jax.experimental.pallas import tpu as pltpu

F32 = jnp.float32
BF16 = jnp.bfloat16

D_MODEL = 1024
DEPTH = 2

MLA_HEADS = 8
MLA_Q_RANK = 512
MLA_KV_RANK = 256
MLA_NOPE = 128
MLA_ROPE = 64
MLA_V = 128
ROPE_THETA = 10000.0

NSA_HEADS = 16
NSA_GROUPS = 4
NSA_HG = NSA_HEADS // NSA_GROUPS
NSA_DK = 96
NSA_DV = 64
CMP_LEN = 32
CMP_STRIDE = 16
SEL_BLOCK = 64
SEL_TOPN = 16
WINDOW = 512
N_BRANCH = 3
FORCE = 1e6

REL_BUCKETS = 32
REL_MAX_DIST = 128

D_FF = 2816
N_EXPERTS = 8
TOP_K = 2
D_FF_EXPERT = 3584

LN_EPS = 1e-5
RMS_EPS = 1e-6

ALPHA = (2.0 * DEPTH) ** 0.25

NEG = -1e30
NEG_TEST = -1e29

V7X_VMEM_LIMIT = 56 * 1024 * 1024
RING_PAD_ROWS = 8

LOG2E = 1.4426950408889634

NSA_TQ = 256
NSA_LANES = NSA_HG * NSA_TQ
NSA_KC = 256
NSA_DKP = 128
NSA_GATE_ROWS = 8
NSA_BAND = 24
NSA_SEL_ROWS = 16
NSA_VROWS = NSA_DV + 16
NSA_HEAD_SLOTS = 4
NSA_FAR_UNROLL = 2
NSA_TILES_PER_VARIANT = 8

MOE_BLK = 512
MOE_TC = 512
MOE_TF = 1792


def _cparams(sem, vmem=V7X_VMEM_LIMIT):
    return pltpu.CompilerParams(dimension_semantics=sem, vmem_limit_bytes=vmem)


def _layer_norm(r, g, b):
    mu = jnp.mean(r, -1, keepdims=True)
    d = r - mu
    var = jnp.mean(d * d, -1, keepdims=True)
    return d * lax.rsqrt(var + LN_EPS) * g + b


def _rms_norm(x, g):
    return x * lax.rsqrt(jnp.mean(x * x, -1, keepdims=True) + RMS_EPS) * g


def _split3(a):
    a1 = a.astype(BF16)
    r1 = a - a1.astype(F32)
    a2 = r1.astype(BF16)
    a3 = (r1 - a2.astype(F32)).astype(BF16)
    return a1, a2, a3


def _dot(a, b):
    return jnp.dot(a, b, preferred_element_type=F32)


def _dot_nt(a, b):
    return lax.dot_general(a, b, (((1,), (1,)), ((), ())), preferred_element_type=F32)


def _linear_body(*refs, has_ln):
    it = iter(refs)
    x_ref = next(it)
    w_ref = next(it)
    if has_ln:
        res_ref, lg_ref, lb_ref = next(it), next(it), next(it)
    o_ref = next(it)
    acc = _dot(x_ref[...].astype(BF16), w_ref[...])
    if has_ln:
        acc = _layer_norm(ALPHA * res_ref[...] + acc, lg_ref[...], lb_ref[...])
    o_ref[...] = acc.astype(o_ref.dtype)


def _linear(x, w, *, tm, tn, out_dtype, ln=None, name):
    m, k = x.shape
    n = w.shape[1]
    assert m % tm == 0 and n % tn == 0
    in_specs = [pl.BlockSpec((tm, k), lambda i, j: (i, 0)),
                pl.BlockSpec((k, tn), lambda i, j: (0, j))]
    args = [x, w]
    if ln is not None:
        assert tn == n
        res, lg, lb = ln
        in_specs += [pl.BlockSpec((tm, n), lambda i, j: (i, 0)),
                     pl.BlockSpec((1, n), lambda i, j: (0, 0)),
                     pl.BlockSpec((1, n), lambda i, j: (0, 0))]
        args += [res, lg.reshape(1, n), lb.reshape(1, n)]
    return pl.pallas_call(
        functools.partial(_linear_body, has_ln=ln is not None),
        out_shape=jax.ShapeDtypeStruct((m, n), out_dtype),
        grid=(m // tm, n // tn),
        in_specs=in_specs,
        out_specs=pl.BlockSpec((tm, tn), lambda i, j: (i, j)),
        compiler_params=_cparams(("parallel", "arbitrary")),
        name=name,
    )(*args)


def _rope_tables(s):
    half = MLA_ROPE // 2
    freq = ROPE_THETA ** (-jnp.arange(half, dtype=F32) / half)
    ang = jnp.arange(s).astype(F32)[:, None] * freq[None, :]
    cos, sin = jnp.cos(ang), jnp.sin(ang)
    return jnp.concatenate([cos, cos], -1), jnp.concatenate([-sin, sin], -1)


MLA_DQ = MLA_NOPE + MLA_ROPE
MLA_QROWS = MLA_NOPE + 2 * MLA_ROPE
MLA_T = 512
MLA_HPS = 2
MLA_SUB = 256


def _mla_q_body(lat_ref, g_ref, w_ref, cos_ref, sin_ref, o_ref, *, qscale):
    xn = _rms_norm(lat_ref[...], g_ref[...]).astype(BF16)
    y = _dot_nt(w_ref[...], xn)
    cos, sin = cos_ref[...], sin_ref[...]
    for h in range(MLA_HEADS):
        r0 = h * MLA_QROWS
        o_ref[0, h, :MLA_NOPE, :] = (y[r0:r0 + MLA_NOPE] * qscale).astype(BF16)
        a = y[r0 + MLA_NOPE:r0 + MLA_DQ]
        bb = y[r0 + MLA_DQ:r0 + MLA_QROWS]
        o_ref[0, h, MLA_NOPE:, :] = ((a * cos + bb * sin) * qscale).astype(BF16)


def _mla_q_proj(lat, gain, w_t, cos_t, sin_t, b, s, *, tm=MLA_T):
    ns = s // tm
    body = functools.partial(_mla_q_body, qscale=(MLA_DQ ** -0.5) * LOG2E)
    return pl.pallas_call(
        body,
        out_shape=jax.ShapeDtypeStruct((b, MLA_HEADS, MLA_DQ, s), BF16),
        grid=(b * ns,),
        in_specs=[pl.BlockSpec((tm, MLA_Q_RANK), lambda i: (i, 0)),
                  pl.BlockSpec((1, MLA_Q_RANK), lambda i: (0, 0)),
                  pl.BlockSpec(w_t.shape, lambda i: (0, 0)),
                  pl.BlockSpec((MLA_ROPE, tm), lambda i: (0, i % ns)),
                  pl.BlockSpec((MLA_ROPE, tm), lambda i: (0, i % ns))],
        out_specs=pl.BlockSpec((1, MLA_HEADS, MLA_DQ, tm), lambda i: (i // ns, 0, 0, i % ns)),
        compiler_params=_cparams(("parallel",)),
        name="mla_q_proj",
    )(lat, gain.reshape(1, -1), w_t, cos_t, sin_t)


def _mla_kv_body(lat_ref, g_ref, wk_ref, wvt_ref, kr_ref, cos_ref, sin_ref, k_ref, vt_ref):
    xn = _rms_norm(lat_ref[...], g_ref[...]).astype(BF16)
    kn = _dot(xn, wk_ref[...]).astype(BF16)
    vt = _dot_nt(wvt_ref[...], xn).astype(BF16)
    kr = kr_ref[...]
    rot = (kr[:, :MLA_ROPE] * cos_ref[...] + kr[:, MLA_ROPE:] * sin_ref[...]).astype(BF16)
    for h in range(MLA_HEADS):
        k_ref[0, h, :, :MLA_NOPE] = kn[:, h * MLA_NOPE:(h + 1) * MLA_NOPE]
        k_ref[0, h, :, MLA_NOPE:] = rot
        for kk in range(MLA_T // MLA_SUB):
            vt_ref[0, h, kk] = vt[h * MLA_V:(h + 1) * MLA_V, kk * MLA_SUB:(kk + 1) * MLA_SUB]


def _mla_kv_proj(lat, gain, wk, wv_t, cosx, sinx, b, s, *, tm=MLA_T):
    ns = s // tm
    nsub = tm // MLA_SUB
    return pl.pallas_call(
        _mla_kv_body,
        out_shape=(jax.ShapeDtypeStruct((b, MLA_HEADS, s, MLA_DQ), BF16),
                   jax.ShapeDtypeStruct((b, MLA_HEADS, ns * nsub, MLA_V, MLA_SUB), BF16)),
        grid=(b * ns,),
        in_specs=[pl.BlockSpec((tm, MLA_KV_RANK), lambda i: (i, MLA_Q_RANK // MLA_KV_RANK)),
                  pl.BlockSpec((1, MLA_KV_RANK), lambda i: (0, 0)),
                  pl.BlockSpec(wk.shape, lambda i: (0, 0)),
                  pl.BlockSpec(wv_t.shape, lambda i: (0, 0)),
                  pl.BlockSpec((tm, 2 * MLA_ROPE),
                               lambda i: (i, (MLA_Q_RANK + MLA_KV_RANK) // (2 * MLA_ROPE))),
                  pl.BlockSpec((tm, MLA_ROPE), lambda i: (i % ns, 0)),
                  pl.BlockSpec((tm, MLA_ROPE), lambda i: (i % ns, 0))],
        out_specs=(pl.BlockSpec((1, MLA_HEADS, tm, MLA_DQ), lambda i: (i // ns, 0, i % ns, 0)),
                   pl.BlockSpec((1, MLA_HEADS, nsub, MLA_V, MLA_SUB), lambda i: (i // ns, 0, i % ns, 0, 0))),
        compiler_params=_cparams(("parallel",)),
        name="mla_kv_proj",
    )(lat, gain.reshape(1, -1), wk, wv_t, lat, cosx, sinx)


def _mla_attn_body(qt_ref, k_ref, vt_ref, o_ref, *scr):
    i = pl.program_id(2)
    sub = MLA_SUB
    nsub = MLA_T // sub
    assert nsub == 2
    chains_all = [(h, ql) for ql in range(nsub) for h in range(MLA_HPS)]
    nch = len(chains_all)
    ring_a, ring_b, scr = scr[:nch], scr[nch:2 * nch], scr[2 * nch:]
    state = {(h, ql): scr[3 * (nsub * h + ql):3 * (nsub * h + ql) + 3]
             for h in range(MLA_HPS) for ql in range(nsub)}
    for m_scr, l_scr, acc_scr in state.values():
        m_scr[...] = jnp.full_like(m_scr, NEG)
        l_scr[...] = jnp.zeros_like(l_scr)
        acc_scr[...] = jnp.zeros_like(acc_scr)

    def issue(sk, ring, chains):
        for idx, (h, ql) in enumerate(chains_all):
            if (h, ql) in chains:
                k = k_ref[0, h, pl.ds(pl.multiple_of(sk * sub, sub), sub), :]
                ring[idx][:sub] = _dot(k, qt_ref[0, h, :, ql * sub:(ql + 1) * sub])
            yield

    def consume(sk, ring, chains, diag_ql=None):
        for idx, (h, ql) in enumerate(chains_all):
            if (h, ql) in chains:
                m_scr, l_scr, acc_scr = state[(h, ql)]
                s = ring[idx][:sub]
                if ql == diag_ql:
                    key = lax.broadcasted_iota(jnp.int32, s.shape, 0)
                    qry = lax.broadcasted_iota(jnp.int32, s.shape, 1)
                    s = jnp.where(key <= qry, s, NEG)
                m_old = m_scr[...]
                m_new = jnp.maximum(m_old, jnp.max(s, 0, keepdims=True))
                a = jnp.exp2(m_old - m_new)
                p = jnp.exp2(s - m_new)
                l_scr[...] = a * l_scr[...] + jnp.sum(p, 0, keepdims=True)
                acc_scr[...] = a * acc_scr[...] + _dot(vt_ref[0, h, sk], p.astype(BF16))
                m_scr[...] = m_new
            yield

    def interleave(*gens):
        for _ in zip(*gens):
            pass

    interleave(issue(0, ring_a, chains_all))

    def trip(j, carry):
        c0 = 2 * j
        interleave(issue(c0 + 1, ring_b, chains_all), consume(c0, ring_a, chains_all))
        interleave(issue(c0 + 2, ring_a, chains_all), consume(c0 + 1, ring_b, chains_all))
        return carry

    lax.fori_loop(0, i, trip, 0)
    upper = [c for c in chains_all if c[1] == 1]
    interleave(issue(2 * i + 1, ring_b, upper), consume(2 * i, ring_a, chains_all, diag_ql=0))
    interleave(consume(2 * i + 1, ring_b, upper, diag_ql=1))
    for (h, ql), (m_scr, l_scr, acc_scr) in state.items():
        o_t = acc_scr[...] * (1.0 / jnp.maximum(l_scr[...], 1e-30))
        o_ref[0, ql * sub:(ql + 1) * sub, h * MLA_V:(h + 1) * MLA_V] = o_t.T.astype(o_ref.dtype)


def _mla_attention(q_t, k, v_t, b, s):
    tq = MLA_T
    nq = s // tq
    hp = MLA_HPS
    nsub = tq // MLA_SUB
    sub_state = [pltpu.VMEM((1, MLA_SUB), F32), pltpu.VMEM((1, MLA_SUB), F32), pltpu.VMEM((MLA_V, MLA_SUB), F32)]
    ring = hp * nsub * [pltpu.VMEM((MLA_SUB + RING_PAD_ROWS, MLA_SUB), F32)]
    return pl.pallas_call(
        _mla_attn_body,
        out_shape=jax.ShapeDtypeStruct((b, s, MLA_HEADS * MLA_V), BF16),
        grid=(b, MLA_HEADS // hp, nq),
        in_specs=[pl.BlockSpec((1, hp, MLA_DQ, tq), lambda bb, h, i: (bb, h, 0, i)),
                  pl.BlockSpec((1, hp, s, MLA_DQ), lambda bb, h, i: (bb, h, 0, 0)),
                  pl.BlockSpec((1, hp, nq * nsub, MLA_V, MLA_SUB), lambda bb, h, i: (bb, h, 0, 0, 0))],
        out_specs=pl.BlockSpec((1, tq, hp * MLA_V), lambda bb, h, i: (bb, i, h)),
        scratch_shapes=2 * ring + hp * nsub * sub_state,
        compiler_params=_cparams(("parallel", "parallel", "arbitrary")),
        name="mla_attention",
    )(q_t, k, v_t)


def _mla_mixer(x2, b, s, w_in, q_norm, w_q_up, kv_norm, w_kv_up):
    r0 = MLA_Q_RANK + MLA_KV_RANK
    half = MLA_ROPE // 2
    w_in_ext = jnp.concatenate([w_in, w_in[:, r0 + half:r0 + MLA_ROPE], w_in[:, r0:r0 + half]], 1)
    lat = _linear(x2, w_in_ext.astype(BF16), tm=512, tn=w_in_ext.shape[1], out_dtype=F32, name="mla_in")
    wq = w_q_up.reshape(MLA_Q_RANK, MLA_HEADS, MLA_DQ)
    wr = wq[..., MLA_NOPE:]
    wq = jnp.concatenate([wq, wr[..., half:], wr[..., :half]], -1)
    wq_t = wq.reshape(MLA_Q_RANK, MLA_HEADS * MLA_QROWS).T.astype(BF16)
    wkv = w_kv_up.reshape(MLA_KV_RANK, MLA_HEADS, MLA_NOPE + MLA_V)
    wk = wkv[..., :MLA_NOPE].reshape(MLA_KV_RANK, MLA_HEADS * MLA_NOPE).astype(BF16)
    wv_t = wkv[..., MLA_NOPE:].reshape(MLA_KV_RANK, MLA_HEADS * MLA_V).T.astype(BF16)
    cosx, sinx = _rope_tables(s)
    q_t = _mla_q_proj(lat, q_norm, wq_t, cosx.T, sinx.T, b, s)
    k, v_t = _mla_kv_proj(lat, kv_norm, wk, wv_t, cosx, sinx, b, s)
    o = _mla_attention(q_t, k, v_t, b, s)
    return o.reshape(b * s, MLA_HEADS * MLA_V)


def _ffn_body(x_ref, wg_ref, wu_ref, wd_ref, lg_ref, lb_ref, o_ref, acc_scr):
    f = pl.program_id(1)
    xb = x_ref[...].astype(BF16)
    h = jax.nn.silu(_dot(xb, wg_ref[...])) * _dot(xb, wu_ref[...])
    part = _dot(h.astype(BF16), wd_ref[...])

    @pl.when(f == 0)
    def _():
        acc_scr[...] = part

    @pl.when(f > 0)
    def _():
        acc_scr[...] += part

    @pl.when(f == pl.num_programs(1) - 1)
    def _():
        o_ref[...] = _layer_norm(ALPHA * x_ref[...] + acc_scr[...], lg_ref[...], lb_ref[...])


def _ffn_ln(x, wg, wu, wd, lg, lb, *, tm=512, tf=1408):
    m, d = x.shape
    dff = wg.shape[1]
    assert dff % tf == 0
    vec = pl.BlockSpec((1, d), lambda i, f: (0, 0))
    return pl.pallas_call(
        _ffn_body,
        out_shape=jax.ShapeDtypeStruct((m, d), F32),
        grid=(m // tm, dff // tf),
        in_specs=[pl.BlockSpec((tm, d), lambda i, f: (i, 0)),
                  pl.BlockSpec((d, tf), lambda i, f: (0, f)),
                  pl.BlockSpec((d, tf), lambda i, f: (0, f)),
                  pl.BlockSpec((tf, d), lambda i, f: (f, 0)),
                  vec, vec],
        out_specs=pl.BlockSpec((tm, d), lambda i, f: (i, 0)),
        scratch_shapes=[pltpu.VMEM((tm, d), F32)],
        compiler_params=_cparams(("parallel", "arbitrary")),
        name="ffn_ln",
    )(x, wg, wu, wd, lg.reshape(1, d), lb.reshape(1, d))


def _compress_body(a_ref, pe_ref, w1_ref, w2_ref, o_ref):
    a = (a_ref[...].astype(F32) + pe_ref[...]).astype(BF16)
    h = jax.nn.gelu(_dot(a, w1_ref[...]))
    o_ref[...] = _dot(h.astype(BF16), w2_ref[...]).astype(o_ref.dtype)


def _compress(a, pe, w1, w2, *, tm=512):
    m, k = a.shape
    dh, d = w2.shape
    return pl.pallas_call(
        _compress_body,
        out_shape=jax.ShapeDtypeStruct((m, d), BF16),
        grid=(m // tm,),
        in_specs=[pl.BlockSpec((tm, k), lambda i: (i, 0)),
                  pl.BlockSpec((1, k), lambda i: (0, 0)),
                  pl.BlockSpec((k, dh), lambda i: (0, 0)),
                  pl.BlockSpec((dh, d), lambda i: (0, 0))],
        out_specs=pl.BlockSpec((tm, d), lambda i: (i, 0)),
        compiler_params=_cparams(("parallel",)),
        name="nsa_compress",
    )(a, pe.reshape(1, k), w1.reshape(k, dh).astype(BF16), w2.astype(BF16))


def _rel_bucket_np(dist):
    n = np.maximum(dist, 0)
    max_exact = REL_BUCKETS // 2
    nf = np.maximum(n, 1).astype(np.float32)
    large = max_exact + (np.log(nf / np.float32(max_exact)) / np.float32(math.log(REL_MAX_DIST / max_exact))
                         * np.float32(REL_BUCKETS - max_exact)).astype(np.int32)
    large = np.minimum(large, REL_BUCKETS - 1)
    return np.where(n < max_exact, n, large).astype(np.int32)


def _nsa_tables(rel_bias, s):
    g, hg, tq, kc = NSA_GROUPS, NSA_HG, NSA_TQ, NSA_KC
    assert tq == kc and tq % CMP_STRIDE == 0 and WINDOW == 2 * kc
    assert np.all(_rel_bucket_np(np.arange(tq // 2 - 15, s + tq)) == REL_BUCKETS - 1)
    rb = rel_bias.reshape(REL_BUCKETS, g, hg) * LOG2E

    def tile(base, step, rows, valid):
        p = tq + step * rows
        k = np.arange(p)
        k = np.where(k < p - step * (rows - 1), k, k - p)
        d = base + k
        vec = jnp.where(valid(d)[:, None, None], rb[_rel_bucket_np(d)], NEG)
        vec = vec.transpose(1, 2, 0)
        flat = jnp.tile(vec, (1, 1, rows))[..., :rows * (p - step)]
        mat = flat.reshape(g, hg, rows, p - step)[..., :tq]
        return mat.transpose(0, 2, 1, 3).reshape(g, rows, hg * tq)

    causal = lambda d: d >= 0
    far = jnp.broadcast_to(rb[REL_BUCKETS - 1][:, None, :, None], (g, 1, hg, tq)).reshape(g, 1, hg * tq)
    rel = lambda x: jnp.where(x > NEG_TEST, x - far, NEG)
    tiles = jnp.stack([rel(tile(0, 1, kc, causal)), rel(tile(tq, 1, kc, causal)),
                       rel(tile(2 * tq, 1, kc, lambda d: d < WINDOW)),
                       jnp.zeros((g, kc, hg * tq), F32)], 1)
    band = jnp.stack([rel(tile(8 * CMP_STRIDE - CMP_LEN + 1, CMP_STRIDE, NSA_BAND, causal)),
                      rel(tile(-(CMP_LEN - 1), CMP_STRIDE, NSA_BAND, causal))], 1)
    return tiles, band


def _overlap_t(nc_pad, nb):
    n = np.arange(nc_pad)[None, :]
    jb = np.arange(nb)[:, None]
    cstart = n * CMP_STRIDE
    cend = cstart + CMP_LEN - 1
    sstart = jb * SEL_BLOCK
    ov = (cstart <= sstart + SEL_BLOCK - 1) & (cend >= sstart) & (n < nc_pad - 1)
    return jnp.asarray(ov.astype(np.float32), BF16)


def _nsa_body(qt_ref, kc_ref, vct_ref, k_ref, vt_ref, gate_ref, tiles_ref,
              band_ref, ovt_ref, o_ref, s_scr, sel_scr, qa_scr, oc_scr, *scr, nb, nqt):
    t = pl.program_id(2)
    ring_a, ring_b, state = scr[:NSA_HG], scr[NSA_HG:2 * NSA_HG], scr[2 * NSA_HG:]
    nkc = k_ref.shape[3]
    L = NSA_LANES
    q_t = qt_ref[0, 0, 0]
    ncp = kc_ref.shape[2]
    per = NSA_KC // SEL_BLOCK
    blocks_per_tile = NSA_TQ // CMP_STRIDE

    n_slab = nb // NSA_SEL_ROWS

    n_win = jnp.minimum(t, 2) + 1
    n_slots = n_win + t + 1

    def slot_params(c):
        is_win = c < n_win
        j = c - n_win
        is_sel = jnp.logical_and(c >= n_win, j <= t)
        delta = t - j
        br = is_win.astype(jnp.int32)
        kidx = jnp.where(is_win, t - c, jnp.where(is_sel, j, 0))
        sidx = jnp.where(is_sel, j // (NSA_SEL_ROWS // per), jnp.where(is_win, 0, n_slab))
        tidx = jnp.where(is_win, c, jnp.where(jnp.logical_and(is_sel, delta < 2), delta, 3))
        return br, kidx, sidx, tidx

    def issue(c, ring, set_rows=True):
        br, kidx, sidx, _ = slot_params(c)
        k = k_ref[0, 0, br, kidx]
        if set_rows:
            qa_scr[NSA_DK:NSA_DK + NSA_SEL_ROWS, :] = sel_scr[sidx]
        for h in range(NSA_HG):
            ring[h][:NSA_KC] = _dot(k, qa_scr[:, h * NSA_TQ:(h + 1) * NSA_TQ])
            yield

    def consume(c, ring, near):
        br, kidx, _, tidx = slot_params(c)
        vt = vt_ref[0, 0, br, kidx]
        for h in range(NSA_HG):
            m_scr, acc_scr = state[2 * h:2 * h + 2]
            sc = ring[h][:NSA_KC]
            if near:
                sc = sc + tiles_ref[0, tidx, :, h * NSA_TQ:(h + 1) * NSA_TQ]
            m_old = m_scr[br]
            m_new = jnp.maximum(m_old, jnp.max(sc, 0, keepdims=True))
            pp = jnp.exp2(sc - m_new)
            acc_scr[br] = jnp.exp2(m_old - m_new) * acc_scr[br] + _dot(vt, pp.astype(BF16))
            m_scr[br] = m_new
            yield

    def interleave(*gens):
        for _ in zip(*gens):
            pass

    for h in range(NSA_HG):
        m_scr, acc_scr = state[2 * h:2 * h + 2]
        m_scr[...] = jnp.full_like(m_scr, NEG)
        acc_scr[...] = jnp.zeros_like(acc_scr)
    qa_scr[...] = q_t

    def compress_and_select(nrows, nblk):
        s = _dot(kc_ref[0, 0, :nrows], q_t)
        first = (t == 0).astype(jnp.int32)
        bs = pl.multiple_of((blocks_per_tile * t - 8) * (1 - first), 8)
        row = lax.broadcasted_iota(jnp.int32, (nrows, L), 0)
        s_scr[:nrows] = jnp.where(row < bs + NSA_BAND, s, NEG)
        s_scr[pl.ds(bs, NSA_BAND), :] += band_ref[0, first]
        s = s_scr[:nrows]
        m = jnp.max(s, 0, keepdims=True)
        m = jnp.where(m < NEG_TEST, 0.0, m)
        p = jnp.exp2(s - m)
        den = jnp.maximum(jnp.sum(p, 0, keepdims=True), 1e-30)
        p = p * (1.0 / den)
        oc_scr[...] = _dot(vct_ref[0, 0, :, :nrows], p.astype(BF16))

        psum = p[:, 0:NSA_TQ]
        for h in range(1, NSA_HG):
            psum = psum + p[:, h * NSA_TQ:(h + 1) * NSA_TQ]
        p1, p2, p3 = _split3(psum)
        ovt = ovt_ref[:nblk, :nrows]
        imp = _dot(ovt, p1) + _dot(ovt, p2) + _dot(ovt, p3)
        interleave(issue(0, ring_a, set_rows=False))
        blk = lax.broadcasted_iota(jnp.int32, (nblk, NSA_TQ), 0)
        lane = lax.broadcasted_iota(jnp.int32, (nblk, NSA_TQ), 1)
        cur = (NSA_TQ // SEL_BLOCK) * t + lane // SEL_BLOCK
        forced = (blk == 0) | (blk == cur) | (blk == cur - 1)
        v = jnp.where(blk > cur, -FORCE, jnp.where(forced, -jnp.inf, imp))
        blk_f = blk.astype(F32)
        sel = jnp.where(forced, 1.0, 0.0)
        for _ in range(min(SEL_TOPN, nblk) - 3):
            mx = jnp.max(v, 0, keepdims=True)
            idx = jnp.min(jnp.where(v == mx, blk_f, float(nblk)), 0, keepdims=True)
            hit = blk_f == idx
            sel = jnp.where(hit, 1.0, sel)
            v = jnp.where(hit, -jnp.inf, v)
        selneg = jnp.where(sel > 0.5, 0.0, NEG)
        selneg = jnp.concatenate([selneg] * NSA_HG, 1).astype(BF16)
        for u in range(nblk // NSA_SEL_ROWS):
            sel_scr[u] = selneg[NSA_SEL_ROWS * u:NSA_SEL_ROWS * (u + 1), :]

    n_var = -(-nqt // NSA_TILES_PER_VARIANT)
    for i in range(n_var):
        nrows = min(ncp, (i + 1) * NSA_TILES_PER_VARIANT * blocks_per_tile)
        nblk = min(nb, (i + 1) * NSA_TILES_PER_VARIANT * NSA_TQ // SEL_BLOCK)

        @pl.when(t // NSA_TILES_PER_VARIANT == i)
        def _(nrows=nrows, nblk=nblk):
            compress_and_select(nrows, nblk)

    sel_scr[n_slab] = jnp.full((NSA_SEL_ROWS, L), NEG, BF16)

    def make_trip(first_slot, near, pairs=1):
        def trip(i, carry):
            for pr in range(pairs):
                c0 = first_slot + 2 * (pairs * i + pr)
                interleave(issue(c0 + 1, ring_b), consume(c0, ring_a, near))
                interleave(issue(c0 + 2, ring_a), consume(c0 + 1, ring_b, near))
            return carry
        return trip

    far_trips = jnp.maximum(n_slots - 2 - NSA_HEAD_SLOTS, 0) // (2 * NSA_FAR_UNROLL)
    tail_first = NSA_HEAD_SLOTS + 2 * NSA_FAR_UNROLL * far_trips
    tail_trips = (jnp.maximum(n_slots - tail_first, 0) + 1) // 2
    lax.fori_loop(0, NSA_HEAD_SLOTS // 2, make_trip(0, True), 0)
    lax.fori_loop(0, far_trips, make_trip(NSA_HEAD_SLOTS, False, NSA_FAR_UNROLL), 0)
    lax.fori_loop(0, tail_trips, make_trip(tail_first, True), 0)

    gate = gate_ref[0, 0, 0]
    outs = []
    for h in range(NSA_HG):
        sl = slice(h * NSA_TQ, (h + 1) * NSA_TQ)
        acc_scr = state[2 * h + 1]
        o_s = acc_scr[0, :NSA_DV] * (1.0 / jnp.maximum(acc_scr[0, NSA_DV:NSA_DV + 1], 1e-30))
        o_w = acc_scr[1, :NSA_DV] * (1.0 / jnp.maximum(acc_scr[1, NSA_DV:NSA_DV + 1], 1e-30))
        outs.append(gate[0:1, sl] * oc_scr[:, sl] + gate[1:2, sl] * o_s + gate[2:3, sl] * o_w)
    o_ref[0] = jnp.concatenate(outs, 0).T.astype(o_ref.dtype)


def _nsa_attention(q_t, kcmp, vcmp_t, k, v_t, gate, tiles, band, ovt):
    b, g, nqt = q_t.shape[:3]
    ncp = kcmp.shape[2]
    nkc = k.shape[3]
    nb = ovt.shape[0]
    L = NSA_LANES
    body = functools.partial(_nsa_body, nb=nb, nqt=nqt)
    ring = NSA_HG * [pltpu.VMEM((NSA_KC + RING_PAD_ROWS, NSA_TQ), F32)]
    head_state = [pltpu.VMEM((2, 1, NSA_TQ), F32), pltpu.VMEM((2, NSA_VROWS, NSA_TQ), F32)]
    grp = lambda bb, gg, t: (bb, gg, 0, 0)
    grp6 = lambda bb, gg, t: (bb, gg, 0, 0, 0, 0)
    return pl.pallas_call(
        body,
        out_shape=jax.ShapeDtypeStruct((b, nqt * NSA_TQ, g * NSA_HG * NSA_DV), BF16),
        grid=(b, g, nqt),
        in_specs=[pl.BlockSpec((1, 1, 1, NSA_DKP, L), lambda bb, gg, t: (bb, gg, t, 0, 0)),
                  pl.BlockSpec((1, 1, ncp, NSA_DKP), grp),
                  pl.BlockSpec((1, 1, NSA_DV, ncp), grp),
                  pl.BlockSpec((1, 1, 2, nkc, NSA_KC, NSA_DKP), grp6),
                  pl.BlockSpec((1, 1, 2, nkc, NSA_VROWS, NSA_KC), grp6),
                  pl.BlockSpec((1, 1, 1, N_BRANCH, L), lambda bb, gg, t: (bb, gg, t, 0, 0)),
                  pl.BlockSpec((1, 4, NSA_KC, L), lambda bb, gg, t: (gg, 0, 0, 0)),
                  pl.BlockSpec((1, 2, NSA_BAND, L), lambda bb, gg, t: (gg, 0, 0, 0)),
                  pl.BlockSpec(ovt.shape, lambda bb, gg, t: (0, 0))],
        out_specs=pl.BlockSpec((1, NSA_TQ, NSA_HG * NSA_DV), lambda bb, gg, t: (bb, t, gg)),
        scratch_shapes=[pltpu.VMEM((ncp, L), F32),
                        pltpu.VMEM((nb // NSA_SEL_ROWS + 1, NSA_SEL_ROWS, L), BF16),
                        pltpu.VMEM((NSA_DKP, L), BF16),
                        pltpu.VMEM((NSA_DV, L), F32)]
        + 2 * ring + NSA_HG * head_state,
        compiler_params=_cparams(("parallel", "parallel", "arbitrary")),
        name="nsa_attention",
    )(q_t, kcmp, vcmp_t, k, v_t, gate, tiles, band, ovt)


def _nsa_proj_body(x_ref, wq_ref, wk_ref, wv_ref, wg_ref, wc_ref,
                   q_ref, k_ref, v_ref, gate_ref, c_ref, *, qscale):
    g, hg, tq = NSA_GROUPS, NSA_HG, NSA_TQ
    xb = x_ref[...].astype(BF16)
    q_t = _dot_nt(wq_ref[...], xb) * qscale
    gate_t = jax.nn.sigmoid(_dot_nt(wg_ref[...], xb))
    for gg in range(g):
        for h in range(hg):
            head = gg * hg + h
            q_ref[0, gg, 0, :, h * tq:(h + 1) * tq] = q_t[head * NSA_DKP:(head + 1) * NSA_DKP].astype(BF16)
            r0 = head * NSA_GATE_ROWS
            gate_ref[0, gg, 0, :, h * tq:(h + 1) * tq] = gate_t[r0:r0 + N_BRANCH]
    k = _dot(xb, wk_ref[...])
    v_t = _dot_nt(wv_ref[...], xb).astype(BF16)
    per = tq // SEL_BLOCK
    chunk = pl.program_id(0) % (NSA_SEL_ROWS // per)
    row = lax.broadcasted_iota(jnp.int32, (tq, NSA_DKP), 0)
    col = lax.broadcasted_iota(jnp.int32, (tq, NSA_DKP), 1)
    blk_flag = jnp.where(col - NSA_DK == per * chunk + row // SEL_BLOCK, 1.0, 0.0)
    ones_rows = jnp.where(lax.broadcasted_iota(jnp.int32, (NSA_VROWS - NSA_DV, tq), 0) == 0, 1.0, 0.0).astype(BF16)
    for gg in range(g):
        for br in range(2):
            kb = k[:, (br * g + gg) * NSA_DKP:(br * g + gg + 1) * NSA_DKP]
            if br == 0:
                kb = kb + blk_flag
            k_ref[0, gg, br, 0] = kb.astype(BF16)
            v_ref[0, gg, br, 0, :NSA_DV, :] = v_t[(br * g + gg) * NSA_DV:(br * g + gg + 1) * NSA_DV]
            v_ref[0, gg, br, 0, NSA_DV:, :] = ones_rows
    c_ref[...] = _dot(xb, wc_ref[...]).astype(BF16)


def _nsa_proj(x2, b, s, wq_t, wk, wv_t, wg_t, wc):
    g, hg, tq, L = NSA_GROUPS, NSA_HG, NSA_TQ, NSA_LANES
    nqt = s // tq
    assert nqt % (NSA_SEL_ROWS * SEL_BLOCK // tq) == 0 and (s // SEL_BLOCK) % NSA_SEL_ROWS == 0
    t, d = x2.shape
    full = lambda i: (0, 0)
    tile5 = lambda i: (i // nqt, 0, i % nqt, 0, 0)
    tile6 = lambda i: (i // nqt, 0, 0, i % nqt, 0, 0)
    body = functools.partial(_nsa_proj_body, qscale=(NSA_DK ** -0.5) * LOG2E)
    return pl.pallas_call(
        body,
        out_shape=(jax.ShapeDtypeStruct((b, g, nqt, NSA_DKP, L), BF16),
                   jax.ShapeDtypeStruct((b, g, 2, nqt, NSA_KC, NSA_DKP), BF16),
                   jax.ShapeDtypeStruct((b, g, 2, nqt, NSA_VROWS, NSA_KC), BF16),
                   jax.ShapeDtypeStruct((b, g, nqt, N_BRANCH, L), F32),
                   jax.ShapeDtypeStruct((t, wc.shape[1]), BF16)),
        grid=(t // tq,),
        in_specs=[pl.BlockSpec((tq, d), lambda i: (i, 0)),
                  pl.BlockSpec(wq_t.shape, full), pl.BlockSpec(wk.shape, full), pl.BlockSpec(wv_t.shape, full),
                  pl.BlockSpec(wg_t.shape, full), pl.BlockSpec(wc.shape, full)],
        out_specs=(pl.BlockSpec((1, g, 1, NSA_DKP, L), tile5),
                   pl.BlockSpec((1, g, 2, 1, NSA_KC, NSA_DKP), tile6),
                   pl.BlockSpec((1, g, 2, 1, NSA_VROWS, NSA_KC), tile6),
                   pl.BlockSpec((1, g, 1, N_BRANCH, L), tile5),
                   pl.BlockSpec((tq, wc.shape[1]), lambda i: (i, 0))),
        compiler_params=_cparams(("parallel",)),
        name="nsa_proj",
    )(x2, wq_t, wk, wv_t, wg_t, wc)


def _nsa_mixer(x2, b, s, w_in, pe_k, w1_k, w2_k, pe_v, w1_v, w2_v, rel_bias):
    assert NSA_TQ == NSA_KC
    t, d = x2.shape
    h, g, hg, dk, dv = NSA_HEADS, NSA_GROUPS, NSA_HG, NSA_DK, NSA_DV
    nb = s // SEL_BLOCK
    sizes = [h * dk, g * dk, g * dv, g * dk, g * dv, g * dk, g * dv, h * N_BRANCH]
    c = [0] + [int(v) for v in np.cumsum(sizes)]
    cols = [w_in[:, c[i]:c[i + 1]] for i in range(len(sizes))]
    w_q, w_kc, w_vc, w_ks, w_vs, w_kw, w_vw, w_gate = cols

    def pad_last(a, n):
        return jnp.pad(a, [(0, 0)] * (a.ndim - 1) + [(0, n - a.shape[-1])])

    wq_t = pad_last(w_q.reshape(d, h, dk), NSA_DKP).reshape(d, h * NSA_DKP).T.astype(BF16)
    wk = jnp.concatenate([pad_last(w.reshape(d, g, dk), NSA_DKP).reshape(d, g * NSA_DKP)
                          for w in (w_ks, w_kw)], 1).astype(BF16)
    wv_t = jnp.concatenate([w_vs, w_vw], 1).T.astype(BF16)
    wg_t = pad_last(w_gate.reshape(d, h, N_BRANCH), NSA_GATE_ROWS).reshape(d, h * NSA_GATE_ROWS).T.astype(BF16)
    wc = jnp.concatenate([w_kc, w_vc], 1).astype(BF16)
    q_t, k, v_t, gate, ctok = _nsa_proj(x2, b, s, wq_t, wk, wv_t, wg_t, wc)
    kc_tok = ctok[:, :g * dk].reshape(b, s, g, dk)
    vc_tok = ctok[:, g * dk:].reshape(b, s, g, dv)

    nch = s // CMP_STRIDE

    def unfold(tok, dd):
        ch = tok.reshape(b, nch, CMP_STRIDE, g, dd).transpose(0, 3, 1, 2, 4).reshape(b, g, nch, CMP_STRIDE * dd)
        nxt = jnp.concatenate([ch[:, :, 1:], jnp.zeros_like(ch[:, :, :1])], 2)
        return jnp.concatenate([ch, nxt], -1).reshape(b * g * nch, CMP_LEN * dd)

    k_cmp = _compress(unfold(kc_tok, dk), pe_k, w1_k, pad_last(w2_k, NSA_DKP)).reshape(b, g, nch, NSA_DKP)
    v_cmp = _compress(unfold(vc_tok, dv), pe_v, w1_v, w2_v).reshape(b, g, nch, dv)
    vcmp_t = v_cmp.transpose(0, 1, 3, 2)

    tiles, band = _nsa_tables(rel_bias, s)
    ovt = _overlap_t(nch, nb)
    o = _nsa_attention(q_t, k_cmp, vcmp_t, k, v_t, gate, tiles, band, ovt)
    return o.reshape(t, h * dv)


def _out_ln_router_body(o_ref, w_ref, res_ref, lg_ref, lb_ref, wr_ref, x_ref, xb_ref, logit_ref):
    y = _layer_norm(ALPHA * res_ref[...] + _dot(o_ref[...], w_ref[...]), lg_ref[...], lb_ref[...])
    x_ref[...] = y
    xb_ref[...] = y.astype(BF16)
    x1, x2, _ = _split3(y)
    w1, w2, _ = _split3(wr_ref[...])
    logit_ref[...] = _dot(x1, w1) + (_dot(x1, w2) + _dot(x2, w1))


def _out_ln_router(o, w_out, res, lg, lb, w_router, *, tm=512):
    m, k = o.shape
    n = w_out.shape[1]
    ne = w_router.shape[1]
    wr = jnp.pad(w_router, ((0, 0), (0, 128 - ne)))
    row = lambda i: (i, 0)
    full = lambda i: (0, 0)
    x, xb, logits = pl.pallas_call(
        _out_ln_router_body,
        out_shape=(jax.ShapeDtypeStruct((m, n), F32), jax.ShapeDtypeStruct((m, n), BF16),
                   jax.ShapeDtypeStruct((m, 128), F32)),
        grid=(m // tm,),
        in_specs=[pl.BlockSpec((tm, k), row), pl.BlockSpec((k, n), full), pl.BlockSpec((tm, n), row),
                  pl.BlockSpec((1, n), full), pl.BlockSpec((1, n), full), pl.BlockSpec((n, 128), full)],
        out_specs=(pl.BlockSpec((tm, n), row), pl.BlockSpec((tm, n), row), pl.BlockSpec((tm, 128), row)),
        compiler_params=_cparams(("parallel",)),
        name="nsa_out_ln_router",
    )(o, w_out, res, lg.reshape(1, n), lb.reshape(1, n), wr)
    return x, xb, logits[:, :ne]


def _dispatch_body(ir_ref, ic_ref, fl_ref, x_ref, rt_ref, o_ref):
    i = pl.program_id(0)
    flag = fl_ref[i]
    tok = ic_ref[i] * MOE_TC + lax.broadcasted_iota(jnp.int32, (MOE_BLK, MOE_TC), 1)
    onehot = jnp.where(rt_ref[...] == tok, 1.0, 0.0).astype(BF16)
    rows = _dot(onehot, x_ref[...])

    @pl.when(flag == 3)
    def _():
        o_ref[...] = rows.astype(o_ref.dtype)

    @pl.when(flag == 1)
    def _():
        o_ref[...] = (o_ref[...].astype(F32) + rows).astype(o_ref.dtype)


def _dispatch(x_bf, row_tok_col, item_r, item_c, flags):
    t, d = x_bf.shape
    r = row_tok_col.shape[0]
    ni = item_r.shape[0]
    gs = pltpu.PrefetchScalarGridSpec(
        num_scalar_prefetch=3, grid=(ni,),
        in_specs=[pl.BlockSpec((MOE_TC, d), lambda i, ir, ic, fl: (ic[i], 0)),
                  pl.BlockSpec((MOE_BLK, 1), lambda i, ir, ic, fl: (ir[i], 0))],
        out_specs=pl.BlockSpec((MOE_BLK, d), lambda i, ir, ic, fl: (ir[i], 0)))
    return pl.pallas_call(
        _dispatch_body, grid_spec=gs,
        out_shape=jax.ShapeDtypeStruct((r, d), BF16),
        compiler_params=_cparams(("arbitrary",)),
        name="moe_dispatch",
    )(item_r, item_c, flags, x_bf, row_tok_col)


def _expert_body(be_ref, x_ref, wg_ref, wu_ref, wd_ref, rw_ref, o_ref, acc_scr):
    f = pl.program_id(1)
    xb = x_ref[...]
    h = jax.nn.silu(_dot(xb, wg_ref[0])) * _dot(xb, wu_ref[0])
    part = _dot(h.astype(BF16), wd_ref[0])

    @pl.when(f == 0)
    def _():
        acc_scr[...] = part

    @pl.when(f > 0)
    def _():
        acc_scr[...] += part

    @pl.when(f == pl.num_programs(1) - 1)
    def _():
        o_ref[...] = (acc_scr[...] * rw_ref[...]).astype(o_ref.dtype)


def _experts(xs, wg, wu, wd, row_w_col, block_expert):
    r, d = xs.shape
    nbk = r // MOE_BLK
    nf = wg.shape[2] // MOE_TF
    gs = pltpu.PrefetchScalarGridSpec(
        num_scalar_prefetch=1, grid=(nbk, nf),
        in_specs=[pl.BlockSpec((MOE_BLK, d), lambda i, f, be: (i, 0)),
                  pl.BlockSpec((1, d, MOE_TF), lambda i, f, be: (be[i], 0, f)),
                  pl.BlockSpec((1, d, MOE_TF), lambda i, f, be: (be[i], 0, f)),
                  pl.BlockSpec((1, MOE_TF, d), lambda i, f, be: (be[i], f, 0)),
                  pl.BlockSpec((MOE_BLK, 1), lambda i, f, be: (i, 0))],
        out_specs=pl.BlockSpec((MOE_BLK, d), lambda i, f, be: (i, 0)),
        scratch_shapes=[pltpu.VMEM((MOE_BLK, d), F32)])
    return pl.pallas_call(
        _expert_body, grid_spec=gs,
        out_shape=jax.ShapeDtypeStruct((r, d), BF16),
        compiler_params=_cparams(("parallel", "arbitrary")),
        name="moe_experts",
    )(block_expert, xs, wg, wu, wd, row_w_col)


def _combine_body(ir_ref, ic_ref, fl_ref, y_ref, rt_ref, x_ref, g_ref, b_ref, o_ref):
    i = pl.program_id(0)
    flag = fl_ref[i]
    tok = ic_ref[i] * MOE_TC + lax.broadcasted_iota(jnp.int32, (MOE_TC, MOE_BLK), 0)
    onehot = jnp.where(rt_ref[0] == tok, 1.0, 0.0).astype(BF16)
    part = _dot(onehot, y_ref[...])

    @pl.when((flag & 3) == 3)
    def _():
        o_ref[...] = part

    @pl.when((flag & 3) == 1)
    def _():
        o_ref[...] += part

    @pl.when((flag & 4) == 4)
    def _():
        o_ref[...] = _layer_norm(ALPHA * x_ref[...] + o_ref[...], g_ref[...], b_ref[...])


def _combine_ln(out_rows, row_tok_lane, item_r, item_c, flags, x, g, b):
    r, d = out_rows.shape
    t = x.shape[0]
    ni = item_r.shape[0]
    vec = pl.BlockSpec((1, d), lambda i, ir, ic, fl: (0, 0))
    gs = pltpu.PrefetchScalarGridSpec(
        num_scalar_prefetch=3, grid=(ni,),
        in_specs=[pl.BlockSpec((MOE_BLK, d), lambda i, ir, ic, fl: (ir[i], 0)),
                  pl.BlockSpec((1, 1, MOE_BLK), lambda i, ir, ic, fl: (ir[i], 0, 0)),
                  pl.BlockSpec((MOE_TC, d), lambda i, ir, ic, fl: (ic[i], 0)),
                  vec, vec],
        out_specs=pl.BlockSpec((MOE_TC, d), lambda i, ir, ic, fl: (ic[i], 0)))
    return pl.pallas_call(
        _combine_body, grid_spec=gs,
        out_shape=jax.ShapeDtypeStruct((t, d), F32),
        compiler_params=_cparams(("arbitrary",)),
        name="moe_combine_ln",
    )(item_r, item_c, flags, out_rows, row_tok_lane, x, g.reshape(1, d), b.reshape(1, d))


def _moe_plan(top_idx, wts, t):
    e, blk, tc = N_EXPERTS, MOE_BLK, MOE_TC
    a = t * TOP_K
    i32 = jnp.int32
    exp_flat = top_idx.reshape(a).astype(i32)
    tok_flat = jnp.arange(a, dtype=i32) // TOP_K
    _, tok_sorted, w_sorted = lax.sort((exp_flat, tok_flat, wts.reshape(a)), num_keys=1, is_stable=True)
    counts = jnp.sum((exp_flat[:, None] == jnp.arange(e, dtype=i32)[None, :]).astype(i32), 0)
    padded = ((counts + blk - 1) // blk) * blk
    grp_start = jnp.cumsum(counts) - counts
    pad_end = jnp.cumsum(padded)
    pad_start = pad_end - padded
    nbk = a // blk + e
    r = nbk * blk
    tok_ext = jnp.concatenate([tok_sorted, jnp.full((r - a,), -1, i32)])
    w_ext = jnp.concatenate([w_sorted, jnp.zeros((r - a,), F32)])
    rows = jnp.arange(r, dtype=i32)
    row_tok = jnp.full((r,), -1, i32)
    row_w = jnp.zeros((r,), F32)
    for ee in range(e):
        inside = (rows >= pad_start[ee]) & (rows < pad_start[ee] + counts[ee])
        shift = pad_start[ee] - grp_start[ee]
        row_tok = jnp.where(inside, jnp.roll(tok_ext, shift), row_tok)
        row_w = jnp.where(inside, jnp.roll(w_ext, shift), row_w)
    blk_first = jnp.arange(nbk, dtype=i32) * blk
    block_expert = jnp.minimum(jnp.sum((pad_end[None, :] <= blk_first[:, None]).astype(i32), 1), e - 1)

    rt = row_tok.reshape(nbk, blk)
    valid = rt >= 0
    t_lo = jnp.min(jnp.where(valid, rt, t), 1)
    t_hi = jnp.max(rt, 1)
    has = t_hi >= 0
    c_lo = jnp.where(has, t_lo // tc, 0)
    c_hi = jnp.where(has, t_hi // tc, 0)
    n_it = c_hi - c_lo + 1
    off_end = jnp.cumsum(n_it)
    off_start = off_end - n_it
    total = off_end[-1]
    ni = nbk + e * (t // tc)
    idx = jnp.arange(ni, dtype=i32)
    ok = idx < total
    ir = jnp.minimum(jnp.sum((off_end[None, :] <= idx[:, None]).astype(i32), 1), nbk - 1)
    ic = jnp.where(ok, c_lo[ir] + idx - off_start[ir], c_hi[nbk - 1]).astype(i32)
    first = ok & (idx == off_start[ir])
    d_flags = ok.astype(i32) + 2 * first.astype(i32)

    key = jnp.where(ok, ic * nbk + ir, jnp.iinfo(jnp.int32).max)
    perm = jnp.argsort(key)
    ok2 = ok[perm]
    last = total - 1
    cr = jnp.where(ok2, ir[perm], ir[perm][last]).astype(i32)
    cc = jnp.where(ok2, ic[perm], ic[perm][last]).astype(i32)
    first2 = ok2 & jnp.concatenate([jnp.ones((1,), bool), cc[1:] != cc[:-1]])
    last2 = ok2 & jnp.concatenate([(cc[1:] != cc[:-1]) | ~ok2[1:], jnp.ones((1,), bool)])
    c_flags = ok2.astype(i32) + 2 * first2.astype(i32) + 4 * last2.astype(i32)
    return dict(row_tok=row_tok, row_w=row_w, block_expert=block_expert,
                d_items=(ir, ic, d_flags), c_items=(cr, cc, c_flags), nbk=nbk)


def _moe_ln(x2, x_bf, logits, wg, wu, wd, ln_g, ln_b):
    t, d = x2.shape
    top_val, top_idx = lax.top_k(logits, TOP_K)
    wts = jax.nn.softmax(top_val, -1)
    plan = _moe_plan(top_idx, wts, t)
    nbk = plan["nbk"]
    xs = _dispatch(x_bf, plan["row_tok"].reshape(-1, 1), *plan["d_items"])
    out_rows = _experts(xs, wg.astype(BF16), wu.astype(BF16), wd.astype(BF16),
                        plan["row_w"].reshape(-1, 1), plan["block_expert"])
    return _combine_ln(out_rows, plan["row_tok"].reshape(nbk, 1, MOE_BLK), *plan["c_items"], x2, ln_g, ln_b)


def _forward(x, mla_w_in, mla_q_norm, mla_w_q_up, mla_kv_norm, mla_w_kv_up, mla_w_out, nsa_w_in,
             nsa_cmp_pe_k, nsa_cmp_w1_k, nsa_cmp_w2_k, nsa_cmp_pe_v, nsa_cmp_w1_v, nsa_cmp_w2_v,
             nsa_w_out, rel_bias, ffn_w_gate, ffn_w_up, ffn_w_down, moe_w_router, moe_w_gate,
             moe_w_up, moe_w_down, ln_mix_g, ln_mix_b, ln_ffn_g, ln_ffn_b):
    b, s, d = x.shape
    x2 = x.reshape(b * s, d)
    o = _mla_mixer(x2, b, s, mla_w_in[0], mla_q_norm[0], mla_w_q_up[0], mla_kv_norm[0], mla_w_kv_up[0])
    x2 = _linear(o, mla_w_out[0].astype(BF16), tm=1024, tn=d, out_dtype=F32,
                 ln=(x2, ln_mix_g[0], ln_mix_b[0]), name="mla_out_ln")
    x2 = _ffn_ln(x2, ffn_w_gate[0].astype(BF16), ffn_w_up[0].astype(BF16), ffn_w_down[0].astype(BF16),
                 ln_ffn_g[0], ln_ffn_b[0])
    o = _nsa_mixer(x2, b, s, nsa_w_in[0], nsa_cmp_pe_k[0], nsa_cmp_w1_k[0], nsa_cmp_w2_k[0],
                   nsa_cmp_pe_v[0], nsa_cmp_w1_v[0], nsa_cmp_w2_v[0], rel_bias)
    x2, x_bf, logits = _out_ln_router(o, nsa_w_out[0].astype(BF16), x2, ln_mix_g[1], ln_mix_b[1],
                                      moe_w_router[0])
    x2 = _moe_ln(x2, x_bf, logits, moe_w_gate[0], moe_w_up[0], moe_w_down[0], ln_ffn_g[1], ln_ffn_b[1])
    return x2.reshape(b, s, d)


@jax.jit
def kernel(x, mla_w_in, mla_q_norm, mla_w_q_up, mla_kv_norm, mla_w_kv_up, mla_w_out, nsa_w_in,
           nsa_cmp_pe_k, nsa_cmp_w1_k, nsa_cmp_w2_k, nsa_cmp_pe_v, nsa_cmp_w1_v, nsa_cmp_w2_v,
           nsa_w_out, rel_bias, ffn_w_gate, ffn_w_up, ffn_w_down, moe_w_router, moe_w_gate,
           moe_w_up, moe_w_down, ln_mix_g, ln_mix_b, ln_ffn_g, ln_ffn_b):
    return _forward(x, mla_w_in, mla_q_norm, mla_w_q_up, mla_kv_norm, mla_w_kv_up, mla_w_out, nsa_w_in,
                    nsa_cmp_pe_k, nsa_cmp_w1_k, nsa_cmp_w2_k, nsa_cmp_pe_v, nsa_cmp_w1_v, nsa_cmp_w2_v,
                    nsa_w_out, rel_bias, ffn_w_gate, ffn_w_up, ffn_w_down, moe_w_router, moe_w_gate,
                    moe_w_up, moe_w_down, ln_mix_g, ln_mix_b, ln_ffn_g, ln_ffn_b)
```

```python
import functools
import math

import numpy as np
import jax
import jax.numpy as jnp
from jax import lax
from jax.experimental import pallas as pl
from jax.experimental.pallas import tpu as pltpu

F32 = jnp.float32
BF16 = jnp.bfloat16

D_MODEL = 1024
DEPTH = 2

MLA_HEADS = 8
MLA_Q_RANK = 512
MLA_KV_RANK = 256
MLA_NOPE = 128
MLA_ROPE = 64
MLA_V = 128
ROPE_THETA = 10000.0

NSA_HEADS = 16
NSA_GROUPS = 4
NSA_HG = NSA_HEADS // NSA_GROUPS
NSA_DK = 96
NSA_DV = 64
CMP_LEN = 32
CMP_STRIDE = 16
SEL_BLOCK = 64
SEL_TOPN = 16
WINDOW = 512
N_BRANCH = 3
FORCE = 1e6

REL_BUCKETS = 32
REL_MAX_DIST = 128

D_FF = 2816
N_EXPERTS = 8
TOP_K = 2
D_FF_EXPERT = 3584

LN_EPS = 1e-5
RMS_EPS = 1e-6

ALPHA = (2.0 * DEPTH) ** 0.25

NEG = -1e30
NEG_TEST = -1e29

V7X_VMEM_LIMIT = 56 * 1024 * 1024
RING_PAD_ROWS = 8

LOG2E = 1.4426950408889634

NSA_TQ = 256
NSA_LANES = NSA_HG * NSA_TQ
NSA_KC = 256
NSA_DKP = 128
NSA_GATE_ROWS = 8
NSA_BAND = 24
NSA_SEL_ROWS = 16
NSA_VROWS = NSA_DV + 16
NSA_HEAD_SLOTS = 4
NSA_FAR_UNROLL = 2
NSA_TILES_PER_VARIANT = 8

MOE_BLK = 512
MOE_TC = 512
MOE_TF = 1792


def _cparams(sem, vmem=V7X_VMEM_LIMIT):
    return pltpu.CompilerParams(dimension_semantics=sem, vmem_limit_bytes=vmem)


def _layer_norm(r, g, b):
    mu = jnp.mean(r, -1, keepdims=True)
    d = r - mu
    var = jnp.mean(d * d, -1, keepdims=True)
    return d * lax.rsqrt(var + LN_EPS) * g + b


def _rms_norm(x, g):
    return x * lax.rsqrt(jnp.mean(x * x, -1, keepdims=True) + RMS_EPS) * g


def _split3(a):
    a1 = a.astype(BF16)
    r1 = a - a1.astype(F32)
    a2 = r1.astype(BF16)
    a3 = (r1 - a2.astype(F32)).astype(BF16)
    return a1, a2, a3


def _dot(a, b):
    return jnp.dot(a, b, preferred_element_type=F32)


def _dot_nt(a, b):
    return lax.dot_general(a, b, (((1,), (1,)), ((), ())), preferred_element_type=F32)


def _linear_body(*refs, has_ln):
    it = iter(refs)
    x_ref = next(it)
    w_ref = next(it)
    if has_ln:
        res_ref, lg_ref, lb_ref = next(it), next(it), next(it)
    o_ref = next(it)
    acc = _dot(x_ref[...].astype(BF16), w_ref[...])
    if has_ln:
        acc = _layer_norm(ALPHA * res_ref[...] + acc, lg_ref[...], lb_ref[...])
    o_ref[...] = acc.astype(o_ref.dtype)


def _linear(x, w, *, tm, tn, out_dtype, ln=None, name):
    m, k = x.shape
    n = w.shape[1]
    assert m % tm == 0 and n % tn == 0
    in_specs = [pl.BlockSpec((tm, k), lambda i, j: (i, 0)),
                pl.BlockSpec((k, tn), lambda i, j: (0, j))]
    args = [x, w]
    if ln is not None:
        assert tn == n
        res, lg, lb = ln
        in_specs += [pl.BlockSpec((tm, n), lambda i, j: (i, 0)),
                     pl.BlockSpec((1, n), lambda i, j: (0, 0)),
                     pl.BlockSpec((1, n), lambda i, j: (0, 0))]
        args += [res, lg.reshape(1, n), lb.reshape(1, n)]
    return pl.pallas_call(
        functools.partial(_linear_body, has_ln=ln is not None),
        out_shape=jax.ShapeDtypeStruct((m, n), out_dtype),
        grid=(m // tm, n // tn),
        in_specs=in_specs,
        out_specs=pl.BlockSpec((tm, tn), lambda i, j: (i, j)),
        compiler_params=_cparams(("parallel", "arbitrary")),
        name=name,
    )(*args)


def _rope_tables(s):
    half = MLA_ROPE // 2
    freq = ROPE_THETA ** (-jnp.arange(half, dtype=F32) / half)
    ang = jnp.arange(s).astype(F32)[:, None] * freq[None, :]
    cos, sin = jnp.cos(ang), jnp.sin(ang)
    return jnp.concatenate([cos, cos], -1), jnp.concatenate([-sin, sin], -1)


MLA_DQ = MLA_NOPE + MLA_ROPE
MLA_QROWS = MLA_NOPE + 2 * MLA_ROPE
MLA_T = 512
MLA_HPS = 2
MLA_SUB = 256
MLA_FAR_UNROLL = 2


def _mla_q_body(lat_ref, g_ref, w_ref, cos_ref, sin_ref, o_ref, *, qscale):
    xn = _rms_norm(lat_ref[...], g_ref[...]).astype(BF16)
    y = _dot_nt(w_ref[...], xn)
    cos, sin = cos_ref[...], sin_ref[...]
    for h in range(MLA_HEADS):
        r0 = h * MLA_QROWS
        o_ref[0, h, :MLA_NOPE, :] = (y[r0:r0 + MLA_NOPE] * qscale).astype(BF16)
        a = y[r0 + MLA_NOPE:r0 + MLA_DQ]
        bb = y[r0 + MLA_DQ:r0 + MLA_QROWS]
        o_ref[0, h, MLA_NOPE:, :] = ((a * cos + bb * sin) * qscale).astype(BF16)


def _mla_q_proj(lat, gain, w_t, cos_t, sin_t, b, s, *, tm=MLA_T):
    ns = s // tm
    body = functools.partial(_mla_q_body, qscale=(MLA_DQ ** -0.5) * LOG2E)
    return pl.pallas_call(
        body,
        out_shape=jax.ShapeDtypeStruct((b, MLA_HEADS, MLA_DQ, s), BF16),
        grid=(b * ns,),
        in_specs=[pl.BlockSpec((tm, MLA_Q_RANK), lambda i: (i, 0)),
                  pl.BlockSpec((1, MLA_Q_RANK), lambda i: (0, 0)),
                  pl.BlockSpec(w_t.shape, lambda i: (0, 0)),
                  pl.BlockSpec((MLA_ROPE, tm), lambda i: (0, i % ns)),
                  pl.BlockSpec((MLA_ROPE, tm), lambda i: (0, i % ns))],
        out_specs=pl.BlockSpec((1, MLA_HEADS, MLA_DQ, tm), lambda i: (i // ns, 0, 0, i % ns)),
        compiler_params=_cparams(("parallel",)),
        name="mla_q_proj",
    )(lat, gain.reshape(1, -1), w_t, cos_t, sin_t)


def _mla_kv_body(lat_ref, g_ref, wk_ref, wvt_ref, kr_ref, cos_ref, sin_ref, k_ref, vt_ref):
    xn = _rms_norm(lat_ref[...], g_ref[...]).astype(BF16)
    kn = _dot(xn, wk_ref[...]).astype(BF16)
    vt = _dot_nt(wvt_ref[...], xn).astype(BF16)
    kr = kr_ref[...]
    rot = (kr[:, :MLA_ROPE] * cos_ref[...] + kr[:, MLA_ROPE:] * sin_ref[...]).astype(BF16)
    for h in range(MLA_HEADS):
        k_ref[0, h, :, :MLA_NOPE] = kn[:, h * MLA_NOPE:(h + 1) * MLA_NOPE]
        k_ref[0, h, :, MLA_NOPE:] = rot
        for kk in range(MLA_T // MLA_SUB):
            vt_ref[0, h, kk] = vt[h * MLA_V:(h + 1) * MLA_V, kk * MLA_SUB:(kk + 1) * MLA_SUB]


def _mla_kv_proj(lat, gain, wk, wv_t, cosx, sinx, b, s, *, tm=MLA_T):
    ns = s // tm
    nsub = tm // MLA_SUB
    return pl.pallas_call(
        _mla_kv_body,
        out_shape=(jax.ShapeDtypeStruct((b, MLA_HEADS, s, MLA_DQ), BF16),
                   jax.ShapeDtypeStruct((b, MLA_HEADS, ns * nsub, MLA_V, MLA_SUB), BF16)),
        grid=(b * ns,),
        in_specs=[pl.BlockSpec((tm, MLA_KV_RANK), lambda i: (i, MLA_Q_RANK // MLA_KV_RANK)),
                  pl.BlockSpec((1, MLA_KV_RANK), lambda i: (0, 0)),
                  pl.BlockSpec(wk.shape, lambda i: (0, 0)),
                  pl.BlockSpec(wv_t.shape, lambda i: (0, 0)),
                  pl.BlockSpec((tm, 2 * MLA_ROPE),
                               lambda i: (i, (MLA_Q_RANK + MLA_KV_RANK) // (2 * MLA_ROPE))),
                  pl.BlockSpec((tm, MLA_ROPE), lambda i: (i % ns, 0)),
                  pl.BlockSpec((tm, MLA_ROPE), lambda i: (i % ns, 0))],
        out_specs=(pl.BlockSpec((1, MLA_HEADS, tm, MLA_DQ), lambda i: (i // ns, 0, i % ns, 0)),
                   pl.BlockSpec((1, MLA_HEADS, nsub, MLA_V, MLA_SUB), lambda i: (i // ns, 0, i % ns, 0, 0))),
        compiler_params=_cparams(("parallel",)),
        name="mla_kv_proj",
    )(lat, gain.reshape(1, -1), wk, wv_t, lat, cosx, sinx)


def _mla_attn_body(qt_ref, k_ref, vt_ref, o_ref, *scr):
    i = pl.program_id(2)
    sub = MLA_SUB
    nsub = MLA_T // sub
    assert nsub == 2
    chains_all = [(h, ql) for ql in range(nsub) for h in range(MLA_HPS)]
    nch = len(chains_all)
    ring_a, ring_b, scr = scr[:nch], scr[nch:2 * nch], scr[2 * nch:]
    state = {(h, ql): scr[3 * (nsub * h + ql):3 * (nsub * h + ql) + 3]
             for h in range(MLA_HPS) for ql in range(nsub)}
    for m_scr, l_scr, acc_scr in state.values():
        m_scr[...] = jnp.full_like(m_scr, NEG)
        l_scr[...] = jnp.zeros_like(l_scr)
        acc_scr[...] = jnp.zeros_like(acc_scr)

    def issue(sk, ring, chains):
        for idx, (h, ql) in enumerate(chains_all):
            if (h, ql) in chains:
                k = k_ref[0, h, pl.ds(pl.multiple_of(sk * sub, sub), sub), :]
                ring[idx][:sub] = _dot(k, qt_ref[0, h, :, ql * sub:(ql + 1) * sub])
            yield

    def consume(sk, ring, chains, diag_ql=None):
        for idx, (h, ql) in enumerate(chains_all):
            if (h, ql) in chains:
                m_scr, l_scr, acc_scr = state[(h, ql)]
                s = ring[idx][:sub]
                if ql == diag_ql:
                    key = lax.broadcasted_iota(jnp.int32, s.shape, 0)
                    qry = lax.broadcasted_iota(jnp.int32, s.shape, 1)
                    s = jnp.where(key <= qry, s, NEG)
                m_old = m_scr[...]
                m_new = jnp.maximum(m_old, jnp.max(s, 0, keepdims=True))
                a = jnp.exp2(m_old - m_new)
                p = jnp.exp2(s - m_new)
                l_scr[...] = a * l_scr[...] + jnp.sum(p, 0, keepdims=True)
                acc_scr[...] = a * acc_scr[...] + _dot(vt_ref[0, h, sk], p.astype(BF16))
                m_scr[...] = m_new
            yield

    def interleave(*gens):
        for _ in zip(*gens):
            pass

    interleave(issue(0, ring_a, chains_all))

    def pair(c0):
        interleave(issue(c0 + 1, ring_b, chains_all), consume(c0, ring_a, chains_all))
        interleave(issue(c0 + 2, ring_a, chains_all), consume(c0 + 1, ring_b, chains_all))

    def trip(j, carry):
        for pr in range(MLA_FAR_UNROLL):
            pair(2 * (MLA_FAR_UNROLL * j + pr))
        return carry

    lax.fori_loop(0, i // MLA_FAR_UNROLL, trip, 0)
    lax.fori_loop(i // MLA_FAR_UNROLL * MLA_FAR_UNROLL, i, lambda j, carry: (pair(2 * j), carry)[1], 0)
    upper = [c for c in chains_all if c[1] == 1]
    interleave(issue(2 * i + 1, ring_b, upper), consume(2 * i, ring_a, chains_all, diag_ql=0))
    interleave(consume(2 * i + 1, ring_b, upper, diag_ql=1))
    for (h, ql), (m_scr, l_scr, acc_scr) in state.items():
        o_t = acc_scr[...] * (1.0 / jnp.maximum(l_scr[...], 1e-30))
        o_ref[0, ql * sub:(ql + 1) * sub, h * MLA_V:(h + 1) * MLA_V] = o_t.T.astype(o_ref.dtype)


def _mla_attention(q_t, k, v_t, b, s):
    tq = MLA_T
    nq = s // tq
    hp = MLA_HPS
    nsub = tq // MLA_SUB
    sub_state = [pltpu.VMEM((1, MLA_SUB), F32), pltpu.VMEM((1, MLA_SUB), F32), pltpu.VMEM((MLA_V, MLA_SUB), F32)]
    ring = hp * nsub * [pltpu.VMEM((MLA_SUB + RING_PAD_ROWS, MLA_SUB), F32)]
    return pl.pallas_call(
        _mla_attn_body,
        out_shape=jax.ShapeDtypeStruct((b, s, MLA_HEADS * MLA_V), BF16),
        grid=(b, MLA_HEADS // hp, nq),
        in_specs=[pl.BlockSpec((1, hp, MLA_DQ, tq), lambda bb, h, i: (bb, h, 0, i)),
                  pl.BlockSpec((1, hp, s, MLA_DQ), lambda bb, h, i: (bb, h, 0, 0)),
                  pl.BlockSpec((1, hp, nq * nsub, MLA_V, MLA_SUB), lambda bb, h, i: (bb, h, 0, 0, 0))],
        out_specs=pl.BlockSpec((1, tq, hp * MLA_V), lambda bb, h, i: (bb, i, h)),
        scratch_shapes=2 * ring + hp * nsub * sub_state,
        compiler_params=_cparams(("parallel", "parallel", "arbitrary")),
        name="mla_attention",
    )(q_t, k, v_t)


def _mla_mixer(x2, b, s, w_in, q_norm, w_q_up, kv_norm, w_kv_up):
    r0 = MLA_Q_RANK + MLA_KV_RANK
    half = MLA_ROPE // 2
    w_in_ext = jnp.concatenate([w_in, w_in[:, r0 + half:r0 + MLA_ROPE], w_in[:, r0:r0 + half]], 1)
    lat = _linear(x2, w_in_ext.astype(BF16), tm=512, tn=w_in_ext.shape[1], out_dtype=F32, name="mla_in")
    wq = w_q_up.reshape(MLA_Q_RANK, MLA_HEADS, MLA_DQ)
    wr = wq[..., MLA_NOPE:]
    wq = jnp.concatenate([wq, wr[..., half:], wr[..., :half]], -1)
    wq_t = wq.reshape(MLA_Q_RANK, MLA_HEADS * MLA_QROWS).T.astype(BF16)
    wkv = w_kv_up.reshape(MLA_KV_RANK, MLA_HEADS, MLA_NOPE + MLA_V)
    wk = wkv[..., :MLA_NOPE].reshape(MLA_KV_RANK, MLA_HEADS * MLA_NOPE).astype(BF16)
    wv_t = wkv[..., MLA_NOPE:].reshape(MLA_KV_RANK, MLA_HEADS * MLA_V).T.astype(BF16)
    cosx, sinx = _rope_tables(s)
    q_t = _mla_q_proj(lat, q_norm, wq_t, cosx.T, sinx.T, b, s)
    k, v_t = _mla_kv_proj(lat, kv_norm, wk, wv_t, cosx, sinx, b, s)
    o = _mla_attention(q_t, k, v_t, b, s)
    return o.reshape(b * s, MLA_HEADS * MLA_V)


def _ffn_body(x_ref, wg_ref, wu_ref, wd_ref, lg_ref, lb_ref, o_ref, acc_scr):
    f = pl.program_id(1)
    xb = x_ref[...].astype(BF16)
    h = jax.nn.silu(_dot(xb, wg_ref[...])) * _dot(xb, wu_ref[...])
    part = _dot(h.astype(BF16), wd_ref[...])

    @pl.when(f == 0)
    def _():
        acc_scr[...] = part

    @pl.when(f > 0)
    def _():
        acc_scr[...] += part

    @pl.when(f == pl.num_programs(1) - 1)
    def _():
        o_ref[...] = _layer_norm(ALPHA * x_ref[...] + acc_scr[...], lg_ref[...], lb_ref[...])


def _ffn_ln(x, wg, wu, wd, lg, lb, *, tm=512, tf=1408):
    m, d = x.shape
    dff = wg.shape[1]
    assert dff % tf == 0
    vec = pl.BlockSpec((1, d), lambda i, f: (0, 0))
    return pl.pallas_call(
        _ffn_body,
        out_shape=jax.ShapeDtypeStruct((m, d), F32),
        grid=(m // tm, dff // tf),
        in_specs=[pl.BlockSpec((tm, d), lambda i, f: (i, 0)),
                  pl.BlockSpec((d, tf), lambda i, f: (0, f)),
                  pl.BlockSpec((d, tf), lambda i, f: (0, f)),
                  pl.BlockSpec((tf, d), lambda i, f: (f, 0)),
                  vec, vec],
        out_specs=pl.BlockSpec((tm, d), lambda i, f: (i, 0)),
        scratch_shapes=[pltpu.VMEM((tm, d), F32)],
        compiler_params=_cparams(("parallel", "arbitrary")),
        name="ffn_ln",
    )(x, wg, wu, wd, lg.reshape(1, d), lb.reshape(1, d))


def _compress_body(c_ref, pea_ref, peb_ref, w1a_ref, w1b_ref, w2_ref, o_ref):
    c = c_ref[0, 0, 0].astype(F32)
    ya = _dot((c + pea_ref[...]).astype(BF16), w1a_ref[...])
    yb = _dot((c + peb_ref[...]).astype(BF16), w1b_ref[...])
    yb_next = jnp.concatenate([yb[1:], jnp.zeros_like(yb[:1])], 0)
    h = jax.nn.gelu(ya + yb_next)
    o_ref[0] = _dot(h.astype(BF16), w2_ref[...]).astype(o_ref.dtype)


def _compress(chunks, kv, pe, w1, w2, dpad):
    b, g, _, nch, k = chunks.shape
    bg = b * g
    d = pe.shape[1]
    dh, dout = w2.shape
    half = CMP_LEN // 2
    assert half == CMP_STRIDE and k == CMP_STRIDE * dpad

    def pad_tok(a):
        a = jnp.pad(a, [(0, 0), (0, dpad - d)] + [(0, 0)] * (a.ndim - 2))
        return a[:half].reshape((k,) + a.shape[2:]), a[half:].reshape((k,) + a.shape[2:])

    pe_a, pe_b = pad_tok(pe)
    w1a, w1b = pad_tok(w1)
    full = lambda i: (0, 0)
    return pl.pallas_call(
        _compress_body,
        out_shape=jax.ShapeDtypeStruct((bg, nch, dout), BF16),
        grid=(bg,),
        in_specs=[pl.BlockSpec((1, 1, 1, nch, k), lambda i: (i // g, i % g, kv, 0, 0)),
                  pl.BlockSpec((1, k), full), pl.BlockSpec((1, k), full),
                  pl.BlockSpec((k, dh), full), pl.BlockSpec((k, dh), full),
                  pl.BlockSpec((dh, dout), full)],
        out_specs=pl.BlockSpec((1, nch, dout), lambda i: (i, 0, 0)),
        compiler_params=_cparams(("parallel",)),
        name="nsa_compress",
    )(chunks, pe_a.reshape(1, k), pe_b.reshape(1, k), w1a.astype(BF16), w1b.astype(BF16), w2.astype(BF16))


def _rel_bucket_np(dist):
    n = np.maximum(dist, 0)
    max_exact = REL_BUCKETS // 2
    nf = np.maximum(n, 1).astype(np.float32)
    large = max_exact + (np.log(nf / np.float32(max_exact)) / np.float32(math.log(REL_MAX_DIST / max_exact))
                         * np.float32(REL_BUCKETS - max_exact)).astype(np.int32)
    large = np.minimum(large, REL_BUCKETS - 1)
    return np.where(n < max_exact, n, large).astype(np.int32)


def _nsa_tables(rel_bias, s):
    g, hg, tq, kc = NSA_GROUPS, NSA_HG, NSA_TQ, NSA_KC
    assert tq == kc and tq % CMP_STRIDE == 0 and WINDOW == 2 * kc
    assert np.all(_rel_bucket_np(np.arange(tq // 2 - 15, s + tq)) == REL_BUCKETS - 1)
    rb = rel_bias.reshape(REL_BUCKETS, g, hg) * LOG2E

    def tile(base, step, rows, valid):
        p = tq + step * rows
        k = np.arange(p)
        k = np.where(k < p - step * (rows - 1), k, k - p)
        d = base + k
        vec = jnp.where(valid(d)[:, None, None], rb[_rel_bucket_np(d)], NEG)
        vec = vec.transpose(1, 2, 0)
        flat = jnp.tile(vec, (1, 1, rows))[..., :rows * (p - step)]
        mat = flat.reshape(g, hg, rows, p - step)[..., :tq]
        return mat.transpose(0, 2, 1, 3).reshape(g, rows, hg * tq)

    causal = lambda d: d >= 0
    far = jnp.broadcast_to(rb[REL_BUCKETS - 1][:, None, :, None], (g, 1, hg, tq)).reshape(g, 1, hg * tq)
    rel = lambda x: jnp.where(x > NEG_TEST, x - far, NEG)
    tiles = jnp.stack([rel(tile(0, 1, kc, causal)), rel(tile(tq, 1, kc, causal)),
                       rel(tile(2 * tq, 1, kc, lambda d: d < WINDOW)),
                       jnp.zeros((g, kc, hg * tq), F32)], 1)
    band = jnp.stack([rel(tile(8 * CMP_STRIDE - CMP_LEN + 1, CMP_STRIDE, NSA_BAND, causal)),
                      rel(tile(-(CMP_LEN - 1), CMP_STRIDE, NSA_BAND, causal))], 1)
    return tiles, band


def _overlap_t(nc_pad, nb):
    n = np.arange(nc_pad)[None, :]
    jb = np.arange(nb)[:, None]
    cstart = n * CMP_STRIDE
    cend = cstart + CMP_LEN - 1
    sstart = jb * SEL_BLOCK
    ov = (cstart <= sstart + SEL_BLOCK - 1) & (cend >= sstart) & (n < nc_pad - 1)
    return jnp.asarray(ov.astype(np.float32), BF16)


def _nsa_body(qt_ref, kc_ref, vct_ref, k_ref, vt_ref, gate_ref, tiles_ref,
              band_ref, ovt_ref, o_ref, s_scr, sel_scr, qa_scr, oc_scr, *scr, nb, nqt):
    t = pl.program_id(2)
    ring_a, ring_b, state = scr[:NSA_HG], scr[NSA_HG:2 * NSA_HG], scr[2 * NSA_HG:]
    nkc = k_ref.shape[3]
    L = NSA_LANES
    q_t = qt_ref[0, 0, 0]
    ncp = kc_ref.shape[2]
    per = NSA_KC // SEL_BLOCK
    blocks_per_tile = NSA_TQ // CMP_STRIDE

    n_slab = nb // NSA_SEL_ROWS

    n_win = jnp.minimum(t, 2) + 1
    n_slots = n_win + t + 1

    def slot_params(c):
        is_win = c < n_win
        j = c - n_win
        is_sel = jnp.logical_and(c >= n_win, j <= t)
        delta = t - j
        br = is_win.astype(jnp.int32)
        kidx = jnp.where(is_win, t - c, jnp.where(is_sel, j, 0))
        sidx = jnp.where(is_sel, j // (NSA_SEL_ROWS // per), jnp.where(is_win, 0, n_slab))
        tidx = jnp.where(is_win, c, jnp.where(jnp.logical_and(is_sel, delta < 2), delta, 3))
        return br, kidx, sidx, tidx

    def issue(c, ring, set_rows=True):
        br, kidx, sidx, _ = slot_params(c)
        k = k_ref[0, 0, br, kidx]
        if set_rows:
            qa_scr[NSA_DK:NSA_DK + NSA_SEL_ROWS, :] = sel_scr[sidx]
        for h in range(NSA_HG):
            ring[h][:NSA_KC] = _dot(k, qa_scr[:, h * NSA_TQ:(h + 1) * NSA_TQ])
            yield

    def consume(c, ring, near):
        br, kidx, _, tidx = slot_params(c)
        vt = vt_ref[0, 0, br, kidx]
        for h in range(NSA_HG):
            m_scr, acc_scr = state[2 * h:2 * h + 2]
            sc = ring[h][:NSA_KC]
            if near:
                sc = sc + tiles_ref[0, tidx, :, h * NSA_TQ:(h + 1) * NSA_TQ]
            m_old = m_scr[br]
            m_new = jnp.maximum(m_old, jnp.max(sc, 0, keepdims=True))
            pp = jnp.exp2(sc - m_new)
            acc_scr[br] = jnp.exp2(m_old - m_new) * acc_scr[br] + _dot(vt, pp.astype(BF16))
            m_scr[br] = m_new
            yield

    def interleave(*gens):
        for _ in zip(*gens):
            pass

    for h in range(NSA_HG):
        m_scr, acc_scr = state[2 * h:2 * h + 2]
        m_scr[...] = jnp.full_like(m_scr, NEG)
        acc_scr[...] = jnp.zeros_like(acc_scr)
    qa_scr[...] = q_t

    def compress_and_select(nrows, nblk):
        s = _dot(kc_ref[0, 0, :nrows], q_t)
        first = (t == 0).astype(jnp.int32)
        bs = pl.multiple_of((blocks_per_tile * t - 8) * (1 - first), 8)
        row = lax.broadcasted_iota(jnp.int32, (nrows, L), 0)
        s_scr[:nrows] = jnp.where(row < bs + NSA_BAND, s, NEG)
        s_scr[pl.ds(bs, NSA_BAND), :] += band_ref[0, first]
        s = s_scr[:nrows]
        m = jnp.max(s, 0, keepdims=True)
        m = jnp.where(m < NEG_TEST, 0.0, m)
        p = jnp.exp2(s - m)
        den = jnp.maximum(jnp.sum(p, 0, keepdims=True), 1e-30)
        p = p * (1.0 / den)
        oc_scr[...] = _dot(vct_ref[0, 0, :, :nrows], p.astype(BF16))

        psum = p[:, 0:NSA_TQ]
        for h in range(1, NSA_HG):
            psum = psum + p[:, h * NSA_TQ:(h + 1) * NSA_TQ]
        p1, p2, p3 = _split3(psum)
        ovt = ovt_ref[:nblk, :nrows]
        imp = _dot(ovt, p1) + _dot(ovt, p2) + _dot(ovt, p3)
        interleave(issue(0, ring_a, set_rows=False))
        blk = lax.broadcasted_iota(jnp.int32, (nblk, NSA_TQ), 0)
        lane = lax.broadcasted_iota(jnp.int32, (nblk, NSA_TQ), 1)
        cur = (NSA_TQ // SEL_BLOCK) * t + lane // SEL_BLOCK
        forced = (blk == 0) | (blk == cur) | (blk == cur - 1)
        v = jnp.where(blk > cur, -FORCE, jnp.where(forced, -jnp.inf, imp))
        blk_f = blk.astype(F32)
        sel = jnp.where(forced, 1.0, 0.0)
        for _ in range(min(SEL_TOPN, nblk) - 3):
            mx = jnp.max(v, 0, keepdims=True)
            idx = jnp.min(jnp.where(v == mx, blk_f, float(nblk)), 0, keepdims=True)
            hit = blk_f == idx
            sel = jnp.where(hit, 1.0, sel)
            v = jnp.where(hit, -jnp.inf, v)
        selneg = jnp.where(sel > 0.5, 0.0, NEG)
        selneg = jnp.concatenate([selneg] * NSA_HG, 1).astype(BF16)
        for u in range(nblk // NSA_SEL_ROWS):
            sel_scr[u] = selneg[NSA_SEL_ROWS * u:NSA_SEL_ROWS * (u + 1), :]

    n_var = -(-nqt // NSA_TILES_PER_VARIANT)
    for i in range(n_var):
        nrows = min(ncp, (i + 1) * NSA_TILES_PER_VARIANT * blocks_per_tile)
        nblk = min(nb, (i + 1) * NSA_TILES_PER_VARIANT * NSA_TQ // SEL_BLOCK)

        @pl.when(t // NSA_TILES_PER_VARIANT == i)
        def _(nrows=nrows, nblk=nblk):
            compress_and_select(nrows, nblk)

    sel_scr[n_slab] = jnp.full((NSA_SEL_ROWS, L), NEG, BF16)

    def make_trip(first_slot, near, pairs=1):
        def trip(i, carry):
            for pr in range(pairs):
                c0 = first_slot + 2 * (pairs * i + pr)
                interleave(issue(c0 + 1, ring_b), consume(c0, ring_a, near))
                interleave(issue(c0 + 2, ring_a), consume(c0 + 1, ring_b, near))
            return carry
        return trip

    far_trips = jnp.maximum(n_slots - 2 - NSA_HEAD_SLOTS, 0) // (2 * NSA_FAR_UNROLL)
    tail_first = NSA_HEAD_SLOTS + 2 * NSA_FAR_UNROLL * far_trips
    tail_trips = (jnp.maximum(n_slots - tail_first, 0) + 1) // 2
    lax.fori_loop(0, NSA_HEAD_SLOTS // 2, make_trip(0, True), 0)
    lax.fori_loop(0, far_trips, make_trip(NSA_HEAD_SLOTS, False, NSA_FAR_UNROLL), 0)
    lax.fori_loop(0, tail_trips, make_trip(tail_first, True), 0)

    gate = gate_ref[0, 0, 0]
    outs = []
    for h in range(NSA_HG):
        sl = slice(h * NSA_TQ, (h + 1) * NSA_TQ)
        acc_scr = state[2 * h + 1]
        o_s = acc_scr[0, :NSA_DV] * (1.0 / jnp.maximum(acc_scr[0, NSA_DV:NSA_DV + 1], 1e-30))
        o_w = acc_scr[1, :NSA_DV] * (1.0 / jnp.maximum(acc_scr[1, NSA_DV:NSA_DV + 1], 1e-30))
        outs.append(gate[0:1, sl] * oc_scr[:, sl] + gate[1:2, sl] * o_s + gate[2:3, sl] * o_w)
    o_ref[0] = jnp.concatenate(outs, 0).T.astype(o_ref.dtype)


def _nsa_attention(q_t, kcmp, vcmp_t, k, v_t, gate, tiles, band, ovt):
    b, g, nqt = q_t.shape[:3]
    ncp = kcmp.shape[2]
    nkc = k.shape[3]
    nb = ovt.shape[0]
    L = NSA_LANES
    body = functools.partial(_nsa_body, nb=nb, nqt=nqt)
    ring = NSA_HG * [pltpu.VMEM((NSA_KC + RING_PAD_ROWS, NSA_TQ), F32)]
    head_state = [pltpu.VMEM((2, 1, NSA_TQ), F32), pltpu.VMEM((2, NSA_VROWS, NSA_TQ), F32)]
    grp = lambda bb, gg, t: (bb, gg, 0, 0)
    grp6 = lambda bb, gg, t: (bb, gg, 0, 0, 0, 0)
    return pl.pallas_call(
        body,
        out_shape=jax.ShapeDtypeStruct((b, nqt * NSA_TQ, g * NSA_HG * NSA_DV), BF16),
        grid=(b, g, nqt),
        in_specs=[pl.BlockSpec((1, 1, 1, NSA_DKP, L), lambda bb, gg, t: (bb, gg, t, 0, 0)),
                  pl.BlockSpec((1, 1, ncp, NSA_DKP), grp),
                  pl.BlockSpec((1, 1, NSA_DV, ncp), grp),
                  pl.BlockSpec((1, 1, 2, nkc, NSA_KC, NSA_DKP), grp6),
                  pl.BlockSpec((1, 1, 2, nkc, NSA_VROWS, NSA_KC), grp6),
                  pl.BlockSpec((1, 1, 1, N_BRANCH, L), lambda bb, gg, t: (bb, gg, t, 0, 0)),
                  pl.BlockSpec((1, 4, NSA_KC, L), lambda bb, gg, t: (gg, 0, 0, 0)),
                  pl.BlockSpec((1, 2, NSA_BAND, L), lambda bb, gg, t: (gg, 0, 0, 0)),
                  pl.BlockSpec(ovt.shape, lambda bb, gg, t: (0, 0))],
        out_specs=pl.BlockSpec((1, NSA_TQ, NSA_HG * NSA_DV), lambda bb, gg, t: (bb, t, gg)),
        scratch_shapes=[pltpu.VMEM((ncp, L), F32),
                        pltpu.VMEM((nb // NSA_SEL_ROWS + 1, NSA_SEL_ROWS, L), BF16),
                        pltpu.VMEM((NSA_DKP, L), BF16),
                        pltpu.VMEM((NSA_DV, L), F32)]
        + 2 * ring + NSA_HG * head_state,
        compiler_params=_cparams(("parallel", "parallel", "arbitrary")),
        name="nsa_attention",
    )(q_t, kcmp, vcmp_t, k, v_t, gate, tiles, band, ovt)


def _nsa_proj_body(x_ref, wq_ref, wk_ref, wv_ref, wg_ref, wc_ref,
                   q_ref, k_ref, v_ref, gate_ref, c_ref, *, qscale):
    g, hg, tq = NSA_GROUPS, NSA_HG, NSA_TQ
    xb = x_ref[...].astype(BF16)
    q_t = _dot_nt(wq_ref[...], xb) * qscale
    gate_t = jax.nn.sigmoid(_dot_nt(wg_ref[...], xb))
    for gg in range(g):
        for h in range(hg):
            head = gg * hg + h
            q_ref[0, gg, 0, :, h * tq:(h + 1) * tq] = q_t[head * NSA_DKP:(head + 1) * NSA_DKP].astype(BF16)
            r0 = head * NSA_GATE_ROWS
            gate_ref[0, gg, 0, :, h * tq:(h + 1) * tq] = gate_t[r0:r0 + N_BRANCH]
    k = _dot(xb, wk_ref[...])
    v_t = _dot_nt(wv_ref[...], xb).astype(BF16)
    per = tq // SEL_BLOCK
    chunk = pl.program_id(0) % (NSA_SEL_ROWS // per)
    row = lax.broadcasted_iota(jnp.int32, (tq, NSA_DKP), 0)
    col = lax.broadcasted_iota(jnp.int32, (tq, NSA_DKP), 1)
    blk_flag = jnp.where(col - NSA_DK == per * chunk + row // SEL_BLOCK, 1.0, 0.0)
    ones_rows = jnp.where(lax.broadcasted_iota(jnp.int32, (NSA_VROWS - NSA_DV, tq), 0) == 0, 1.0, 0.0).astype(BF16)
    for gg in range(g):
        for br in range(2):
            kb = k[:, (br * g + gg) * NSA_DKP:(br * g + gg + 1) * NSA_DKP]
            if br == 0:
                kb = kb + blk_flag
            k_ref[0, gg, br, 0] = kb.astype(BF16)
            v_ref[0, gg, br, 0, :NSA_DV, :] = v_t[(br * g + gg) * NSA_DV:(br * g + gg + 1) * NSA_DV]
            v_ref[0, gg, br, 0, NSA_DV:, :] = ones_rows
    ctok = _dot(xb, wc_ref[...]).astype(BF16)
    for gg in range(g):
        for kv in range(2):
            c_ref[0, gg, kv, 0] = ctok[:, (kv * g + gg) * NSA_DKP:(kv * g + gg + 1) * NSA_DKP]


def _nsa_proj(x2, b, s, wq_t, wk, wv_t, wg_t, wc):
    g, hg, tq, L = NSA_GROUPS, NSA_HG, NSA_TQ, NSA_LANES
    nqt = s // tq
    assert nqt % (NSA_SEL_ROWS * SEL_BLOCK // tq) == 0 and (s // SEL_BLOCK) % NSA_SEL_ROWS == 0
    t, d = x2.shape
    full = lambda i: (0, 0)
    tile5 = lambda i: (i // nqt, 0, i % nqt, 0, 0)
    tile6 = lambda i: (i // nqt, 0, 0, i % nqt, 0, 0)
    body = functools.partial(_nsa_proj_body, qscale=(NSA_DK ** -0.5) * LOG2E)
    return pl.pallas_call(
        body,
        out_shape=(jax.ShapeDtypeStruct((b, g, nqt, NSA_DKP, L), BF16),
                   jax.ShapeDtypeStruct((b, g, 2, nqt, NSA_KC, NSA_DKP), BF16),
                   jax.ShapeDtypeStruct((b, g, 2, nqt, NSA_VROWS, NSA_KC), BF16),
                   jax.ShapeDtypeStruct((b, g, nqt, N_BRANCH, L), F32),
                   jax.ShapeDtypeStruct((b, g, 2, nqt, NSA_KC, NSA_DKP), BF16)),
        grid=(t // tq,),
        in_specs=[pl.BlockSpec((tq, d), lambda i: (i, 0)),
                  pl.BlockSpec(wq_t.shape, full), pl.BlockSpec(wk.shape, full), pl.BlockSpec(wv_t.shape, full),
                  pl.BlockSpec(wg_t.shape, full), pl.BlockSpec(wc.shape, full)],
        out_specs=(pl.BlockSpec((1, g, 1, NSA_DKP, L), tile5),
                   pl.BlockSpec((1, g, 2, 1, NSA_KC, NSA_DKP), tile6),
                   pl.BlockSpec((1, g, 2, 1, NSA_VROWS, NSA_KC), tile6),
                   pl.BlockSpec((1, g, 1, N_BRANCH, L), tile5),
                   pl.BlockSpec((1, g, 2, 1, NSA_KC, NSA_DKP), tile6)),
        compiler_params=_cparams(("parallel",)),
        name="nsa_proj",
    )(x2, wq_t, wk, wv_t, wg_t, wc)


def _nsa_mixer(x2, b, s, w_in, pe_k, w1_k, w2_k, pe_v, w1_v, w2_v, rel_bias):
    assert NSA_TQ == NSA_KC
    t, d = x2.shape
    h, g, hg, dk, dv = NSA_HEADS, NSA_GROUPS, NSA_HG, NSA_DK, NSA_DV
    nb = s // SEL_BLOCK
    sizes = [h * dk, g * dk, g * dv, g * dk, g * dv, g * dk, g * dv, h * N_BRANCH]
    c = [0] + [int(v) for v in np.cumsum(sizes)]
    cols = [w_in[:, c[i]:c[i + 1]] for i in range(len(sizes))]
    w_q, w_kc, w_vc, w_ks, w_vs, w_kw, w_vw, w_gate = cols

    def pad_last(a, n):
        return jnp.pad(a, [(0, 0)] * (a.ndim - 1) + [(0, n - a.shape[-1])])

    wq_t = pad_last(w_q.reshape(d, h, dk), NSA_DKP).reshape(d, h * NSA_DKP).T.astype(BF16)
    wk = jnp.concatenate([pad_last(w.reshape(d, g, dk), NSA_DKP).reshape(d, g * NSA_DKP)
                          for w in (w_ks, w_kw)], 1).astype(BF16)
    wv_t = jnp.concatenate([w_vs, w_vw], 1).T.astype(BF16)
    wg_t = pad_last(w_gate.reshape(d, h, N_BRANCH), NSA_GATE_ROWS).reshape(d, h * NSA_GATE_ROWS).T.astype(BF16)
    wc = jnp.concatenate([pad_last(w_kc.reshape(d, g, dk), NSA_DKP).reshape(d, g * NSA_DKP),
                          pad_last(w_vc.reshape(d, g, dv), NSA_DKP).reshape(d, g * NSA_DKP)], 1).astype(BF16)
    q_t, k, v_t, gate, ctok = _nsa_proj(x2, b, s, wq_t, wk, wv_t, wg_t, wc)

    nch = s // CMP_STRIDE
    ctok = ctok.reshape(b, g, 2, nch, CMP_STRIDE * NSA_DKP)
    k_cmp = _compress(ctok, 0, pe_k, w1_k, pad_last(w2_k, NSA_DKP), NSA_DKP)
    v_cmp = _compress(ctok, 1, pe_v, w1_v, w2_v, NSA_DKP)
    k_cmp = k_cmp.reshape(b, g, nch, NSA_DKP)
    vcmp_t = v_cmp.reshape(b, g, nch, dv).transpose(0, 1, 3, 2)

    tiles, band = _nsa_tables(rel_bias, s)
    ovt = _overlap_t(nch, nb)
    o = _nsa_attention(q_t, k_cmp, vcmp_t, k, v_t, gate, tiles, band, ovt)
    return o.reshape(t, h * dv)


def _out_ln_router_body(o_ref, w_ref, res_ref, lg_ref, lb_ref, wr_ref, x_ref, xb_ref, logit_ref):
    y = _layer_norm(ALPHA * res_ref[...] + _dot(o_ref[...], w_ref[...]), lg_ref[...], lb_ref[...])
    x_ref[...] = y
    xb_ref[...] = y.astype(BF16)
    x1, x2, _ = _split3(y)
    w1, w2, _ = _split3(wr_ref[...])
    logit_ref[...] = _dot(x1, w1) + (_dot(x1, w2) + _dot(x2, w1))


def _out_ln_router(o, w_out, res, lg, lb, w_router, *, tm=512):
    m, k = o.shape
    n = w_out.shape[1]
    ne = w_router.shape[1]
    wr = jnp.pad(w_router, ((0, 0), (0, 128 - ne)))
    row = lambda i: (i, 0)
    full = lambda i: (0, 0)
    x, xb, logits = pl.pallas_call(
        _out_ln_router_body,
        out_shape=(jax.ShapeDtypeStruct((m, n), F32), jax.ShapeDtypeStruct((m, n), BF16),
                   jax.ShapeDtypeStruct((m, 128), F32)),
        grid=(m // tm,),
        in_specs=[pl.BlockSpec((tm, k), row), pl.BlockSpec((k, n), full), pl.BlockSpec((tm, n), row),
                  pl.BlockSpec((1, n), full), pl.BlockSpec((1, n), full), pl.BlockSpec((n, 128), full)],
        out_specs=(pl.BlockSpec((tm, n), row), pl.BlockSpec((tm, n), row), pl.BlockSpec((tm, 128), row)),
        compiler_params=_cparams(("parallel",)),
        name="nsa_out_ln_router",
    )(o, w_out, res, lg.reshape(1, n), lb.reshape(1, n), wr)
    return x, xb, logits[:, :ne]


def _dispatch_body(ir_ref, ic_ref, fl_ref, x_ref, rt_ref, o_ref):
    i = pl.program_id(0)
    flag = fl_ref[i]
    tok = ic_ref[i] * MOE_TC + lax.broadcasted_iota(jnp.int32, (MOE_BLK, MOE_TC), 1)
    onehot = jnp.where(rt_ref[...] == tok, 1.0, 0.0).astype(BF16)
    rows = _dot(onehot, x_ref[...])

    @pl.when(flag == 3)
    def _():
        o_ref[...] = rows.astype(o_ref.dtype)

    @pl.when(flag == 1)
    def _():
        o_ref[...] = (o_ref[...].astype(F32) + rows).astype(o_ref.dtype)


def _dispatch(x_bf, row_tok_col, item_r, item_c, flags):
    t, d = x_bf.shape
    r = row_tok_col.shape[0]
    ni = item_r.shape[0]
    gs = pltpu.PrefetchScalarGridSpec(
        num_scalar_prefetch=3, grid=(ni,),
        in_specs=[pl.BlockSpec((MOE_TC, d), lambda i, ir, ic, fl: (ic[i], 0)),
                  pl.BlockSpec((MOE_BLK, 1), lambda i, ir, ic, fl: (ir[i], 0))],
        out_specs=pl.BlockSpec((MOE_BLK, d), lambda i, ir, ic, fl: (ir[i], 0)))
    return pl.pallas_call(
        _dispatch_body, grid_spec=gs,
        out_shape=jax.ShapeDtypeStruct((r, d), BF16),
        compiler_params=_cparams(("arbitrary",)),
        name="moe_dispatch",
    )(item_r, item_c, flags, x_bf, row_tok_col)


def _expert_body(be_ref, x_ref, wg_ref, wu_ref, wd_ref, rw_ref, o_ref, acc_scr):
    f = pl.program_id(1)
    xb = x_ref[...]
    h = jax.nn.silu(_dot(xb, wg_ref[0])) * _dot(xb, wu_ref[0])
    part = _dot(h.astype(BF16), wd_ref[0])

    @pl.when(f == 0)
    def _():
        acc_scr[...] = part

    @pl.when(f > 0)
    def _():
        acc_scr[...] += part

    @pl.when(f == pl.num_programs(1) - 1)
    def _():
        o_ref[...] = (acc_scr[...] * rw_ref[...]).astype(o_ref.dtype)


def _experts(xs, wg, wu, wd, row_w_col, block_expert):
    r, d = xs.shape
    nbk = r // MOE_BLK
    nf = wg.shape[2] // MOE_TF
    gs = pltpu.PrefetchScalarGridSpec(
        num_scalar_prefetch=1, grid=(nbk, nf),
        in_specs=[pl.BlockSpec((MOE_BLK, d), lambda i, f, be: (i, 0)),
                  pl.BlockSpec((1, d, MOE_TF), lambda i, f, be: (be[i], 0, f)),
                  pl.BlockSpec((1, d, MOE_TF), lambda i, f, be: (be[i], 0, f)),
                  pl.BlockSpec((1, MOE_TF, d), lambda i, f, be: (be[i], f, 0)),
                  pl.BlockSpec((MOE_BLK, 1), lambda i, f, be: (i, 0))],
        out_specs=pl.BlockSpec((MOE_BLK, d), lambda i, f, be: (i, 0)),
        scratch_shapes=[pltpu.VMEM((MOE_BLK, d), F32)])
    return pl.pallas_call(
        _expert_body, grid_spec=gs,
        out_shape=jax.ShapeDtypeStruct((r, d), BF16),
        compiler_params=_cparams(("parallel", "arbitrary")),
        name="moe_experts",
    )(block_expert, xs, wg, wu, wd, row_w_col)


def _combine_body(ir_ref, ic_ref, fl_ref, y_ref, rt_ref, x_ref, g_ref, b_ref, o_ref):
    i = pl.program_id(0)
    flag = fl_ref[i]
    tok = ic_ref[i] * MOE_TC + lax.broadcasted_iota(jnp.int32, (MOE_TC, MOE_BLK), 0)
    onehot = jnp.where(rt_ref[0] == tok, 1.0, 0.0).astype(BF16)
    part = _dot(onehot, y_ref[...])

    @pl.when((flag & 3) == 3)
    def _():
        o_ref[...] = part

    @pl.when((flag & 3) == 1)
    def _():
        o_ref[...] += part

    @pl.when((flag & 4) == 4)
    def _():
        o_ref[...] = _layer_norm(ALPHA * x_ref[...] + o_ref[...], g_ref[...], b_ref[...])


def _combine_ln(out_rows, row_tok_lane, item_r, item_c, flags, x, g, b):
    r, d = out_rows.shape
    t = x.shape[0]
    ni = item_r.shape[0]
    vec = pl.BlockSpec((1, d), lambda i, ir, ic, fl: (0, 0))
    gs = pltpu.PrefetchScalarGridSpec(
        num_scalar_prefetch=3, grid=(ni,),
        in_specs=[pl.BlockSpec((MOE_BLK, d), lambda i, ir, ic, fl: (ir[i], 0)),
                  pl.BlockSpec((1, 1, MOE_BLK), lambda i, ir, ic, fl: (ir[i], 0, 0)),
                  pl.BlockSpec((MOE_TC, d), lambda i, ir, ic, fl: (ic[i], 0)),
                  vec, vec],
        out_specs=pl.BlockSpec((MOE_TC, d), lambda i, ir, ic, fl: (ic[i], 0)))
    return pl.pallas_call(
        _combine_body, grid_spec=gs,
        out_shape=jax.ShapeDtypeStruct((t, d), F32),
        compiler_params=_cparams(("arbitrary",)),
        name="moe_combine_ln",
    )(item_r, item_c, flags, out_rows, row_tok_lane, x, g.reshape(1, d), b.reshape(1, d))


def _moe_plan(top_idx, wts, t):
    e, blk, tc = N_EXPERTS, MOE_BLK, MOE_TC
    a = t * TOP_K
    i32 = jnp.int32
    exp_flat = top_idx.reshape(a).astype(i32)
    tok_flat = jnp.arange(a, dtype=i32) // TOP_K
    _, tok_sorted, w_sorted = lax.sort((exp_flat, tok_flat, wts.reshape(a)), num_keys=1, is_stable=True)
    counts = jnp.sum((exp_flat[:, None] == jnp.arange(e, dtype=i32)[None, :]).astype(i32), 0)
    padded = ((counts + blk - 1) // blk) * blk
    grp_start = jnp.cumsum(counts) - counts
    pad_end = jnp.cumsum(padded)
    pad_start = pad_end - padded
    nbk = a // blk + e
    r = nbk * blk
    tok_ext = jnp.concatenate([tok_sorted, jnp.full((r - a,), -1, i32)])
    w_ext = jnp.concatenate([w_sorted, jnp.zeros((r - a,), F32)])
    rows = jnp.arange(r, dtype=i32)
    row_tok = jnp.full((r,), -1, i32)
    row_w = jnp.zeros((r,), F32)
    for ee in range(e):
        inside = (rows >= pad_start[ee]) & (rows < pad_start[ee] + counts[ee])
        shift = pad_start[ee] - grp_start[ee]
        row_tok = jnp.where(inside, jnp.roll(tok_ext, shift), row_tok)
        row_w = jnp.where(inside, jnp.roll(w_ext, shift), row_w)
    blk_first = jnp.arange(nbk, dtype=i32) * blk
    block_expert = jnp.minimum(jnp.sum((pad_end[None, :] <= blk_first[:, None]).astype(i32), 1), e - 1)

    rt = row_tok.reshape(nbk, blk)
    valid = rt >= 0
    t_lo = jnp.min(jnp.where(valid, rt, t), 1)
    t_hi = jnp.max(rt, 1)
    has = t_hi >= 0
    c_lo = jnp.where(has, t_lo // tc, 0)
    c_hi = jnp.where(has, t_hi // tc, 0)
    n_it = c_hi - c_lo + 1
    off_end = jnp.cumsum(n_it)
    off_start = off_end - n_it
    total = off_end[-1]
    ni = nbk + e * (t // tc)
    idx = jnp.arange(ni, dtype=i32)
    ok = idx < total
    ir = jnp.minimum(jnp.sum((off_end[None, :] <= idx[:, None]).astype(i32), 1), nbk - 1)
    ic = jnp.where(ok, c_lo[ir] + idx - off_start[ir], c_hi[nbk - 1]).astype(i32)
    first = ok & (idx == off_start[ir])
    d_flags = ok.astype(i32) + 2 * first.astype(i32)

    key = jnp.where(ok, ic * nbk + ir, jnp.iinfo(jnp.int32).max)
    perm = jnp.argsort(key)
    ok2 = ok[perm]
    last = total - 1
    cr = jnp.where(ok2, ir[perm], ir[perm][last]).astype(i32)
    cc = jnp.where(ok2, ic[perm], ic[perm][last]).astype(i32)
    first2 = ok2 & jnp.concatenate([jnp.ones((1,), bool), cc[1:] != cc[:-1]])
    last2 = ok2 & jnp.concatenate([(cc[1:] != cc[:-1]) | ~ok2[1:], jnp.ones((1,), bool)])
    c_flags = ok2.astype(i32) + 2 * first2.astype(i32) + 4 * last2.astype(i32)
    return dict(row_tok=row_tok, row_w=row_w, block_expert=block_expert,
                d_items=(ir, ic, d_flags), c_items=(cr, cc, c_flags), nbk=nbk)


def _moe_ln(x2, x_bf, logits, wg, wu, wd, ln_g, ln_b):
    t, d = x2.shape
    top_val, top_idx = lax.top_k(logits, TOP_K)
    wts = jax.nn.softmax(top_val, -1)
    plan = _moe_plan(top_idx, wts, t)
    nbk = plan["nbk"]
    xs = _dispatch(x_bf, plan["row_tok"].reshape(-1, 1), *plan["d_items"])
    out_rows = _experts(xs, wg.astype(BF16), wu.astype(BF16), wd.astype(BF16),
                        plan["row_w"].reshape(-1, 1), plan["block_expert"])
    return _combine_ln(out_rows, plan["row_tok"].reshape(nbk, 1, MOE_BLK), *plan["c_items"], x2, ln_g, ln_b)


def _forward(x, mla_w_in, mla_q_norm, mla_w_q_up, mla_kv_norm, mla_w_kv_up, mla_w_out, nsa_w_in,
             nsa_cmp_pe_k, nsa_cmp_w1_k, nsa_cmp_w2_k, nsa_cmp_pe_v, nsa_cmp_w1_v, nsa_cmp_w2_v,
             nsa_w_out, rel_bias, ffn_w_gate, ffn_w_up, ffn_w_down, moe_w_router, moe_w_gate,
             moe_w_up, moe_w_down, ln_mix_g, ln_mix_b, ln_ffn_g, ln_ffn_b):
    b, s, d = x.shape
    x2 = x.reshape(b * s, d)
    o = _mla_mixer(x2, b, s, mla_w_in[0], mla_q_norm[0], mla_w_q_up[0], mla_kv_norm[0], mla_w_kv_up[0])
    x2 = _linear(o, mla_w_out[0].astype(BF16), tm=1024, tn=d, out_dtype=F32,
                 ln=(x2, ln_mix_g[0], ln_mix_b[0]), name="mla_out_ln")
    x2 = _ffn_ln(x2, ffn_w_gate[0].astype(BF16), ffn_w_up[0].astype(BF16), ffn_w_down[0].astype(BF16),
                 ln_ffn_g[0], ln_ffn_b[0])
    o = _nsa_mixer(x2, b, s, nsa_w_in[0], nsa_cmp_pe_k[0], nsa_cmp_w1_k[0], nsa_cmp_w2_k[0],
                   nsa_cmp_pe_v[0], nsa_cmp_w1_v[0], nsa_cmp_w2_v[0], rel_bias)
    x2, x_bf, logits = _out_ln_router(o, nsa_w_out[0].astype(BF16), x2, ln_mix_g[1], ln_mix_b[1],
                                      moe_w_router[0])
    x2 = _moe_ln(x2, x_bf, logits, moe_w_gate[0], moe_w_up[0], moe_w_down[0], ln_ffn_g[1], ln_ffn_b[1])
    return x2.reshape(b, s, d)


@jax.jit
def kernel(x, mla_w_in, mla_q_norm, mla_w_q_up, mla_kv_norm, mla_w_kv_up, mla_w_out, nsa_w_in,
           nsa_cmp_pe_k, nsa_cmp_w1_k, nsa_cmp_w2_k, nsa_cmp_pe_v, nsa_cmp_w1_v, nsa_cmp_w2_v,
           nsa_w_out, rel_bias, ffn_w_gate, ffn_w_up, ffn_w_down, moe_w_router, moe_w_gate,
           moe_w_up, moe_w_down, ln_mix_g, ln_mix_b, ln_ffn_g, ln_ffn_b):
    return _forward(x, mla_w_in, mla_q_norm, mla_w_q_up, mla_kv_norm, mla_w_kv_up, mla_w_out, nsa_w_in,
                    nsa_cmp_pe_k, nsa_cmp_w1_k, nsa_cmp_w2_k, nsa_cmp_pe_v, nsa_cmp_w1_v, nsa_cmp_w2_v,
                    nsa_w_out, rel_bias, ffn_w_gate, ffn_w_up, ffn_w_down, moe_w_router, moe_w_gate,
                    moe_w_up, moe_w_down, ln_mix_g, ln_mix_b, ln_ffn_g, ln_ffn_b)
```

```python
import functools
import math

import numpy as np
import jax
import jax.numpy as jnp
from jax import lax
from jax.experimental import pallas as pl
from jax.experimental.pallas import tpu as pltpu

F32 = jnp.float32
BF16 = jnp.bfloat16

D_MODEL = 1024
DEPTH = 2

MLA_HEADS = 8
MLA_Q_RANK = 512
MLA_KV_RANK = 256
MLA_NOPE = 128
MLA_ROPE = 64
MLA_V = 128
ROPE_THETA = 10000.0

NSA_HEADS = 16
NSA_GROUPS = 4
NSA_HG = NSA_HEADS // NSA_GROUPS
NSA_DK = 96
NSA_DV = 64
CMP_LEN = 32
CMP_STRIDE = 16
SEL_BLOCK = 64
SEL_TOPN = 16
WINDOW = 512
N_BRANCH = 3
FORCE = 1e6

REL_BUCKETS = 32
REL_MAX_DIST = 128

D_FF = 2816
N_EXPERTS = 8
TOP_K = 2
D_FF_EXPERT = 3584

LN_EPS = 1e-5
RMS_EPS = 1e-6

ALPHA = (2.0 * DEPTH) ** 0.25

NEG = -1e30
NEG_TEST = -1e29

V7X_VMEM_LIMIT = 56 * 1024 * 1024
RING_PAD_ROWS = 8

LOG2E = 1.4426950408889634

NSA_TQ = 256
NSA_LANES = NSA_HG * NSA_TQ
NSA_KC = 256
NSA_DKP = 128
NSA_GATE_ROWS = 8
NSA_BAND = 24
NSA_SEL_ROWS = 16
NSA_VROWS = NSA_DV + 16
NSA_HEAD_SLOTS = 4
NSA_FAR_UNROLL = 4
NSA_TILES_PER_VARIANT = 8

MOE_BLK = 512
MOE_TC = 512
MOE_TF = 1792


def _cparams(sem, vmem=V7X_VMEM_LIMIT):
    return pltpu.CompilerParams(dimension_semantics=sem, vmem_limit_bytes=vmem)


def _layer_norm(r, g, b):
    mu = jnp.mean(r, -1, keepdims=True)
    d = r - mu
    var = jnp.mean(d * d, -1, keepdims=True)
    return d * lax.rsqrt(var + LN_EPS) * g + b


def _rms_norm(x, g):
    return x * lax.rsqrt(jnp.mean(x * x, -1, keepdims=True) + RMS_EPS) * g


def _split3(a):
    a1 = a.astype(BF16)
    r1 = a - a1.astype(F32)
    a2 = r1.astype(BF16)
    a3 = (r1 - a2.astype(F32)).astype(BF16)
    return a1, a2, a3


def _dot(a, b):
    return jnp.dot(a, b, preferred_element_type=F32)


def _dot_nt(a, b):
    return lax.dot_general(a, b, (((1,), (1,)), ((), ())), preferred_element_type=F32)


def _linear_body(*refs, has_ln):
    it = iter(refs)
    x_ref = next(it)
    w_ref = next(it)
    if has_ln:
        res_ref, lg_ref, lb_ref = next(it), next(it), next(it)
    o_ref = next(it)
    acc = _dot(x_ref[...].astype(BF16), w_ref[...])
    if has_ln:
        acc = _layer_norm(ALPHA * res_ref[...] + acc, lg_ref[...], lb_ref[...])
    o_ref[...] = acc.astype(o_ref.dtype)


def _linear(x, w, *, tm, tn, out_dtype, ln=None, name):
    m, k = x.shape
    n = w.shape[1]
    assert m % tm == 0 and n % tn == 0
    in_specs = [pl.BlockSpec((tm, k), lambda i, j: (i, 0)),
                pl.BlockSpec((k, tn), lambda i, j: (0, j))]
    args = [x, w]
    if ln is not None:
        assert tn == n
        res, lg, lb = ln
        in_specs += [pl.BlockSpec((tm, n), lambda i, j: (i, 0)),
                     pl.BlockSpec((1, n), lambda i, j: (0, 0)),
                     pl.BlockSpec((1, n), lambda i, j: (0, 0))]
        args += [res, lg.reshape(1, n), lb.reshape(1, n)]
    return pl.pallas_call(
        functools.partial(_linear_body, has_ln=ln is not None),
        out_shape=jax.ShapeDtypeStruct((m, n), out_dtype),
        grid=(m // tm, n // tn),
        in_specs=in_specs,
        out_specs=pl.BlockSpec((tm, tn), lambda i, j: (i, j)),
        compiler_params=_cparams(("parallel", "arbitrary")),
        name=name,
    )(*args)


def _rope_tables(s):
    half = MLA_ROPE // 2
    freq = ROPE_THETA ** (-jnp.arange(half, dtype=F32) / half)
    ang = jnp.arange(s).astype(F32)[:, None] * freq[None, :]
    cos, sin = jnp.cos(ang), jnp.sin(ang)
    return jnp.concatenate([cos, cos], -1), jnp.concatenate([-sin, sin], -1)


MLA_DQ = MLA_NOPE + MLA_ROPE
MLA_QROWS = MLA_NOPE + 2 * MLA_ROPE
MLA_T = 512
MLA_HPS = 2
MLA_SUB = 256
MLA_FAR_UNROLL = 4


def _mla_q_body(lat_ref, g_ref, w_ref, cos_ref, sin_ref, o_ref, *, qscale):
    xn = _rms_norm(lat_ref[...], g_ref[...]).astype(BF16)
    y = _dot_nt(w_ref[...], xn)
    cos, sin = cos_ref[...], sin_ref[...]
    for h in range(MLA_HEADS):
        r0 = h * MLA_QROWS
        o_ref[0, h, :MLA_NOPE, :] = (y[r0:r0 + MLA_NOPE] * qscale).astype(BF16)
        a = y[r0 + MLA_NOPE:r0 + MLA_DQ]
        bb = y[r0 + MLA_DQ:r0 + MLA_QROWS]
        o_ref[0, h, MLA_NOPE:, :] = ((a * cos + bb * sin) * qscale).astype(BF16)


def _mla_q_proj(lat, gain, w_t, cos_t, sin_t, b, s, *, tm=MLA_T):
    ns = s // tm
    body = functools.partial(_mla_q_body, qscale=(MLA_DQ ** -0.5) * LOG2E)
    return pl.pallas_call(
        body,
        out_shape=jax.ShapeDtypeStruct((b, MLA_HEADS, MLA_DQ, s), BF16),
        grid=(b * ns,),
        in_specs=[pl.BlockSpec((tm, MLA_Q_RANK), lambda i: (i, 0)),
                  pl.BlockSpec((1, MLA_Q_RANK), lambda i: (0, 0)),
                  pl.BlockSpec(w_t.shape, lambda i: (0, 0)),
                  pl.BlockSpec((MLA_ROPE, tm), lambda i: (0, i % ns)),
                  pl.BlockSpec((MLA_ROPE, tm), lambda i: (0, i % ns))],
        out_specs=pl.BlockSpec((1, MLA_HEADS, MLA_DQ, tm), lambda i: (i // ns, 0, 0, i % ns)),
        compiler_params=_cparams(("parallel",)),
        name="mla_q_proj",
    )(lat, gain.reshape(1, -1), w_t, cos_t, sin_t)


def _mla_kv_body(lat_ref, g_ref, wk_ref, wvt_ref, kr_ref, cos_ref, sin_ref, k_ref, vt_ref):
    xn = _rms_norm(lat_ref[...], g_ref[...]).astype(BF16)
    kn = _dot(xn, wk_ref[...]).astype(BF16)
    vt = _dot_nt(wvt_ref[...], xn).astype(BF16)
    kr = kr_ref[...]
    rot = (kr[:, :MLA_ROPE] * cos_ref[...] + kr[:, MLA_ROPE:] * sin_ref[...]).astype(BF16)
    for h in range(MLA_HEADS):
        k_ref[0, h, :, :MLA_NOPE] = kn[:, h * MLA_NOPE:(h + 1) * MLA_NOPE]
        k_ref[0, h, :, MLA_NOPE:] = rot
        for kk in range(MLA_T // MLA_SUB):
            vt_ref[0, h, kk] = vt[h * MLA_V:(h + 1) * MLA_V, kk * MLA_SUB:(kk + 1) * MLA_SUB]


def _mla_kv_proj(lat, gain, wk, wv_t, cosx, sinx, b, s, *, tm=MLA_T):
    ns = s // tm
    nsub = tm // MLA_SUB
    return pl.pallas_call(
        _mla_kv_body,
        out_shape=(jax.ShapeDtypeStruct((b, MLA_HEADS, s, MLA_DQ), BF16),
                   jax.ShapeDtypeStruct((b, MLA_HEADS, ns * nsub, MLA_V, MLA_SUB), BF16)),
        grid=(b * ns,),
        in_specs=[pl.BlockSpec((tm, MLA_KV_RANK), lambda i: (i, MLA_Q_RANK // MLA_KV_RANK)),
                  pl.BlockSpec((1, MLA_KV_RANK), lambda i: (0, 0)),
                  pl.BlockSpec(wk.shape, lambda i: (0, 0)),
                  pl.BlockSpec(wv_t.shape, lambda i: (0, 0)),
                  pl.BlockSpec((tm, 2 * MLA_ROPE),
                               lambda i: (i, (MLA_Q_RANK + MLA_KV_RANK) // (2 * MLA_ROPE))),
                  pl.BlockSpec((tm, MLA_ROPE), lambda i: (i % ns, 0)),
                  pl.BlockSpec((tm, MLA_ROPE), lambda i: (i % ns, 0))],
        out_specs=(pl.BlockSpec((1, MLA_HEADS, tm, MLA_DQ), lambda i: (i // ns, 0, i % ns, 0)),
                   pl.BlockSpec((1, MLA_HEADS, nsub, MLA_V, MLA_SUB), lambda i: (i // ns, 0, i % ns, 0, 0))),
        compiler_params=_cparams(("parallel",)),
        name="mla_kv_proj",
    )(lat, gain.reshape(1, -1), wk, wv_t, lat, cosx, sinx)


def _mla_attn_body(qt_ref, k_ref, vt_ref, o_ref, *scr):
    i = pl.program_id(2)
    sub = MLA_SUB
    nsub = MLA_T // sub
    assert nsub == 2
    chains_all = [(h, ql) for ql in range(nsub) for h in range(MLA_HPS)]
    nch = len(chains_all)
    ring_a, ring_b, scr = scr[:nch], scr[nch:2 * nch], scr[2 * nch:]
    state = {(h, ql): scr[3 * (nsub * h + ql):3 * (nsub * h + ql) + 3]
             for h in range(MLA_HPS) for ql in range(nsub)}
    for m_scr, l_scr, acc_scr in state.values():
        m_scr[...] = jnp.full_like(m_scr, NEG)
        l_scr[...] = jnp.zeros_like(l_scr)
        acc_scr[...] = jnp.zeros_like(acc_scr)

    def issue(sk, ring, chains):
        for idx, (h, ql) in enumerate(chains_all):
            if (h, ql) in chains:
                k = k_ref[0, h, pl.ds(pl.multiple_of(sk * sub, sub), sub), :]
                ring[idx][:sub] = _dot(k, qt_ref[0, h, :, ql * sub:(ql + 1) * sub])
            yield

    def consume(sk, ring, chains, diag_ql=None):
        for idx, (h, ql) in enumerate(chains_all):
            if (h, ql) in chains:
                m_scr, l_scr, acc_scr = state[(h, ql)]
                s = ring[idx][:sub]
                if ql == diag_ql:
                    key = lax.broadcasted_iota(jnp.int32, s.shape, 0)
                    qry = lax.broadcasted_iota(jnp.int32, s.shape, 1)
                    s = jnp.where(key <= qry, s, NEG)
                m_old = m_scr[...]
                m_new = jnp.maximum(m_old, jnp.max(s, 0, keepdims=True))
                a = jnp.exp2(m_old - m_new)
                p = jnp.exp2(s - m_new)
                l_scr[...] = a * l_scr[...] + jnp.sum(p, 0, keepdims=True)
                acc_scr[...] = a * acc_scr[...] + _dot(vt_ref[0, h, sk], p.astype(BF16))
                m_scr[...] = m_new
            yield

    def interleave(*gens):
        for _ in zip(*gens):
            pass

    interleave(issue(0, ring_a, chains_all))

    def pair(c0):
        interleave(issue(c0 + 1, ring_b, chains_all), consume(c0, ring_a, chains_all))
        interleave(issue(c0 + 2, ring_a, chains_all), consume(c0 + 1, ring_b, chains_all))

    def trip(j, carry):
        for pr in range(MLA_FAR_UNROLL):
            pair(2 * (MLA_FAR_UNROLL * j + pr))
        return carry

    lax.fori_loop(0, i // MLA_FAR_UNROLL, trip, 0)
    lax.fori_loop(i // MLA_FAR_UNROLL * MLA_FAR_UNROLL, i, lambda j, carry: (pair(2 * j), carry)[1], 0)
    upper = [c for c in chains_all if c[1] == 1]
    interleave(issue(2 * i + 1, ring_b, upper), consume(2 * i, ring_a, chains_all, diag_ql=0))
    interleave(consume(2 * i + 1, ring_b, upper, diag_ql=1))
    for (h, ql), (m_scr, l_scr, acc_scr) in state.items():
        o_t = acc_scr[...] * (1.0 / jnp.maximum(l_scr[...], 1e-30))
        o_ref[0, ql * sub:(ql + 1) * sub, h * MLA_V:(h + 1) * MLA_V] = o_t.T.astype(o_ref.dtype)


def _mla_attention(q_t, k, v_t, b, s):
    tq = MLA_T
    nq = s // tq
    hp = MLA_HPS
    nsub = tq // MLA_SUB
    sub_state = [pltpu.VMEM((1, MLA_SUB), F32), pltpu.VMEM((1, MLA_SUB), F32), pltpu.VMEM((MLA_V, MLA_SUB), F32)]
    ring = hp * nsub * [pltpu.VMEM((MLA_SUB + RING_PAD_ROWS, MLA_SUB), F32)]
    return pl.pallas_call(
        _mla_attn_body,
        out_shape=jax.ShapeDtypeStruct((b, s, MLA_HEADS * MLA_V), BF16),
        grid=(b, MLA_HEADS // hp, nq),
        in_specs=[pl.BlockSpec((1, hp, MLA_DQ, tq), lambda bb, h, i: (bb, h, 0, i)),
                  pl.BlockSpec((1, hp, s, MLA_DQ), lambda bb, h, i: (bb, h, 0, 0)),
                  pl.BlockSpec((1, hp, nq * nsub, MLA_V, MLA_SUB), lambda bb, h, i: (bb, h, 0, 0, 0))],
        out_specs=pl.BlockSpec((1, tq, hp * MLA_V), lambda bb, h, i: (bb, i, h)),
        scratch_shapes=2 * ring + hp * nsub * sub_state,
        compiler_params=_cparams(("parallel", "parallel", "arbitrary")),
        name="mla_attention",
    )(q_t, k, v_t)


def _mla_mixer(x2, b, s, w_in, q_norm, w_q_up, kv_norm, w_kv_up):
    r0 = MLA_Q_RANK + MLA_KV_RANK
    half = MLA_ROPE // 2
    w_in_ext = jnp.concatenate([w_in, w_in[:, r0 + half:r0 + MLA_ROPE], w_in[:, r0:r0 + half]], 1)
    lat = _linear(x2, w_in_ext.astype(BF16), tm=512, tn=w_in_ext.shape[1], out_dtype=F32, name="mla_in")
    wq = w_q_up.reshape(MLA_Q_RANK, MLA_HEADS, MLA_DQ)
    wr = wq[..., MLA_NOPE:]
    wq = jnp.concatenate([wq, wr[..., half:], wr[..., :half]], -1)
    wq_t = wq.reshape(MLA_Q_RANK, MLA_HEADS * MLA_QROWS).T.astype(BF16)
    wkv = w_kv_up.reshape(MLA_KV_RANK, MLA_HEADS, MLA_NOPE + MLA_V)
    wk = wkv[..., :MLA_NOPE].reshape(MLA_KV_RANK, MLA_HEADS * MLA_NOPE).astype(BF16)
    wv_t = wkv[..., MLA_NOPE:].reshape(MLA_KV_RANK, MLA_HEADS * MLA_V).T.astype(BF16)
    cosx, sinx = _rope_tables(s)
    q_t = _mla_q_proj(lat, q_norm, wq_t, cosx.T, sinx.T, b, s)
    k, v_t = _mla_kv_proj(lat, kv_norm, wk, wv_t, cosx, sinx, b, s)
    o = _mla_attention(q_t, k, v_t, b, s)
    return o.reshape(b * s, MLA_HEADS * MLA_V)


def _ffn_body(x_ref, wg_ref, wu_ref, wd_ref, lg_ref, lb_ref, o_ref, acc_scr):
    f = pl.program_id(1)
    xb = x_ref[...].astype(BF16)
    h = jax.nn.silu(_dot(xb, wg_ref[...])) * _dot(xb, wu_ref[...])
    part = _dot(h.astype(BF16), wd_ref[...])

    @pl.when(f == 0)
    def _():
        acc_scr[...] = part

    @pl.when(f > 0)
    def _():
        acc_scr[...] += part

    @pl.when(f == pl.num_programs(1) - 1)
    def _():
        o_ref[...] = _layer_norm(ALPHA * x_ref[...] + acc_scr[...], lg_ref[...], lb_ref[...])


def _ffn_ln(x, wg, wu, wd, lg, lb, *, tm=512, tf=1408):
    m, d = x.shape
    dff = wg.shape[1]
    assert dff % tf == 0
    vec = pl.BlockSpec((1, d), lambda i, f: (0, 0))
    return pl.pallas_call(
        _ffn_body,
        out_shape=jax.ShapeDtypeStruct((m, d), F32),
        grid=(m // tm, dff // tf),
        in_specs=[pl.BlockSpec((tm, d), lambda i, f: (i, 0)),
                  pl.BlockSpec((d, tf), lambda i, f: (0, f)),
                  pl.BlockSpec((d, tf), lambda i, f: (0, f)),
                  pl.BlockSpec((tf, d), lambda i, f: (f, 0)),
                  vec, vec],
        out_specs=pl.BlockSpec((tm, d), lambda i, f: (i, 0)),
        scratch_shapes=[pltpu.VMEM((tm, d), F32)],
        compiler_params=_cparams(("parallel", "arbitrary")),
        name="ffn_ln",
    )(x, wg, wu, wd, lg.reshape(1, d), lb.reshape(1, d))


def _compress_body(c_ref, pea_ref, peb_ref, w1a_ref, w1b_ref, w2_ref, o_ref):
    c = c_ref[0, 0, 0].astype(F32)
    ya = _dot((c + pea_ref[...]).astype(BF16), w1a_ref[...])
    yb = _dot((c + peb_ref[...]).astype(BF16), w1b_ref[...])
    yb_next = jnp.concatenate([yb[1:], jnp.zeros_like(yb[:1])], 0)
    h = jax.nn.gelu(ya + yb_next)
    o_ref[0] = _dot(h.astype(BF16), w2_ref[...]).astype(o_ref.dtype)


def _compress(chunks, kv, pe, w1, w2, dpad):
    b, g, _, nch, k = chunks.shape
    bg = b * g
    d = pe.shape[1]
    dh, dout = w2.shape
    half = CMP_LEN // 2
    assert half == CMP_STRIDE and k == CMP_STRIDE * dpad

    def pad_tok(a):
        a = jnp.pad(a, [(0, 0), (0, dpad - d)] + [(0, 0)] * (a.ndim - 2))
        return a[:half].reshape((k,) + a.shape[2:]), a[half:].reshape((k,) + a.shape[2:])

    pe_a, pe_b = pad_tok(pe)
    w1a, w1b = pad_tok(w1)
    full = lambda i: (0, 0)
    return pl.pallas_call(
        _compress_body,
        out_shape=jax.ShapeDtypeStruct((bg, nch, dout), BF16),
        grid=(bg,),
        in_specs=[pl.BlockSpec((1, 1, 1, nch, k), lambda i: (i // g, i % g, kv, 0, 0)),
                  pl.BlockSpec((1, k), full), pl.BlockSpec((1, k), full),
                  pl.BlockSpec((k, dh), full), pl.BlockSpec((k, dh), full),
                  pl.BlockSpec((dh, dout), full)],
        out_specs=pl.BlockSpec((1, nch, dout), lambda i: (i, 0, 0)),
        compiler_params=_cparams(("parallel",)),
        name="nsa_compress",
    )(chunks, pe_a.reshape(1, k), pe_b.reshape(1, k), w1a.astype(BF16), w1b.astype(BF16), w2.astype(BF16))


def _rel_bucket_np(dist):
    n = np.maximum(dist, 0)
    max_exact = REL_BUCKETS // 2
    nf = np.maximum(n, 1).astype(np.float32)
    large = max_exact + (np.log(nf / np.float32(max_exact)) / np.float32(math.log(REL_MAX_DIST / max_exact))
                         * np.float32(REL_BUCKETS - max_exact)).astype(np.int32)
    large = np.minimum(large, REL_BUCKETS - 1)
    return np.where(n < max_exact, n, large).astype(np.int32)


def _nsa_tables(rel_bias, s):
    g, hg, tq, kc = NSA_GROUPS, NSA_HG, NSA_TQ, NSA_KC
    assert tq == kc and tq % CMP_STRIDE == 0 and WINDOW == 2 * kc
    assert np.all(_rel_bucket_np(np.arange(tq // 2 - 15, s + tq)) == REL_BUCKETS - 1)
    rb = rel_bias.reshape(REL_BUCKETS, g, hg) * LOG2E

    def tile(base, step, rows, valid):
        p = tq + step * rows
        k = np.arange(p)
        k = np.where(k < p - step * (rows - 1), k, k - p)
        d = base + k
        vec = jnp.where(valid(d)[:, None, None], rb[_rel_bucket_np(d)], NEG)
        vec = vec.transpose(1, 2, 0)
        flat = jnp.tile(vec, (1, 1, rows))[..., :rows * (p - step)]
        mat = flat.reshape(g, hg, rows, p - step)[..., :tq]
        return mat.transpose(0, 2, 1, 3).reshape(g, rows, hg * tq)

    causal = lambda d: d >= 0
    far = jnp.broadcast_to(rb[REL_BUCKETS - 1][:, None, :, None], (g, 1, hg, tq)).reshape(g, 1, hg * tq)
    rel = lambda x: jnp.where(x > NEG_TEST, x - far, NEG)
    tiles = jnp.stack([rel(tile(0, 1, kc, causal)), rel(tile(tq, 1, kc, causal)),
                       rel(tile(2 * tq, 1, kc, lambda d: d < WINDOW)),
                       jnp.zeros((g, kc, hg * tq), F32)], 1)
    band = jnp.stack([rel(tile(8 * CMP_STRIDE - CMP_LEN + 1, CMP_STRIDE, NSA_BAND, causal)),
                      rel(tile(-(CMP_LEN - 1), CMP_STRIDE, NSA_BAND, causal))], 1)
    return tiles, band


def _overlap_t(nc_pad, nb):
    n = np.arange(nc_pad)[None, :]
    jb = np.arange(nb)[:, None]
    cstart = n * CMP_STRIDE
    cend = cstart + CMP_LEN - 1
    sstart = jb * SEL_BLOCK
    ov = (cstart <= sstart + SEL_BLOCK - 1) & (cend >= sstart) & (n < nc_pad - 1)
    return jnp.asarray(ov.astype(np.float32), BF16)


def _nsa_body(qt_ref, kc_ref, vct_ref, k_ref, vt_ref, gate_ref, tiles_ref,
              band_ref, ovt_ref, o_ref, s_scr, sel_scr, qa_scr, oc_scr, *scr, nb, nqt):
    t = pl.program_id(2)
    ring_a, ring_b, state = scr[:NSA_HG], scr[NSA_HG:2 * NSA_HG], scr[2 * NSA_HG:]
    nkc = k_ref.shape[3]
    L = NSA_LANES
    q_t = qt_ref[0, 0, 0]
    ncp = kc_ref.shape[2]
    per = NSA_KC // SEL_BLOCK
    blocks_per_tile = NSA_TQ // CMP_STRIDE

    n_slab = nb // NSA_SEL_ROWS

    n_win = jnp.minimum(t, 2) + 1
    n_slots = n_win + t + 1

    def slot_params(c):
        is_win = c < n_win
        j = c - n_win
        is_sel = jnp.logical_and(c >= n_win, j <= t)
        delta = t - j
        br = is_win.astype(jnp.int32)
        kidx = jnp.where(is_win, t - c, jnp.where(is_sel, j, 0))
        sidx = jnp.where(is_sel, j // (NSA_SEL_ROWS // per), jnp.where(is_win, 0, n_slab))
        tidx = jnp.where(is_win, c, jnp.where(jnp.logical_and(is_sel, delta < 2), delta, 3))
        return br, kidx, sidx, tidx

    def issue(c, ring, set_rows=True):
        br, kidx, sidx, _ = slot_params(c)
        k = k_ref[0, 0, br, kidx]
        if set_rows:
            qa_scr[NSA_DK:NSA_DK + NSA_SEL_ROWS, :] = sel_scr[sidx]
        for h in range(NSA_HG):
            ring[h][:NSA_KC] = _dot(k, qa_scr[:, h * NSA_TQ:(h + 1) * NSA_TQ])
            yield

    def consume(c, ring, near):
        br, kidx, _, tidx = slot_params(c)
        vt = vt_ref[0, 0, br, kidx]
        for h in range(NSA_HG):
            m_scr, acc_scr = state[2 * h:2 * h + 2]
            sc = ring[h][:NSA_KC]
            if near:
                sc = sc + tiles_ref[0, tidx, :, h * NSA_TQ:(h + 1) * NSA_TQ]
            m_old = m_scr[br]
            m_new = jnp.maximum(m_old, jnp.max(sc, 0, keepdims=True))
            pp = jnp.exp2(sc - m_new)
            acc_scr[br] = jnp.exp2(m_old - m_new) * acc_scr[br] + _dot(vt, pp.astype(BF16))
            m_scr[br] = m_new
            yield

    def interleave(*gens):
        for _ in zip(*gens):
            pass

    for h in range(NSA_HG):
        m_scr, acc_scr = state[2 * h:2 * h + 2]
        m_scr[...] = jnp.full_like(m_scr, NEG)
        acc_scr[...] = jnp.zeros_like(acc_scr)
    qa_scr[...] = q_t

    def compress_and_select(nrows, nblk):
        s = _dot(kc_ref[0, 0, :nrows], q_t)
        first = (t == 0).astype(jnp.int32)
        bs = pl.multiple_of((blocks_per_tile * t - 8) * (1 - first), 8)
        row = lax.broadcasted_iota(jnp.int32, (nrows, L), 0)
        s_scr[:nrows] = jnp.where(row < bs + NSA_BAND, s, NEG)
        s_scr[pl.ds(bs, NSA_BAND), :] += band_ref[0, first]
        s = s_scr[:nrows]
        m = jnp.max(s, 0, keepdims=True)
        m = jnp.where(m < NEG_TEST, 0.0, m)
        p = jnp.exp2(s - m)
        den = jnp.maximum(jnp.sum(p, 0, keepdims=True), 1e-30)
        p = p * (1.0 / den)
        oc_scr[...] = _dot(vct_ref[0, 0, :, :nrows], p.astype(BF16))

        psum = p[:, 0:NSA_TQ]
        for h in range(1, NSA_HG):
            psum = psum + p[:, h * NSA_TQ:(h + 1) * NSA_TQ]
        p1, p2, p3 = _split3(psum)
        ovt = ovt_ref[:nblk, :nrows]
        imp = _dot(ovt, p1) + _dot(ovt, p2) + _dot(ovt, p3)
        interleave(issue(0, ring_a, set_rows=False))
        blk = lax.broadcasted_iota(jnp.int32, (nblk, NSA_TQ), 0)
        lane = lax.broadcasted_iota(jnp.int32, (nblk, NSA_TQ), 1)
        cur = (NSA_TQ // SEL_BLOCK) * t + lane // SEL_BLOCK
        forced = (blk == 0) | (blk == cur) | (blk == cur - 1)
        v = jnp.where(blk > cur, -FORCE, jnp.where(forced, -jnp.inf, imp))
        blk_f = blk.astype(F32)
        sel = jnp.where(forced, 1.0, 0.0)
        for _ in range(min(SEL_TOPN, nblk) - 3):
            mx = jnp.max(v, 0, keepdims=True)
            idx = jnp.min(jnp.where(v == mx, blk_f, float(nblk)), 0, keepdims=True)
            hit = blk_f == idx
            sel = jnp.where(hit, 1.0, sel)
            v = jnp.where(hit, -jnp.inf, v)
        selneg = jnp.where(sel > 0.5, 0.0, NEG)
        selneg = jnp.concatenate([selneg] * NSA_HG, 1).astype(BF16)
        for u in range(nblk // NSA_SEL_ROWS):
            sel_scr[u] = selneg[NSA_SEL_ROWS * u:NSA_SEL_ROWS * (u + 1), :]

    n_var = -(-nqt // NSA_TILES_PER_VARIANT)
    for i in range(n_var):
        nrows = min(ncp, (i + 1) * NSA_TILES_PER_VARIANT * blocks_per_tile)
        nblk = min(nb, (i + 1) * NSA_TILES_PER_VARIANT * NSA_TQ // SEL_BLOCK)

        @pl.when(t // NSA_TILES_PER_VARIANT == i)
        def _(nrows=nrows, nblk=nblk):
            compress_and_select(nrows, nblk)

    sel_scr[n_slab] = jnp.full((NSA_SEL_ROWS, L), NEG, BF16)

    def make_trip(first_slot, near, pairs=1):
        def trip(i, carry):
            for pr in range(pairs):
                c0 = first_slot + 2 * (pairs * i + pr)
                interleave(issue(c0 + 1, ring_b), consume(c0, ring_a, near))
                interleave(issue(c0 + 2, ring_a), consume(c0 + 1, ring_b, near))
            return carry
        return trip

    far_pairs = jnp.maximum(n_slots - 2 - NSA_HEAD_SLOTS, 0) // 2
    far_trips = far_pairs // NSA_FAR_UNROLL
    rest_first = NSA_HEAD_SLOTS + 2 * NSA_FAR_UNROLL * far_trips
    tail_first = NSA_HEAD_SLOTS + 2 * far_pairs
    tail_trips = (jnp.maximum(n_slots - tail_first, 0) + 1) // 2
    make_trip(0, True, NSA_HEAD_SLOTS // 2)(0, 0)
    lax.fori_loop(0, far_trips, make_trip(NSA_HEAD_SLOTS, False, NSA_FAR_UNROLL), 0)
    lax.fori_loop(0, far_pairs - NSA_FAR_UNROLL * far_trips, make_trip(rest_first, False), 0)
    lax.fori_loop(0, tail_trips, make_trip(tail_first, True), 0)

    gate = gate_ref[0, 0, 0]
    outs = []
    for h in range(NSA_HG):
        sl = slice(h * NSA_TQ, (h + 1) * NSA_TQ)
        acc_scr = state[2 * h + 1]
        o_s = acc_scr[0, :NSA_DV] * (1.0 / jnp.maximum(acc_scr[0, NSA_DV:NSA_DV + 1], 1e-30))
        o_w = acc_scr[1, :NSA_DV] * (1.0 / jnp.maximum(acc_scr[1, NSA_DV:NSA_DV + 1], 1e-30))
        outs.append(gate[0:1, sl] * oc_scr[:, sl] + gate[1:2, sl] * o_s + gate[2:3, sl] * o_w)
    o_ref[0] = jnp.concatenate(outs, 0).T.astype(o_ref.dtype)


def _nsa_attention(q_t, kcmp, vcmp_t, k, v_t, gate, tiles, band, ovt):
    b, g, nqt = q_t.shape[:3]
    ncp = kcmp.shape[2]
    nkc = k.shape[3]
    nb = ovt.shape[0]
    L = NSA_LANES
    body = functools.partial(_nsa_body, nb=nb, nqt=nqt)
    ring = NSA_HG * [pltpu.VMEM((NSA_KC + RING_PAD_ROWS, NSA_TQ), F32)]
    head_state = [pltpu.VMEM((2, 1, NSA_TQ), F32), pltpu.VMEM((2, NSA_VROWS, NSA_TQ), F32)]
    grp = lambda bb, gg, t: (bb, gg, 0, 0)
    grp6 = lambda bb, gg, t: (bb, gg, 0, 0, 0, 0)
    return pl.pallas_call(
        body,
        out_shape=jax.ShapeDtypeStruct((b, nqt * NSA_TQ, g * NSA_HG * NSA_DV), BF16),
        grid=(b, g, nqt),
        in_specs=[pl.BlockSpec((1, 1, 1, NSA_DKP, L), lambda bb, gg, t: (bb, gg, t, 0, 0)),
                  pl.BlockSpec((1, 1, ncp, NSA_DKP), grp),
                  pl.BlockSpec((1, 1, NSA_DV, ncp), grp),
                  pl.BlockSpec((1, 1, 2, nkc, NSA_KC, NSA_DKP), grp6),
                  pl.BlockSpec((1, 1, 2, nkc, NSA_VROWS, NSA_KC), grp6),
                  pl.BlockSpec((1, 1, 1, N_BRANCH, L), lambda bb, gg, t: (bb, gg, t, 0, 0)),
                  pl.BlockSpec((1, 4, NSA_KC, L), lambda bb, gg, t: (gg, 0, 0, 0)),
                  pl.BlockSpec((1, 2, NSA_BAND, L), lambda bb, gg, t: (gg, 0, 0, 0)),
                  pl.BlockSpec(ovt.shape, lambda bb, gg, t: (0, 0))],
        out_specs=pl.BlockSpec((1, NSA_TQ, NSA_HG * NSA_DV), lambda bb, gg, t: (bb, t, gg)),
        scratch_shapes=[pltpu.VMEM((ncp, L), F32),
                        pltpu.VMEM((nb // NSA_SEL_ROWS + 1, NSA_SEL_ROWS, L), BF16),
                        pltpu.VMEM((NSA_DKP, L), BF16),
                        pltpu.VMEM((NSA_DV, L), F32)]
        + 2 * ring + NSA_HG * head_state,
        compiler_params=_cparams(("parallel", "parallel", "arbitrary")),
        name="nsa_attention",
    )(q_t, kcmp, vcmp_t, k, v_t, gate, tiles, band, ovt)


def _nsa_proj_body(x_ref, wq_ref, wk_ref, wv_ref, wg_ref, wc_ref,
                   q_ref, k_ref, v_ref, gate_ref, c_ref, *, qscale):
    g, hg, tq = NSA_GROUPS, NSA_HG, NSA_TQ
    xb = x_ref[...].astype(BF16)
    q_t = _dot_nt(wq_ref[...], xb) * qscale
    gate_t = jax.nn.sigmoid(_dot_nt(wg_ref[...], xb))
    for gg in range(g):
        for h in range(hg):
            head = gg * hg + h
            q_ref[0, gg, 0, :, h * tq:(h + 1) * tq] = q_t[head * NSA_DKP:(head + 1) * NSA_DKP].astype(BF16)
            r0 = head * NSA_GATE_ROWS
            gate_ref[0, gg, 0, :, h * tq:(h + 1) * tq] = gate_t[r0:r0 + N_BRANCH]
    k = _dot(xb, wk_ref[...])
    v_t = _dot_nt(wv_ref[...], xb).astype(BF16)
    per = tq // SEL_BLOCK
    chunk = pl.program_id(0) % (NSA_SEL_ROWS // per)
    row = lax.broadcasted_iota(jnp.int32, (tq, NSA_DKP), 0)
    col = lax.broadcasted_iota(jnp.int32, (tq, NSA_DKP), 1)
    blk_flag = jnp.where(col - NSA_DK == per * chunk + row // SEL_BLOCK, 1.0, 0.0)
    ones_rows = jnp.where(lax.broadcasted_iota(jnp.int32, (NSA_VROWS - NSA_DV, tq), 0) == 0, 1.0, 0.0).astype(BF16)
    for gg in range(g):
        for br in range(2):
            kb = k[:, (br * g + gg) * NSA_DKP:(br * g + gg + 1) * NSA_DKP]
            if br == 0:
                kb = kb + blk_flag
            k_ref[0, gg, br, 0] = kb.astype(BF16)
            v_ref[0, gg, br, 0, :NSA_DV, :] = v_t[(br * g + gg) * NSA_DV:(br * g + gg + 1) * NSA_DV]
            v_ref[0, gg, br, 0, NSA_DV:, :] = ones_rows
    ctok = _dot(xb, wc_ref[...]).astype(BF16)
    for gg in range(g):
        for kv in range(2):
            c_ref[0, gg, kv, 0] = ctok[:, (kv * g + gg) * NSA_DKP:(kv * g + gg + 1) * NSA_DKP]


def _nsa_proj(x2, b, s, wq_t, wk, wv_t, wg_t, wc):
    g, hg, tq, L = NSA_GROUPS, NSA_HG, NSA_TQ, NSA_LANES
    nqt = s // tq
    assert nqt % (NSA_SEL_ROWS * SEL_BLOCK // tq) == 0 and (s // SEL_BLOCK) % NSA_SEL_ROWS == 0
    t, d = x2.shape
    full = lambda i: (0, 0)
    tile5 = lambda i: (i // nqt, 0, i % nqt, 0, 0)
    tile6 = lambda i: (i // nqt, 0, 0, i % nqt, 0, 0)
    body = functools.partial(_nsa_proj_body, qscale=(NSA_DK ** -0.5) * LOG2E)
    return pl.pallas_call(
        body,
        out_shape=(jax.ShapeDtypeStruct((b, g, nqt, NSA_DKP, L), BF16),
                   jax.ShapeDtypeStruct((b, g, 2, nqt, NSA_KC, NSA_DKP), BF16),
                   jax.ShapeDtypeStruct((b, g, 2, nqt, NSA_VROWS, NSA_KC), BF16),
                   jax.ShapeDtypeStruct((b, g, nqt, N_BRANCH, L), F32),
                   jax.ShapeDtypeStruct((b, g, 2, nqt, NSA_KC, NSA_DKP), BF16)),
        grid=(t // tq,),
        in_specs=[pl.BlockSpec((tq, d), lambda i: (i, 0)),
                  pl.BlockSpec(wq_t.shape, full), pl.BlockSpec(wk.shape, full), pl.BlockSpec(wv_t.shape, full),
                  pl.BlockSpec(wg_t.shape, full), pl.BlockSpec(wc.shape, full)],
        out_specs=(pl.BlockSpec((1, g, 1, NSA_DKP, L), tile5),
                   pl.BlockSpec((1, g, 2, 1, NSA_KC, NSA_DKP), tile6),
                   pl.BlockSpec((1, g, 2, 1, NSA_VROWS, NSA_KC), tile6),
                   pl.BlockSpec((1, g, 1, N_BRANCH, L), tile5),
                   pl.BlockSpec((1, g, 2, 1, NSA_KC, NSA_DKP), tile6)),
        compiler_params=_cparams(("parallel",)),
        name="nsa_proj",
    )(x2, wq_t, wk, wv_t, wg_t, wc)


def _nsa_mixer(x2, b, s, w_in, pe_k, w1_k, w2_k, pe_v, w1_v, w2_v, rel_bias):
    assert NSA_TQ == NSA_KC
    t, d = x2.shape
    h, g, hg, dk, dv = NSA_HEADS, NSA_GROUPS, NSA_HG, NSA_DK, NSA_DV
    nb = s // SEL_BLOCK
    sizes = [h * dk, g * dk, g * dv, g * dk, g * dv, g * dk, g * dv, h * N_BRANCH]
    c = [0] + [int(v) for v in np.cumsum(sizes)]
    cols = [w_in[:, c[i]:c[i + 1]] for i in range(len(sizes))]
    w_q, w_kc, w_vc, w_ks, w_vs, w_kw, w_vw, w_gate = cols

    def pad_last(a, n):
        return jnp.pad(a, [(0, 0)] * (a.ndim - 1) + [(0, n - a.shape[-1])])

    wq_t = pad_last(w_q.reshape(d, h, dk), NSA_DKP).reshape(d, h * NSA_DKP).T.astype(BF16)
    wk = jnp.concatenate([pad_last(w.reshape(d, g, dk), NSA_DKP).reshape(d, g * NSA_DKP)
                          for w in (w_ks, w_kw)], 1).astype(BF16)
    wv_t = jnp.concatenate([w_vs, w_vw], 1).T.astype(BF16)
    wg_t = pad_last(w_gate.reshape(d, h, N_BRANCH), NSA_GATE_ROWS).reshape(d, h * NSA_GATE_ROWS).T.astype(BF16)
    wc = jnp.concatenate([pad_last(w_kc.reshape(d, g, dk), NSA_DKP).reshape(d, g * NSA_DKP),
                          pad_last(w_vc.reshape(d, g, dv), NSA_DKP).reshape(d, g * NSA_DKP)], 1).astype(BF16)
    q_t, k, v_t, gate, ctok = _nsa_proj(x2, b, s, wq_t, wk, wv_t, wg_t, wc)

    nch = s // CMP_STRIDE
    ctok = ctok.reshape(b, g, 2, nch, CMP_STRIDE * NSA_DKP)
    k_cmp = _compress(ctok, 0, pe_k, w1_k, pad_last(w2_k, NSA_DKP), NSA_DKP)
    v_cmp = _compress(ctok, 1, pe_v, w1_v, w2_v, NSA_DKP)
    k_cmp = k_cmp.reshape(b, g, nch, NSA_DKP)
    vcmp_t = v_cmp.reshape(b, g, nch, dv).transpose(0, 1, 3, 2)

    tiles, band = _nsa_tables(rel_bias, s)
    ovt = _overlap_t(nch, nb)
    o = _nsa_attention(q_t, k_cmp, vcmp_t, k, v_t, gate, tiles, band, ovt)
    return o.reshape(t, h * dv)


def _out_ln_router_body(o_ref, w_ref, res_ref, lg_ref, lb_ref, wr_ref, x_ref, xb_ref, logit_ref):
    y = _layer_norm(ALPHA * res_ref[...] + _dot(o_ref[...], w_ref[...]), lg_ref[...], lb_ref[...])
    x_ref[...] = y
    xb_ref[...] = y.astype(BF16)
    x1, x2, _ = _split3(y)
    w1, w2, _ = _split3(wr_ref[...])
    logit_ref[...] = _dot(x1, w1) + (_dot(x1, w2) + _dot(x2, w1))


def _out_ln_router(o, w_out, res, lg, lb, w_router, *, tm=512):
    m, k = o.shape
    n = w_out.shape[1]
    ne = w_router.shape[1]
    wr = jnp.pad(w_router, ((0, 0), (0, 128 - ne)))
    row = lambda i: (i, 0)
    full = lambda i: (0, 0)
    x, xb, logits = pl.pallas_call(
        _out_ln_router_body,
        out_shape=(jax.ShapeDtypeStruct((m, n), F32), jax.ShapeDtypeStruct((m, n), BF16),
                   jax.ShapeDtypeStruct((m, 128), F32)),
        grid=(m // tm,),
        in_specs=[pl.BlockSpec((tm, k), row), pl.BlockSpec((k, n), full), pl.BlockSpec((tm, n), row),
                  pl.BlockSpec((1, n), full), pl.BlockSpec((1, n), full), pl.BlockSpec((n, 128), full)],
        out_specs=(pl.BlockSpec((tm, n), row), pl.BlockSpec((tm, n), row), pl.BlockSpec((tm, 128), row)),
        compiler_params=_cparams(("parallel",)),
        name="nsa_out_ln_router",
    )(o, w_out, res, lg.reshape(1, n), lb.reshape(1, n), wr)
    return x, xb, logits[:, :ne]


def _dispatch_body(ir_ref, ic_ref, fl_ref, x_ref, rt_ref, o_ref):
    i = pl.program_id(0)
    flag = fl_ref[i]
    tok = ic_ref[i] * MOE_TC + lax.broadcasted_iota(jnp.int32, (MOE_BLK, MOE_TC), 1)
    onehot = jnp.where(rt_ref[...] == tok, 1.0, 0.0).astype(BF16)
    rows = _dot(onehot, x_ref[...])

    @pl.when(flag == 3)
    def _():
        o_ref[...] = rows.astype(o_ref.dtype)

    @pl.when(flag == 1)
    def _():
        o_ref[...] = (o_ref[...].astype(F32) + rows).astype(o_ref.dtype)


def _dispatch(x_bf, row_tok_col, item_r, item_c, flags):
    t, d = x_bf.shape
    r = row_tok_col.shape[0]
    ni = item_r.shape[0]
    gs = pltpu.PrefetchScalarGridSpec(
        num_scalar_prefetch=3, grid=(ni,),
        in_specs=[pl.BlockSpec((MOE_TC, d), lambda i, ir, ic, fl: (ic[i], 0)),
                  pl.BlockSpec((MOE_BLK, 1), lambda i, ir, ic, fl: (ir[i], 0))],
        out_specs=pl.BlockSpec((MOE_BLK, d), lambda i, ir, ic, fl: (ir[i], 0)))
    return pl.pallas_call(
        _dispatch_body, grid_spec=gs,
        out_shape=jax.ShapeDtypeStruct((r, d), BF16),
        compiler_params=_cparams(("arbitrary",)),
        name="moe_dispatch",
    )(item_r, item_c, flags, x_bf, row_tok_col)


def _expert_body(be_ref, x_ref, wg_ref, wu_ref, wd_ref, rw_ref, o_ref, acc_scr):
    f = pl.program_id(1)
    xb = x_ref[...]
    h = jax.nn.silu(_dot(xb, wg_ref[0])) * _dot(xb, wu_ref[0])
    part = _dot(h.astype(BF16), wd_ref[0])

    @pl.when(f == 0)
    def _():
        acc_scr[...] = part

    @pl.when(f > 0)
    def _():
        acc_scr[...] += part

    @pl.when(f == pl.num_programs(1) - 1)
    def _():
        o_ref[...] = (acc_scr[...] * rw_ref[...]).astype(o_ref.dtype)


def _experts(xs, wg, wu, wd, row_w_col, block_expert):
    r, d = xs.shape
    nbk = r // MOE_BLK
    nf = wg.shape[2] // MOE_TF
    gs = pltpu.PrefetchScalarGridSpec(
        num_scalar_prefetch=1, grid=(nbk, nf),
        in_specs=[pl.BlockSpec((MOE_BLK, d), lambda i, f, be: (i, 0)),
                  pl.BlockSpec((1, d, MOE_TF), lambda i, f, be: (be[i], 0, f)),
                  pl.BlockSpec((1, d, MOE_TF), lambda i, f, be: (be[i], 0, f)),
                  pl.BlockSpec((1, MOE_TF, d), lambda i, f, be: (be[i], f, 0)),
                  pl.BlockSpec((MOE_BLK, 1), lambda i, f, be: (i, 0))],
        out_specs=pl.BlockSpec((MOE_BLK, d), lambda i, f, be: (i, 0)),
        scratch_shapes=[pltpu.VMEM((MOE_BLK, d), F32)])
    return pl.pallas_call(
        _expert_body, grid_spec=gs,
        out_shape=jax.ShapeDtypeStruct((r, d), BF16),
        compiler_params=_cparams(("parallel", "arbitrary")),
        name="moe_experts",
    )(block_expert, xs, wg, wu, wd, row_w_col)


def _combine_body(ir_ref, ic_ref, fl_ref, y_ref, rt_ref, x_ref, g_ref, b_ref, o_ref):
    i = pl.program_id(0)
    flag = fl_ref[i]
    tok = ic_ref[i] * MOE_TC + lax.broadcasted_iota(jnp.int32, (MOE_TC, MOE_BLK), 0)
    onehot = jnp.where(rt_ref[0] == tok, 1.0, 0.0).astype(BF16)
    part = _dot(onehot, y_ref[...])

    @pl.when((flag & 3) == 3)
    def _():
        o_ref[...] = part

    @pl.when((flag & 3) == 1)
    def _():
        o_ref[...] += part

    @pl.when((flag & 4) == 4)
    def _():
        o_ref[...] = _layer_norm(ALPHA * x_ref[...] + o_ref[...], g_ref[...], b_ref[...])


def _combine_ln(out_rows, row_tok_lane, item_r, item_c, flags, x, g, b):
    r, d = out_rows.shape
    t = x.shape[0]
    ni = item_r.shape[0]
    vec = pl.BlockSpec((1, d), lambda i, ir, ic, fl: (0, 0))
    gs = pltpu.PrefetchScalarGridSpec(
        num_scalar_prefetch=3, grid=(ni,),
        in_specs=[pl.BlockSpec((MOE_BLK, d), lambda i, ir, ic, fl: (ir[i], 0)),
                  pl.BlockSpec((1, 1, MOE_BLK), lambda i, ir, ic, fl: (ir[i], 0, 0)),
                  pl.BlockSpec((MOE_TC, d), lambda i, ir, ic, fl: (ic[i], 0)),
                  vec, vec],
        out_specs=pl.BlockSpec((MOE_TC, d), lambda i, ir, ic, fl: (ic[i], 0)))
    return pl.pallas_call(
        _combine_body, grid_spec=gs,
        out_shape=jax.ShapeDtypeStruct((t, d), F32),
        compiler_params=_cparams(("arbitrary",)),
        name="moe_combine_ln",
    )(item_r, item_c, flags, out_rows, row_tok_lane, x, g.reshape(1, d), b.reshape(1, d))


def _moe_plan(top_idx, wts, t):
    e, blk, tc = N_EXPERTS, MOE_BLK, MOE_TC
    a = t * TOP_K
    i32 = jnp.int32
    exp_flat = top_idx.reshape(a).astype(i32)
    tok_flat = jnp.arange(a, dtype=i32) // TOP_K
    _, tok_sorted, w_sorted = lax.sort((exp_flat, tok_flat, wts.reshape(a)), num_keys=1, is_stable=True)
    counts = jnp.sum((exp_flat[:, None] == jnp.arange(e, dtype=i32)[None, :]).astype(i32), 0)
    padded = ((counts + blk - 1) // blk) * blk
    grp_start = jnp.cumsum(counts) - counts
    pad_end = jnp.cumsum(padded)
    pad_start = pad_end - padded
    nbk = a // blk + e
    r = nbk * blk
    tok_ext = jnp.concatenate([tok_sorted, jnp.full((r - a,), -1, i32)])
    w_ext = jnp.concatenate([w_sorted, jnp.zeros((r - a,), F32)])
    rows = jnp.arange(r, dtype=i32)
    row_tok = jnp.full((r,), -1, i32)
    row_w = jnp.zeros((r,), F32)
    for ee in range(e):
        inside = (rows >= pad_start[ee]) & (rows < pad_start[ee] + counts[ee])
        shift = pad_start[ee] - grp_start[ee]
        row_tok = jnp.where(inside, jnp.roll(tok_ext, shift), row_tok)
        row_w = jnp.where(inside, jnp.roll(w_ext, shift), row_w)
    blk_first = jnp.arange(nbk, dtype=i32) * blk
    block_expert = jnp.minimum(jnp.sum((pad_end[None, :] <= blk_first[:, None]).astype(i32), 1), e - 1)

    rt = row_tok.reshape(nbk, blk)
    valid = rt >= 0
    t_lo = jnp.min(jnp.where(valid, rt, t), 1)
    t_hi = jnp.max(rt, 1)
    has = t_hi >= 0
    c_lo = jnp.where(has, t_lo // tc, 0)
    c_hi = jnp.where(has, t_hi // tc, 0)
    n_it = c_hi - c_lo + 1
    off_end = jnp.cumsum(n_it)
    off_start = off_end - n_it
    total = off_end[-1]
    ni = nbk + e * (t // tc)
    idx = jnp.arange(ni, dtype=i32)
    ok = idx < total
    ir = jnp.minimum(jnp.sum((off_end[None, :] <= idx[:, None]).astype(i32), 1), nbk - 1)
    ic = jnp.where(ok, c_lo[ir] + idx - off_start[ir], c_hi[nbk - 1]).astype(i32)
    first = ok & (idx == off_start[ir])
    d_flags = ok.astype(i32) + 2 * first.astype(i32)

    key = jnp.where(ok, ic * nbk + ir, jnp.iinfo(jnp.int32).max)
    perm = jnp.argsort(key)
    ok2 = ok[perm]
    last = total - 1
    cr = jnp.where(ok2, ir[perm], ir[perm][last]).astype(i32)
    cc = jnp.where(ok2, ic[perm], ic[perm][last]).astype(i32)
    first2 = ok2 & jnp.concatenate([jnp.ones((1,), bool), cc[1:] != cc[:-1]])
    last2 = ok2 & jnp.concatenate([(cc[1:] != cc[:-1]) | ~ok2[1:], jnp.ones((1,), bool)])
    c_flags = ok2.astype(i32) + 2 * first2.astype(i32) + 4 * last2.astype(i32)
    return dict(row_tok=row_tok, row_w=row_w, block_expert=block_expert,
                d_items=(ir, ic, d_flags), c_items=(cr, cc, c_flags), nbk=nbk)


def _moe_ln(x2, x_bf, logits, wg, wu, wd, ln_g, ln_b):
    t, d = x2.shape
    top_val, top_idx = lax.top_k(logits, TOP_K)
    wts = jax.nn.softmax(top_val, -1)
    plan = _moe_plan(top_idx, wts, t)
    nbk = plan["nbk"]
    xs = _dispatch(x_bf, plan["row_tok"].reshape(-1, 1), *plan["d_items"])
    out_rows = _experts(xs, wg.astype(BF16), wu.astype(BF16), wd.astype(BF16),
                        plan["row_w"].reshape(-1, 1), plan["block_expert"])
    return _combine_ln(out_rows, plan["row_tok"].reshape(nbk, 1, MOE_BLK), *plan["c_items"], x2, ln_g, ln_b)


def _forward(x, mla_w_in, mla_q_norm, mla_w_q_up, mla_kv_norm, mla_w_kv_up, mla_w_out, nsa_w_in,
             nsa_cmp_pe_k, nsa_cmp_w1_k, nsa_cmp_w2_k, nsa_cmp_pe_v, nsa_cmp_w1_v, nsa_cmp_w2_v,
             nsa_w_out, rel_bias, ffn_w_gate, ffn_w_up, ffn_w_down, moe_w_router, moe_w_gate,
             moe_w_up, moe_w_down, ln_mix_g, ln_mix_b, ln_ffn_g, ln_ffn_b):
    b, s, d = x.shape
    x2 = x.reshape(b * s, d)
    o = _mla_mixer(x2, b, s, mla_w_in[0], mla_q_norm[0], mla_w_q_up[0], mla_kv_norm[0], mla_w_kv_up[0])
    x2 = _linear(o, mla_w_out[0].astype(BF16), tm=1024, tn=d, out_dtype=F32,
                 ln=(x2, ln_mix_g[0], ln_mix_b[0]), name="mla_out_ln")
    x2 = _ffn_ln(x2, ffn_w_gate[0].astype(BF16), ffn_w_up[0].astype(BF16), ffn_w_down[0].astype(BF16),
                 ln_ffn_g[0], ln_ffn_b[0])
    o = _nsa_mixer(x2, b, s, nsa_w_in[0], nsa_cmp_pe_k[0], nsa_cmp_w1_k[0], nsa_cmp_w2_k[0],
                   nsa_cmp_pe_v[0], nsa_cmp_w1_v[0], nsa_cmp_w2_v[0], rel_bias)
    x2, x_bf, logits = _out_ln_router(o, nsa_w_out[0].astype(BF16), x2, ln_mix_g[1], ln_mix_b[1],
                                      moe_w_router[0])
    x2 = _moe_ln(x2, x_bf, logits, moe_w_gate[0], moe_w_up[0], moe_w_down[0], ln_ffn_g[1], ln_ffn_b[1])
    return x2.reshape(b, s, d)


@jax.jit
def kernel(x, mla_w_in, mla_q_norm, mla_w_q_up, mla_kv_norm, mla_w_kv_up, mla_w_out, nsa_w_in,
           nsa_cmp_pe_k, nsa_cmp_w1_k, nsa_cmp_w2_k, nsa_cmp_pe_v, nsa_cmp_w1_v, nsa_cmp_w2_v,
           nsa_w_out, rel_bias, ffn_w_gate, ffn_w_up, ffn_w_down, moe_w_router, moe_w_gate,
           moe_w_up, moe_w_down, ln_mix_g, ln_mix_b, ln_ffn_g, ln_ffn_b):
    return _forward(x, mla_w_in, mla_q_norm, mla_w_q_up, mla_kv_norm, mla_w_kv_up, mla_w_out, nsa_w_in,
                    nsa_cmp_pe_k, nsa_cmp_w1_k, nsa_cmp_w2_k, nsa_cmp_pe_v, nsa_cmp_w1_v, nsa_cmp_w2_v,
                    nsa_w_out, rel_bias, ffn_w_gate, ffn_w_up, ffn_w_down, moe_w_router, moe_w_gate,
                    moe_w_up, moe_w_down, ln_mix_g, ln_mix_b, ln_ffn_g, ln_ffn_b)
```

```python
import functools
import math

import numpy as np
import jax
import jax.numpy as jnp
from jax import lax
from jax.experimental import pallas as pl
from jax.experimental.pallas import tpu as pltpu

F32 = jnp.float32
BF16 = jnp.bfloat16

D_MODEL = 1024
DEPTH = 2

MLA_HEADS = 8
MLA_Q_RANK = 512
MLA_KV_RANK = 256
MLA_NOPE = 128
MLA_ROPE = 64
MLA_V = 128
ROPE_THETA = 10000.0

NSA_HEADS = 16
NSA_GROUPS = 4
NSA_HG = NSA_HEADS // NSA_GROUPS
NSA_DK = 96
NSA_DV = 64
CMP_LEN = 32
CMP_STRIDE = 16
SEL_BLOCK = 64
SEL_TOPN = 16
WINDOW = 512
N_BRANCH = 3
FORCE = 1e6

REL_BUCKETS = 32
REL_MAX_DIST = 128

D_FF = 2816
N_EXPERTS = 8
TOP_K = 2
D_FF_EXPERT = 3584

LN_EPS = 1e-5
RMS_EPS = 1e-6

ALPHA = (2.0 * DEPTH) ** 0.25

NEG = -1e30
NEG_TEST = -1e29

V7X_VMEM_LIMIT = 56 * 1024 * 1024

LOG2E = 1.4426950408889634

NSA_TQ = 256
NSA_LANES = NSA_HG * NSA_TQ
NSA_KC = 256
NSA_DKP = 128
NSA_GATE_ROWS = 8
NSA_BAND = 24
NSA_SEL_ROWS = 16
NSA_VROWS = NSA_DV + 16
NSA_HEAD_SLOTS = 4
NSA_FAR_UNROLL = 4
NSA_TILES_PER_VARIANT = 4

MOE_BLK = 512
MOE_TC = 512
MOE_TF = 1792


def _cparams(sem, vmem=V7X_VMEM_LIMIT):
    return pltpu.CompilerParams(dimension_semantics=sem, vmem_limit_bytes=vmem)


def _layer_norm(r, g, b):
    mu = jnp.mean(r, -1, keepdims=True)
    d = r - mu
    var = jnp.mean(d * d, -1, keepdims=True)
    return d * lax.rsqrt(var + LN_EPS) * g + b


def _rms_norm(x, g):
    return x * lax.rsqrt(jnp.mean(x * x, -1, keepdims=True) + RMS_EPS) * g


def _split3(a):
    a1 = a.astype(BF16)
    r1 = a - a1.astype(F32)
    a2 = r1.astype(BF16)
    a3 = (r1 - a2.astype(F32)).astype(BF16)
    return a1, a2, a3


def _dot(a, b):
    return jnp.dot(a, b, preferred_element_type=F32)


def _dot_nt(a, b):
    return lax.dot_general(a, b, (((1,), (1,)), ((), ())), preferred_element_type=F32)


def _linear_body(*refs, has_ln):
    it = iter(refs)
    x_ref = next(it)
    w_ref = next(it)
    if has_ln:
        res_ref, lg_ref, lb_ref = next(it), next(it), next(it)
    o_ref = next(it)
    acc = _dot(x_ref[...].astype(BF16), w_ref[...])
    if has_ln:
        acc = _layer_norm(ALPHA * res_ref[...] + acc, lg_ref[...], lb_ref[...])
    o_ref[...] = acc.astype(o_ref.dtype)


def _linear(x, w, *, tm, tn, out_dtype, ln=None, name):
    m, k = x.shape
    n = w.shape[1]
    assert m % tm == 0 and n % tn == 0
    in_specs = [pl.BlockSpec((tm, k), lambda i, j: (i, 0)),
                pl.BlockSpec((k, tn), lambda i, j: (0, j))]
    args = [x, w]
    if ln is not None:
        assert tn == n
        res, lg, lb = ln
        in_specs += [pl.BlockSpec((tm, n), lambda i, j: (i, 0)),
                     pl.BlockSpec((1, n), lambda i, j: (0, 0)),
                     pl.BlockSpec((1, n), lambda i, j: (0, 0))]
        args += [res, lg.reshape(1, n), lb.reshape(1, n)]
    return pl.pallas_call(
        functools.partial(_linear_body, has_ln=ln is not None),
        out_shape=jax.ShapeDtypeStruct((m, n), out_dtype),
        grid=(m // tm, n // tn),
        in_specs=in_specs,
        out_specs=pl.BlockSpec((tm, tn), lambda i, j: (i, j)),
        compiler_params=_cparams(("parallel", "arbitrary")),
        name=name,
    )(*args)


def _rope_tables(s):
    half = MLA_ROPE // 2
    freq = ROPE_THETA ** (-jnp.arange(half, dtype=F32) / half)
    ang = jnp.arange(s).astype(F32)[:, None] * freq[None, :]
    cos, sin = jnp.cos(ang), jnp.sin(ang)
    return jnp.concatenate([cos, cos], -1), jnp.concatenate([-sin, sin], -1)


MLA_DQ = MLA_NOPE + MLA_ROPE
MLA_QROWS = MLA_NOPE + 2 * MLA_ROPE
MLA_T = 512
MLA_HPS = 2
MLA_SUB = 256
MLA_FAR_UNROLL = 4


def _mla_q_body(lat_ref, g_ref, w_ref, cos_ref, sin_ref, o_ref, *, qscale):
    xn = _rms_norm(lat_ref[...], g_ref[...]).astype(BF16)
    y = _dot_nt(w_ref[...], xn)
    cos, sin = cos_ref[...], sin_ref[...]
    for h in range(MLA_HEADS):
        r0 = h * MLA_QROWS
        o_ref[0, h, :MLA_NOPE, :] = (y[r0:r0 + MLA_NOPE] * qscale).astype(BF16)
        a = y[r0 + MLA_NOPE:r0 + MLA_DQ]
        bb = y[r0 + MLA_DQ:r0 + MLA_QROWS]
        o_ref[0, h, MLA_NOPE:, :] = ((a * cos + bb * sin) * qscale).astype(BF16)


def _mla_q_proj(lat, gain, w_t, cos_t, sin_t, b, s, *, tm=MLA_T):
    ns = s // tm
    body = functools.partial(_mla_q_body, qscale=(MLA_DQ ** -0.5) * LOG2E)
    return pl.pallas_call(
        body,
        out_shape=jax.ShapeDtypeStruct((b, MLA_HEADS, MLA_DQ, s), BF16),
        grid=(b * ns,),
        in_specs=[pl.BlockSpec((tm, MLA_Q_RANK), lambda i: (i, 0)),
                  pl.BlockSpec((1, MLA_Q_RANK), lambda i: (0, 0)),
                  pl.BlockSpec(w_t.shape, lambda i: (0, 0)),
                  pl.BlockSpec((MLA_ROPE, tm), lambda i: (0, i % ns)),
                  pl.BlockSpec((MLA_ROPE, tm), lambda i: (0, i % ns))],
        out_specs=pl.BlockSpec((1, MLA_HEADS, MLA_DQ, tm), lambda i: (i // ns, 0, 0, i % ns)),
        compiler_params=_cparams(("parallel",)),
        name="mla_q_proj",
    )(lat, gain.reshape(1, -1), w_t, cos_t, sin_t)


def _mla_kv_body(lat_ref, g_ref, wk_ref, wvt_ref, kr_ref, cos_ref, sin_ref, k_ref, vt_ref):
    xn = _rms_norm(lat_ref[...], g_ref[...]).astype(BF16)
    kn = _dot(xn, wk_ref[...]).astype(BF16)
    vt = _dot_nt(wvt_ref[...], xn).astype(BF16)
    kr = kr_ref[...]
    rot = (kr[:, :MLA_ROPE] * cos_ref[...] + kr[:, MLA_ROPE:] * sin_ref[...]).astype(BF16)
    for h in range(MLA_HEADS):
        k_ref[0, h, :, :MLA_NOPE] = kn[:, h * MLA_NOPE:(h + 1) * MLA_NOPE]
        k_ref[0, h, :, MLA_NOPE:] = rot
        for kk in range(MLA_T // MLA_SUB):
            vt_ref[0, h, kk] = vt[h * MLA_V:(h + 1) * MLA_V, kk * MLA_SUB:(kk + 1) * MLA_SUB]


def _mla_kv_proj(lat, gain, wk, wv_t, cosx, sinx, b, s, *, tm=MLA_T):
    ns = s // tm
    nsub = tm // MLA_SUB
    return pl.pallas_call(
        _mla_kv_body,
        out_shape=(jax.ShapeDtypeStruct((b, MLA_HEADS, s, MLA_DQ), BF16),
                   jax.ShapeDtypeStruct((b, MLA_HEADS, ns * nsub, MLA_V, MLA_SUB), BF16)),
        grid=(b * ns,),
        in_specs=[pl.BlockSpec((tm, MLA_KV_RANK), lambda i: (i, MLA_Q_RANK // MLA_KV_RANK)),
                  pl.BlockSpec((1, MLA_KV_RANK), lambda i: (0, 0)),
                  pl.BlockSpec(wk.shape, lambda i: (0, 0)),
                  pl.BlockSpec(wv_t.shape, lambda i: (0, 0)),
                  pl.BlockSpec((tm, 2 * MLA_ROPE),
                               lambda i: (i, (MLA_Q_RANK + MLA_KV_RANK) // (2 * MLA_ROPE))),
                  pl.BlockSpec((tm, MLA_ROPE), lambda i: (i % ns, 0)),
                  pl.BlockSpec((tm, MLA_ROPE), lambda i: (i % ns, 0))],
        out_specs=(pl.BlockSpec((1, MLA_HEADS, tm, MLA_DQ), lambda i: (i // ns, 0, i % ns, 0)),
                   pl.BlockSpec((1, MLA_HEADS, nsub, MLA_V, MLA_SUB), lambda i: (i // ns, 0, i % ns, 0, 0))),
        compiler_params=_cparams(("parallel",)),
        name="mla_kv_proj",
    )(lat, gain.reshape(1, -1), wk, wv_t, lat, cosx, sinx)


def _mla_attn_body(qt_ref, k_ref, vt_ref, o_ref, *scr):
    i = pl.program_id(2)
    sub = MLA_SUB
    nsub = MLA_T // sub
    assert nsub == 2
    chains_all = [(h, ql) for ql in range(nsub) for h in range(MLA_HPS)]
    nch = len(chains_all)
    ring_a, ring_b, scr = scr[:nch], scr[nch:2 * nch], scr[2 * nch:]
    state = {(h, ql): scr[3 * (nsub * h + ql):3 * (nsub * h + ql) + 3]
             for h in range(MLA_HPS) for ql in range(nsub)}
    for m_scr, l_scr, acc_scr in state.values():
        m_scr[...] = jnp.full_like(m_scr, NEG)
        l_scr[...] = jnp.zeros_like(l_scr)
        acc_scr[...] = jnp.zeros_like(acc_scr)

    def issue(sk, ring, chains):
        for idx, (h, ql) in enumerate(chains_all):
            if (h, ql) in chains:
                k = k_ref[0, h, pl.ds(pl.multiple_of(sk * sub, sub), sub), :]
                ring[idx][...] = _dot(k, qt_ref[0, h, :, ql * sub:(ql + 1) * sub])
            yield

    def consume(sk, ring, chains, diag_ql=None):
        for idx, (h, ql) in enumerate(chains_all):
            if (h, ql) in chains:
                m_scr, l_scr, acc_scr = state[(h, ql)]
                s = ring[idx][...]
                if ql == diag_ql:
                    key = lax.broadcasted_iota(jnp.int32, s.shape, 0)
                    qry = lax.broadcasted_iota(jnp.int32, s.shape, 1)
                    s = jnp.where(key <= qry, s, NEG)
                m_old = m_scr[...]
                m_new = jnp.maximum(m_old, jnp.max(s, 0, keepdims=True))
                a = jnp.exp2(m_old - m_new)
                p = jnp.exp2(s - m_new)
                l_scr[...] = a * l_scr[...] + jnp.sum(p, 0, keepdims=True)
                acc_scr[...] = a * acc_scr[...] + _dot(vt_ref[0, h, sk], p.astype(BF16))
                m_scr[...] = m_new
            yield

    def interleave(*gens):
        for _ in zip(*gens):
            pass

    interleave(issue(0, ring_a, chains_all))

    def pair(c0):
        interleave(issue(c0 + 1, ring_b, chains_all), consume(c0, ring_a, chains_all))
        interleave(issue(c0 + 2, ring_a, chains_all), consume(c0 + 1, ring_b, chains_all))

    def trip(j, carry):
        for pr in range(MLA_FAR_UNROLL):
            pair(2 * (MLA_FAR_UNROLL * j + pr))
        return carry

    lax.fori_loop(0, i // MLA_FAR_UNROLL, trip, 0)
    lax.fori_loop(i // MLA_FAR_UNROLL * MLA_FAR_UNROLL, i, lambda j, carry: (pair(2 * j), carry)[1], 0)
    upper = [c for c in chains_all if c[1] == 1]
    interleave(issue(2 * i + 1, ring_b, upper), consume(2 * i, ring_a, chains_all, diag_ql=0))
    interleave(consume(2 * i + 1, ring_b, upper, diag_ql=1))
    for (h, ql), (m_scr, l_scr, acc_scr) in state.items():
        o_t = acc_scr[...] * (1.0 / jnp.maximum(l_scr[...], 1e-30))
        o_ref[0, ql * sub:(ql + 1) * sub, h * MLA_V:(h + 1) * MLA_V] = o_t.T.astype(o_ref.dtype)


def _mla_attention(q_t, k, v_t, b, s):
    tq = MLA_T
    nq = s // tq
    hp = MLA_HPS
    nsub = tq // MLA_SUB
    sub_state = [pltpu.VMEM((1, MLA_SUB), F32), pltpu.VMEM((1, MLA_SUB), F32), pltpu.VMEM((MLA_V, MLA_SUB), F32)]
    ring = hp * nsub * [pltpu.VMEM((MLA_SUB, MLA_SUB), F32)]
    return pl.pallas_call(
        _mla_attn_body,
        out_shape=jax.ShapeDtypeStruct((b, s, MLA_HEADS * MLA_V), BF16),
        grid=(b, MLA_HEADS // hp, nq),
        in_specs=[pl.BlockSpec((1, hp, MLA_DQ, tq), lambda bb, h, i: (bb, h, 0, i)),
                  pl.BlockSpec((1, hp, s, MLA_DQ), lambda bb, h, i: (bb, h, 0, 0)),
                  pl.BlockSpec((1, hp, nq * nsub, MLA_V, MLA_SUB), lambda bb, h, i: (bb, h, 0, 0, 0))],
        out_specs=pl.BlockSpec((1, tq, hp * MLA_V), lambda bb, h, i: (bb, i, h)),
        scratch_shapes=2 * ring + hp * nsub * sub_state,
        compiler_params=_cparams(("parallel", "parallel", "arbitrary")),
        name="mla_attention",
    )(q_t, k, v_t)


def _mla_mixer(x2, b, s, w_in, q_norm, w_q_up, kv_norm, w_kv_up):
    r0 = MLA_Q_RANK + MLA_KV_RANK
    half = MLA_ROPE // 2
    w_in_ext = jnp.concatenate([w_in, w_in[:, r0 + half:r0 + MLA_ROPE], w_in[:, r0:r0 + half]], 1)
    lat = _linear(x2, w_in_ext.astype(BF16), tm=512, tn=w_in_ext.shape[1], out_dtype=F32, name="mla_in")
    wq = w_q_up.reshape(MLA_Q_RANK, MLA_HEADS, MLA_DQ)
    wr = wq[..., MLA_NOPE:]
    wq = jnp.concatenate([wq, wr[..., half:], wr[..., :half]], -1)
    wq_t = wq.reshape(MLA_Q_RANK, MLA_HEADS * MLA_QROWS).T.astype(BF16)
    wkv = w_kv_up.reshape(MLA_KV_RANK, MLA_HEADS, MLA_NOPE + MLA_V)
    wk = wkv[..., :MLA_NOPE].reshape(MLA_KV_RANK, MLA_HEADS * MLA_NOPE).astype(BF16)
    wv_t = wkv[..., MLA_NOPE:].reshape(MLA_KV_RANK, MLA_HEADS * MLA_V).T.astype(BF16)
    cosx, sinx = _rope_tables(s)
    q_t = _mla_q_proj(lat, q_norm, wq_t, cosx.T, sinx.T, b, s)
    k, v_t = _mla_kv_proj(lat, kv_norm, wk, wv_t, cosx, sinx, b, s)
    o = _mla_attention(q_t, k, v_t, b, s)
    return o.reshape(b * s, MLA_HEADS * MLA_V)


def _ffn_body(x_ref, wg_ref, wu_ref, wd_ref, lg_ref, lb_ref, o_ref, acc_scr):
    f = pl.program_id(1)
    xb = x_ref[...].astype(BF16)
    h = jax.nn.silu(_dot(xb, wg_ref[...])) * _dot(xb, wu_ref[...])
    part = _dot(h.astype(BF16), wd_ref[...])

    @pl.when(f == 0)
    def _():
        acc_scr[...] = part

    @pl.when(f > 0)
    def _():
        acc_scr[...] += part

    @pl.when(f == pl.num_programs(1) - 1)
    def _():
        o_ref[...] = _layer_norm(ALPHA * x_ref[...] + acc_scr[...], lg_ref[...], lb_ref[...])


def _ffn_ln(x, wg, wu, wd, lg, lb, *, tm=512, tf=1408):
    m, d = x.shape
    dff = wg.shape[1]
    assert dff % tf == 0
    vec = pl.BlockSpec((1, d), lambda i, f: (0, 0))
    return pl.pallas_call(
        _ffn_body,
        out_shape=jax.ShapeDtypeStruct((m, d), F32),
        grid=(m // tm, dff // tf),
        in_specs=[pl.BlockSpec((tm, d), lambda i, f: (i, 0)),
                  pl.BlockSpec((d, tf), lambda i, f: (0, f)),
                  pl.BlockSpec((d, tf), lambda i, f: (0, f)),
                  pl.BlockSpec((tf, d), lambda i, f: (f, 0)),
                  vec, vec],
        out_specs=pl.BlockSpec((tm, d), lambda i, f: (i, 0)),
        scratch_shapes=[pltpu.VMEM((tm, d), F32)],
        compiler_params=_cparams(("parallel", "arbitrary")),
        name="ffn_ln",
    )(x, wg, wu, wd, lg.reshape(1, d), lb.reshape(1, d))


def _compress_body(c_ref, pea_ref, peb_ref, w1a_ref, w1b_ref, w2_ref, o_ref):
    c = c_ref[0, 0, 0].astype(F32)
    ya = _dot((c + pea_ref[...]).astype(BF16), w1a_ref[...])
    yb = _dot((c + peb_ref[...]).astype(BF16), w1b_ref[...])
    yb_next = jnp.concatenate([yb[1:], jnp.zeros_like(yb[:1])], 0)
    h = jax.nn.gelu(ya + yb_next)
    o_ref[0] = _dot(h.astype(BF16), w2_ref[...]).astype(o_ref.dtype)


def _compress(chunks, kv, pe, w1, w2, dpad):
    b, g, _, nch, k = chunks.shape
    bg = b * g
    d = pe.shape[1]
    dh, dout = w2.shape
    half = CMP_LEN // 2
    assert half == CMP_STRIDE and k == CMP_STRIDE * dpad

    def pad_tok(a):
        a = jnp.pad(a, [(0, 0), (0, dpad - d)] + [(0, 0)] * (a.ndim - 2))
        return a[:half].reshape((k,) + a.shape[2:]), a[half:].reshape((k,) + a.shape[2:])

    pe_a, pe_b = pad_tok(pe)
    w1a, w1b = pad_tok(w1)
    full = lambda i: (0, 0)
    return pl.pallas_call(
        _compress_body,
        out_shape=jax.ShapeDtypeStruct((bg, nch, dout), BF16),
        grid=(bg,),
        in_specs=[pl.BlockSpec((1, 1, 1, nch, k), lambda i: (i // g, i % g, kv, 0, 0)),
                  pl.BlockSpec((1, k), full), pl.BlockSpec((1, k), full),
                  pl.BlockSpec((k, dh), full), pl.BlockSpec((k, dh), full),
                  pl.BlockSpec((dh, dout), full)],
        out_specs=pl.BlockSpec((1, nch, dout), lambda i: (i, 0, 0)),
        compiler_params=_cparams(("parallel",)),
        name="nsa_compress",
    )(chunks, pe_a.reshape(1, k), pe_b.reshape(1, k), w1a.astype(BF16), w1b.astype(BF16), w2.astype(BF16))


def _rel_bucket_np(dist):
    n = np.maximum(dist, 0)
    max_exact = REL_BUCKETS // 2
    nf = np.maximum(n, 1).astype(np.float32)
    large = max_exact + (np.log(nf / np.float32(max_exact)) / np.float32(math.log(REL_MAX_DIST / max_exact))
                         * np.float32(REL_BUCKETS - max_exact)).astype(np.int32)
    large = np.minimum(large, REL_BUCKETS - 1)
    return np.where(n < max_exact, n, large).astype(np.int32)


def _nsa_tables(rel_bias, s):
    g, hg, tq, kc = NSA_GROUPS, NSA_HG, NSA_TQ, NSA_KC
    assert tq == kc and tq % CMP_STRIDE == 0 and WINDOW == 2 * kc
    assert np.all(_rel_bucket_np(np.arange(tq // 2 - 15, s + tq)) == REL_BUCKETS - 1)
    rb = rel_bias.reshape(REL_BUCKETS, g, hg) * LOG2E

    def tile(base, step, rows, valid):
        p = tq + step * rows
        k = np.arange(p)
        k = np.where(k < p - step * (rows - 1), k, k - p)
        d = base + k
        vec = jnp.where(valid(d)[:, None, None], rb[_rel_bucket_np(d)], NEG)
        vec = vec.transpose(1, 2, 0)
        flat = jnp.tile(vec, (1, 1, rows))[..., :rows * (p - step)]
        mat = flat.reshape(g, hg, rows, p - step)[..., :tq]
        return mat.transpose(0, 2, 1, 3).reshape(g, rows, hg * tq)

    causal = lambda d: d >= 0
    far = jnp.broadcast_to(rb[REL_BUCKETS - 1][:, None, :, None], (g, 1, hg, tq)).reshape(g, 1, hg * tq)
    rel = lambda x: jnp.where(x > NEG_TEST, x - far, NEG)
    tiles = jnp.stack([rel(tile(0, 1, kc, causal)), rel(tile(tq, 1, kc, causal)),
                       rel(tile(2 * tq, 1, kc, lambda d: d < WINDOW)),
                       jnp.zeros((g, kc, hg * tq), F32)], 1)
    band = jnp.stack([rel(tile(8 * CMP_STRIDE - CMP_LEN + 1, CMP_STRIDE, NSA_BAND, causal)),
                      rel(tile(-(CMP_LEN - 1), CMP_STRIDE, NSA_BAND, causal))], 1)
    return tiles, band


def _overlap_t(nc_pad, nb):
    n = np.arange(nc_pad)[None, :]
    jb = np.arange(nb)[:, None]
    cstart = n * CMP_STRIDE
    cend = cstart + CMP_LEN - 1
    sstart = jb * SEL_BLOCK
    ov = (cstart <= sstart + SEL_BLOCK - 1) & (cend >= sstart) & (n < nc_pad - 1)
    return jnp.asarray(ov.astype(np.float32), BF16)


def _nsa_body(qt_ref, kc_ref, vct_ref, k_ref, vt_ref, gate_ref, tiles_ref,
              band_ref, ovt_ref, o_ref, s_scr, sel_scr, qa_scr, oc_scr, *scr, nb, nqt):
    t = pl.program_id(2)
    ring_a, ring_b, state = scr[:NSA_HG], scr[NSA_HG:2 * NSA_HG], scr[2 * NSA_HG:]
    nkc = k_ref.shape[3]
    L = NSA_LANES
    q_t = qt_ref[0, 0, 0]
    ncp = kc_ref.shape[2]
    per = NSA_KC // SEL_BLOCK
    blocks_per_tile = NSA_TQ // CMP_STRIDE

    n_slab = nb // NSA_SEL_ROWS

    n_win = jnp.minimum(t, 2) + 1
    n_slots = n_win + t + 1

    def slot_params(c):
        is_win = c < n_win
        j = c - n_win
        is_sel = jnp.logical_and(c >= n_win, j <= t)
        delta = t - j
        br = is_win.astype(jnp.int32)
        kidx = jnp.where(is_win, t - c, jnp.where(is_sel, j, 0))
        sidx = jnp.where(is_sel, j // (NSA_SEL_ROWS // per), jnp.where(is_win, 0, n_slab))
        tidx = jnp.where(is_win, c, jnp.where(jnp.logical_and(is_sel, delta < 2), delta, 3))
        return br, kidx, sidx, tidx

    def issue(c, ring, set_rows=True):
        br, kidx, sidx, _ = slot_params(c)
        k = k_ref[0, 0, br, kidx]
        if set_rows:
            qa_scr[NSA_DK:NSA_DK + NSA_SEL_ROWS, :] = sel_scr[sidx]
        for h in range(NSA_HG):
            ring[h][...] = _dot(k, qa_scr[:, h * NSA_TQ:(h + 1) * NSA_TQ])
            yield

    def consume(c, ring, near):
        br, kidx, _, tidx = slot_params(c)
        vt = vt_ref[0, 0, br, kidx]
        for h in range(NSA_HG):
            m_scr, acc_scr = state[2 * h:2 * h + 2]
            sc = ring[h][...]
            if near:
                sc = sc + tiles_ref[0, tidx, :, h * NSA_TQ:(h + 1) * NSA_TQ]
            m_old = m_scr[br]
            m_new = jnp.maximum(m_old, jnp.max(sc, 0, keepdims=True))
            pp = jnp.exp2(sc - m_new)
            acc_scr[br] = jnp.exp2(m_old - m_new) * acc_scr[br] + _dot(vt, pp.astype(BF16))
            m_scr[br] = m_new
            yield

    def interleave(*gens):
        for _ in zip(*gens):
            pass

    for h in range(NSA_HG):
        m_scr, acc_scr = state[2 * h:2 * h + 2]
        m_scr[...] = jnp.full_like(m_scr, NEG)
        acc_scr[...] = jnp.zeros_like(acc_scr)
    qa_scr[...] = q_t

    def compress_and_select(nrows, nblk):
        s = _dot(kc_ref[0, 0, :nrows], q_t)
        first = (t == 0).astype(jnp.int32)
        bs = pl.multiple_of((blocks_per_tile * t - 8) * (1 - first), 8)
        row = lax.broadcasted_iota(jnp.int32, (nrows, L), 0)
        s_scr[:nrows] = jnp.where(row < bs + NSA_BAND, s, NEG)
        s_scr[pl.ds(bs, NSA_BAND), :] += band_ref[0, first]
        s = s_scr[:nrows]
        m = jnp.max(s, 0, keepdims=True)
        m = jnp.where(m < NEG_TEST, 0.0, m)
        p = jnp.exp2(s - m)
        den = jnp.maximum(jnp.sum(p, 0, keepdims=True), 1e-30)
        p = p * (1.0 / den)
        oc_scr[...] = _dot(vct_ref[0, 0, :, :nrows], p.astype(BF16))

        psum = p[:, 0:NSA_TQ]
        for h in range(1, NSA_HG):
            psum = psum + p[:, h * NSA_TQ:(h + 1) * NSA_TQ]
        p1, p2, p3 = _split3(psum)
        ovt = ovt_ref[:nblk, :nrows]
        imp = _dot(ovt, p1) + _dot(ovt, p2) + _dot(ovt, p3)
        interleave(issue(0, ring_a, set_rows=False))
        blk = lax.broadcasted_iota(jnp.int32, (nblk, NSA_TQ), 0)
        lane = lax.broadcasted_iota(jnp.int32, (nblk, NSA_TQ), 1)
        cur = (NSA_TQ // SEL_BLOCK) * t + lane // SEL_BLOCK
        forced = (blk == 0) | (blk == cur) | (blk == cur - 1)
        v = jnp.where(blk > cur, -FORCE, jnp.where(forced, -jnp.inf, imp))
        blk_f = blk.astype(F32)
        sel = jnp.where(forced, 1.0, 0.0)
        for _ in range(min(SEL_TOPN, nblk) - 3):
            mx = jnp.max(v, 0, keepdims=True)
            idx = jnp.min(jnp.where(v == mx, blk_f, float(nblk)), 0, keepdims=True)
            hit = blk_f == idx
            sel = jnp.where(hit, 1.0, sel)
            v = jnp.where(hit, -jnp.inf, v)
        selneg = jnp.where(sel > 0.5, 0.0, NEG)
        selneg = jnp.concatenate([selneg] * NSA_HG, 1).astype(BF16)
        for u in range(nblk // NSA_SEL_ROWS):
            sel_scr[u] = selneg[NSA_SEL_ROWS * u:NSA_SEL_ROWS * (u + 1), :]

    n_var = -(-nqt // NSA_TILES_PER_VARIANT)
    for i in range(n_var):
        nrows = min(ncp, (i + 1) * NSA_TILES_PER_VARIANT * blocks_per_tile)
        nblk = min(nb, (i + 1) * NSA_TILES_PER_VARIANT * NSA_TQ // SEL_BLOCK)

        @pl.when(t // NSA_TILES_PER_VARIANT == i)
        def _(nrows=nrows, nblk=nblk):
            compress_and_select(nrows, nblk)

    sel_scr[n_slab] = jnp.full((NSA_SEL_ROWS, L), NEG, BF16)

    def make_trip(first_slot, near, pairs=1):
        def trip(i, carry):
            for pr in range(pairs):
                c0 = first_slot + 2 * (pairs * i + pr)
                interleave(issue(c0 + 1, ring_b), consume(c0, ring_a, near))
                interleave(issue(c0 + 2, ring_a), consume(c0 + 1, ring_b, near))
            return carry
        return trip

    far_pairs = jnp.maximum(n_slots - 2 - NSA_HEAD_SLOTS, 0) // 2
    far_trips = far_pairs // NSA_FAR_UNROLL
    rest_first = NSA_HEAD_SLOTS + 2 * NSA_FAR_UNROLL * far_trips
    tail_first = NSA_HEAD_SLOTS + 2 * far_pairs
    tail_trips = (jnp.maximum(n_slots - tail_first, 0) + 1) // 2
    make_trip(0, True, NSA_HEAD_SLOTS // 2)(0, 0)
    lax.fori_loop(0, far_trips, make_trip(NSA_HEAD_SLOTS, False, NSA_FAR_UNROLL), 0)
    lax.fori_loop(0, far_pairs - NSA_FAR_UNROLL * far_trips, make_trip(rest_first, False), 0)
    lax.fori_loop(0, tail_trips, make_trip(tail_first, True), 0)

    gate = gate_ref[0, 0, 0]
    outs = []
    for h in range(NSA_HG):
        sl = slice(h * NSA_TQ, (h + 1) * NSA_TQ)
        acc_scr = state[2 * h + 1]
        o_s = acc_scr[0, :NSA_DV] * (1.0 / jnp.maximum(acc_scr[0, NSA_DV:NSA_DV + 1], 1e-30))
        o_w = acc_scr[1, :NSA_DV] * (1.0 / jnp.maximum(acc_scr[1, NSA_DV:NSA_DV + 1], 1e-30))
        outs.append(gate[0:1, sl] * oc_scr[:, sl] + gate[1:2, sl] * o_s + gate[2:3, sl] * o_w)
    o_ref[0] = jnp.concatenate(outs, 0).T.astype(o_ref.dtype)


def _nsa_attention(q_t, kcmp, vcmp_t, k, v_t, gate, tiles, band, ovt):
    b, g, nqt = q_t.shape[:3]
    ncp = kcmp.shape[2]
    nkc = k.shape[3]
    nb = ovt.shape[0]
    L = NSA_LANES
    body = functools.partial(_nsa_body, nb=nb, nqt=nqt)
    ring = NSA_HG * [pltpu.VMEM((NSA_KC, NSA_TQ), F32)]
    head_state = [pltpu.VMEM((2, 1, NSA_TQ), F32), pltpu.VMEM((2, NSA_VROWS, NSA_TQ), F32)]
    grp = lambda bb, gg, t: (bb, gg, 0, 0)
    grp6 = lambda bb, gg, t: (bb, gg, 0, 0, 0, 0)
    return pl.pallas_call(
        body,
        out_shape=jax.ShapeDtypeStruct((b, nqt * NSA_TQ, g * NSA_HG * NSA_DV), BF16),
        grid=(b, g, nqt),
        in_specs=[pl.BlockSpec((1, 1, 1, NSA_DKP, L), lambda bb, gg, t: (bb, gg, t, 0, 0)),
                  pl.BlockSpec((1, 1, ncp, NSA_DKP), grp),
                  pl.BlockSpec((1, 1, NSA_DV, ncp), grp),
                  pl.BlockSpec((1, 1, 2, nkc, NSA_KC, NSA_DKP), grp6),
                  pl.BlockSpec((1, 1, 2, nkc, NSA_VROWS, NSA_KC), grp6),
                  pl.BlockSpec((1, 1, 1, N_BRANCH, L), lambda bb, gg, t: (bb, gg, t, 0, 0)),
                  pl.BlockSpec((1, 4, NSA_KC, L), lambda bb, gg, t: (gg, 0, 0, 0)),
                  pl.BlockSpec((1, 2, NSA_BAND, L), lambda bb, gg, t: (gg, 0, 0, 0)),
                  pl.BlockSpec(ovt.shape, lambda bb, gg, t: (0, 0))],
        out_specs=pl.BlockSpec((1, NSA_TQ, NSA_HG * NSA_DV), lambda bb, gg, t: (bb, t, gg)),
        scratch_shapes=[pltpu.VMEM((ncp, L), F32),
                        pltpu.VMEM((nb // NSA_SEL_ROWS + 1, NSA_SEL_ROWS, L), BF16),
                        pltpu.VMEM((NSA_DKP, L), BF16),
                        pltpu.VMEM((NSA_DV, L), F32)]
        + 2 * ring + NSA_HG * head_state,
        compiler_params=_cparams(("parallel", "parallel", "arbitrary")),
        name="nsa_attention",
    )(q_t, kcmp, vcmp_t, k, v_t, gate, tiles, band, ovt)


def _nsa_proj_body(x_ref, wq_ref, wk_ref, wv_ref, wg_ref, wc_ref,
                   q_ref, k_ref, v_ref, gate_ref, c_ref, *, qscale):
    g, hg, tq = NSA_GROUPS, NSA_HG, NSA_TQ
    xb = x_ref[...].astype(BF16)
    q_t = _dot_nt(wq_ref[...], xb) * qscale
    gate_t = jax.nn.sigmoid(_dot_nt(wg_ref[...], xb))
    for gg in range(g):
        for h in range(hg):
            head = gg * hg + h
            q_ref[0, gg, 0, :, h * tq:(h + 1) * tq] = q_t[head * NSA_DKP:(head + 1) * NSA_DKP].astype(BF16)
            r0 = head * NSA_GATE_ROWS
            gate_ref[0, gg, 0, :, h * tq:(h + 1) * tq] = gate_t[r0:r0 + N_BRANCH]
    k = _dot(xb, wk_ref[...])
    v_t = _dot_nt(wv_ref[...], xb).astype(BF16)
    per = tq // SEL_BLOCK
    chunk = pl.program_id(0) % (NSA_SEL_ROWS // per)
    row = lax.broadcasted_iota(jnp.int32, (tq, NSA_DKP), 0)
    col = lax.broadcasted_iota(jnp.int32, (tq, NSA_DKP), 1)
    blk_flag = jnp.where(col - NSA_DK == per * chunk + row // SEL_BLOCK, 1.0, 0.0)
    ones_rows = jnp.where(lax.broadcasted_iota(jnp.int32, (NSA_VROWS - NSA_DV, tq), 0) == 0, 1.0, 0.0).astype(BF16)
    for gg in range(g):
        for br in range(2):
            kb = k[:, (br * g + gg) * NSA_DKP:(br * g + gg + 1) * NSA_DKP]
            if br == 0:
                kb = kb + blk_flag
            k_ref[0, gg, br, 0] = kb.astype(BF16)
            v_ref[0, gg, br, 0, :NSA_DV, :] = v_t[(br * g + gg) * NSA_DV:(br * g + gg + 1) * NSA_DV]
            v_ref[0, gg, br, 0, NSA_DV:, :] = ones_rows
    ctok = _dot(xb, wc_ref[...]).astype(BF16)
    for gg in range(g):
        for kv in range(2):
            c_ref[0, gg, kv, 0] = ctok[:, (kv * g + gg) * NSA_DKP:(kv * g + gg + 1) * NSA_DKP]


def _nsa_proj(x2, b, s, wq_t, wk, wv_t, wg_t, wc):
    g, hg, tq, L = NSA_GROUPS, NSA_HG, NSA_TQ, NSA_LANES
    nqt = s // tq
    assert nqt % (NSA_SEL_ROWS * SEL_BLOCK // tq) == 0 and (s // SEL_BLOCK) % NSA_SEL_ROWS == 0
    t, d = x2.shape
    full = lambda i: (0, 0)
    tile5 = lambda i: (i // nqt, 0, i % nqt, 0, 0)
    tile6 = lambda i: (i // nqt, 0, 0, i % nqt, 0, 0)
    body = functools.partial(_nsa_proj_body, qscale=(NSA_DK ** -0.5) * LOG2E)
    return pl.pallas_call(
        body,
        out_shape=(jax.ShapeDtypeStruct((b, g, nqt, NSA_DKP, L), BF16),
                   jax.ShapeDtypeStruct((b, g, 2, nqt, NSA_KC, NSA_DKP), BF16),
                   jax.ShapeDtypeStruct((b, g, 2, nqt, NSA_VROWS, NSA_KC), BF16),
                   jax.ShapeDtypeStruct((b, g, nqt, N_BRANCH, L), F32),
                   jax.ShapeDtypeStruct((b, g, 2, nqt, NSA_KC, NSA_DKP), BF16)),
        grid=(t // tq,),
        in_specs=[pl.BlockSpec((tq, d), lambda i: (i, 0)),
                  pl.BlockSpec(wq_t.shape, full), pl.BlockSpec(wk.shape, full), pl.BlockSpec(wv_t.shape, full),
                  pl.BlockSpec(wg_t.shape, full), pl.BlockSpec(wc.shape, full)],
        out_specs=(pl.BlockSpec((1, g, 1, NSA_DKP, L), tile5),
                   pl.BlockSpec((1, g, 2, 1, NSA_KC, NSA_DKP), tile6),
                   pl.BlockSpec((1, g, 2, 1, NSA_VROWS, NSA_KC), tile6),
                   pl.BlockSpec((1, g, 1, N_BRANCH, L), tile5),
                   pl.BlockSpec((1, g, 2, 1, NSA_KC, NSA_DKP), tile6)),
        compiler_params=_cparams(("parallel",)),
        name="nsa_proj",
    )(x2, wq_t, wk, wv_t, wg_t, wc)


def _nsa_mixer(x2, b, s, w_in, pe_k, w1_k, w2_k, pe_v, w1_v, w2_v, rel_bias):
    assert NSA_TQ == NSA_KC
    t, d = x2.shape
    h, g, hg, dk, dv = NSA_HEADS, NSA_GROUPS, NSA_HG, NSA_DK, NSA_DV
    nb = s // SEL_BLOCK
    sizes = [h * dk, g * dk, g * dv, g * dk, g * dv, g * dk, g * dv, h * N_BRANCH]
    c = [0] + [int(v) for v in np.cumsum(sizes)]
    cols = [w_in[:, c[i]:c[i + 1]] for i in range(len(sizes))]
    w_q, w_kc, w_vc, w_ks, w_vs, w_kw, w_vw, w_gate = cols

    def pad_last(a, n):
        return jnp.pad(a, [(0, 0)] * (a.ndim - 1) + [(0, n - a.shape[-1])])

    wq_t = pad_last(w_q.reshape(d, h, dk), NSA_DKP).reshape(d, h * NSA_DKP).T.astype(BF16)
    wk = jnp.concatenate([pad_last(w.reshape(d, g, dk), NSA_DKP).reshape(d, g * NSA_DKP)
                          for w in (w_ks, w_kw)], 1).astype(BF16)
    wv_t = jnp.concatenate([w_vs, w_vw], 1).T.astype(BF16)
    wg_t = pad_last(w_gate.reshape(d, h, N_BRANCH), NSA_GATE_ROWS).reshape(d, h * NSA_GATE_ROWS).T.astype(BF16)
    wc = jnp.concatenate([pad_last(w_kc.reshape(d, g, dk), NSA_DKP).reshape(d, g * NSA_DKP),
                          pad_last(w_vc.reshape(d, g, dv), NSA_DKP).reshape(d, g * NSA_DKP)], 1).astype(BF16)
    q_t, k, v_t, gate, ctok = _nsa_proj(x2, b, s, wq_t, wk, wv_t, wg_t, wc)

    nch = s // CMP_STRIDE
    ctok = ctok.reshape(b, g, 2, nch, CMP_STRIDE * NSA_DKP)
    k_cmp = _compress(ctok, 0, pe_k, w1_k, pad_last(w2_k, NSA_DKP), NSA_DKP)
    v_cmp = _compress(ctok, 1, pe_v, w1_v, w2_v, NSA_DKP)
    k_cmp = k_cmp.reshape(b, g, nch, NSA_DKP)
    vcmp_t = v_cmp.reshape(b, g, nch, dv).transpose(0, 1, 3, 2)

    tiles, band = _nsa_tables(rel_bias, s)
    ovt = _overlap_t(nch, nb)
    o = _nsa_attention(q_t, k_cmp, vcmp_t, k, v_t, gate, tiles, band, ovt)
    return o.reshape(t, h * dv)


def _out_ln_router_body(o_ref, w_ref, res_ref, lg_ref, lb_ref, wr_ref, x_ref, xb_ref, logit_ref):
    y = _layer_norm(ALPHA * res_ref[...] + _dot(o_ref[...], w_ref[...]), lg_ref[...], lb_ref[...])
    x_ref[...] = y
    xb_ref[...] = y.astype(BF16)
    x1, x2, _ = _split3(y)
    w1, w2, _ = _split3(wr_ref[...])
    logit_ref[...] = _dot(x1, w1) + (_dot(x1, w2) + _dot(x2, w1))


def _out_ln_router(o, w_out, res, lg, lb, w_router, *, tm=512):
    m, k = o.shape
    n = w_out.shape[1]
    ne = w_router.shape[1]
    wr = jnp.pad(w_router, ((0, 0), (0, 128 - ne)))
    row = lambda i: (i, 0)
    full = lambda i: (0, 0)
    x, xb, logits = pl.pallas_call(
        _out_ln_router_body,
        out_shape=(jax.ShapeDtypeStruct((m, n), F32), jax.ShapeDtypeStruct((m, n), BF16),
                   jax.ShapeDtypeStruct((m, 128), F32)),
        grid=(m // tm,),
        in_specs=[pl.BlockSpec((tm, k), row), pl.BlockSpec((k, n), full), pl.BlockSpec((tm, n), row),
                  pl.BlockSpec((1, n), full), pl.BlockSpec((1, n), full), pl.BlockSpec((n, 128), full)],
        out_specs=(pl.BlockSpec((tm, n), row), pl.BlockSpec((tm, n), row), pl.BlockSpec((tm, 128), row)),
        compiler_params=_cparams(("parallel",)),
        name="nsa_out_ln_router",
    )(o, w_out, res, lg.reshape(1, n), lb.reshape(1, n), wr)
    return x, xb, logits[:, :ne]


def _dispatch_body(ir_ref, ic_ref, fl_ref, x_ref, rt_ref, o_ref):
    i = pl.program_id(0)
    flag = fl_ref[i]
    tok = ic_ref[i] * MOE_TC + lax.broadcasted_iota(jnp.int32, (MOE_BLK, MOE_TC), 1)
    onehot = jnp.where(rt_ref[...] == tok, 1.0, 0.0).astype(BF16)
    rows = _dot(onehot, x_ref[...])

    @pl.when(flag == 3)
    def _():
        o_ref[...] = rows.astype(o_ref.dtype)

    @pl.when(flag == 1)
    def _():
        o_ref[...] = (o_ref[...].astype(F32) + rows).astype(o_ref.dtype)


def _dispatch(x_bf, row_tok_col, item_r, item_c, flags):
    t, d = x_bf.shape
    r = row_tok_col.shape[0]
    ni = item_r.shape[0]
    gs = pltpu.PrefetchScalarGridSpec(
        num_scalar_prefetch=3, grid=(ni,),
        in_specs=[pl.BlockSpec((MOE_TC, d), lambda i, ir, ic, fl: (ic[i], 0)),
                  pl.BlockSpec((MOE_BLK, 1), lambda i, ir, ic, fl: (ir[i], 0))],
        out_specs=pl.BlockSpec((MOE_BLK, d), lambda i, ir, ic, fl: (ir[i], 0)))
    return pl.pallas_call(
        _dispatch_body, grid_spec=gs,
        out_shape=jax.ShapeDtypeStruct((r, d), BF16),
        compiler_params=_cparams(("arbitrary",)),
        name="moe_dispatch",
    )(item_r, item_c, flags, x_bf, row_tok_col)


def _expert_body(be_ref, x_ref, wg_ref, wu_ref, wd_ref, rw_ref, o_ref, acc_scr):
    f = pl.program_id(1)
    xb = x_ref[...]
    h = jax.nn.silu(_dot(xb, wg_ref[0])) * _dot(xb, wu_ref[0])
    part = _dot(h.astype(BF16), wd_ref[0])

    @pl.when(f == 0)
    def _():
        acc_scr[...] = part

    @pl.when(f > 0)
    def _():
        acc_scr[...] += part

    @pl.when(f == pl.num_programs(1) - 1)
    def _():
        o_ref[...] = (acc_scr[...] * rw_ref[...]).astype(o_ref.dtype)


def _experts(xs, wg, wu, wd, row_w_col, block_expert):
    r, d = xs.shape
    nbk = r // MOE_BLK
    nf = wg.shape[2] // MOE_TF
    gs = pltpu.PrefetchScalarGridSpec(
        num_scalar_prefetch=1, grid=(nbk, nf),
        in_specs=[pl.BlockSpec((MOE_BLK, d), lambda i, f, be: (i, 0)),
                  pl.BlockSpec((1, d, MOE_TF), lambda i, f, be: (be[i], 0, f)),
                  pl.BlockSpec((1, d, MOE_TF), lambda i, f, be: (be[i], 0, f)),
                  pl.BlockSpec((1, MOE_TF, d), lambda i, f, be: (be[i], f, 0)),
                  pl.BlockSpec((MOE_BLK, 1), lambda i, f, be: (i, 0))],
        out_specs=pl.BlockSpec((MOE_BLK, d), lambda i, f, be: (i, 0)),
        scratch_shapes=[pltpu.VMEM((MOE_BLK, d), F32)])
    return pl.pallas_call(
        _expert_body, grid_spec=gs,
        out_shape=jax.ShapeDtypeStruct((r, d), BF16),
        compiler_params=_cparams(("parallel", "arbitrary")),
        name="moe_experts",
    )(block_expert, xs, wg, wu, wd, row_w_col)


def _combine_body(ir_ref, ic_ref, fl_ref, y_ref, rt_ref, x_ref, g_ref, b_ref, o_ref):
    i = pl.program_id(0)
    flag = fl_ref[i]
    tok = ic_ref[i] * MOE_TC + lax.broadcasted_iota(jnp.int32, (MOE_TC, MOE_BLK), 0)
    onehot = jnp.where(rt_ref[0] == tok, 1.0, 0.0).astype(BF16)
    part = _dot(onehot, y_ref[...])

    @pl.when((flag & 3) == 3)
    def _():
        o_ref[...] = part

    @pl.when((flag & 3) == 1)
    def _():
        o_ref[...] += part

    @pl.when((flag & 4) == 4)
    def _():
        o_ref[...] = _layer_norm(ALPHA * x_ref[...] + o_ref[...], g_ref[...], b_ref[...])


def _combine_ln(out_rows, row_tok_lane, item_r, item_c, flags, x, g, b):
    r, d = out_rows.shape
    t = x.shape[0]
    ni = item_r.shape[0]
    vec = pl.BlockSpec((1, d), lambda i, ir, ic, fl: (0, 0))
    gs = pltpu.PrefetchScalarGridSpec(
        num_scalar_prefetch=3, grid=(ni,),
        in_specs=[pl.BlockSpec((MOE_BLK, d), lambda i, ir, ic, fl: (ir[i], 0)),
                  pl.BlockSpec((1, 1, MOE_BLK), lambda i, ir, ic, fl: (ir[i], 0, 0)),
                  pl.BlockSpec((MOE_TC, d), lambda i, ir, ic, fl: (ic[i], 0)),
                  vec, vec],
        out_specs=pl.BlockSpec((MOE_TC, d), lambda i, ir, ic, fl: (ic[i], 0)))
    return pl.pallas_call(
        _combine_body, grid_spec=gs,
        out_shape=jax.ShapeDtypeStruct((t, d), F32),
        compiler_params=_cparams(("arbitrary",)),
        name="moe_combine_ln",
    )(item_r, item_c, flags, out_rows, row_tok_lane, x, g.reshape(1, d), b.reshape(1, d))


def _moe_plan(top_idx, wts, t):
    e, blk, tc = N_EXPERTS, MOE_BLK, MOE_TC
    a = t * TOP_K
    i32 = jnp.int32
    exp_flat = top_idx.reshape(a).astype(i32)
    tok_flat = jnp.arange(a, dtype=i32) // TOP_K
    _, tok_sorted, w_sorted = lax.sort((exp_flat, tok_flat, wts.reshape(a)), num_keys=1, is_stable=True)
    counts = jnp.sum((exp_flat[:, None] == jnp.arange(e, dtype=i32)[None, :]).astype(i32), 0)
    padded = ((counts + blk - 1) // blk) * blk
    grp_start = jnp.cumsum(counts) - counts
    pad_end = jnp.cumsum(padded)
    pad_start = pad_end - padded
    nbk = a // blk + e
    r = nbk * blk
    tok_ext = jnp.concatenate([tok_sorted, jnp.full((r - a,), -1, i32)])
    w_ext = jnp.concatenate([w_sorted, jnp.zeros((r - a,), F32)])
    rows = jnp.arange(r, dtype=i32)
    row_tok = jnp.full((r,), -1, i32)
    row_w = jnp.zeros((r,), F32)
    for ee in range(e):
        inside = (rows >= pad_start[ee]) & (rows < pad_start[ee] + counts[ee])
        shift = pad_start[ee] - grp_start[ee]
        row_tok = jnp.where(inside, jnp.roll(tok_ext, shift), row_tok)
        row_w = jnp.where(inside, jnp.roll(w_ext, shift), row_w)
    blk_first = jnp.arange(nbk, dtype=i32) * blk
    block_expert = jnp.minimum(jnp.sum((pad_end[None, :] <= blk_first[:, None]).astype(i32), 1), e - 1)

    rt = row_tok.reshape(nbk, blk)
    valid = rt >= 0
    t_lo = jnp.min(jnp.where(valid, rt, t), 1)
    t_hi = jnp.max(rt, 1)
    has = t_hi >= 0
    c_lo = jnp.where(has, t_lo // tc, 0)
    c_hi = jnp.where(has, t_hi // tc, 0)
    n_it = c_hi - c_lo + 1
    off_end = jnp.cumsum(n_it)
    off_start = off_end - n_it
    total = off_end[-1]
    ni = nbk + e * (t // tc)
    idx = jnp.arange(ni, dtype=i32)
    ok = idx < total
    ir = jnp.minimum(jnp.sum((off_end[None, :] <= idx[:, None]).astype(i32), 1), nbk - 1)
    ic = jnp.where(ok, c_lo[ir] + idx - off_start[ir], c_hi[nbk - 1]).astype(i32)
    first = ok & (idx == off_start[ir])
    d_flags = ok.astype(i32) + 2 * first.astype(i32)

    key = jnp.where(ok, ic * nbk + ir, jnp.iinfo(jnp.int32).max)
    perm = jnp.argsort(key)
    ok2 = ok[perm]
    last = total - 1
    cr = jnp.where(ok2, ir[perm], ir[perm][last]).astype(i32)
    cc = jnp.where(ok2, ic[perm], ic[perm][last]).astype(i32)
    first2 = ok2 & jnp.concatenate([jnp.ones((1,), bool), cc[1:] != cc[:-1]])
    last2 = ok2 & jnp.concatenate([(cc[1:] != cc[:-1]) | ~ok2[1:], jnp.ones((1,), bool)])
    c_flags = ok2.astype(i32) + 2 * first2.astype(i32) + 4 * last2.astype(i32)
    return dict(row_tok=row_tok, row_w=row_w, block_expert=block_expert,
                d_items=(ir, ic, d_flags), c_items=(cr, cc, c_flags), nbk=nbk)


def _moe_ln(x2, x_bf, logits, wg, wu, wd, ln_g, ln_b):
    t, d = x2.shape
    top_val, top_idx = lax.top_k(logits, TOP_K)
    wts = jax.nn.softmax(top_val, -1)
    plan = _moe_plan(top_idx, wts, t)
    nbk = plan["nbk"]
    xs = _dispatch(x_bf, plan["row_tok"].reshape(-1, 1), *plan["d_items"])
    out_rows = _experts(xs, wg.astype(BF16), wu.astype(BF16), wd.astype(BF16),
                        plan["row_w"].reshape(-1, 1), plan["block_expert"])
    return _combine_ln(out_rows, plan["row_tok"].reshape(nbk, 1, MOE_BLK), *plan["c_items"], x2, ln_g, ln_b)


def _forward(x, mla_w_in, mla_q_norm, mla_w_q_up, mla_kv_norm, mla_w_kv_up, mla_w_out, nsa_w_in,
             nsa_cmp_pe_k, nsa_cmp_w1_k, nsa_cmp_w2_k, nsa_cmp_pe_v, nsa_cmp_w1_v, nsa_cmp_w2_v,
             nsa_w_out, rel_bias, ffn_w_gate, ffn_w_up, ffn_w_down, moe_w_router, moe_w_gate,
             moe_w_up, moe_w_down, ln_mix_g, ln_mix_b, ln_ffn_g, ln_ffn_b):
    b, s, d = x.shape
    x2 = x.reshape(b * s, d)
    o = _mla_mixer(x2, b, s, mla_w_in[0], mla_q_norm[0], mla_w_q_up[0], mla_kv_norm[0], mla_w_kv_up[0])
    x2 = _linear(o, mla_w_out[0].astype(BF16), tm=1024, tn=d, out_dtype=F32,
                 ln=(x2, ln_mix_g[0], ln_mix_b[0]), name="mla_out_ln")
    x2 = _ffn_ln(x2, ffn_w_gate[0].astype(BF16), ffn_w_up[0].astype(BF16), ffn_w_down[0].astype(BF16),
                 ln_ffn_g[0], ln_ffn_b[0])
    o = _nsa_mixer(x2, b, s, nsa_w_in[0], nsa_cmp_pe_k[0], nsa_cmp_w1_k[0], nsa_cmp_w2_k[0],
                   nsa_cmp_pe_v[0], nsa_cmp_w1_v[0], nsa_cmp_w2_v[0], rel_bias)
    x2, x_bf, logits = _out_ln_router(o, nsa_w_out[0].astype(BF16), x2, ln_mix_g[1], ln_mix_b[1],
                                      moe_w_router[0])
    x2 = _moe_ln(x2, x_bf, logits, moe_w_gate[0], moe_w_up[0], moe_w_down[0], ln_ffn_g[1], ln_ffn_b[1])
    return x2.reshape(b, s, d)


@jax.jit
def kernel(x, mla_w_in, mla_q_norm, mla_w_q_up, mla_kv_norm, mla_w_kv_up, mla_w_out, nsa_w_in,
           nsa_cmp_pe_k, nsa_cmp_w1_k, nsa_cmp_w2_k, nsa_cmp_pe_v, nsa_cmp_w1_v, nsa_cmp_w2_v,
           nsa_w_out, rel_bias, ffn_w_gate, ffn_w_up, ffn_w_down, moe_w_router, moe_w_gate,
           moe_w_up, moe_w_down, ln_mix_g, ln_mix_b, ln_ffn_g, ln_ffn_b):
    return _forward(x, mla_w_in, mla_q_norm, mla_w_q_up, mla_kv_norm, mla_w_kv_up, mla_w_out, nsa_w_in,
                    nsa_cmp_pe_k, nsa_cmp_w1_k, nsa_cmp_w2_k, nsa_cmp_pe_v, nsa_cmp_w1_v, nsa_cmp_w2_v,
                    nsa_w_out, rel_bias, ffn_w_gate, ffn_w_up, ffn_w_down, moe_w_router, moe_w_gate,
                    moe_w_up, moe_w_down, ln_mix_g, ln_mix_b, ln_ffn_g, ln_ffn_b)
```

```python
import functools
import math

import numpy as np
import jax
import jax.numpy as jnp
from jax import lax
from jax.experimental import pallas as pl
from jax.experimental.pallas import tpu as pltpu

F32 = jnp.float32
BF16 = jnp.bfloat16

D_MODEL = 1024
DEPTH = 2

MLA_HEADS = 8
MLA_Q_RANK = 512
MLA_KV_RANK = 256
MLA_NOPE = 128
MLA_ROPE = 64
MLA_V = 128
ROPE_THETA = 10000.0

NSA_HEADS = 16
NSA_GROUPS = 4
NSA_HG = NSA_HEADS // NSA_GROUPS
NSA_DK = 96
NSA_DV = 64
CMP_LEN = 32
CMP_STRIDE = 16
SEL_BLOCK = 64
SEL_TOPN = 16
WINDOW = 512
N_BRANCH = 3
FORCE = 1e6

REL_BUCKETS = 32
REL_MAX_DIST = 128

D_FF = 2816
N_EXPERTS = 8
TOP_K = 2
D_FF_EXPERT = 3584

LN_EPS = 1e-5
RMS_EPS = 1e-6

ALPHA = (2.0 * DEPTH) ** 0.25

NEG = -1e30
NEG_TEST = -1e29

V7X_VMEM_LIMIT = 56 * 1024 * 1024

LOG2E = 1.4426950408889634

NSA_TQ = 256
NSA_LANES = NSA_HG * NSA_TQ
NSA_KC = 256
NSA_DKP = 128
NSA_GATE_ROWS = 8
NSA_BAND = 24
NSA_SEL_ROWS = 16
NSA_VROWS = NSA_DV + 16
NSA_HEAD_SLOTS = 4
NSA_FAR_UNROLL = 4
NSA_TILES_PER_VARIANT = 4

MOE_BLK = 512
MOE_TC = 512
MOE_TF = 1792


def _cparams(sem, vmem=V7X_VMEM_LIMIT):
    return pltpu.CompilerParams(dimension_semantics=sem, vmem_limit_bytes=vmem)


def _layer_norm(r, g, b):
    mu = jnp.mean(r, -1, keepdims=True)
    d = r - mu
    var = jnp.mean(d * d, -1, keepdims=True)
    return d * lax.rsqrt(var + LN_EPS) * g + b


def _rms_norm(x, g):
    return x * lax.rsqrt(jnp.mean(x * x, -1, keepdims=True) + RMS_EPS) * g


def _split3(a):
    a1 = a.astype(BF16)
    r1 = a - a1.astype(F32)
    a2 = r1.astype(BF16)
    a3 = (r1 - a2.astype(F32)).astype(BF16)
    return a1, a2, a3


def _dot(a, b):
    return jnp.dot(a, b, preferred_element_type=F32)


def _dot_nt(a, b):
    return lax.dot_general(a, b, (((1,), (1,)), ((), ())), preferred_element_type=F32)


def _linear_body(*refs, has_ln):
    it = iter(refs)
    x_ref = next(it)
    w_ref = next(it)
    if has_ln:
        res_ref, lg_ref, lb_ref = next(it), next(it), next(it)
    o_ref = next(it)
    acc = _dot(x_ref[...].astype(BF16), w_ref[...])
    if has_ln:
        acc = _layer_norm(ALPHA * res_ref[...] + acc, lg_ref[...], lb_ref[...])
    o_ref[...] = acc.astype(o_ref.dtype)


def _linear(x, w, *, tm, tn, out_dtype, ln=None, name):
    m, k = x.shape
    n = w.shape[1]
    assert m % tm == 0 and n % tn == 0
    in_specs = [pl.BlockSpec((tm, k), lambda i, j: (i, 0)),
                pl.BlockSpec((k, tn), lambda i, j: (0, j))]
    args = [x, w]
    if ln is not None:
        assert tn == n
        res, lg, lb = ln
        in_specs += [pl.BlockSpec((tm, n), lambda i, j: (i, 0)),
                     pl.BlockSpec((1, n), lambda i, j: (0, 0)),
                     pl.BlockSpec((1, n), lambda i, j: (0, 0))]
        args += [res, lg.reshape(1, n), lb.reshape(1, n)]
    return pl.pallas_call(
        functools.partial(_linear_body, has_ln=ln is not None),
        out_shape=jax.ShapeDtypeStruct((m, n), out_dtype),
        grid=(m // tm, n // tn),
        in_specs=in_specs,
        out_specs=pl.BlockSpec((tm, tn), lambda i, j: (i, j)),
        compiler_params=_cparams(("parallel", "arbitrary")),
        name=name,
    )(*args)


def _rope_tables(s):
    half = MLA_ROPE // 2
    freq = ROPE_THETA ** (-jnp.arange(half, dtype=F32) / half)
    ang = jnp.arange(s).astype(F32)[:, None] * freq[None, :]
    cos, sin = jnp.cos(ang), jnp.sin(ang)
    return jnp.concatenate([cos, cos], -1), jnp.concatenate([-sin, sin], -1)


MLA_DQ = MLA_NOPE + MLA_ROPE
MLA_QROWS = MLA_NOPE + 2 * MLA_ROPE
MLA_T = 512
MLA_HPS = 2
MLA_SUB = 256
MLA_FAR_UNROLL = 4


def _mla_q_body(lat_ref, g_ref, w_ref, cos_ref, sin_ref, o_ref, *, qscale):
    xn = _rms_norm(lat_ref[...], g_ref[...]).astype(BF16)
    y = _dot_nt(w_ref[...], xn)
    cos, sin = cos_ref[...], sin_ref[...]
    for h in range(MLA_HEADS):
        r0 = h * MLA_QROWS
        o_ref[0, h, :MLA_NOPE, :] = (y[r0:r0 + MLA_NOPE] * qscale).astype(BF16)
        a = y[r0 + MLA_NOPE:r0 + MLA_DQ]
        bb = y[r0 + MLA_DQ:r0 + MLA_QROWS]
        o_ref[0, h, MLA_NOPE:, :] = ((a * cos + bb * sin) * qscale).astype(BF16)


def _mla_q_proj(lat, gain, w_t, cos_t, sin_t, b, s, *, tm=MLA_T):
    ns = s // tm
    body = functools.partial(_mla_q_body, qscale=(MLA_DQ ** -0.5) * LOG2E)
    return pl.pallas_call(
        body,
        out_shape=jax.ShapeDtypeStruct((b, MLA_HEADS, MLA_DQ, s), BF16),
        grid=(b * ns,),
        in_specs=[pl.BlockSpec((tm, MLA_Q_RANK), lambda i: (i, 0)),
                  pl.BlockSpec((1, MLA_Q_RANK), lambda i: (0, 0)),
                  pl.BlockSpec(w_t.shape, lambda i: (0, 0)),
                  pl.BlockSpec((MLA_ROPE, tm), lambda i: (0, i % ns)),
                  pl.BlockSpec((MLA_ROPE, tm), lambda i: (0, i % ns))],
        out_specs=pl.BlockSpec((1, MLA_HEADS, MLA_DQ, tm), lambda i: (i // ns, 0, 0, i % ns)),
        compiler_params=_cparams(("parallel",)),
        name="mla_q_proj",
    )(lat, gain.reshape(1, -1), w_t, cos_t, sin_t)


def _mla_kv_body(lat_ref, g_ref, wk_ref, wvt_ref, kr_ref, cos_ref, sin_ref, k_ref, vt_ref):
    xn = _rms_norm(lat_ref[...], g_ref[...]).astype(BF16)
    kn = _dot(xn, wk_ref[...]).astype(BF16)
    vt = _dot_nt(wvt_ref[...], xn).astype(BF16)
    kr = kr_ref[...]
    rot = (kr[:, :MLA_ROPE] * cos_ref[...] + kr[:, MLA_ROPE:] * sin_ref[...]).astype(BF16)
    for h in range(MLA_HEADS):
        k_ref[0, h, :, :MLA_NOPE] = kn[:, h * MLA_NOPE:(h + 1) * MLA_NOPE]
        k_ref[0, h, :, MLA_NOPE:] = rot
        for kk in range(MLA_T // MLA_SUB):
            vt_ref[0, h, kk] = vt[h * MLA_V:(h + 1) * MLA_V, kk * MLA_SUB:(kk + 1) * MLA_SUB]


def _mla_kv_proj(lat, gain, wk, wv_t, cosx, sinx, b, s, *, tm=MLA_T):
    ns = s // tm
    nsub = tm // MLA_SUB
    return pl.pallas_call(
        _mla_kv_body,
        out_shape=(jax.ShapeDtypeStruct((b, MLA_HEADS, s, MLA_DQ), BF16),
                   jax.ShapeDtypeStruct((b, MLA_HEADS, ns * nsub, MLA_V, MLA_SUB), BF16)),
        grid=(b * ns,),
        in_specs=[pl.BlockSpec((tm, MLA_KV_RANK), lambda i: (i, MLA_Q_RANK // MLA_KV_RANK)),
                  pl.BlockSpec((1, MLA_KV_RANK), lambda i: (0, 0)),
                  pl.BlockSpec(wk.shape, lambda i: (0, 0)),
                  pl.BlockSpec(wv_t.shape, lambda i: (0, 0)),
                  pl.BlockSpec((tm, 2 * MLA_ROPE),
                               lambda i: (i, (MLA_Q_RANK + MLA_KV_RANK) // (2 * MLA_ROPE))),
                  pl.BlockSpec((tm, MLA_ROPE), lambda i: (i % ns, 0)),
                  pl.BlockSpec((tm, MLA_ROPE), lambda i: (i % ns, 0))],
        out_specs=(pl.BlockSpec((1, MLA_HEADS, tm, MLA_DQ), lambda i: (i // ns, 0, i % ns, 0)),
                   pl.BlockSpec((1, MLA_HEADS, nsub, MLA_V, MLA_SUB), lambda i: (i // ns, 0, i % ns, 0, 0))),
        compiler_params=_cparams(("parallel",)),
        name="mla_kv_proj",
    )(lat, gain.reshape(1, -1), wk, wv_t, lat, cosx, sinx)


def _mla_attn_body(qt_ref, k_ref, vt_ref, o_ref, *scr):
    i = pl.program_id(2)
    sub = MLA_SUB
    nsub = MLA_T // sub
    assert nsub == 2
    chains_all = [(h, ql) for ql in range(nsub) for h in range(MLA_HPS)]
    nch = len(chains_all)
    ring_a, ring_b, scr = scr[:nch], scr[nch:2 * nch], scr[2 * nch:]
    state = {(h, ql): scr[3 * (nsub * h + ql):3 * (nsub * h + ql) + 3]
             for h in range(MLA_HPS) for ql in range(nsub)}
    for m_scr, l_scr, acc_scr in state.values():
        m_scr[...] = jnp.full_like(m_scr, NEG)
        l_scr[...] = jnp.zeros_like(l_scr)
        acc_scr[...] = jnp.zeros_like(acc_scr)

    def issue(sk, ring, chains):
        for idx, (h, ql) in enumerate(chains_all):
            if (h, ql) in chains:
                k = k_ref[0, h, pl.ds(pl.multiple_of(sk * sub, sub), sub), :]
                ring[idx][...] = _dot(k, qt_ref[0, h, :, ql * sub:(ql + 1) * sub])
            yield

    def consume(sk, ring, chains, diag_ql=None):
        for idx, (h, ql) in enumerate(chains_all):
            if (h, ql) in chains:
                m_scr, l_scr, acc_scr = state[(h, ql)]
                s = ring[idx][...]
                if ql == diag_ql:
                    key = lax.broadcasted_iota(jnp.int32, s.shape, 0)
                    qry = lax.broadcasted_iota(jnp.int32, s.shape, 1)
                    s = jnp.where(key <= qry, s, NEG)
                m_old = m_scr[...]
                m_new = jnp.maximum(m_old, jnp.max(s, 0, keepdims=True))
                a = jnp.exp2(m_old - m_new)
                p = jnp.exp2(s - m_new)
                l_scr[...] = a * l_scr[...] + jnp.sum(p, 0, keepdims=True)
                acc_scr[...] = a * acc_scr[...] + _dot(vt_ref[0, h, sk], p.astype(BF16))
                m_scr[...] = m_new
            yield

    def interleave(*gens):
        for _ in zip(*gens):
            pass

    interleave(issue(0, ring_a, chains_all))

    def pair(c0):
        interleave(issue(c0 + 1, ring_b, chains_all), consume(c0, ring_a, chains_all))
        interleave(issue(c0 + 2, ring_a, chains_all), consume(c0 + 1, ring_b, chains_all))

    def trip(j, carry):
        for pr in range(MLA_FAR_UNROLL):
            pair(2 * (MLA_FAR_UNROLL * j + pr))
        return carry

    lax.fori_loop(0, i // MLA_FAR_UNROLL, trip, 0)
    lax.fori_loop(i // MLA_FAR_UNROLL * MLA_FAR_UNROLL, i, lambda j, carry: (pair(2 * j), carry)[1], 0)
    upper = [c for c in chains_all if c[1] == 1]
    interleave(issue(2 * i + 1, ring_b, upper), consume(2 * i, ring_a, chains_all, diag_ql=0))
    interleave(consume(2 * i + 1, ring_b, upper, diag_ql=1))
    for (h, ql), (m_scr, l_scr, acc_scr) in state.items():
        o_t = acc_scr[...] * (1.0 / jnp.maximum(l_scr[...], 1e-30))
        o_ref[0, ql * sub:(ql + 1) * sub, h * MLA_V:(h + 1) * MLA_V] = o_t.T.astype(o_ref.dtype)


def _mla_attention(q_t, k, v_t, b, s):
    tq = MLA_T
    nq = s // tq
    hp = MLA_HPS
    nsub = tq // MLA_SUB
    sub_state = [pltpu.VMEM((1, MLA_SUB), F32), pltpu.VMEM((1, MLA_SUB), F32), pltpu.VMEM((MLA_V, MLA_SUB), F32)]
    ring = hp * nsub * [pltpu.VMEM((MLA_SUB, MLA_SUB), F32)]
    return pl.pallas_call(
        _mla_attn_body,
        out_shape=jax.ShapeDtypeStruct((b, s, MLA_HEADS * MLA_V), BF16),
        grid=(b, MLA_HEADS // hp, nq),
        in_specs=[pl.BlockSpec((1, hp, MLA_DQ, tq), lambda bb, h, i: (bb, h, 0, i)),
                  pl.BlockSpec((1, hp, s, MLA_DQ), lambda bb, h, i: (bb, h, 0, 0)),
                  pl.BlockSpec((1, hp, nq * nsub, MLA_V, MLA_SUB), lambda bb, h, i: (bb, h, 0, 0, 0))],
        out_specs=pl.BlockSpec((1, tq, hp * MLA_V), lambda bb, h, i: (bb, i, h)),
        scratch_shapes=2 * ring + hp * nsub * sub_state,
        compiler_params=_cparams(("parallel", "parallel", "arbitrary")),
        name="mla_attention",
    )(q_t, k, v_t)


def _mla_mixer(x2, b, s, w_in, q_norm, w_q_up, kv_norm, w_kv_up):
    r0 = MLA_Q_RANK + MLA_KV_RANK
    half = MLA_ROPE // 2
    w_in_ext = jnp.concatenate([w_in, w_in[:, r0 + half:r0 + MLA_ROPE], w_in[:, r0:r0 + half]], 1)
    lat = _linear(x2, w_in_ext.astype(BF16), tm=512, tn=w_in_ext.shape[1], out_dtype=F32, name="mla_in")
    wq = w_q_up.reshape(MLA_Q_RANK, MLA_HEADS, MLA_DQ)
    wr = wq[..., MLA_NOPE:]
    wq = jnp.concatenate([wq, wr[..., half:], wr[..., :half]], -1)
    wq_t = wq.reshape(MLA_Q_RANK, MLA_HEADS * MLA_QROWS).T.astype(BF16)
    wkv = w_kv_up.reshape(MLA_KV_RANK, MLA_HEADS, MLA_NOPE + MLA_V)
    wk = wkv[..., :MLA_NOPE].reshape(MLA_KV_RANK, MLA_HEADS * MLA_NOPE).astype(BF16)
    wv_t = wkv[..., MLA_NOPE:].reshape(MLA_KV_RANK, MLA_HEADS * MLA_V).T.astype(BF16)
    cosx, sinx = _rope_tables(s)
    q_t = _mla_q_proj(lat, q_norm, wq_t, cosx.T, sinx.T, b, s)
    k, v_t = _mla_kv_proj(lat, kv_norm, wk, wv_t, cosx, sinx, b, s)
    o = _mla_attention(q_t, k, v_t, b, s)
    return o.reshape(b * s, MLA_HEADS * MLA_V)


def _swiglu_tile(xb, wg, wu, wd, width):
    mid = (width // 2 + 127) // 128 * 128
    splits = [(0, mid), (mid, width)]
    gu = [(_dot(xb, wg(a, b)), _dot(xb, wu(a, b))) for a, b in splits]
    out = None
    for (g, u), (a, b) in zip(gu, splits):
        part = _dot((jax.nn.silu(g) * u).astype(BF16), wd(a, b))
        out = part if out is None else out + part
    return out


def _ffn_body(x_ref, wg_ref, wu_ref, wd_ref, lg_ref, lb_ref, o_ref, acc_scr):
    f = pl.program_id(1)
    xb = x_ref[...].astype(BF16)
    part = _swiglu_tile(xb, lambda a, b: wg_ref[:, a:b], lambda a, b: wu_ref[:, a:b],
                        lambda a, b: wd_ref[a:b, :], wg_ref.shape[1])

    @pl.when(f == 0)
    def _():
        acc_scr[...] = part

    @pl.when(f > 0)
    def _():
        acc_scr[...] += part

    @pl.when(f == pl.num_programs(1) - 1)
    def _():
        o_ref[...] = _layer_norm(ALPHA * x_ref[...] + acc_scr[...], lg_ref[...], lb_ref[...])


def _ffn_ln(x, wg, wu, wd, lg, lb, *, tm=512, tf=1408):
    m, d = x.shape
    dff = wg.shape[1]
    assert dff % tf == 0
    vec = pl.BlockSpec((1, d), lambda i, f: (0, 0))
    return pl.pallas_call(
        _ffn_body,
        out_shape=jax.ShapeDtypeStruct((m, d), F32),
        grid=(m // tm, dff // tf),
        in_specs=[pl.BlockSpec((tm, d), lambda i, f: (i, 0)),
                  pl.BlockSpec((d, tf), lambda i, f: (0, f)),
                  pl.BlockSpec((d, tf), lambda i, f: (0, f)),
                  pl.BlockSpec((tf, d), lambda i, f: (f, 0)),
                  vec, vec],
        out_specs=pl.BlockSpec((tm, d), lambda i, f: (i, 0)),
        scratch_shapes=[pltpu.VMEM((tm, d), F32)],
        compiler_params=_cparams(("parallel", "arbitrary")),
        name="ffn_ln",
    )(x, wg, wu, wd, lg.reshape(1, d), lb.reshape(1, d))


def _compress_body(c_ref, pea_ref, peb_ref, w1a_ref, w1b_ref, w2_ref, o_ref):
    c = c_ref[0, 0, 0].astype(F32)
    ya = _dot((c + pea_ref[...]).astype(BF16), w1a_ref[...])
    yb = _dot((c + peb_ref[...]).astype(BF16), w1b_ref[...])
    yb_next = jnp.concatenate([yb[1:], jnp.zeros_like(yb[:1])], 0)
    h = jax.nn.gelu(ya + yb_next)
    o_ref[0] = _dot(h.astype(BF16), w2_ref[...]).astype(o_ref.dtype)


def _compress(chunks, kv, pe, w1, w2, dpad):
    b, g, _, nch, k = chunks.shape
    bg = b * g
    d = pe.shape[1]
    dh, dout = w2.shape
    half = CMP_LEN // 2
    assert half == CMP_STRIDE and k == CMP_STRIDE * dpad

    def pad_tok(a):
        a = jnp.pad(a, [(0, 0), (0, dpad - d)] + [(0, 0)] * (a.ndim - 2))
        return a[:half].reshape((k,) + a.shape[2:]), a[half:].reshape((k,) + a.shape[2:])

    pe_a, pe_b = pad_tok(pe)
    w1a, w1b = pad_tok(w1)
    full = lambda i: (0, 0)
    return pl.pallas_call(
        _compress_body,
        out_shape=jax.ShapeDtypeStruct((bg, nch, dout), BF16),
        grid=(bg,),
        in_specs=[pl.BlockSpec((1, 1, 1, nch, k), lambda i: (i // g, i % g, kv, 0, 0)),
                  pl.BlockSpec((1, k), full), pl.BlockSpec((1, k), full),
                  pl.BlockSpec((k, dh), full), pl.BlockSpec((k, dh), full),
                  pl.BlockSpec((dh, dout), full)],
        out_specs=pl.BlockSpec((1, nch, dout), lambda i: (i, 0, 0)),
        compiler_params=_cparams(("parallel",)),
        name="nsa_compress",
    )(chunks, pe_a.reshape(1, k), pe_b.reshape(1, k), w1a.astype(BF16), w1b.astype(BF16), w2.astype(BF16))


def _rel_bucket_np(dist):
    n = np.maximum(dist, 0)
    max_exact = REL_BUCKETS // 2
    nf = np.maximum(n, 1).astype(np.float32)
    large = max_exact + (np.log(nf / np.float32(max_exact)) / np.float32(math.log(REL_MAX_DIST / max_exact))
                         * np.float32(REL_BUCKETS - max_exact)).astype(np.int32)
    large = np.minimum(large, REL_BUCKETS - 1)
    return np.where(n < max_exact, n, large).astype(np.int32)


def _nsa_tables(rel_bias, s):
    g, hg, tq, kc = NSA_GROUPS, NSA_HG, NSA_TQ, NSA_KC
    assert tq == kc and tq % CMP_STRIDE == 0 and WINDOW == 2 * kc
    assert np.all(_rel_bucket_np(np.arange(tq // 2 - 15, s + tq)) == REL_BUCKETS - 1)
    rb = rel_bias.reshape(REL_BUCKETS, g, hg) * LOG2E

    def tile(base, step, rows, valid):
        p = tq + step * rows
        k = np.arange(p)
        k = np.where(k < p - step * (rows - 1), k, k - p)
        d = base + k
        vec = jnp.where(valid(d)[:, None, None], rb[_rel_bucket_np(d)], NEG)
        vec = vec.transpose(1, 2, 0)
        flat = jnp.tile(vec, (1, 1, rows))[..., :rows * (p - step)]
        mat = flat.reshape(g, hg, rows, p - step)[..., :tq]
        return mat.transpose(0, 2, 1, 3).reshape(g, rows, hg * tq)

    causal = lambda d: d >= 0
    far = jnp.broadcast_to(rb[REL_BUCKETS - 1][:, None, :, None], (g, 1, hg, tq)).reshape(g, 1, hg * tq)
    rel = lambda x: jnp.where(x > NEG_TEST, x - far, NEG)
    tiles = jnp.stack([rel(tile(0, 1, kc, causal)), rel(tile(tq, 1, kc, causal)),
                       rel(tile(2 * tq, 1, kc, lambda d: d < WINDOW)),
                       jnp.zeros((g, kc, hg * tq), F32)], 1)
    band = jnp.stack([rel(tile(8 * CMP_STRIDE - CMP_LEN + 1, CMP_STRIDE, NSA_BAND, causal)),
                      rel(tile(-(CMP_LEN - 1), CMP_STRIDE, NSA_BAND, causal))], 1)
    return tiles, band


def _overlap_t(nc_pad, nb):
    n = np.arange(nc_pad)[None, :]
    jb = np.arange(nb)[:, None]
    cstart = n * CMP_STRIDE
    cend = cstart + CMP_LEN - 1
    sstart = jb * SEL_BLOCK
    ov = (cstart <= sstart + SEL_BLOCK - 1) & (cend >= sstart) & (n < nc_pad - 1)
    return jnp.asarray(ov.astype(np.float32), BF16)


def _nsa_body(qt_ref, kc_ref, vct_ref, k_ref, vt_ref, gate_ref, tiles_ref,
              band_ref, ovt_ref, o_ref, s_scr, sel_scr, qa_scr, oc_scr, *scr, nb, nqt):
    t = pl.program_id(2)
    ring_a, ring_b, state = scr[:NSA_HG], scr[NSA_HG:2 * NSA_HG], scr[2 * NSA_HG:]
    nkc = k_ref.shape[3]
    L = NSA_LANES
    q_t = qt_ref[0, 0, 0]
    ncp = kc_ref.shape[2]
    per = NSA_KC // SEL_BLOCK
    blocks_per_tile = NSA_TQ // CMP_STRIDE

    n_slab = nb // NSA_SEL_ROWS

    n_win = jnp.minimum(t, 2) + 1
    n_slots = n_win + t + 1

    def slot_params(c):
        is_win = c < n_win
        j = c - n_win
        is_sel = jnp.logical_and(c >= n_win, j <= t)
        delta = t - j
        br = is_win.astype(jnp.int32)
        kidx = jnp.where(is_win, t - c, jnp.where(is_sel, j, 0))
        sidx = jnp.where(is_sel, j // (NSA_SEL_ROWS // per), jnp.where(is_win, 0, n_slab))
        tidx = jnp.where(is_win, c, jnp.where(jnp.logical_and(is_sel, delta < 2), delta, 3))
        return br, kidx, sidx, tidx

    def issue(c, ring, set_rows=True):
        br, kidx, sidx, _ = slot_params(c)
        k = k_ref[0, 0, br, kidx]
        if set_rows:
            qa_scr[NSA_DK:NSA_DK + NSA_SEL_ROWS, :] = sel_scr[sidx]
        for h in range(NSA_HG):
            ring[h][...] = _dot(k, qa_scr[:, h * NSA_TQ:(h + 1) * NSA_TQ])
            yield

    def consume(c, ring, near):
        br, kidx, _, tidx = slot_params(c)
        vt = vt_ref[0, 0, br, kidx]
        for h in range(NSA_HG):
            m_scr, acc_scr = state[2 * h:2 * h + 2]
            sc = ring[h][...]
            if near:
                sc = sc + tiles_ref[0, tidx, :, h * NSA_TQ:(h + 1) * NSA_TQ]
            m_old = m_scr[br]
            m_new = jnp.maximum(m_old, jnp.max(sc, 0, keepdims=True))
            pp = jnp.exp2(sc - m_new)
            acc_scr[br] = jnp.exp2(m_old - m_new) * acc_scr[br] + _dot(vt, pp.astype(BF16))
            m_scr[br] = m_new
            yield

    def interleave(*gens):
        for _ in zip(*gens):
            pass

    for h in range(NSA_HG):
        m_scr, acc_scr = state[2 * h:2 * h + 2]
        m_scr[...] = jnp.full_like(m_scr, NEG)
        acc_scr[...] = jnp.zeros_like(acc_scr)
    qa_scr[...] = q_t

    def compress_and_select(nrows, nblk):
        s = _dot(kc_ref[0, 0, :nrows], q_t)
        first = (t == 0).astype(jnp.int32)
        bs = pl.multiple_of((blocks_per_tile * t - 8) * (1 - first), 8)
        row = lax.broadcasted_iota(jnp.int32, (nrows, L), 0)
        s_scr[:nrows] = jnp.where(row < bs + NSA_BAND, s, NEG)
        s_scr[pl.ds(bs, NSA_BAND), :] += band_ref[0, first]
        s = s_scr[:nrows]
        m = jnp.max(s, 0, keepdims=True)
        m = jnp.where(m < NEG_TEST, 0.0, m)
        p = jnp.exp2(s - m)
        den = jnp.maximum(jnp.sum(p, 0, keepdims=True), 1e-30)
        p = p * (1.0 / den)
        oc_scr[...] = _dot(vct_ref[0, 0, :, :nrows], p.astype(BF16))

        psum = p[:, 0:NSA_TQ]
        for h in range(1, NSA_HG):
            psum = psum + p[:, h * NSA_TQ:(h + 1) * NSA_TQ]
        p1, p2, p3 = _split3(psum)
        ovt = ovt_ref[:nblk, :nrows]
        imp = _dot(ovt, p1) + _dot(ovt, p2) + _dot(ovt, p3)
        interleave(issue(0, ring_a, set_rows=False))
        blk = lax.broadcasted_iota(jnp.int32, (nblk, NSA_TQ), 0)
        lane = lax.broadcasted_iota(jnp.int32, (nblk, NSA_TQ), 1)
        cur = (NSA_TQ // SEL_BLOCK) * t + lane // SEL_BLOCK
        forced = (blk == 0) | (blk == cur) | (blk == cur - 1)
        v = jnp.where(blk > cur, -FORCE, jnp.where(forced, -jnp.inf, imp))
        blk_f = blk.astype(F32)
        sel = jnp.where(forced, 1.0, 0.0)
        for _ in range(min(SEL_TOPN, nblk) - 3):
            mx = jnp.max(v, 0, keepdims=True)
            idx = jnp.min(jnp.where(v == mx, blk_f, float(nblk)), 0, keepdims=True)
            hit = blk_f == idx
            sel = jnp.where(hit, 1.0, sel)
            v = jnp.where(hit, -jnp.inf, v)
        selneg = jnp.where(sel > 0.5, 0.0, NEG)
        selneg = jnp.concatenate([selneg] * NSA_HG, 1).astype(BF16)
        for u in range(nblk // NSA_SEL_ROWS):
            sel_scr[u] = selneg[NSA_SEL_ROWS * u:NSA_SEL_ROWS * (u + 1), :]

    n_var = -(-nqt // NSA_TILES_PER_VARIANT)
    for i in range(n_var):
        nrows = min(ncp, (i + 1) * NSA_TILES_PER_VARIANT * blocks_per_tile)
        nblk = min(nb, (i + 1) * NSA_TILES_PER_VARIANT * NSA_TQ // SEL_BLOCK)

        @pl.when(t // NSA_TILES_PER_VARIANT == i)
        def _(nrows=nrows, nblk=nblk):
            compress_and_select(nrows, nblk)

    sel_scr[n_slab] = jnp.full((NSA_SEL_ROWS, L), NEG, BF16)

    def make_trip(first_slot, near, pairs=1):
        def trip(i, carry):
            for pr in range(pairs):
                c0 = first_slot + 2 * (pairs * i + pr)
                interleave(issue(c0 + 1, ring_b), consume(c0, ring_a, near))
                interleave(issue(c0 + 2, ring_a), consume(c0 + 1, ring_b, near))
            return carry
        return trip

    far_pairs = jnp.maximum(n_slots - 2 - NSA_HEAD_SLOTS, 0) // 2
    far_trips = far_pairs // NSA_FAR_UNROLL
    rest_first = NSA_HEAD_SLOTS + 2 * NSA_FAR_UNROLL * far_trips
    tail_first = NSA_HEAD_SLOTS + 2 * far_pairs
    tail_trips = (jnp.maximum(n_slots - tail_first, 0) + 1) // 2
    make_trip(0, True, NSA_HEAD_SLOTS // 2)(0, 0)
    lax.fori_loop(0, far_trips, make_trip(NSA_HEAD_SLOTS, False, NSA_FAR_UNROLL), 0)
    lax.fori_loop(0, far_pairs - NSA_FAR_UNROLL * far_trips, make_trip(rest_first, False), 0)
    lax.fori_loop(0, tail_trips, make_trip(tail_first, True), 0)

    gate = gate_ref[0, 0, 0]
    outs = []
    for h in range(NSA_HG):
        sl = slice(h * NSA_TQ, (h + 1) * NSA_TQ)
        acc_scr = state[2 * h + 1]
        o_s = acc_scr[0, :NSA_DV] * (1.0 / jnp.maximum(acc_scr[0, NSA_DV:NSA_DV + 1], 1e-30))
        o_w = acc_scr[1, :NSA_DV] * (1.0 / jnp.maximum(acc_scr[1, NSA_DV:NSA_DV + 1], 1e-30))
        outs.append(gate[0:1, sl] * oc_scr[:, sl] + gate[1:2, sl] * o_s + gate[2:3, sl] * o_w)
    o_ref[0] = jnp.concatenate(outs, 0).T.astype(o_ref.dtype)


def _nsa_attention(q_t, kcmp, vcmp_t, k, v_t, gate, tiles, band, ovt):
    b, g, nqt = q_t.shape[:3]
    ncp = kcmp.shape[2]
    nkc = k.shape[3]
    nb = ovt.shape[0]
    L = NSA_LANES
    body = functools.partial(_nsa_body, nb=nb, nqt=nqt)
    ring = NSA_HG * [pltpu.VMEM((NSA_KC, NSA_TQ), F32)]
    head_state = [pltpu.VMEM((2, 1, NSA_TQ), F32), pltpu.VMEM((2, NSA_VROWS, NSA_TQ), F32)]
    grp = lambda bb, gg, t: (bb, gg, 0, 0)
    grp6 = lambda bb, gg, t: (bb, gg, 0, 0, 0, 0)
    return pl.pallas_call(
        body,
        out_shape=jax.ShapeDtypeStruct((b, nqt * NSA_TQ, g * NSA_HG * NSA_DV), BF16),
        grid=(b, g, nqt),
        in_specs=[pl.BlockSpec((1, 1, 1, NSA_DKP, L), lambda bb, gg, t: (bb, gg, t, 0, 0)),
                  pl.BlockSpec((1, 1, ncp, NSA_DKP), grp),
                  pl.BlockSpec((1, 1, NSA_DV, ncp), grp),
                  pl.BlockSpec((1, 1, 2, nkc, NSA_KC, NSA_DKP), grp6),
                  pl.BlockSpec((1, 1, 2, nkc, NSA_VROWS, NSA_KC), grp6),
                  pl.BlockSpec((1, 1, 1, N_BRANCH, L), lambda bb, gg, t: (bb, gg, t, 0, 0)),
                  pl.BlockSpec((1, 4, NSA_KC, L), lambda bb, gg, t: (gg, 0, 0, 0)),
                  pl.BlockSpec((1, 2, NSA_BAND, L), lambda bb, gg, t: (gg, 0, 0, 0)),
                  pl.BlockSpec(ovt.shape, lambda bb, gg, t: (0, 0))],
        out_specs=pl.BlockSpec((1, NSA_TQ, NSA_HG * NSA_DV), lambda bb, gg, t: (bb, t, gg)),
        scratch_shapes=[pltpu.VMEM((ncp, L), F32),
                        pltpu.VMEM((nb // NSA_SEL_ROWS + 1, NSA_SEL_ROWS, L), BF16),
                        pltpu.VMEM((NSA_DKP, L), BF16),
                        pltpu.VMEM((NSA_DV, L), F32)]
        + 2 * ring + NSA_HG * head_state,
        compiler_params=_cparams(("parallel", "parallel", "arbitrary")),
        name="nsa_attention",
    )(q_t, kcmp, vcmp_t, k, v_t, gate, tiles, band, ovt)


def _nsa_proj_body(x_ref, wq_ref, wk_ref, wv_ref, wg_ref, wc_ref,
                   q_ref, k_ref, v_ref, gate_ref, c_ref, *, qscale):
    g, hg, tq = NSA_GROUPS, NSA_HG, NSA_TQ
    xb = x_ref[...].astype(BF16)
    q_t = _dot_nt(wq_ref[...], xb) * qscale
    gate_t = jax.nn.sigmoid(_dot_nt(wg_ref[...], xb))
    for gg in range(g):
        for h in range(hg):
            head = gg * hg + h
            q_ref[0, gg, 0, :, h * tq:(h + 1) * tq] = q_t[head * NSA_DKP:(head + 1) * NSA_DKP].astype(BF16)
            r0 = head * NSA_GATE_ROWS
            gate_ref[0, gg, 0, :, h * tq:(h + 1) * tq] = gate_t[r0:r0 + N_BRANCH]
    k = _dot(xb, wk_ref[...])
    v_t = _dot_nt(wv_ref[...], xb).astype(BF16)
    per = tq // SEL_BLOCK
    chunk = pl.program_id(0) % (NSA_SEL_ROWS // per)
    row = lax.broadcasted_iota(jnp.int32, (tq, NSA_DKP), 0)
    col = lax.broadcasted_iota(jnp.int32, (tq, NSA_DKP), 1)
    blk_flag = jnp.where(col - NSA_DK == per * chunk + row // SEL_BLOCK, 1.0, 0.0)
    ones_rows = jnp.where(lax.broadcasted_iota(jnp.int32, (NSA_VROWS - NSA_DV, tq), 0) == 0, 1.0, 0.0).astype(BF16)
    for gg in range(g):
        for br in range(2):
            kb = k[:, (br * g + gg) * NSA_DKP:(br * g + gg + 1) * NSA_DKP]
            if br == 0:
                kb = kb + blk_flag
            k_ref[0, gg, br, 0] = kb.astype(BF16)
            v_ref[0, gg, br, 0, :NSA_DV, :] = v_t[(br * g + gg) * NSA_DV:(br * g + gg + 1) * NSA_DV]
            v_ref[0, gg, br, 0, NSA_DV:, :] = ones_rows
    ctok = _dot(xb, wc_ref[...]).astype(BF16)
    for gg in range(g):
        for kv in range(2):
            c_ref[0, gg, kv, 0] = ctok[:, (kv * g + gg) * NSA_DKP:(kv * g + gg + 1) * NSA_DKP]


def _nsa_proj(x2, b, s, wq_t, wk, wv_t, wg_t, wc):
    g, hg, tq, L = NSA_GROUPS, NSA_HG, NSA_TQ, NSA_LANES
    nqt = s // tq
    assert nqt % (NSA_SEL_ROWS * SEL_BLOCK // tq) == 0 and (s // SEL_BLOCK) % NSA_SEL_ROWS == 0
    t, d = x2.shape
    full = lambda i: (0, 0)
    tile5 = lambda i: (i // nqt, 0, i % nqt, 0, 0)
    tile6 = lambda i: (i // nqt, 0, 0, i % nqt, 0, 0)
    body = functools.partial(_nsa_proj_body, qscale=(NSA_DK ** -0.5) * LOG2E)
    return pl.pallas_call(
        body,
        out_shape=(jax.ShapeDtypeStruct((b, g, nqt, NSA_DKP, L), BF16),
                   jax.ShapeDtypeStruct((b, g, 2, nqt, NSA_KC, NSA_DKP), BF16),
                   jax.ShapeDtypeStruct((b, g, 2, nqt, NSA_VROWS, NSA_KC), BF16),
                   jax.ShapeDtypeStruct((b, g, nqt, N_BRANCH, L), F32),
                   jax.ShapeDtypeStruct((b, g, 2, nqt, NSA_KC, NSA_DKP), BF16)),
        grid=(t // tq,),
        in_specs=[pl.BlockSpec((tq, d), lambda i: (i, 0)),
                  pl.BlockSpec(wq_t.shape, full), pl.BlockSpec(wk.shape, full), pl.BlockSpec(wv_t.shape, full),
                  pl.BlockSpec(wg_t.shape, full), pl.BlockSpec(wc.shape, full)],
        out_specs=(pl.BlockSpec((1, g, 1, NSA_DKP, L), tile5),
                   pl.BlockSpec((1, g, 2, 1, NSA_KC, NSA_DKP), tile6),
                   pl.BlockSpec((1, g, 2, 1, NSA_VROWS, NSA_KC), tile6),
                   pl.BlockSpec((1, g, 1, N_BRANCH, L), tile5),
                   pl.BlockSpec((1, g, 2, 1, NSA_KC, NSA_DKP), tile6)),
        compiler_params=_cparams(("parallel",)),
        name="nsa_proj",
    )(x2, wq_t, wk, wv_t, wg_t, wc)


def _nsa_mixer(x2, b, s, w_in, pe_k, w1_k, w2_k, pe_v, w1_v, w2_v, rel_bias):
    assert NSA_TQ == NSA_KC
    t, d = x2.shape
    h, g, hg, dk, dv = NSA_HEADS, NSA_GROUPS, NSA_HG, NSA_DK, NSA_DV
    nb = s // SEL_BLOCK
    sizes = [h * dk, g * dk, g * dv, g * dk, g * dv, g * dk, g * dv, h * N_BRANCH]
    c = [0] + [int(v) for v in np.cumsum(sizes)]
    cols = [w_in[:, c[i]:c[i + 1]] for i in range(len(sizes))]
    w_q, w_kc, w_vc, w_ks, w_vs, w_kw, w_vw, w_gate = cols

    def pad_last(a, n):
        return jnp.pad(a, [(0, 0)] * (a.ndim - 1) + [(0, n - a.shape[-1])])

    wq_t = pad_last(w_q.reshape(d, h, dk), NSA_DKP).reshape(d, h * NSA_DKP).T.astype(BF16)
    wk = jnp.concatenate([pad_last(w.reshape(d, g, dk), NSA_DKP).reshape(d, g * NSA_DKP)
                          for w in (w_ks, w_kw)], 1).astype(BF16)
    wv_t = jnp.concatenate([w_vs, w_vw], 1).T.astype(BF16)
    wg_t = pad_last(w_gate.reshape(d, h, N_BRANCH), NSA_GATE_ROWS).reshape(d, h * NSA_GATE_ROWS).T.astype(BF16)
    wc = jnp.concatenate([pad_last(w_kc.reshape(d, g, dk), NSA_DKP).reshape(d, g * NSA_DKP),
                          pad_last(w_vc.reshape(d, g, dv), NSA_DKP).reshape(d, g * NSA_DKP)], 1).astype(BF16)
    q_t, k, v_t, gate, ctok = _nsa_proj(x2, b, s, wq_t, wk, wv_t, wg_t, wc)

    nch = s // CMP_STRIDE
    ctok = ctok.reshape(b, g, 2, nch, CMP_STRIDE * NSA_DKP)
    k_cmp = _compress(ctok, 0, pe_k, w1_k, pad_last(w2_k, NSA_DKP), NSA_DKP)
    v_cmp = _compress(ctok, 1, pe_v, w1_v, w2_v, NSA_DKP)
    k_cmp = k_cmp.reshape(b, g, nch, NSA_DKP)
    vcmp_t = v_cmp.reshape(b, g, nch, dv).transpose(0, 1, 3, 2)

    tiles, band = _nsa_tables(rel_bias, s)
    ovt = _overlap_t(nch, nb)
    o = _nsa_attention(q_t, k_cmp, vcmp_t, k, v_t, gate, tiles, band, ovt)
    return o.reshape(t, h * dv)


def _out_ln_router_body(o_ref, w_ref, res_ref, lg_ref, lb_ref, wr_ref, x_ref, xb_ref, logit_ref):
    y = _layer_norm(ALPHA * res_ref[...] + _dot(o_ref[...], w_ref[...]), lg_ref[...], lb_ref[...])
    x_ref[...] = y
    xb_ref[...] = y.astype(BF16)
    x1, x2, _ = _split3(y)
    w1, w2, _ = _split3(wr_ref[...])
    logit_ref[...] = _dot(x1, w1) + (_dot(x1, w2) + _dot(x2, w1))


def _out_ln_router(o, w_out, res, lg, lb, w_router, *, tm=512):
    m, k = o.shape
    n = w_out.shape[1]
    ne = w_router.shape[1]
    wr = jnp.pad(w_router, ((0, 0), (0, 128 - ne)))
    row = lambda i: (i, 0)
    full = lambda i: (0, 0)
    x, xb, logits = pl.pallas_call(
        _out_ln_router_body,
        out_shape=(jax.ShapeDtypeStruct((m, n), F32), jax.ShapeDtypeStruct((m, n), BF16),
                   jax.ShapeDtypeStruct((m, 128), F32)),
        grid=(m // tm,),
        in_specs=[pl.BlockSpec((tm, k), row), pl.BlockSpec((k, n), full), pl.BlockSpec((tm, n), row),
                  pl.BlockSpec((1, n), full), pl.BlockSpec((1, n), full), pl.BlockSpec((n, 128), full)],
        out_specs=(pl.BlockSpec((tm, n), row), pl.BlockSpec((tm, n), row), pl.BlockSpec((tm, 128), row)),
        compiler_params=_cparams(("parallel",)),
        name="nsa_out_ln_router",
    )(o, w_out, res, lg.reshape(1, n), lb.reshape(1, n), wr)
    return x, xb, logits[:, :ne]


def _dispatch_body(ir_ref, ic_ref, fl_ref, x_ref, rt_ref, o_ref):
    i = pl.program_id(0)
    flag = fl_ref[i]
    tok = ic_ref[i] * MOE_TC + lax.broadcasted_iota(jnp.int32, (MOE_BLK, MOE_TC), 1)
    onehot = jnp.where(rt_ref[...] == tok, 1.0, 0.0).astype(BF16)
    rows = _dot(onehot, x_ref[...])

    @pl.when(flag == 3)
    def _():
        o_ref[...] = rows.astype(o_ref.dtype)

    @pl.when(flag == 1)
    def _():
        o_ref[...] = (o_ref[...].astype(F32) + rows).astype(o_ref.dtype)


def _dispatch(x_bf, row_tok_col, item_r, item_c, flags):
    t, d = x_bf.shape
    r = row_tok_col.shape[0]
    ni = item_r.shape[0]
    gs = pltpu.PrefetchScalarGridSpec(
        num_scalar_prefetch=3, grid=(ni,),
        in_specs=[pl.BlockSpec((MOE_TC, d), lambda i, ir, ic, fl: (ic[i], 0)),
                  pl.BlockSpec((MOE_BLK, 1), lambda i, ir, ic, fl: (ir[i], 0))],
        out_specs=pl.BlockSpec((MOE_BLK, d), lambda i, ir, ic, fl: (ir[i], 0)))
    return pl.pallas_call(
        _dispatch_body, grid_spec=gs,
        out_shape=jax.ShapeDtypeStruct((r, d), BF16),
        compiler_params=_cparams(("arbitrary",)),
        name="moe_dispatch",
    )(item_r, item_c, flags, x_bf, row_tok_col)


def _expert_body(be_ref, x_ref, wg_ref, wu_ref, wd_ref, rw_ref, o_ref, acc_scr):
    f = pl.program_id(1)
    part = _swiglu_tile(x_ref[...], lambda a, b: wg_ref[0, :, a:b], lambda a, b: wu_ref[0, :, a:b],
                        lambda a, b: wd_ref[0, a:b, :], wg_ref.shape[2])

    @pl.when(f == 0)
    def _():
        acc_scr[...] = part

    @pl.when(f > 0)
    def _():
        acc_scr[...] += part

    @pl.when(f == pl.num_programs(1) - 1)
    def _():
        o_ref[...] = (acc_scr[...] * rw_ref[...]).astype(o_ref.dtype)


def _experts(xs, wg, wu, wd, row_w_col, block_expert):
    r, d = xs.shape
    nbk = r // MOE_BLK
    nf = wg.shape[2] // MOE_TF
    gs = pltpu.PrefetchScalarGridSpec(
        num_scalar_prefetch=1, grid=(nbk, nf),
        in_specs=[pl.BlockSpec((MOE_BLK, d), lambda i, f, be: (i, 0)),
                  pl.BlockSpec((1, d, MOE_TF), lambda i, f, be: (be[i], 0, f)),
                  pl.BlockSpec((1, d, MOE_TF), lambda i, f, be: (be[i], 0, f)),
                  pl.BlockSpec((1, MOE_TF, d), lambda i, f, be: (be[i], f, 0)),
                  pl.BlockSpec((MOE_BLK, 1), lambda i, f, be: (i, 0))],
        out_specs=pl.BlockSpec((MOE_BLK, d), lambda i, f, be: (i, 0)),
        scratch_shapes=[pltpu.VMEM((MOE_BLK, d), F32)])
    return pl.pallas_call(
        _expert_body, grid_spec=gs,
        out_shape=jax.ShapeDtypeStruct((r, d), BF16),
        compiler_params=_cparams(("parallel", "arbitrary")),
        name="moe_experts",
    )(block_expert, xs, wg, wu, wd, row_w_col)


def _combine_body(ir_ref, ic_ref, fl_ref, y_ref, rt_ref, x_ref, g_ref, b_ref, o_ref):
    i = pl.program_id(0)
    flag = fl_ref[i]
    tok = ic_ref[i] * MOE_TC + lax.broadcasted_iota(jnp.int32, (MOE_TC, MOE_BLK), 0)
    onehot = jnp.where(rt_ref[0] == tok, 1.0, 0.0).astype(BF16)
    part = _dot(onehot, y_ref[...])

    @pl.when((flag & 3) == 3)
    def _():
        o_ref[...] = part

    @pl.when((flag & 3) == 1)
    def _():
        o_ref[...] += part

    @pl.when((flag & 4) == 4)
    def _():
        o_ref[...] = _layer_norm(ALPHA * x_ref[...] + o_ref[...], g_ref[...], b_ref[...])


def _combine_ln(out_rows, row_tok_lane, item_r, item_c, flags, x, g, b):
    r, d = out_rows.shape
    t = x.shape[0]
    ni = item_r.shape[0]
    vec = pl.BlockSpec((1, d), lambda i, ir, ic, fl: (0, 0))
    gs = pltpu.PrefetchScalarGridSpec(
        num_scalar_prefetch=3, grid=(ni,),
        in_specs=[pl.BlockSpec((MOE_BLK, d), lambda i, ir, ic, fl: (ir[i], 0)),
                  pl.BlockSpec((1, 1, MOE_BLK), lambda i, ir, ic, fl: (ir[i], 0, 0)),
                  pl.BlockSpec((MOE_TC, d), lambda i, ir, ic, fl: (ic[i], 0)),
                  vec, vec],
        out_specs=pl.BlockSpec((MOE_TC, d), lambda i, ir, ic, fl: (ic[i], 0)))
    return pl.pallas_call(
        _combine_body, grid_spec=gs,
        out_shape=jax.ShapeDtypeStruct((t, d), F32),
        compiler_params=_cparams(("arbitrary",)),
        name="moe_combine_ln",
    )(item_r, item_c, flags, out_rows, row_tok_lane, x, g.reshape(1, d), b.reshape(1, d))


def _moe_plan(top_idx, wts, t):
    e, blk, tc = N_EXPERTS, MOE_BLK, MOE_TC
    a = t * TOP_K
    i32 = jnp.int32
    exp_flat = top_idx.reshape(a).astype(i32)
    tok_flat = jnp.arange(a, dtype=i32) // TOP_K
    _, tok_sorted, w_sorted = lax.sort((exp_flat, tok_flat, wts.reshape(a)), num_keys=1, is_stable=True)
    counts = jnp.sum((exp_flat[:, None] == jnp.arange(e, dtype=i32)[None, :]).astype(i32), 0)
    padded = ((counts + blk - 1) // blk) * blk
    grp_start = jnp.cumsum(counts) - counts
    pad_end = jnp.cumsum(padded)
    pad_start = pad_end - padded
    nbk = a // blk + e
    r = nbk * blk
    tok_ext = jnp.concatenate([tok_sorted, jnp.full((r - a,), -1, i32)])
    w_ext = jnp.concatenate([w_sorted, jnp.zeros((r - a,), F32)])
    rows = jnp.arange(r, dtype=i32)
    row_tok = jnp.full((r,), -1, i32)
    row_w = jnp.zeros((r,), F32)
    for ee in range(e):
        inside = (rows >= pad_start[ee]) & (rows < pad_start[ee] + counts[ee])
        shift = pad_start[ee] - grp_start[ee]
        row_tok = jnp.where(inside, jnp.roll(tok_ext, shift), row_tok)
        row_w = jnp.where(inside, jnp.roll(w_ext, shift), row_w)
    blk_first = jnp.arange(nbk, dtype=i32) * blk
    block_expert = jnp.minimum(jnp.sum((pad_end[None, :] <= blk_first[:, None]).astype(i32), 1), e - 1)

    rt = row_tok.reshape(nbk, blk)
    valid = rt >= 0
    t_lo = jnp.min(jnp.where(valid, rt, t), 1)
    t_hi = jnp.max(rt, 1)
    has = t_hi >= 0
    c_lo = jnp.where(has, t_lo // tc, 0)
    c_hi = jnp.where(has, t_hi // tc, 0)
    n_it = c_hi - c_lo + 1
    off_end = jnp.cumsum(n_it)
    off_start = off_end - n_it
    total = off_end[-1]
    ni = nbk + e * (t // tc)
    idx = jnp.arange(ni, dtype=i32)
    ok = idx < total
    ir = jnp.minimum(jnp.sum((off_end[None, :] <= idx[:, None]).astype(i32), 1), nbk - 1)
    ic = jnp.where(ok, c_lo[ir] + idx - off_start[ir], c_hi[nbk - 1]).astype(i32)
    first = ok & (idx == off_start[ir])
    d_flags = ok.astype(i32) + 2 * first.astype(i32)

    key = jnp.where(ok, ic * nbk + ir, jnp.iinfo(jnp.int32).max)
    perm = jnp.argsort(key)
    ok2 = ok[perm]
    last = total - 1
    cr = jnp.where(ok2, ir[perm], ir[perm][last]).astype(i32)
    cc = jnp.where(ok2, ic[perm], ic[perm][last]).astype(i32)
    first2 = ok2 & jnp.concatenate([jnp.ones((1,), bool), cc[1:] != cc[:-1]])
    last2 = ok2 & jnp.concatenate([(cc[1:] != cc[:-1]) | ~ok2[1:], jnp.ones((1,), bool)])
    c_flags = ok2.astype(i32) + 2 * first2.astype(i32) + 4 * last2.astype(i32)
    return dict(row_tok=row_tok, row_w=row_w, block_expert=block_expert,
                d_items=(ir, ic, d_flags), c_items=(cr, cc, c_flags), nbk=nbk)


def _moe_ln(x2, x_bf, logits, wg, wu, wd, ln_g, ln_b):
    t, d = x2.shape
    top_val, top_idx = lax.top_k(logits, TOP_K)
    wts = jax.nn.softmax(top_val, -1)
    plan = _moe_plan(top_idx, wts, t)
    nbk = plan["nbk"]
    xs = _dispatch(x_bf, plan["row_tok"].reshape(-1, 1), *plan["d_items"])
    out_rows = _experts(xs, wg.astype(BF16), wu.astype(BF16), wd.astype(BF16),
                        plan["row_w"].reshape(-1, 1), plan["block_expert"])
    return _combine_ln(out_rows, plan["row_tok"].reshape(nbk, 1, MOE_BLK), *plan["c_items"], x2, ln_g, ln_b)


def _forward(x, mla_w_in, mla_q_norm, mla_w_q_up, mla_kv_norm, mla_w_kv_up, mla_w_out, nsa_w_in,
             nsa_cmp_pe_k, nsa_cmp_w1_k, nsa_cmp_w2_k, nsa_cmp_pe_v, nsa_cmp_w1_v, nsa_cmp_w2_v,
             nsa_w_out, rel_bias, ffn_w_gate, ffn_w_up, ffn_w_down, moe_w_router, moe_w_gate,
             moe_w_up, moe_w_down, ln_mix_g, ln_mix_b, ln_ffn_g, ln_ffn_b):
    b, s, d = x.shape
    x2 = x.reshape(b * s, d)
    o = _mla_mixer(x2, b, s, mla_w_in[0], mla_q_norm[0], mla_w_q_up[0], mla_kv_norm[0], mla_w_kv_up[0])
    x2 = _linear(o, mla_w_out[0].astype(BF16), tm=1024, tn=d, out_dtype=F32,
                 ln=(x2, ln_mix_g[0], ln_mix_b[0]), name="mla_out_ln")
    x2 = _ffn_ln(x2, ffn_w_gate[0].astype(BF16), ffn_w_up[0].astype(BF16), ffn_w_down[0].astype(BF16),
                 ln_ffn_g[0], ln_ffn_b[0])
    o = _nsa_mixer(x2, b, s, nsa_w_in[0], nsa_cmp_pe_k[0], nsa_cmp_w1_k[0], nsa_cmp_w2_k[0],
                   nsa_cmp_pe_v[0], nsa_cmp_w1_v[0], nsa_cmp_w2_v[0], rel_bias)
    x2, x_bf, logits = _out_ln_router(o, nsa_w_out[0].astype(BF16), x2, ln_mix_g[1], ln_mix_b[1],
                                      moe_w_router[0])
    x2 = _moe_ln(x2, x_bf, logits, moe_w_gate[0], moe_w_up[0], moe_w_down[0], ln_ffn_g[1], ln_ffn_b[1])
    return x2.reshape(b, s, d)


@jax.jit
def kernel(x, mla_w_in, mla_q_norm, mla_w_q_up, mla_kv_norm, mla_w_kv_up, mla_w_out, nsa_w_in,
           nsa_cmp_pe_k, nsa_cmp_w1_k, nsa_cmp_w2_k, nsa_cmp_pe_v, nsa_cmp_w1_v, nsa_cmp_w2_v,
           nsa_w_out, rel_bias, ffn_w_gate, ffn_w_up, ffn_w_down, moe_w_router, moe_w_gate,
           moe_w_up, moe_w_down, ln_mix_g, ln_mix_b, ln_ffn_g, ln_ffn_b):
    return _forward(x, mla_w_in, mla_q_norm, mla_w_q_up, mla_kv_norm, mla_w_kv_up, mla_w_out, nsa_w_in,
                    nsa_cmp_pe_k, nsa_cmp_w1_k, nsa_cmp_w2_k, nsa_cmp_pe_v, nsa_cmp_w1_v, nsa_cmp_w2_v,
                    nsa_w_out, rel_bias, ffn_w_gate, ffn_w_up, ffn_w_down, moe_w_router, moe_w_gate,
                    moe_w_up, moe_w_down, ln_mix_g, ln_mix_b, ln_ffn_g, ln_ffn_b)
```

```python
import functools
import math

import numpy as np
import jax
import jax.numpy as jnp
from jax import lax
from jax.experimental import pallas as pl
from jax.experimental.pallas import tpu as pltpu

F32 = jnp.float32
BF16 = jnp.bfloat16

D_MODEL = 1024
DEPTH = 2

MLA_HEADS = 8
MLA_Q_RANK = 512
MLA_KV_RANK = 256
MLA_NOPE = 128
MLA_ROPE = 64
MLA_V = 128
ROPE_THETA = 10000.0

NSA_HEADS = 16
NSA_GROUPS = 4
NSA_HG = NSA_HEADS // NSA_GROUPS
NSA_DK = 96
NSA_DV = 64
CMP_LEN = 32
CMP_STRIDE = 16
SEL_BLOCK = 64
SEL_TOPN = 16
WINDOW = 512
N_BRANCH = 3
FORCE = 1e6

REL_BUCKETS = 32
REL_MAX_DIST = 128

D_FF = 2816
N_EXPERTS = 8
TOP_K = 2
D_FF_EXPERT = 3584

LN_EPS = 1e-5
RMS_EPS = 1e-6

ALPHA = (2.0 * DEPTH) ** 0.25

NEG = -1e30
NEG_TEST = -1e29

V7X_VMEM_LIMIT = 56 * 1024 * 1024

LOG2E = 1.4426950408889634

NSA_TQ = 256
NSA_LANES = NSA_HG * NSA_TQ
NSA_KC = 256
NSA_DKP = 128
NSA_GATE_ROWS = 8
NSA_BAND = 24
NSA_SEL_ROWS = 16
NSA_VROWS = NSA_DV + 16
NSA_HEAD_SLOTS = 4
NSA_FAR_UNROLL = 4
NSA_TILES_PER_VARIANT = 4

MOE_BLK = 512
MOE_TC = 512
MOE_TF = 1792


def _cparams(sem, vmem=V7X_VMEM_LIMIT):
    return pltpu.CompilerParams(dimension_semantics=sem, vmem_limit_bytes=vmem)


def _layer_norm(r, g, b):
    mu = jnp.mean(r, -1, keepdims=True)
    d = r - mu
    var = jnp.mean(d * d, -1, keepdims=True)
    return d * lax.rsqrt(var + LN_EPS) * g + b


def _rms_norm(x, g):
    return x * lax.rsqrt(jnp.mean(x * x, -1, keepdims=True) + RMS_EPS) * g


def _split3(a):
    a1 = a.astype(BF16)
    r1 = a - a1.astype(F32)
    a2 = r1.astype(BF16)
    a3 = (r1 - a2.astype(F32)).astype(BF16)
    return a1, a2, a3


def _dot(a, b):
    return jnp.dot(a, b, preferred_element_type=F32)


def _dot_nt(a, b):
    return lax.dot_general(a, b, (((1,), (1,)), ((), ())), preferred_element_type=F32)


def _linear_body(*refs, has_ln):
    it = iter(refs)
    x_ref = next(it)
    w_ref = next(it)
    if has_ln:
        res_ref, lg_ref, lb_ref = next(it), next(it), next(it)
    o_ref = next(it)
    acc = _dot(x_ref[...].astype(BF16), w_ref[...])
    if has_ln:
        acc = _layer_norm(ALPHA * res_ref[...] + acc, lg_ref[...], lb_ref[...])
    o_ref[...] = acc.astype(o_ref.dtype)


def _linear(x, w, *, tm, tn, out_dtype, ln=None, name):
    m, k = x.shape
    n = w.shape[1]
    assert m % tm == 0 and n % tn == 0
    in_specs = [pl.BlockSpec((tm, k), lambda i, j: (i, 0)),
                pl.BlockSpec((k, tn), lambda i, j: (0, j))]
    args = [x, w]
    if ln is not None:
        assert tn == n
        res, lg, lb = ln
        in_specs += [pl.BlockSpec((tm, n), lambda i, j: (i, 0)),
                     pl.BlockSpec((1, n), lambda i, j: (0, 0)),
                     pl.BlockSpec((1, n), lambda i, j: (0, 0))]
        args += [res, lg.reshape(1, n), lb.reshape(1, n)]
    return pl.pallas_call(
        functools.partial(_linear_body, has_ln=ln is not None),
        out_shape=jax.ShapeDtypeStruct((m, n), out_dtype),
        grid=(m // tm, n // tn),
        in_specs=in_specs,
        out_specs=pl.BlockSpec((tm, tn), lambda i, j: (i, j)),
        compiler_params=_cparams(("parallel", "arbitrary")),
        name=name,
    )(*args)


def _rope_tables(s):
    half = MLA_ROPE // 2
    freq = ROPE_THETA ** (-jnp.arange(half, dtype=F32) / half)
    ang = jnp.arange(s).astype(F32)[:, None] * freq[None, :]
    cos, sin = jnp.cos(ang), jnp.sin(ang)
    return jnp.concatenate([cos, cos], -1), jnp.concatenate([-sin, sin], -1)


MLA_DQ = MLA_NOPE + MLA_ROPE
MLA_QROWS = MLA_NOPE + 2 * MLA_ROPE
MLA_T = 512
MLA_HPS = 2
MLA_SUB = 256
MLA_FAR_UNROLL = 4


def _mla_q_body(lat_ref, g_ref, w_ref, cos_ref, sin_ref, o_ref, *, qscale):
    xn = _rms_norm(lat_ref[...], g_ref[...]).astype(BF16)
    y = _dot_nt(w_ref[...], xn)
    cos, sin = cos_ref[...], sin_ref[...]
    for h in range(MLA_HEADS):
        r0 = h * MLA_QROWS
        o_ref[0, h, :MLA_NOPE, :] = (y[r0:r0 + MLA_NOPE] * qscale).astype(BF16)
        a = y[r0 + MLA_NOPE:r0 + MLA_DQ]
        bb = y[r0 + MLA_DQ:r0 + MLA_QROWS]
        o_ref[0, h, MLA_NOPE:, :] = ((a * cos + bb * sin) * qscale).astype(BF16)


def _mla_q_proj(lat, gain, w_t, cos_t, sin_t, b, s, *, tm=MLA_T):
    ns = s // tm
    body = functools.partial(_mla_q_body, qscale=(MLA_DQ ** -0.5) * LOG2E)
    return pl.pallas_call(
        body,
        out_shape=jax.ShapeDtypeStruct((b, MLA_HEADS, MLA_DQ, s), BF16),
        grid=(b * ns,),
        in_specs=[pl.BlockSpec((tm, MLA_Q_RANK), lambda i: (i, 0)),
                  pl.BlockSpec((1, MLA_Q_RANK), lambda i: (0, 0)),
                  pl.BlockSpec(w_t.shape, lambda i: (0, 0)),
                  pl.BlockSpec((MLA_ROPE, tm), lambda i: (0, i % ns)),
                  pl.BlockSpec((MLA_ROPE, tm), lambda i: (0, i % ns))],
        out_specs=pl.BlockSpec((1, MLA_HEADS, MLA_DQ, tm), lambda i: (i // ns, 0, 0, i % ns)),
        compiler_params=_cparams(("parallel",)),
        name="mla_q_proj",
    )(lat, gain.reshape(1, -1), w_t, cos_t, sin_t)


def _mla_kv_body(lat_ref, g_ref, wk_ref, wvt_ref, kr_ref, cos_ref, sin_ref, k_ref, vt_ref):
    xn = _rms_norm(lat_ref[...], g_ref[...]).astype(BF16)
    kn = _dot(xn, wk_ref[...]).astype(BF16)
    vt = _dot_nt(wvt_ref[...], xn).astype(BF16)
    kr = kr_ref[...]
    rot = (kr[:, :MLA_ROPE] * cos_ref[...] + kr[:, MLA_ROPE:] * sin_ref[...]).astype(BF16)
    for h in range(MLA_HEADS):
        k_ref[0, h, :, :MLA_NOPE] = kn[:, h * MLA_NOPE:(h + 1) * MLA_NOPE]
        k_ref[0, h, :, MLA_NOPE:] = rot
        for kk in range(MLA_T // MLA_SUB):
            vt_ref[0, h, kk] = vt[h * MLA_V:(h + 1) * MLA_V, kk * MLA_SUB:(kk + 1) * MLA_SUB]


def _mla_kv_proj(lat, gain, wk, wv_t, cosx, sinx, b, s, *, tm=MLA_T):
    ns = s // tm
    nsub = tm // MLA_SUB
    return pl.pallas_call(
        _mla_kv_body,
        out_shape=(jax.ShapeDtypeStruct((b, MLA_HEADS, s, MLA_DQ), BF16),
                   jax.ShapeDtypeStruct((b, MLA_HEADS, ns * nsub, MLA_V, MLA_SUB), BF16)),
        grid=(b * ns,),
        in_specs=[pl.BlockSpec((tm, MLA_KV_RANK), lambda i: (i, MLA_Q_RANK // MLA_KV_RANK)),
                  pl.BlockSpec((1, MLA_KV_RANK), lambda i: (0, 0)),
                  pl.BlockSpec(wk.shape, lambda i: (0, 0)),
                  pl.BlockSpec(wv_t.shape, lambda i: (0, 0)),
                  pl.BlockSpec((tm, 2 * MLA_ROPE),
                               lambda i: (i, (MLA_Q_RANK + MLA_KV_RANK) // (2 * MLA_ROPE))),
                  pl.BlockSpec((tm, MLA_ROPE), lambda i: (i % ns, 0)),
                  pl.BlockSpec((tm, MLA_ROPE), lambda i: (i % ns, 0))],
        out_specs=(pl.BlockSpec((1, MLA_HEADS, tm, MLA_DQ), lambda i: (i // ns, 0, i % ns, 0)),
                   pl.BlockSpec((1, MLA_HEADS, nsub, MLA_V, MLA_SUB), lambda i: (i // ns, 0, i % ns, 0, 0))),
        compiler_params=_cparams(("parallel",)),
        name="mla_kv_proj",
    )(lat, gain.reshape(1, -1), wk, wv_t, lat, cosx, sinx)


def _mla_attn_body(qt_ref, k_ref, vt_ref, o_ref, *scr):
    i = pl.program_id(2)
    sub = MLA_SUB
    nsub = MLA_T // sub
    assert nsub == 2
    chains_all = [(h, ql) for ql in range(nsub) for h in range(MLA_HPS)]
    nch = len(chains_all)
    ring_a, ring_b, scr = scr[:nch], scr[nch:2 * nch], scr[2 * nch:]
    state = {(h, ql): scr[3 * (nsub * h + ql):3 * (nsub * h + ql) + 3]
             for h in range(MLA_HPS) for ql in range(nsub)}
    for m_scr, l_scr, acc_scr in state.values():
        m_scr[...] = jnp.full_like(m_scr, NEG)
        l_scr[...] = jnp.zeros_like(l_scr)
        acc_scr[...] = jnp.zeros_like(acc_scr)

    def issue(sk, ring, chains):
        for idx, (h, ql) in enumerate(chains_all):
            if (h, ql) in chains:
                k = k_ref[0, h, pl.ds(pl.multiple_of(sk * sub, sub), sub), :]
                ring[idx][...] = _dot(k, qt_ref[0, h, :, ql * sub:(ql + 1) * sub])
            yield

    def consume(sk, ring, chains, diag_ql=None):
        for idx, (h, ql) in enumerate(chains_all):
            if (h, ql) in chains:
                m_scr, l_scr, acc_scr = state[(h, ql)]
                s = ring[idx][...]
                if ql == diag_ql:
                    key = lax.broadcasted_iota(jnp.int32, s.shape, 0)
                    qry = lax.broadcasted_iota(jnp.int32, s.shape, 1)
                    s = jnp.where(key <= qry, s, NEG)
                m_old = m_scr[...]
                m_new = jnp.maximum(m_old, jnp.max(s, 0, keepdims=True))
                a = jnp.exp2(m_old - m_new)
                p = jnp.exp2(s - m_new)
                l_scr[...] = a * l_scr[...] + jnp.sum(p, 0, keepdims=True)
                acc_scr[...] = a * acc_scr[...] + _dot(vt_ref[0, h, sk], p.astype(BF16))
                m_scr[...] = m_new
            yield

    def interleave(*gens):
        for _ in zip(*gens):
            pass

    interleave(issue(0, ring_a, chains_all))

    def pair(c0):
        interleave(issue(c0 + 1, ring_b, chains_all), consume(c0, ring_a, chains_all))
        interleave(issue(c0 + 2, ring_a, chains_all), consume(c0 + 1, ring_b, chains_all))

    def trip(j, carry):
        for pr in range(MLA_FAR_UNROLL):
            pair(2 * (MLA_FAR_UNROLL * j + pr))
        return carry

    lax.fori_loop(0, i // MLA_FAR_UNROLL, trip, 0)
    lax.fori_loop(i // MLA_FAR_UNROLL * MLA_FAR_UNROLL, i, lambda j, carry: (pair(2 * j), carry)[1], 0)
    upper = [c for c in chains_all if c[1] == 1]
    interleave(issue(2 * i + 1, ring_b, upper), consume(2 * i, ring_a, chains_all, diag_ql=0))
    interleave(consume(2 * i + 1, ring_b, upper, diag_ql=1))
    for (h, ql), (m_scr, l_scr, acc_scr) in state.items():
        o_t = acc_scr[...] * (1.0 / jnp.maximum(l_scr[...], 1e-30))
        o_ref[0, ql * sub:(ql + 1) * sub, h * MLA_V:(h + 1) * MLA_V] = o_t.T.astype(o_ref.dtype)


def _mla_attention(q_t, k, v_t, b, s):
    tq = MLA_T
    nq = s // tq
    hp = MLA_HPS
    nsub = tq // MLA_SUB
    sub_state = [pltpu.VMEM((1, MLA_SUB), F32), pltpu.VMEM((1, MLA_SUB), F32), pltpu.VMEM((MLA_V, MLA_SUB), F32)]
    ring = hp * nsub * [pltpu.VMEM((MLA_SUB, MLA_SUB), F32)]
    return pl.pallas_call(
        _mla_attn_body,
        out_shape=jax.ShapeDtypeStruct((b, s, MLA_HEADS * MLA_V), BF16),
        grid=(b, MLA_HEADS // hp, nq),
        in_specs=[pl.BlockSpec((1, hp, MLA_DQ, tq), lambda bb, h, i: (bb, h, 0, i)),
                  pl.BlockSpec((1, hp, s, MLA_DQ), lambda bb, h, i: (bb, h, 0, 0)),
                  pl.BlockSpec((1, hp, nq * nsub, MLA_V, MLA_SUB), lambda bb, h, i: (bb, h, 0, 0, 0))],
        out_specs=pl.BlockSpec((1, tq, hp * MLA_V), lambda bb, h, i: (bb, i, h)),
        scratch_shapes=2 * ring + hp * nsub * sub_state,
        compiler_params=_cparams(("parallel", "parallel", "arbitrary")),
        name="mla_attention",
    )(q_t, k, v_t)


def _mla_mixer(x2, b, s, w_in, q_norm, w_q_up, kv_norm, w_kv_up):
    r0 = MLA_Q_RANK + MLA_KV_RANK
    half = MLA_ROPE // 2
    w_in_ext = jnp.concatenate([w_in, w_in[:, r0 + half:r0 + MLA_ROPE], w_in[:, r0:r0 + half]], 1)
    lat = _linear(x2, w_in_ext.astype(BF16), tm=512, tn=w_in_ext.shape[1], out_dtype=F32, name="mla_in")
    wq = w_q_up.reshape(MLA_Q_RANK, MLA_HEADS, MLA_DQ)
    wr = wq[..., MLA_NOPE:]
    wq = jnp.concatenate([wq, wr[..., half:], wr[..., :half]], -1)
    wq_t = wq.reshape(MLA_Q_RANK, MLA_HEADS * MLA_QROWS).T.astype(BF16)
    wkv = w_kv_up.reshape(MLA_KV_RANK, MLA_HEADS, MLA_NOPE + MLA_V)
    wk = wkv[..., :MLA_NOPE].reshape(MLA_KV_RANK, MLA_HEADS * MLA_NOPE).astype(BF16)
    wv_t = wkv[..., MLA_NOPE:].reshape(MLA_KV_RANK, MLA_HEADS * MLA_V).T.astype(BF16)
    cosx, sinx = _rope_tables(s)
    q_t = _mla_q_proj(lat, q_norm, wq_t, cosx.T, sinx.T, b, s)
    k, v_t = _mla_kv_proj(lat, kv_norm, wk, wv_t, cosx, sinx, b, s)
    o = _mla_attention(q_t, k, v_t, b, s)
    return o.reshape(b * s, MLA_HEADS * MLA_V)


def _ffn_body(x_ref, wg_ref, wu_ref, wd_ref, lg_ref, lb_ref, o_ref, acc_scr):
    f = pl.program_id(1)
    xb = x_ref[...].astype(BF16)
    h = jax.nn.silu(_dot(xb, wg_ref[...])) * _dot(xb, wu_ref[...])
    part = _dot(h.astype(BF16), wd_ref[...])

    @pl.when(f == 0)
    def _():
        acc_scr[...] = part

    @pl.when(f > 0)
    def _():
        acc_scr[...] += part

    @pl.when(f == pl.num_programs(1) - 1)
    def _():
        o_ref[...] = _layer_norm(ALPHA * x_ref[...] + acc_scr[...], lg_ref[...], lb_ref[...])


def _ffn_ln(x, wg, wu, wd, lg, lb, *, tm=512, tf=1408):
    m, d = x.shape
    dff = wg.shape[1]
    assert dff % tf == 0
    vec = pl.BlockSpec((1, d), lambda i, f: (0, 0))
    return pl.pallas_call(
        _ffn_body,
        out_shape=jax.ShapeDtypeStruct((m, d), F32),
        grid=(m // tm, dff // tf),
        in_specs=[pl.BlockSpec((tm, d), lambda i, f: (i, 0)),
                  pl.BlockSpec((d, tf), lambda i, f: (0, f)),
                  pl.BlockSpec((d, tf), lambda i, f: (0, f)),
                  pl.BlockSpec((tf, d), lambda i, f: (f, 0)),
                  vec, vec],
        out_specs=pl.BlockSpec((tm, d), lambda i, f: (i, 0)),
        scratch_shapes=[pltpu.VMEM((tm, d), F32)],
        compiler_params=_cparams(("parallel", "arbitrary")),
        name="ffn_ln",
    )(x, wg, wu, wd, lg.reshape(1, d), lb.reshape(1, d))


def _compress_body(c_ref, pea_ref, peb_ref, w1a_ref, w1b_ref, w2_ref, o_ref):
    c = c_ref[0, 0, 0].astype(F32)
    ya = _dot((c + pea_ref[...]).astype(BF16), w1a_ref[...])
    yb = _dot((c + peb_ref[...]).astype(BF16), w1b_ref[...])
    yb_next = jnp.concatenate([yb[1:], jnp.zeros_like(yb[:1])], 0)
    h = jax.nn.gelu(ya + yb_next)
    o_ref[0] = _dot(h.astype(BF16), w2_ref[...]).astype(o_ref.dtype)


def _compress(chunks, kv, pe, w1, w2, dpad):
    b, g, _, nch, k = chunks.shape
    bg = b * g
    d = pe.shape[1]
    dh, dout = w2.shape
    half = CMP_LEN // 2
    assert half == CMP_STRIDE and k == CMP_STRIDE * dpad

    def pad_tok(a):
        a = jnp.pad(a, [(0, 0), (0, dpad - d)] + [(0, 0)] * (a.ndim - 2))
        return a[:half].reshape((k,) + a.shape[2:]), a[half:].reshape((k,) + a.shape[2:])

    pe_a, pe_b = pad_tok(pe)
    w1a, w1b = pad_tok(w1)
    full = lambda i: (0, 0)
    return pl.pallas_call(
        _compress_body,
        out_shape=jax.ShapeDtypeStruct((bg, nch, dout), BF16),
        grid=(bg,),
        in_specs=[pl.BlockSpec((1, 1, 1, nch, k), lambda i: (i // g, i % g, kv, 0, 0)),
                  pl.BlockSpec((1, k), full), pl.BlockSpec((1, k), full),
                  pl.BlockSpec((k, dh), full), pl.BlockSpec((k, dh), full),
                  pl.BlockSpec((dh, dout), full)],
        out_specs=pl.BlockSpec((1, nch, dout), lambda i: (i, 0, 0)),
        compiler_params=_cparams(("parallel",)),
        name="nsa_compress",
    )(chunks, pe_a.reshape(1, k), pe_b.reshape(1, k), w1a.astype(BF16), w1b.astype(BF16), w2.astype(BF16))


def _rel_bucket_np(dist):
    n = np.maximum(dist, 0)
    max_exact = REL_BUCKETS // 2
    nf = np.maximum(n, 1).astype(np.float32)
    large = max_exact + (np.log(nf / np.float32(max_exact)) / np.float32(math.log(REL_MAX_DIST / max_exact))
                         * np.float32(REL_BUCKETS - max_exact)).astype(np.int32)
    large = np.minimum(large, REL_BUCKETS - 1)
    return np.where(n < max_exact, n, large).astype(np.int32)


def _nsa_tables(rel_bias, s):
    g, hg, tq, kc = NSA_GROUPS, NSA_HG, NSA_TQ, NSA_KC
    assert tq == kc and tq % CMP_STRIDE == 0 and WINDOW == 2 * kc
    assert np.all(_rel_bucket_np(np.arange(tq // 2 - 15, s + tq)) == REL_BUCKETS - 1)
    rb = rel_bias.reshape(REL_BUCKETS, g, hg) * LOG2E

    def tile(base, step, rows, valid):
        p = tq + step * rows
        k = np.arange(p)
        k = np.where(k < p - step * (rows - 1), k, k - p)
        d = base + k
        vec = jnp.where(valid(d)[:, None, None], rb[_rel_bucket_np(d)], NEG)
        vec = vec.transpose(1, 2, 0)
        flat = jnp.tile(vec, (1, 1, rows))[..., :rows * (p - step)]
        mat = flat.reshape(g, hg, rows, p - step)[..., :tq]
        return mat.transpose(0, 2, 1, 3).reshape(g, rows, hg * tq)

    causal = lambda d: d >= 0
    far = jnp.broadcast_to(rb[REL_BUCKETS - 1][:, None, :, None], (g, 1, hg, tq)).reshape(g, 1, hg * tq)
    rel = lambda x: jnp.where(x > NEG_TEST, x - far, NEG)
    tiles = jnp.stack([rel(tile(0, 1, kc, causal)), rel(tile(tq, 1, kc, causal)),
                       rel(tile(2 * tq, 1, kc, lambda d: d < WINDOW)),
                       jnp.zeros((g, kc, hg * tq), F32)], 1)
    band = jnp.stack([rel(tile(8 * CMP_STRIDE - CMP_LEN + 1, CMP_STRIDE, NSA_BAND, causal)),
                      rel(tile(-(CMP_LEN - 1), CMP_STRIDE, NSA_BAND, causal))], 1)
    return tiles, band


def _overlap_t(nc_pad, nb):
    n = np.arange(nc_pad)[None, :]
    jb = np.arange(nb)[:, None]
    cstart = n * CMP_STRIDE
    cend = cstart + CMP_LEN - 1
    sstart = jb * SEL_BLOCK
    ov = (cstart <= sstart + SEL_BLOCK - 1) & (cend >= sstart) & (n < nc_pad - 1)
    return jnp.asarray(ov.astype(np.float32), BF16)


def _nsa_body(qt_ref, kc_ref, vct_ref, k_ref, vt_ref, gate_ref, tiles_ref,
              band_ref, ovt_ref, o_ref, s_scr, sel_scr, qa_scr, oc_scr, *scr, nb, nqt):
    t = pl.program_id(2)
    ring_a, ring_b, state = scr[:NSA_HG], scr[NSA_HG:2 * NSA_HG], scr[2 * NSA_HG:]
    nkc = k_ref.shape[3]
    L = NSA_LANES
    q_t = qt_ref[0, 0, 0]
    ncp = kc_ref.shape[2]
    per = NSA_KC // SEL_BLOCK
    blocks_per_tile = NSA_TQ // CMP_STRIDE

    n_slab = nb // NSA_SEL_ROWS

    n_win = jnp.minimum(t, 2) + 1
    n_slots = n_win + t + 1

    def slot_params(c):
        is_win = c < n_win
        j = c - n_win
        is_sel = jnp.logical_and(c >= n_win, j <= t)
        delta = t - j
        br = is_win.astype(jnp.int32)
        kidx = jnp.where(is_win, t - c, jnp.where(is_sel, j, 0))
        sidx = jnp.where(is_sel, j // (NSA_SEL_ROWS // per), jnp.where(is_win, 0, n_slab))
        tidx = jnp.where(is_win, c, jnp.where(jnp.logical_and(is_sel, delta < 2), delta, 3))
        return br, kidx, sidx, tidx

    def issue(c, ring, set_rows=True):
        br, kidx, sidx, _ = slot_params(c)
        k = k_ref[0, 0, br, kidx]
        if set_rows:
            qa_scr[NSA_DK:NSA_DK + NSA_SEL_ROWS, :] = sel_scr[sidx]
        for h in range(NSA_HG):
            ring[h][...] = _dot(k, qa_scr[:, h * NSA_TQ:(h + 1) * NSA_TQ])
            yield

    def consume(c, ring, near):
        br, kidx, _, tidx = slot_params(c)
        vt = vt_ref[0, 0, br, kidx]
        for h in range(NSA_HG):
            m_scr, acc_scr = state[2 * h:2 * h + 2]
            sc = ring[h][...]
            if near:
                sc = sc + tiles_ref[0, tidx, :, h * NSA_TQ:(h + 1) * NSA_TQ]
            m_old = m_scr[br]
            m_new = jnp.maximum(m_old, jnp.max(sc, 0, keepdims=True))
            pp = jnp.exp2(sc - m_new)
            acc_scr[br] = jnp.exp2(m_old - m_new) * acc_scr[br] + _dot(vt, pp.astype(BF16))
            m_scr[br] = m_new
            yield

    def interleave(*gens):
        for _ in zip(*gens):
            pass

    for h in range(NSA_HG):
        m_scr, acc_scr = state[2 * h:2 * h + 2]
        m_scr[...] = jnp.full_like(m_scr, NEG)
        acc_scr[...] = jnp.zeros_like(acc_scr)
    qa_scr[...] = q_t

    def compress_and_select(nrows, nblk):
        s = _dot(kc_ref[0, 0, :nrows], q_t)
        first = (t == 0).astype(jnp.int32)
        bs = pl.multiple_of((blocks_per_tile * t - 8) * (1 - first), 8)
        row = lax.broadcasted_iota(jnp.int32, (nrows, L), 0)
        s_scr[:nrows] = jnp.where(row < bs + NSA_BAND, s, NEG)
        s_scr[pl.ds(bs, NSA_BAND), :] += band_ref[0, first]
        s = s_scr[:nrows]
        m = jnp.max(s, 0, keepdims=True)
        m = jnp.where(m < NEG_TEST, 0.0, m)
        p = jnp.exp2(s - m)
        den = jnp.maximum(jnp.sum(p, 0, keepdims=True), 1e-30)
        p = p * (1.0 / den)
        oc_scr[...] = _dot(vct_ref[0, 0, :, :nrows], p.astype(BF16))

        psum = p[:, 0:NSA_TQ]
        for h in range(1, NSA_HG):
            psum = psum + p[:, h * NSA_TQ:(h + 1) * NSA_TQ]
        p1, p2, p3 = _split3(psum)
        ovt = ovt_ref[:nblk, :nrows]
        imp = _dot(ovt, p1) + _dot(ovt, p2) + _dot(ovt, p3)
        interleave(issue(0, ring_a, set_rows=False))
        blk = lax.broadcasted_iota(jnp.int32, (nblk, NSA_TQ), 0)
        lane = lax.broadcasted_iota(jnp.int32, (nblk, NSA_TQ), 1)
        cur = (NSA_TQ // SEL_BLOCK) * t + lane // SEL_BLOCK
        forced = (blk == 0) | (blk == cur) | (blk == cur - 1)
        v = jnp.where(blk > cur, -FORCE, jnp.where(forced, -jnp.inf, imp))
        blk_f = blk.astype(F32)
        sel = jnp.where(forced, 1.0, 0.0)
        for _ in range(min(SEL_TOPN, nblk) - 3):
            mx = jnp.max(v, 0, keepdims=True)
            idx = jnp.min(jnp.where(v == mx, blk_f, float(nblk)), 0, keepdims=True)
            hit = blk_f == idx
            sel = jnp.where(hit, 1.0, sel)
            v = jnp.where(hit, -jnp.inf, v)
        selneg = jnp.where(sel > 0.5, 0.0, NEG)
        selneg = jnp.concatenate([selneg] * NSA_HG, 1).astype(BF16)
        for u in range(nblk // NSA_SEL_ROWS):
            sel_scr[u] = selneg[NSA_SEL_ROWS * u:NSA_SEL_ROWS * (u + 1), :]

    n_var = -(-nqt // NSA_TILES_PER_VARIANT)
    for i in range(n_var):
        nrows = min(ncp, (i + 1) * NSA_TILES_PER_VARIANT * blocks_per_tile)
        nblk = min(nb, (i + 1) * NSA_TILES_PER_VARIANT * NSA_TQ // SEL_BLOCK)

        @pl.when(t // NSA_TILES_PER_VARIANT == i)
        def _(nrows=nrows, nblk=nblk):
            compress_and_select(nrows, nblk)

    sel_scr[n_slab] = jnp.full((NSA_SEL_ROWS, L), NEG, BF16)

    def make_trip(first_slot, near, pairs=1):
        def trip(i, carry):
            for pr in range(pairs):
                c0 = first_slot + 2 * (pairs * i + pr)
                interleave(issue(c0 + 1, ring_b), consume(c0, ring_a, near))
                interleave(issue(c0 + 2, ring_a), consume(c0 + 1, ring_b, near))
            return carry
        return trip

    far_pairs = jnp.maximum(n_slots - 2 - NSA_HEAD_SLOTS, 0) // 2
    far_trips = far_pairs // NSA_FAR_UNROLL
    rest_first = NSA_HEAD_SLOTS + 2 * NSA_FAR_UNROLL * far_trips
    tail_first = NSA_HEAD_SLOTS + 2 * far_pairs
    tail_trips = (jnp.maximum(n_slots - tail_first, 0) + 1) // 2
    make_trip(0, True, NSA_HEAD_SLOTS // 2)(0, 0)
    lax.fori_loop(0, far_trips, make_trip(NSA_HEAD_SLOTS, False, NSA_FAR_UNROLL), 0)
    lax.fori_loop(0, far_pairs - NSA_FAR_UNROLL * far_trips, make_trip(rest_first, False), 0)
    lax.fori_loop(0, tail_trips, make_trip(tail_first, True), 0)

    gate = gate_ref[0, 0, 0]
    outs = []
    for h in range(NSA_HG):
        sl = slice(h * NSA_TQ, (h + 1) * NSA_TQ)
        acc_scr = state[2 * h + 1]
        o_s = acc_scr[0, :NSA_DV] * (1.0 / jnp.maximum(acc_scr[0, NSA_DV:NSA_DV + 1], 1e-30))
        o_w = acc_scr[1, :NSA_DV] * (1.0 / jnp.maximum(acc_scr[1, NSA_DV:NSA_DV + 1], 1e-30))
        outs.append(gate[0:1, sl] * oc_scr[:, sl] + gate[1:2, sl] * o_s + gate[2:3, sl] * o_w)
    o_ref[0] = jnp.concatenate(outs, 0).T.astype(o_ref.dtype)


def _nsa_attention(q_t, kcmp, vcmp_t, k, v_t, gate, tiles, band, ovt):
    b, g, nqt = q_t.shape[:3]
    ncp = kcmp.shape[2]
    nkc = k.shape[3]
    nb = ovt.shape[0]
    L = NSA_LANES
    body = functools.partial(_nsa_body, nb=nb, nqt=nqt)
    ring = NSA_HG * [pltpu.VMEM((NSA_KC, NSA_TQ), F32)]
    head_state = [pltpu.VMEM((2, 1, NSA_TQ), F32), pltpu.VMEM((2, NSA_VROWS, NSA_TQ), F32)]
    grp = lambda bb, gg, t: (bb, gg, 0, 0)
    grp6 = lambda bb, gg, t: (bb, gg, 0, 0, 0, 0)
    return pl.pallas_call(
        body,
        out_shape=jax.ShapeDtypeStruct((b, nqt * NSA_TQ, g * NSA_HG * NSA_DV), BF16),
        grid=(b, g, nqt),
        in_specs=[pl.BlockSpec((1, 1, 1, NSA_DKP, L), lambda bb, gg, t: (bb, gg, t, 0, 0)),
                  pl.BlockSpec((1, 1, ncp, NSA_DKP), grp),
                  pl.BlockSpec((1, 1, NSA_DV, ncp), grp),
                  pl.BlockSpec((1, 1, 2, nkc, NSA_KC, NSA_DKP), grp6),
                  pl.BlockSpec((1, 1, 2, nkc, NSA_VROWS, NSA_KC), grp6),
                  pl.BlockSpec((1, 1, 1, N_BRANCH, L), lambda bb, gg, t: (bb, gg, t, 0, 0)),
                  pl.BlockSpec((1, 4, NSA_KC, L), lambda bb, gg, t: (gg, 0, 0, 0)),
                  pl.BlockSpec((1, 2, NSA_BAND, L), lambda bb, gg, t: (gg, 0, 0, 0)),
                  pl.BlockSpec(ovt.shape, lambda bb, gg, t: (0, 0))],
        out_specs=pl.BlockSpec((1, NSA_TQ, NSA_HG * NSA_DV), lambda bb, gg, t: (bb, t, gg)),
        scratch_shapes=[pltpu.VMEM((ncp, L), F32),
                        pltpu.VMEM((nb // NSA_SEL_ROWS + 1, NSA_SEL_ROWS, L), BF16),
                        pltpu.VMEM((NSA_DKP, L), BF16),
                        pltpu.VMEM((NSA_DV, L), F32)]
        + 2 * ring + NSA_HG * head_state,
        compiler_params=_cparams(("parallel", "parallel", "arbitrary")),
        name="nsa_attention",
    )(q_t, kcmp, vcmp_t, k, v_t, gate, tiles, band, ovt)


def _nsa_proj_body(x_ref, wq_ref, wk_ref, wv_ref, wg_ref, wc_ref,
                   q_ref, k_ref, v_ref, gate_ref, c_ref, ctok_scr, *, qscale):
    g, hg, tq = NSA_GROUPS, NSA_HG, NSA_TQ
    xb = x_ref[...].astype(BF16)
    q_t = _dot_nt(wq_ref[...], xb) * qscale
    gate_t = jax.nn.sigmoid(_dot_nt(wg_ref[...], xb))
    for gg in range(g):
        for h in range(hg):
            head = gg * hg + h
            q_ref[0, gg, 0, :, h * tq:(h + 1) * tq] = q_t[head * NSA_DKP:(head + 1) * NSA_DKP].astype(BF16)
            r0 = head * NSA_GATE_ROWS
            gate_ref[0, gg, 0, :, h * tq:(h + 1) * tq] = gate_t[r0:r0 + N_BRANCH]
    k = _dot(xb, wk_ref[...])
    v_t = _dot_nt(wv_ref[...], xb).astype(BF16)
    per = tq // SEL_BLOCK
    chunk = pl.program_id(0) % (NSA_SEL_ROWS // per)
    row = lax.broadcasted_iota(jnp.int32, (tq, NSA_DKP), 0)
    col = lax.broadcasted_iota(jnp.int32, (tq, NSA_DKP), 1)
    blk_flag = jnp.where(col - NSA_DK == per * chunk + row // SEL_BLOCK, 1.0, 0.0)
    ones_rows = jnp.where(lax.broadcasted_iota(jnp.int32, (NSA_VROWS - NSA_DV, tq), 0) == 0, 1.0, 0.0).astype(BF16)
    for gg in range(g):
        for br in range(2):
            kb = k[:, (br * g + gg) * NSA_DKP:(br * g + gg + 1) * NSA_DKP]
            if br == 0:
                kb = kb + blk_flag
            k_ref[0, gg, br, 0] = kb.astype(BF16)
            v_ref[0, gg, br, 0, :NSA_DV, :] = v_t[(br * g + gg) * NSA_DV:(br * g + gg + 1) * NSA_DV]
            v_ref[0, gg, br, 0, NSA_DV:, :] = ones_rows
    ctok = _dot(xb, wc_ref[...])
    for gg in range(g):
        for kv in range(2):
            j = kv * g + gg
            ctok_scr[j] = ctok[:, j * NSA_DKP:(j + 1) * NSA_DKP]
            for l in range(CMP_STRIDE):
                rows = ctok_scr[j, pl.ds(l, tq // CMP_STRIDE, stride=CMP_STRIDE), :]
                c_ref[0, gg, kv, :, l * NSA_DKP:(l + 1) * NSA_DKP] = rows.astype(BF16)


def _nsa_proj(x2, b, s, wq_t, wk, wv_t, wg_t, wc):
    g, hg, tq, L = NSA_GROUPS, NSA_HG, NSA_TQ, NSA_LANES
    nqt = s // tq
    assert nqt % (NSA_SEL_ROWS * SEL_BLOCK // tq) == 0 and (s // SEL_BLOCK) % NSA_SEL_ROWS == 0
    t, d = x2.shape
    full = lambda i: (0, 0)
    tile5 = lambda i: (i // nqt, 0, i % nqt, 0, 0)
    tile6 = lambda i: (i // nqt, 0, 0, i % nqt, 0, 0)
    body = functools.partial(_nsa_proj_body, qscale=(NSA_DK ** -0.5) * LOG2E)
    return pl.pallas_call(
        body,
        out_shape=(jax.ShapeDtypeStruct((b, g, nqt, NSA_DKP, L), BF16),
                   jax.ShapeDtypeStruct((b, g, 2, nqt, NSA_KC, NSA_DKP), BF16),
                   jax.ShapeDtypeStruct((b, g, 2, nqt, NSA_VROWS, NSA_KC), BF16),
                   jax.ShapeDtypeStruct((b, g, nqt, N_BRANCH, L), F32),
                   jax.ShapeDtypeStruct((b, g, 2, s // CMP_STRIDE, CMP_STRIDE * NSA_DKP), BF16)),
        grid=(t // tq,),
        in_specs=[pl.BlockSpec((tq, d), lambda i: (i, 0)),
                  pl.BlockSpec(wq_t.shape, full), pl.BlockSpec(wk.shape, full), pl.BlockSpec(wv_t.shape, full),
                  pl.BlockSpec(wg_t.shape, full), pl.BlockSpec(wc.shape, full)],
        out_specs=(pl.BlockSpec((1, g, 1, NSA_DKP, L), tile5),
                   pl.BlockSpec((1, g, 2, 1, NSA_KC, NSA_DKP), tile6),
                   pl.BlockSpec((1, g, 2, 1, NSA_VROWS, NSA_KC), tile6),
                   pl.BlockSpec((1, g, 1, N_BRANCH, L), tile5),
                   pl.BlockSpec((1, g, 2, tq // CMP_STRIDE, CMP_STRIDE * NSA_DKP),
                                lambda i: (i // nqt, 0, 0, i % nqt, 0))),
        scratch_shapes=[pltpu.VMEM((wc.shape[1] // NSA_DKP, tq, NSA_DKP), F32)],
        compiler_params=_cparams(("parallel",)),
        name="nsa_proj",
    )(x2, wq_t, wk, wv_t, wg_t, wc)


def _nsa_mixer(x2, b, s, w_in, pe_k, w1_k, w2_k, pe_v, w1_v, w2_v, rel_bias):
    assert NSA_TQ == NSA_KC
    t, d = x2.shape
    h, g, hg, dk, dv = NSA_HEADS, NSA_GROUPS, NSA_HG, NSA_DK, NSA_DV
    nb = s // SEL_BLOCK
    sizes = [h * dk, g * dk, g * dv, g * dk, g * dv, g * dk, g * dv, h * N_BRANCH]
    c = [0] + [int(v) for v in np.cumsum(sizes)]
    cols = [w_in[:, c[i]:c[i + 1]] for i in range(len(sizes))]
    w_q, w_kc, w_vc, w_ks, w_vs, w_kw, w_vw, w_gate = cols

    def pad_last(a, n):
        return jnp.pad(a, [(0, 0)] * (a.ndim - 1) + [(0, n - a.shape[-1])])

    wq_t = pad_last(w_q.reshape(d, h, dk), NSA_DKP).reshape(d, h * NSA_DKP).T.astype(BF16)
    wk = jnp.concatenate([pad_last(w.reshape(d, g, dk), NSA_DKP).reshape(d, g * NSA_DKP)
                          for w in (w_ks, w_kw)], 1).astype(BF16)
    wv_t = jnp.concatenate([w_vs, w_vw], 1).T.astype(BF16)
    wg_t = pad_last(w_gate.reshape(d, h, N_BRANCH), NSA_GATE_ROWS).reshape(d, h * NSA_GATE_ROWS).T.astype(BF16)
    wc = jnp.concatenate([pad_last(w_kc.reshape(d, g, dk), NSA_DKP).reshape(d, g * NSA_DKP),
                          pad_last(w_vc.reshape(d, g, dv), NSA_DKP).reshape(d, g * NSA_DKP)], 1).astype(BF16)
    q_t, k, v_t, gate, ctok = _nsa_proj(x2, b, s, wq_t, wk, wv_t, wg_t, wc)

    nch = s // CMP_STRIDE
    k_cmp = _compress(ctok, 0, pe_k, w1_k, pad_last(w2_k, NSA_DKP), NSA_DKP)
    v_cmp = _compress(ctok, 1, pe_v, w1_v, w2_v, NSA_DKP)
    k_cmp = k_cmp.reshape(b, g, nch, NSA_DKP)
    vcmp_t = v_cmp.reshape(b, g, nch, dv).transpose(0, 1, 3, 2)

    tiles, band = _nsa_tables(rel_bias, s)
    ovt = _overlap_t(nch, nb)
    o = _nsa_attention(q_t, k_cmp, vcmp_t, k, v_t, gate, tiles, band, ovt)
    return o.reshape(t, h * dv)


def _out_ln_router_body(o_ref, w_ref, res_ref, lg_ref, lb_ref, wr_ref, x_ref, xb_ref, logit_ref):
    y = _layer_norm(ALPHA * res_ref[...] + _dot(o_ref[...], w_ref[...]), lg_ref[...], lb_ref[...])
    x_ref[...] = y
    xb_ref[...] = y.astype(BF16)
    x1, x2, _ = _split3(y)
    w1, w2, _ = _split3(wr_ref[...])
    logit_ref[...] = _dot(x1, w1) + (_dot(x1, w2) + _dot(x2, w1))


def _out_ln_router(o, w_out, res, lg, lb, w_router, *, tm=512):
    m, k = o.shape
    n = w_out.shape[1]
    ne = w_router.shape[1]
    wr = jnp.pad(w_router, ((0, 0), (0, 128 - ne)))
    row = lambda i: (i, 0)
    full = lambda i: (0, 0)
    x, xb, logits = pl.pallas_call(
        _out_ln_router_body,
        out_shape=(jax.ShapeDtypeStruct((m, n), F32), jax.ShapeDtypeStruct((m, n), BF16),
                   jax.ShapeDtypeStruct((m, 128), F32)),
        grid=(m // tm,),
        in_specs=[pl.BlockSpec((tm, k), row), pl.BlockSpec((k, n), full), pl.BlockSpec((tm, n), row),
                  pl.BlockSpec((1, n), full), pl.BlockSpec((1, n), full), pl.BlockSpec((n, 128), full)],
        out_specs=(pl.BlockSpec((tm, n), row), pl.BlockSpec((tm, n), row), pl.BlockSpec((tm, 128), row)),
        compiler_params=_cparams(("parallel",)),
        name="nsa_out_ln_router",
    )(o, w_out, res, lg.reshape(1, n), lb.reshape(1, n), wr)
    return x, xb, logits[:, :ne]


def _dispatch_body(ir_ref, ic_ref, fl_ref, x_ref, rt_ref, o_ref):
    i = pl.program_id(0)
    flag = fl_ref[i]
    tok = ic_ref[i] * MOE_TC + lax.broadcasted_iota(jnp.int32, (MOE_BLK, MOE_TC), 1)
    onehot = jnp.where(rt_ref[...] == tok, 1.0, 0.0).astype(BF16)
    rows = _dot(onehot, x_ref[...])

    @pl.when(flag == 3)
    def _():
        o_ref[...] = rows.astype(o_ref.dtype)

    @pl.when(flag == 1)
    def _():
        o_ref[...] = (o_ref[...].astype(F32) + rows).astype(o_ref.dtype)


def _dispatch(x_bf, row_tok_col, item_r, item_c, flags):
    t, d = x_bf.shape
    r = row_tok_col.shape[0]
    ni = item_r.shape[0]
    gs = pltpu.PrefetchScalarGridSpec(
        num_scalar_prefetch=3, grid=(ni,),
        in_specs=[pl.BlockSpec((MOE_TC, d), lambda i, ir, ic, fl: (ic[i], 0)),
                  pl.BlockSpec((MOE_BLK, 1), lambda i, ir, ic, fl: (ir[i], 0))],
        out_specs=pl.BlockSpec((MOE_BLK, d), lambda i, ir, ic, fl: (ir[i], 0)))
    return pl.pallas_call(
        _dispatch_body, grid_spec=gs,
        out_shape=jax.ShapeDtypeStruct((r, d), BF16),
        compiler_params=_cparams(("arbitrary",)),
        name="moe_dispatch",
    )(item_r, item_c, flags, x_bf, row_tok_col)


def _expert_body(be_ref, x_ref, wg_ref, wu_ref, wd_ref, rw_ref, o_ref, acc_scr):
    f = pl.program_id(1)
    xb = x_ref[...]
    h = jax.nn.silu(_dot(xb, wg_ref[0])) * _dot(xb, wu_ref[0])
    part = _dot(h.astype(BF16), wd_ref[0])

    @pl.when(f == 0)
    def _():
        acc_scr[...] = part

    @pl.when(f > 0)
    def _():
        acc_scr[...] += part

    @pl.when(f == pl.num_programs(1) - 1)
    def _():
        o_ref[...] = (acc_scr[...] * rw_ref[...]).astype(o_ref.dtype)


def _experts(xs, wg, wu, wd, row_w_col, block_expert):
    r, d = xs.shape
    nbk = r // MOE_BLK
    nf = wg.shape[2] // MOE_TF
    gs = pltpu.PrefetchScalarGridSpec(
        num_scalar_prefetch=1, grid=(nbk, nf),
        in_specs=[pl.BlockSpec((MOE_BLK, d), lambda i, f, be: (i, 0)),
                  pl.BlockSpec((1, d, MOE_TF), lambda i, f, be: (be[i], 0, f)),
                  pl.BlockSpec((1, d, MOE_TF), lambda i, f, be: (be[i], 0, f)),
                  pl.BlockSpec((1, MOE_TF, d), lambda i, f, be: (be[i], f, 0)),
                  pl.BlockSpec((MOE_BLK, 1), lambda i, f, be: (i, 0))],
        out_specs=pl.BlockSpec((MOE_BLK, d), lambda i, f, be: (i, 0)),
        scratch_shapes=[pltpu.VMEM((MOE_BLK, d), F32)])
    return pl.pallas_call(
        _expert_body, grid_spec=gs,
        out_shape=jax.ShapeDtypeStruct((r, d), BF16),
        compiler_params=_cparams(("parallel", "arbitrary")),
        name="moe_experts",
    )(block_expert, xs, wg, wu, wd, row_w_col)


def _combine_body(ir_ref, ic_ref, fl_ref, y_ref, rt_ref, x_ref, g_ref, b_ref, o_ref):
    i = pl.program_id(0)
    flag = fl_ref[i]
    tok = ic_ref[i] * MOE_TC + lax.broadcasted_iota(jnp.int32, (MOE_TC, MOE_BLK), 0)
    onehot = jnp.where(rt_ref[0] == tok, 1.0, 0.0).astype(BF16)
    part = _dot(onehot, y_ref[...])

    @pl.when((flag & 3) == 3)
    def _():
        o_ref[...] = part

    @pl.when((flag & 3) == 1)
    def _():
        o_ref[...] += part

    @pl.when((flag & 4) == 4)
    def _():
        o_ref[...] = _layer_norm(ALPHA * x_ref[...] + o_ref[...], g_ref[...], b_ref[...])


def _combine_ln(out_rows, row_tok_lane, item_r, item_c, flags, x, g, b):
    r, d = out_rows.shape
    t = x.shape[0]
    ni = item_r.shape[0]
    vec = pl.BlockSpec((1, d), lambda i, ir, ic, fl: (0, 0))
    gs = pltpu.PrefetchScalarGridSpec(
        num_scalar_prefetch=3, grid=(ni,),
        in_specs=[pl.BlockSpec((MOE_BLK, d), lambda i, ir, ic, fl: (ir[i], 0)),
                  pl.BlockSpec((1, 1, MOE_BLK), lambda i, ir, ic, fl: (ir[i], 0, 0)),
                  pl.BlockSpec((MOE_TC, d), lambda i, ir, ic, fl: (ic[i], 0)),
                  vec, vec],
        out_specs=pl.BlockSpec((MOE_TC, d), lambda i, ir, ic, fl: (ic[i], 0)))
    return pl.pallas_call(
        _combine_body, grid_spec=gs,
        out_shape=jax.ShapeDtypeStruct((t, d), F32),
        compiler_params=_cparams(("arbitrary",)),
        name="moe_combine_ln",
    )(item_r, item_c, flags, out_rows, row_tok_lane, x, g.reshape(1, d), b.reshape(1, d))


def _moe_plan(top_idx, wts, t):
    e, blk, tc = N_EXPERTS, MOE_BLK, MOE_TC
    a = t * TOP_K
    i32 = jnp.int32
    exp_flat = top_idx.reshape(a).astype(i32)
    tok_flat = jnp.arange(a, dtype=i32) // TOP_K
    _, tok_sorted, w_sorted = lax.sort((exp_flat, tok_flat, wts.reshape(a)), num_keys=1, is_stable=True)
    counts = jnp.sum((exp_flat[:, None] == jnp.arange(e, dtype=i32)[None, :]).astype(i32), 0)
    padded = ((counts + blk - 1) // blk) * blk
    grp_start = jnp.cumsum(counts) - counts
    pad_end = jnp.cumsum(padded)
    pad_start = pad_end - padded
    nbk = a // blk + e
    r = nbk * blk
    tok_ext = jnp.concatenate([tok_sorted, jnp.full((r - a,), -1, i32)])
    w_ext = jnp.concatenate([w_sorted, jnp.zeros((r - a,), F32)])
    rows = jnp.arange(r, dtype=i32)
    row_tok = jnp.full((r,), -1, i32)
    row_w = jnp.zeros((r,), F32)
    for ee in range(e):
        inside = (rows >= pad_start[ee]) & (rows < pad_start[ee] + counts[ee])
        shift = pad_start[ee] - grp_start[ee]
        row_tok = jnp.where(inside, jnp.roll(tok_ext, shift), row_tok)
        row_w = jnp.where(inside, jnp.roll(w_ext, shift), row_w)
    blk_first = jnp.arange(nbk, dtype=i32) * blk
    block_expert = jnp.minimum(jnp.sum((pad_end[None, :] <= blk_first[:, None]).astype(i32), 1), e - 1)

    rt = row_tok.reshape(nbk, blk)
    valid = rt >= 0
    t_lo = jnp.min(jnp.where(valid, rt, t), 1)
    t_hi = jnp.max(rt, 1)
    has = t_hi >= 0
    c_lo = jnp.where(has, t_lo // tc, 0)
    c_hi = jnp.where(has, t_hi // tc, 0)
    n_it = c_hi - c_lo + 1
    off_end = jnp.cumsum(n_it)
    off_start = off_end - n_it
    total = off_end[-1]
    ni = nbk + e * (t // tc)
    idx = jnp.arange(ni, dtype=i32)
    ok = idx < total
    ir = jnp.minimum(jnp.sum((off_end[None, :] <= idx[:, None]).astype(i32), 1), nbk - 1)
    ic = jnp.where(ok, c_lo[ir] + idx - off_start[ir], c_hi[nbk - 1]).astype(i32)
    first = ok & (idx == off_start[ir])
    d_flags = ok.astype(i32) + 2 * first.astype(i32)

    key = jnp.where(ok, ic * nbk + ir, jnp.iinfo(jnp.int32).max)
    perm = jnp.argsort(key)
    ok2 = ok[perm]
    last = total - 1
    cr = jnp.where(ok2, ir[perm], ir[perm][last]).astype(i32)
    cc = jnp.where(ok2, ic[perm], ic[perm][last]).astype(i32)
    first2 = ok2 & jnp.concatenate([jnp.ones((1,), bool), cc[1:] != cc[:-1]])
    last2 = ok2 & jnp.concatenate([(cc[1:] != cc[:-1]) | ~ok2[1:], jnp.ones((1,), bool)])
    c_flags = ok2.astype(i32) + 2 * first2.astype(i32) + 4 * last2.astype(i32)
    return dict(row_tok=row_tok, row_w=row_w, block_expert=block_expert,
                d_items=(ir, ic, d_flags), c_items=(cr, cc, c_flags), nbk=nbk)


def _moe_ln(x2, x_bf, logits, wg, wu, wd, ln_g, ln_b):
    t, d = x2.shape
    top_val, top_idx = lax.top_k(logits, TOP_K)
    wts = jax.nn.softmax(top_val, -1)
    plan = _moe_plan(top_idx, wts, t)
    nbk = plan["nbk"]
    xs = _dispatch(x_bf, plan["row_tok"].reshape(-1, 1), *plan["d_items"])
    out_rows = _experts(xs, wg.astype(BF16), wu.astype(BF16), wd.astype(BF16),
                        plan["row_w"].reshape(-1, 1), plan["block_expert"])
    return _combine_ln(out_rows, plan["row_tok"].reshape(nbk, 1, MOE_BLK), *plan["c_items"], x2, ln_g, ln_b)


def _forward(x, mla_w_in, mla_q_norm, mla_w_q_up, mla_kv_norm, mla_w_kv_up, mla_w_out, nsa_w_in,
             nsa_cmp_pe_k, nsa_cmp_w1_k, nsa_cmp_w2_k, nsa_cmp_pe_v, nsa_cmp_w1_v, nsa_cmp_w2_v,
             nsa_w_out, rel_bias, ffn_w_gate, ffn_w_up, ffn_w_down, moe_w_router, moe_w_gate,
             moe_w_up, moe_w_down, ln_mix_g, ln_mix_b, ln_ffn_g, ln_ffn_b):
    b, s, d = x.shape
    x2 = x.reshape(b * s, d)
    o = _mla_mixer(x2, b, s, mla_w_in[0], mla_q_norm[0], mla_w_q_up[0], mla_kv_norm[0], mla_w_kv_up[0])
    x2 = _linear(o, mla_w_out[0].astype(BF16), tm=1024, tn=d, out_dtype=F32,
                 ln=(x2, ln_mix_g[0], ln_mix_b[0]), name="mla_out_ln")
    x2 = _ffn_ln(x2, ffn_w_gate[0].astype(BF16), ffn_w_up[0].astype(BF16), ffn_w_down[0].astype(BF16),
                 ln_ffn_g[0], ln_ffn_b[0])
    o = _nsa_mixer(x2, b, s, nsa_w_in[0], nsa_cmp_pe_k[0], nsa_cmp_w1_k[0], nsa_cmp_w2_k[0],
                   nsa_cmp_pe_v[0], nsa_cmp_w1_v[0], nsa_cmp_w2_v[0], rel_bias)
    x2, x_bf, logits = _out_ln_router(o, nsa_w_out[0].astype(BF16), x2, ln_mix_g[1], ln_mix_b[1],
                                      moe_w_router[0])
    x2 = _moe_ln(x2, x_bf, logits, moe_w_gate[0], moe_w_up[0], moe_w_down[0], ln_ffn_g[1], ln_ffn_b[1])
    return x2.reshape(b, s, d)


@jax.jit
def kernel(x, mla_w_in, mla_q_norm, mla_w_q_up, mla_kv_norm, mla_w_kv_up, mla_w_out, nsa_w_in,
           nsa_cmp_pe_k, nsa_cmp_w1_k, nsa_cmp_w2_k, nsa_cmp_pe_v, nsa_cmp_w1_v, nsa_cmp_w2_v,
           nsa_w_out, rel_bias, ffn_w_gate, ffn_w_up, ffn_w_down, moe_w_router, moe_w_gate,
           moe_w_up, moe_w_down, ln_mix_g, ln_mix_b, ln_ffn_g, ln_ffn_b):
    return _forward(x, mla_w_in, mla_q_norm, mla_w_q_up, mla_kv_norm, mla_w_kv_up, mla_w_out, nsa_w_in,
                    nsa_cmp_pe_k, nsa_cmp_w1_k, nsa_cmp_w2_k, nsa_cmp_pe_v, nsa_cmp_w1_v, nsa_cmp_w2_v,
                    nsa_w_out, rel_bias, ffn_w_gate, ffn_w_up, ffn_w_down, moe_w_router, moe_w_gate,
                    moe_w_up, moe_w_down, ln_mix_g, ln_mix_b, ln_ffn_g, ln_ffn_b)
```

```python
import functools
import math

import numpy as np
import jax
import jax.numpy as jnp
from jax import lax
from jax.experimental import pallas as pl
from jax.experimental.pallas import tpu as pltpu

F32 = jnp.float32
BF16 = jnp.bfloat16

D_MODEL = 1024
DEPTH = 2

MLA_HEADS = 8
MLA_Q_RANK = 512
MLA_KV_RANK = 256
MLA_NOPE = 128
MLA_ROPE = 64
MLA_V = 128
ROPE_THETA = 10000.0

NSA_HEADS = 16
NSA_GROUPS = 4
NSA_HG = NSA_HEADS // NSA_GROUPS
NSA_DK = 96
NSA_DV = 64
CMP_LEN = 32
CMP_STRIDE = 16
SEL_BLOCK = 64
SEL_TOPN = 16
WINDOW = 512
N_BRANCH = 3
FORCE = 1e6

REL_BUCKETS = 32
REL_MAX_DIST = 128

D_FF = 2816
N_EXPERTS = 8
TOP_K = 2
D_FF_EXPERT = 3584

LN_EPS = 1e-5
RMS_EPS = 1e-6

ALPHA = (2.0 * DEPTH) ** 0.25

NEG = -1e30
NEG_TEST = -1e29

V7X_VMEM_LIMIT = 56 * 1024 * 1024

LOG2E = 1.4426950408889634

NSA_TQ = 256
NSA_LANES = NSA_HG * NSA_TQ
NSA_KC = 256
NSA_DKP = 128
NSA_GATE_ROWS = 8
NSA_BAND = 24
NSA_SEL_ROWS = 16
NSA_VROWS = NSA_DV + 16
NSA_HEAD_SLOTS = 4
NSA_FAR_UNROLL = 4
NSA_TILES_PER_VARIANT = 4

MOE_BLK = 512
MOE_TC = 512
MOE_TF = 1792


def _cparams(sem, vmem=V7X_VMEM_LIMIT):
    return pltpu.CompilerParams(dimension_semantics=sem, vmem_limit_bytes=vmem)


def _layer_norm(r, g, b):
    mu = jnp.mean(r, -1, keepdims=True)
    d = r - mu
    var = jnp.mean(d * d, -1, keepdims=True)
    return d * lax.rsqrt(var + LN_EPS) * g + b


def _rms_norm(x, g):
    return x * lax.rsqrt(jnp.mean(x * x, -1, keepdims=True) + RMS_EPS) * g


def _split3(a):
    a1 = a.astype(BF16)
    r1 = a - a1.astype(F32)
    a2 = r1.astype(BF16)
    a3 = (r1 - a2.astype(F32)).astype(BF16)
    return a1, a2, a3


def _dot(a, b):
    return jnp.dot(a, b, preferred_element_type=F32)


def _dot_nt(a, b):
    return lax.dot_general(a, b, (((1,), (1,)), ((), ())), preferred_element_type=F32)


def _linear_body(*refs, has_ln):
    it = iter(refs)
    x_ref = next(it)
    w_ref = next(it)
    if has_ln:
        res_ref, lg_ref, lb_ref = next(it), next(it), next(it)
    o_ref = next(it)
    acc = _dot(x_ref[...].astype(BF16), w_ref[...])
    if has_ln:
        acc = _layer_norm(ALPHA * res_ref[...] + acc, lg_ref[...], lb_ref[...])
    o_ref[...] = acc.astype(o_ref.dtype)


def _linear(x, w, *, tm, tn, out_dtype, ln=None, name):
    m, k = x.shape
    n = w.shape[1]
    assert m % tm == 0 and n % tn == 0
    in_specs = [pl.BlockSpec((tm, k), lambda i, j: (i, 0)),
                pl.BlockSpec((k, tn), lambda i, j: (0, j))]
    args = [x, w]
    if ln is not None:
        assert tn == n
        res, lg, lb = ln
        in_specs += [pl.BlockSpec((tm, n), lambda i, j: (i, 0)),
                     pl.BlockSpec((1, n), lambda i, j: (0, 0)),
                     pl.BlockSpec((1, n), lambda i, j: (0, 0))]
        args += [res, lg.reshape(1, n), lb.reshape(1, n)]
    return pl.pallas_call(
        functools.partial(_linear_body, has_ln=ln is not None),
        out_shape=jax.ShapeDtypeStruct((m, n), out_dtype),
        grid=(m // tm, n // tn),
        in_specs=in_specs,
        out_specs=pl.BlockSpec((tm, tn), lambda i, j: (i, j)),
        compiler_params=_cparams(("parallel", "arbitrary")),
        name=name,
    )(*args)


def _rope_tables(s):
    half = MLA_ROPE // 2
    freq = ROPE_THETA ** (-jnp.arange(half, dtype=F32) / half)
    ang = jnp.arange(s).astype(F32)[:, None] * freq[None, :]
    cos, sin = jnp.cos(ang), jnp.sin(ang)
    return jnp.concatenate([cos, cos], -1), jnp.concatenate([-sin, sin], -1)


MLA_DQ = MLA_NOPE + MLA_ROPE
MLA_QROWS = MLA_NOPE + 2 * MLA_ROPE
MLA_T = 512
MLA_HPS = 2
MLA_SUB = 256
MLA_FAR_UNROLL = 4


def _mla_q_body(lat_ref, g_ref, w_ref, cos_ref, sin_ref, o_ref, *, qscale):
    xn = _rms_norm(lat_ref[...], g_ref[...]).astype(BF16)
    y = _dot_nt(w_ref[...], xn)
    cos, sin = cos_ref[...], sin_ref[...]
    for h in range(MLA_HEADS):
        r0 = h * MLA_QROWS
        o_ref[0, h, :MLA_NOPE, :] = (y[r0:r0 + MLA_NOPE] * qscale).astype(BF16)
        a = y[r0 + MLA_NOPE:r0 + MLA_DQ]
        bb = y[r0 + MLA_DQ:r0 + MLA_QROWS]
        o_ref[0, h, MLA_NOPE:, :] = ((a * cos + bb * sin) * qscale).astype(BF16)


def _mla_q_proj(lat, gain, w_t, cos_t, sin_t, b, s, *, tm=MLA_T):
    ns = s // tm
    body = functools.partial(_mla_q_body, qscale=(MLA_DQ ** -0.5) * LOG2E)
    return pl.pallas_call(
        body,
        out_shape=jax.ShapeDtypeStruct((b, MLA_HEADS, MLA_DQ, s), BF16),
        grid=(b * ns,),
        in_specs=[pl.BlockSpec((tm, MLA_Q_RANK), lambda i: (i, 0)),
                  pl.BlockSpec((1, MLA_Q_RANK), lambda i: (0, 0)),
                  pl.BlockSpec(w_t.shape, lambda i: (0, 0)),
                  pl.BlockSpec((MLA_ROPE, tm), lambda i: (0, i % ns)),
                  pl.BlockSpec((MLA_ROPE, tm), lambda i: (0, i % ns))],
        out_specs=pl.BlockSpec((1, MLA_HEADS, MLA_DQ, tm), lambda i: (i // ns, 0, 0, i % ns)),
        compiler_params=_cparams(("parallel",)),
        name="mla_q_proj",
    )(lat, gain.reshape(1, -1), w_t, cos_t, sin_t)


def _mla_kv_body(lat_ref, g_ref, wk_ref, wvt_ref, kr_ref, cos_ref, sin_ref, k_ref, vt_ref):
    xn = _rms_norm(lat_ref[...], g_ref[...]).astype(BF16)
    kn = _dot(xn, wk_ref[...]).astype(BF16)
    vt = _dot_nt(wvt_ref[...], xn).astype(BF16)
    kr = kr_ref[...]
    rot = (kr[:, :MLA_ROPE] * cos_ref[...] + kr[:, MLA_ROPE:] * sin_ref[...]).astype(BF16)
    for h in range(MLA_HEADS):
        k_ref[0, h, :, :MLA_NOPE] = kn[:, h * MLA_NOPE:(h + 1) * MLA_NOPE]
        k_ref[0, h, :, MLA_NOPE:] = rot
        for kk in range(MLA_T // MLA_SUB):
            vt_ref[0, h, kk] = vt[h * MLA_V:(h + 1) * MLA_V, kk * MLA_SUB:(kk + 1) * MLA_SUB]


def _mla_kv_proj(lat, gain, wk, wv_t, cosx, sinx, b, s, *, tm=MLA_T):
    ns = s // tm
    nsub = tm // MLA_SUB
    return pl.pallas_call(
        _mla_kv_body,
        out_shape=(jax.ShapeDtypeStruct((b, MLA_HEADS, s, MLA_DQ), BF16),
                   jax.ShapeDtypeStruct((b, MLA_HEADS, ns * nsub, MLA_V, MLA_SUB), BF16)),
        grid=(b * ns,),
        in_specs=[pl.BlockSpec((tm, MLA_KV_RANK), lambda i: (i, MLA_Q_RANK // MLA_KV_RANK)),
                  pl.BlockSpec((1, MLA_KV_RANK), lambda i: (0, 0)),
                  pl.BlockSpec(wk.shape, lambda i: (0, 0)),
                  pl.BlockSpec(wv_t.shape, lambda i: (0, 0)),
                  pl.BlockSpec((tm, 2 * MLA_ROPE),
                               lambda i: (i, (MLA_Q_RANK + MLA_KV_RANK) // (2 * MLA_ROPE))),
                  pl.BlockSpec((tm, MLA_ROPE), lambda i: (i % ns, 0)),
                  pl.BlockSpec((tm, MLA_ROPE), lambda i: (i % ns, 0))],
        out_specs=(pl.BlockSpec((1, MLA_HEADS, tm, MLA_DQ), lambda i: (i // ns, 0, i % ns, 0)),
                   pl.BlockSpec((1, MLA_HEADS, nsub, MLA_V, MLA_SUB), lambda i: (i // ns, 0, i % ns, 0, 0))),
        compiler_params=_cparams(("parallel",)),
        name="mla_kv_proj",
    )(lat, gain.reshape(1, -1), wk, wv_t, lat, cosx, sinx)


def _mla_attn_body(qt_ref, k_ref, vt_ref, o_ref, *scr):
    i = pl.program_id(2)
    sub = MLA_SUB
    nsub = MLA_T // sub
    assert nsub == 2
    chains_all = [(h, ql) for ql in range(nsub) for h in range(MLA_HPS)]
    nch = len(chains_all)
    ring_a, ring_b, scr = scr[:nch], scr[nch:2 * nch], scr[2 * nch:]
    state = {(h, ql): scr[3 * (nsub * h + ql):3 * (nsub * h + ql) + 3]
             for h in range(MLA_HPS) for ql in range(nsub)}
    for m_scr, l_scr, acc_scr in state.values():
        m_scr[...] = jnp.full_like(m_scr, NEG)
        l_scr[...] = jnp.zeros_like(l_scr)
        acc_scr[...] = jnp.zeros_like(acc_scr)

    def issue(sk, ring, chains):
        for idx, (h, ql) in enumerate(chains_all):
            if (h, ql) in chains:
                k = k_ref[0, h, pl.ds(pl.multiple_of(sk * sub, sub), sub), :]
                ring[idx][...] = _dot(k, qt_ref[0, h, :, ql * sub:(ql + 1) * sub])
            yield

    def consume(sk, ring, chains, diag_ql=None):
        for idx, (h, ql) in enumerate(chains_all):
            if (h, ql) in chains:
                m_scr, l_scr, acc_scr = state[(h, ql)]
                s = ring[idx][...]
                if ql == diag_ql:
                    key = lax.broadcasted_iota(jnp.int32, s.shape, 0)
                    qry = lax.broadcasted_iota(jnp.int32, s.shape, 1)
                    s = jnp.where(key <= qry, s, NEG)
                m_old = m_scr[...]
                m_new = jnp.maximum(m_old, jnp.max(s, 0, keepdims=True))
                a = jnp.exp2(m_old - m_new)
                p = jnp.exp2(s - m_new)
                l_scr[...] = a * l_scr[...] + jnp.sum(p, 0, keepdims=True)
                acc_scr[...] = a * acc_scr[...] + _dot(vt_ref[0, h, sk], p.astype(BF16))
                m_scr[...] = m_new
            yield

    def interleave(*gens):
        for _ in zip(*gens):
            pass

    interleave(issue(0, ring_a, chains_all))

    def pair(c0):
        interleave(issue(c0 + 1, ring_b, chains_all), consume(c0, ring_a, chains_all))
        interleave(issue(c0 + 2, ring_a, chains_all), consume(c0 + 1, ring_b, chains_all))

    def trip(j, carry):
        for pr in range(MLA_FAR_UNROLL):
            pair(2 * (MLA_FAR_UNROLL * j + pr))
        return carry

    lax.fori_loop(0, i // MLA_FAR_UNROLL, trip, 0)
    lax.fori_loop(i // MLA_FAR_UNROLL * MLA_FAR_UNROLL, i, lambda j, carry: (pair(2 * j), carry)[1], 0)
    upper = [c for c in chains_all if c[1] == 1]
    interleave(issue(2 * i + 1, ring_b, upper), consume(2 * i, ring_a, chains_all, diag_ql=0))
    interleave(consume(2 * i + 1, ring_b, upper, diag_ql=1))
    for (h, ql), (m_scr, l_scr, acc_scr) in state.items():
        o_t = acc_scr[...] * (1.0 / jnp.maximum(l_scr[...], 1e-30))
        o_ref[0, ql * sub:(ql + 1) * sub, h * MLA_V:(h + 1) * MLA_V] = o_t.T.astype(o_ref.dtype)


def _mla_attention(q_t, k, v_t, b, s):
    tq = MLA_T
    nq = s // tq
    hp = MLA_HPS
    nsub = tq // MLA_SUB
    sub_state = [pltpu.VMEM((1, MLA_SUB), F32), pltpu.VMEM((1, MLA_SUB), F32), pltpu.VMEM((MLA_V, MLA_SUB), F32)]
    ring = hp * nsub * [pltpu.VMEM((MLA_SUB, MLA_SUB), F32)]
    return pl.pallas_call(
        _mla_attn_body,
        out_shape=jax.ShapeDtypeStruct((b, s, MLA_HEADS * MLA_V), BF16),
        grid=(b, MLA_HEADS // hp, nq),
        in_specs=[pl.BlockSpec((1, hp, MLA_DQ, tq), lambda bb, h, i: (bb, h, 0, i)),
                  pl.BlockSpec((1, hp, s, MLA_DQ), lambda bb, h, i: (bb, h, 0, 0)),
                  pl.BlockSpec((1, hp, nq * nsub, MLA_V, MLA_SUB), lambda bb, h, i: (bb, h, 0, 0, 0))],
        out_specs=pl.BlockSpec((1, tq, hp * MLA_V), lambda bb, h, i: (bb, i, h)),
        scratch_shapes=2 * ring + hp * nsub * sub_state,
        compiler_params=_cparams(("parallel", "parallel", "arbitrary")),
        name="mla_attention",
    )(q_t, k, v_t)


def _mla_mixer(x2, b, s, w_in, q_norm, w_q_up, kv_norm, w_kv_up):
    r0 = MLA_Q_RANK + MLA_KV_RANK
    half = MLA_ROPE // 2
    w_in_ext = jnp.concatenate([w_in, w_in[:, r0 + half:r0 + MLA_ROPE], w_in[:, r0:r0 + half]], 1)
    lat = _linear(x2, w_in_ext.astype(BF16), tm=512, tn=w_in_ext.shape[1], out_dtype=F32, name="mla_in")
    wq = w_q_up.reshape(MLA_Q_RANK, MLA_HEADS, MLA_DQ)
    wr = wq[..., MLA_NOPE:]
    wq = jnp.concatenate([wq, wr[..., half:], wr[..., :half]], -1)
    wq_t = wq.reshape(MLA_Q_RANK, MLA_HEADS * MLA_QROWS).T.astype(BF16)
    wkv = w_kv_up.reshape(MLA_KV_RANK, MLA_HEADS, MLA_NOPE + MLA_V)
    wk = wkv[..., :MLA_NOPE].reshape(MLA_KV_RANK, MLA_HEADS * MLA_NOPE).astype(BF16)
    wv_t = wkv[..., MLA_NOPE:].reshape(MLA_KV_RANK, MLA_HEADS * MLA_V).T.astype(BF16)
    cosx, sinx = _rope_tables(s)
    q_t = _mla_q_proj(lat, q_norm, wq_t, cosx.T, sinx.T, b, s)
    k, v_t = _mla_kv_proj(lat, kv_norm, wk, wv_t, cosx, sinx, b, s)
    o = _mla_attention(q_t, k, v_t, b, s)
    return o.reshape(b * s, MLA_HEADS * MLA_V)


def _ffn_body(x_ref, wg_ref, wu_ref, wd_ref, lg_ref, lb_ref, o_ref, acc_scr):
    f = pl.program_id(1)
    xb = x_ref[...].astype(BF16)
    h = jax.nn.silu(_dot(xb, wg_ref[...])) * _dot(xb, wu_ref[...])
    part = _dot(h.astype(BF16), wd_ref[...])

    @pl.when(f == 0)
    def _():
        acc_scr[...] = part

    @pl.when(f > 0)
    def _():
        acc_scr[...] += part

    @pl.when(f == pl.num_programs(1) - 1)
    def _():
        o_ref[...] = _layer_norm(ALPHA * x_ref[...] + acc_scr[...], lg_ref[...], lb_ref[...])


def _ffn_ln(x, wg, wu, wd, lg, lb, *, tm=512, tf=1408):
    m, d = x.shape
    dff = wg.shape[1]
    assert dff % tf == 0
    vec = pl.BlockSpec((1, d), lambda i, f: (0, 0))
    return pl.pallas_call(
        _ffn_body,
        out_shape=jax.ShapeDtypeStruct((m, d), F32),
        grid=(m // tm, dff // tf),
        in_specs=[pl.BlockSpec((tm, d), lambda i, f: (i, 0)),
                  pl.BlockSpec((d, tf), lambda i, f: (0, f)),
                  pl.BlockSpec((d, tf), lambda i, f: (0, f)),
                  pl.BlockSpec((tf, d), lambda i, f: (f, 0)),
                  vec, vec],
        out_specs=pl.BlockSpec((tm, d), lambda i, f: (i, 0)),
        scratch_shapes=[pltpu.VMEM((tm, d), F32)],
        compiler_params=_cparams(("parallel", "arbitrary")),
        name="ffn_ln",
    )(x, wg, wu, wd, lg.reshape(1, d), lb.reshape(1, d))


def _compress_body(c_ref, pea_ref, peb_ref, w1a_ref, w1b_ref, w2_ref, o_ref):
    c = c_ref[0, 0, 0].astype(F32)
    ya = _dot((c + pea_ref[...]).astype(BF16), w1a_ref[...])
    yb = _dot((c + peb_ref[...]).astype(BF16), w1b_ref[...])
    yb_next = jnp.concatenate([yb[1:], jnp.zeros_like(yb[:1])], 0)
    h = jax.nn.gelu(ya + yb_next)
    o_ref[0] = _dot(h.astype(BF16), w2_ref[...]).astype(o_ref.dtype)


def _compress(chunks, kv, pe, w1, w2, dpad):
    b, g, _, nch, k = chunks.shape
    bg = b * g
    d = pe.shape[1]
    dh, dout = w2.shape
    half = CMP_LEN // 2
    assert half == CMP_STRIDE and k == CMP_STRIDE * dpad

    def pad_tok(a):
        a = jnp.pad(a, [(0, 0), (0, dpad - d)] + [(0, 0)] * (a.ndim - 2))
        return a[:half].reshape((k,) + a.shape[2:]), a[half:].reshape((k,) + a.shape[2:])

    pe_a, pe_b = pad_tok(pe)
    w1a, w1b = pad_tok(w1)
    full = lambda i: (0, 0)
    return pl.pallas_call(
        _compress_body,
        out_shape=jax.ShapeDtypeStruct((bg, nch, dout), BF16),
        grid=(bg,),
        in_specs=[pl.BlockSpec((1, 1, 1, nch, k), lambda i: (i // g, i % g, kv, 0, 0)),
                  pl.BlockSpec((1, k), full), pl.BlockSpec((1, k), full),
                  pl.BlockSpec((k, dh), full), pl.BlockSpec((k, dh), full),
                  pl.BlockSpec((dh, dout), full)],
        out_specs=pl.BlockSpec((1, nch, dout), lambda i: (i, 0, 0)),
        compiler_params=_cparams(("parallel",)),
        name="nsa_compress",
    )(chunks, pe_a.reshape(1, k), pe_b.reshape(1, k), w1a.astype(BF16), w1b.astype(BF16), w2.astype(BF16))


def _rel_bucket_np(dist):
    n = np.maximum(dist, 0)
    max_exact = REL_BUCKETS // 2
    nf = np.maximum(n, 1).astype(np.float32)
    large = max_exact + (np.log(nf / np.float32(max_exact)) / np.float32(math.log(REL_MAX_DIST / max_exact))
                         * np.float32(REL_BUCKETS - max_exact)).astype(np.int32)
    large = np.minimum(large, REL_BUCKETS - 1)
    return np.where(n < max_exact, n, large).astype(np.int32)


def _nsa_tables(rel_bias, s):
    g, hg, tq, kc = NSA_GROUPS, NSA_HG, NSA_TQ, NSA_KC
    assert tq == kc and tq % CMP_STRIDE == 0 and WINDOW == 2 * kc
    assert np.all(_rel_bucket_np(np.arange(tq // 2 - 15, s + tq)) == REL_BUCKETS - 1)
    rb = rel_bias.reshape(REL_BUCKETS, g, hg) * LOG2E

    def tile(base, step, rows, valid):
        p = tq + step * rows
        k = np.arange(p)
        k = np.where(k < p - step * (rows - 1), k, k - p)
        d = base + k
        vec = jnp.where(valid(d)[:, None, None], rb[_rel_bucket_np(d)], NEG)
        vec = vec.transpose(1, 2, 0)
        flat = jnp.tile(vec, (1, 1, rows))[..., :rows * (p - step)]
        mat = flat.reshape(g, hg, rows, p - step)[..., :tq]
        return mat.transpose(0, 2, 1, 3).reshape(g, rows, hg * tq)

    causal = lambda d: d >= 0
    far = jnp.broadcast_to(rb[REL_BUCKETS - 1][:, None, :, None], (g, 1, hg, tq)).reshape(g, 1, hg * tq)
    rel = lambda x: jnp.where(x > NEG_TEST, x - far, NEG)
    tiles = jnp.stack([rel(tile(0, 1, kc, causal)), rel(tile(tq, 1, kc, causal)),
                       rel(tile(2 * tq, 1, kc, lambda d: d < WINDOW)),
                       jnp.zeros((g, kc, hg * tq), F32)], 1)
    band = jnp.stack([rel(tile(8 * CMP_STRIDE - CMP_LEN + 1, CMP_STRIDE, NSA_BAND, causal)),
                      rel(tile(-(CMP_LEN - 1), CMP_STRIDE, NSA_BAND, causal))], 1)
    return tiles, band


def _overlap_t(nc_pad, nb):
    n = np.arange(nc_pad)[None, :]
    jb = np.arange(nb)[:, None]
    cstart = n * CMP_STRIDE
    cend = cstart + CMP_LEN - 1
    sstart = jb * SEL_BLOCK
    ov = (cstart <= sstart + SEL_BLOCK - 1) & (cend >= sstart) & (n < nc_pad - 1)
    return jnp.asarray(ov.astype(np.float32), BF16)


def _nsa_body(qt_ref, kc_ref, vct_ref, k_ref, vt_ref, gate_ref, tiles_ref,
              band_ref, ovt_ref, o_ref, s_scr, sel_scr, qa_scr, oc_scr, *scr, nb, nqt):
    t = pl.program_id(2)
    ring_a, ring_b, state = scr[:NSA_HG], scr[NSA_HG:2 * NSA_HG], scr[2 * NSA_HG:]
    nkc = k_ref.shape[3]
    L = NSA_LANES
    q_t = qt_ref[0, 0, 0]
    ncp = kc_ref.shape[2]
    per = NSA_KC // SEL_BLOCK
    blocks_per_tile = NSA_TQ // CMP_STRIDE

    n_slab = nb // NSA_SEL_ROWS

    n_win = jnp.minimum(t, 2) + 1
    n_slots = n_win + t + 1

    def slot_params(c):
        is_win = c < n_win
        j = c - n_win
        is_sel = jnp.logical_and(c >= n_win, j <= t)
        delta = t - j
        br = is_win.astype(jnp.int32)
        kidx = jnp.where(is_win, t - c, jnp.where(is_sel, j, 0))
        sidx = jnp.where(is_sel, j // (NSA_SEL_ROWS // per), jnp.where(is_win, 0, n_slab))
        tidx = jnp.where(is_win, c, jnp.where(jnp.logical_and(is_sel, delta < 2), delta, 3))
        return br, kidx, sidx, tidx

    def issue(c, ring, set_rows=True):
        br, kidx, sidx, _ = slot_params(c)
        k = k_ref[0, 0, br, kidx]
        if set_rows:
            qa_scr[NSA_DK:NSA_DK + NSA_SEL_ROWS, :] = sel_scr[sidx]
        for h in range(NSA_HG):
            ring[h][...] = _dot(k, qa_scr[:, h * NSA_TQ:(h + 1) * NSA_TQ])
            yield

    def consume(c, ring, near):
        br, kidx, _, tidx = slot_params(c)
        vt = vt_ref[0, 0, br, kidx]
        for h in range(NSA_HG):
            m_scr, acc_scr = state[2 * h:2 * h + 2]
            sc = ring[h][...]
            if near:
                sc = sc + tiles_ref[0, tidx, :, h * NSA_TQ:(h + 1) * NSA_TQ]
            m_old = m_scr[br]
            m_new = jnp.maximum(m_old, jnp.max(sc, 0, keepdims=True))
            pp = jnp.exp2(sc - m_new)
            acc_scr[br] = jnp.exp2(m_old - m_new) * acc_scr[br] + _dot(vt, pp.astype(BF16))
            m_scr[br] = m_new
            yield

    def interleave(*gens):
        for _ in zip(*gens):
            pass

    for h in range(NSA_HG):
        m_scr, acc_scr = state[2 * h:2 * h + 2]
        m_scr[...] = jnp.full_like(m_scr, NEG)
        acc_scr[...] = jnp.zeros_like(acc_scr)
    qa_scr[...] = q_t

    def compress_and_select(nrows, nblk):
        s = _dot(kc_ref[0, 0, :nrows], q_t)
        first = (t == 0).astype(jnp.int32)
        bs = pl.multiple_of((blocks_per_tile * t - 8) * (1 - first), 8)
        row = lax.broadcasted_iota(jnp.int32, (nrows, L), 0)
        s_scr[:nrows] = jnp.where(row < bs + NSA_BAND, s, NEG)
        s_scr[pl.ds(bs, NSA_BAND), :] += band_ref[0, first]
        s = s_scr[:nrows]
        m = jnp.max(s, 0, keepdims=True)
        m = jnp.where(m < NEG_TEST, 0.0, m)
        p = jnp.exp2(s - m)
        den = jnp.maximum(jnp.sum(p, 0, keepdims=True), 1e-30)
        p = p * (1.0 / den)
        oc_scr[...] = _dot(vct_ref[0, 0, :, :nrows], p.astype(BF16))

        psum = p[:, 0:NSA_TQ]
        for h in range(1, NSA_HG):
            psum = psum + p[:, h * NSA_TQ:(h + 1) * NSA_TQ]
        p1, p2, p3 = _split3(psum)
        ovt = ovt_ref[:nblk, :nrows]
        imp = _dot(ovt, p1) + _dot(ovt, p2) + _dot(ovt, p3)
        interleave(issue(0, ring_a, set_rows=False))
        blk = lax.broadcasted_iota(jnp.int32, (nblk, NSA_TQ), 0)
        lane = lax.broadcasted_iota(jnp.int32, (nblk, NSA_TQ), 1)
        cur = (NSA_TQ // SEL_BLOCK) * t + lane // SEL_BLOCK
        forced = (blk == 0) | (blk == cur) | (blk == cur - 1)
        v = jnp.where(blk > cur, -FORCE, jnp.where(forced, -jnp.inf, imp))
        blk_f = blk.astype(F32)
        sel = jnp.where(forced, 1.0, 0.0)
        for _ in range(min(SEL_TOPN, nblk) - 3):
            mx = jnp.max(v, 0, keepdims=True)
            idx = jnp.min(jnp.where(v == mx, blk_f, float(nblk)), 0, keepdims=True)
            hit = blk_f == idx
            sel = jnp.where(hit, 1.0, sel)
            v = jnp.where(hit, -jnp.inf, v)
        selneg = jnp.where(sel > 0.5, 0.0, NEG)
        selneg = jnp.concatenate([selneg] * NSA_HG, 1).astype(BF16)
        for u in range(nblk // NSA_SEL_ROWS):
            sel_scr[u] = selneg[NSA_SEL_ROWS * u:NSA_SEL_ROWS * (u + 1), :]

    n_var = -(-nqt // NSA_TILES_PER_VARIANT)
    for i in range(n_var):
        nrows = min(ncp, (i + 1) * NSA_TILES_PER_VARIANT * blocks_per_tile)
        nblk = min(nb, (i + 1) * NSA_TILES_PER_VARIANT * NSA_TQ // SEL_BLOCK)

        @pl.when(t // NSA_TILES_PER_VARIANT == i)
        def _(nrows=nrows, nblk=nblk):
            compress_and_select(nrows, nblk)

    sel_scr[n_slab] = jnp.full((NSA_SEL_ROWS, L), NEG, BF16)

    def make_trip(first_slot, near, pairs=1):
        def trip(i, carry):
            for pr in range(pairs):
                c0 = first_slot + 2 * (pairs * i + pr)
                interleave(issue(c0 + 1, ring_b), consume(c0, ring_a, near))
                interleave(issue(c0 + 2, ring_a), consume(c0 + 1, ring_b, near))
            return carry
        return trip

    far_pairs = jnp.maximum(n_slots - 2 - NSA_HEAD_SLOTS, 0) // 2
    far_trips = far_pairs // NSA_FAR_UNROLL
    rest_first = NSA_HEAD_SLOTS + 2 * NSA_FAR_UNROLL * far_trips
    tail_first = NSA_HEAD_SLOTS + 2 * far_pairs
    tail_trips = (jnp.maximum(n_slots - tail_first, 0) + 1) // 2
    make_trip(0, True, NSA_HEAD_SLOTS // 2)(0, 0)
    lax.fori_loop(0, far_trips, make_trip(NSA_HEAD_SLOTS, False, NSA_FAR_UNROLL), 0)
    lax.fori_loop(0, far_pairs - NSA_FAR_UNROLL * far_trips, make_trip(rest_first, False), 0)
    lax.fori_loop(0, tail_trips, make_trip(tail_first, True), 0)

    gate = gate_ref[0, 0, 0]
    outs = []
    for h in range(NSA_HG):
        sl = slice(h * NSA_TQ, (h + 1) * NSA_TQ)
        acc_scr = state[2 * h + 1]
        o_s = acc_scr[0, :NSA_DV] * (1.0 / jnp.maximum(acc_scr[0, NSA_DV:NSA_DV + 1], 1e-30))
        o_w = acc_scr[1, :NSA_DV] * (1.0 / jnp.maximum(acc_scr[1, NSA_DV:NSA_DV + 1], 1e-30))
        outs.append(gate[0:1, sl] * oc_scr[:, sl] + gate[1:2, sl] * o_s + gate[2:3, sl] * o_w)
    o_ref[0] = jnp.concatenate(outs, 0).T.astype(o_ref.dtype)


def _nsa_attention(q_t, kcmp, vcmp_t, k, v_t, gate, tiles, band, ovt):
    b, g, nqt = q_t.shape[:3]
    ncp = kcmp.shape[2]
    nkc = k.shape[3]
    nb = ovt.shape[0]
    L = NSA_LANES
    body = functools.partial(_nsa_body, nb=nb, nqt=nqt)
    ring = NSA_HG * [pltpu.VMEM((NSA_KC, NSA_TQ), F32)]
    head_state = [pltpu.VMEM((2, 1, NSA_TQ), F32), pltpu.VMEM((2, NSA_VROWS, NSA_TQ), F32)]
    grp = lambda bb, gg, t: (bb, gg, 0, 0)
    grp6 = lambda bb, gg, t: (bb, gg, 0, 0, 0, 0)
    return pl.pallas_call(
        body,
        out_shape=jax.ShapeDtypeStruct((b, nqt * NSA_TQ, g * NSA_HG * NSA_DV), BF16),
        grid=(b, g, nqt),
        in_specs=[pl.BlockSpec((1, 1, 1, NSA_DKP, L), lambda bb, gg, t: (bb, gg, t, 0, 0)),
                  pl.BlockSpec((1, 1, ncp, NSA_DKP), grp),
                  pl.BlockSpec((1, 1, NSA_DV, ncp), grp),
                  pl.BlockSpec((1, 1, 2, nkc, NSA_KC, NSA_DKP), grp6),
                  pl.BlockSpec((1, 1, 2, nkc, NSA_VROWS, NSA_KC), grp6),
                  pl.BlockSpec((1, 1, 1, N_BRANCH, L), lambda bb, gg, t: (bb, gg, t, 0, 0)),
                  pl.BlockSpec((1, 4, NSA_KC, L), lambda bb, gg, t: (gg, 0, 0, 0)),
                  pl.BlockSpec((1, 2, NSA_BAND, L), lambda bb, gg, t: (gg, 0, 0, 0)),
                  pl.BlockSpec(ovt.shape, lambda bb, gg, t: (0, 0))],
        out_specs=pl.BlockSpec((1, NSA_TQ, NSA_HG * NSA_DV), lambda bb, gg, t: (bb, t, gg)),
        scratch_shapes=[pltpu.VMEM((ncp, L), F32),
                        pltpu.VMEM((nb // NSA_SEL_ROWS + 1, NSA_SEL_ROWS, L), BF16),
                        pltpu.VMEM((NSA_DKP, L), BF16),
                        pltpu.VMEM((NSA_DV, L), F32)]
        + 2 * ring + NSA_HG * head_state,
        compiler_params=_cparams(("parallel", "parallel", "arbitrary")),
        name="nsa_attention",
    )(q_t, kcmp, vcmp_t, k, v_t, gate, tiles, band, ovt)


def _nsa_proj_body(x_ref, wq_ref, wk_ref, wv_ref, wg_ref, wc_ref,
                   q_ref, k_ref, v_ref, gate_ref, c_ref, ctok_scr, *, qscale):
    g, hg, tq = NSA_GROUPS, NSA_HG, NSA_TQ
    xb = x_ref[...].astype(BF16)
    q_t = _dot_nt(wq_ref[...], xb) * qscale
    gate_t = jax.nn.sigmoid(_dot_nt(wg_ref[...], xb))
    for gg in range(g):
        for h in range(hg):
            head = gg * hg + h
            q_ref[0, gg, 0, :, h * tq:(h + 1) * tq] = q_t[head * NSA_DKP:(head + 1) * NSA_DKP].astype(BF16)
            r0 = head * NSA_GATE_ROWS
            gate_ref[0, gg, 0, :, h * tq:(h + 1) * tq] = gate_t[r0:r0 + N_BRANCH]
    k = _dot(xb, wk_ref[...])
    v_t = _dot_nt(wv_ref[...], xb).astype(BF16)
    per = tq // SEL_BLOCK
    chunk = pl.program_id(0) % (NSA_SEL_ROWS // per)
    row = lax.broadcasted_iota(jnp.int32, (tq, NSA_DKP), 0)
    col = lax.broadcasted_iota(jnp.int32, (tq, NSA_DKP), 1)
    blk_flag = jnp.where(col - NSA_DK == per * chunk + row // SEL_BLOCK, 1.0, 0.0)
    ones_rows = jnp.where(lax.broadcasted_iota(jnp.int32, (NSA_VROWS - NSA_DV, tq), 0) == 0, 1.0, 0.0).astype(BF16)
    for gg in range(g):
        for br in range(2):
            kb = k[:, (br * g + gg) * NSA_DKP:(br * g + gg + 1) * NSA_DKP]
            if br == 0:
                kb = kb + blk_flag
            k_ref[0, gg, br, 0] = kb.astype(BF16)
            v_ref[0, gg, br, 0, :NSA_DV, :] = v_t[(br * g + gg) * NSA_DV:(br * g + gg + 1) * NSA_DV]
            v_ref[0, gg, br, 0, NSA_DV:, :] = ones_rows
    ctok = _dot(xb, wc_ref[...])
    for gg in range(g):
        for kv in range(2):
            j = kv * g + gg
            ctok_scr[j] = ctok[:, j * NSA_DKP:(j + 1) * NSA_DKP]
            for l in range(CMP_STRIDE):
                rows = ctok_scr[j, pl.ds(l, tq // CMP_STRIDE, stride=CMP_STRIDE), :]
                c_ref[0, gg, kv, :, l * NSA_DKP:(l + 1) * NSA_DKP] = rows.astype(BF16)


def _nsa_proj(x2, b, s, wq_t, wk, wv_t, wg_t, wc):
    g, hg, tq, L = NSA_GROUPS, NSA_HG, NSA_TQ, NSA_LANES
    nqt = s // tq
    assert nqt % (NSA_SEL_ROWS * SEL_BLOCK // tq) == 0 and (s // SEL_BLOCK) % NSA_SEL_ROWS == 0
    t, d = x2.shape
    full = lambda i: (0, 0)
    tile5 = lambda i: (i // nqt, 0, i % nqt, 0, 0)
    tile6 = lambda i: (i // nqt, 0, 0, i % nqt, 0, 0)
    body = functools.partial(_nsa_proj_body, qscale=(NSA_DK ** -0.5) * LOG2E)
    return pl.pallas_call(
        body,
        out_shape=(jax.ShapeDtypeStruct((b, g, nqt, NSA_DKP, L), BF16),
                   jax.ShapeDtypeStruct((b, g, 2, nqt, NSA_KC, NSA_DKP), BF16),
                   jax.ShapeDtypeStruct((b, g, 2, nqt, NSA_VROWS, NSA_KC), BF16),
                   jax.ShapeDtypeStruct((b, g, nqt, N_BRANCH, L), F32),
                   jax.ShapeDtypeStruct((b, g, 2, s // CMP_STRIDE, CMP_STRIDE * NSA_DKP), BF16)),
        grid=(t // tq,),
        in_specs=[pl.BlockSpec((tq, d), lambda i: (i, 0)),
                  pl.BlockSpec(wq_t.shape, full), pl.BlockSpec(wk.shape, full), pl.BlockSpec(wv_t.shape, full),
                  pl.BlockSpec(wg_t.shape, full), pl.BlockSpec(wc.shape, full)],
        out_specs=(pl.BlockSpec((1, g, 1, NSA_DKP, L), tile5),
                   pl.BlockSpec((1, g, 2, 1, NSA_KC, NSA_DKP), tile6),
                   pl.BlockSpec((1, g, 2, 1, NSA_VROWS, NSA_KC), tile6),
                   pl.BlockSpec((1, g, 1, N_BRANCH, L), tile5),
                   pl.BlockSpec((1, g, 2, tq // CMP_STRIDE, CMP_STRIDE * NSA_DKP),
                                lambda i: (i // nqt, 0, 0, i % nqt, 0))),
        scratch_shapes=[pltpu.VMEM((wc.shape[1] // NSA_DKP, tq, NSA_DKP), F32)],
        compiler_params=_cparams(("parallel",)),
        name="nsa_proj",
    )(x2, wq_t, wk, wv_t, wg_t, wc)


def _nsa_mixer(x2, b, s, w_in, pe_k, w1_k, w2_k, pe_v, w1_v, w2_v, rel_bias):
    assert NSA_TQ == NSA_KC
    t, d = x2.shape
    h, g, hg, dk, dv = NSA_HEADS, NSA_GROUPS, NSA_HG, NSA_DK, NSA_DV
    nb = s // SEL_BLOCK
    sizes = [h * dk, g * dk, g * dv, g * dk, g * dv, g * dk, g * dv, h * N_BRANCH]
    c = [0] + [int(v) for v in np.cumsum(sizes)]
    cols = [w_in[:, c[i]:c[i + 1]] for i in range(len(sizes))]
    w_q, w_kc, w_vc, w_ks, w_vs, w_kw, w_vw, w_gate = cols

    def pad_last(a, n):
        return jnp.pad(a, [(0, 0)] * (a.ndim - 1) + [(0, n - a.shape[-1])])

    wq_t = pad_last(w_q.reshape(d, h, dk), NSA_DKP).reshape(d, h * NSA_DKP).T.astype(BF16)
    wk = jnp.concatenate([pad_last(w.reshape(d, g, dk), NSA_DKP).reshape(d, g * NSA_DKP)
                          for w in (w_ks, w_kw)], 1).astype(BF16)
    wv_t = jnp.concatenate([w_vs, w_vw], 1).T.astype(BF16)
    wg_t = pad_last(w_gate.reshape(d, h, N_BRANCH), NSA_GATE_ROWS).reshape(d, h * NSA_GATE_ROWS).T.astype(BF16)
    wc = jnp.concatenate([pad_last(w_kc.reshape(d, g, dk), NSA_DKP).reshape(d, g * NSA_DKP),
                          pad_last(w_vc.reshape(d, g, dv), NSA_DKP).reshape(d, g * NSA_DKP)], 1).astype(BF16)
    q_t, k, v_t, gate, ctok = _nsa_proj(x2, b, s, wq_t, wk, wv_t, wg_t, wc)

    nch = s // CMP_STRIDE
    k_cmp = _compress(ctok, 0, pe_k, w1_k, pad_last(w2_k, NSA_DKP), NSA_DKP)
    v_cmp = _compress(ctok, 1, pe_v, w1_v, w2_v, NSA_DKP)
    k_cmp = k_cmp.reshape(b, g, nch, NSA_DKP)
    vcmp_t = v_cmp.reshape(b, g, nch, dv).transpose(0, 1, 3, 2)

    tiles, band = _nsa_tables(rel_bias, s)
    ovt = _overlap_t(nch, nb)
    o = _nsa_attention(q_t, k_cmp, vcmp_t, k, v_t, gate, tiles, band, ovt)
    return o.reshape(t, h * dv)


def _out_ln_router_body(o_ref, w_ref, res_ref, lg_ref, lb_ref, wr_ref, x_ref, xb_ref, logit_ref):
    y = _layer_norm(ALPHA * res_ref[...] + _dot(o_ref[...], w_ref[...]), lg_ref[...], lb_ref[...])
    x_ref[...] = y
    xb_ref[...] = y.astype(BF16)
    x1, x2, _ = _split3(y)
    w1, w2, _ = _split3(wr_ref[...])
    logit_ref[...] = _dot(x1, w1) + (_dot(x1, w2) + _dot(x2, w1))


def _out_ln_router(o, w_out, res, lg, lb, w_router, *, tm=512):
    m, k = o.shape
    n = w_out.shape[1]
    ne = w_router.shape[1]
    wr = jnp.pad(w_router, ((0, 0), (0, 128 - ne)))
    row = lambda i: (i, 0)
    full = lambda i: (0, 0)
    x, xb, logits = pl.pallas_call(
        _out_ln_router_body,
        out_shape=(jax.ShapeDtypeStruct((m, n), F32), jax.ShapeDtypeStruct((m, n), BF16),
                   jax.ShapeDtypeStruct((m, 128), F32)),
        grid=(m // tm,),
        in_specs=[pl.BlockSpec((tm, k), row), pl.BlockSpec((k, n), full), pl.BlockSpec((tm, n), row),
                  pl.BlockSpec((1, n), full), pl.BlockSpec((1, n), full), pl.BlockSpec((n, 128), full)],
        out_specs=(pl.BlockSpec((tm, n), row), pl.BlockSpec((tm, n), row), pl.BlockSpec((tm, 128), row)),
        compiler_params=_cparams(("parallel",)),
        name="nsa_out_ln_router",
    )(o, w_out, res, lg.reshape(1, n), lb.reshape(1, n), wr)
    return x, xb, logits[:, :ne]


def _dispatch_body(ir_ref, ic_ref, fl_ref, x_ref, rt_ref, o_ref):
    i = pl.program_id(0)
    flag = fl_ref[i]
    tok = ic_ref[i] * MOE_TC + lax.broadcasted_iota(jnp.int32, (MOE_BLK, MOE_TC), 1)
    onehot = jnp.where(rt_ref[...] == tok, 1.0, 0.0).astype(BF16)
    rows = _dot(onehot, x_ref[...])

    @pl.when(flag == 3)
    def _():
        o_ref[...] = rows.astype(o_ref.dtype)

    @pl.when(flag == 1)
    def _():
        o_ref[...] = (o_ref[...].astype(F32) + rows).astype(o_ref.dtype)


def _dispatch(x_bf, row_tok_col, item_r, item_c, flags):
    t, d = x_bf.shape
    r = row_tok_col.shape[0]
    ni = item_r.shape[0]
    gs = pltpu.PrefetchScalarGridSpec(
        num_scalar_prefetch=3, grid=(ni,),
        in_specs=[pl.BlockSpec((MOE_TC, d), lambda i, ir, ic, fl: (ic[i], 0)),
                  pl.BlockSpec((MOE_BLK, 1), lambda i, ir, ic, fl: (ir[i], 0))],
        out_specs=pl.BlockSpec((MOE_BLK, d), lambda i, ir, ic, fl: (ir[i], 0)))
    return pl.pallas_call(
        _dispatch_body, grid_spec=gs,
        out_shape=jax.ShapeDtypeStruct((r, d), BF16),
        compiler_params=_cparams(("arbitrary",)),
        name="moe_dispatch",
    )(item_r, item_c, flags, x_bf, row_tok_col)


def _expert_body(be_ref, live_ref, x_ref, wg_ref, wu_ref, wd_ref, rw_ref, o_ref, acc_scr):
    f = pl.program_id(1)
    live = live_ref[pl.program_id(0)] == 1
    last = f == pl.num_programs(1) - 1

    @pl.when(live)
    def _():
        xb = x_ref[...]
        h = jax.nn.silu(_dot(xb, wg_ref[0])) * _dot(xb, wu_ref[0])
        part = _dot(h.astype(BF16), wd_ref[0])

        @pl.when(f == 0)
        def _():
            acc_scr[...] = part

        @pl.when(f > 0)
        def _():
            acc_scr[...] += part

        @pl.when(last)
        def _():
            o_ref[...] = (acc_scr[...] * rw_ref[...]).astype(o_ref.dtype)

    @pl.when(jnp.logical_and(last, jnp.logical_not(live)))
    def _():
        o_ref[...] = jnp.zeros_like(o_ref)


def _experts(xs, wg, wu, wd, row_w_col, block_expert, block_live):
    r, d = xs.shape
    nbk = r // MOE_BLK
    nf = wg.shape[2] // MOE_TF
    ftile = lambda i, f, live: jnp.where(live[i] == 1, f, nf - 1)
    gs = pltpu.PrefetchScalarGridSpec(
        num_scalar_prefetch=2, grid=(nbk, nf),
        in_specs=[pl.BlockSpec((MOE_BLK, d), lambda i, f, be, live: (i, 0)),
                  pl.BlockSpec((1, d, MOE_TF), lambda i, f, be, live: (be[i], 0, ftile(i, f, live))),
                  pl.BlockSpec((1, d, MOE_TF), lambda i, f, be, live: (be[i], 0, ftile(i, f, live))),
                  pl.BlockSpec((1, MOE_TF, d), lambda i, f, be, live: (be[i], ftile(i, f, live), 0)),
                  pl.BlockSpec((MOE_BLK, 1), lambda i, f, be, live: (i, 0))],
        out_specs=pl.BlockSpec((MOE_BLK, d), lambda i, f, be, live: (i, 0)),
        scratch_shapes=[pltpu.VMEM((MOE_BLK, d), F32)])
    return pl.pallas_call(
        _expert_body, grid_spec=gs,
        out_shape=jax.ShapeDtypeStruct((r, d), BF16),
        compiler_params=_cparams(("parallel", "arbitrary")),
        name="moe_experts",
    )(block_expert, block_live, xs, wg, wu, wd, row_w_col)


def _combine_body(ir_ref, ic_ref, fl_ref, y_ref, rt_ref, x_ref, g_ref, b_ref, o_ref):
    i = pl.program_id(0)
    flag = fl_ref[i]
    tok = ic_ref[i] * MOE_TC + lax.broadcasted_iota(jnp.int32, (MOE_TC, MOE_BLK), 0)
    onehot = jnp.where(rt_ref[0] == tok, 1.0, 0.0).astype(BF16)
    part = _dot(onehot, y_ref[...])

    @pl.when((flag & 3) == 3)
    def _():
        o_ref[...] = part

    @pl.when((flag & 3) == 1)
    def _():
        o_ref[...] += part

    @pl.when((flag & 4) == 4)
    def _():
        o_ref[...] = _layer_norm(ALPHA * x_ref[...] + o_ref[...], g_ref[...], b_ref[...])


def _combine_ln(out_rows, row_tok_lane, item_r, item_c, flags, x, g, b):
    r, d = out_rows.shape
    t = x.shape[0]
    ni = item_r.shape[0]
    vec = pl.BlockSpec((1, d), lambda i, ir, ic, fl: (0, 0))
    gs = pltpu.PrefetchScalarGridSpec(
        num_scalar_prefetch=3, grid=(ni,),
        in_specs=[pl.BlockSpec((MOE_BLK, d), lambda i, ir, ic, fl: (ir[i], 0)),
                  pl.BlockSpec((1, 1, MOE_BLK), lambda i, ir, ic, fl: (ir[i], 0, 0)),
                  pl.BlockSpec((MOE_TC, d), lambda i, ir, ic, fl: (ic[i], 0)),
                  vec, vec],
        out_specs=pl.BlockSpec((MOE_TC, d), lambda i, ir, ic, fl: (ic[i], 0)))
    return pl.pallas_call(
        _combine_body, grid_spec=gs,
        out_shape=jax.ShapeDtypeStruct((t, d), F32),
        compiler_params=_cparams(("arbitrary",)),
        name="moe_combine_ln",
    )(item_r, item_c, flags, out_rows, row_tok_lane, x, g.reshape(1, d), b.reshape(1, d))


def _moe_plan(top_idx, wts, t):
    e, blk, tc = N_EXPERTS, MOE_BLK, MOE_TC
    a = t * TOP_K
    i32 = jnp.int32
    exp_flat = top_idx.reshape(a).astype(i32)
    tok_flat = jnp.arange(a, dtype=i32) // TOP_K
    _, tok_sorted, w_sorted = lax.sort((exp_flat, tok_flat, wts.reshape(a)), num_keys=1, is_stable=True)
    counts = jnp.sum((exp_flat[:, None] == jnp.arange(e, dtype=i32)[None, :]).astype(i32), 0)
    padded = ((counts + blk - 1) // blk) * blk
    grp_start = jnp.cumsum(counts) - counts
    pad_end = jnp.cumsum(padded)
    pad_start = pad_end - padded
    nbk = a // blk + e
    r = nbk * blk
    tok_ext = jnp.concatenate([tok_sorted, jnp.full((r - a,), -1, i32)])
    w_ext = jnp.concatenate([w_sorted, jnp.zeros((r - a,), F32)])
    rows = jnp.arange(r, dtype=i32)
    row_tok = jnp.full((r,), -1, i32)
    row_w = jnp.zeros((r,), F32)
    for ee in range(e):
        inside = (rows >= pad_start[ee]) & (rows < pad_start[ee] + counts[ee])
        shift = pad_start[ee] - grp_start[ee]
        row_tok = jnp.where(inside, jnp.roll(tok_ext, shift), row_tok)
        row_w = jnp.where(inside, jnp.roll(w_ext, shift), row_w)
    blk_first = jnp.arange(nbk, dtype=i32) * blk
    block_expert = jnp.minimum(jnp.sum((pad_end[None, :] <= blk_first[:, None]).astype(i32), 1), e - 1)

    rt = row_tok.reshape(nbk, blk)
    valid = rt >= 0
    t_lo = jnp.min(jnp.where(valid, rt, t), 1)
    t_hi = jnp.max(rt, 1)
    has = t_hi >= 0
    c_lo = jnp.where(has, t_lo // tc, 0)
    c_hi = jnp.where(has, t_hi // tc, 0)
    n_it = c_hi - c_lo + 1
    off_end = jnp.cumsum(n_it)
    off_start = off_end - n_it
    total = off_end[-1]
    ni = nbk + e * (t // tc)
    idx = jnp.arange(ni, dtype=i32)
    ok = idx < total
    ir = jnp.minimum(jnp.sum((off_end[None, :] <= idx[:, None]).astype(i32), 1), nbk - 1)
    ic = jnp.where(ok, c_lo[ir] + idx - off_start[ir], c_hi[nbk - 1]).astype(i32)
    first = ok & (idx == off_start[ir])
    d_flags = ok.astype(i32) + 2 * first.astype(i32)

    key = jnp.where(ok, ic * nbk + ir, jnp.iinfo(jnp.int32).max)
    perm = jnp.argsort(key)
    ok2 = ok[perm]
    last = total - 1
    cr = jnp.where(ok2, ir[perm], ir[perm][last]).astype(i32)
    cc = jnp.where(ok2, ic[perm], ic[perm][last]).astype(i32)
    first2 = ok2 & jnp.concatenate([jnp.ones((1,), bool), cc[1:] != cc[:-1]])
    last2 = ok2 & jnp.concatenate([(cc[1:] != cc[:-1]) | ~ok2[1:], jnp.ones((1,), bool)])
    c_flags = ok2.astype(i32) + 2 * first2.astype(i32) + 4 * last2.astype(i32)
    return dict(row_tok=row_tok, row_w=row_w, block_expert=block_expert, block_live=has.astype(i32),
                d_items=(ir, ic, d_flags), c_items=(cr, cc, c_flags), nbk=nbk)


def _moe_ln(x2, x_bf, logits, wg, wu, wd, ln_g, ln_b):
    t, d = x2.shape
    top_val, top_idx = lax.top_k(logits, TOP_K)
    wts = jax.nn.softmax(top_val, -1)
    plan = _moe_plan(top_idx, wts, t)
    nbk = plan["nbk"]
    xs = _dispatch(x_bf, plan["row_tok"].reshape(-1, 1), *plan["d_items"])
    out_rows = _experts(xs, wg.astype(BF16), wu.astype(BF16), wd.astype(BF16),
                        plan["row_w"].reshape(-1, 1), plan["block_expert"], plan["block_live"])
    return _combine_ln(out_rows, plan["row_tok"].reshape(nbk, 1, MOE_BLK), *plan["c_items"], x2, ln_g, ln_b)


def _forward(x, mla_w_in, mla_q_norm, mla_w_q_up, mla_kv_norm, mla_w_kv_up, mla_w_out, nsa_w_in,
             nsa_cmp_pe_k, nsa_cmp_w1_k, nsa_cmp_w2_k, nsa_cmp_pe_v, nsa_cmp_w1_v, nsa_cmp_w2_v,
             nsa_w_out, rel_bias, ffn_w_gate, ffn_w_up, ffn_w_down, moe_w_router, moe_w_gate,
             moe_w_up, moe_w_down, ln_mix_g, ln_mix_b, ln_ffn_g, ln_ffn_b):
    b, s, d = x.shape
    x2 = x.reshape(b * s, d)
    o = _mla_mixer(x2, b, s, mla_w_in[0], mla_q_norm[0], mla_w_q_up[0], mla_kv_norm[0], mla_w_kv_up[0])
    x2 = _linear(o, mla_w_out[0].astype(BF16), tm=1024, tn=d, out_dtype=F32,
                 ln=(x2, ln_mix_g[0], ln_mix_b[0]), name="mla_out_ln")
    x2 = _ffn_ln(x2, ffn_w_gate[0].astype(BF16), ffn_w_up[0].astype(BF16), ffn_w_down[0].astype(BF16),
                 ln_ffn_g[0], ln_ffn_b[0])
    o = _nsa_mixer(x2, b, s, nsa_w_in[0], nsa_cmp_pe_k[0], nsa_cmp_w1_k[0], nsa_cmp_w2_k[0],
                   nsa_cmp_pe_v[0], nsa_cmp_w1_v[0], nsa_cmp_w2_v[0], rel_bias)
    x2, x_bf, logits = _out_ln_router(o, nsa_w_out[0].astype(BF16), x2, ln_mix_g[1], ln_mix_b[1],
                                      moe_w_router[0])
    x2 = _moe_ln(x2, x_bf, logits, moe_w_gate[0], moe_w_up[0], moe_w_down[0], ln_ffn_g[1], ln_ffn_b[1])
    return x2.reshape(b, s, d)


@jax.jit
def kernel(x, mla_w_in, mla_q_norm, mla_w_q_up, mla_kv_norm, mla_w_kv_up, mla_w_out, nsa_w_in,
           nsa_cmp_pe_k, nsa_cmp_w1_k, nsa_cmp_w2_k, nsa_cmp_pe_v, nsa_cmp_w1_v, nsa_cmp_w2_v,
           nsa_w_out, rel_bias, ffn_w_gate, ffn_w_up, ffn_w_down, moe_w_router, moe_w_gate,
           moe_w_up, moe_w_down, ln_mix_g, ln_mix_b, ln_ffn_g, ln_ffn_b):
    return _forward(x, mla_w_in, mla_q_norm, mla_w_q_up, mla_kv_norm, mla_w_kv_up, mla_w_out, nsa_w_in,
                    nsa_cmp_pe_k, nsa_cmp_w1_k, nsa_cmp_w2_k, nsa_cmp_pe_v, nsa_cmp_w1_v, nsa_cmp_w2_v,
                    nsa_w_out, rel_bias, ffn_w_gate, ffn_w_up, ffn_w_down, moe_w_router, moe_w_gate,
                    moe_w_up, moe_w_down, ln_mix_g, ln_mix_b, ln_ffn_g, ln_ffn_b)
```

```python
import functools
import math

import numpy as np
import jax
import jax.numpy as jnp
from jax import lax
from jax.experimental import pallas as pl
from jax.experimental.pallas import tpu as pltpu

F32 = jnp.float32
BF16 = jnp.bfloat16

D_MODEL = 1024
DEPTH = 2

MLA_HEADS = 8
MLA_Q_RANK = 512
MLA_KV_RANK = 256
MLA_NOPE = 128
MLA_ROPE = 64
MLA_V = 128
ROPE_THETA = 10000.0

NSA_HEADS = 16
NSA_GROUPS = 4
NSA_HG = NSA_HEADS // NSA_GROUPS
NSA_DK = 96
NSA_DV = 64
CMP_LEN = 32
CMP_STRIDE = 16
SEL_BLOCK = 64
SEL_TOPN = 16
WINDOW = 512
N_BRANCH = 3
FORCE = 1e6

REL_BUCKETS = 32
REL_MAX_DIST = 128

D_FF = 2816
N_EXPERTS = 8
TOP_K = 2
D_FF_EXPERT = 3584

LN_EPS = 1e-5
RMS_EPS = 1e-6

ALPHA = (2.0 * DEPTH) ** 0.25

NEG = -1e30
NEG_TEST = -1e29

V7X_VMEM_LIMIT = 56 * 1024 * 1024

LOG2E = 1.4426950408889634

NSA_TQ = 256
NSA_LANES = NSA_HG * NSA_TQ
NSA_KC = 256
NSA_DKP = 128
NSA_GATE_ROWS = 8
NSA_BAND = 24
NSA_SEL_ROWS = 16
NSA_VROWS = NSA_DV + 16
NSA_HEAD_SLOTS = 4
NSA_FAR_UNROLL = 4
NSA_TILES_PER_VARIANT = 4

MOE_BLK = 512
MOE_TC = 512
MOE_TF = 1792


def _cparams(sem, vmem=V7X_VMEM_LIMIT):
    return pltpu.CompilerParams(dimension_semantics=sem, vmem_limit_bytes=vmem)


def _layer_norm(r, g, b):
    mu = jnp.mean(r, -1, keepdims=True)
    d = r - mu
    var = jnp.mean(d * d, -1, keepdims=True)
    return d * lax.rsqrt(var + LN_EPS) * g + b


def _rms_norm(x, g):
    return x * lax.rsqrt(jnp.mean(x * x, -1, keepdims=True) + RMS_EPS) * g


def _split3(a):
    a1 = a.astype(BF16)
    r1 = a - a1.astype(F32)
    a2 = r1.astype(BF16)
    a3 = (r1 - a2.astype(F32)).astype(BF16)
    return a1, a2, a3


def _dot(a, b):
    return jnp.dot(a, b, preferred_element_type=F32)


def _dot_nt(a, b):
    return lax.dot_general(a, b, (((1,), (1,)), ((), ())), preferred_element_type=F32)


def _linear_body(*refs, has_ln):
    it = iter(refs)
    x_ref = next(it)
    w_ref = next(it)
    if has_ln:
        res_ref, lg_ref, lb_ref = next(it), next(it), next(it)
    o_ref = next(it)
    acc = _dot(x_ref[...].astype(BF16), w_ref[...])
    if has_ln:
        acc = _layer_norm(ALPHA * res_ref[...] + acc, lg_ref[...], lb_ref[...])
    o_ref[...] = acc.astype(o_ref.dtype)


def _linear(x, w, *, tm, tn, out_dtype, ln=None, name):
    m, k = x.shape
    n = w.shape[1]
    assert m % tm == 0 and n % tn == 0
    in_specs = [pl.BlockSpec((tm, k), lambda i, j: (i, 0)),
                pl.BlockSpec((k, tn), lambda i, j: (0, j))]
    args = [x, w]
    if ln is not None:
        assert tn == n
        res, lg, lb = ln
        in_specs += [pl.BlockSpec((tm, n), lambda i, j: (i, 0)),
                     pl.BlockSpec((1, n), lambda i, j: (0, 0)),
                     pl.BlockSpec((1, n), lambda i, j: (0, 0))]
        args += [res, lg.reshape(1, n), lb.reshape(1, n)]
    return pl.pallas_call(
        functools.partial(_linear_body, has_ln=ln is not None),
        out_shape=jax.ShapeDtypeStruct((m, n), out_dtype),
        grid=(m // tm, n // tn),
        in_specs=in_specs,
        out_specs=pl.BlockSpec((tm, tn), lambda i, j: (i, j)),
        compiler_params=_cparams(("parallel", "arbitrary")),
        name=name,
    )(*args)


def _rope_tables(s):
    half = MLA_ROPE // 2
    freq = ROPE_THETA ** (-jnp.arange(half, dtype=F32) / half)
    ang = jnp.arange(s).astype(F32)[:, None] * freq[None, :]
    cos, sin = jnp.cos(ang), jnp.sin(ang)
    return jnp.concatenate([cos, cos], -1), jnp.concatenate([-sin, sin], -1)


MLA_DQ = MLA_NOPE + MLA_ROPE
MLA_QROWS = MLA_NOPE + 2 * MLA_ROPE
MLA_T = 512
MLA_HPS = 2
MLA_SUB = 256
MLA_FAR_UNROLL = 4


def _mla_q_body(lat_ref, g_ref, w_ref, cos_ref, sin_ref, o_ref, *, qscale):
    xn = _rms_norm(lat_ref[...], g_ref[...]).astype(BF16)
    y = _dot_nt(w_ref[...], xn)
    cos, sin = cos_ref[...], sin_ref[...]
    for h in range(MLA_HEADS):
        r0 = h * MLA_QROWS
        o_ref[0, h, :MLA_NOPE, :] = (y[r0:r0 + MLA_NOPE] * qscale).astype(BF16)
        a = y[r0 + MLA_NOPE:r0 + MLA_DQ]
        bb = y[r0 + MLA_DQ:r0 + MLA_QROWS]
        o_ref[0, h, MLA_NOPE:, :] = ((a * cos + bb * sin) * qscale).astype(BF16)


def _mla_q_proj(lat, gain, w_t, cos_t, sin_t, b, s, *, tm=MLA_T):
    ns = s // tm
    body = functools.partial(_mla_q_body, qscale=(MLA_DQ ** -0.5) * LOG2E)
    return pl.pallas_call(
        body,
        out_shape=jax.ShapeDtypeStruct((b, MLA_HEADS, MLA_DQ, s), BF16),
        grid=(b * ns,),
        in_specs=[pl.BlockSpec((tm, MLA_Q_RANK), lambda i: (i, 0)),
                  pl.BlockSpec((1, MLA_Q_RANK), lambda i: (0, 0)),
                  pl.BlockSpec(w_t.shape, lambda i: (0, 0)),
                  pl.BlockSpec((MLA_ROPE, tm), lambda i: (0, i % ns)),
                  pl.BlockSpec((MLA_ROPE, tm), lambda i: (0, i % ns))],
        out_specs=pl.BlockSpec((1, MLA_HEADS, MLA_DQ, tm), lambda i: (i // ns, 0, 0, i % ns)),
        compiler_params=_cparams(("parallel",)),
        name="mla_q_proj",
    )(lat, gain.reshape(1, -1), w_t, cos_t, sin_t)


def _mla_kv_body(lat_ref, g_ref, wk_ref, wvt_ref, kr_ref, cos_ref, sin_ref, k_ref, vt_ref):
    xn = _rms_norm(lat_ref[...], g_ref[...]).astype(BF16)
    kn = _dot(xn, wk_ref[...]).astype(BF16)
    vt = _dot_nt(wvt_ref[...], xn).astype(BF16)
    kr = kr_ref[...]
    rot = (kr[:, :MLA_ROPE] * cos_ref[...] + kr[:, MLA_ROPE:] * sin_ref[...]).astype(BF16)
    for h in range(MLA_HEADS):
        k_ref[0, h, :, :MLA_NOPE] = kn[:, h * MLA_NOPE:(h + 1) * MLA_NOPE]
        k_ref[0, h, :, MLA_NOPE:] = rot
        for kk in range(MLA_T // MLA_SUB):
            vt_ref[0, h, kk] = vt[h * MLA_V:(h + 1) * MLA_V, kk * MLA_SUB:(kk + 1) * MLA_SUB]


def _mla_kv_proj(lat, gain, wk, wv_t, cosx, sinx, b, s, *, tm=MLA_T):
    ns = s // tm
    nsub = tm // MLA_SUB
    return pl.pallas_call(
        _mla_kv_body,
        out_shape=(jax.ShapeDtypeStruct((b, MLA_HEADS, s, MLA_DQ), BF16),
                   jax.ShapeDtypeStruct((b, MLA_HEADS, ns * nsub, MLA_V, MLA_SUB), BF16)),
        grid=(b * ns,),
        in_specs=[pl.BlockSpec((tm, MLA_KV_RANK), lambda i: (i, MLA_Q_RANK // MLA_KV_RANK)),
                  pl.BlockSpec((1, MLA_KV_RANK), lambda i: (0, 0)),
                  pl.BlockSpec(wk.shape, lambda i: (0, 0)),
                  pl.BlockSpec(wv_t.shape, lambda i: (0, 0)),
                  pl.BlockSpec((tm, 2 * MLA_ROPE),
                               lambda i: (i, (MLA_Q_RANK + MLA_KV_RANK) // (2 * MLA_ROPE))),
                  pl.BlockSpec((tm, MLA_ROPE), lambda i: (i % ns, 0)),
                  pl.BlockSpec((tm, MLA_ROPE), lambda i: (i % ns, 0))],
        out_specs=(pl.BlockSpec((1, MLA_HEADS, tm, MLA_DQ), lambda i: (i // ns, 0, i % ns, 0)),
                   pl.BlockSpec((1, MLA_HEADS, nsub, MLA_V, MLA_SUB), lambda i: (i // ns, 0, i % ns, 0, 0))),
        compiler_params=_cparams(("parallel",)),
        name="mla_kv_proj",
    )(lat, gain.reshape(1, -1), wk, wv_t, lat, cosx, sinx)


def _mla_attn_body(qt_ref, k_ref, vt_ref, o_ref, *scr):
    i = pl.program_id(2)
    sub = MLA_SUB
    nsub = MLA_T // sub
    assert nsub == 2
    chains_all = [(h, ql) for ql in range(nsub) for h in range(MLA_HPS)]
    nch = len(chains_all)
    ring_a, ring_b, scr = scr[:nch], scr[nch:2 * nch], scr[2 * nch:]
    state = {(h, ql): scr[3 * (nsub * h + ql):3 * (nsub * h + ql) + 3]
             for h in range(MLA_HPS) for ql in range(nsub)}
    for m_scr, l_scr, acc_scr in state.values():
        m_scr[...] = jnp.full_like(m_scr, NEG)
        l_scr[...] = jnp.zeros_like(l_scr)
        acc_scr[...] = jnp.zeros_like(acc_scr)

    def issue(sk, ring, chains):
        for idx, (h, ql) in enumerate(chains_all):
            if (h, ql) in chains:
                k = k_ref[0, h, pl.ds(pl.multiple_of(sk * sub, sub), sub), :]
                ring[idx][...] = _dot(k, qt_ref[0, h, :, ql * sub:(ql + 1) * sub])
            yield

    def consume(sk, ring, chains, diag_ql=None):
        for idx, (h, ql) in enumerate(chains_all):
            if (h, ql) in chains:
                m_scr, l_scr, acc_scr = state[(h, ql)]
                s = ring[idx][...]
                if ql == diag_ql:
                    key = lax.broadcasted_iota(jnp.int32, s.shape, 0)
                    qry = lax.broadcasted_iota(jnp.int32, s.shape, 1)
                    s = jnp.where(key <= qry, s, NEG)
                m_old = m_scr[...]
                m_new = jnp.maximum(m_old, jnp.max(s, 0, keepdims=True))
                a = jnp.exp2(m_old - m_new)
                p = jnp.exp2(s - m_new)
                l_scr[...] = a * l_scr[...] + jnp.sum(p, 0, keepdims=True)
                acc_scr[...] = a * acc_scr[...] + _dot(vt_ref[0, h, sk], p.astype(BF16))
                m_scr[...] = m_new
            yield

    def interleave(*gens):
        for _ in zip(*gens):
            pass

    interleave(issue(0, ring_a, chains_all))

    def pair(c0):
        interleave(issue(c0 + 1, ring_b, chains_all), consume(c0, ring_a, chains_all))
        interleave(issue(c0 + 2, ring_a, chains_all), consume(c0 + 1, ring_b, chains_all))

    def trip(j, carry):
        for pr in range(MLA_FAR_UNROLL):
            pair(2 * (MLA_FAR_UNROLL * j + pr))
        return carry

    lax.fori_loop(0, i // MLA_FAR_UNROLL, trip, 0)
    lax.fori_loop(i // MLA_FAR_UNROLL * MLA_FAR_UNROLL, i, lambda j, carry: (pair(2 * j), carry)[1], 0)
    upper = [c for c in chains_all if c[1] == 1]
    interleave(issue(2 * i + 1, ring_b, upper), consume(2 * i, ring_a, chains_all, diag_ql=0))
    interleave(consume(2 * i + 1, ring_b, upper, diag_ql=1))
    for (h, ql), (m_scr, l_scr, acc_scr) in state.items():
        o_t = acc_scr[...] * (1.0 / jnp.maximum(l_scr[...], 1e-30))
        o_ref[0, ql * sub:(ql + 1) * sub, h * MLA_V:(h + 1) * MLA_V] = o_t.T.astype(o_ref.dtype)


def _mla_attention(q_t, k, v_t, b, s):
    tq = MLA_T
    nq = s // tq
    hp = MLA_HPS
    nsub = tq // MLA_SUB
    sub_state = [pltpu.VMEM((1, MLA_SUB), F32), pltpu.VMEM((1, MLA_SUB), F32), pltpu.VMEM((MLA_V, MLA_SUB), F32)]
    ring = hp * nsub * [pltpu.VMEM((MLA_SUB, MLA_SUB), F32)]
    return pl.pallas_call(
        _mla_attn_body,
        out_shape=jax.ShapeDtypeStruct((b, s, MLA_HEADS * MLA_V), BF16),
        grid=(b, MLA_HEADS // hp, nq),
        in_specs=[pl.BlockSpec((1, hp, MLA_DQ, tq), lambda bb, h, i: (bb, h, 0, i)),
                  pl.BlockSpec((1, hp, s, MLA_DQ), lambda bb, h, i: (bb, h, 0, 0)),
                  pl.BlockSpec((1, hp, nq * nsub, MLA_V, MLA_SUB), lambda bb, h, i: (bb, h, 0, 0, 0))],
        out_specs=pl.BlockSpec((1, tq, hp * MLA_V), lambda bb, h, i: (bb, i, h)),
        scratch_shapes=2 * ring + hp * nsub * sub_state,
        compiler_params=_cparams(("parallel", "parallel", "arbitrary")),
        name="mla_attention",
    )(q_t, k, v_t)


def _mla_mixer(x2, b, s, w_in, q_norm, w_q_up, kv_norm, w_kv_up):
    r0 = MLA_Q_RANK + MLA_KV_RANK
    half = MLA_ROPE // 2
    w_in_ext = jnp.concatenate([w_in, w_in[:, r0 + half:r0 + MLA_ROPE], w_in[:, r0:r0 + half]], 1)
    lat = _linear(x2, w_in_ext.astype(BF16), tm=512, tn=w_in_ext.shape[1], out_dtype=F32, name="mla_in")
    wq = w_q_up.reshape(MLA_Q_RANK, MLA_HEADS, MLA_DQ)
    wr = wq[..., MLA_NOPE:]
    wq = jnp.concatenate([wq, wr[..., half:], wr[..., :half]], -1)
    wq_t = wq.reshape(MLA_Q_RANK, MLA_HEADS * MLA_QROWS).T.astype(BF16)
    wkv = w_kv_up.reshape(MLA_KV_RANK, MLA_HEADS, MLA_NOPE + MLA_V)
    wk = wkv[..., :MLA_NOPE].reshape(MLA_KV_RANK, MLA_HEADS * MLA_NOPE).astype(BF16)
    wv_t = wkv[..., MLA_NOPE:].reshape(MLA_KV_RANK, MLA_HEADS * MLA_V).T.astype(BF16)
    cosx, sinx = _rope_tables(s)
    q_t = _mla_q_proj(lat, q_norm, wq_t, cosx.T, sinx.T, b, s)
    k, v_t = _mla_kv_proj(lat, kv_norm, wk, wv_t, cosx, sinx, b, s)
    o = _mla_attention(q_t, k, v_t, b, s)
    return o.reshape(b * s, MLA_HEADS * MLA_V)


def _ffn_body(x_ref, wg_ref, wu_ref, wd_ref, lg_ref, lb_ref, o_ref, acc_scr):
    f = pl.program_id(1)
    xb = x_ref[...].astype(BF16)
    h = jax.nn.silu(_dot(xb, wg_ref[...])) * _dot(xb, wu_ref[...])
    part = _dot(h.astype(BF16), wd_ref[...])

    @pl.when(f == 0)
    def _():
        acc_scr[...] = part

    @pl.when(f > 0)
    def _():
        acc_scr[...] += part

    @pl.when(f == pl.num_programs(1) - 1)
    def _():
        o_ref[...] = _layer_norm(ALPHA * x_ref[...] + acc_scr[...], lg_ref[...], lb_ref[...])


def _ffn_ln(x, wg, wu, wd, lg, lb, *, tm=1024, tf=256):
    m, d = x.shape
    dff = wg.shape[1]
    assert dff % tf == 0
    vec = pl.BlockSpec((1, d), lambda i, f: (0, 0))
    return pl.pallas_call(
        _ffn_body,
        out_shape=jax.ShapeDtypeStruct((m, d), F32),
        grid=(m // tm, dff // tf),
        in_specs=[pl.BlockSpec((tm, d), lambda i, f: (i, 0)),
                  pl.BlockSpec((d, tf), lambda i, f: (0, f)),
                  pl.BlockSpec((d, tf), lambda i, f: (0, f)),
                  pl.BlockSpec((tf, d), lambda i, f: (f, 0)),
                  vec, vec],
        out_specs=pl.BlockSpec((tm, d), lambda i, f: (i, 0)),
        scratch_shapes=[pltpu.VMEM((tm, d), F32)],
        compiler_params=_cparams(("parallel", "arbitrary")),
        name="ffn_ln",
    )(x, wg, wu, wd, lg.reshape(1, d), lb.reshape(1, d))


def _compress_body(c_ref, pea_ref, peb_ref, w1a_ref, w1b_ref, w2_ref, o_ref):
    c = c_ref[0, 0, 0].astype(F32)
    ya = _dot((c + pea_ref[...]).astype(BF16), w1a_ref[...])
    yb = _dot((c + peb_ref[...]).astype(BF16), w1b_ref[...])
    yb_next = jnp.concatenate([yb[1:], jnp.zeros_like(yb[:1])], 0)
    h = jax.nn.gelu(ya + yb_next)
    o_ref[0] = _dot(h.astype(BF16), w2_ref[...]).astype(o_ref.dtype)


def _compress(chunks, kv, pe, w1, w2, dpad):
    b, g, _, nch, k = chunks.shape
    bg = b * g
    d = pe.shape[1]
    dh, dout = w2.shape
    half = CMP_LEN // 2
    assert half == CMP_STRIDE and k == CMP_STRIDE * dpad

    def pad_tok(a):
        a = jnp.pad(a, [(0, 0), (0, dpad - d)] + [(0, 0)] * (a.ndim - 2))
        return a[:half].reshape((k,) + a.shape[2:]), a[half:].reshape((k,) + a.shape[2:])

    pe_a, pe_b = pad_tok(pe)
    w1a, w1b = pad_tok(w1)
    full = lambda i: (0, 0)
    return pl.pallas_call(
        _compress_body,
        out_shape=jax.ShapeDtypeStruct((bg, nch, dout), BF16),
        grid=(bg,),
        in_specs=[pl.BlockSpec((1, 1, 1, nch, k), lambda i: (i // g, i % g, kv, 0, 0)),
                  pl.BlockSpec((1, k), full), pl.BlockSpec((1, k), full),
                  pl.BlockSpec((k, dh), full), pl.BlockSpec((k, dh), full),
                  pl.BlockSpec((dh, dout), full)],
        out_specs=pl.BlockSpec((1, nch, dout), lambda i: (i, 0, 0)),
        compiler_params=_cparams(("parallel",)),
        name="nsa_compress",
    )(chunks, pe_a.reshape(1, k), pe_b.reshape(1, k), w1a.astype(BF16), w1b.astype(BF16), w2.astype(BF16))


def _rel_bucket_np(dist):
    n = np.maximum(dist, 0)
    max_exact = REL_BUCKETS // 2
    nf = np.maximum(n, 1).astype(np.float32)
    large = max_exact + (np.log(nf / np.float32(max_exact)) / np.float32(math.log(REL_MAX_DIST / max_exact))
                         * np.float32(REL_BUCKETS - max_exact)).astype(np.int32)
    large = np.minimum(large, REL_BUCKETS - 1)
    return np.where(n < max_exact, n, large).astype(np.int32)


def _nsa_tables(rel_bias, s):
    g, hg, tq, kc = NSA_GROUPS, NSA_HG, NSA_TQ, NSA_KC
    assert tq == kc and tq % CMP_STRIDE == 0 and WINDOW == 2 * kc
    assert np.all(_rel_bucket_np(np.arange(tq // 2 - 15, s + tq)) == REL_BUCKETS - 1)
    rb = rel_bias.reshape(REL_BUCKETS, g, hg) * LOG2E

    def tile(base, step, rows, valid):
        p = tq + step * rows
        k = np.arange(p)
        k = np.where(k < p - step * (rows - 1), k, k - p)
        d = base + k
        vec = jnp.where(valid(d)[:, None, None], rb[_rel_bucket_np(d)], NEG)
        vec = vec.transpose(1, 2, 0)
        flat = jnp.tile(vec, (1, 1, rows))[..., :rows * (p - step)]
        mat = flat.reshape(g, hg, rows, p - step)[..., :tq]
        return mat.transpose(0, 2, 1, 3).reshape(g, rows, hg * tq)

    causal = lambda d: d >= 0
    far = jnp.broadcast_to(rb[REL_BUCKETS - 1][:, None, :, None], (g, 1, hg, tq)).reshape(g, 1, hg * tq)
    rel = lambda x: jnp.where(x > NEG_TEST, x - far, NEG)
    tiles = jnp.stack([rel(tile(0, 1, kc, causal)), rel(tile(tq, 1, kc, causal)),
                       rel(tile(2 * tq, 1, kc, lambda d: d < WINDOW)),
                       jnp.zeros((g, kc, hg * tq), F32)], 1)
    band = jnp.stack([rel(tile(8 * CMP_STRIDE - CMP_LEN + 1, CMP_STRIDE, NSA_BAND, causal)),
                      rel(tile(-(CMP_LEN - 1), CMP_STRIDE, NSA_BAND, causal))], 1)
    return tiles, band


def _overlap_t(nc_pad, nb):
    n = np.arange(nc_pad)[None, :]
    jb = np.arange(nb)[:, None]
    cstart = n * CMP_STRIDE
    cend = cstart + CMP_LEN - 1
    sstart = jb * SEL_BLOCK
    ov = (cstart <= sstart + SEL_BLOCK - 1) & (cend >= sstart) & (n < nc_pad - 1)
    return jnp.asarray(ov.astype(np.float32), BF16)


def _nsa_body(qt_ref, kc_ref, vct_ref, k_ref, vt_ref, gate_ref, tiles_ref,
              band_ref, ovt_ref, o_ref, s_scr, sel_scr, qa_scr, oc_scr, *scr, nb, nqt):
    t = pl.program_id(2)
    ring_a, ring_b, state = scr[:NSA_HG], scr[NSA_HG:2 * NSA_HG], scr[2 * NSA_HG:]
    nkc = k_ref.shape[3]
    L = NSA_LANES
    q_t = qt_ref[0, 0, 0]
    ncp = kc_ref.shape[2]
    per = NSA_KC // SEL_BLOCK
    blocks_per_tile = NSA_TQ // CMP_STRIDE

    n_slab = nb // NSA_SEL_ROWS

    n_win = jnp.minimum(t, 2) + 1
    n_slots = n_win + t + 1

    def slot_params(c):
        is_win = c < n_win
        j = c - n_win
        is_sel = jnp.logical_and(c >= n_win, j <= t)
        delta = t - j
        br = is_win.astype(jnp.int32)
        kidx = jnp.where(is_win, t - c, jnp.where(is_sel, j, 0))
        sidx = jnp.where(is_sel, j // (NSA_SEL_ROWS // per), jnp.where(is_win, 0, n_slab))
        tidx = jnp.where(is_win, c, jnp.where(jnp.logical_and(is_sel, delta < 2), delta, 3))
        return br, kidx, sidx, tidx

    def issue(c, ring, set_rows=True):
        br, kidx, sidx, _ = slot_params(c)
        k = k_ref[0, 0, br, kidx]
        if set_rows:
            qa_scr[NSA_DK:NSA_DK + NSA_SEL_ROWS, :] = sel_scr[sidx]
        for h in range(NSA_HG):
            ring[h][...] = _dot(k, qa_scr[:, h * NSA_TQ:(h + 1) * NSA_TQ])
            yield

    def consume(c, ring, near):
        br, kidx, _, tidx = slot_params(c)
        vt = vt_ref[0, 0, br, kidx]
        for h in range(NSA_HG):
            m_scr, acc_scr = state[2 * h:2 * h + 2]
            sc = ring[h][...]
            if near:
                sc = sc + tiles_ref[0, tidx, :, h * NSA_TQ:(h + 1) * NSA_TQ]
            m_old = m_scr[br]
            m_new = jnp.maximum(m_old, jnp.max(sc, 0, keepdims=True))
            pp = jnp.exp2(sc - m_new)
            acc_scr[br] = jnp.exp2(m_old - m_new) * acc_scr[br] + _dot(vt, pp.astype(BF16))
            m_scr[br] = m_new
            yield

    def interleave(*gens):
        for _ in zip(*gens):
            pass

    for h in range(NSA_HG):
        m_scr, acc_scr = state[2 * h:2 * h + 2]
        m_scr[...] = jnp.full_like(m_scr, NEG)
        acc_scr[...] = jnp.zeros_like(acc_scr)
    qa_scr[...] = q_t

    def compress_and_select(nrows, nblk):
        s = _dot(kc_ref[0, 0, :nrows], q_t)
        first = (t == 0).astype(jnp.int32)
        bs = pl.multiple_of((blocks_per_tile * t - 8) * (1 - first), 8)
        row = lax.broadcasted_iota(jnp.int32, (nrows, L), 0)
        s_scr[:nrows] = jnp.where(row < bs + NSA_BAND, s, NEG)
        s_scr[pl.ds(bs, NSA_BAND), :] += band_ref[0, first]
        s = s_scr[:nrows]
        m = jnp.max(s, 0, keepdims=True)
        m = jnp.where(m < NEG_TEST, 0.0, m)
        p = jnp.exp2(s - m)
        den = jnp.maximum(jnp.sum(p, 0, keepdims=True), 1e-30)
        p = p * (1.0 / den)
        oc_scr[...] = _dot(vct_ref[0, 0, :, :nrows], p.astype(BF16))

        psum = p[:, 0:NSA_TQ]
        for h in range(1, NSA_HG):
            psum = psum + p[:, h * NSA_TQ:(h + 1) * NSA_TQ]
        p1, p2, p3 = _split3(psum)
        ovt = ovt_ref[:nblk, :nrows]
        imp = _dot(ovt, p1) + _dot(ovt, p2) + _dot(ovt, p3)
        interleave(issue(0, ring_a, set_rows=False))
        blk = lax.broadcasted_iota(jnp.int32, (nblk, NSA_TQ), 0)
        lane = lax.broadcasted_iota(jnp.int32, (nblk, NSA_TQ), 1)
        cur = (NSA_TQ // SEL_BLOCK) * t + lane // SEL_BLOCK
        forced = (blk == 0) | (blk == cur) | (blk == cur - 1)
        v = jnp.where(blk > cur, -FORCE, jnp.where(forced, -jnp.inf, imp))
        blk_f = blk.astype(F32)
        sel = jnp.where(forced, 1.0, 0.0)
        for _ in range(min(SEL_TOPN, nblk) - 3):
            mx = jnp.max(v, 0, keepdims=True)
            idx = jnp.min(jnp.where(v == mx, blk_f, float(nblk)), 0, keepdims=True)
            hit = blk_f == idx
            sel = jnp.where(hit, 1.0, sel)
            v = jnp.where(hit, -jnp.inf, v)
        selneg = jnp.where(sel > 0.5, 0.0, NEG)
        selneg = jnp.concatenate([selneg] * NSA_HG, 1).astype(BF16)
        for u in range(nblk // NSA_SEL_ROWS):
            sel_scr[u] = selneg[NSA_SEL_ROWS * u:NSA_SEL_ROWS * (u + 1), :]

    n_var = -(-nqt // NSA_TILES_PER_VARIANT)
    for i in range(n_var):
        nrows = min(ncp, (i + 1) * NSA_TILES_PER_VARIANT * blocks_per_tile)
        nblk = min(nb, (i + 1) * NSA_TILES_PER_VARIANT * NSA_TQ // SEL_BLOCK)

        @pl.when(t // NSA_TILES_PER_VARIANT == i)
        def _(nrows=nrows, nblk=nblk):
            compress_and_select(nrows, nblk)

    sel_scr[n_slab] = jnp.full((NSA_SEL_ROWS, L), NEG, BF16)

    def make_trip(first_slot, near, pairs=1):
        def trip(i, carry):
            for pr in range(pairs):
                c0 = first_slot + 2 * (pairs * i + pr)
                interleave(issue(c0 + 1, ring_b), consume(c0, ring_a, near))
                interleave(issue(c0 + 2, ring_a), consume(c0 + 1, ring_b, near))
            return carry
        return trip

    far_pairs = jnp.maximum(n_slots - 2 - NSA_HEAD_SLOTS, 0) // 2
    far_trips = far_pairs // NSA_FAR_UNROLL
    rest_first = NSA_HEAD_SLOTS + 2 * NSA_FAR_UNROLL * far_trips
    tail_first = NSA_HEAD_SLOTS + 2 * far_pairs
    tail_trips = (jnp.maximum(n_slots - tail_first, 0) + 1) // 2
    make_trip(0, True, NSA_HEAD_SLOTS // 2)(0, 0)
    lax.fori_loop(0, far_trips, make_trip(NSA_HEAD_SLOTS, False, NSA_FAR_UNROLL), 0)
    lax.fori_loop(0, far_pairs - NSA_FAR_UNROLL * far_trips, make_trip(rest_first, False), 0)
    lax.fori_loop(0, tail_trips, make_trip(tail_first, True), 0)

    gate = gate_ref[0, 0, 0]
    outs = []
    for h in range(NSA_HG):
        sl = slice(h * NSA_TQ, (h + 1) * NSA_TQ)
        acc_scr = state[2 * h + 1]
        o_s = acc_scr[0, :NSA_DV] * (1.0 / jnp.maximum(acc_scr[0, NSA_DV:NSA_DV + 1], 1e-30))
        o_w = acc_scr[1, :NSA_DV] * (1.0 / jnp.maximum(acc_scr[1, NSA_DV:NSA_DV + 1], 1e-30))
        outs.append(gate[0:1, sl] * oc_scr[:, sl] + gate[1:2, sl] * o_s + gate[2:3, sl] * o_w)
    o_ref[0] = jnp.concatenate(outs, 0).T.astype(o_ref.dtype)


def _nsa_attention(q_t, kcmp, vcmp_t, k, v_t, gate, tiles, band, ovt):
    b, g, nqt = q_t.shape[:3]
    ncp = kcmp.shape[2]
    nkc = k.shape[3]
    nb = ovt.shape[0]
    L = NSA_LANES
    body = functools.partial(_nsa_body, nb=nb, nqt=nqt)
    ring = NSA_HG * [pltpu.VMEM((NSA_KC, NSA_TQ), F32)]
    head_state = [pltpu.VMEM((2, 1, NSA_TQ), F32), pltpu.VMEM((2, NSA_VROWS, NSA_TQ), F32)]
    grp = lambda bb, gg, t: (bb, gg, 0, 0)
    grp6 = lambda bb, gg, t: (bb, gg, 0, 0, 0, 0)
    return pl.pallas_call(
        body,
        out_shape=jax.ShapeDtypeStruct((b, nqt * NSA_TQ, g * NSA_HG * NSA_DV), BF16),
        grid=(b, g, nqt),
        in_specs=[pl.BlockSpec((1, 1, 1, NSA_DKP, L), lambda bb, gg, t: (bb, gg, t, 0, 0)),
                  pl.BlockSpec((1, 1, ncp, NSA_DKP), grp),
                  pl.BlockSpec((1, 1, NSA_DV, ncp), grp),
                  pl.BlockSpec((1, 1, 2, nkc, NSA_KC, NSA_DKP), grp6),
                  pl.BlockSpec((1, 1, 2, nkc, NSA_VROWS, NSA_KC), grp6),
                  pl.BlockSpec((1, 1, 1, N_BRANCH, L), lambda bb, gg, t: (bb, gg, t, 0, 0)),
                  pl.BlockSpec((1, 4, NSA_KC, L), lambda bb, gg, t: (gg, 0, 0, 0)),
                  pl.BlockSpec((1, 2, NSA_BAND, L), lambda bb, gg, t: (gg, 0, 0, 0)),
                  pl.BlockSpec(ovt.shape, lambda bb, gg, t: (0, 0))],
        out_specs=pl.BlockSpec((1, NSA_TQ, NSA_HG * NSA_DV), lambda bb, gg, t: (bb, t, gg)),
        scratch_shapes=[pltpu.VMEM((ncp, L), F32),
                        pltpu.VMEM((nb // NSA_SEL_ROWS + 1, NSA_SEL_ROWS, L), BF16),
                        pltpu.VMEM((NSA_DKP, L), BF16),
                        pltpu.VMEM((NSA_DV, L), F32)]
        + 2 * ring + NSA_HG * head_state,
        compiler_params=_cparams(("parallel", "parallel", "arbitrary")),
        name="nsa_attention",
    )(q_t, kcmp, vcmp_t, k, v_t, gate, tiles, band, ovt)


def _nsa_proj_body(x_ref, wq_ref, wk_ref, wv_ref, wg_ref, wc_ref,
                   q_ref, k_ref, v_ref, gate_ref, c_ref, ctok_scr, *, qscale):
    g, hg, tq = NSA_GROUPS, NSA_HG, NSA_TQ
    xb = x_ref[...].astype(BF16)
    q_t = _dot_nt(wq_ref[...], xb) * qscale
    gate_t = jax.nn.sigmoid(_dot_nt(wg_ref[...], xb))
    for gg in range(g):
        for h in range(hg):
            head = gg * hg + h
            q_ref[0, gg, 0, :, h * tq:(h + 1) * tq] = q_t[head * NSA_DKP:(head + 1) * NSA_DKP].astype(BF16)
            r0 = head * NSA_GATE_ROWS
            gate_ref[0, gg, 0, :, h * tq:(h + 1) * tq] = gate_t[r0:r0 + N_BRANCH]
    k = _dot(xb, wk_ref[...])
    v_t = _dot_nt(wv_ref[...], xb).astype(BF16)
    per = tq // SEL_BLOCK
    chunk = pl.program_id(0) % (NSA_SEL_ROWS // per)
    row = lax.broadcasted_iota(jnp.int32, (tq, NSA_DKP), 0)
    col = lax.broadcasted_iota(jnp.int32, (tq, NSA_DKP), 1)
    blk_flag = jnp.where(col - NSA_DK == per * chunk + row // SEL_BLOCK, 1.0, 0.0)
    ones_rows = jnp.where(lax.broadcasted_iota(jnp.int32, (NSA_VROWS - NSA_DV, tq), 0) == 0, 1.0, 0.0).astype(BF16)
    for gg in range(g):
        for br in range(2):
            kb = k[:, (br * g + gg) * NSA_DKP:(br * g + gg + 1) * NSA_DKP]
            if br == 0:
                kb = kb + blk_flag
            k_ref[0, gg, br, 0] = kb.astype(BF16)
            v_ref[0, gg, br, 0, :NSA_DV, :] = v_t[(br * g + gg) * NSA_DV:(br * g + gg + 1) * NSA_DV]
            v_ref[0, gg, br, 0, NSA_DV:, :] = ones_rows
    ctok = _dot(xb, wc_ref[...])
    for gg in range(g):
        for kv in range(2):
            j = kv * g + gg
            ctok_scr[j] = ctok[:, j * NSA_DKP:(j + 1) * NSA_DKP]
            for l in range(CMP_STRIDE):
                rows = ctok_scr[j, pl.ds(l, tq // CMP_STRIDE, stride=CMP_STRIDE), :]
                c_ref[0, gg, kv, :, l * NSA_DKP:(l + 1) * NSA_DKP] = rows.astype(BF16)


def _nsa_proj(x2, b, s, wq_t, wk, wv_t, wg_t, wc):
    g, hg, tq, L = NSA_GROUPS, NSA_HG, NSA_TQ, NSA_LANES
    nqt = s // tq
    assert nqt % (NSA_SEL_ROWS * SEL_BLOCK // tq) == 0 and (s // SEL_BLOCK) % NSA_SEL_ROWS == 0
    t, d = x2.shape
    full = lambda i: (0, 0)
    tile5 = lambda i: (i // nqt, 0, i % nqt, 0, 0)
    tile6 = lambda i: (i // nqt, 0, 0, i % nqt, 0, 0)
    body = functools.partial(_nsa_proj_body, qscale=(NSA_DK ** -0.5) * LOG2E)
    return pl.pallas_call(
        body,
        out_shape=(jax.ShapeDtypeStruct((b, g, nqt, NSA_DKP, L), BF16),
                   jax.ShapeDtypeStruct((b, g, 2, nqt, NSA_KC, NSA_DKP), BF16),
                   jax.ShapeDtypeStruct((b, g, 2, nqt, NSA_VROWS, NSA_KC), BF16),
                   jax.ShapeDtypeStruct((b, g, nqt, N_BRANCH, L), F32),
                   jax.ShapeDtypeStruct((b, g, 2, s // CMP_STRIDE, CMP_STRIDE * NSA_DKP), BF16)),
        grid=(t // tq,),
        in_specs=[pl.BlockSpec((tq, d), lambda i: (i, 0)),
                  pl.BlockSpec(wq_t.shape, full), pl.BlockSpec(wk.shape, full), pl.BlockSpec(wv_t.shape, full),
                  pl.BlockSpec(wg_t.shape, full), pl.BlockSpec(wc.shape, full)],
        out_specs=(pl.BlockSpec((1, g, 1, NSA_DKP, L), tile5),
                   pl.BlockSpec((1, g, 2, 1, NSA_KC, NSA_DKP), tile6),
                   pl.BlockSpec((1, g, 2, 1, NSA_VROWS, NSA_KC), tile6),
                   pl.BlockSpec((1, g, 1, N_BRANCH, L), tile5),
                   pl.BlockSpec((1, g, 2, tq // CMP_STRIDE, CMP_STRIDE * NSA_DKP),
                                lambda i: (i // nqt, 0, 0, i % nqt, 0))),
        scratch_shapes=[pltpu.VMEM((wc.shape[1] // NSA_DKP, tq, NSA_DKP), F32)],
        compiler_params=_cparams(("parallel",)),
        name="nsa_proj",
    )(x2, wq_t, wk, wv_t, wg_t, wc)


def _nsa_mixer(x2, b, s, w_in, pe_k, w1_k, w2_k, pe_v, w1_v, w2_v, rel_bias):
    assert NSA_TQ == NSA_KC
    t, d = x2.shape
    h, g, hg, dk, dv = NSA_HEADS, NSA_GROUPS, NSA_HG, NSA_DK, NSA_DV
    nb = s // SEL_BLOCK
    sizes = [h * dk, g * dk, g * dv, g * dk, g * dv, g * dk, g * dv, h * N_BRANCH]
    c = [0] + [int(v) for v in np.cumsum(sizes)]
    cols = [w_in[:, c[i]:c[i + 1]] for i in range(len(sizes))]
    w_q, w_kc, w_vc, w_ks, w_vs, w_kw, w_vw, w_gate = cols

    def pad_last(a, n):
        return jnp.pad(a, [(0, 0)] * (a.ndim - 1) + [(0, n - a.shape[-1])])

    wq_t = pad_last(w_q.reshape(d, h, dk), NSA_DKP).reshape(d, h * NSA_DKP).T.astype(BF16)
    wk = jnp.concatenate([pad_last(w.reshape(d, g, dk), NSA_DKP).reshape(d, g * NSA_DKP)
                          for w in (w_ks, w_kw)], 1).astype(BF16)
    wv_t = jnp.concatenate([w_vs, w_vw], 1).T.astype(BF16)
    wg_t = pad_last(w_gate.reshape(d, h, N_BRANCH), NSA_GATE_ROWS).reshape(d, h * NSA_GATE_ROWS).T.astype(BF16)
    wc = jnp.concatenate([pad_last(w_kc.reshape(d, g, dk), NSA_DKP).reshape(d, g * NSA_DKP),
                          pad_last(w_vc.reshape(d, g, dv), NSA_DKP).reshape(d, g * NSA_DKP)], 1).astype(BF16)
    q_t, k, v_t, gate, ctok = _nsa_proj(x2, b, s, wq_t, wk, wv_t, wg_t, wc)

    nch = s // CMP_STRIDE
    k_cmp = _compress(ctok, 0, pe_k, w1_k, pad_last(w2_k, NSA_DKP), NSA_DKP)
    v_cmp = _compress(ctok, 1, pe_v, w1_v, w2_v, NSA_DKP)
    k_cmp = k_cmp.reshape(b, g, nch, NSA_DKP)
    vcmp_t = v_cmp.reshape(b, g, nch, dv).transpose(0, 1, 3, 2)

    tiles, band = _nsa_tables(rel_bias, s)
    ovt = _overlap_t(nch, nb)
    o = _nsa_attention(q_t, k_cmp, vcmp_t, k, v_t, gate, tiles, band, ovt)
    return o.reshape(t, h * dv)


def _out_ln_router_body(o_ref, w_ref, res_ref, lg_ref, lb_ref, wr_ref, x_ref, xb_ref, logit_ref):
    y = _layer_norm(ALPHA * res_ref[...] + _dot(o_ref[...], w_ref[...]), lg_ref[...], lb_ref[...])
    x_ref[...] = y
    xb_ref[...] = y.astype(BF16)
    x1, x2, _ = _split3(y)
    w1, w2, _ = _split3(wr_ref[...])
    logit_ref[...] = _dot(x1, w1) + (_dot(x1, w2) + _dot(x2, w1))


def _out_ln_router(o, w_out, res, lg, lb, w_router, *, tm=512):
    m, k = o.shape
    n = w_out.shape[1]
    ne = w_router.shape[1]
    wr = jnp.pad(w_router, ((0, 0), (0, 128 - ne)))
    row = lambda i: (i, 0)
    full = lambda i: (0, 0)
    x, xb, logits = pl.pallas_call(
        _out_ln_router_body,
        out_shape=(jax.ShapeDtypeStruct((m, n), F32), jax.ShapeDtypeStruct((m, n), BF16),
                   jax.ShapeDtypeStruct((m, 128), F32)),
        grid=(m // tm,),
        in_specs=[pl.BlockSpec((tm, k), row), pl.BlockSpec((k, n), full), pl.BlockSpec((tm, n), row),
                  pl.BlockSpec((1, n), full), pl.BlockSpec((1, n), full), pl.BlockSpec((n, 128), full)],
        out_specs=(pl.BlockSpec((tm, n), row), pl.BlockSpec((tm, n), row), pl.BlockSpec((tm, 128), row)),
        compiler_params=_cparams(("parallel",)),
        name="nsa_out_ln_router",
    )(o, w_out, res, lg.reshape(1, n), lb.reshape(1, n), wr)
    return x, xb, logits[:, :ne]


def _dispatch_body(ir_ref, ic_ref, fl_ref, x_ref, rt_ref, o_ref):
    i = pl.program_id(0)
    flag = fl_ref[i]
    tok = ic_ref[i] * MOE_TC + lax.broadcasted_iota(jnp.int32, (MOE_BLK, MOE_TC), 1)
    onehot = jnp.where(rt_ref[...] == tok, 1.0, 0.0).astype(BF16)
    rows = _dot(onehot, x_ref[...])

    @pl.when(flag == 3)
    def _():
        o_ref[...] = rows.astype(o_ref.dtype)

    @pl.when(flag == 1)
    def _():
        o_ref[...] = (o_ref[...].astype(F32) + rows).astype(o_ref.dtype)


def _dispatch(x_bf, row_tok_col, item_r, item_c, flags):
    t, d = x_bf.shape
    r = row_tok_col.shape[0]
    ni = item_r.shape[0]
    gs = pltpu.PrefetchScalarGridSpec(
        num_scalar_prefetch=3, grid=(ni,),
        in_specs=[pl.BlockSpec((MOE_TC, d), lambda i, ir, ic, fl: (ic[i], 0)),
                  pl.BlockSpec((MOE_BLK, 1), lambda i, ir, ic, fl: (ir[i], 0))],
        out_specs=pl.BlockSpec((MOE_BLK, d), lambda i, ir, ic, fl: (ir[i], 0)))
    return pl.pallas_call(
        _dispatch_body, grid_spec=gs,
        out_shape=jax.ShapeDtypeStruct((r, d), BF16),
        compiler_params=_cparams(("arbitrary",)),
        name="moe_dispatch",
    )(item_r, item_c, flags, x_bf, row_tok_col)


def _expert_body(be_ref, live_ref, x_ref, wg_ref, wu_ref, wd_ref, rw_ref, o_ref, acc_scr):
    f = pl.program_id(1)
    live = live_ref[pl.program_id(0)] == 1
    last = f == pl.num_programs(1) - 1

    @pl.when(live)
    def _():
        xb = x_ref[...]
        h = jax.nn.silu(_dot(xb, wg_ref[0])) * _dot(xb, wu_ref[0])
        part = _dot(h.astype(BF16), wd_ref[0])

        @pl.when(f == 0)
        def _():
            acc_scr[...] = part

        @pl.when(f > 0)
        def _():
            acc_scr[...] += part

        @pl.when(last)
        def _():
            o_ref[...] = (acc_scr[...] * rw_ref[...]).astype(o_ref.dtype)

    @pl.when(jnp.logical_and(last, jnp.logical_not(live)))
    def _():
        o_ref[...] = jnp.zeros_like(o_ref)


def _experts(xs, wg, wu, wd, row_w_col, block_expert, block_live):
    r, d = xs.shape
    nbk = r // MOE_BLK
    nf = wg.shape[2] // MOE_TF
    ftile = lambda i, f, live: jnp.where(live[i] == 1, f, nf - 1)
    gs = pltpu.PrefetchScalarGridSpec(
        num_scalar_prefetch=2, grid=(nbk, nf),
        in_specs=[pl.BlockSpec((MOE_BLK, d), lambda i, f, be, live: (i, 0)),
                  pl.BlockSpec((1, d, MOE_TF), lambda i, f, be, live: (be[i], 0, ftile(i, f, live))),
                  pl.BlockSpec((1, d, MOE_TF), lambda i, f, be, live: (be[i], 0, ftile(i, f, live))),
                  pl.BlockSpec((1, MOE_TF, d), lambda i, f, be, live: (be[i], ftile(i, f, live), 0)),
                  pl.BlockSpec((MOE_BLK, 1), lambda i, f, be, live: (i, 0))],
        out_specs=pl.BlockSpec((MOE_BLK, d), lambda i, f, be, live: (i, 0)),
        scratch_shapes=[pltpu.VMEM((MOE_BLK, d), F32)])
    return pl.pallas_call(
        _expert_body, grid_spec=gs,
        out_shape=jax.ShapeDtypeStruct((r, d), BF16),
        compiler_params=_cparams(("parallel", "arbitrary")),
        name="moe_experts",
    )(block_expert, block_live, xs, wg, wu, wd, row_w_col)


def _combine_body(ir_ref, ic_ref, fl_ref, y_ref, rt_ref, x_ref, g_ref, b_ref, o_ref):
    i = pl.program_id(0)
    flag = fl_ref[i]
    tok = ic_ref[i] * MOE_TC + lax.broadcasted_iota(jnp.int32, (MOE_TC, MOE_BLK), 0)
    onehot = jnp.where(rt_ref[0] == tok, 1.0, 0.0).astype(BF16)
    part = _dot(onehot, y_ref[...])

    @pl.when((flag & 3) == 3)
    def _():
        o_ref[...] = part

    @pl.when((flag & 3) == 1)
    def _():
        o_ref[...] += part

    @pl.when((flag & 4) == 4)
    def _():
        o_ref[...] = _layer_norm(ALPHA * x_ref[...] + o_ref[...], g_ref[...], b_ref[...])


def _combine_ln(out_rows, row_tok_lane, item_r, item_c, flags, x, g, b):
    r, d = out_rows.shape
    t = x.shape[0]
    ni = item_r.shape[0]
    vec = pl.BlockSpec((1, d), lambda i, ir, ic, fl: (0, 0))
    gs = pltpu.PrefetchScalarGridSpec(
        num_scalar_prefetch=3, grid=(ni,),
        in_specs=[pl.BlockSpec((MOE_BLK, d), lambda i, ir, ic, fl: (ir[i], 0)),
                  pl.BlockSpec((1, 1, MOE_BLK), lambda i, ir, ic, fl: (ir[i], 0, 0)),
                  pl.BlockSpec((MOE_TC, d), lambda i, ir, ic, fl: (ic[i], 0)),
                  vec, vec],
        out_specs=pl.BlockSpec((MOE_TC, d), lambda i, ir, ic, fl: (ic[i], 0)))
    return pl.pallas_call(
        _combine_body, grid_spec=gs,
        out_shape=jax.ShapeDtypeStruct((t, d), F32),
        compiler_params=_cparams(("arbitrary",)),
        name="moe_combine_ln",
    )(item_r, item_c, flags, out_rows, row_tok_lane, x, g.reshape(1, d), b.reshape(1, d))


def _moe_plan(top_idx, wts, t):
    e, blk, tc = N_EXPERTS, MOE_BLK, MOE_TC
    a = t * TOP_K
    i32 = jnp.int32
    exp_flat = top_idx.reshape(a).astype(i32)
    tok_flat = jnp.arange(a, dtype=i32) // TOP_K
    _, tok_sorted, w_sorted = lax.sort((exp_flat, tok_flat, wts.reshape(a)), num_keys=1, is_stable=True)
    counts = jnp.sum((exp_flat[:, None] == jnp.arange(e, dtype=i32)[None, :]).astype(i32), 0)
    padded = ((counts + blk - 1) // blk) * blk
    grp_start = jnp.cumsum(counts) - counts
    pad_end = jnp.cumsum(padded)
    pad_start = pad_end - padded
    nbk = a // blk + e
    r = nbk * blk
    tok_ext = jnp.concatenate([tok_sorted, jnp.full((r - a,), -1, i32)])
    w_ext = jnp.concatenate([w_sorted, jnp.zeros((r - a,), F32)])
    rows = jnp.arange(r, dtype=i32)
    row_tok = jnp.full((r,), -1, i32)
    row_w = jnp.zeros((r,), F32)
    for ee in range(e):
        inside = (rows >= pad_start[ee]) & (rows < pad_start[ee] + counts[ee])
        shift = pad_start[ee] - grp_start[ee]
        row_tok = jnp.where(inside, jnp.roll(tok_ext, shift), row_tok)
        row_w = jnp.where(inside, jnp.roll(w_ext, shift), row_w)
    blk_first = jnp.arange(nbk, dtype=i32) * blk
    block_expert = jnp.minimum(jnp.sum((pad_end[None, :] <= blk_first[:, None]).astype(i32), 1), e - 1)

    rt = row_tok.reshape(nbk, blk)
    valid = rt >= 0
    t_lo = jnp.min(jnp.where(valid, rt, t), 1)
    t_hi = jnp.max(rt, 1)
    has = t_hi >= 0
    c_lo = jnp.where(has, t_lo // tc, 0)
    c_hi = jnp.where(has, t_hi // tc, 0)
    n_it = c_hi - c_lo + 1
    off_end = jnp.cumsum(n_it)
    off_start = off_end - n_it
    total = off_end[-1]
    ni = nbk + e * (t // tc)
    idx = jnp.arange(ni, dtype=i32)
    ok = idx < total
    ir = jnp.minimum(jnp.sum((off_end[None, :] <= idx[:, None]).astype(i32), 1), nbk - 1)
    ic = jnp.where(ok, c_lo[ir] + idx - off_start[ir], c_hi[nbk - 1]).astype(i32)
    first = ok & (idx == off_start[ir])
    d_flags = ok.astype(i32) + 2 * first.astype(i32)

    key = jnp.where(ok, ic * nbk + ir, jnp.iinfo(jnp.int32).max)
    perm = jnp.argsort(key)
    ok2 = ok[perm]
    last = total - 1
    cr = jnp.where(ok2, ir[perm], ir[perm][last]).astype(i32)
    cc = jnp.where(ok2, ic[perm], ic[perm][last]).astype(i32)
    first2 = ok2 & jnp.concatenate([jnp.ones((1,), bool), cc[1:] != cc[:-1]])
    last2 = ok2 & jnp.concatenate([(cc[1:] != cc[:-1]) | ~ok2[1:], jnp.ones((1,), bool)])
    c_flags = ok2.astype(i32) + 2 * first2.astype(i32) + 4 * last2.astype(i32)
    return dict(row_tok=row_tok, row_w=row_w, block_expert=block_expert, block_live=has.astype(i32),
                d_items=(ir, ic, d_flags), c_items=(cr, cc, c_flags), nbk=nbk)


def _moe_ln(x2, x_bf, logits, wg, wu, wd, ln_g, ln_b):
    t, d = x2.shape
    top_val, top_idx = lax.top_k(logits, TOP_K)
    wts = jax.nn.softmax(top_val, -1)
    plan = _moe_plan(top_idx, wts, t)
    nbk = plan["nbk"]
    xs = _dispatch(x_bf, plan["row_tok"].reshape(-1, 1), *plan["d_items"])
    out_rows = _experts(xs, wg.astype(BF16), wu.astype(BF16), wd.astype(BF16),
                        plan["row_w"].reshape(-1, 1), plan["block_expert"], plan["block_live"])
    return _combine_ln(out_rows, plan["row_tok"].reshape(nbk, 1, MOE_BLK), *plan["c_items"], x2, ln_g, ln_b)


def _forward(x, mla_w_in, mla_q_norm, mla_w_q_up, mla_kv_norm, mla_w_kv_up, mla_w_out, nsa_w_in,
             nsa_cmp_pe_k, nsa_cmp_w1_k, nsa_cmp_w2_k, nsa_cmp_pe_v, nsa_cmp_w1_v, nsa_cmp_w2_v,
             nsa_w_out, rel_bias, ffn_w_gate, ffn_w_up, ffn_w_down, moe_w_router, moe_w_gate,
             moe_w_up, moe_w_down, ln_mix_g, ln_mix_b, ln_ffn_g, ln_ffn_b):
    b, s, d = x.shape
    x2 = x.reshape(b * s, d)
    o = _mla_mixer(x2, b, s, mla_w_in[0], mla_q_norm[0], mla_w_q_up[0], mla_kv_norm[0], mla_w_kv_up[0])
    x2 = _linear(o, mla_w_out[0].astype(BF16), tm=1024, tn=d, out_dtype=F32,
                 ln=(x2, ln_mix_g[0], ln_mix_b[0]), name="mla_out_ln")
    x2 = _ffn_ln(x2, ffn_w_gate[0].astype(BF16), ffn_w_up[0].astype(BF16), ffn_w_down[0].astype(BF16),
                 ln_ffn_g[0], ln_ffn_b[0])
    o = _nsa_mixer(x2, b, s, nsa_w_in[0], nsa_cmp_pe_k[0], nsa_cmp_w1_k[0], nsa_cmp_w2_k[0],
                   nsa_cmp_pe_v[0], nsa_cmp_w1_v[0], nsa_cmp_w2_v[0], rel_bias)
    x2, x_bf, logits = _out_ln_router(o, nsa_w_out[0].astype(BF16), x2, ln_mix_g[1], ln_mix_b[1],
                                      moe_w_router[0])
    x2 = _moe_ln(x2, x_bf, logits, moe_w_gate[0], moe_w_up[0], moe_w_down[0], ln_ffn_g[1], ln_ffn_b[1])
    return x2.reshape(b, s, d)


@jax.jit
def kernel(x, mla_w_in, mla_q_norm, mla_w_q_up, mla_kv_norm, mla_w_kv_up, mla_w_out, nsa_w_in,
           nsa_cmp_pe_k, nsa_cmp_w1_k, nsa_cmp_w2_k, nsa_cmp_pe_v, nsa_cmp_w1_v, nsa_cmp_w2_v,
           nsa_w_out, rel_bias, ffn_w_gate, ffn_w_up, ffn_w_down, moe_w_router, moe_w_gate,
           moe_w_up, moe_w_down, ln_mix_g, ln_mix_b, ln_ffn_g, ln_ffn_b):
    return _forward(x, mla_w_in, mla_q_norm, mla_w_q_up, mla_kv_norm, mla_w_kv_up, mla_w_out, nsa_w_in,
                    nsa_cmp_pe_k, nsa_cmp_w1_k, nsa_cmp_w2_k, nsa_cmp_pe_v, nsa_cmp_w1_v, nsa_cmp_w2_v,
                    nsa_w_out, rel_bias, ffn_w_gate, ffn_w_up, ffn_w_down, moe_w_router, moe_w_gate,
                    moe_w_up, moe_w_down, ln_mix_g, ln_mix_b, ln_ffn_g, ln_ffn_b)
```
